```python
import math
import jax, jax.numpy as jnp
from jax import lax
import numpy as np

D_MODEL = 1024
BATCH = 16
SEQ = 4096
DEPTH = 2

N_META = 16
N_A_LAYERS = DEPTH // 2
N_B_LAYERS = DEPTH - N_A_LAYERS
D_FF = 2816
SSM_WIDTH = D_MODEL // 2
SSM_GROUP = 16
SSM_GROUPS = SSM_WIDTH // SSM_GROUP
SSM_STATE = 64
STEP_MIN = 1e-3
STEP_MAX = 1e-1
HEAD_DIM = 64
N_Q_HEADS = D_MODEL // HEAD_DIM
N_KV_HEADS = 4
Q_PER_KV = N_Q_HEADS // N_KV_HEADS
WINDOW = 128
BLOCK = 128
ROPE_THETA = 10000.0
EPS = 1e-6
NEG_INF = -1e30

kernel_name = "yoco_s5_swa_sink_macaron"


def rms_norm(x, g):
    xf = x.astype(jnp.float32)
    y = xf * lax.rsqrt(jnp.mean(xf * xf, axis=-1, keepdims=True) + EPS)
    return (y * g.astype(jnp.float32)).astype(x.dtype)


def rope(x, pos):
    half = HEAD_DIM // 2
    freqs = ROPE_THETA ** (-jnp.arange(0, half, dtype=jnp.float32) * 2.0 / HEAD_DIM)
    ang = pos.astype(jnp.float32)[:, None] * freqs[None, :]
    bshape = (pos.shape[0],) + (1,) * (x.ndim - 3) + (half,)
    cos = jnp.cos(ang).reshape(bshape)
    sin = jnp.sin(ang).reshape(bshape)
    xf = x.astype(jnp.float32)
    x1, x2 = xf[..., :half], xf[..., half:]
    return jnp.concatenate([x1 * cos - x2 * sin, x2 * cos + x1 * sin], axis=-1).astype(x.dtype)


def swiglu_ffn(h, g, w_gate_up, w_down):
    a, b = jnp.split(rms_norm(h, g) @ w_gate_up, 2, axis=-1)
    return (jax.nn.silu(a) * b) @ w_down


def _complex_scan_op(e1, e2):
    a1r, a1i, b1r, b1i = e1
    a2r, a2i, b2r, b2i = e2
    return (a2r * a1r - a2i * a1i,
            a2r * a1i + a2i * a1r,
            a2r * b1r - a2i * b1i + b2r,
            a2r * b1i + a2i * b1r + b2i)


def s5_mixer(hn, w_in, lam_re, lam_im, b_re, b_im, c_re, c_im, log_step, d_skip, w_out):
    bsz, L, _ = hn.shape
    f = lambda t: t.astype(jnp.float32)
    u = f(hn @ w_in)
    ug = u.reshape(bsz, L, SSM_GROUPS, SSM_GROUP)
    lr, li = f(lam_re), f(lam_im)
    step = jnp.exp(f(log_step))[:, None]
    mag = jnp.exp(lr * step)
    ar = mag * jnp.cos(li * step)
    ai = mag * jnp.sin(li * step)
    den = lr * lr + li * li
    nr, ni = ar - 1.0, ai
    cr = (nr * lr + ni * li) / den
    ci = (ni * lr - nr * li) / den
    br, bi = f(b_re), f(b_im)
    bbar_r = cr[..., None] * br - ci[..., None] * bi
    bbar_i = cr[..., None] * bi + ci[..., None] * br
    bu_r = jnp.einsum('blgc,gpc->blgp', ug, bbar_r)
    bu_i = jnp.einsum('blgc,gpc->blgp', ug, bbar_i)
    a_r = jnp.broadcast_to(ar, (1, L, SSM_GROUPS, SSM_STATE))
    a_i = jnp.broadcast_to(ai, (1, L, SSM_GROUPS, SSM_STATE))
    _, _, xr, xi = lax.associative_scan(_complex_scan_op, (a_r, a_i, bu_r, bu_i), axis=1)
    y = jnp.einsum('blgp,gcp->blgc', xr, f(c_re)) - jnp.einsum('blgp,gcp->blgc', xi, f(c_im))
    y = y.reshape(bsz, L, SSM_WIDTH) + f(d_skip) * u
    z = jax.nn.gelu(y).astype(hn.dtype) @ w_out
    a, g = jnp.split(z, 2, axis=-1)
    return a * jax.nn.sigmoid(g)


def shared_kv(h, g_kv, w_kv, k_gain):
    bsz, L, _ = h.shape
    k, v = jnp.split(rms_norm(h, g_kv) @ w_kv, 2, axis=-1)
    k = k.reshape(bsz, L, N_KV_HEADS, HEAD_DIM)
    v = v.reshape(bsz, L, N_KV_HEADS, HEAD_DIM)
    k = rope(rms_norm(k, k_gain), jnp.arange(L))
    return k, v


def swa_sink_attention(hn, k, v, w_q, q_gain, sinks, w_o):
    bsz, S, _ = hn.shape
    nb = S // BLOCK
    q = (hn @ w_q).reshape(bsz, S, N_KV_HEADS, Q_PER_KV, HEAD_DIM)
    q = rope(rms_norm(q, q_gain), N_META + jnp.arange(S))
    qb = q.reshape(bsz, nb, BLOCK, N_KV_HEADS, Q_PER_KV, HEAD_DIM)
    k_meta, v_meta = k[:, :N_META], v[:, :N_META]
    k_blk = k[:, N_META:].reshape(bsz, nb, BLOCK, N_KV_HEADS, HEAD_DIM)
    v_blk = v[:, N_META:].reshape(bsz, nb, BLOCK, N_KV_HEADS, HEAD_DIM)
    pad = ((0, 0), (1, 0), (0, 0), (0, 0), (0, 0))
    k_band = jnp.concatenate([jnp.pad(k_blk, pad)[:, :-1], k_blk], axis=2)
    v_band = jnp.concatenate([jnp.pad(v_blk, pad)[:, :-1], v_blk], axis=2)
    scale = HEAD_DIM ** -0.5
    s_band = jnp.einsum('bnqhgd,bnkhd->bnhgqk', qb, k_band,
                        preferred_element_type=jnp.float32) * scale
    qi = jnp.arange(BLOCK)[:, None]
    kj = jnp.arange(2 * BLOCK)[None, :]
    rel = qi + BLOCK - kj
    blk = jnp.arange(nb)[:, None, None]
    valid = (rel >= 0) & (rel < WINDOW) & ((blk > 0) | (kj >= BLOCK))
    s_band = jnp.where(valid[None, :, None, None], s_band, NEG_INF)
    s_meta = jnp.einsum('bnqhgd,bmhd->bnhgqm', qb, k_meta,
                        preferred_element_type=jnp.float32) * scale
    sink = sinks.astype(jnp.float32).reshape(N_KV_HEADS, Q_PER_KV)[None, None, :, :, None]
    m = jnp.maximum(jnp.maximum(s_band.max(-1), s_meta.max(-1)), sink)
    p_band = jnp.exp(s_band - m[..., None])
    p_meta = jnp.exp(s_meta - m[..., None])
    denom = p_band.sum(-1) + p_meta.sum(-1) + jnp.exp(sink - m)
    o = (jnp.einsum('bnhgqk,bnkhd->bnqhgd', p_band, v_band.astype(jnp.float32))
         + jnp.einsum('bnhgqm,bmhd->bnqhgd', p_meta, v_meta.astype(jnp.float32)))
    o = o / jnp.moveaxis(denom, -1, 2)[..., None]
    return o.reshape(bsz, S, N_Q_HEADS * HEAD_DIM).astype(hn.dtype) @ w_o


def _fwd_setup_inputs(seed: int = 0) -> dict:
    key = jax.random.key(seed)
    ks = jax.random.split(key, 32)
    f32 = jnp.float32

    def nrm(k, shape, scale):
        return jax.random.normal(k, shape, f32) * scale

    H, G, P, C = SSM_WIDTH, SSM_GROUPS, SSM_STATE, SSM_GROUP
    return {
        "x": nrm(ks[0], (BATCH, SEQ, D_MODEL), 1.0),
        "meta_tokens": nrm(ks[1], (N_META, D_MODEL), 1.0),
        "ffn1_norm": 1.0 + nrm(ks[2], (DEPTH, D_MODEL), 0.02),
        "ffn1_w_gate_up": nrm(ks[3], (DEPTH, D_MODEL, 2 * D_FF), D_MODEL ** -0.5),
        "ffn1_w_down": nrm(ks[4], (DEPTH, D_FF, D_MODEL), D_FF ** -0.5),
        "mix_norm": 1.0 + nrm(ks[5], (DEPTH, D_MODEL), 0.02),
        "ffn2_norm": 1.0 + nrm(ks[6], (DEPTH, D_MODEL), 0.02),
        "ffn2_w_gate_up": nrm(ks[7], (DEPTH, D_MODEL, 2 * D_FF), D_MODEL ** -0.5),
        "ffn2_w_down": nrm(ks[8], (DEPTH, D_FF, D_MODEL), D_FF ** -0.5),
        "ssm_w_in": nrm(ks[9], (N_A_LAYERS, D_MODEL, H), D_MODEL ** -0.5),
        "ssm_lambda_re": -0.5 + nrm(ks[10], (N_A_LAYERS, G, P), 0.01),
        "ssm_lambda_im": jnp.pi * jnp.arange(P, dtype=f32) + nrm(ks[11], (N_A_LAYERS, G, P), 0.01),
        "ssm_b_re": nrm(ks[12], (N_A_LAYERS, G, P, C), (2 * C) ** -0.5),
        "ssm_b_im": nrm(ks[13], (N_A_LAYERS, G, P, C), (2 * C) ** -0.5),
        "ssm_c_re": nrm(ks[14], (N_A_LAYERS, G, C, P), P ** -0.5),
        "ssm_c_im": nrm(ks[15], (N_A_LAYERS, G, C, P), P ** -0.5),
        "ssm_log_step": jax.random.uniform(ks[16], (N_A_LAYERS, G), f32,
                                           minval=math.log(STEP_MIN), maxval=math.log(STEP_MAX)),
        "ssm_d": nrm(ks[17], (N_A_LAYERS, H), 1.0),
        "ssm_w_out": nrm(ks[18], (N_A_LAYERS, H, 2 * D_MODEL), H ** -0.5),
        "kv_norm": 1.0 + nrm(ks[19], (D_MODEL,), 0.02),
        "w_kv": nrm(ks[20], (D_MODEL, 2 * N_KV_HEADS * HEAD_DIM), D_MODEL ** -0.5),
        "k_norm": 1.0 + nrm(ks[21], (HEAD_DIM,), 0.02),
        "attn_w_q": nrm(ks[22], (N_B_LAYERS, D_MODEL, N_Q_HEADS * HEAD_DIM), D_MODEL ** -0.5),
        "q_norm": 1.0 + nrm(ks[23], (N_B_LAYERS, HEAD_DIM), 0.02),
        "attn_sinks": nrm(ks[24], (N_B_LAYERS, N_Q_HEADS), 0.5),
        "attn_w_o": nrm(ks[25], (N_B_LAYERS, N_Q_HEADS * HEAD_DIM, D_MODEL), (N_Q_HEADS * HEAD_DIM) ** -0.5),
    }


def _fwd_reference(x, meta_tokens, ffn1_norm, ffn1_w_gate_up, ffn1_w_down, mix_norm, ffn2_norm,
              ffn2_w_gate_up, ffn2_w_down, ssm_w_in, ssm_lambda_re, ssm_lambda_im, ssm_b_re,
              ssm_b_im, ssm_c_re, ssm_c_im, ssm_log_step, ssm_d, ssm_w_out, kv_norm, w_kv,
              k_norm, attn_w_q, q_norm, attn_sinks, attn_w_o):
    bsz = x.shape[0]
    meta = jnp.broadcast_to(meta_tokens.astype(x.dtype)[None], (bsz, N_META, D_MODEL))
    h = jnp.concatenate([meta, x], axis=1)
    k = v = None
    for layer in range(DEPTH):
        if layer == N_A_LAYERS:
            k, v = shared_kv(h, kv_norm, w_kv, k_norm)
            h = h[:, N_META:]
        h = h + 0.5 * swiglu_ffn(h, ffn1_norm[layer], ffn1_w_gate_up[layer], ffn1_w_down[layer])
        hn = rms_norm(h, mix_norm[layer])
        if layer < N_A_LAYERS:
            h = h + s5_mixer(hn, ssm_w_in[layer], ssm_lambda_re[layer], ssm_lambda_im[layer],
                             ssm_b_re[layer], ssm_b_im[layer], ssm_c_re[layer], ssm_c_im[layer],
                             ssm_log_step[layer], ssm_d[layer], ssm_w_out[layer])
        else:
            j = layer - N_A_LAYERS
            h = h + swa_sink_attention(hn, k, v, attn_w_q[j], q_norm[j], attn_sinks[j], attn_w_o[j])
        h = h + 0.5 * swiglu_ffn(h, ffn2_norm[layer], ffn2_w_gate_up[layer], ffn2_w_down[layer])
    return h


import jax as _jax
import jax.numpy as _jnp

TWIN_FORMAT = 'train_step'
FWD_PARAMS = ['x', 'meta_tokens', 'ffn1_norm', 'ffn1_w_gate_up', 'ffn1_w_down', 'mix_norm', 'ffn2_norm', 'ffn2_w_gate_up', 'ffn2_w_down', 'ssm_w_in', 'ssm_lambda_re', 'ssm_lambda_im', 'ssm_b_re', 'ssm_b_im', 'ssm_c_re', 'ssm_c_im', 'ssm_log_step', 'ssm_d', 'ssm_w_out', 'kv_norm', 'w_kv', 'k_norm', 'attn_w_q', 'q_norm', 'attn_sinks', 'attn_w_o']
TWIN_WEIGHTS = ['meta_tokens', 'ffn1_norm', 'ffn1_w_gate_up', 'ffn1_w_down', 'mix_norm', 'ffn2_norm', 'ffn2_w_gate_up', 'ffn2_w_down', 'ssm_w_in', 'ssm_lambda_re', 'ssm_lambda_im', 'ssm_b_re', 'ssm_b_im', 'ssm_c_re', 'ssm_c_im', 'ssm_log_step', 'ssm_d', 'ssm_w_out', 'kv_norm', 'w_kv', 'k_norm', 'attn_w_q', 'q_norm', 'attn_sinks', 'attn_w_o']
TWIN_DIFF_INPUT = 'x'
TWIN_INPUTS = ['x', 'meta_tokens', 'ffn1_norm', 'ffn1_w_gate_up', 'ffn1_w_down', 'mix_norm', 'ffn2_norm', 'ffn2_w_gate_up', 'ffn2_w_down', 'ssm_w_in', 'ssm_lambda_re', 'ssm_lambda_im', 'ssm_b_re', 'ssm_b_im', 'ssm_c_re', 'ssm_c_im', 'ssm_log_step', 'ssm_d', 'ssm_w_out', 'kv_norm', 'w_kv', 'k_norm', 'attn_w_q', 'q_norm', 'attn_sinks', 'attn_w_o', 'loss_target', 'm_meta_tokens', 'm_ffn1_norm', 'm_ffn1_w_gate_up', 'm_ffn1_w_down', 'm_mix_norm', 'm_ffn2_norm', 'm_ffn2_w_gate_up', 'm_ffn2_w_down', 'm_ssm_w_in', 'm_ssm_lambda_re', 'm_ssm_lambda_im', 'm_ssm_b_re', 'm_ssm_b_im', 'm_ssm_c_re', 'm_ssm_c_im', 'm_ssm_log_step', 'm_ssm_d', 'm_ssm_w_out', 'm_kv_norm', 'm_w_kv', 'm_k_norm', 'm_attn_w_q', 'm_q_norm', 'm_attn_sinks', 'm_attn_w_o', 'v_meta_tokens', 'v_ffn1_norm', 'v_ffn1_w_gate_up', 'v_ffn1_w_down', 'v_mix_norm', 'v_ffn2_norm', 'v_ffn2_w_gate_up', 'v_ffn2_w_down', 'v_ssm_w_in', 'v_ssm_lambda_re', 'v_ssm_lambda_im', 'v_ssm_b_re', 'v_ssm_b_im', 'v_ssm_c_re', 'v_ssm_c_im', 'v_ssm_log_step', 'v_ssm_d', 'v_ssm_w_out', 'v_kv_norm', 'v_w_kv', 'v_k_norm', 'v_attn_w_q', 'v_q_norm', 'v_attn_sinks', 'v_attn_w_o']
TWIN_OUTPUTS = ['loss', 'grad_x', 'grad_meta_tokens', 'grad_ffn1_norm', 'grad_ffn1_w_gate_up', 'grad_ffn1_w_down', 'grad_mix_norm', 'grad_ffn2_norm', 'grad_ffn2_w_gate_up', 'grad_ffn2_w_down', 'grad_ssm_w_in', 'grad_ssm_lambda_re', 'grad_ssm_lambda_im', 'grad_ssm_b_re', 'grad_ssm_b_im', 'grad_ssm_c_re', 'grad_ssm_c_im', 'grad_ssm_log_step', 'grad_ssm_d', 'grad_ssm_w_out', 'grad_kv_norm', 'grad_w_kv', 'grad_k_norm', 'grad_attn_w_q', 'grad_q_norm', 'grad_attn_sinks', 'grad_attn_w_o', 'delta_meta_tokens', 'delta_ffn1_norm', 'delta_ffn1_w_gate_up', 'delta_ffn1_w_down', 'delta_mix_norm', 'delta_ffn2_norm', 'delta_ffn2_w_gate_up', 'delta_ffn2_w_down', 'delta_ssm_w_in', 'delta_ssm_lambda_re', 'delta_ssm_lambda_im', 'delta_ssm_b_re', 'delta_ssm_b_im', 'delta_ssm_c_re', 'delta_ssm_c_im', 'delta_ssm_log_step', 'delta_ssm_d', 'delta_ssm_w_out', 'delta_kv_norm', 'delta_w_kv', 'delta_k_norm', 'delta_attn_w_q', 'delta_q_norm', 'delta_attn_sinks', 'delta_attn_w_o', 'new_m_meta_tokens', 'new_m_ffn1_norm', 'new_m_ffn1_w_gate_up', 'new_m_ffn1_w_down', 'new_m_mix_norm', 'new_m_ffn2_norm', 'new_m_ffn2_w_gate_up', 'new_m_ffn2_w_down', 'new_m_ssm_w_in', 'new_m_ssm_lambda_re', 'new_m_ssm_lambda_im', 'new_m_ssm_b_re', 'new_m_ssm_b_im', 'new_m_ssm_c_re', 'new_m_ssm_c_im', 'new_m_ssm_log_step', 'new_m_ssm_d', 'new_m_ssm_w_out', 'new_m_kv_norm', 'new_m_w_kv', 'new_m_k_norm', 'new_m_attn_w_q', 'new_m_q_norm', 'new_m_attn_sinks', 'new_m_attn_w_o', 'new_v_meta_tokens', 'new_v_ffn1_norm', 'new_v_ffn1_w_gate_up', 'new_v_ffn1_w_down', 'new_v_mix_norm', 'new_v_ffn2_norm', 'new_v_ffn2_w_gate_up', 'new_v_ffn2_w_down', 'new_v_ssm_w_in', 'new_v_ssm_lambda_re', 'new_v_ssm_lambda_im', 'new_v_ssm_b_re', 'new_v_ssm_b_im', 'new_v_ssm_c_re', 'new_v_ssm_c_im', 'new_v_ssm_log_step', 'new_v_ssm_d', 'new_v_ssm_w_out', 'new_v_kv_norm', 'new_v_w_kv', 'new_v_k_norm', 'new_v_attn_w_q', 'new_v_q_norm', 'new_v_attn_sinks', 'new_v_attn_w_o']
TWIN_LEAF_KINDS = {'loss': 'loss', 'grad_x': 'grad_x', 'grad_meta_tokens': 'grad_w', 'grad_ffn1_norm': 'grad_w', 'grad_ffn1_w_gate_up': 'grad_w', 'grad_ffn1_w_down': 'grad_w', 'grad_mix_norm': 'grad_w', 'grad_ffn2_norm': 'grad_w', 'grad_ffn2_w_gate_up': 'grad_w', 'grad_ffn2_w_down': 'grad_w', 'grad_ssm_w_in': 'grad_w', 'grad_ssm_lambda_re': 'grad_w', 'grad_ssm_lambda_im': 'grad_w', 'grad_ssm_b_re': 'grad_w', 'grad_ssm_b_im': 'grad_w', 'grad_ssm_c_re': 'grad_w', 'grad_ssm_c_im': 'grad_w', 'grad_ssm_log_step': 'grad_w', 'grad_ssm_d': 'grad_w', 'grad_ssm_w_out': 'grad_w', 'grad_kv_norm': 'grad_w', 'grad_w_kv': 'grad_w', 'grad_k_norm': 'grad_w', 'grad_attn_w_q': 'grad_w', 'grad_q_norm': 'grad_w', 'grad_attn_sinks': 'grad_w', 'grad_attn_w_o': 'grad_w', 'delta_meta_tokens': 'delta_w', 'delta_ffn1_norm': 'delta_w', 'delta_ffn1_w_gate_up': 'delta_w', 'delta_ffn1_w_down': 'delta_w', 'delta_mix_norm': 'delta_w', 'delta_ffn2_norm': 'delta_w', 'delta_ffn2_w_gate_up': 'delta_w', 'delta_ffn2_w_down': 'delta_w', 'delta_ssm_w_in': 'delta_w', 'delta_ssm_lambda_re': 'delta_w', 'delta_ssm_lambda_im': 'delta_w', 'delta_ssm_b_re': 'delta_w', 'delta_ssm_b_im': 'delta_w', 'delta_ssm_c_re': 'delta_w', 'delta_ssm_c_im': 'delta_w', 'delta_ssm_log_step': 'delta_w', 'delta_ssm_d': 'delta_w', 'delta_ssm_w_out': 'delta_w', 'delta_kv_norm': 'delta_w', 'delta_w_kv': 'delta_w', 'delta_k_norm': 'delta_w', 'delta_attn_w_q': 'delta_w', 'delta_q_norm': 'delta_w', 'delta_attn_sinks': 'delta_w', 'delta_attn_w_o': 'delta_w', 'new_m_meta_tokens': 'new_m', 'new_m_ffn1_norm': 'new_m', 'new_m_ffn1_w_gate_up': 'new_m', 'new_m_ffn1_w_down': 'new_m', 'new_m_mix_norm': 'new_m', 'new_m_ffn2_norm': 'new_m', 'new_m_ffn2_w_gate_up': 'new_m', 'new_m_ffn2_w_down': 'new_m', 'new_m_ssm_w_in': 'new_m', 'new_m_ssm_lambda_re': 'new_m', 'new_m_ssm_lambda_im': 'new_m', 'new_m_ssm_b_re': 'new_m', 'new_m_ssm_b_im': 'new_m', 'new_m_ssm_c_re': 'new_m', 'new_m_ssm_c_im': 'new_m', 'new_m_ssm_log_step': 'new_m', 'new_m_ssm_d': 'new_m', 'new_m_ssm_w_out': 'new_m', 'new_m_kv_norm': 'new_m', 'new_m_w_kv': 'new_m', 'new_m_k_norm': 'new_m', 'new_m_attn_w_q': 'new_m', 'new_m_q_norm': 'new_m', 'new_m_attn_sinks': 'new_m', 'new_m_attn_w_o': 'new_m', 'new_v_meta_tokens': 'new_v', 'new_v_ffn1_norm': 'new_v', 'new_v_ffn1_w_gate_up': 'new_v', 'new_v_ffn1_w_down': 'new_v', 'new_v_mix_norm': 'new_v', 'new_v_ffn2_norm': 'new_v', 'new_v_ffn2_w_gate_up': 'new_v', 'new_v_ffn2_w_down': 'new_v', 'new_v_ssm_w_in': 'new_v', 'new_v_ssm_lambda_re': 'new_v', 'new_v_ssm_lambda_im': 'new_v', 'new_v_ssm_b_re': 'new_v', 'new_v_ssm_b_im': 'new_v', 'new_v_ssm_c_re': 'new_v', 'new_v_ssm_c_im': 'new_v', 'new_v_ssm_log_step': 'new_v', 'new_v_ssm_d': 'new_v', 'new_v_ssm_w_out': 'new_v', 'new_v_kv_norm': 'new_v', 'new_v_w_kv': 'new_v', 'new_v_k_norm': 'new_v', 'new_v_attn_w_q': 'new_v', 'new_v_q_norm': 'new_v', 'new_v_attn_sinks': 'new_v', 'new_v_attn_w_o': 'new_v'}


def _forward(args):
    return _fwd_reference(*[args[k] for k in FWD_PARAMS])


def _output_shape():
    out = _jax.eval_shape(lambda: _forward(_fwd_setup_inputs(0)))
    return out.shape, out.dtype

N_MICROBATCH = 1
ADAM_LR = 0.001
ADAM_B1 = 0.9
ADAM_B2 = 0.999
ADAM_EPS = 1e-08
ADAM_WD = 0.01
ADAM_STEP = 10
PER_EXAMPLE_BATCH_AXIS = {'x': 0, 'loss_target': 0}
SHARED_INPUTS = []
_WEIGHT_DTYPES = {'meta_tokens': _jnp.float32, 'ffn1_norm': _jnp.float32, 'ffn1_w_gate_up': _jnp.float32, 'ffn1_w_down': _jnp.float32, 'mix_norm': _jnp.float32, 'ffn2_norm': _jnp.float32, 'ffn2_w_gate_up': _jnp.float32, 'ffn2_w_down': _jnp.float32, 'ssm_w_in': _jnp.float32, 'ssm_lambda_re': _jnp.float32, 'ssm_lambda_im': _jnp.float32, 'ssm_b_re': _jnp.float32, 'ssm_b_im': _jnp.float32, 'ssm_c_re': _jnp.float32, 'ssm_c_im': _jnp.float32, 'ssm_log_step': _jnp.float32, 'ssm_d': _jnp.float32, 'ssm_w_out': _jnp.float32, 'kv_norm': _jnp.float32, 'w_kv': _jnp.float32, 'k_norm': _jnp.float32, 'attn_w_q': _jnp.float32, 'q_norm': _jnp.float32, 'attn_sinks': _jnp.float32, 'attn_w_o': _jnp.float32}
MOMENT_SCALE = {'meta_tokens': 8.461961e-02, 'ffn1_norm': 1.223320e+01, 'ffn1_w_gate_up': 1.382654e-01, 'ffn1_w_down': 2.259098e-01, 'mix_norm': 6.480005e+00, 'ffn2_norm': 1.234557e+01, 'ffn2_w_gate_up': 1.689571e-01, 'ffn2_w_down': 2.676393e-01, 'ssm_w_in': 7.445950e-01, 'ssm_lambda_re': 6.521934e-02, 'ssm_lambda_im': 2.504213e-02, 'ssm_b_re': 3.541267e-02, 'ssm_b_im': 3.441623e-02, 'ssm_c_re': 5.159530e-02, 'ssm_c_im': 5.694903e-02, 'ssm_log_step': 1.328708e+01, 'ssm_d': 2.035860e+01, 'ssm_w_out': 2.825841e+00, 'kv_norm': 3.143147e+00, 'w_kv': 3.301450e+00, 'k_norm': 1.010336e+01, 'attn_w_q': 7.045635e-02, 'q_norm': 1.002693e+01, 'attn_sinks': 3.297080e-01, 'attn_w_o': 1.419626e+00}


def _to_microbatches(a, axis):
    t = _jnp.moveaxis(a, axis, 0)
    t = t.reshape((N_MICROBATCH, t.shape[0] // N_MICROBATCH) + t.shape[1:])
    return _jnp.moveaxis(t, 1, axis + 1)


def setup_inputs(seed: int = 0) -> dict:
    inp = _fwd_setup_inputs(seed)
    key = _jax.random.fold_in(_jax.random.key(seed), 7919)
    shape, _ = _output_shape()
    out = dict(inp)
    out["loss_target"] = _jax.random.normal(_jax.random.fold_in(key, 0), shape, _jnp.float32)
    for i, name in enumerate(TWIN_WEIGHTS):
        w = inp[name].astype(_jnp.float32)
        if MOMENT_SCALE is None:
            s = _jnp.sqrt(_jnp.mean(_jnp.square(w)) + 1e-30)
        else:
            s = MOMENT_SCALE[name]
        km, kv = _jax.random.split(_jax.random.fold_in(key, i + 1))
        out[name] = w
        out["m_" + name] = s * _jax.random.normal(km, w.shape, _jnp.float32)
        out["v_" + name] = (s * s) * _jax.random.uniform(kv, w.shape, _jnp.float32, 0.5, 1.5)
    if N_MICROBATCH > 1:
        for name, axis in PER_EXAMPLE_BATCH_AXIS.items():
            out[name] = _to_microbatches(out[name], axis)
    return {'x': out['x'], 'meta_tokens': out['meta_tokens'], 'ffn1_norm': out['ffn1_norm'], 'ffn1_w_gate_up': out['ffn1_w_gate_up'], 'ffn1_w_down': out['ffn1_w_down'], 'mix_norm': out['mix_norm'], 'ffn2_norm': out['ffn2_norm'], 'ffn2_w_gate_up': out['ffn2_w_gate_up'], 'ffn2_w_down': out['ffn2_w_down'], 'ssm_w_in': out['ssm_w_in'], 'ssm_lambda_re': out['ssm_lambda_re'], 'ssm_lambda_im': out['ssm_lambda_im'], 'ssm_b_re': out['ssm_b_re'], 'ssm_b_im': out['ssm_b_im'], 'ssm_c_re': out['ssm_c_re'], 'ssm_c_im': out['ssm_c_im'], 'ssm_log_step': out['ssm_log_step'], 'ssm_d': out['ssm_d'], 'ssm_w_out': out['ssm_w_out'], 'kv_norm': out['kv_norm'], 'w_kv': out['w_kv'], 'k_norm': out['k_norm'], 'attn_w_q': out['attn_w_q'], 'q_norm': out['q_norm'], 'attn_sinks': out['attn_sinks'], 'attn_w_o': out['attn_w_o'], 'loss_target': out['loss_target'], 'm_meta_tokens': out['m_meta_tokens'], 'm_ffn1_norm': out['m_ffn1_norm'], 'm_ffn1_w_gate_up': out['m_ffn1_w_gate_up'], 'm_ffn1_w_down': out['m_ffn1_w_down'], 'm_mix_norm': out['m_mix_norm'], 'm_ffn2_norm': out['m_ffn2_norm'], 'm_ffn2_w_gate_up': out['m_ffn2_w_gate_up'], 'm_ffn2_w_down': out['m_ffn2_w_down'], 'm_ssm_w_in': out['m_ssm_w_in'], 'm_ssm_lambda_re': out['m_ssm_lambda_re'], 'm_ssm_lambda_im': out['m_ssm_lambda_im'], 'm_ssm_b_re': out['m_ssm_b_re'], 'm_ssm_b_im': out['m_ssm_b_im'], 'm_ssm_c_re': out['m_ssm_c_re'], 'm_ssm_c_im': out['m_ssm_c_im'], 'm_ssm_log_step': out['m_ssm_log_step'], 'm_ssm_d': out['m_ssm_d'], 'm_ssm_w_out': out['m_ssm_w_out'], 'm_kv_norm': out['m_kv_norm'], 'm_w_kv': out['m_w_kv'], 'm_k_norm': out['m_k_norm'], 'm_attn_w_q': out['m_attn_w_q'], 'm_q_norm': out['m_q_norm'], 'm_attn_sinks': out['m_attn_sinks'], 'm_attn_w_o': out['m_attn_w_o'], 'v_meta_tokens': out['v_meta_tokens'], 'v_ffn1_norm': out['v_ffn1_norm'], 'v_ffn1_w_gate_up': out['v_ffn1_w_gate_up'], 'v_ffn1_w_down': out['v_ffn1_w_down'], 'v_mix_norm': out['v_mix_norm'], 'v_ffn2_norm': out['v_ffn2_norm'], 'v_ffn2_w_gate_up': out['v_ffn2_w_gate_up'], 'v_ffn2_w_down': out['v_ffn2_w_down'], 'v_ssm_w_in': out['v_ssm_w_in'], 'v_ssm_lambda_re': out['v_ssm_lambda_re'], 'v_ssm_lambda_im': out['v_ssm_lambda_im'], 'v_ssm_b_re': out['v_ssm_b_re'], 'v_ssm_b_im': out['v_ssm_b_im'], 'v_ssm_c_re': out['v_ssm_c_re'], 'v_ssm_c_im': out['v_ssm_c_im'], 'v_ssm_log_step': out['v_ssm_log_step'], 'v_ssm_d': out['v_ssm_d'], 'v_ssm_w_out': out['v_ssm_w_out'], 'v_kv_norm': out['v_kv_norm'], 'v_w_kv': out['v_w_kv'], 'v_k_norm': out['v_k_norm'], 'v_attn_w_q': out['v_attn_w_q'], 'v_q_norm': out['v_q_norm'], 'v_attn_sinks': out['v_attn_sinks'], 'v_attn_w_o': out['v_attn_w_o']}


def _loss(weights, diff, rest, loss_target):
    with _jax.named_scope("forward"):
        args = {**rest, TWIN_DIFF_INPUT: diff, **{k: w.astype(_WEIGHT_DTYPES[k]) for k, w in weights.items()}}
        y = _forward(args)
    with _jax.named_scope("loss_head"):
        err = _jnp.square(y.astype(_jnp.float32) - loss_target)
        return 0.5 * _jnp.sum(_jnp.mean(err, axis=-1)) if err.ndim else 0.5 * err


def _adamw(w, g, m, v):
    m = ADAM_B1 * m + (1.0 - ADAM_B1) * g
    v = ADAM_B2 * v + (1.0 - ADAM_B2) * _jnp.square(g)
    m_hat = m / (1.0 - ADAM_B1 ** ADAM_STEP)
    v_hat = v / (1.0 - ADAM_B2 ** ADAM_STEP)
    delta = -ADAM_LR * (m_hat / (_jnp.sqrt(v_hat) + ADAM_EPS) + ADAM_WD * w)
    return delta, m, v


def reference(x, meta_tokens, ffn1_norm, ffn1_w_gate_up, ffn1_w_down, mix_norm, ffn2_norm, ffn2_w_gate_up, ffn2_w_down, ssm_w_in, ssm_lambda_re, ssm_lambda_im, ssm_b_re, ssm_b_im, ssm_c_re, ssm_c_im, ssm_log_step, ssm_d, ssm_w_out, kv_norm, w_kv, k_norm, attn_w_q, q_norm, attn_sinks, attn_w_o, loss_target, m_meta_tokens, m_ffn1_norm, m_ffn1_w_gate_up, m_ffn1_w_down, m_mix_norm, m_ffn2_norm, m_ffn2_w_gate_up, m_ffn2_w_down, m_ssm_w_in, m_ssm_lambda_re, m_ssm_lambda_im, m_ssm_b_re, m_ssm_b_im, m_ssm_c_re, m_ssm_c_im, m_ssm_log_step, m_ssm_d, m_ssm_w_out, m_kv_norm, m_w_kv, m_k_norm, m_attn_w_q, m_q_norm, m_attn_sinks, m_attn_w_o, v_meta_tokens, v_ffn1_norm, v_ffn1_w_gate_up, v_ffn1_w_down, v_mix_norm, v_ffn2_norm, v_ffn2_w_gate_up, v_ffn2_w_down, v_ssm_w_in, v_ssm_lambda_re, v_ssm_lambda_im, v_ssm_b_re, v_ssm_b_im, v_ssm_c_re, v_ssm_c_im, v_ssm_log_step, v_ssm_d, v_ssm_w_out, v_kv_norm, v_w_kv, v_k_norm, v_attn_w_q, v_q_norm, v_attn_sinks, v_attn_w_o):
    given = dict(x=x, meta_tokens=meta_tokens, ffn1_norm=ffn1_norm, ffn1_w_gate_up=ffn1_w_gate_up, ffn1_w_down=ffn1_w_down, mix_norm=mix_norm, ffn2_norm=ffn2_norm, ffn2_w_gate_up=ffn2_w_gate_up, ffn2_w_down=ffn2_w_down, ssm_w_in=ssm_w_in, ssm_lambda_re=ssm_lambda_re, ssm_lambda_im=ssm_lambda_im, ssm_b_re=ssm_b_re, ssm_b_im=ssm_b_im, ssm_c_re=ssm_c_re, ssm_c_im=ssm_c_im, ssm_log_step=ssm_log_step, ssm_d=ssm_d, ssm_w_out=ssm_w_out, kv_norm=kv_norm, w_kv=w_kv, k_norm=k_norm, attn_w_q=attn_w_q, q_norm=q_norm, attn_sinks=attn_sinks, attn_w_o=attn_w_o, loss_target=loss_target, m_meta_tokens=m_meta_tokens, m_ffn1_norm=m_ffn1_norm, m_ffn1_w_gate_up=m_ffn1_w_gate_up, m_ffn1_w_down=m_ffn1_w_down, m_mix_norm=m_mix_norm, m_ffn2_norm=m_ffn2_norm, m_ffn2_w_gate_up=m_ffn2_w_gate_up, m_ffn2_w_down=m_ffn2_w_down, m_ssm_w_in=m_ssm_w_in, m_ssm_lambda_re=m_ssm_lambda_re, m_ssm_lambda_im=m_ssm_lambda_im, m_ssm_b_re=m_ssm_b_re, m_ssm_b_im=m_ssm_b_im, m_ssm_c_re=m_ssm_c_re, m_ssm_c_im=m_ssm_c_im, m_ssm_log_step=m_ssm_log_step, m_ssm_d=m_ssm_d, m_ssm_w_out=m_ssm_w_out, m_kv_norm=m_kv_norm, m_w_kv=m_w_kv, m_k_norm=m_k_norm, m_attn_w_q=m_attn_w_q, m_q_norm=m_q_norm, m_attn_sinks=m_attn_sinks, m_attn_w_o=m_attn_w_o, v_meta_tokens=v_meta_tokens, v_ffn1_norm=v_ffn1_norm, v_ffn1_w_gate_up=v_ffn1_w_gate_up, v_ffn1_w_down=v_ffn1_w_down, v_mix_norm=v_mix_norm, v_ffn2_norm=v_ffn2_norm, v_ffn2_w_gate_up=v_ffn2_w_gate_up, v_ffn2_w_down=v_ffn2_w_down, v_ssm_w_in=v_ssm_w_in, v_ssm_lambda_re=v_ssm_lambda_re, v_ssm_lambda_im=v_ssm_lambda_im, v_ssm_b_re=v_ssm_b_re, v_ssm_b_im=v_ssm_b_im, v_ssm_c_re=v_ssm_c_re, v_ssm_c_im=v_ssm_c_im, v_ssm_log_step=v_ssm_log_step, v_ssm_d=v_ssm_d, v_ssm_w_out=v_ssm_w_out, v_kv_norm=v_kv_norm, v_w_kv=v_w_kv, v_k_norm=v_k_norm, v_attn_w_q=v_attn_w_q, v_q_norm=v_q_norm, v_attn_sinks=v_attn_sinks, v_attn_w_o=v_attn_w_o)
    weights = {n: given[n] for n in TWIN_WEIGHTS}
    shared = {n: given[n] for n in SHARED_INPUTS}
    per_example = {n: given[n] for n in ['x']}
    grad_fn = _jax.value_and_grad(_loss, argnums=(0, 1))

    def one_microbatch(ex, loss_target):
        ex = dict(ex)
        diff = ex.pop(TWIN_DIFF_INPUT)
        return grad_fn(weights, diff, {**shared, **ex}, loss_target)

    if N_MICROBATCH == 1:
        loss, (grad_w, grad_x) = one_microbatch(per_example, given["loss_target"])
    else:
        def body(carry, xs):
            loss_sum, grad_sum = carry
            l_k, (gw_k, gx_k) = one_microbatch(xs[0], xs[1])
            with _jax.named_scope("update"):
                return (loss_sum + l_k, _jax.tree.map(_jnp.add, grad_sum, gw_k)), gx_k

        init = (_jnp.zeros((), _jnp.float32), _jax.tree.map(_jnp.zeros_like, weights))
        (loss, grad_w), grad_x = _jax.lax.scan(body, init, (per_example, given["loss_target"]))
    with _jax.named_scope("update"):
        delta_w, new_m, new_v = {}, {}, {}
        for n in TWIN_WEIGHTS:
            delta_w[n], new_m[n], new_v[n] = _adamw(weights[n], grad_w[n], given["m_" + n], given["v_" + n])
    return (loss, grad_x, *[grad_w[n] for n in TWIN_WEIGHTS], *[delta_w[n] for n in TWIN_WEIGHTS],
            *[new_m[n] for n in TWIN_WEIGHTS], *[new_v[n] for n in TWIN_WEIGHTS])
```

```python
import functools
import math

import jax
import jax.numpy as jnp
from jax import lax
from jax.experimental import pallas as pl
from jax.experimental.pallas import tpu as pltpu

F32 = jnp.float32
BF16 = jnp.bfloat16
MESH = pl.DeviceIdType.MESH

EPS = 1e-6
NEG_INF = -1e30
ROPE_THETA = 10000.0
WINDOW = 128
META_BLOCK = 128
ROW_TILE = 256
SUBLANES = 8
V7X_VMEM_LIMIT = 56 * 2**20
N_CHIPS = 4
N_DEV = 8

ADAM_LR = 0.001
ADAM_B1 = 0.9
ADAM_B2 = 0.999
ADAM_EPS = 1e-08
ADAM_WD = 0.01
ADAM_STEP = 10

_HBM = pl.BlockSpec(memory_space=pltpu.HBM)
_VMEM = pl.BlockSpec(memory_space=pltpu.VMEM)


def _call(name, body, grid, in_specs, out_specs, out_shape, scratch=()):
    return pl.pallas_call(
        body, name=name, grid=grid, in_specs=in_specs, out_specs=out_specs, out_shape=out_shape,
        scratch_shapes=list(scratch),
        compiler_params=pltpu.CompilerParams(dimension_semantics=("arbitrary",) * len(grid),
                                             vmem_limit_bytes=V7X_VMEM_LIMIT))


def _sds(shape, dtype):
    return jax.ShapeDtypeStruct(tuple(shape), dtype)


def _dot(a, b):
    return jnp.dot(a.astype(BF16), b.astype(BF16), preferred_element_type=F32)


def _dot_nt(a, b):
    return lax.dot_general(a.astype(BF16), b.astype(BF16), (((1,), (1,)), ((), ())), preferred_element_type=F32)


def _dot_tn(a, b):
    return lax.dot_general(a.astype(BF16), b.astype(BF16), (((0,), (0,)), ((), ())), preferred_element_type=F32)


def _rms(h, g):
    return h * lax.rsqrt(jnp.mean(h * h, axis=-1, keepdims=True) + EPS) * g


def _rms_bwd(h, g, dn):
    r = lax.rsqrt(jnp.mean(h * h, axis=-1, keepdims=True) + EPS)
    xh = h * r
    dxh = dn * g
    dg = jnp.sum(dn * xh, axis=0, keepdims=True)
    dh = r * (dxh - xh * jnp.mean(dxh * xh, axis=-1, keepdims=True))
    return dh, dg


def _gelu(y):
    k = math.sqrt(2.0 / math.pi)
    return 0.5 * y * (1.0 + jnp.tanh(k * (y + 0.044715 * y * y * y)))


def _gelu_grad(y):
    k = math.sqrt(2.0 / math.pi)
    t = jnp.tanh(k * (y + 0.044715 * y * y * y))
    return 0.5 * (1.0 + t) + 0.5 * y * (1.0 - t * t) * k * (1.0 + 3.0 * 0.044715 * y * y)


def _swap_halves(x):
    half = x.shape[-1] // 2
    return jnp.concatenate([x[:, half:], x[:, :half]], axis=1)


def _head_prep(x, gain, cos, sin_s, n_heads, hd):
    out = []
    for h in range(n_heads):
        xh = x[:, h * hd:(h + 1) * hd]
        y = xh * lax.rsqrt(jnp.mean(xh * xh, axis=-1, keepdims=True) + EPS) * gain
        out.append(y * cos + _swap_halves(y) * sin_s)
    return out


def _head_prep_bwd(x, gain, cos, sin_s, d_out, n_heads, hd):
    dxs = []
    dgain = jnp.zeros((1, hd), F32)
    for h in range(n_heads):
        xh = x[:, h * hd:(h + 1) * hd]
        do = d_out[:, h * hd:(h + 1) * hd]
        r = lax.rsqrt(jnp.mean(xh * xh, axis=-1, keepdims=True) + EPS)
        xhat = xh * r
        dy = do * cos + _swap_halves(do * sin_s)
        dgain = dgain + jnp.sum(dy * xhat, axis=0, keepdims=True)
        dxh = dy * gain
        dxs.append(r * (dxh - xhat * jnp.mean(dxh * xhat, axis=-1, keepdims=True)))
    return dxs, dgain


def _acc_out(ref, val, first):
    @pl.when(first)
    def _():
        ref[...] = jnp.zeros_like(ref)
    ref[...] += val


def _ffn_up(name, h, g, w4, n_rows):
    nj, d, fc = w4.shape
    tm = ROW_TILE

    def body(h_ref, g_ref, w_ref, o_ref):
        n = _rms(h_ref[...], g_ref[...])
        o_ref[...] = _dot(n, w_ref[0]).astype(BF16)

    return _call(name, body, (nj, n_rows // tm),
                 [pl.BlockSpec((tm, d), lambda j, i: (i, 0)), pl.BlockSpec((1, d), lambda j, i: (0, 0)),
                  pl.BlockSpec((1, d, fc), lambda j, i: (j, 0, 0))],
                 pl.BlockSpec((tm, fc), lambda j, i: (i, j)), _sds((n_rows, nj * fc), BF16))(h, g, w4)


def _swiglu(gu, f):
    a = gu[:, :f].astype(F32)
    b = gu[:, f:].astype(F32)
    return a * jax.nn.sigmoid(a) * b


def _ffn_down(name, gu, h, wd, n_rows):
    f, d = wd.shape
    tm = ROW_TILE

    def body(gu_ref, h_ref, w_ref, o_ref):
        o_ref[...] = h_ref[...] + 0.5 * _dot(_swiglu(gu_ref[...], f), w_ref[...])

    return _call(name, body, (n_rows // tm,),
                 [pl.BlockSpec((tm, 2 * f), lambda i: (i, 0)), pl.BlockSpec((tm, d), lambda i: (i, 0)),
                  pl.BlockSpec((f, d), lambda i: (0, 0))],
                 pl.BlockSpec((tm, d), lambda i: (i, 0)), _sds((n_rows, d), F32))(gu, h, wd)


def _ffn_dgu(name, dh, gu, wd, n_rows):
    f, d = wd.shape
    tm = ROW_TILE

    def body(dh_ref, gu_ref, w_ref, o_ref):
        ds = _dot_nt(0.5 * dh_ref[...], w_ref[...])
        a = gu_ref[:, :f].astype(F32)
        b = gu_ref[:, f:].astype(F32)
        sg = jax.nn.sigmoid(a)
        o_ref[:, :f] = (ds * b * (sg * (1.0 + a * (1.0 - sg)))).astype(BF16)
        o_ref[:, f:] = (ds * (a * sg)).astype(BF16)

    return _call(name, body, (n_rows // tm,),
                 [pl.BlockSpec((tm, d), lambda i: (i, 0)), pl.BlockSpec((tm, 2 * f), lambda i: (i, 0)),
                  pl.BlockSpec((f, d), lambda i: (0, 0))],
                 pl.BlockSpec((tm, 2 * f), lambda i: (i, 0)), _sds((n_rows, 2 * f), BF16))(dh, gu, wd)


def _ffn_dh(name, dgu, h, g, dh, w4, n_rows):
    nj, d, fc = w4.shape
    tm = ROW_TILE

    def body(dgu_ref, h_ref, g_ref, dh_ref, w_ref, o_ref, dg_ref):
        dn = _dot_nt(dgu_ref[:, 0:fc], w_ref[0])
        for j in range(1, nj):
            dn = dn + _dot_nt(dgu_ref[:, j * fc:(j + 1) * fc], w_ref[j])
        dhn, dg = _rms_bwd(h_ref[...], g_ref[...], dn)
        o_ref[...] = dh_ref[...] + dhn
        _acc_out(dg_ref, dg, pl.program_id(0) == 0)

    return _call(name, body, (n_rows // tm,),
                 [pl.BlockSpec((tm, nj * fc), lambda i: (i, 0)), pl.BlockSpec((tm, d), lambda i: (i, 0)),
                  pl.BlockSpec((1, d), lambda i: (0, 0)), pl.BlockSpec((tm, d), lambda i: (i, 0)),
                  pl.BlockSpec((nj, d, fc), lambda i: (0, 0, 0))],
                 [pl.BlockSpec((tm, d), lambda i: (i, 0)), pl.BlockSpec((1, d), lambda i: (0, 0))],
                 [_sds((n_rows, d), F32), _sds((1, d), F32)])(dgu, h, g, dh, w4)


def _tn(name, operands, in_specs, prologue, nj, ma, nb, n_rows, tk=ROW_TILE):
    def body(*refs):
        o_ref = refs[-1]
        a, b = prologue(pl.program_id(0), *refs[:-1])
        _acc_out(o_ref, _dot_tn(a, b)[None], pl.program_id(1) == 0)

    return _call(name, body, (nj, n_rows // tk), in_specs, pl.BlockSpec((1, ma, nb), lambda j, k: (j, 0, 0)),
                 _sds((nj, ma, nb), F32))(*operands)


def _ffn_dwgu(name, h, g, dgu, nj, n_rows):
    d = h.shape[1]
    fc = dgu.shape[1] // nj
    tk = ROW_TILE
    return _tn(name, (h, g, dgu),
               [pl.BlockSpec((tk, d), lambda j, k: (k, 0)), pl.BlockSpec((1, d), lambda j, k: (0, 0)),
                pl.BlockSpec((tk, fc), lambda j, k: (k, j))],
               lambda j, h_ref, g_ref, b_ref: (_rms(h_ref[...], g_ref[...]), b_ref[...]),
               nj, d, fc, n_rows)


def _ffn_dwd(name, gu, dh, n_rows):
    f = gu.shape[1] // 2
    d = dh.shape[1]
    tk = ROW_TILE
    return _tn(name, (gu, dh),
               [pl.BlockSpec((tk, 2 * f), lambda j, k: (k, 0)), pl.BlockSpec((tk, d), lambda j, k: (k, 0))],
               lambda j, gu_ref, dh_ref: (_swiglu(gu_ref[...], f), 0.5 * dh_ref[...]),
               1, f, d, n_rows)


def _ffn_fwd(tag, h, g, w4, wd, n_rows):
    gu = _ffn_up(tag + "_up", h, g, w4, n_rows)
    return _ffn_down(tag + "_down", gu, h, wd, n_rows), gu


def _ffn_bwd(tag, dh_out, h, g, gu, w4, wd, n_rows):
    nj = w4.shape[0]
    dgu = _ffn_dgu(tag + "_dgu", dh_out, gu, wd, n_rows)
    dwd = _ffn_dwd(tag + "_dwd", gu, dh_out, n_rows)
    dh_in, dg = _ffn_dh(tag + "_dh", dgu, h, g, dh_out, w4, n_rows)
    dwgu = _ffn_dwgu(tag + "_dwgu", h, g, dgu, nj, n_rows)
    f, d = wd.shape
    return dh_in, dg, dwgu, dwd.reshape(N_CHIPS, f // N_CHIPS, d)


def _ssm_in(name, h, g, w_in, bb, n_rows):
    d, hw = w_in.shape
    nj, uc, xc = bb.shape
    tm = ROW_TILE

    def body(h_ref, g_ref, w_ref, bb_ref, u_ref, bu_ref):
        u = _dot(_rms(h_ref[...], g_ref[...]), w_ref[...])
        u_ref[...] = u
        for j in range(nj):
            bu_ref[:, j * xc:(j + 1) * xc] = _dot(u[:, j * uc:(j + 1) * uc], bb_ref[j])

    return _call(name, body, (n_rows // tm,),
                 [pl.BlockSpec((tm, d), lambda i: (i, 0)), pl.BlockSpec((1, d), lambda i: (0, 0)),
                  pl.BlockSpec((d, hw), lambda i: (0, 0)), pl.BlockSpec((nj, uc, xc), lambda i: (0, 0, 0))],
                 [pl.BlockSpec((tm, hw), lambda i: (i, 0)), pl.BlockSpec((tm, nj * xc), lambda i: (i, 0))],
                 [_sds((n_rows, hw), F32), _sds((n_rows, nj * xc), F32)])(h, g, w_in, bb)


def _cmul_add(xr, xi, ar, ai, sr, si):
    return xr + ar * sr - ai * si, xi + ar * si + ai * sr


def _scan_row_block(n_main_blocks, seq_blocks):
    return lambda b, i: jnp.where(i == 0, n_main_blocks + b, b * seq_blocks + i - 1)


def _scan_fwd(name, bu, tabs, n_ex, seq):
    n_rows, width = bu.shape
    nj = 4
    cw = width // nj
    half = cw // 2
    tq = META_BLOCK
    seq_blocks = seq // tq
    rb = _scan_row_block(n_ex * seq_blocks, seq_blocks)

    def body(bu_ref, tab_ref, x_ref, carry_ref):
        @pl.when(pl.program_id(2) == 0)
        def _():
            carry_ref[...] = jnp.zeros_like(carry_ref)
        def blk(k, c):
            t = [tab_ref[n * SUBLANES:(n + 1) * SUBLANES, :] for n in range(8)]
            r0 = pl.multiple_of(k * SUBLANES, SUBLANES)
            xr = bu_ref[pl.ds(r0, SUBLANES), 0:half]
            xi = bu_ref[pl.ds(r0, SUBLANES), half:cw]
            for s, d in enumerate((1, 2, 4)):
                xr, xi = _cmul_add(xr, xi, t[2 * s], t[2 * s + 1], pltpu.roll(xr, d, 0), pltpu.roll(xi, d, 0))
            xr, xi = _cmul_add(xr, xi, t[6], t[7], c[0], c[1])
            x_ref[pl.ds(r0, SUBLANES), 0:half] = xr
            x_ref[pl.ds(r0, SUBLANES), half:cw] = xi
            last = SUBLANES - 1
            return (jnp.broadcast_to(xr[last:last + 1, :], xr.shape), jnp.broadcast_to(xi[last:last + 1, :], xi.shape))

        c = lax.fori_loop(0, tq // SUBLANES, blk, (carry_ref[0], carry_ref[1]))
        carry_ref[0] = c[0]
        carry_ref[1] = c[1]

    return _call(name, body, (n_ex, nj, seq_blocks + 1),
                 [pl.BlockSpec((tq, cw), lambda b, j, i: (rb(b, i), j)), pl.BlockSpec((8 * SUBLANES, half), lambda b, j, i: (0, j))],
                 pl.BlockSpec((tq, cw), lambda b, j, i: (rb(b, i), j)), _sds((n_rows, width), F32),
                 scratch=[pltpu.VMEM((2, SUBLANES, half), F32)])(bu, tabs)


def _scan_bwd(name, gx, x, tabs, n_ex, seq):
    n_rows, width = gx.shape
    nj = 4
    cw = width // nj
    half = cw // 2
    tq = META_BLOCK
    seq_blocks = seq // tq
    n_steps = seq_blocks + 1
    rb = _scan_row_block(n_ex * seq_blocks, seq_blocks)
    rbr = lambda b, i: rb(b, n_steps - 1 - i)

    def body(gx_ref, x_ref, tab_ref, g_ref, da_ref, carry_ref):
        @pl.when(pl.program_id(2) == 0)
        def _():
            carry_ref[...] = jnp.zeros_like(carry_ref)
            da_ref[...] = jnp.zeros_like(da_ref)
        row = lax.broadcasted_iota(jnp.int32, (SUBLANES, half), 0)
        n_blk = tq // SUBLANES

        def blk(kk, st):
            t = [tab_ref[n * SUBLANES:(n + 1) * SUBLANES, :] for n in range(8)]
            cr, ci, dar, dai = st
            r0 = pl.multiple_of((n_blk - 1 - kk) * SUBLANES, SUBLANES)
            gr = gx_ref[pl.ds(r0, SUBLANES), 0:half]
            gi = gx_ref[pl.ds(r0, SUBLANES), half:cw]
            for s, d in enumerate((1, 2, 4)):
                gr, gi = _cmul_add(gr, gi, t[2 * s], t[2 * s + 1],
                                   pltpu.roll(gr, SUBLANES - d, 0), pltpu.roll(gi, SUBLANES - d, 0))
            gr, gi = _cmul_add(gr, gi, t[6], t[7], cr, ci)
            g_ref[pl.ds(r0, SUBLANES), 0:half] = gr.astype(BF16)
            g_ref[pl.ds(r0, SUBLANES), half:cw] = gi.astype(BF16)
            hr = jnp.where(row == SUBLANES - 1, cr, pltpu.roll(gr, SUBLANES - 1, 0))
            hi = jnp.where(row == SUBLANES - 1, ci, pltpu.roll(gi, SUBLANES - 1, 0))
            xr = x_ref[pl.ds(r0, SUBLANES), 0:half]
            xi = x_ref[pl.ds(r0, SUBLANES), half:cw]
            dar = dar + xr * hr + xi * hi
            dai = dai + xr * hi - xi * hr
            return (jnp.broadcast_to(gr[0:1, :], gr.shape), jnp.broadcast_to(gi[0:1, :], gi.shape), dar, dai)

        st = lax.fori_loop(0, n_blk, blk, (carry_ref[0], carry_ref[1], da_ref[0, :, 0:half], da_ref[0, :, half:cw]))
        carry_ref[0] = st[0]
        carry_ref[1] = st[1]
        da_ref[0, :, 0:half] = st[2]
        da_ref[0, :, half:cw] = st[3]

    return _call(name, body, (n_ex, nj, n_steps),
                 [pl.BlockSpec((tq, cw), lambda b, j, i: (rbr(b, i), j)), pl.BlockSpec((tq, cw), lambda b, j, i: (rbr(b, i), j)),
                  pl.BlockSpec((8 * SUBLANES, half), lambda b, j, i: (0, j))],
                 [pl.BlockSpec((tq, cw), lambda b, j, i: (rbr(b, i), j)), pl.BlockSpec((1, SUBLANES, cw), lambda b, j, i: (b, 0, j))],
                 [_sds((n_rows, width), BF16), _sds((n_ex, SUBLANES, width), F32)],
                 scratch=[pltpu.VMEM((2, SUBLANES, half), F32)])(gx, x, tabs)


def _ssm_z(gy, wout_ref, nj):
    return jnp.concatenate([_dot(gy, wout_ref[j]) for j in range(nj)], axis=1)


def _ssm_out(name, x, u, dskip, cb, wout4, h, n_rows):
    nj, xc, uc = cb.shape
    no, hw, oc = wout4.shape
    d = h.shape[1]
    tm = ROW_TILE

    def body(x_ref, u_ref, ds_ref, cb_ref, w_ref, h_ref, o_ref, y_ref):
        y = jnp.concatenate([_dot(x_ref[:, j * xc:(j + 1) * xc], cb_ref[j]) for j in range(nj)], axis=1)
        y = y + ds_ref[...] * u_ref[...]
        y_ref[...] = y
        z = _ssm_z(_gelu(y), w_ref, no)
        o_ref[...] = h_ref[...] + z[:, :d] * jax.nn.sigmoid(z[:, d:])

    return _call(name, body, (n_rows // tm,),
                 [pl.BlockSpec((tm, nj * xc), lambda i: (i, 0)), pl.BlockSpec((tm, hw), lambda i: (i, 0)),
                  pl.BlockSpec((1, hw), lambda i: (0, 0)), pl.BlockSpec((nj, xc, uc), lambda i: (0, 0, 0)),
                  pl.BlockSpec((no, hw, oc), lambda i: (0, 0, 0)), pl.BlockSpec((tm, d), lambda i: (i, 0))],
                 [pl.BlockSpec((tm, d), lambda i: (i, 0)), pl.BlockSpec((tm, hw), lambda i: (i, 0))],
                 [_sds((n_rows, d), F32), _sds((n_rows, hw), F32)])(x, u, dskip, cb, wout4, h)


def _ssm_out_bwd(name, dh, y, u, cb, wout4, n_rows):
    nj, xc, uc = cb.shape
    no, hw, oc = wout4.shape
    d = dh.shape[1]
    tm = ROW_TILE

    def body(dh_ref, y_ref, u_ref, cb_ref, w_ref, dy_ref, dz_ref, gx_ref, dd_ref):
        y = y_ref[...]
        z = _ssm_z(_gelu(y), w_ref, no)
        za = z[:, :d]
        sg = jax.nn.sigmoid(z[:, d:])
        dmix = dh_ref[...]
        dz = jnp.concatenate([dmix * sg, dmix * za * sg * (1.0 - sg)], axis=1).astype(BF16)
        dz_ref[...] = dz
        dgy = _dot_nt(dz[:, 0:oc], w_ref[0])
        for j in range(1, no):
            dgy = dgy + _dot_nt(dz[:, j * oc:(j + 1) * oc], w_ref[j])
        dy = dgy * _gelu_grad(y)
        dy_ref[...] = dy
        _acc_out(dd_ref, jnp.sum(dy * u_ref[...], axis=0, keepdims=True), pl.program_id(0) == 0)
        for j in range(nj):
            gx_ref[:, j * xc:(j + 1) * xc] = _dot_nt(dy[:, j * uc:(j + 1) * uc], cb_ref[j])

    return _call(name, body, (n_rows // tm,),
                 [pl.BlockSpec((tm, d), lambda i: (i, 0)), pl.BlockSpec((tm, hw), lambda i: (i, 0)),
                  pl.BlockSpec((tm, hw), lambda i: (i, 0)), pl.BlockSpec((nj, xc, uc), lambda i: (0, 0, 0)),
                  pl.BlockSpec((no, hw, oc), lambda i: (0, 0, 0))],
                 [pl.BlockSpec((tm, hw), lambda i: (i, 0)), pl.BlockSpec((tm, no * oc), lambda i: (i, 0)),
                  pl.BlockSpec((tm, nj * xc), lambda i: (i, 0)), pl.BlockSpec((1, hw), lambda i: (0, 0))],
                 [_sds((n_rows, hw), F32), _sds((n_rows, no * oc), BF16), _sds((n_rows, nj * xc), F32),
                  _sds((1, hw), F32)])(dh, y, u, cb, wout4)


def _ssm_in_bwd(name, gbu, dy, dskip, bb, w_in, h, g, dh, n_rows):
    nj, uc, xc = bb.shape
    d, hw = w_in.shape
    tm = ROW_TILE

    def body(gb_ref, dy_ref, ds_ref, bb_ref, w_ref, h_ref, g_ref, dh_ref, du_ref, o_ref, dg_ref):
        du = jnp.concatenate([_dot_nt(gb_ref[:, j * xc:(j + 1) * xc], bb_ref[j]) for j in range(nj)], axis=1)
        du = du + dy_ref[...] * ds_ref[...]
        du_ref[...] = du.astype(BF16)
        dhn, dg = _rms_bwd(h_ref[...], g_ref[...], _dot_nt(du, w_ref[...]))
        o_ref[...] = dh_ref[...] + dhn
        _acc_out(dg_ref, dg, pl.program_id(0) == 0)

    return _call(name, body, (n_rows // tm,),
                 [pl.BlockSpec((tm, nj * xc), lambda i: (i, 0)), pl.BlockSpec((tm, hw), lambda i: (i, 0)),
                  pl.BlockSpec((1, hw), lambda i: (0, 0)), pl.BlockSpec((nj, uc, xc), lambda i: (0, 0, 0)),
                  pl.BlockSpec((d, hw), lambda i: (0, 0)), pl.BlockSpec((tm, d), lambda i: (i, 0)),
                  pl.BlockSpec((1, d), lambda i: (0, 0)), pl.BlockSpec((tm, d), lambda i: (i, 0))],
                 [pl.BlockSpec((tm, hw), lambda i: (i, 0)), pl.BlockSpec((tm, d), lambda i: (i, 0)),
                  pl.BlockSpec((1, d), lambda i: (0, 0))],
                 [_sds((n_rows, hw), BF16), _sds((n_rows, d), F32), _sds((1, d), F32)])(gbu, dy, dskip, bb, w_in, h, g, dh)


def _discretize(lam_re, lam_im, log_step, b_re, b_im):
    step = jnp.exp(log_step)[:, None]
    mag = jnp.exp(lam_re * step)
    ar = mag * jnp.cos(lam_im * step)
    ai = mag * jnp.sin(lam_im * step)
    den = lam_re * lam_re + lam_im * lam_im
    nr, ni = ar - 1.0, ai
    cr = (nr * lam_re + ni * lam_im) / den
    ci = (ni * lam_re - nr * lam_im) / den
    bbar_r = cr[..., None] * b_re - ci[..., None] * b_im
    bbar_i = cr[..., None] * b_im + ci[..., None] * b_re
    return ar, ai, bbar_r, bbar_i


def _ssm_mats(lam_re, lam_im, log_step, b_re, b_im, c_re, c_im):
    n_g, n_p, n_c = b_re.shape
    gpc = n_g // 4
    ar, ai, bbar_r, bbar_i = _discretize(lam_re, lam_im, log_step, b_re, b_im)
    eye = jnp.eye(gpc, dtype=F32)

    def in_map(bbar):
        return jnp.einsum('jgpc,gh->jgchp', bbar.reshape(4, gpc, n_p, n_c), eye).reshape(4, gpc * n_c, gpc * n_p)

    def out_map(c):
        return jnp.einsum('jgcp,gh->jgphc', c.reshape(4, gpc, n_c, n_p), eye).reshape(4, gpc * n_p, gpc * n_c)

    bb = jnp.concatenate([in_map(bbar_r), in_map(bbar_i)], axis=2)
    cb = jnp.concatenate([out_map(c_re), -out_map(c_im)], axis=1)
    return bb, cb, ar.reshape(-1), ai.reshape(-1)


def _chunked(v, half):
    return v.reshape(v.shape[:-1] + (4, half))


def _scan_tables(ar, ai, reverse):
    if reverse:
        ai = -ai
    pr, pi = [ar], [ai]
    for _ in range(SUBLANES - 1):
        pr, pi = pr + [pr[-1] * ar - pi[-1] * ai], pi + [pr[-1] * ai + pi[-1] * ar]
    row = jnp.arange(SUBLANES)[:, None]
    tabs = []
    for d in (1, 2, 4):
        keep = (row <= SUBLANES - 1 - d) if reverse else (row >= d)
        tabs += [jnp.where(keep, pr[d - 1][None, :], 0.0), jnp.where(keep, pi[d - 1][None, :], 0.0)]
    order = list(range(SUBLANES))[::-1] if reverse else list(range(SUBLANES))
    tabs += [jnp.stack([pr[k] for k in order]), jnp.stack([pi[k] for k in order])]
    return jnp.concatenate(tabs, axis=0)


def _kv_proj(name, h, g, w_kv, k_gain, cos, sin_s, n_rows, n_kv, hd):
    d, kvw = w_kv.shape
    kw = n_kv * hd
    tm = ROW_TILE

    def body(h_ref, g_ref, w_ref, kg_ref, c_ref, s_ref, raw_ref, k_ref, v_ref):
        raw = _dot(_rms(h_ref[...], g_ref[...]), w_ref[...])
        raw_ref[...] = raw
        ks = _head_prep(raw[:, :kw], kg_ref[...], c_ref[...], s_ref[...], n_kv, hd)
        k_ref[...] = jnp.concatenate(ks, axis=1).astype(BF16)
        v_ref[...] = raw[:, kw:].astype(BF16)

    return _call(name, body, (n_rows // tm,),
                 [pl.BlockSpec((tm, d), lambda i: (i, 0)), pl.BlockSpec((1, d), lambda i: (0, 0)),
                  pl.BlockSpec((d, kvw), lambda i: (0, 0)), pl.BlockSpec((1, hd), lambda i: (0, 0)),
                  pl.BlockSpec((tm, hd), lambda i: (i, 0)), pl.BlockSpec((tm, hd), lambda i: (i, 0))],
                 [pl.BlockSpec((tm, kvw), lambda i: (i, 0)), pl.BlockSpec((tm, kw), lambda i: (i, 0)),
                  pl.BlockSpec((tm, kw), lambda i: (i, 0))],
                 [_sds((n_rows, kvw), F32), _sds((n_rows, kw), BF16), _sds((n_rows, kw), BF16)])(
                     h, g, w_kv, k_gain, cos, sin_s)


def _q_proj(name, h, g, w_q, q_gain, cos, sin_s, n_rows, n_q, hd):
    d, qw = w_q.shape
    tm = ROW_TILE

    def body(h_ref, g_ref, w_ref, qg_ref, c_ref, s_ref, raw_ref, q_ref):
        raw = _dot(_rms(h_ref[...], g_ref[...]), w_ref[...])
        raw_ref[...] = raw
        qs = _head_prep(raw, qg_ref[...], c_ref[...], s_ref[...], n_q, hd)
        q_ref[...] = jnp.concatenate(qs, axis=1).astype(BF16)

    return _call(name, body, (n_rows // tm,),
                 [pl.BlockSpec((tm, d), lambda i: (i, 0)), pl.BlockSpec((1, d), lambda i: (0, 0)),
                  pl.BlockSpec((d, qw), lambda i: (0, 0)), pl.BlockSpec((1, hd), lambda i: (0, 0)),
                  pl.BlockSpec((tm, hd), lambda i: (i, 0)), pl.BlockSpec((tm, hd), lambda i: (i, 0))],
                 [pl.BlockSpec((tm, qw), lambda i: (i, 0)), pl.BlockSpec((tm, qw), lambda i: (i, 0))],
                 [_sds((n_rows, qw), F32), _sds((n_rows, qw), BF16)])(h, g, w_q, q_gain, cos, sin_s)


def _attn_specs(seq, n_ex, n_meta, kw):
    nb = seq // WINDOW
    meta_blk = lambda b: (n_ex * seq + META_BLOCK * b + META_BLOCK - n_meta) // n_meta
    return [pl.BlockSpec((WINDOW, kw), lambda b, n: (b * nb + jnp.maximum(n - 1, 0), 0)),
            pl.BlockSpec((WINDOW, kw), lambda b, n: (b * nb + n, 0)),
            pl.BlockSpec((n_meta, kw), lambda b, n: (meta_blk(b), 0))]


def _attn_mask(n, qpk, n_keys):
    rows = qpk * WINDOW
    qi = lax.broadcasted_iota(jnp.int32, (rows, n_keys), 0) & (WINDOW - 1)
    kj = lax.broadcasted_iota(jnp.int32, (rows, n_keys), 1)
    rel = qi + WINDOW - kj
    band = (rel >= 0) & (rel < WINDOW) & ((n > 0) | (kj >= WINDOW))
    return band | (kj >= 2 * WINDOW)


def _stack_heads(ref, h, qpk, hd, dtype=None):
    parts = [ref[:, (h * qpk + gq) * hd:(h * qpk + gq + 1) * hd] for gq in range(qpk)]
    out = jnp.concatenate(parts, axis=0)
    return out if dtype is None else out.astype(dtype)


def _col(tile, c):
    lane = lax.broadcasted_iota(jnp.int32, tile.shape, 1)
    return jnp.sum(jnp.where(lane == c, tile, 0.0), axis=-1, keepdims=True)


def _put_col(col, c, n):
    lane = lax.broadcasted_iota(jnp.int32, (col.shape[0], n), 1)
    return jnp.where(lane == c, col, 0.0)


def _stack_cols(tile, h, qpk):
    return jnp.concatenate([_col(tile, h * qpk + gq) for gq in range(qpk)], axis=0)


def _sink_col(sinks, h, qpk):
    return jnp.concatenate([jnp.broadcast_to(_col(sinks, h * qpk + gq), (WINDOW, 1)) for gq in range(qpk)], axis=0)


def _attn_fwd(name, q, k, v, sinks, n_ex, seq, n_meta, n_kv, qpk, hd):
    nb = seq // WINDOW
    n_q = n_kv * qpk
    kw = n_kv * hd
    qw = n_q * hd
    n_keys = 2 * WINDOW + n_meta
    scale = hd ** -0.5

    def body(q_ref, kp_ref, kc_ref, km_ref, vp_ref, vc_ref, vm_ref, sk_ref, o_ref, lse_ref):
        valid = _attn_mask(pl.program_id(1), qpk, n_keys)
        sinks_v = sk_ref[...]
        o_parts = []
        lse_all = jnp.zeros((WINDOW, n_q), F32)
        for h in range(n_kv):
            hs = slice(h * hd, (h + 1) * hd)
            kb = jnp.concatenate([kp_ref[:, hs], kc_ref[:, hs], km_ref[:, hs]], axis=0)
            vb = jnp.concatenate([vp_ref[:, hs], vc_ref[:, hs], vm_ref[:, hs]], axis=0)
            s = jnp.where(valid, _dot_nt(_stack_heads(q_ref, h, qpk, hd), kb) * scale, NEG_INF)
            skc = _sink_col(sinks_v, h, qpk)
            m = jnp.maximum(jnp.max(s, axis=-1, keepdims=True), skc)
            p = jnp.exp(s - m)
            den = jnp.sum(p, axis=-1, keepdims=True) + jnp.exp(skc - m)
            o = _dot(p, vb) / den
            lse = m + jnp.log(den)
            for gq in range(qpk):
                o_parts.append(o[gq * WINDOW:(gq + 1) * WINDOW])
                lse_all = lse_all + _put_col(lse[gq * WINDOW:(gq + 1) * WINDOW], h * qpk + gq, n_q)
        o_ref[...] = jnp.concatenate(o_parts, axis=1)
        lse_ref[...] = lse_all

    qspec = pl.BlockSpec((WINDOW, qw), lambda b, n: (b * nb + n, 0))
    return _call(name, body, (n_ex, nb),
                 [qspec] + _attn_specs(seq, n_ex, n_meta, kw) + _attn_specs(seq, n_ex, n_meta, kw)
                 + [pl.BlockSpec((1, n_q), lambda b, n: (0, 0))],
                 [qspec, pl.BlockSpec((WINDOW, n_q), lambda b, n: (b * nb + n, 0))],
                 [_sds((n_ex * seq, qw), F32), _sds((n_ex * seq, n_q), F32)])(q, k, k, k, v, v, v, sinks)


def _attn_bwd(name, q, k, v, sinks, o, lse, do, n_ex, seq, n_meta, n_kv, qpk, hd):
    nb = seq // WINDOW
    n_q = n_kv * qpk
    kw = n_kv * hd
    qw = n_q * hd
    n_keys = 2 * WINDOW + n_meta
    scale = hd ** -0.5

    def body(q_ref, kp_ref, kc_ref, km_ref, vp_ref, vc_ref, vm_ref, sk_ref, o_ref, lse_ref, do_ref,
             dq_ref, dk_ref, dv_ref, dkm_ref, dvm_ref, dsk_ref):
        n = pl.program_id(1)

        @pl.when(n == 0)
        def _():
            dk_ref[...] = jnp.zeros_like(dk_ref)
            dv_ref[...] = jnp.zeros_like(dv_ref)
            dkm_ref[...] = jnp.zeros_like(dkm_ref)
            dvm_ref[...] = jnp.zeros_like(dvm_ref)

        @pl.when((n == 0) & (pl.program_id(0) == 0))
        def _():
            dsk_ref[...] = jnp.zeros_like(dsk_ref)

        valid = _attn_mask(n, qpk, n_keys)
        sinks_v = sk_ref[...]
        lse_v = lse_ref[...]
        dq_parts, dk_parts, dv_parts = [], [], []
        dsk = jnp.zeros((1, n_q), F32)
        for h in range(n_kv):
            hs = slice(h * hd, (h + 1) * hd)
            kb = jnp.concatenate([kp_ref[:, hs], kc_ref[:, hs], km_ref[:, hs]], axis=0)
            vb = jnp.concatenate([vp_ref[:, hs], vc_ref[:, hs], vm_ref[:, hs]], axis=0)
            qs = _stack_heads(q_ref, h, qpk, hd)
            dos = _stack_heads(do_ref, h, qpk, hd)
            delta = jnp.sum(dos * _stack_heads(o_ref, h, qpk, hd), axis=-1, keepdims=True)
            lse_c = _stack_cols(lse_v, h, qpk)
            s = jnp.where(valid, _dot_nt(qs, kb) * scale, NEG_INF)
            p = jnp.exp(s - lse_c)
            ds = p * (_dot_nt(dos, vb) - delta)
            dqs = _dot(ds, kb) * scale
            dk_parts.append(_dot_tn(ds, qs) * scale)
            dv_parts.append(_dot_tn(p, dos))
            dsink = -jnp.exp(_sink_col(sinks_v, h, qpk) - lse_c) * delta
            for gq in range(qpk):
                dq_parts.append(dqs[gq * WINDOW:(gq + 1) * WINDOW])
                dsk = dsk + _put_col(jnp.sum(dsink[gq * WINDOW:(gq + 1) * WINDOW], axis=0, keepdims=True), h * qpk + gq, n_q)
        dq_ref[...] = jnp.concatenate(dq_parts, axis=1)
        dsk_ref[...] += dsk
        dkb = jnp.concatenate(dk_parts, axis=1)
        dvb = jnp.concatenate(dv_parts, axis=1)
        prev = pl.ds(pl.multiple_of(jnp.maximum(n - 1, 0) * WINDOW, WINDOW), WINDOW)
        cur = pl.ds(pl.multiple_of(n * WINDOW, WINDOW), WINDOW)
        dk_ref[prev, :] += dkb[0:WINDOW]
        dv_ref[prev, :] += dvb[0:WINDOW]
        dk_ref[cur, :] += dkb[WINDOW:2 * WINDOW]
        dv_ref[cur, :] += dvb[WINDOW:2 * WINDOW]
        dkm_ref[...] += dkb[2 * WINDOW:]
        dvm_ref[...] += dvb[2 * WINDOW:]

    qspec = pl.BlockSpec((WINDOW, qw), lambda b, n: (b * nb + n, 0))
    exspec = pl.BlockSpec((seq, kw), lambda b, n: (b, 0))
    mspec = pl.BlockSpec((n_meta, kw), lambda b, n: (b, 0))
    return _call(name, body, (n_ex, nb),
                 [qspec] + _attn_specs(seq, n_ex, n_meta, kw) + _attn_specs(seq, n_ex, n_meta, kw)
                 + [pl.BlockSpec((1, n_q), lambda b, n: (0, 0)), qspec,
                    pl.BlockSpec((WINDOW, n_q), lambda b, n: (b * nb + n, 0)), qspec],
                 [qspec, exspec, exspec, mspec, mspec, pl.BlockSpec((1, n_q), lambda b, n: (0, 0))],
                 [_sds((n_ex * seq, qw), F32), _sds((n_ex * seq, kw), F32), _sds((n_ex * seq, kw), F32),
                  _sds((n_ex * n_meta, kw), F32), _sds((n_ex * n_meta, kw), F32), _sds((1, n_q), F32)])(
                      q, k, k, k, v, v, v, sinks, o, lse, do)


def _attn_out(name, o, h, w_o, n_rows):
    qw, d = w_o.shape
    tm = ROW_TILE

    def body(o_ref, h_ref, w_ref, out_ref):
        out_ref[...] = h_ref[...] + _dot(o_ref[...], w_ref[...])

    return _call(name, body, (n_rows // tm,),
                 [pl.BlockSpec((tm, qw), lambda i: (i, 0)), pl.BlockSpec((tm, d), lambda i: (i, 0)),
                  pl.BlockSpec((qw, d), lambda i: (0, 0))],
                 pl.BlockSpec((tm, d), lambda i: (i, 0)), _sds((n_rows, d), F32))(o, h, w_o)


def _attn_out_bwd(name, dh, w_o, n_rows):
    qw, d = w_o.shape
    tm = ROW_TILE

    def body(dh_ref, w_ref, do_ref):
        do_ref[...] = _dot_nt(dh_ref[...], w_ref[...])

    return _call(name, body, (n_rows // tm,),
                 [pl.BlockSpec((tm, d), lambda i: (i, 0)), pl.BlockSpec((qw, d), lambda i: (0, 0))],
                 pl.BlockSpec((tm, qw), lambda i: (i, 0)), _sds((n_rows, qw), F32))(dh, w_o)


def _q_bwd(name, dq, qraw, q_gain, cos, sin_s, w_q, h, g, dh, n_rows, n_q, hd):
    d, qw = w_q.shape
    tm = ROW_TILE

    def body(dq_ref, raw_ref, qg_ref, c_ref, s_ref, w_ref, h_ref, g_ref, dh_ref, draw_ref, o_ref, dqg_ref, dg_ref):
        dxs, dgain = _head_prep_bwd(raw_ref[...], qg_ref[...], c_ref[...], s_ref[...], dq_ref[...], n_q, hd)
        draw = jnp.concatenate(dxs, axis=1).astype(BF16)
        draw_ref[...] = draw
        dhn, dg = _rms_bwd(h_ref[...], g_ref[...], _dot_nt(draw, w_ref[...]))
        o_ref[...] = dh_ref[...] + dhn
        first = pl.program_id(0) == 0
        _acc_out(dqg_ref, dgain, first)
        _acc_out(dg_ref, dg, first)

    row = lambda w: pl.BlockSpec((tm, w), lambda i: (i, 0))
    one = lambda w: pl.BlockSpec((1, w), lambda i: (0, 0))
    return _call(name, body, (n_rows // tm,),
                 [row(qw), row(qw), one(hd), row(hd), row(hd), pl.BlockSpec((d, qw), lambda i: (0, 0)), row(d), one(d), row(d)],
                 [row(qw), row(d), one(hd), one(d)],
                 [_sds((n_rows, qw), BF16), _sds((n_rows, d), F32), _sds((1, hd), F32), _sds((1, d), F32)])(
                     dq, qraw, q_gain, cos, sin_s, w_q, h, g, dh)


def _kv_bwd(name, dk, dv, kvraw, k_gain, cos, sin_s, w_kv, h, g, dh_main, n_rows, n_main, n_kv, hd):
    d, kvw = w_kv.shape
    kw = n_kv * hd
    tm = ROW_TILE
    n_main_tiles = n_main // tm

    def body(dk_ref, dv_ref, raw_ref, kg_ref, c_ref, s_ref, w_ref, h_ref, g_ref, dh_ref, draw_ref, o_ref, dkg_ref, dg_ref):
        i = pl.program_id(0)
        dxs, dgain = _head_prep_bwd(raw_ref[:, :kw], kg_ref[...], c_ref[...], s_ref[...], dk_ref[...], n_kv, hd)
        draw = jnp.concatenate(dxs + [dv_ref[...]], axis=1).astype(BF16)
        draw_ref[...] = draw
        dhn, dg = _rms_bwd(h_ref[...], g_ref[...], _dot_nt(draw, w_ref[...]))
        o_ref[...] = jnp.where(i < n_main_tiles, dh_ref[...], 0.0) + dhn
        _acc_out(dkg_ref, dgain, i == 0)
        _acc_out(dg_ref, dg, i == 0)

    row = lambda w: pl.BlockSpec((tm, w), lambda i: (i, 0))
    one = lambda w: pl.BlockSpec((1, w), lambda i: (0, 0))
    return _call(name, body, (n_rows // tm,),
                 [row(kw), row(kw), row(kvw), one(hd), row(hd), row(hd), pl.BlockSpec((d, kvw), lambda i: (0, 0)), row(d),
                  one(d), pl.BlockSpec((tm, d), lambda i: (jnp.minimum(i, n_main_tiles - 1), 0))],
                 [row(kvw), row(d), one(hd), one(d)],
                 [_sds((n_rows, kvw), BF16), _sds((n_rows, d), F32), _sds((1, hd), F32), _sds((1, d), F32)])(
                     dk, dv, kvraw, k_gain, cos, sin_s, w_kv, h, g, dh_main)


def _tn_rms(name, h, g, b, n_rows):
    d = h.shape[1]
    nb = b.shape[1]
    tk = ROW_TILE
    return _tn(name, (h, g, b),
               [pl.BlockSpec((tk, d), lambda j, k: (k, 0)), pl.BlockSpec((1, d), lambda j, k: (0, 0)),
                pl.BlockSpec((tk, nb), lambda j, k: (k, 0))],
               lambda j, h_ref, g_ref, b_ref: (_rms(h_ref[...], g_ref[...]), b_ref[...]), 1, d, nb, n_rows)


def _tn_plain(name, a, b, nj, a_cols, b_cols, n_rows, a_fn=None):
    tk = ROW_TILE
    fa = (lambda v: v) if a_fn is None else a_fn
    a_map = (lambda j, k: (k, j)) if a.shape[1] != a_cols else (lambda j, k: (k, 0))
    b_map = (lambda j, k: (k, j)) if b.shape[1] != b_cols else (lambda j, k: (k, 0))
    return _tn(name, (a, b), [pl.BlockSpec((tk, a_cols), a_map), pl.BlockSpec((tk, b_cols), b_map)],
               lambda j, a_ref, b_ref: (fa(a_ref[...]), b_ref[...]), nj, a_cols, b_cols, n_rows)


def _loss_head(name, y, target, n_rows):
    d = y.shape[1]
    tm = ROW_TILE

    def body(y_ref, t_ref, dy_ref, l_ref):
        e = y_ref[...] - t_ref[...]
        dy_ref[...] = e * (1.0 / d)
        e2 = jnp.sum((e * e).reshape(tm // SUBLANES, SUBLANES, d), axis=0)
        part = e2[:, 0:128]
        for k in range(1, d // 128):
            part = part + e2[:, k * 128:(k + 1) * 128]
        _acc_out(l_ref, part * (0.5 / d), pl.program_id(0) == 0)

    return _call(name, body, (n_rows // tm,),
                 [pl.BlockSpec((tm, d), lambda i: (i, 0)), pl.BlockSpec((tm, d), lambda i: (i, 0))],
                 [pl.BlockSpec((tm, d), lambda i: (i, 0)), pl.BlockSpec((SUBLANES, 128), lambda i: (0, 0))],
                 [_sds((n_rows, d), F32), _sds((SUBLANES, 128), F32)])(y, target)


def _adamw(name, w, g, m, v):
    rows, cols = w.shape
    tr = 128 if rows % 128 == 0 else rows
    c1 = 1.0 - ADAM_B1 ** ADAM_STEP
    c2 = 1.0 - ADAM_B2 ** ADAM_STEP

    def body(w_ref, g_ref, m_ref, v_ref, d_ref, nm_ref, nv_ref):
        gg = g_ref[...]
        nm = ADAM_B1 * m_ref[...] + (1.0 - ADAM_B1) * gg
        nv = ADAM_B2 * v_ref[...] + (1.0 - ADAM_B2) * (gg * gg)
        nm_ref[...] = nm
        nv_ref[...] = nv
        d_ref[...] = -ADAM_LR * ((nm / c1) / (jnp.sqrt(nv / c2) + ADAM_EPS) + ADAM_WD * w_ref[...])

    spec = pl.BlockSpec((tr, cols), lambda i: (i, 0))
    return _call(name, body, (rows // tr,), [spec] * 4, [spec] * 3, [_sds((rows, cols), F32)] * 3)(w, g, m, v)


def _position():
    return lax.axis_index("x"), lax.axis_index("y"), lax.axis_index("c")


def _other_chips(x, y):
    return [(1 - x, y), (x, 1 - y), (1 - x, 1 - y)]


def _comm_call(name, body, n_in, out_shape, scratch):
    return pl.pallas_call(
        body, name=name, in_specs=[_HBM] * n_in, out_specs=[_HBM] * len(out_shape), out_shape=out_shape,
        scratch_shapes=list(scratch),
        compiler_params=pltpu.CompilerParams(has_side_effects=True, vmem_limit_bytes=V7X_VMEM_LIMIT))


def _all_gather_chips(name, shards, split):
    n = len(shards)

    def body(*refs):
        ins, outs = refs[:n], refs[n:2 * n]
        send_sems, recv_sems, local_sems = refs[2 * n:]
        x, y, c = _position()
        me = 2 * x + y
        chips = _other_chips(x, y)
        sibling = (x, y, 1 - c)
        sends, forwards = [], []
        for t in range(n):
            pltpu.make_async_copy(ins[t], outs[t].at[me], local_sems.at[t]).start()
        for t in range(n):
            r = ins[t].shape[0]
            rows = pl.ds(c * (r // 2), r // 2) if split[t] else pl.ds(0, r)
            for k, (cx, cy) in enumerate(chips):
                cp = pltpu.make_async_remote_copy(
                    src_ref=ins[t].at[rows], dst_ref=outs[t].at[me, rows],
                    send_sem=send_sems.at[t, k], recv_sem=recv_sems.at[t, k], device_id=(cx, cy, c), device_id_type=MESH)
                cp.start()
                sends.append(cp)
        for t in range(n):
            r = ins[t].shape[0]
            rows = pl.ds(c * (r // 2), r // 2) if split[t] else pl.ds(0, r)
            for k, (cx, cy) in enumerate(chips):
                landed = outs[t].at[2 * cx + cy, rows]
                pltpu.make_async_remote_copy(
                    src_ref=landed, dst_ref=landed, send_sem=send_sems.at[t, k], recv_sem=recv_sems.at[t, k],
                    device_id=(cx, cy, c), device_id_type=MESH).wait_recv()
                if split[t]:
                    fw = pltpu.make_async_remote_copy(
                        src_ref=landed, dst_ref=landed, send_sem=send_sems.at[t, 3 + k], recv_sem=recv_sems.at[t, 3 + k],
                        device_id=sibling, device_id_type=MESH)
                    fw.start()
                    forwards.append(fw)
        for t in range(n):
            if split[t]:
                r = ins[t].shape[0]
                other = pl.ds((1 - c) * (r // 2), r // 2)
                for k, (cx, cy) in enumerate(chips):
                    landed = outs[t].at[2 * cx + cy, other]
                    pltpu.make_async_remote_copy(
                        src_ref=landed, dst_ref=landed, send_sem=send_sems.at[t, 3 + k], recv_sem=recv_sems.at[t, 3 + k],
                        device_id=sibling, device_id_type=MESH).wait_recv()
        for cp in sends + forwards:
            cp.wait_send()
        for t in range(n):
            pltpu.make_async_copy(ins[t], outs[t].at[me], local_sems.at[t]).wait()

    out_shape = [_sds((N_CHIPS,) + s.shape, s.dtype) for s in shards]
    return _comm_call(name, body, n, out_shape,
                      [pltpu.SemaphoreType.DMA((n, 6)), pltpu.SemaphoreType.DMA((n, 6)), pltpu.SemaphoreType.DMA((n,))])(*shards)


def _swap_halves_with_sibling(name, blob):
    def body(b_ref, mine_ref, theirs_ref, send_sem, recv_sem, local_sem):
        x, y, c = _position()
        loc = pltpu.make_async_copy(b_ref.at[c], mine_ref, local_sem)
        loc.start()
        rc = pltpu.make_async_remote_copy(src_ref=b_ref.at[1 - c], dst_ref=theirs_ref, send_sem=send_sem, recv_sem=recv_sem,
                                          device_id=(x, y, 1 - c), device_id_type=MESH)
        rc.start()
        rc.wait()
        loc.wait()

    half = _sds(blob.shape[1:], blob.dtype)
    return _comm_call(name, body, 1, [half, half],
                      [pltpu.SemaphoreType.DMA(()), pltpu.SemaphoreType.DMA(()), pltpu.SemaphoreType.DMA(())])(blob)


def _scatter_to_chips(name, parts):
    def body(p_ref, o_ref, send_sems, recv_sems, local_sem):
        x, y, c = _position()
        me = 2 * x + y
        loc = pltpu.make_async_copy(p_ref.at[me], o_ref.at[me], local_sem)
        loc.start()
        sends = []
        for k, (cx, cy) in enumerate(_other_chips(x, y)):
            cp = pltpu.make_async_remote_copy(src_ref=p_ref.at[2 * cx + cy], dst_ref=o_ref.at[me], send_sem=send_sems.at[k],
                                              recv_sem=recv_sems.at[k], device_id=(cx, cy, c), device_id_type=MESH)
            cp.start()
            sends.append(cp)
        for k, (cx, cy) in enumerate(_other_chips(x, y)):
            landed = o_ref.at[2 * cx + cy]
            pltpu.make_async_remote_copy(src_ref=landed, dst_ref=landed, send_sem=send_sems.at[k], recv_sem=recv_sems.at[k],
                                         device_id=(cx, cy, c), device_id_type=MESH).wait_recv()
        for cp in sends:
            cp.wait_send()
        loc.wait()

    return _comm_call(name, body, 1, [_sds(parts.shape, parts.dtype)],
                      [pltpu.SemaphoreType.DMA((3,)), pltpu.SemaphoreType.DMA((3,)), pltpu.SemaphoreType.DMA(())])(parts)[0]


def _share_half_with_sibling(name, mine):
    def body(m_ref, o_ref, send_sem, recv_sem, local_sem):
        x, y, c = _position()
        loc = pltpu.make_async_copy(m_ref, o_ref.at[c], local_sem)
        loc.start()
        rc = pltpu.make_async_remote_copy(src_ref=m_ref, dst_ref=o_ref.at[c], send_sem=send_sem, recv_sem=recv_sem,
                                          device_id=(x, y, 1 - c), device_id_type=MESH)
        rc.start()
        rc.wait_send()
        theirs = o_ref.at[1 - c]
        pltpu.make_async_remote_copy(src_ref=theirs, dst_ref=theirs, send_sem=send_sem, recv_sem=recv_sem,
                                     device_id=(x, y, 1 - c), device_id_type=MESH).wait_recv()
        loc.wait()

    return _comm_call(name, body, 1, [_sds((2,) + mine.shape, mine.dtype)],
                      [pltpu.SemaphoreType.DMA(()), pltpu.SemaphoreType.DMA(()), pltpu.SemaphoreType.DMA(())])(mine)[0]


def _row_tile(rows, cap=256):
    best = rows
    for t in range(16, min(rows, cap) + 1, 16):
        if rows % t == 0:
            best = t
    return best


def _add_pairs(name, a, b, out_dtype):
    n, rows, cols = a.shape
    tr = _row_tile(rows)

    def body(a_ref, b_ref, o_ref):
        o_ref[...] = (a_ref[...].astype(F32) + b_ref[...].astype(F32)).astype(out_dtype)

    spec = pl.BlockSpec((1, tr, cols), lambda k, i: (k, i, 0))
    return _call(name, body, (n, rows // tr), [spec, spec], spec, _sds(a.shape, out_dtype))(a, b)


def _sum_slots(name, parts):
    n, rows, cols = parts.shape
    tr = _row_tile(rows)

    def body(p_ref, o_ref):
        acc = p_ref[0].astype(F32)
        for k in range(1, n):
            acc = acc + p_ref[k].astype(F32)
        o_ref[...] = acc

    return _call(name, body, (rows // tr,), [pl.BlockSpec((n, tr, cols), lambda i: (0, i, 0))],
                 pl.BlockSpec((tr, cols), lambda i: (i, 0)), _sds((rows, cols), F32))(parts)


def _all_reduce_small(name, vec):
    rows, cols = vec.shape

    def body(v_ref, o_ref, buf_ref, send_sems, recv_sems):
        x, y, c = _position()
        me = 4 * x + 2 * y + c
        buf_ref[me] = v_ref[...]
        sends = []
        for dlt in range(1, N_DEV):
            tx = 1 - x if dlt & 4 else x
            ty = 1 - y if dlt & 2 else y
            tc = 1 - c if dlt & 1 else c
            cp = pltpu.make_async_remote_copy(src_ref=v_ref, dst_ref=buf_ref.at[me], send_sem=send_sems.at[dlt - 1],
                                              recv_sem=recv_sems.at[dlt - 1], device_id=(tx, ty, tc), device_id_type=MESH)
            cp.start()
            sends.append(cp)
        for dlt in range(1, N_DEV):
            tx = 1 - x if dlt & 4 else x
            ty = 1 - y if dlt & 2 else y
            tc = 1 - c if dlt & 1 else c
            landed = buf_ref.at[4 * tx + 2 * ty + tc]
            pltpu.make_async_remote_copy(src_ref=landed, dst_ref=landed, send_sem=send_sems.at[dlt - 1],
                                         recv_sem=recv_sems.at[dlt - 1], device_id=(tx, ty, tc), device_id_type=MESH).wait_recv()
        for cp in sends:
            cp.wait_send()
        acc = buf_ref[0]
        for k in range(1, N_DEV):
            acc = acc + buf_ref[k]
        o_ref[...] = acc

    return pl.pallas_call(
        body, name=name, in_specs=[_VMEM], out_specs=_VMEM, out_shape=_sds((rows, cols), F32),
        scratch_shapes=[pltpu.VMEM((N_DEV, rows, cols), F32), pltpu.SemaphoreType.DMA((N_DEV - 1,)),
                        pltpu.SemaphoreType.DMA((N_DEV - 1,))],
        compiler_params=pltpu.CompilerParams(has_side_effects=True, vmem_limit_bytes=V7X_VMEM_LIMIT))(vec)


def _pack(arrays):
    flat = []
    for a in arrays:
        v = a.reshape(-1).astype(F32)
        flat.append(jnp.pad(v, (0, (-v.shape[0]) % 128)))
    v = jnp.concatenate(flat)
    v = jnp.pad(v, (0, (-v.shape[0]) % (128 * SUBLANES)))
    return v.reshape(-1, 128)


def _unpack(packed, shapes):
    v = packed.reshape(-1)
    out, off = [], 0
    for s in shapes:
        n = math.prod(s)
        out.append(v[off:off + n].reshape(s))
        off += n + (-n) % 128
    return out


_BIG = ("ffn1_w_gate_up", "ffn1_w_down", "ffn2_w_gate_up", "ffn2_w_down", "ssm_w_in", "ssm_w_out", "w_kv", "attn_w_q", "attn_w_o")
_SMALL = ("meta_tokens", "ffn1_norm", "mix_norm", "ffn2_norm", "ssm_lambda_re", "ssm_lambda_im", "ssm_b_re", "ssm_b_im",
          "ssm_c_re", "ssm_c_im", "ssm_log_step", "ssm_d", "kv_norm", "k_norm", "q_norm", "attn_sinks")
_ORDER = ("meta_tokens", "ffn1_norm", "ffn1_w_gate_up", "ffn1_w_down", "mix_norm", "ffn2_norm", "ffn2_w_gate_up", "ffn2_w_down",
          "ssm_w_in", "ssm_lambda_re", "ssm_lambda_im", "ssm_b_re", "ssm_b_im", "ssm_c_re", "ssm_c_im", "ssm_log_step", "ssm_d",
          "ssm_w_out", "kv_norm", "w_kv", "k_norm", "attn_w_q", "q_norm", "attn_sinks", "attn_w_o")


def _step(x, target, w, m, v):
    n_ex, seq, d = x.shape
    n_meta = w["meta_tokens"].shape[0]
    n_main = n_ex * seq
    n_all = n_main + n_ex * META_BLOCK
    n_g, n_p, n_c = w["ssm_b_re"].shape[1:]
    hd = w["k_norm"].shape[0]
    n_kv = w["w_kv"].shape[1] // (2 * hd)
    n_q = w["attn_w_q"].shape[2] // hd
    qpk = n_q // n_kv
    px, py, pc = _position()
    chip = 2 * px + py

    items, split = [], []
    for name in ("ffn1_w_gate_up", "ffn1_w_down", "ffn2_w_gate_up", "ffn2_w_down"):
        for layer in range(2):
            items.append(w[name][layer].astype(BF16))
            split.append(True)
    for name in ("ssm_w_in", "ssm_w_out", "attn_w_q", "attn_w_o"):
        items.append(w[name][0].astype(BF16))
        split.append(True)
    items.append(w["w_kv"].astype(BF16))
    split.append(True)
    items += [w["meta_tokens"], w["ssm_d"]]
    split += [False, False]
    gathered = _all_gather_chips("gather_weights", items, split)
    wgu = {("ffn1", 0): gathered[0], ("ffn1", 1): gathered[1], ("ffn2", 0): gathered[4], ("ffn2", 1): gathered[5]}
    wd = {("ffn1", 0): gathered[2], ("ffn1", 1): gathered[3], ("ffn2", 0): gathered[6], ("ffn2", 1): gathered[7]}
    wd = {key: a.reshape(-1, d) for key, a in wd.items()}
    w_in = gathered[8].reshape(d, -1)
    wout4 = gathered[9]
    w_q = gathered[10].reshape(d, -1)
    w_o = gathered[11].reshape(-1, d)
    w_kv = gathered[12].reshape(d, -1)
    meta_full = jnp.transpose(gathered[13], (1, 0, 2)).reshape(n_meta, d)
    dskip = gathered[14].reshape(1, -1)

    row1 = lambda a: a.reshape(1, -1)
    ssm_args = tuple(w[k][0] for k in ("ssm_lambda_re", "ssm_lambda_im", "ssm_log_step", "ssm_b_re", "ssm_b_im", "ssm_c_re", "ssm_c_im"))
    (bb, cb, a_re, a_im), ssm_vjp = jax.vjp(_ssm_mats, *ssm_args)
    bb16, cb16 = bb.astype(BF16), cb.astype(BF16)
    a_re_s, a_im_s = lax.stop_gradient(a_re), lax.stop_gradient(a_im)
    half = n_g * n_p // 4
    tabs_f = _scan_tables(a_re_s, a_im_s, False)
    tabs_b = _scan_tables(a_re_s, a_im_s, True)

    freqs = ROPE_THETA ** (-jnp.arange(0, hd // 2, dtype=F32) * 2.0 / hd)
    pos_main = jnp.tile(n_meta + jnp.arange(seq), n_ex)
    pos_meta = jnp.tile(jnp.maximum(jnp.arange(META_BLOCK) - (META_BLOCK - n_meta), 0), n_ex)
    ang = jnp.concatenate([pos_main, pos_meta]).astype(F32)[:, None] * freqs[None, :]
    cos = jnp.concatenate([jnp.cos(ang), jnp.cos(ang)], axis=1)
    sin_s = jnp.concatenate([-jnp.sin(ang), jnp.sin(ang)], axis=1)

    meta_block = jnp.concatenate([jnp.zeros((META_BLOCK - n_meta, d), F32), meta_full], axis=0)
    h0 = jnp.concatenate([x.reshape(n_main, d)] + [meta_block] * n_ex, axis=0)

    g = lambda name, layer: row1(w[name][layer])
    h1, gu1 = _ffn_fwd("l0_ffn1", h0, g("ffn1_norm", 0), wgu["ffn1", 0], wd["ffn1", 0], n_all)
    u, bu = _ssm_in("ssm_in", h1, g("mix_norm", 0), w_in, bb16, n_all)
    xs = _scan_fwd("ssm_scan", bu, tabs_f, n_ex, seq)
    h2, y = _ssm_out("ssm_out", xs, u, dskip, cb16, wout4, h1, n_all)
    h3, gu2 = _ffn_fwd("l0_ffn2", h2, g("ffn2_norm", 0), wgu["ffn2", 0], wd["ffn2", 0], n_all)
    kvraw, k, vv = _kv_proj("kv_proj", h3, row1(w["kv_norm"]), w_kv, row1(w["k_norm"]), cos, sin_s, n_all, n_kv, hd)
    h4, gu3 = _ffn_fwd("l1_ffn1", h3, g("ffn1_norm", 1), wgu["ffn1", 1], wd["ffn1", 1], n_main)
    qraw, q = _q_proj("q_proj", h4, g("mix_norm", 1), w_q, row1(w["q_norm"][0]), cos, sin_s, n_main, n_q, hd)
    sinks = row1(w["attn_sinks"][0])
    o, lse = _attn_fwd("attn_fwd", q, k, vv, sinks, n_ex, seq, n_meta, n_kv, qpk, hd)
    h5 = _attn_out("attn_out", o, h4, w_o, n_main)
    h6, gu4 = _ffn_fwd("l1_ffn2", h5, g("ffn2_norm", 1), wgu["ffn2", 1], wd["ffn2", 1], n_main)
    dh6, loss_tile = _loss_head("loss_head", h6, target.reshape(n_main, d), n_main)

    big, small = {}, {}
    dh5, dg_f2l1, dwgu_f2l1, dwd_f2l1 = _ffn_bwd("l1_ffn2", dh6, h5, g("ffn2_norm", 1), gu4, wgu["ffn2", 1], wd["ffn2", 1], n_main)
    do = _attn_out_bwd("attn_out_bwd", dh5, w_o, n_main)
    big["attn_w_o"] = _tn_plain("attn_dwo", o, dh5, 1, o.shape[1], d, n_main).reshape(N_CHIPS, -1, d)
    dq, dk_main, dv_main, dk_meta, dv_meta, dsinks = _attn_bwd("attn_bwd", q, k, vv, sinks, o, lse, do, n_ex, seq, n_meta, n_kv, qpk, hd)
    dqraw, dh4, dq_gain, dg_mix1 = _q_bwd("q_bwd", dq, qraw, row1(w["q_norm"][0]), cos, sin_s, w_q, h4, g("mix_norm", 1), dh5, n_main, n_q, hd)
    big["attn_w_q"] = _tn_rms("attn_dwq", h4, g("mix_norm", 1), dqraw, n_main).reshape(N_CHIPS, -1, dqraw.shape[1])
    dh3m, dg_f1l1, dwgu_f1l1, dwd_f1l1 = _ffn_bwd("l1_ffn1", dh4, h3, g("ffn1_norm", 1), gu3, wgu["ffn1", 1], wd["ffn1", 1], n_main)

    def with_meta(main, meta):
        blocks = [jnp.pad(meta[b * n_meta:(b + 1) * n_meta], ((META_BLOCK - n_meta, 0), (0, 0))) for b in range(n_ex)]
        return jnp.concatenate([main] + blocks, axis=0)

    dkvraw, dh3, dk_gain, dg_kv = _kv_bwd("kv_bwd", with_meta(dk_main, dk_meta), with_meta(dv_main, dv_meta), kvraw, row1(w["k_norm"]),
                                          cos, sin_s, w_kv, h3, row1(w["kv_norm"]), dh3m, n_all, n_main, n_kv, hd)
    big["w_kv"] = _tn_rms("kv_dw", h3, row1(w["kv_norm"]), dkvraw, n_all).reshape(N_CHIPS, -1, dkvraw.shape[1])
    dh2, dg_f2l0, dwgu_f2l0, dwd_f2l0 = _ffn_bwd("l0_ffn2", dh3, h2, g("ffn2_norm", 0), gu2, wgu["ffn2", 0], wd["ffn2", 0], n_all)

    dy, dz, gx, dd = _ssm_out_bwd("ssm_out_bwd", dh2, y, u, cb16, wout4, n_all)
    hw = y.shape[1]
    oc = wout4.shape[2]
    big["ssm_w_out"] = _tn_plain("ssm_dwout", y, dz, wout4.shape[0], hw, oc, n_all, a_fn=_gelu)
    dcb = _tn_plain("ssm_dcb", xs, dy, 4, xs.shape[1] // 4, hw // 4, n_all)
    gbu, da = _scan_bwd("ssm_scan_bwd", gx, xs, tabs_b, n_ex, seq)
    du, dh1, dg_mix0 = _ssm_in_bwd("ssm_in_bwd", gbu, dy, dskip, bb16, w_in, h1, g("mix_norm", 0), dh2, n_all)
    dbb = _tn_plain("ssm_dbb", u, gbu, 4, hw // 4, gbu.shape[1] // 4, n_all)
    big["ssm_w_in"] = _tn_rms("ssm_dwin", h1, g("mix_norm", 0), du, n_all).reshape(N_CHIPS, -1, hw)
    dh0, dg_f1l0, dwgu_f1l0, dwd_f1l0 = _ffn_bwd("l0_ffn1", dh1, h0, g("ffn1_norm", 0), gu1, wgu["ffn1", 0], wd["ffn1", 0], n_all)

    grad_x = dh0[:n_main].reshape(n_ex, seq, d)
    da_sum = jnp.sum(da, axis=(0, 1)).reshape(4, 2, half)
    d_ssm = ssm_vjp((dbb, dcb, da_sum[:, 0].reshape(-1), da_sum[:, 1].reshape(-1)))
    for key, val in zip(("ssm_lambda_re", "ssm_lambda_im", "ssm_log_step", "ssm_b_re", "ssm_b_im", "ssm_c_re", "ssm_c_im"), d_ssm):
        small[key] = val[None]
    small["meta_tokens"] = sum(dh0[n_main + META_BLOCK * (b + 1) - n_meta:n_main + META_BLOCK * (b + 1)] for b in range(n_ex))
    small["ffn1_norm"] = jnp.concatenate([dg_f1l0, dg_f1l1], axis=0)
    small["ffn2_norm"] = jnp.concatenate([dg_f2l0, dg_f2l1], axis=0)
    small["mix_norm"] = jnp.concatenate([dg_mix0, dg_mix1], axis=0)
    small["ssm_d"] = dd
    small["kv_norm"] = dg_kv.reshape(-1)
    small["k_norm"] = dk_gain.reshape(-1)
    small["q_norm"] = dq_gain
    small["attn_sinks"] = dsinks
    big["ffn1_w_gate_up"] = (dwgu_f1l0, dwgu_f1l1)
    big["ffn1_w_down"] = (dwd_f1l0, dwd_f1l1)
    big["ffn2_w_gate_up"] = (dwgu_f2l0, dwgu_f2l1)
    big["ffn2_w_down"] = (dwd_f2l0, dwd_f2l1)

    small_shapes = [small[k].shape for k in _SMALL] + [(SUBLANES * 128,)]
    reduced = _unpack(_all_reduce_small("reduce_small", _pack([small[k] for k in _SMALL] + [loss_tile])), small_shapes)
    loss = jnp.sum(reduced[-1])
    grads = dict(zip(_SMALL, reduced[:-1]))
    cols = d // N_CHIPS
    grads["meta_tokens"] = lax.dynamic_slice_in_dim(grads["meta_tokens"], chip * cols, cols, axis=1)
    dcols = hw // N_CHIPS
    grads["ssm_d"] = lax.dynamic_slice_in_dim(grads["ssm_d"], chip * dcols, dcols, axis=1)

    lanes = 1024
    pieces = []
    for name in _BIG:
        parts = big[name] if isinstance(big[name], tuple) else (big[name],)
        pieces += [p.reshape(N_CHIPS, 2, -1, lanes) for p in parts]
    blob = jnp.transpose(jnp.concatenate(pieces, axis=2), (1, 0, 2, 3)).astype(BF16)
    mine, theirs = _swap_halves_with_sibling("rs_swap", blob)
    chip_sum = _add_pairs("rs_chip_sum", mine, theirs, BF16)
    landed = _scatter_to_chips("rs_scatter", chip_sum)
    total = _sum_slots("rs_sum", landed)
    full = _share_half_with_sibling("rs_share", total)
    off = 0
    for name in _BIG:
        shard_shape = w[name].shape
        n_layers = 2 if isinstance(big[name], tuple) else 1
        per_layer = math.prod(shard_shape) // n_layers // lanes // 2
        layers = []
        for _ in range(n_layers):
            layers.append(full[:, off:off + per_layer].reshape(-1))
            off += per_layer
        grads[name] = jnp.concatenate(layers).reshape(shard_shape)

    deltas, new_m, new_v = {}, {}, {}
    for name in _BIG:
        shape = w[name].shape
        two_d = lambda a: a.reshape(-1, shape[-1])
        dl, nm, nv = _adamw("adamw_" + name, two_d(w[name]), two_d(grads[name]), two_d(m[name]), two_d(v[name]))
        deltas[name], new_m[name], new_v[name] = dl.reshape(shape), nm.reshape(shape), nv.reshape(shape)
    packed = [_pack([t[k] for k in _SMALL]) for t in (w, grads, m, v)]
    outs = _adamw("adamw_small", *packed)
    shapes = [w[k].shape for k in _SMALL]
    for tgt, arr in zip((deltas, new_m, new_v), outs):
        for k, val in zip(_SMALL, _unpack(arr, shapes)):
            tgt[k] = val
    grads = {k: grads[k].reshape(w[k].shape) for k in _ORDER}
    return (loss, grad_x, *[grads[k] for k in _ORDER], *[deltas[k] for k in _ORDER], *[new_m[k] for k in _ORDER],
            *[new_v[k] for k in _ORDER])


def kernel(x, meta_tokens, ffn1_norm, ffn1_w_gate_up, ffn1_w_down, mix_norm, ffn2_norm, ffn2_w_gate_up, ffn2_w_down, ssm_w_in, ssm_lambda_re, ssm_lambda_im, ssm_b_re, ssm_b_im, ssm_c_re, ssm_c_im, ssm_log_step, ssm_d, ssm_w_out, kv_norm, w_kv, k_norm, attn_w_q, q_norm, attn_sinks, attn_w_o, loss_target, m_meta_tokens, m_ffn1_norm, m_ffn1_w_gate_up, m_ffn1_w_down, m_mix_norm, m_ffn2_norm, m_ffn2_w_gate_up, m_ffn2_w_down, m_ssm_w_in, m_ssm_lambda_re, m_ssm_lambda_im, m_ssm_b_re, m_ssm_b_im, m_ssm_c_re, m_ssm_c_im, m_ssm_log_step, m_ssm_d, m_ssm_w_out, m_kv_norm, m_w_kv, m_k_norm, m_attn_w_q, m_q_norm, m_attn_sinks, m_attn_w_o, v_meta_tokens, v_ffn1_norm, v_ffn1_w_gate_up, v_ffn1_w_down, v_mix_norm, v_ffn2_norm, v_ffn2_w_gate_up, v_ffn2_w_down, v_ssm_w_in, v_ssm_lambda_re, v_ssm_lambda_im, v_ssm_b_re, v_ssm_b_im, v_ssm_c_re, v_ssm_c_im, v_ssm_log_step, v_ssm_d, v_ssm_w_out, v_kv_norm, v_w_kv, v_k_norm, v_attn_w_q, v_q_norm, v_attn_sinks, v_attn_w_o):
    args = locals()
    w = {k: args[k] for k in _ORDER}
    m = {k: args["m_" + k] for k in _ORDER}
    v = {k: args["v_" + k] for k in _ORDER}
    return _step(x, loss_target, w, m, v)
```

```python
import functools
import math

import jax
import jax.numpy as jnp
from jax import lax
from jax.experimental import pallas as pl
from jax.experimental.pallas import tpu as pltpu

F32 = jnp.float32
BF16 = jnp.bfloat16
MESH = pl.DeviceIdType.MESH

EPS = 1e-6
NEG_INF = -1e30
ROPE_THETA = 10000.0
WINDOW = 128
META_BLOCK = 128
ROW_TILE = 256
SUBLANES = 8
V7X_VMEM_LIMIT = 56 * 2**20
N_CHIPS = 4
N_DEV = 8

ADAM_LR = 0.001
ADAM_B1 = 0.9
ADAM_B2 = 0.999
ADAM_EPS = 1e-08
ADAM_WD = 0.01
ADAM_STEP = 10

_HBM = pl.BlockSpec(memory_space=pltpu.HBM)
_VMEM = pl.BlockSpec(memory_space=pltpu.VMEM)


def _call(name, body, grid, in_specs, out_specs, out_shape, scratch=()):
    return pl.pallas_call(
        body, name=name, grid=grid, in_specs=in_specs, out_specs=out_specs, out_shape=out_shape,
        scratch_shapes=list(scratch),
        compiler_params=pltpu.CompilerParams(dimension_semantics=("arbitrary",) * len(grid),
                                             vmem_limit_bytes=V7X_VMEM_LIMIT))


def _sds(shape, dtype):
    return jax.ShapeDtypeStruct(tuple(shape), dtype)


def _dot(a, b):
    return jnp.dot(a.astype(BF16), b.astype(BF16), preferred_element_type=F32)


def _dot_nt(a, b):
    return lax.dot_general(a.astype(BF16), b.astype(BF16), (((1,), (1,)), ((), ())), preferred_element_type=F32)


def _dot_tn(a, b):
    return lax.dot_general(a.astype(BF16), b.astype(BF16), (((0,), (0,)), ((), ())), preferred_element_type=F32)


def _rms(h, g):
    return h * lax.rsqrt(jnp.mean(h * h, axis=-1, keepdims=True) + EPS) * g


def _rms_bwd(h, g, dn):
    r = lax.rsqrt(jnp.mean(h * h, axis=-1, keepdims=True) + EPS)
    xh = h * r
    dxh = dn * g
    dg = jnp.sum(dn * xh, axis=0, keepdims=True)
    dh = r * (dxh - xh * jnp.mean(dxh * xh, axis=-1, keepdims=True))
    return dh, dg


def _gelu(y):
    k = math.sqrt(2.0 / math.pi)
    return 0.5 * y * (1.0 + jnp.tanh(k * (y + 0.044715 * y * y * y)))


def _gelu_grad(y):
    k = math.sqrt(2.0 / math.pi)
    t = jnp.tanh(k * (y + 0.044715 * y * y * y))
    return 0.5 * (1.0 + t) + 0.5 * y * (1.0 - t * t) * k * (1.0 + 3.0 * 0.044715 * y * y)


def _swap_halves(x):
    half = x.shape[-1] // 2
    return jnp.concatenate([x[:, half:], x[:, :half]], axis=1)


def _head_prep(x, gain, cos, sin_s, n_heads, hd):
    out = []
    for h in range(n_heads):
        xh = x[:, h * hd:(h + 1) * hd]
        y = xh * lax.rsqrt(jnp.mean(xh * xh, axis=-1, keepdims=True) + EPS) * gain
        out.append(y * cos + _swap_halves(y) * sin_s)
    return out


def _head_prep_bwd(x, gain, cos, sin_s, d_out, n_heads, hd):
    dxs = []
    dgain = jnp.zeros((1, hd), F32)
    for h in range(n_heads):
        xh = x[:, h * hd:(h + 1) * hd]
        do = d_out[:, h * hd:(h + 1) * hd]
        r = lax.rsqrt(jnp.mean(xh * xh, axis=-1, keepdims=True) + EPS)
        xhat = xh * r
        dy = do * cos + _swap_halves(do * sin_s)
        dgain = dgain + jnp.sum(dy * xhat, axis=0, keepdims=True)
        dxh = dy * gain
        dxs.append(r * (dxh - xhat * jnp.mean(dxh * xhat, axis=-1, keepdims=True)))
    return dxs, dgain


def _acc_out(ref, val, first):
    @pl.when(first)
    def _():
        ref[...] = jnp.zeros_like(ref)
    ref[...] += val


def _ffn_up(name, h, g, w4, n_rows):
    nj, d, fc = w4.shape
    tm = ROW_TILE

    def body(h_ref, g_ref, w_ref, o_ref, n_ref):
        n = _rms(h_ref[...], g_ref[...]).astype(BF16)
        n_ref[...] = n
        for j in range(nj):
            o_ref[:, j * fc:(j + 1) * fc] = _dot(n, w_ref[j]).astype(BF16)

    return _call(name, body, (n_rows // tm,),
                 [pl.BlockSpec((tm, d), lambda i: (i, 0)), pl.BlockSpec((1, d), lambda i: (0, 0)),
                  pl.BlockSpec((nj, d, fc), lambda i: (0, 0, 0))],
                 [pl.BlockSpec((tm, nj * fc), lambda i: (i, 0)), pl.BlockSpec((tm, d), lambda i: (i, 0))],
                 [_sds((n_rows, nj * fc), BF16), _sds((n_rows, d), BF16)])(h, g, w4)


def _ffn_down(name, gu, h, wd, n_rows):
    f, d = wd.shape
    tm = ROW_TILE

    def body(gu_ref, h_ref, w_ref, o_ref, s_ref):
        a = gu_ref[:, :f].astype(F32)
        b = gu_ref[:, f:].astype(F32)
        s = (a * jax.nn.sigmoid(a) * b).astype(BF16)
        s_ref[...] = s
        o_ref[...] = h_ref[...] + 0.5 * _dot(s, w_ref[...])

    return _call(name, body, (n_rows // tm,),
                 [pl.BlockSpec((tm, 2 * f), lambda i: (i, 0)), pl.BlockSpec((tm, d), lambda i: (i, 0)),
                  pl.BlockSpec((f, d), lambda i: (0, 0))],
                 [pl.BlockSpec((tm, d), lambda i: (i, 0)), pl.BlockSpec((tm, f), lambda i: (i, 0))],
                 [_sds((n_rows, d), F32), _sds((n_rows, f), BF16)])(gu, h, wd)


def _ffn_dgu(name, dh, gu, wd, n_rows):
    f, d = wd.shape
    tm = ROW_TILE

    def body(dh_ref, gu_ref, w_ref, o_ref):
        ds = _dot_nt(0.5 * dh_ref[...], w_ref[...])
        a = gu_ref[:, :f].astype(F32)
        b = gu_ref[:, f:].astype(F32)
        sg = jax.nn.sigmoid(a)
        o_ref[:, :f] = (ds * b * (sg * (1.0 + a * (1.0 - sg)))).astype(BF16)
        o_ref[:, f:] = (ds * (a * sg)).astype(BF16)

    return _call(name, body, (n_rows // tm,),
                 [pl.BlockSpec((tm, d), lambda i: (i, 0)), pl.BlockSpec((tm, 2 * f), lambda i: (i, 0)),
                  pl.BlockSpec((f, d), lambda i: (0, 0))],
                 pl.BlockSpec((tm, 2 * f), lambda i: (i, 0)), _sds((n_rows, 2 * f), BF16))(dh, gu, wd)


def _ffn_dh(name, dgu, h, g, dh, w4, n_rows):
    nj, d, fc = w4.shape
    tm = ROW_TILE

    def body(dgu_ref, h_ref, g_ref, dh_ref, w_ref, o_ref, dg_ref):
        dn = _dot_nt(dgu_ref[:, 0:fc], w_ref[0])
        for j in range(1, nj):
            dn = dn + _dot_nt(dgu_ref[:, j * fc:(j + 1) * fc], w_ref[j])
        dhn, dg = _rms_bwd(h_ref[...], g_ref[...], dn)
        o_ref[...] = dh_ref[...] + dhn
        _acc_out(dg_ref, dg, pl.program_id(0) == 0)

    return _call(name, body, (n_rows // tm,),
                 [pl.BlockSpec((tm, nj * fc), lambda i: (i, 0)), pl.BlockSpec((tm, d), lambda i: (i, 0)),
                  pl.BlockSpec((1, d), lambda i: (0, 0)), pl.BlockSpec((tm, d), lambda i: (i, 0)),
                  pl.BlockSpec((nj, d, fc), lambda i: (0, 0, 0))],
                 [pl.BlockSpec((tm, d), lambda i: (i, 0)), pl.BlockSpec((1, d), lambda i: (0, 0))],
                 [_sds((n_rows, d), F32), _sds((1, d), F32)])(dgu, h, g, dh, w4)


def _contract_tile(n_rows, cap=1024):
    best = ROW_TILE
    for t in range(ROW_TILE, cap + 1, ROW_TILE):
        if n_rows % t == 0:
            best = t
    return best


def _tn(name, operands, in_specs, prologue, nj, ma, nb, n_rows, tk):
    def body(*refs):
        o_ref = refs[-1]
        a, b = prologue(pl.program_id(0), *refs[:-1])
        _acc_out(o_ref, _dot_tn(a, b)[None], pl.program_id(1) == 0)

    return _call(name, body, (nj, n_rows // tk), in_specs, pl.BlockSpec((1, ma, nb), lambda j, k: (j, 0, 0)),
                 _sds((nj, ma, nb), F32))(*operands)


def _ffn_dwgu(name, n, dgu, nj, n_rows):
    d = n.shape[1]
    fc = dgu.shape[1] // nj
    tk = _contract_tile(n_rows)
    return _tn(name, (n, dgu),
               [pl.BlockSpec((tk, d), lambda j, k: (k, 0)), pl.BlockSpec((tk, fc), lambda j, k: (k, j))],
               lambda j, a_ref, b_ref: (a_ref[...], b_ref[...]), nj, d, fc, n_rows, tk)


def _ffn_dwd(name, s, dh, n_rows):
    f = s.shape[1]
    d = dh.shape[1]
    tk = _contract_tile(n_rows)
    return _tn(name, (s, dh),
               [pl.BlockSpec((tk, f), lambda j, k: (k, 0)), pl.BlockSpec((tk, d), lambda j, k: (k, 0))],
               lambda j, s_ref, dh_ref: (s_ref[...], 0.5 * dh_ref[...]), 1, f, d, n_rows, tk)


def _ffn_fwd(tag, h, g, w4, wd, n_rows):
    gu, n = _ffn_up(tag + "_up", h, g, w4, n_rows)
    h_out, s = _ffn_down(tag + "_down", gu, h, wd, n_rows)
    return h_out, (gu, n, s)


def _ffn_bwd(tag, dh_out, h, g, saved, w4, wd, n_rows):
    gu, n, s = saved
    nj = w4.shape[0]
    dgu = _ffn_dgu(tag + "_dgu", dh_out, gu, wd, n_rows)
    dwd = _ffn_dwd(tag + "_dwd", s, dh_out, n_rows)
    dh_in, dg = _ffn_dh(tag + "_dh", dgu, h, g, dh_out, w4, n_rows)
    dwgu = _ffn_dwgu(tag + "_dwgu", n, dgu, nj, n_rows)
    f, d = wd.shape
    return dh_in, dg, dwgu, dwd.reshape(N_CHIPS, f // N_CHIPS, d)


def _ssm_in(name, h, g, w_in, bb, n_rows):
    d, hw = w_in.shape
    nj, uc, xc = bb.shape
    tm = ROW_TILE

    def body(h_ref, g_ref, w_ref, bb_ref, u_ref, bu_ref):
        u = _dot(_rms(h_ref[...], g_ref[...]), w_ref[...])
        u_ref[...] = u
        for j in range(nj):
            bu_ref[:, j * xc:(j + 1) * xc] = _dot(u[:, j * uc:(j + 1) * uc], bb_ref[j])

    return _call(name, body, (n_rows // tm,),
                 [pl.BlockSpec((tm, d), lambda i: (i, 0)), pl.BlockSpec((1, d), lambda i: (0, 0)),
                  pl.BlockSpec((d, hw), lambda i: (0, 0)), pl.BlockSpec((nj, uc, xc), lambda i: (0, 0, 0))],
                 [pl.BlockSpec((tm, hw), lambda i: (i, 0)), pl.BlockSpec((tm, nj * xc), lambda i: (i, 0))],
                 [_sds((n_rows, hw), F32), _sds((n_rows, nj * xc), F32)])(h, g, w_in, bb)


def _cmul_add(xr, xi, ar, ai, sr, si):
    return xr + ar * sr - ai * si, xi + ar * si + ai * sr


def _scan_row_block(n_main_blocks, seq_blocks):
    return lambda b, i: jnp.where(i == 0, n_main_blocks + b, b * seq_blocks + i - 1)


def _scan_fwd(name, bu, tabs, n_ex, seq):
    n_rows, width = bu.shape
    nj = 4
    cw = width // nj
    half = cw // 2
    tq = META_BLOCK
    seq_blocks = seq // tq
    rb = _scan_row_block(n_ex * seq_blocks, seq_blocks)

    def body(bu_ref, tab_ref, x_ref, carry_ref):
        @pl.when(pl.program_id(2) == 0)
        def _():
            carry_ref[...] = jnp.zeros_like(carry_ref)
        def blk(k, c):
            t = [tab_ref[n * SUBLANES:(n + 1) * SUBLANES, :] for n in range(8)]
            r0 = pl.multiple_of(k * SUBLANES, SUBLANES)
            xr = bu_ref[pl.ds(r0, SUBLANES), 0:half]
            xi = bu_ref[pl.ds(r0, SUBLANES), half:cw]
            for s, d in enumerate((1, 2, 4)):
                xr, xi = _cmul_add(xr, xi, t[2 * s], t[2 * s + 1], pltpu.roll(xr, d, 0), pltpu.roll(xi, d, 0))
            xr, xi = _cmul_add(xr, xi, t[6], t[7], c[0], c[1])
            x_ref[pl.ds(r0, SUBLANES), 0:half] = xr
            x_ref[pl.ds(r0, SUBLANES), half:cw] = xi
            last = SUBLANES - 1
            return (jnp.broadcast_to(xr[last:last + 1, :], xr.shape), jnp.broadcast_to(xi[last:last + 1, :], xi.shape))

        c = lax.fori_loop(0, tq // SUBLANES, blk, (carry_ref[0], carry_ref[1]))
        carry_ref[0] = c[0]
        carry_ref[1] = c[1]

    return _call(name, body, (n_ex, nj, seq_blocks + 1),
                 [pl.BlockSpec((tq, cw), lambda b, j, i: (rb(b, i), j)), pl.BlockSpec((8 * SUBLANES, half), lambda b, j, i: (0, j))],
                 pl.BlockSpec((tq, cw), lambda b, j, i: (rb(b, i), j)), _sds((n_rows, width), F32),
                 scratch=[pltpu.VMEM((2, SUBLANES, half), F32)])(bu, tabs)


def _scan_bwd(name, gx, x, tabs, n_ex, seq):
    n_rows, width = gx.shape
    nj = 4
    cw = width // nj
    half = cw // 2
    tq = META_BLOCK
    seq_blocks = seq // tq
    n_steps = seq_blocks + 1
    rb = _scan_row_block(n_ex * seq_blocks, seq_blocks)
    rbr = lambda b, i: rb(b, n_steps - 1 - i)

    def body(gx_ref, x_ref, tab_ref, g_ref, da_ref, carry_ref):
        @pl.when(pl.program_id(2) == 0)
        def _():
            carry_ref[...] = jnp.zeros_like(carry_ref)
            da_ref[...] = jnp.zeros_like(da_ref)
        row = lax.broadcasted_iota(jnp.int32, (SUBLANES, half), 0)
        n_blk = tq // SUBLANES

        def blk(kk, st):
            t = [tab_ref[n * SUBLANES:(n + 1) * SUBLANES, :] for n in range(8)]
            cr, ci, dar, dai = st
            r0 = pl.multiple_of((n_blk - 1 - kk) * SUBLANES, SUBLANES)
            gr = gx_ref[pl.ds(r0, SUBLANES), 0:half]
            gi = gx_ref[pl.ds(r0, SUBLANES), half:cw]
            for s, d in enumerate((1, 2, 4)):
                gr, gi = _cmul_add(gr, gi, t[2 * s], t[2 * s + 1],
                                   pltpu.roll(gr, SUBLANES - d, 0), pltpu.roll(gi, SUBLANES - d, 0))
            gr, gi = _cmul_add(gr, gi, t[6], t[7], cr, ci)
            g_ref[pl.ds(r0, SUBLANES), 0:half] = gr.astype(BF16)
            g_ref[pl.ds(r0, SUBLANES), half:cw] = gi.astype(BF16)
            hr = jnp.where(row == SUBLANES - 1, cr, pltpu.roll(gr, SUBLANES - 1, 0))
            hi = jnp.where(row == SUBLANES - 1, ci, pltpu.roll(gi, SUBLANES - 1, 0))
            xr = x_ref[pl.ds(r0, SUBLANES), 0:half]
            xi = x_ref[pl.ds(r0, SUBLANES), half:cw]
            dar = dar + xr * hr + xi * hi
            dai = dai + xr * hi - xi * hr
            return (jnp.broadcast_to(gr[0:1, :], gr.shape), jnp.broadcast_to(gi[0:1, :], gi.shape), dar, dai)

        st = lax.fori_loop(0, n_blk, blk, (carry_ref[0], carry_ref[1], da_ref[0, :, 0:half], da_ref[0, :, half:cw]))
        carry_ref[0] = st[0]
        carry_ref[1] = st[1]
        da_ref[0, :, 0:half] = st[2]
        da_ref[0, :, half:cw] = st[3]

    return _call(name, body, (n_ex, nj, n_steps),
                 [pl.BlockSpec((tq, cw), lambda b, j, i: (rbr(b, i), j)), pl.BlockSpec((tq, cw), lambda b, j, i: (rbr(b, i), j)),
                  pl.BlockSpec((8 * SUBLANES, half), lambda b, j, i: (0, j))],
                 [pl.BlockSpec((tq, cw), lambda b, j, i: (rbr(b, i), j)), pl.BlockSpec((1, SUBLANES, cw), lambda b, j, i: (b, 0, j))],
                 [_sds((n_rows, width), BF16), _sds((n_ex, SUBLANES, width), F32)],
                 scratch=[pltpu.VMEM((2, SUBLANES, half), F32)])(gx, x, tabs)


def _ssm_z(gy, wout_ref, nj):
    return jnp.concatenate([_dot(gy, wout_ref[j]) for j in range(nj)], axis=1)


def _ssm_out(name, x, u, dskip, cb, wout4, h, n_rows):
    nj, xc, uc = cb.shape
    no, hw, oc = wout4.shape
    d = h.shape[1]
    tm = ROW_TILE

    def body(x_ref, u_ref, ds_ref, cb_ref, w_ref, h_ref, o_ref, y_ref):
        y = jnp.concatenate([_dot(x_ref[:, j * xc:(j + 1) * xc], cb_ref[j]) for j in range(nj)], axis=1)
        y = y + ds_ref[...] * u_ref[...]
        y_ref[...] = y
        z = _ssm_z(_gelu(y), w_ref, no)
        o_ref[...] = h_ref[...] + z[:, :d] * jax.nn.sigmoid(z[:, d:])

    return _call(name, body, (n_rows // tm,),
                 [pl.BlockSpec((tm, nj * xc), lambda i: (i, 0)), pl.BlockSpec((tm, hw), lambda i: (i, 0)),
                  pl.BlockSpec((1, hw), lambda i: (0, 0)), pl.BlockSpec((nj, xc, uc), lambda i: (0, 0, 0)),
                  pl.BlockSpec((no, hw, oc), lambda i: (0, 0, 0)), pl.BlockSpec((tm, d), lambda i: (i, 0))],
                 [pl.BlockSpec((tm, d), lambda i: (i, 0)), pl.BlockSpec((tm, hw), lambda i: (i, 0))],
                 [_sds((n_rows, d), F32), _sds((n_rows, hw), F32)])(x, u, dskip, cb, wout4, h)


def _ssm_out_bwd(name, dh, y, u, cb, wout4, n_rows):
    nj, xc, uc = cb.shape
    no, hw, oc = wout4.shape
    d = dh.shape[1]
    tm = ROW_TILE

    def body(dh_ref, y_ref, u_ref, cb_ref, w_ref, dy_ref, dz_ref, gx_ref, dd_ref):
        y = y_ref[...]
        z = _ssm_z(_gelu(y), w_ref, no)
        za = z[:, :d]
        sg = jax.nn.sigmoid(z[:, d:])
        dmix = dh_ref[...]
        dz = jnp.concatenate([dmix * sg, dmix * za * sg * (1.0 - sg)], axis=1).astype(BF16)
        dz_ref[...] = dz
        dgy = _dot_nt(dz[:, 0:oc], w_ref[0])
        for j in range(1, no):
            dgy = dgy + _dot_nt(dz[:, j * oc:(j + 1) * oc], w_ref[j])
        dy = dgy * _gelu_grad(y)
        dy_ref[...] = dy
        _acc_out(dd_ref, jnp.sum(dy * u_ref[...], axis=0, keepdims=True), pl.program_id(0) == 0)
        for j in range(nj):
            gx_ref[:, j * xc:(j + 1) * xc] = _dot_nt(dy[:, j * uc:(j + 1) * uc], cb_ref[j])

    return _call(name, body, (n_rows // tm,),
                 [pl.BlockSpec((tm, d), lambda i: (i, 0)), pl.BlockSpec((tm, hw), lambda i: (i, 0)),
                  pl.BlockSpec((tm, hw), lambda i: (i, 0)), pl.BlockSpec((nj, xc, uc), lambda i: (0, 0, 0)),
                  pl.BlockSpec((no, hw, oc), lambda i: (0, 0, 0))],
                 [pl.BlockSpec((tm, hw), lambda i: (i, 0)), pl.BlockSpec((tm, no * oc), lambda i: (i, 0)),
                  pl.BlockSpec((tm, nj * xc), lambda i: (i, 0)), pl.BlockSpec((1, hw), lambda i: (0, 0))],
                 [_sds((n_rows, hw), F32), _sds((n_rows, no * oc), BF16), _sds((n_rows, nj * xc), F32),
                  _sds((1, hw), F32)])(dh, y, u, cb, wout4)


def _ssm_in_bwd(name, gbu, dy, dskip, bb, w_in, h, g, dh, n_rows):
    nj, uc, xc = bb.shape
    d, hw = w_in.shape
    tm = ROW_TILE

    def body(gb_ref, dy_ref, ds_ref, bb_ref, w_ref, h_ref, g_ref, dh_ref, du_ref, o_ref, dg_ref):
        du = jnp.concatenate([_dot_nt(gb_ref[:, j * xc:(j + 1) * xc], bb_ref[j]) for j in range(nj)], axis=1)
        du = du + dy_ref[...] * ds_ref[...]
        du_ref[...] = du.astype(BF16)
        dhn, dg = _rms_bwd(h_ref[...], g_ref[...], _dot_nt(du, w_ref[...]))
        o_ref[...] = dh_ref[...] + dhn
        _acc_out(dg_ref, dg, pl.program_id(0) == 0)

    return _call(name, body, (n_rows // tm,),
                 [pl.BlockSpec((tm, nj * xc), lambda i: (i, 0)), pl.BlockSpec((tm, hw), lambda i: (i, 0)),
                  pl.BlockSpec((1, hw), lambda i: (0, 0)), pl.BlockSpec((nj, uc, xc), lambda i: (0, 0, 0)),
                  pl.BlockSpec((d, hw), lambda i: (0, 0)), pl.BlockSpec((tm, d), lambda i: (i, 0)),
                  pl.BlockSpec((1, d), lambda i: (0, 0)), pl.BlockSpec((tm, d), lambda i: (i, 0))],
                 [pl.BlockSpec((tm, hw), lambda i: (i, 0)), pl.BlockSpec((tm, d), lambda i: (i, 0)),
                  pl.BlockSpec((1, d), lambda i: (0, 0))],
                 [_sds((n_rows, hw), BF16), _sds((n_rows, d), F32), _sds((1, d), F32)])(gbu, dy, dskip, bb, w_in, h, g, dh)


def _discretize(lam_re, lam_im, log_step, b_re, b_im):
    step = jnp.exp(log_step)[:, None]
    mag = jnp.exp(lam_re * step)
    ar = mag * jnp.cos(lam_im * step)
    ai = mag * jnp.sin(lam_im * step)
    den = lam_re * lam_re + lam_im * lam_im
    nr, ni = ar - 1.0, ai
    cr = (nr * lam_re + ni * lam_im) / den
    ci = (ni * lam_re - nr * lam_im) / den
    bbar_r = cr[..., None] * b_re - ci[..., None] * b_im
    bbar_i = cr[..., None] * b_im + ci[..., None] * b_re
    return ar, ai, bbar_r, bbar_i


def _ssm_mats(lam_re, lam_im, log_step, b_re, b_im, c_re, c_im):
    n_g, n_p, n_c = b_re.shape
    gpc = n_g // 4
    ar, ai, bbar_r, bbar_i = _discretize(lam_re, lam_im, log_step, b_re, b_im)
    eye = jnp.eye(gpc, dtype=F32)

    def in_map(bbar):
        return jnp.einsum('jgpc,gh->jgchp', bbar.reshape(4, gpc, n_p, n_c), eye).reshape(4, gpc * n_c, gpc * n_p)

    def out_map(c):
        return jnp.einsum('jgcp,gh->jgphc', c.reshape(4, gpc, n_c, n_p), eye).reshape(4, gpc * n_p, gpc * n_c)

    bb = jnp.concatenate([in_map(bbar_r), in_map(bbar_i)], axis=2)
    cb = jnp.concatenate([out_map(c_re), -out_map(c_im)], axis=1)
    return bb, cb, ar.reshape(-1), ai.reshape(-1)


def _chunked(v, half):
    return v.reshape(v.shape[:-1] + (4, half))


def _scan_tables(ar, ai, reverse):
    if reverse:
        ai = -ai
    pr, pi = [ar], [ai]
    for _ in range(SUBLANES - 1):
        pr, pi = pr + [pr[-1] * ar - pi[-1] * ai], pi + [pr[-1] * ai + pi[-1] * ar]
    row = jnp.arange(SUBLANES)[:, None]
    tabs = []
    for d in (1, 2, 4):
        keep = (row <= SUBLANES - 1 - d) if reverse else (row >= d)
        tabs += [jnp.where(keep, pr[d - 1][None, :], 0.0), jnp.where(keep, pi[d - 1][None, :], 0.0)]
    order = list(range(SUBLANES))[::-1] if reverse else list(range(SUBLANES))
    tabs += [jnp.stack([pr[k] for k in order]), jnp.stack([pi[k] for k in order])]
    return jnp.concatenate(tabs, axis=0)


def _kv_proj(name, h, g, w_kv, k_gain, cos, sin_s, n_rows, n_kv, hd):
    d, kvw = w_kv.shape
    kw = n_kv * hd
    tm = ROW_TILE

    def body(h_ref, g_ref, w_ref, kg_ref, c_ref, s_ref, raw_ref, k_ref, v_ref):
        raw = _dot(_rms(h_ref[...], g_ref[...]), w_ref[...])
        raw_ref[...] = raw
        ks = _head_prep(raw[:, :kw], kg_ref[...], c_ref[...], s_ref[...], n_kv, hd)
        k_ref[...] = jnp.concatenate(ks, axis=1).astype(BF16)
        v_ref[...] = raw[:, kw:].astype(BF16)

    return _call(name, body, (n_rows // tm,),
                 [pl.BlockSpec((tm, d), lambda i: (i, 0)), pl.BlockSpec((1, d), lambda i: (0, 0)),
                  pl.BlockSpec((d, kvw), lambda i: (0, 0)), pl.BlockSpec((1, hd), lambda i: (0, 0)),
                  pl.BlockSpec((tm, hd), lambda i: (i, 0)), pl.BlockSpec((tm, hd), lambda i: (i, 0))],
                 [pl.BlockSpec((tm, kvw), lambda i: (i, 0)), pl.BlockSpec((tm, kw), lambda i: (i, 0)),
                  pl.BlockSpec((tm, kw), lambda i: (i, 0))],
                 [_sds((n_rows, kvw), F32), _sds((n_rows, kw), BF16), _sds((n_rows, kw), BF16)])(
                     h, g, w_kv, k_gain, cos, sin_s)


def _q_proj(name, h, g, w_q, q_gain, cos, sin_s, n_rows, n_q, hd):
    d, qw = w_q.shape
    tm = ROW_TILE

    def body(h_ref, g_ref, w_ref, qg_ref, c_ref, s_ref, raw_ref, q_ref):
        raw = _dot(_rms(h_ref[...], g_ref[...]), w_ref[...])
        raw_ref[...] = raw
        qs = _head_prep(raw, qg_ref[...], c_ref[...], s_ref[...], n_q, hd)
        q_ref[...] = jnp.concatenate(qs, axis=1).astype(BF16)

    return _call(name, body, (n_rows // tm,),
                 [pl.BlockSpec((tm, d), lambda i: (i, 0)), pl.BlockSpec((1, d), lambda i: (0, 0)),
                  pl.BlockSpec((d, qw), lambda i: (0, 0)), pl.BlockSpec((1, hd), lambda i: (0, 0)),
                  pl.BlockSpec((tm, hd), lambda i: (i, 0)), pl.BlockSpec((tm, hd), lambda i: (i, 0))],
                 [pl.BlockSpec((tm, qw), lambda i: (i, 0)), pl.BlockSpec((tm, qw), lambda i: (i, 0))],
                 [_sds((n_rows, qw), F32), _sds((n_rows, qw), BF16)])(h, g, w_q, q_gain, cos, sin_s)


def _attn_specs(seq, n_ex, n_meta, kw):
    nb = seq // WINDOW
    meta_blk = lambda b: (n_ex * seq + META_BLOCK * b + META_BLOCK - n_meta) // n_meta
    return [pl.BlockSpec((WINDOW, kw), lambda b, n: (b * nb + jnp.maximum(n - 1, 0), 0)),
            pl.BlockSpec((WINDOW, kw), lambda b, n: (b * nb + n, 0)),
            pl.BlockSpec((n_meta, kw), lambda b, n: (meta_blk(b), 0))]


def _attn_mask(n, qpk, n_keys):
    rows = qpk * WINDOW
    qi = lax.broadcasted_iota(jnp.int32, (rows, n_keys), 0) & (WINDOW - 1)
    kj = lax.broadcasted_iota(jnp.int32, (rows, n_keys), 1)
    rel = qi + WINDOW - kj
    band = (rel >= 0) & (rel < WINDOW) & ((n > 0) | (kj >= WINDOW))
    return band | (kj >= 2 * WINDOW)


def _stack_heads(ref, h, qpk, hd, dtype=None):
    parts = [ref[:, (h * qpk + gq) * hd:(h * qpk + gq + 1) * hd] for gq in range(qpk)]
    out = jnp.concatenate(parts, axis=0)
    return out if dtype is None else out.astype(dtype)


def _col(tile, c):
    lane = lax.broadcasted_iota(jnp.int32, tile.shape, 1)
    return jnp.sum(jnp.where(lane == c, tile, 0.0), axis=-1, keepdims=True)


def _put_col(col, c, n):
    lane = lax.broadcasted_iota(jnp.int32, (col.shape[0], n), 1)
    return jnp.where(lane == c, col, 0.0)


def _stack_cols(tile, h, qpk):
    return jnp.concatenate([_col(tile, h * qpk + gq) for gq in range(qpk)], axis=0)


def _sink_col(sinks, h, qpk):
    return jnp.concatenate([jnp.broadcast_to(_col(sinks, h * qpk + gq), (WINDOW, 1)) for gq in range(qpk)], axis=0)


def _attn_fwd(name, q, k, v, sinks, n_ex, seq, n_meta, n_kv, qpk, hd):
    nb = seq // WINDOW
    n_q = n_kv * qpk
    kw = n_kv * hd
    qw = n_q * hd
    n_keys = 2 * WINDOW + n_meta
    scale = hd ** -0.5

    def body(q_ref, kp_ref, kc_ref, km_ref, vp_ref, vc_ref, vm_ref, sk_ref, o_ref, lse_ref):
        valid = _attn_mask(pl.program_id(1), qpk, n_keys)
        sinks_v = sk_ref[...]
        o_parts = []
        lse_all = jnp.zeros((WINDOW, n_q), F32)
        for h in range(n_kv):
            hs = slice(h * hd, (h + 1) * hd)
            kb = jnp.concatenate([kp_ref[:, hs], kc_ref[:, hs], km_ref[:, hs]], axis=0)
            vb = jnp.concatenate([vp_ref[:, hs], vc_ref[:, hs], vm_ref[:, hs]], axis=0)
            s = jnp.where(valid, _dot_nt(_stack_heads(q_ref, h, qpk, hd), kb) * scale, NEG_INF)
            skc = _sink_col(sinks_v, h, qpk)
            m = jnp.maximum(jnp.max(s, axis=-1, keepdims=True), skc)
            p = jnp.exp(s - m)
            den = jnp.sum(p, axis=-1, keepdims=True) + jnp.exp(skc - m)
            o = _dot(p, vb) / den
            lse = m + jnp.log(den)
            for gq in range(qpk):
                o_parts.append(o[gq * WINDOW:(gq + 1) * WINDOW])
                lse_all = lse_all + _put_col(lse[gq * WINDOW:(gq + 1) * WINDOW], h * qpk + gq, n_q)
        o_ref[...] = jnp.concatenate(o_parts, axis=1)
        lse_ref[...] = lse_all

    qspec = pl.BlockSpec((WINDOW, qw), lambda b, n: (b * nb + n, 0))
    return _call(name, body, (n_ex, nb),
                 [qspec] + _attn_specs(seq, n_ex, n_meta, kw) + _attn_specs(seq, n_ex, n_meta, kw)
                 + [pl.BlockSpec((1, n_q), lambda b, n: (0, 0))],
                 [qspec, pl.BlockSpec((WINDOW, n_q), lambda b, n: (b * nb + n, 0))],
                 [_sds((n_ex * seq, qw), F32), _sds((n_ex * seq, n_q), F32)])(q, k, k, k, v, v, v, sinks)


def _attn_bwd(name, q, k, v, sinks, o, lse, do, n_ex, seq, n_meta, n_kv, qpk, hd):
    nb = seq // WINDOW
    n_q = n_kv * qpk
    kw = n_kv * hd
    qw = n_q * hd
    n_keys = 2 * WINDOW + n_meta
    scale = hd ** -0.5

    def body(q_ref, kp_ref, kc_ref, km_ref, vp_ref, vc_ref, vm_ref, sk_ref, o_ref, lse_ref, do_ref,
             dq_ref, dk_ref, dv_ref, dkm_ref, dvm_ref, dsk_ref):
        n = pl.program_id(1)

        @pl.when(n == 0)
        def _():
            dk_ref[...] = jnp.zeros_like(dk_ref)
            dv_ref[...] = jnp.zeros_like(dv_ref)
            dkm_ref[...] = jnp.zeros_like(dkm_ref)
            dvm_ref[...] = jnp.zeros_like(dvm_ref)

        @pl.when((n == 0) & (pl.program_id(0) == 0))
        def _():
            dsk_ref[...] = jnp.zeros_like(dsk_ref)

        valid = _attn_mask(n, qpk, n_keys)
        sinks_v = sk_ref[...]
        lse_v = lse_ref[...]
        dq_parts, dk_parts, dv_parts = [], [], []
        dsk = jnp.zeros((1, n_q), F32)
        for h in range(n_kv):
            hs = slice(h * hd, (h + 1) * hd)
            kb = jnp.concatenate([kp_ref[:, hs], kc_ref[:, hs], km_ref[:, hs]], axis=0)
            vb = jnp.concatenate([vp_ref[:, hs], vc_ref[:, hs], vm_ref[:, hs]], axis=0)
            qs = _stack_heads(q_ref, h, qpk, hd)
            dos = _stack_heads(do_ref, h, qpk, hd)
            delta = jnp.sum(dos * _stack_heads(o_ref, h, qpk, hd), axis=-1, keepdims=True)
            lse_c = _stack_cols(lse_v, h, qpk)
            s = jnp.where(valid, _dot_nt(qs, kb) * scale, NEG_INF)
            p = jnp.exp(s - lse_c)
            ds = p * (_dot_nt(dos, vb) - delta)
            dqs = _dot(ds, kb) * scale
            dk_parts.append(_dot_tn(ds, qs) * scale)
            dv_parts.append(_dot_tn(p, dos))
            dsink = -jnp.exp(_sink_col(sinks_v, h, qpk) - lse_c) * delta
            for gq in range(qpk):
                dq_parts.append(dqs[gq * WINDOW:(gq + 1) * WINDOW])
                dsk = dsk + _put_col(jnp.sum(dsink[gq * WINDOW:(gq + 1) * WINDOW], axis=0, keepdims=True), h * qpk + gq, n_q)
        dq_ref[...] = jnp.concatenate(dq_parts, axis=1)
        dsk_ref[...] += dsk
        dkb = jnp.concatenate(dk_parts, axis=1)
        dvb = jnp.concatenate(dv_parts, axis=1)
        prev = pl.ds(pl.multiple_of(jnp.maximum(n - 1, 0) * WINDOW, WINDOW), WINDOW)
        cur = pl.ds(pl.multiple_of(n * WINDOW, WINDOW), WINDOW)
        dk_ref[prev, :] += dkb[0:WINDOW]
        dv_ref[prev, :] += dvb[0:WINDOW]
        dk_ref[cur, :] += dkb[WINDOW:2 * WINDOW]
        dv_ref[cur, :] += dvb[WINDOW:2 * WINDOW]
        dkm_ref[...] += dkb[2 * WINDOW:]
        dvm_ref[...] += dvb[2 * WINDOW:]

    qspec = pl.BlockSpec((WINDOW, qw), lambda b, n: (b * nb + n, 0))
    exspec = pl.BlockSpec((seq, kw), lambda b, n: (b, 0))
    mspec = pl.BlockSpec((n_meta, kw), lambda b, n: (b, 0))
    return _call(name, body, (n_ex, nb),
                 [qspec] + _attn_specs(seq, n_ex, n_meta, kw) + _attn_specs(seq, n_ex, n_meta, kw)
                 + [pl.BlockSpec((1, n_q), lambda b, n: (0, 0)), qspec,
                    pl.BlockSpec((WINDOW, n_q), lambda b, n: (b * nb + n, 0)), qspec],
                 [qspec, exspec, exspec, mspec, mspec, pl.BlockSpec((1, n_q), lambda b, n: (0, 0))],
                 [_sds((n_ex * seq, qw), F32), _sds((n_ex * seq, kw), F32), _sds((n_ex * seq, kw), F32),
                  _sds((n_ex * n_meta, kw), F32), _sds((n_ex * n_meta, kw), F32), _sds((1, n_q), F32)])(
                      q, k, k, k, v, v, v, sinks, o, lse, do)


def _attn_out(name, o, h, w_o, n_rows):
    qw, d = w_o.shape
    tm = ROW_TILE

    def body(o_ref, h_ref, w_ref, out_ref):
        out_ref[...] = h_ref[...] + _dot(o_ref[...], w_ref[...])

    return _call(name, body, (n_rows // tm,),
                 [pl.BlockSpec((tm, qw), lambda i: (i, 0)), pl.BlockSpec((tm, d), lambda i: (i, 0)),
                  pl.BlockSpec((qw, d), lambda i: (0, 0))],
                 pl.BlockSpec((tm, d), lambda i: (i, 0)), _sds((n_rows, d), F32))(o, h, w_o)


def _attn_out_bwd(name, dh, w_o, n_rows):
    qw, d = w_o.shape
    tm = ROW_TILE

    def body(dh_ref, w_ref, do_ref):
        do_ref[...] = _dot_nt(dh_ref[...], w_ref[...])

    return _call(name, body, (n_rows // tm,),
                 [pl.BlockSpec((tm, d), lambda i: (i, 0)), pl.BlockSpec((qw, d), lambda i: (0, 0))],
                 pl.BlockSpec((tm, qw), lambda i: (i, 0)), _sds((n_rows, qw), F32))(dh, w_o)


def _q_bwd(name, dq, qraw, q_gain, cos, sin_s, w_q, h, g, dh, n_rows, n_q, hd):
    d, qw = w_q.shape
    tm = ROW_TILE

    def body(dq_ref, raw_ref, qg_ref, c_ref, s_ref, w_ref, h_ref, g_ref, dh_ref, draw_ref, o_ref, dqg_ref, dg_ref):
        dxs, dgain = _head_prep_bwd(raw_ref[...], qg_ref[...], c_ref[...], s_ref[...], dq_ref[...], n_q, hd)
        draw = jnp.concatenate(dxs, axis=1).astype(BF16)
        draw_ref[...] = draw
        dhn, dg = _rms_bwd(h_ref[...], g_ref[...], _dot_nt(draw, w_ref[...]))
        o_ref[...] = dh_ref[...] + dhn
        first = pl.program_id(0) == 0
        _acc_out(dqg_ref, dgain, first)
        _acc_out(dg_ref, dg, first)

    row = lambda w: pl.BlockSpec((tm, w), lambda i: (i, 0))
    one = lambda w: pl.BlockSpec((1, w), lambda i: (0, 0))
    return _call(name, body, (n_rows // tm,),
                 [row(qw), row(qw), one(hd), row(hd), row(hd), pl.BlockSpec((d, qw), lambda i: (0, 0)), row(d), one(d), row(d)],
                 [row(qw), row(d), one(hd), one(d)],
                 [_sds((n_rows, qw), BF16), _sds((n_rows, d), F32), _sds((1, hd), F32), _sds((1, d), F32)])(
                     dq, qraw, q_gain, cos, sin_s, w_q, h, g, dh)


def _kv_bwd(name, dk, dv, kvraw, k_gain, cos, sin_s, w_kv, h, g, dh_main, n_rows, n_main, n_kv, hd):
    d, kvw = w_kv.shape
    kw = n_kv * hd
    tm = ROW_TILE
    n_main_tiles = n_main // tm

    def body(dk_ref, dv_ref, raw_ref, kg_ref, c_ref, s_ref, w_ref, h_ref, g_ref, dh_ref, draw_ref, o_ref, dkg_ref, dg_ref):
        i = pl.program_id(0)
        dxs, dgain = _head_prep_bwd(raw_ref[:, :kw], kg_ref[...], c_ref[...], s_ref[...], dk_ref[...], n_kv, hd)
        draw = jnp.concatenate(dxs + [dv_ref[...]], axis=1).astype(BF16)
        draw_ref[...] = draw
        dhn, dg = _rms_bwd(h_ref[...], g_ref[...], _dot_nt(draw, w_ref[...]))
        o_ref[...] = jnp.where(i < n_main_tiles, dh_ref[...], 0.0) + dhn
        _acc_out(dkg_ref, dgain, i == 0)
        _acc_out(dg_ref, dg, i == 0)

    row = lambda w: pl.BlockSpec((tm, w), lambda i: (i, 0))
    one = lambda w: pl.BlockSpec((1, w), lambda i: (0, 0))
    return _call(name, body, (n_rows // tm,),
                 [row(kw), row(kw), row(kvw), one(hd), row(hd), row(hd), pl.BlockSpec((d, kvw), lambda i: (0, 0)), row(d),
                  one(d), pl.BlockSpec((tm, d), lambda i: (jnp.minimum(i, n_main_tiles - 1), 0))],
                 [row(kvw), row(d), one(hd), one(d)],
                 [_sds((n_rows, kvw), BF16), _sds((n_rows, d), F32), _sds((1, hd), F32), _sds((1, d), F32)])(
                     dk, dv, kvraw, k_gain, cos, sin_s, w_kv, h, g, dh_main)


def _tn_rms(name, h, g, b, n_rows):
    d = h.shape[1]
    nb = b.shape[1]
    tk = _contract_tile(n_rows)
    return _tn(name, (h, g, b),
               [pl.BlockSpec((tk, d), lambda j, k: (k, 0)), pl.BlockSpec((1, d), lambda j, k: (0, 0)),
                pl.BlockSpec((tk, nb), lambda j, k: (k, 0))],
               lambda j, h_ref, g_ref, b_ref: (_rms(h_ref[...], g_ref[...]), b_ref[...]), 1, d, nb, n_rows, tk)


def _tn_plain(name, a, b, nj, a_cols, b_cols, n_rows, a_fn=None):
    tk = _contract_tile(n_rows)
    fa = (lambda v: v) if a_fn is None else a_fn
    a_map = (lambda j, k: (k, j)) if a.shape[1] != a_cols else (lambda j, k: (k, 0))
    b_map = (lambda j, k: (k, j)) if b.shape[1] != b_cols else (lambda j, k: (k, 0))
    return _tn(name, (a, b), [pl.BlockSpec((tk, a_cols), a_map), pl.BlockSpec((tk, b_cols), b_map)],
               lambda j, a_ref, b_ref: (fa(a_ref[...]), b_ref[...]), nj, a_cols, b_cols, n_rows, tk)


def _loss_head(name, y, target, n_rows):
    d = y.shape[1]
    tm = ROW_TILE

    def body(y_ref, t_ref, dy_ref, l_ref):
        e = y_ref[...] - t_ref[...]
        dy_ref[...] = e * (1.0 / d)
        e2 = jnp.sum((e * e).reshape(tm // SUBLANES, SUBLANES, d), axis=0)
        part = e2[:, 0:128]
        for k in range(1, d // 128):
            part = part + e2[:, k * 128:(k + 1) * 128]
        _acc_out(l_ref, part * (0.5 / d), pl.program_id(0) == 0)

    return _call(name, body, (n_rows // tm,),
                 [pl.BlockSpec((tm, d), lambda i: (i, 0)), pl.BlockSpec((tm, d), lambda i: (i, 0))],
                 [pl.BlockSpec((tm, d), lambda i: (i, 0)), pl.BlockSpec((SUBLANES, 128), lambda i: (0, 0))],
                 [_sds((n_rows, d), F32), _sds((SUBLANES, 128), F32)])(y, target)


def _adamw(name, w, g, m, v):
    rows, cols = w.shape
    tr = 128 if rows % 128 == 0 else rows
    c1 = 1.0 - ADAM_B1 ** ADAM_STEP
    c2 = 1.0 - ADAM_B2 ** ADAM_STEP

    def body(w_ref, g_ref, m_ref, v_ref, d_ref, nm_ref, nv_ref):
        gg = g_ref[...]
        nm = ADAM_B1 * m_ref[...] + (1.0 - ADAM_B1) * gg
        nv = ADAM_B2 * v_ref[...] + (1.0 - ADAM_B2) * (gg * gg)
        nm_ref[...] = nm
        nv_ref[...] = nv
        d_ref[...] = -ADAM_LR * ((nm / c1) / (jnp.sqrt(nv / c2) + ADAM_EPS) + ADAM_WD * w_ref[...])

    spec = pl.BlockSpec((tr, cols), lambda i: (i, 0))
    return _call(name, body, (rows // tr,), [spec] * 4, [spec] * 3, [_sds((rows, cols), F32)] * 3)(w, g, m, v)


def _position():
    return lax.axis_index("x"), lax.axis_index("y"), lax.axis_index("c")


def _other_chips(x, y):
    return [(1 - x, y), (x, 1 - y), (1 - x, 1 - y)]


def _comm_call(name, body, n_in, out_shape, scratch):
    return pl.pallas_call(
        body, name=name, in_specs=[_HBM] * n_in, out_specs=[_HBM] * len(out_shape), out_shape=out_shape,
        scratch_shapes=list(scratch),
        compiler_params=pltpu.CompilerParams(has_side_effects=True, vmem_limit_bytes=V7X_VMEM_LIMIT))


def _n_chunks(rows, want, dtype):
    align = 16 if dtype == BF16 else 8
    n = want
    while n > 1 and (rows % n or (rows // n) % align):
        n -= 1
    return n


def _remote(src, dst, send_sem, recv_sem, device):
    return pltpu.make_async_remote_copy(src_ref=src, dst_ref=dst, send_sem=send_sem, recv_sem=recv_sem,
                                        device_id=device, device_id_type=MESH)


def _start_in_chunks(src, dst, send_sem, recv_sem, device, want=8):
    rows = src.shape[0]
    n = _n_chunks(rows, want, src.dtype)
    for i in range(n):
        part = pl.ds(i * (rows // n), rows // n)
        _remote(src.at[part], dst.at[part], send_sem, recv_sem, device).start()


def _all_gather_chips(name, shards, split):
    n = len(shards)

    def body(*refs):
        ins, outs = refs[:n], refs[n:2 * n]
        send_sems, recv_sems, local_sems = refs[2 * n:]
        x, y, c = _position()
        me = 2 * x + y
        chips = _other_chips(x, y)
        sibling = (x, y, 1 - c)
        sends, forwards = [], []
        for t in range(n):
            pltpu.make_async_copy(ins[t], outs[t].at[me], local_sems.at[t]).start()
        for t in range(n):
            r = ins[t].shape[0]
            rows = pl.ds(c * (r // 2), r // 2) if split[t] else pl.ds(0, r)
            for k, (cx, cy) in enumerate(chips):
                src, dst = ins[t].at[rows], outs[t].at[me, rows]
                _start_in_chunks(src, dst, send_sems.at[t, k], recv_sems.at[t, k], (cx, cy, c), want=4)
                sends.append(_remote(src, dst, send_sems.at[t, k], recv_sems.at[t, k], (cx, cy, c)))
        for t in range(n):
            r = ins[t].shape[0]
            rows = pl.ds(c * (r // 2), r // 2) if split[t] else pl.ds(0, r)
            for k, (cx, cy) in enumerate(chips):
                landed = outs[t].at[2 * cx + cy, rows]
                _remote(landed, landed, send_sems.at[t, k], recv_sems.at[t, k], (cx, cy, c)).wait_recv()
                if split[t]:
                    _start_in_chunks(landed, landed, send_sems.at[t, 3 + k], recv_sems.at[t, 3 + k], sibling, want=4)
                    forwards.append(_remote(landed, landed, send_sems.at[t, 3 + k], recv_sems.at[t, 3 + k], sibling))
        for t in range(n):
            if split[t]:
                r = ins[t].shape[0]
                other = pl.ds((1 - c) * (r // 2), r // 2)
                for k, (cx, cy) in enumerate(chips):
                    landed = outs[t].at[2 * cx + cy, other]
                    pltpu.make_async_remote_copy(
                        src_ref=landed, dst_ref=landed, send_sem=send_sems.at[t, 3 + k], recv_sem=recv_sems.at[t, 3 + k],
                        device_id=sibling, device_id_type=MESH).wait_recv()
        for cp in sends + forwards:
            cp.wait_send()
        for t in range(n):
            pltpu.make_async_copy(ins[t], outs[t].at[me], local_sems.at[t]).wait()

    out_shape = [_sds((N_CHIPS,) + s.shape, s.dtype) for s in shards]
    return _comm_call(name, body, n, out_shape,
                      [pltpu.SemaphoreType.DMA((n, 6)), pltpu.SemaphoreType.DMA((n, 6)), pltpu.SemaphoreType.DMA((n,))])(*shards)


def _swap_halves_with_sibling(name, blob):
    def body(b_ref, mine_ref, theirs_ref, send_sem, recv_sem, local_sem):
        x, y, c = _position()
        loc = pltpu.make_async_copy(b_ref.at[c], mine_ref, local_sem)
        loc.start()
        sibling = (x, y, 1 - c)
        for k in range(b_ref.shape[1]):
            _start_in_chunks(b_ref.at[1 - c, k], theirs_ref.at[k], send_sem, recv_sem, sibling)
        _remote(b_ref.at[1 - c], theirs_ref, send_sem, recv_sem, sibling).wait()
        loc.wait()

    half = _sds(blob.shape[1:], blob.dtype)
    return _comm_call(name, body, 1, [half, half],
                      [pltpu.SemaphoreType.DMA(()), pltpu.SemaphoreType.DMA(()), pltpu.SemaphoreType.DMA(())])(blob)


def _scatter_to_chips(name, parts):
    def body(p_ref, o_ref, send_sems, recv_sems, local_sem):
        x, y, c = _position()
        me = 2 * x + y
        loc = pltpu.make_async_copy(p_ref.at[me], o_ref.at[me], local_sem)
        loc.start()
        sends = []
        for k, (cx, cy) in enumerate(_other_chips(x, y)):
            src, dst = p_ref.at[2 * cx + cy], o_ref.at[me]
            _start_in_chunks(src, dst, send_sems.at[k], recv_sems.at[k], (cx, cy, c))
            sends.append(_remote(src, dst, send_sems.at[k], recv_sems.at[k], (cx, cy, c)))
        for k, (cx, cy) in enumerate(_other_chips(x, y)):
            landed = o_ref.at[2 * cx + cy]
            _remote(landed, landed, send_sems.at[k], recv_sems.at[k], (cx, cy, c)).wait_recv()
        for cp in sends:
            cp.wait_send()
        loc.wait()

    return _comm_call(name, body, 1, [_sds(parts.shape, parts.dtype)],
                      [pltpu.SemaphoreType.DMA((3,)), pltpu.SemaphoreType.DMA((3,)), pltpu.SemaphoreType.DMA(())])(parts)[0]


def _share_half_with_sibling(name, mine):
    def body(m_ref, o_ref, send_sem, recv_sem, local_sem):
        x, y, c = _position()
        loc = pltpu.make_async_copy(m_ref, o_ref.at[c], local_sem)
        loc.start()
        sibling = (x, y, 1 - c)
        _start_in_chunks(m_ref, o_ref.at[c], send_sem, recv_sem, sibling, want=16)
        _remote(m_ref, o_ref.at[c], send_sem, recv_sem, sibling).wait_send()
        theirs = o_ref.at[1 - c]
        _remote(theirs, theirs, send_sem, recv_sem, sibling).wait_recv()
        loc.wait()

    return _comm_call(name, body, 1, [_sds((2,) + mine.shape, mine.dtype)],
                      [pltpu.SemaphoreType.DMA(()), pltpu.SemaphoreType.DMA(()), pltpu.SemaphoreType.DMA(())])(mine)[0]


def _row_tile(rows, cap=256):
    best = rows
    for t in range(16, min(rows, cap) + 1, 16):
        if rows % t == 0:
            best = t
    return best


def _add_pairs(name, a, b, out_dtype):
    n, rows, cols = a.shape
    tr = _row_tile(rows)

    def body(a_ref, b_ref, o_ref):
        o_ref[...] = (a_ref[...].astype(F32) + b_ref[...].astype(F32)).astype(out_dtype)

    spec = pl.BlockSpec((1, tr, cols), lambda k, i: (k, i, 0))
    return _call(name, body, (n, rows // tr), [spec, spec], spec, _sds(a.shape, out_dtype))(a, b)


def _sum_slots(name, parts):
    n, rows, cols = parts.shape
    tr = _row_tile(rows)

    def body(p_ref, o_ref):
        acc = p_ref[0].astype(F32)
        for k in range(1, n):
            acc = acc + p_ref[k].astype(F32)
        o_ref[...] = acc

    return _call(name, body, (rows // tr,), [pl.BlockSpec((n, tr, cols), lambda i: (0, i, 0))],
                 pl.BlockSpec((tr, cols), lambda i: (i, 0)), _sds((rows, cols), F32))(parts)


def _all_reduce_small(name, vec):
    rows, cols = vec.shape

    def body(v_ref, o_ref, buf_ref, send_sems, recv_sems):
        x, y, c = _position()
        me = 4 * x + 2 * y + c
        buf_ref[me] = v_ref[...]
        sends = []
        for dlt in range(1, N_DEV):
            tx = 1 - x if dlt & 4 else x
            ty = 1 - y if dlt & 2 else y
            tc = 1 - c if dlt & 1 else c
            cp = pltpu.make_async_remote_copy(src_ref=v_ref, dst_ref=buf_ref.at[me], send_sem=send_sems.at[dlt - 1],
                                              recv_sem=recv_sems.at[dlt - 1], device_id=(tx, ty, tc), device_id_type=MESH)
            cp.start()
            sends.append(cp)
        for dlt in range(1, N_DEV):
            tx = 1 - x if dlt & 4 else x
            ty = 1 - y if dlt & 2 else y
            tc = 1 - c if dlt & 1 else c
            landed = buf_ref.at[4 * tx + 2 * ty + tc]
            pltpu.make_async_remote_copy(src_ref=landed, dst_ref=landed, send_sem=send_sems.at[dlt - 1],
                                         recv_sem=recv_sems.at[dlt - 1], device_id=(tx, ty, tc), device_id_type=MESH).wait_recv()
        for cp in sends:
            cp.wait_send()
        acc = buf_ref[0]
        for k in range(1, N_DEV):
            acc = acc + buf_ref[k]
        o_ref[...] = acc

    return pl.pallas_call(
        body, name=name, in_specs=[_VMEM], out_specs=_VMEM, out_shape=_sds((rows, cols), F32),
        scratch_shapes=[pltpu.VMEM((N_DEV, rows, cols), F32), pltpu.SemaphoreType.DMA((N_DEV - 1,)),
                        pltpu.SemaphoreType.DMA((N_DEV - 1,))],
        compiler_params=pltpu.CompilerParams(has_side_effects=True, vmem_limit_bytes=V7X_VMEM_LIMIT))(vec)


def _pack(arrays):
    flat = []
    for a in arrays:
        v = a.reshape(-1).astype(F32)
        flat.append(jnp.pad(v, (0, (-v.shape[0]) % 128)))
    v = jnp.concatenate(flat)
    v = jnp.pad(v, (0, (-v.shape[0]) % (128 * SUBLANES)))
    return v.reshape(-1, 128)


def _unpack(packed, shapes):
    v = packed.reshape(-1)
    out, off = [], 0
    for s in shapes:
        n = math.prod(s)
        out.append(v[off:off + n].reshape(s))
        off += n + (-n) % 128
    return out


_BIG = ("ffn1_w_gate_up", "ffn1_w_down", "ffn2_w_gate_up", "ffn2_w_down", "ssm_w_in", "ssm_w_out", "w_kv", "attn_w_q", "attn_w_o")
_SMALL = ("meta_tokens", "ffn1_norm", "mix_norm", "ffn2_norm", "ssm_lambda_re", "ssm_lambda_im", "ssm_b_re", "ssm_b_im",
          "ssm_c_re", "ssm_c_im", "ssm_log_step", "ssm_d", "kv_norm", "k_norm", "q_norm", "attn_sinks")
_ORDER = ("meta_tokens", "ffn1_norm", "ffn1_w_gate_up", "ffn1_w_down", "mix_norm", "ffn2_norm", "ffn2_w_gate_up", "ffn2_w_down",
          "ssm_w_in", "ssm_lambda_re", "ssm_lambda_im", "ssm_b_re", "ssm_b_im", "ssm_c_re", "ssm_c_im", "ssm_log_step", "ssm_d",
          "ssm_w_out", "kv_norm", "w_kv", "k_norm", "attn_w_q", "q_norm", "attn_sinks", "attn_w_o")


def _step(x, target, w, m, v):
    n_ex, seq, d = x.shape
    n_meta = w["meta_tokens"].shape[0]
    n_main = n_ex * seq
    n_all = n_main + n_ex * META_BLOCK
    n_g, n_p, n_c = w["ssm_b_re"].shape[1:]
    hd = w["k_norm"].shape[0]
    n_kv = w["w_kv"].shape[1] // (2 * hd)
    n_q = w["attn_w_q"].shape[2] // hd
    qpk = n_q // n_kv
    px, py, pc = _position()
    chip = 2 * px + py

    items, split = [], []
    for name in ("ffn1_w_gate_up", "ffn1_w_down", "ffn2_w_gate_up", "ffn2_w_down"):
        for layer in range(2):
            items.append(w[name][layer].astype(BF16))
            split.append(True)
    for name in ("ssm_w_in", "ssm_w_out", "attn_w_q", "attn_w_o"):
        items.append(w[name][0].astype(BF16))
        split.append(True)
    items.append(w["w_kv"].astype(BF16))
    split.append(True)
    items += [w["meta_tokens"], w["ssm_d"]]
    split += [False, False]
    gathered = _all_gather_chips("gather_weights", items, split)
    wgu = {("ffn1", 0): gathered[0], ("ffn1", 1): gathered[1], ("ffn2", 0): gathered[4], ("ffn2", 1): gathered[5]}
    wd = {("ffn1", 0): gathered[2], ("ffn1", 1): gathered[3], ("ffn2", 0): gathered[6], ("ffn2", 1): gathered[7]}
    wd = {key: a.reshape(-1, d) for key, a in wd.items()}
    w_in = gathered[8].reshape(d, -1)
    wout4 = gathered[9]
    w_q = gathered[10].reshape(d, -1)
    w_o = gathered[11].reshape(-1, d)
    w_kv = gathered[12].reshape(d, -1)
    meta_full = jnp.transpose(gathered[13], (1, 0, 2)).reshape(n_meta, d)
    dskip = gathered[14].reshape(1, -1)

    row1 = lambda a: a.reshape(1, -1)
    ssm_args = tuple(w[k][0] for k in ("ssm_lambda_re", "ssm_lambda_im", "ssm_log_step", "ssm_b_re", "ssm_b_im", "ssm_c_re", "ssm_c_im"))
    (bb, cb, a_re, a_im), ssm_vjp = jax.vjp(_ssm_mats, *ssm_args)
    bb16, cb16 = bb.astype(BF16), cb.astype(BF16)
    a_re_s, a_im_s = lax.stop_gradient(a_re), lax.stop_gradient(a_im)
    half = n_g * n_p // 4
    tabs_f = _scan_tables(a_re_s, a_im_s, False)
    tabs_b = _scan_tables(a_re_s, a_im_s, True)

    freqs = ROPE_THETA ** (-jnp.arange(0, hd // 2, dtype=F32) * 2.0 / hd)
    pos_main = jnp.tile(n_meta + jnp.arange(seq), n_ex)
    pos_meta = jnp.tile(jnp.maximum(jnp.arange(META_BLOCK) - (META_BLOCK - n_meta), 0), n_ex)
    ang = jnp.concatenate([pos_main, pos_meta]).astype(F32)[:, None] * freqs[None, :]
    cos = jnp.concatenate([jnp.cos(ang), jnp.cos(ang)], axis=1)
    sin_s = jnp.concatenate([-jnp.sin(ang), jnp.sin(ang)], axis=1)

    meta_block = jnp.concatenate([jnp.zeros((META_BLOCK - n_meta, d), F32), meta_full], axis=0)
    h0 = jnp.concatenate([x.reshape(n_main, d)] + [meta_block] * n_ex, axis=0)

    g = lambda name, layer: row1(w[name][layer])
    h1, gu1 = _ffn_fwd("l0_ffn1", h0, g("ffn1_norm", 0), wgu["ffn1", 0], wd["ffn1", 0], n_all)
    u, bu = _ssm_in("ssm_in", h1, g("mix_norm", 0), w_in, bb16, n_all)
    xs = _scan_fwd("ssm_scan", bu, tabs_f, n_ex, seq)
    h2, y = _ssm_out("ssm_out", xs, u, dskip, cb16, wout4, h1, n_all)
    h3, gu2 = _ffn_fwd("l0_ffn2", h2, g("ffn2_norm", 0), wgu["ffn2", 0], wd["ffn2", 0], n_all)
    kvraw, k, vv = _kv_proj("kv_proj", h3, row1(w["kv_norm"]), w_kv, row1(w["k_norm"]), cos, sin_s, n_all, n_kv, hd)
    h4, gu3 = _ffn_fwd("l1_ffn1", h3, g("ffn1_norm", 1), wgu["ffn1", 1], wd["ffn1", 1], n_main)
    qraw, q = _q_proj("q_proj", h4, g("mix_norm", 1), w_q, row1(w["q_norm"][0]), cos, sin_s, n_main, n_q, hd)
    sinks = row1(w["attn_sinks"][0])
    o, lse = _attn_fwd("attn_fwd", q, k, vv, sinks, n_ex, seq, n_meta, n_kv, qpk, hd)
    h5 = _attn_out("attn_out", o, h4, w_o, n_main)
    h6, gu4 = _ffn_fwd("l1_ffn2", h5, g("ffn2_norm", 1), wgu["ffn2", 1], wd["ffn2", 1], n_main)
    dh6, loss_tile = _loss_head("loss_head", h6, target.reshape(n_main, d), n_main)

    big, small = {}, {}
    dh5, dg_f2l1, dwgu_f2l1, dwd_f2l1 = _ffn_bwd("l1_ffn2", dh6, h5, g("ffn2_norm", 1), gu4, wgu["ffn2", 1], wd["ffn2", 1], n_main)
    do = _attn_out_bwd("attn_out_bwd", dh5, w_o, n_main)
    big["attn_w_o"] = _tn_plain("attn_dwo", o, dh5, 1, o.shape[1], d, n_main).reshape(N_CHIPS, -1, d)
    dq, dk_main, dv_main, dk_meta, dv_meta, dsinks = _attn_bwd("attn_bwd", q, k, vv, sinks, o, lse, do, n_ex, seq, n_meta, n_kv, qpk, hd)
    dqraw, dh4, dq_gain, dg_mix1 = _q_bwd("q_bwd", dq, qraw, row1(w["q_norm"][0]), cos, sin_s, w_q, h4, g("mix_norm", 1), dh5, n_main, n_q, hd)
    big["attn_w_q"] = _tn_rms("attn_dwq", h4, g("mix_norm", 1), dqraw, n_main).reshape(N_CHIPS, -1, dqraw.shape[1])
    dh3m, dg_f1l1, dwgu_f1l1, dwd_f1l1 = _ffn_bwd("l1_ffn1", dh4, h3, g("ffn1_norm", 1), gu3, wgu["ffn1", 1], wd["ffn1", 1], n_main)

    def with_meta(main, meta):
        blocks = [jnp.pad(meta[b * n_meta:(b + 1) * n_meta], ((META_BLOCK - n_meta, 0), (0, 0))) for b in range(n_ex)]
        return jnp.concatenate([main] + blocks, axis=0)

    dkvraw, dh3, dk_gain, dg_kv = _kv_bwd("kv_bwd", with_meta(dk_main, dk_meta), with_meta(dv_main, dv_meta), kvraw, row1(w["k_norm"]),
                                          cos, sin_s, w_kv, h3, row1(w["kv_norm"]), dh3m, n_all, n_main, n_kv, hd)
    big["w_kv"] = _tn_rms("kv_dw", h3, row1(w["kv_norm"]), dkvraw, n_all).reshape(N_CHIPS, -1, dkvraw.shape[1])
    dh2, dg_f2l0, dwgu_f2l0, dwd_f2l0 = _ffn_bwd("l0_ffn2", dh3, h2, g("ffn2_norm", 0), gu2, wgu["ffn2", 0], wd["ffn2", 0], n_all)

    dy, dz, gx, dd = _ssm_out_bwd("ssm_out_bwd", dh2, y, u, cb16, wout4, n_all)
    hw = y.shape[1]
    oc = wout4.shape[2]
    big["ssm_w_out"] = _tn_plain("ssm_dwout", y, dz, wout4.shape[0], hw, oc, n_all, a_fn=_gelu)
    dcb = _tn_plain("ssm_dcb", xs, dy, 4, xs.shape[1] // 4, hw // 4, n_all)
    gbu, da = _scan_bwd("ssm_scan_bwd", gx, xs, tabs_b, n_ex, seq)
    du, dh1, dg_mix0 = _ssm_in_bwd("ssm_in_bwd", gbu, dy, dskip, bb16, w_in, h1, g("mix_norm", 0), dh2, n_all)
    dbb = _tn_plain("ssm_dbb", u, gbu, 4, hw // 4, gbu.shape[1] // 4, n_all)
    big["ssm_w_in"] = _tn_rms("ssm_dwin", h1, g("mix_norm", 0), du, n_all).reshape(N_CHIPS, -1, hw)
    dh0, dg_f1l0, dwgu_f1l0, dwd_f1l0 = _ffn_bwd("l0_ffn1", dh1, h0, g("ffn1_norm", 0), gu1, wgu["ffn1", 0], wd["ffn1", 0], n_all)

    grad_x = dh0[:n_main].reshape(n_ex, seq, d)
    da_sum = jnp.sum(da, axis=(0, 1)).reshape(4, 2, half)
    d_ssm = ssm_vjp((dbb, dcb, da_sum[:, 0].reshape(-1), da_sum[:, 1].reshape(-1)))
    for key, val in zip(("ssm_lambda_re", "ssm_lambda_im", "ssm_log_step", "ssm_b_re", "ssm_b_im", "ssm_c_re", "ssm_c_im"), d_ssm):
        small[key] = val[None]
    small["meta_tokens"] = sum(dh0[n_main + META_BLOCK * (b + 1) - n_meta:n_main + META_BLOCK * (b + 1)] for b in range(n_ex))
    small["ffn1_norm"] = jnp.concatenate([dg_f1l0, dg_f1l1], axis=0)
    small["ffn2_norm"] = jnp.concatenate([dg_f2l0, dg_f2l1], axis=0)
    small["mix_norm"] = jnp.concatenate([dg_mix0, dg_mix1], axis=0)
    small["ssm_d"] = dd
    small["kv_norm"] = dg_kv.reshape(-1)
    small["k_norm"] = dk_gain.reshape(-1)
    small["q_norm"] = dq_gain
    small["attn_sinks"] = dsinks
    big["ffn1_w_gate_up"] = (dwgu_f1l0, dwgu_f1l1)
    big["ffn1_w_down"] = (dwd_f1l0, dwd_f1l1)
    big["ffn2_w_gate_up"] = (dwgu_f2l0, dwgu_f2l1)
    big["ffn2_w_down"] = (dwd_f2l0, dwd_f2l1)

    small_shapes = [small[k].shape for k in _SMALL] + [(SUBLANES * 128,)]
    reduced = _unpack(_all_reduce_small("reduce_small", _pack([small[k] for k in _SMALL] + [loss_tile])), small_shapes)
    loss = jnp.sum(reduced[-1])
    grads = dict(zip(_SMALL, reduced[:-1]))
    cols = d // N_CHIPS
    grads["meta_tokens"] = lax.dynamic_slice_in_dim(grads["meta_tokens"], chip * cols, cols, axis=1)
    dcols = hw // N_CHIPS
    grads["ssm_d"] = lax.dynamic_slice_in_dim(grads["ssm_d"], chip * dcols, dcols, axis=1)

    lanes = 1024
    pieces = []
    for name in _BIG:
        parts = big[name] if isinstance(big[name], tuple) else (big[name],)
        pieces += [p.reshape(N_CHIPS, 2, -1, lanes) for p in parts]
    blob = jnp.transpose(jnp.concatenate(pieces, axis=2), (1, 0, 2, 3)).astype(BF16)
    mine, theirs = _swap_halves_with_sibling("rs_swap", blob)
    chip_sum = _add_pairs("rs_chip_sum", mine, theirs, BF16)
    landed = _scatter_to_chips("rs_scatter", chip_sum)
    total = _sum_slots("rs_sum", landed)
    full = _share_half_with_sibling("rs_share", total)
    off = 0
    for name in _BIG:
        shard_shape = w[name].shape
        n_layers = 2 if isinstance(big[name], tuple) else 1
        per_layer = math.prod(shard_shape) // n_layers // lanes // 2
        layers = []
        for _ in range(n_layers):
            layers.append(full[:, off:off + per_layer].reshape(-1))
            off += per_layer
        grads[name] = jnp.concatenate(layers).reshape(shard_shape)

    deltas, new_m, new_v = {}, {}, {}
    for name in _BIG:
        shape = w[name].shape
        two_d = lambda a: a.reshape(-1, shape[-1])
        dl, nm, nv = _adamw("adamw_" + name, two_d(w[name]), two_d(grads[name]), two_d(m[name]), two_d(v[name]))
        deltas[name], new_m[name], new_v[name] = dl.reshape(shape), nm.reshape(shape), nv.reshape(shape)
    packed = [_pack([t[k] for k in _SMALL]) for t in (w, grads, m, v)]
    outs = _adamw("adamw_small", *packed)
    shapes = [w[k].shape for k in _SMALL]
    for tgt, arr in zip((deltas, new_m, new_v), outs):
        for k, val in zip(_SMALL, _unpack(arr, shapes)):
            tgt[k] = val
    grads = {k: grads[k].reshape(w[k].shape) for k in _ORDER}
    return (loss, grad_x, *[grads[k] for k in _ORDER], *[deltas[k] for k in _ORDER], *[new_m[k] for k in _ORDER],
            *[new_v[k] for k in _ORDER])


def kernel(x, meta_tokens, ffn1_norm, ffn1_w_gate_up, ffn1_w_down, mix_norm, ffn2_norm, ffn2_w_gate_up, ffn2_w_down, ssm_w_in, ssm_lambda_re, ssm_lambda_im, ssm_b_re, ssm_b_im, ssm_c_re, ssm_c_im, ssm_log_step, ssm_d, ssm_w_out, kv_norm, w_kv, k_norm, attn_w_q, q_norm, attn_sinks, attn_w_o, loss_target, m_meta_tokens, m_ffn1_norm, m_ffn1_w_gate_up, m_ffn1_w_down, m_mix_norm, m_ffn2_norm, m_ffn2_w_gate_up, m_ffn2_w_down, m_ssm_w_in, m_ssm_lambda_re, m_ssm_lambda_im, m_ssm_b_re, m_ssm_b_im, m_ssm_c_re, m_ssm_c_im, m_ssm_log_step, m_ssm_d, m_ssm_w_out, m_kv_norm, m_w_kv, m_k_norm, m_attn_w_q, m_q_norm, m_attn_sinks, m_attn_w_o, v_meta_tokens, v_ffn1_norm, v_ffn1_w_gate_up, v_ffn1_w_down, v_mix_norm, v_ffn2_norm, v_ffn2_w_gate_up, v_ffn2_w_down, v_ssm_w_in, v_ssm_lambda_re, v_ssm_lambda_im, v_ssm_b_re, v_ssm_b_im, v_ssm_c_re, v_ssm_c_im, v_ssm_log_step, v_ssm_d, v_ssm_w_out, v_kv_norm, v_w_kv, v_k_norm, v_attn_w_q, v_q_norm, v_attn_sinks, v_attn_w_o):
    args = locals()
    w = {k: args[k] for k in _ORDER}
    m = {k: args["m_" + k] for k in _ORDER}
    v = {k: args["v_" + k] for k in _ORDER}
    return _step(x, loss_target, w, m, v)
```

```python
import functools
import math

import jax
import jax.numpy as jnp
from jax import lax
from jax.experimental import pallas as pl
from jax.experimental.pallas import tpu as pltpu

F32 = jnp.float32
BF16 = jnp.bfloat16
MESH = pl.DeviceIdType.MESH

EPS = 1e-6
NEG_INF = -1e30
ROPE_THETA = 10000.0
WINDOW = 128
META_BLOCK = 128
ROW_TILE = 256
SUBLANES = 8
V7X_VMEM_LIMIT = 56 * 2**20
N_CHIPS = 4
N_DEV = 8

ADAM_LR = 0.001
ADAM_B1 = 0.9
ADAM_B2 = 0.999
ADAM_EPS = 1e-08
ADAM_WD = 0.01
ADAM_STEP = 10

_HBM = pl.BlockSpec(memory_space=pltpu.HBM)
_VMEM = pl.BlockSpec(memory_space=pltpu.VMEM)


def _call(name, body, grid, in_specs, out_specs, out_shape, scratch=()):
    return pl.pallas_call(
        body, name=name, grid=grid, in_specs=in_specs, out_specs=out_specs, out_shape=out_shape,
        scratch_shapes=list(scratch),
        compiler_params=pltpu.CompilerParams(dimension_semantics=("arbitrary",) * len(grid),
                                             vmem_limit_bytes=V7X_VMEM_LIMIT))


def _sds(shape, dtype):
    return jax.ShapeDtypeStruct(tuple(shape), dtype)


def _dot(a, b):
    return jnp.dot(a.astype(BF16), b.astype(BF16), preferred_element_type=F32)


def _dot_nt(a, b):
    return lax.dot_general(a.astype(BF16), b.astype(BF16), (((1,), (1,)), ((), ())), preferred_element_type=F32)


def _dot_tn(a, b):
    return lax.dot_general(a.astype(BF16), b.astype(BF16), (((0,), (0,)), ((), ())), preferred_element_type=F32)


def _rms(h, g):
    return h * lax.rsqrt(jnp.mean(h * h, axis=-1, keepdims=True) + EPS) * g


def _rms_bwd(h, g, dn):
    r = lax.rsqrt(jnp.mean(h * h, axis=-1, keepdims=True) + EPS)
    xh = h * r
    dxh = dn * g
    dg = jnp.sum(dn * xh, axis=0, keepdims=True)
    dh = r * (dxh - xh * jnp.mean(dxh * xh, axis=-1, keepdims=True))
    return dh, dg


def _sigmoid(x):
    return 0.5 * jnp.tanh(0.5 * x) + 0.5


def _gelu(y):
    k = math.sqrt(2.0 / math.pi)
    return 0.5 * y * (1.0 + jnp.tanh(k * (y + 0.044715 * y * y * y)))


def _gelu_grad(y):
    k = math.sqrt(2.0 / math.pi)
    t = jnp.tanh(k * (y + 0.044715 * y * y * y))
    return 0.5 * (1.0 + t) + 0.5 * y * (1.0 - t * t) * k * (1.0 + 3.0 * 0.044715 * y * y)


def _swap_halves(x):
    half = x.shape[-1] // 2
    return jnp.concatenate([x[:, half:], x[:, :half]], axis=1)


def _head_prep(x, gain, cos, sin_s, n_heads, hd):
    out = []
    for h in range(n_heads):
        xh = x[:, h * hd:(h + 1) * hd]
        y = xh * lax.rsqrt(jnp.mean(xh * xh, axis=-1, keepdims=True) + EPS) * gain
        out.append(y * cos + _swap_halves(y) * sin_s)
    return out


def _head_prep_bwd(x, gain, cos, sin_s, d_out, n_heads, hd):
    dxs = []
    dgain = jnp.zeros((1, hd), F32)
    for h in range(n_heads):
        xh = x[:, h * hd:(h + 1) * hd]
        do = d_out[:, h * hd:(h + 1) * hd]
        r = lax.rsqrt(jnp.mean(xh * xh, axis=-1, keepdims=True) + EPS)
        xhat = xh * r
        dy = do * cos + _swap_halves(do * sin_s)
        dgain = dgain + jnp.sum(dy * xhat, axis=0, keepdims=True)
        dxh = dy * gain
        dxs.append(r * (dxh - xhat * jnp.mean(dxh * xhat, axis=-1, keepdims=True)))
    return dxs, dgain


def _acc_out(ref, val, first):
    @pl.when(first)
    def _():
        ref[...] = jnp.zeros_like(ref)
    ref[...] += val


def _ffn_up(name, h, g, w4, n_rows):
    nj, d, fc = w4.shape
    tm = ROW_TILE

    def body(h_ref, g_ref, w_ref, o_ref, n_ref):
        n = _rms(h_ref[...], g_ref[...]).astype(BF16)
        n_ref[...] = n
        for j in range(nj):
            o_ref[:, j * fc:(j + 1) * fc] = _dot(n, w_ref[j]).astype(BF16)

    return _call(name, body, (n_rows // tm,),
                 [pl.BlockSpec((tm, d), lambda i: (i, 0)), pl.BlockSpec((1, d), lambda i: (0, 0)),
                  pl.BlockSpec((nj, d, fc), lambda i: (0, 0, 0))],
                 [pl.BlockSpec((tm, nj * fc), lambda i: (i, 0)), pl.BlockSpec((tm, d), lambda i: (i, 0))],
                 [_sds((n_rows, nj * fc), BF16), _sds((n_rows, d), BF16)])(h, g, w4)


def _ffn_down(name, gu, h, wd, n_rows):
    f, d = wd.shape
    tm = ROW_TILE

    def body(gu_ref, h_ref, w_ref, o_ref, s_ref):
        a = gu_ref[:, :f].astype(F32)
        b = gu_ref[:, f:].astype(F32)
        s = (a * _sigmoid(a) * b).astype(BF16)
        s_ref[...] = s
        o_ref[...] = h_ref[...] + 0.5 * _dot(s, w_ref[...])

    return _call(name, body, (n_rows // tm,),
                 [pl.BlockSpec((tm, 2 * f), lambda i: (i, 0)), pl.BlockSpec((tm, d), lambda i: (i, 0)),
                  pl.BlockSpec((f, d), lambda i: (0, 0))],
                 [pl.BlockSpec((tm, d), lambda i: (i, 0)), pl.BlockSpec((tm, f), lambda i: (i, 0))],
                 [_sds((n_rows, d), F32), _sds((n_rows, f), BF16)])(gu, h, wd)


def _ffn_dgu(name, dh, gu, wd, n_rows):
    f, d = wd.shape
    tm = ROW_TILE

    def body(dh_ref, gu_ref, w_ref, o_ref):
        ds = _dot_nt(0.5 * dh_ref[...], w_ref[...])
        a = gu_ref[:, :f].astype(F32)
        b = gu_ref[:, f:].astype(F32)
        sg = _sigmoid(a)
        o_ref[:, :f] = (ds * b * (sg * (1.0 + a * (1.0 - sg)))).astype(BF16)
        o_ref[:, f:] = (ds * (a * sg)).astype(BF16)

    return _call(name, body, (n_rows // tm,),
                 [pl.BlockSpec((tm, d), lambda i: (i, 0)), pl.BlockSpec((tm, 2 * f), lambda i: (i, 0)),
                  pl.BlockSpec((f, d), lambda i: (0, 0))],
                 pl.BlockSpec((tm, 2 * f), lambda i: (i, 0)), _sds((n_rows, 2 * f), BF16))(dh, gu, wd)


def _ffn_dh(name, dgu, h, g, dh, w4, n_rows):
    nj, d, fc = w4.shape
    tm = ROW_TILE

    def body(dgu_ref, h_ref, g_ref, dh_ref, w_ref, o_ref, dg_ref):
        dn = _dot_nt(dgu_ref[:, 0:fc], w_ref[0])
        for j in range(1, nj):
            dn = dn + _dot_nt(dgu_ref[:, j * fc:(j + 1) * fc], w_ref[j])
        dhn, dg = _rms_bwd(h_ref[...], g_ref[...], dn)
        o_ref[...] = dh_ref[...] + dhn
        _acc_out(dg_ref, dg, pl.program_id(0) == 0)

    return _call(name, body, (n_rows // tm,),
                 [pl.BlockSpec((tm, nj * fc), lambda i: (i, 0)), pl.BlockSpec((tm, d), lambda i: (i, 0)),
                  pl.BlockSpec((1, d), lambda i: (0, 0)), pl.BlockSpec((tm, d), lambda i: (i, 0)),
                  pl.BlockSpec((nj, d, fc), lambda i: (0, 0, 0))],
                 [pl.BlockSpec((tm, d), lambda i: (i, 0)), pl.BlockSpec((1, d), lambda i: (0, 0))],
                 [_sds((n_rows, d), F32), _sds((1, d), F32)])(dgu, h, g, dh, w4)


def _contract_tile(n_rows, cap=1024):
    best = ROW_TILE
    for t in range(ROW_TILE, cap + 1, ROW_TILE):
        if n_rows % t == 0:
            best = t
    return best


def _tn(name, operands, in_specs, prologue, nj, ma, nb, n_rows, tk):
    def body(*refs):
        o_ref = refs[-1]
        a, b = prologue(pl.program_id(0), *refs[:-1])
        _acc_out(o_ref, _dot_tn(a, b)[None], pl.program_id(1) == 0)

    return _call(name, body, (nj, n_rows // tk), in_specs, pl.BlockSpec((1, ma, nb), lambda j, k: (j, 0, 0)),
                 _sds((nj, ma, nb), F32))(*operands)


def _ffn_dwgu(name, n, dgu, nj, n_rows):
    d = n.shape[1]
    fc = dgu.shape[1] // nj
    tk = _contract_tile(n_rows)
    return _tn(name, (dgu, n),
               [pl.BlockSpec((tk, fc), lambda j, k: (k, j)), pl.BlockSpec((tk, d), lambda j, k: (k, 0))],
               lambda j, a_ref, b_ref: (a_ref[...], b_ref[...]), nj, fc, d, n_rows, tk)


def _ffn_dwd(name, s, dh, n_rows):
    f = s.shape[1]
    d = dh.shape[1]
    tk = _contract_tile(n_rows)
    return _tn(name, (s, dh),
               [pl.BlockSpec((tk, f), lambda j, k: (k, 0)), pl.BlockSpec((tk, d), lambda j, k: (k, 0))],
               lambda j, s_ref, dh_ref: (s_ref[...], 0.5 * dh_ref[...]), 1, f, d, n_rows, tk)


def _ffn_fwd(tag, h, g, w4, wd, n_rows):
    gu, n = _ffn_up(tag + "_up", h, g, w4, n_rows)
    h_out, s = _ffn_down(tag + "_down", gu, h, wd, n_rows)
    return h_out, (gu, n, s)


def _ffn_bwd(tag, dh_out, h, g, saved, w4, wd, n_rows):
    gu, n, s = saved
    nj = w4.shape[0]
    dgu = _ffn_dgu(tag + "_dgu", dh_out, gu, wd, n_rows)
    dwd = _ffn_dwd(tag + "_dwd", s, dh_out, n_rows)
    dh_in, dg = _ffn_dh(tag + "_dh", dgu, h, g, dh_out, w4, n_rows)
    dwgu = _ffn_dwgu(tag + "_dwgu", n, dgu, nj, n_rows)
    f, d = wd.shape
    return dh_in, dg, dwgu, dwd.reshape(N_CHIPS, f // N_CHIPS, d)


def _ssm_in(name, h, g, w_in, bb, n_rows):
    d, hw = w_in.shape
    nj, uc, xc = bb.shape
    tm = ROW_TILE

    def body(h_ref, g_ref, w_ref, bb_ref, u_ref, bu_ref):
        u = _dot(_rms(h_ref[...], g_ref[...]), w_ref[...])
        u_ref[...] = u
        for j in range(nj):
            bu_ref[:, j * xc:(j + 1) * xc] = _dot(u[:, j * uc:(j + 1) * uc], bb_ref[j])

    return _call(name, body, (n_rows // tm,),
                 [pl.BlockSpec((tm, d), lambda i: (i, 0)), pl.BlockSpec((1, d), lambda i: (0, 0)),
                  pl.BlockSpec((d, hw), lambda i: (0, 0)), pl.BlockSpec((nj, uc, xc), lambda i: (0, 0, 0))],
                 [pl.BlockSpec((tm, hw), lambda i: (i, 0)), pl.BlockSpec((tm, nj * xc), lambda i: (i, 0))],
                 [_sds((n_rows, hw), F32), _sds((n_rows, nj * xc), F32)])(h, g, w_in, bb)


def _cmul_add(xr, xi, ar, ai, sr, si):
    return xr + ar * sr - ai * si, xi + ar * si + ai * sr


def _scan_row_block(n_main_blocks, seq_blocks):
    return lambda b, i: jnp.where(i == 0, n_main_blocks + b, b * seq_blocks + i - 1)


def _scan_fwd(name, bu, tabs, n_ex, seq):
    n_rows, width = bu.shape
    nj = 4
    cw = width // nj
    half = cw // 2
    tq = META_BLOCK
    seq_blocks = seq // tq
    rb = _scan_row_block(n_ex * seq_blocks, seq_blocks)

    def body(bu_ref, tab_ref, x_ref, carry_ref):
        @pl.when(pl.program_id(2) == 0)
        def _():
            carry_ref[...] = jnp.zeros_like(carry_ref)
        def blk(k, c):
            t = [tab_ref[n * SUBLANES:(n + 1) * SUBLANES, :] for n in range(8)]
            r0 = pl.multiple_of(k * SUBLANES, SUBLANES)
            xr = bu_ref[pl.ds(r0, SUBLANES), 0:half]
            xi = bu_ref[pl.ds(r0, SUBLANES), half:cw]
            for s, d in enumerate((1, 2, 4)):
                xr, xi = _cmul_add(xr, xi, t[2 * s], t[2 * s + 1], pltpu.roll(xr, d, 0), pltpu.roll(xi, d, 0))
            xr, xi = _cmul_add(xr, xi, t[6], t[7], c[0], c[1])
            x_ref[pl.ds(r0, SUBLANES), 0:half] = xr
            x_ref[pl.ds(r0, SUBLANES), half:cw] = xi
            last = SUBLANES - 1
            return (jnp.broadcast_to(xr[last:last + 1, :], xr.shape), jnp.broadcast_to(xi[last:last + 1, :], xi.shape))

        c = lax.fori_loop(0, tq // SUBLANES, blk, (carry_ref[0], carry_ref[1]))
        carry_ref[0] = c[0]
        carry_ref[1] = c[1]

    return _call(name, body, (n_ex, nj, seq_blocks + 1),
                 [pl.BlockSpec((tq, cw), lambda b, j, i: (rb(b, i), j)), pl.BlockSpec((8 * SUBLANES, half), lambda b, j, i: (0, j))],
                 pl.BlockSpec((tq, cw), lambda b, j, i: (rb(b, i), j)), _sds((n_rows, width), F32),
                 scratch=[pltpu.VMEM((2, SUBLANES, half), F32)])(bu, tabs)


def _scan_bwd(name, gx, x, tabs, n_ex, seq):
    n_rows, width = gx.shape
    nj = 4
    cw = width // nj
    half = cw // 2
    tq = META_BLOCK
    seq_blocks = seq // tq
    n_steps = seq_blocks + 1
    rb = _scan_row_block(n_ex * seq_blocks, seq_blocks)
    rbr = lambda b, i: rb(b, n_steps - 1 - i)

    def body(gx_ref, x_ref, tab_ref, g_ref, da_ref, carry_ref):
        @pl.when(pl.program_id(2) == 0)
        def _():
            carry_ref[...] = jnp.zeros_like(carry_ref)
            da_ref[...] = jnp.zeros_like(da_ref)
        row = lax.broadcasted_iota(jnp.int32, (SUBLANES, half), 0)
        n_blk = tq // SUBLANES

        def blk(kk, st):
            t = [tab_ref[n * SUBLANES:(n + 1) * SUBLANES, :] for n in range(8)]
            cr, ci, dar, dai = st
            r0 = pl.multiple_of((n_blk - 1 - kk) * SUBLANES, SUBLANES)
            gr = gx_ref[pl.ds(r0, SUBLANES), 0:half]
            gi = gx_ref[pl.ds(r0, SUBLANES), half:cw]
            for s, d in enumerate((1, 2, 4)):
                gr, gi = _cmul_add(gr, gi, t[2 * s], t[2 * s + 1],
                                   pltpu.roll(gr, SUBLANES - d, 0), pltpu.roll(gi, SUBLANES - d, 0))
            gr, gi = _cmul_add(gr, gi, t[6], t[7], cr, ci)
            g_ref[pl.ds(r0, SUBLANES), 0:half] = gr.astype(BF16)
            g_ref[pl.ds(r0, SUBLANES), half:cw] = gi.astype(BF16)
            hr = jnp.where(row == SUBLANES - 1, cr, pltpu.roll(gr, SUBLANES - 1, 0))
            hi = jnp.where(row == SUBLANES - 1, ci, pltpu.roll(gi, SUBLANES - 1, 0))
            xr = x_ref[pl.ds(r0, SUBLANES), 0:half]
            xi = x_ref[pl.ds(r0, SUBLANES), half:cw]
            dar = dar + xr * hr + xi * hi
            dai = dai + xr * hi - xi * hr
            return (jnp.broadcast_to(gr[0:1, :], gr.shape), jnp.broadcast_to(gi[0:1, :], gi.shape), dar, dai)

        st = lax.fori_loop(0, n_blk, blk, (carry_ref[0], carry_ref[1], da_ref[0, :, 0:half], da_ref[0, :, half:cw]))
        carry_ref[0] = st[0]
        carry_ref[1] = st[1]
        da_ref[0, :, 0:half] = st[2]
        da_ref[0, :, half:cw] = st[3]

    return _call(name, body, (n_ex, nj, n_steps),
                 [pl.BlockSpec((tq, cw), lambda b, j, i: (rbr(b, i), j)), pl.BlockSpec((tq, cw), lambda b, j, i: (rbr(b, i), j)),
                  pl.BlockSpec((8 * SUBLANES, half), lambda b, j, i: (0, j))],
                 [pl.BlockSpec((tq, cw), lambda b, j, i: (rbr(b, i), j)), pl.BlockSpec((1, SUBLANES, cw), lambda b, j, i: (b, 0, j))],
                 [_sds((n_rows, width), BF16), _sds((n_ex, SUBLANES, width), F32)],
                 scratch=[pltpu.VMEM((2, SUBLANES, half), F32)])(gx, x, tabs)


def _ssm_z(gy, wout_ref, nj):
    return jnp.concatenate([_dot(gy, wout_ref[j]) for j in range(nj)], axis=1)


def _ssm_out(name, x, u, dskip, cb, wout4, h, n_rows):
    nj, xc, uc = cb.shape
    no, hw, oc = wout4.shape
    d = h.shape[1]
    tm = ROW_TILE

    def body(x_ref, u_ref, ds_ref, cb_ref, w_ref, h_ref, o_ref, y_ref):
        y = jnp.concatenate([_dot(x_ref[:, j * xc:(j + 1) * xc], cb_ref[j]) for j in range(nj)], axis=1)
        y = y + ds_ref[...] * u_ref[...]
        y_ref[...] = y
        z = _ssm_z(_gelu(y), w_ref, no)
        o_ref[...] = h_ref[...] + z[:, :d] * _sigmoid(z[:, d:])

    return _call(name, body, (n_rows // tm,),
                 [pl.BlockSpec((tm, nj * xc), lambda i: (i, 0)), pl.BlockSpec((tm, hw), lambda i: (i, 0)),
                  pl.BlockSpec((1, hw), lambda i: (0, 0)), pl.BlockSpec((nj, xc, uc), lambda i: (0, 0, 0)),
                  pl.BlockSpec((no, hw, oc), lambda i: (0, 0, 0)), pl.BlockSpec((tm, d), lambda i: (i, 0))],
                 [pl.BlockSpec((tm, d), lambda i: (i, 0)), pl.BlockSpec((tm, hw), lambda i: (i, 0))],
                 [_sds((n_rows, d), F32), _sds((n_rows, hw), F32)])(x, u, dskip, cb, wout4, h)


def _ssm_out_bwd(name, dh, y, u, cb, wout4, n_rows):
    nj, xc, uc = cb.shape
    no, hw, oc = wout4.shape
    d = dh.shape[1]
    tm = ROW_TILE

    def body(dh_ref, y_ref, u_ref, cb_ref, w_ref, dy_ref, dz_ref, gx_ref, dd_ref):
        y = y_ref[...]
        z = _ssm_z(_gelu(y), w_ref, no)
        za = z[:, :d]
        sg = _sigmoid(z[:, d:])
        dmix = dh_ref[...]
        dz = jnp.concatenate([dmix * sg, dmix * za * sg * (1.0 - sg)], axis=1).astype(BF16)
        dz_ref[...] = dz
        dgy = _dot_nt(dz[:, 0:oc], w_ref[0])
        for j in range(1, no):
            dgy = dgy + _dot_nt(dz[:, j * oc:(j + 1) * oc], w_ref[j])
        dy = dgy * _gelu_grad(y)
        dy_ref[...] = dy
        _acc_out(dd_ref, jnp.sum(dy * u_ref[...], axis=0, keepdims=True), pl.program_id(0) == 0)
        for j in range(nj):
            gx_ref[:, j * xc:(j + 1) * xc] = _dot_nt(dy[:, j * uc:(j + 1) * uc], cb_ref[j])

    return _call(name, body, (n_rows // tm,),
                 [pl.BlockSpec((tm, d), lambda i: (i, 0)), pl.BlockSpec((tm, hw), lambda i: (i, 0)),
                  pl.BlockSpec((tm, hw), lambda i: (i, 0)), pl.BlockSpec((nj, xc, uc), lambda i: (0, 0, 0)),
                  pl.BlockSpec((no, hw, oc), lambda i: (0, 0, 0))],
                 [pl.BlockSpec((tm, hw), lambda i: (i, 0)), pl.BlockSpec((tm, no * oc), lambda i: (i, 0)),
                  pl.BlockSpec((tm, nj * xc), lambda i: (i, 0)), pl.BlockSpec((1, hw), lambda i: (0, 0))],
                 [_sds((n_rows, hw), F32), _sds((n_rows, no * oc), BF16), _sds((n_rows, nj * xc), F32),
                  _sds((1, hw), F32)])(dh, y, u, cb, wout4)


def _ssm_in_bwd(name, gbu, dy, dskip, bb, w_in, h, g, dh, n_rows):
    nj, uc, xc = bb.shape
    d, hw = w_in.shape
    tm = ROW_TILE

    def body(gb_ref, dy_ref, ds_ref, bb_ref, w_ref, h_ref, g_ref, dh_ref, du_ref, o_ref, dg_ref):
        du = jnp.concatenate([_dot_nt(gb_ref[:, j * xc:(j + 1) * xc], bb_ref[j]) for j in range(nj)], axis=1)
        du = du + dy_ref[...] * ds_ref[...]
        du_ref[...] = du.astype(BF16)
        dhn, dg = _rms_bwd(h_ref[...], g_ref[...], _dot_nt(du, w_ref[...]))
        o_ref[...] = dh_ref[...] + dhn
        _acc_out(dg_ref, dg, pl.program_id(0) == 0)

    return _call(name, body, (n_rows // tm,),
                 [pl.BlockSpec((tm, nj * xc), lambda i: (i, 0)), pl.BlockSpec((tm, hw), lambda i: (i, 0)),
                  pl.BlockSpec((1, hw), lambda i: (0, 0)), pl.BlockSpec((nj, uc, xc), lambda i: (0, 0, 0)),
                  pl.BlockSpec((d, hw), lambda i: (0, 0)), pl.BlockSpec((tm, d), lambda i: (i, 0)),
                  pl.BlockSpec((1, d), lambda i: (0, 0)), pl.BlockSpec((tm, d), lambda i: (i, 0))],
                 [pl.BlockSpec((tm, hw), lambda i: (i, 0)), pl.BlockSpec((tm, d), lambda i: (i, 0)),
                  pl.BlockSpec((1, d), lambda i: (0, 0))],
                 [_sds((n_rows, hw), BF16), _sds((n_rows, d), F32), _sds((1, d), F32)])(gbu, dy, dskip, bb, w_in, h, g, dh)


def _discretize(lam_re, lam_im, log_step, b_re, b_im):
    step = jnp.exp(log_step)[:, None]
    mag = jnp.exp(lam_re * step)
    ar = mag * jnp.cos(lam_im * step)
    ai = mag * jnp.sin(lam_im * step)
    den = lam_re * lam_re + lam_im * lam_im
    nr, ni = ar - 1.0, ai
    cr = (nr * lam_re + ni * lam_im) / den
    ci = (ni * lam_re - nr * lam_im) / den
    bbar_r = cr[..., None] * b_re - ci[..., None] * b_im
    bbar_i = cr[..., None] * b_im + ci[..., None] * b_re
    return ar, ai, bbar_r, bbar_i


def _ssm_mats(lam_re, lam_im, log_step, b_re, b_im, c_re, c_im):
    n_g, n_p, n_c = b_re.shape
    gpc = n_g // 4
    ar, ai, bbar_r, bbar_i = _discretize(lam_re, lam_im, log_step, b_re, b_im)
    eye = jnp.eye(gpc, dtype=F32)

    def in_map(bbar):
        return jnp.einsum('jgpc,gh->jgchp', bbar.reshape(4, gpc, n_p, n_c), eye).reshape(4, gpc * n_c, gpc * n_p)

    def out_map(c):
        return jnp.einsum('jgcp,gh->jgphc', c.reshape(4, gpc, n_c, n_p), eye).reshape(4, gpc * n_p, gpc * n_c)

    bb = jnp.concatenate([in_map(bbar_r), in_map(bbar_i)], axis=2)
    cb = jnp.concatenate([out_map(c_re), -out_map(c_im)], axis=1)
    return bb, cb, ar.reshape(-1), ai.reshape(-1)


def _chunked(v, half):
    return v.reshape(v.shape[:-1] + (4, half))


def _scan_tables(ar, ai, reverse):
    if reverse:
        ai = -ai
    pr, pi = [ar], [ai]
    for _ in range(SUBLANES - 1):
        pr, pi = pr + [pr[-1] * ar - pi[-1] * ai], pi + [pr[-1] * ai + pi[-1] * ar]
    row = jnp.arange(SUBLANES)[:, None]
    tabs = []
    for d in (1, 2, 4):
        keep = (row <= SUBLANES - 1 - d) if reverse else (row >= d)
        tabs += [jnp.where(keep, pr[d - 1][None, :], 0.0), jnp.where(keep, pi[d - 1][None, :], 0.0)]
    order = list(range(SUBLANES))[::-1] if reverse else list(range(SUBLANES))
    tabs += [jnp.stack([pr[k] for k in order]), jnp.stack([pi[k] for k in order])]
    return jnp.concatenate(tabs, axis=0)


def _kv_proj(name, h, g, w_kv, k_gain, cos, sin_s, n_rows, n_kv, hd):
    d, kvw = w_kv.shape
    kw = n_kv * hd
    tm = ROW_TILE

    def body(h_ref, g_ref, w_ref, kg_ref, c_ref, s_ref, raw_ref, k_ref, v_ref):
        raw = _dot(_rms(h_ref[...], g_ref[...]), w_ref[...])
        raw_ref[...] = raw
        ks = _head_prep(raw[:, :kw], kg_ref[...], c_ref[...], s_ref[...], n_kv, hd)
        k_ref[...] = jnp.concatenate(ks, axis=1).astype(BF16)
        v_ref[...] = raw[:, kw:].astype(BF16)

    return _call(name, body, (n_rows // tm,),
                 [pl.BlockSpec((tm, d), lambda i: (i, 0)), pl.BlockSpec((1, d), lambda i: (0, 0)),
                  pl.BlockSpec((d, kvw), lambda i: (0, 0)), pl.BlockSpec((1, hd), lambda i: (0, 0)),
                  pl.BlockSpec((tm, hd), lambda i: (i, 0)), pl.BlockSpec((tm, hd), lambda i: (i, 0))],
                 [pl.BlockSpec((tm, kvw), lambda i: (i, 0)), pl.BlockSpec((tm, kw), lambda i: (i, 0)),
                  pl.BlockSpec((tm, kw), lambda i: (i, 0))],
                 [_sds((n_rows, kvw), F32), _sds((n_rows, kw), BF16), _sds((n_rows, kw), BF16)])(
                     h, g, w_kv, k_gain, cos, sin_s)


def _q_proj(name, h, g, w_q, q_gain, cos, sin_s, n_rows, n_q, hd):
    d, qw = w_q.shape
    tm = ROW_TILE

    def body(h_ref, g_ref, w_ref, qg_ref, c_ref, s_ref, raw_ref, q_ref):
        raw = _dot(_rms(h_ref[...], g_ref[...]), w_ref[...])
        raw_ref[...] = raw
        qs = _head_prep(raw, qg_ref[...], c_ref[...], s_ref[...], n_q, hd)
        q_ref[...] = jnp.concatenate(qs, axis=1).astype(BF16)

    return _call(name, body, (n_rows // tm,),
                 [pl.BlockSpec((tm, d), lambda i: (i, 0)), pl.BlockSpec((1, d), lambda i: (0, 0)),
                  pl.BlockSpec((d, qw), lambda i: (0, 0)), pl.BlockSpec((1, hd), lambda i: (0, 0)),
                  pl.BlockSpec((tm, hd), lambda i: (i, 0)), pl.BlockSpec((tm, hd), lambda i: (i, 0))],
                 [pl.BlockSpec((tm, qw), lambda i: (i, 0)), pl.BlockSpec((tm, qw), lambda i: (i, 0))],
                 [_sds((n_rows, qw), F32), _sds((n_rows, qw), BF16)])(h, g, w_q, q_gain, cos, sin_s)


def _attn_specs(seq, n_ex, n_meta, kw):
    nb = seq // WINDOW
    meta_blk = lambda b: (n_ex * seq + META_BLOCK * b + META_BLOCK - n_meta) // n_meta
    return [pl.BlockSpec((WINDOW, kw), lambda b, n: (b * nb + jnp.maximum(n - 1, 0), 0)),
            pl.BlockSpec((WINDOW, kw), lambda b, n: (b * nb + n, 0)),
            pl.BlockSpec((n_meta, kw), lambda b, n: (meta_blk(b), 0))]


def _attn_mask(n, qpk, n_keys):
    rows = qpk * WINDOW
    qi = lax.broadcasted_iota(jnp.int32, (rows, n_keys), 0) & (WINDOW - 1)
    kj = lax.broadcasted_iota(jnp.int32, (rows, n_keys), 1)
    rel = qi + WINDOW - kj
    band = (rel >= 0) & (rel < WINDOW) & ((n > 0) | (kj >= WINDOW))
    return band | (kj >= 2 * WINDOW)


def _stack_heads(ref, h, qpk, hd, dtype=None):
    parts = [ref[:, (h * qpk + gq) * hd:(h * qpk + gq + 1) * hd] for gq in range(qpk)]
    out = jnp.concatenate(parts, axis=0)
    return out if dtype is None else out.astype(dtype)


def _col(tile, c):
    lane = lax.broadcasted_iota(jnp.int32, tile.shape, 1)
    return jnp.sum(jnp.where(lane == c, tile, 0.0), axis=-1, keepdims=True)


def _put_col(col, c, n):
    lane = lax.broadcasted_iota(jnp.int32, (col.shape[0], n), 1)
    return jnp.where(lane == c, col, 0.0)


def _stack_cols(tile, h, qpk):
    return jnp.concatenate([_col(tile, h * qpk + gq) for gq in range(qpk)], axis=0)


def _sink_col(sinks, h, qpk):
    return jnp.concatenate([jnp.broadcast_to(_col(sinks, h * qpk + gq), (WINDOW, 1)) for gq in range(qpk)], axis=0)


def _attn_fwd(name, q, k, v, sinks, n_ex, seq, n_meta, n_kv, qpk, hd):
    nb = seq // WINDOW
    n_q = n_kv * qpk
    kw = n_kv * hd
    qw = n_q * hd
    n_keys = 2 * WINDOW + n_meta
    scale = hd ** -0.5

    def body(q_ref, kp_ref, kc_ref, km_ref, vp_ref, vc_ref, vm_ref, sk_ref, o_ref, lse_ref):
        valid = _attn_mask(pl.program_id(1), qpk, n_keys)
        sinks_v = sk_ref[...]
        o_parts = []
        lse_all = jnp.zeros((WINDOW, n_q), F32)
        for h in range(n_kv):
            hs = slice(h * hd, (h + 1) * hd)
            kb = jnp.concatenate([kp_ref[:, hs], kc_ref[:, hs], km_ref[:, hs]], axis=0)
            vb = jnp.concatenate([vp_ref[:, hs], vc_ref[:, hs], vm_ref[:, hs]], axis=0)
            s = jnp.where(valid, _dot_nt(_stack_heads(q_ref, h, qpk, hd), kb) * scale, NEG_INF)
            skc = _sink_col(sinks_v, h, qpk)
            m = jnp.maximum(jnp.max(s, axis=-1, keepdims=True), skc)
            p = jnp.exp(s - m)
            den = jnp.sum(p, axis=-1, keepdims=True) + jnp.exp(skc - m)
            o = _dot(p, vb) / den
            lse = m + jnp.log(den)
            for gq in range(qpk):
                o_parts.append(o[gq * WINDOW:(gq + 1) * WINDOW])
                lse_all = lse_all + _put_col(lse[gq * WINDOW:(gq + 1) * WINDOW], h * qpk + gq, n_q)
        o_ref[...] = jnp.concatenate(o_parts, axis=1)
        lse_ref[...] = lse_all

    qspec = pl.BlockSpec((WINDOW, qw), lambda b, n: (b * nb + n, 0))
    return _call(name, body, (n_ex, nb),
                 [qspec] + _attn_specs(seq, n_ex, n_meta, kw) + _attn_specs(seq, n_ex, n_meta, kw)
                 + [pl.BlockSpec((1, n_q), lambda b, n: (0, 0))],
                 [qspec, pl.BlockSpec((WINDOW, n_q), lambda b, n: (b * nb + n, 0))],
                 [_sds((n_ex * seq, qw), F32), _sds((n_ex * seq, n_q), F32)])(q, k, k, k, v, v, v, sinks)


def _attn_bwd(name, q, k, v, sinks, o, lse, do, n_ex, seq, n_meta, n_kv, qpk, hd):
    nb = seq // WINDOW
    n_q = n_kv * qpk
    kw = n_kv * hd
    qw = n_q * hd
    n_keys = 2 * WINDOW + n_meta
    scale = hd ** -0.5

    def body(q_ref, kp_ref, kc_ref, km_ref, vp_ref, vc_ref, vm_ref, sk_ref, o_ref, lse_ref, do_ref,
             dq_ref, dk_ref, dv_ref, dkm_ref, dvm_ref, dsk_ref):
        n = pl.program_id(1)

        @pl.when(n == 0)
        def _():
            dk_ref[...] = jnp.zeros_like(dk_ref)
            dv_ref[...] = jnp.zeros_like(dv_ref)
            dkm_ref[...] = jnp.zeros_like(dkm_ref)
            dvm_ref[...] = jnp.zeros_like(dvm_ref)

        @pl.when((n == 0) & (pl.program_id(0) == 0))
        def _():
            dsk_ref[...] = jnp.zeros_like(dsk_ref)

        valid = _attn_mask(n, qpk, n_keys)
        sinks_v = sk_ref[...]
        lse_v = lse_ref[...]
        dq_parts, dk_parts, dv_parts = [], [], []
        dsk = jnp.zeros((1, n_q), F32)
        for h in range(n_kv):
            hs = slice(h * hd, (h + 1) * hd)
            kb = jnp.concatenate([kp_ref[:, hs], kc_ref[:, hs], km_ref[:, hs]], axis=0)
            vb = jnp.concatenate([vp_ref[:, hs], vc_ref[:, hs], vm_ref[:, hs]], axis=0)
            qs = _stack_heads(q_ref, h, qpk, hd)
            dos = _stack_heads(do_ref, h, qpk, hd)
            delta = jnp.sum(dos * _stack_heads(o_ref, h, qpk, hd), axis=-1, keepdims=True)
            lse_c = _stack_cols(lse_v, h, qpk)
            s = jnp.where(valid, _dot_nt(qs, kb) * scale, NEG_INF)
            p = jnp.exp(s - lse_c)
            ds = p * (_dot_nt(dos, vb) - delta)
            dqs = _dot(ds, kb) * scale
            dk_parts.append(_dot_tn(ds, qs) * scale)
            dv_parts.append(_dot_tn(p, dos))
            dsink = -jnp.exp(_sink_col(sinks_v, h, qpk) - lse_c) * delta
            for gq in range(qpk):
                dq_parts.append(dqs[gq * WINDOW:(gq + 1) * WINDOW])
                dsk = dsk + _put_col(jnp.sum(dsink[gq * WINDOW:(gq + 1) * WINDOW], axis=0, keepdims=True), h * qpk + gq, n_q)
        dq_ref[...] = jnp.concatenate(dq_parts, axis=1)
        dsk_ref[...] += dsk
        dkb = jnp.concatenate(dk_parts, axis=1)
        dvb = jnp.concatenate(dv_parts, axis=1)
        prev = pl.ds(pl.multiple_of(jnp.maximum(n - 1, 0) * WINDOW, WINDOW), WINDOW)
        cur = pl.ds(pl.multiple_of(n * WINDOW, WINDOW), WINDOW)
        dk_ref[prev, :] += dkb[0:WINDOW]
        dv_ref[prev, :] += dvb[0:WINDOW]
        dk_ref[cur, :] += dkb[WINDOW:2 * WINDOW]
        dv_ref[cur, :] += dvb[WINDOW:2 * WINDOW]
        dkm_ref[...] += dkb[2 * WINDOW:]
        dvm_ref[...] += dvb[2 * WINDOW:]

    qspec = pl.BlockSpec((WINDOW, qw), lambda b, n: (b * nb + n, 0))
    exspec = pl.BlockSpec((seq, kw), lambda b, n: (b, 0))
    mspec = pl.BlockSpec((n_meta, kw), lambda b, n: (b, 0))
    return _call(name, body, (n_ex, nb),
                 [qspec] + _attn_specs(seq, n_ex, n_meta, kw) + _attn_specs(seq, n_ex, n_meta, kw)
                 + [pl.BlockSpec((1, n_q), lambda b, n: (0, 0)), qspec,
                    pl.BlockSpec((WINDOW, n_q), lambda b, n: (b * nb + n, 0)), qspec],
                 [qspec, exspec, exspec, mspec, mspec, pl.BlockSpec((1, n_q), lambda b, n: (0, 0))],
                 [_sds((n_ex * seq, qw), F32), _sds((n_ex * seq, kw), F32), _sds((n_ex * seq, kw), F32),
                  _sds((n_ex * n_meta, kw), F32), _sds((n_ex * n_meta, kw), F32), _sds((1, n_q), F32)])(
                      q, k, k, k, v, v, v, sinks, o, lse, do)


def _attn_out(name, o, h, w_o, n_rows):
    qw, d = w_o.shape
    tm = ROW_TILE

    def body(o_ref, h_ref, w_ref, out_ref):
        out_ref[...] = h_ref[...] + _dot(o_ref[...], w_ref[...])

    return _call(name, body, (n_rows // tm,),
                 [pl.BlockSpec((tm, qw), lambda i: (i, 0)), pl.BlockSpec((tm, d), lambda i: (i, 0)),
                  pl.BlockSpec((qw, d), lambda i: (0, 0))],
                 pl.BlockSpec((tm, d), lambda i: (i, 0)), _sds((n_rows, d), F32))(o, h, w_o)


def _attn_out_bwd(name, dh, w_o, n_rows):
    qw, d = w_o.shape
    tm = ROW_TILE

    def body(dh_ref, w_ref, do_ref):
        do_ref[...] = _dot_nt(dh_ref[...], w_ref[...])

    return _call(name, body, (n_rows // tm,),
                 [pl.BlockSpec((tm, d), lambda i: (i, 0)), pl.BlockSpec((qw, d), lambda i: (0, 0))],
                 pl.BlockSpec((tm, qw), lambda i: (i, 0)), _sds((n_rows, qw), F32))(dh, w_o)


def _q_bwd(name, dq, qraw, q_gain, cos, sin_s, w_q, h, g, dh, n_rows, n_q, hd):
    d, qw = w_q.shape
    tm = ROW_TILE

    def body(dq_ref, raw_ref, qg_ref, c_ref, s_ref, w_ref, h_ref, g_ref, dh_ref, draw_ref, o_ref, dqg_ref, dg_ref):
        dxs, dgain = _head_prep_bwd(raw_ref[...], qg_ref[...], c_ref[...], s_ref[...], dq_ref[...], n_q, hd)
        draw = jnp.concatenate(dxs, axis=1).astype(BF16)
        draw_ref[...] = draw
        dhn, dg = _rms_bwd(h_ref[...], g_ref[...], _dot_nt(draw, w_ref[...]))
        o_ref[...] = dh_ref[...] + dhn
        first = pl.program_id(0) == 0
        _acc_out(dqg_ref, dgain, first)
        _acc_out(dg_ref, dg, first)

    row = lambda w: pl.BlockSpec((tm, w), lambda i: (i, 0))
    one = lambda w: pl.BlockSpec((1, w), lambda i: (0, 0))
    return _call(name, body, (n_rows // tm,),
                 [row(qw), row(qw), one(hd), row(hd), row(hd), pl.BlockSpec((d, qw), lambda i: (0, 0)), row(d), one(d), row(d)],
                 [row(qw), row(d), one(hd), one(d)],
                 [_sds((n_rows, qw), BF16), _sds((n_rows, d), F32), _sds((1, hd), F32), _sds((1, d), F32)])(
                     dq, qraw, q_gain, cos, sin_s, w_q, h, g, dh)


def _kv_bwd(name, dk, dv, kvraw, k_gain, cos, sin_s, w_kv, h, g, dh_main, n_rows, n_main, n_kv, hd):
    d, kvw = w_kv.shape
    kw = n_kv * hd
    tm = ROW_TILE
    n_main_tiles = n_main // tm

    def body(dk_ref, dv_ref, raw_ref, kg_ref, c_ref, s_ref, w_ref, h_ref, g_ref, dh_ref, draw_ref, o_ref, dkg_ref, dg_ref):
        i = pl.program_id(0)
        dxs, dgain = _head_prep_bwd(raw_ref[:, :kw], kg_ref[...], c_ref[...], s_ref[...], dk_ref[...], n_kv, hd)
        draw = jnp.concatenate(dxs + [dv_ref[...]], axis=1).astype(BF16)
        draw_ref[...] = draw
        dhn, dg = _rms_bwd(h_ref[...], g_ref[...], _dot_nt(draw, w_ref[...]))
        o_ref[...] = jnp.where(i < n_main_tiles, dh_ref[...], 0.0) + dhn
        _acc_out(dkg_ref, dgain, i == 0)
        _acc_out(dg_ref, dg, i == 0)

    row = lambda w: pl.BlockSpec((tm, w), lambda i: (i, 0))
    one = lambda w: pl.BlockSpec((1, w), lambda i: (0, 0))
    return _call(name, body, (n_rows // tm,),
                 [row(kw), row(kw), row(kvw), one(hd), row(hd), row(hd), pl.BlockSpec((d, kvw), lambda i: (0, 0)), row(d),
                  one(d), pl.BlockSpec((tm, d), lambda i: (jnp.minimum(i, n_main_tiles - 1), 0))],
                 [row(kvw), row(d), one(hd), one(d)],
                 [_sds((n_rows, kvw), BF16), _sds((n_rows, d), F32), _sds((1, hd), F32), _sds((1, d), F32)])(
                     dk, dv, kvraw, k_gain, cos, sin_s, w_kv, h, g, dh_main)


def _tn_rms(name, h, g, b, n_rows):
    d = h.shape[1]
    nb = b.shape[1]
    tk = _contract_tile(n_rows)
    return _tn(name, (h, g, b),
               [pl.BlockSpec((tk, d), lambda j, k: (k, 0)), pl.BlockSpec((1, d), lambda j, k: (0, 0)),
                pl.BlockSpec((tk, nb), lambda j, k: (k, 0))],
               lambda j, h_ref, g_ref, b_ref: (_rms(h_ref[...], g_ref[...]), b_ref[...]), 1, d, nb, n_rows, tk)


def _tn_plain(name, a, b, nj, a_cols, b_cols, n_rows, a_fn=None):
    tk = _contract_tile(n_rows)
    fa = (lambda v: v) if a_fn is None else a_fn
    a_map = (lambda j, k: (k, j)) if a.shape[1] != a_cols else (lambda j, k: (k, 0))
    b_map = (lambda j, k: (k, j)) if b.shape[1] != b_cols else (lambda j, k: (k, 0))
    return _tn(name, (a, b), [pl.BlockSpec((tk, a_cols), a_map), pl.BlockSpec((tk, b_cols), b_map)],
               lambda j, a_ref, b_ref: (fa(a_ref[...]), b_ref[...]), nj, a_cols, b_cols, n_rows, tk)


def _loss_head(name, y, target, n_rows):
    d = y.shape[1]
    tm = ROW_TILE

    def body(y_ref, t_ref, dy_ref, l_ref):
        e = y_ref[...] - t_ref[...]
        dy_ref[...] = e * (1.0 / d)
        e2 = jnp.sum((e * e).reshape(tm // SUBLANES, SUBLANES, d), axis=0)
        part = e2[:, 0:128]
        for k in range(1, d // 128):
            part = part + e2[:, k * 128:(k + 1) * 128]
        _acc_out(l_ref, part * (0.5 / d), pl.program_id(0) == 0)

    return _call(name, body, (n_rows // tm,),
                 [pl.BlockSpec((tm, d), lambda i: (i, 0)), pl.BlockSpec((tm, d), lambda i: (i, 0))],
                 [pl.BlockSpec((tm, d), lambda i: (i, 0)), pl.BlockSpec((SUBLANES, 128), lambda i: (0, 0))],
                 [_sds((n_rows, d), F32), _sds((SUBLANES, 128), F32)])(y, target)


def _adamw(name, w, g, m, v):
    rows, cols = w.shape
    tr = 128 if rows % 128 == 0 else rows
    c1 = 1.0 - ADAM_B1 ** ADAM_STEP
    c2 = 1.0 - ADAM_B2 ** ADAM_STEP

    def body(w_ref, g_ref, m_ref, v_ref, d_ref, nm_ref, nv_ref):
        gg = g_ref[...]
        nm = ADAM_B1 * m_ref[...] + (1.0 - ADAM_B1) * gg
        nv = ADAM_B2 * v_ref[...] + (1.0 - ADAM_B2) * (gg * gg)
        nm_ref[...] = nm
        nv_ref[...] = nv
        d_ref[...] = -ADAM_LR * ((nm / c1) / (jnp.sqrt(nv / c2) + ADAM_EPS) + ADAM_WD * w_ref[...])

    spec = pl.BlockSpec((tr, cols), lambda i: (i, 0))
    return _call(name, body, (rows // tr,), [spec] * 4, [spec] * 3, [_sds((rows, cols), F32)] * 3)(w, g, m, v)


def _position():
    return lax.axis_index("x"), lax.axis_index("y"), lax.axis_index("c")


def _other_chips(x, y):
    return [(1 - x, y), (x, 1 - y), (1 - x, 1 - y)]


def _comm_call(name, body, n_in, out_shape, scratch):
    return pl.pallas_call(
        body, name=name, in_specs=[_HBM] * n_in, out_specs=[_HBM] * len(out_shape), out_shape=out_shape,
        scratch_shapes=list(scratch),
        compiler_params=pltpu.CompilerParams(has_side_effects=True, vmem_limit_bytes=V7X_VMEM_LIMIT))


def _n_chunks(rows, want, dtype):
    align = 16 if dtype == BF16 else 8
    n = want
    while n > 1 and (rows % n or (rows // n) % align):
        n -= 1
    return n


def _remote(src, dst, send_sem, recv_sem, device):
    return pltpu.make_async_remote_copy(src_ref=src, dst_ref=dst, send_sem=send_sem, recv_sem=recv_sem,
                                        device_id=device, device_id_type=MESH)


def _start_in_chunks(src, dst, send_sem, recv_sem, device, want=8):
    rows = src.shape[0]
    n = _n_chunks(rows, want, src.dtype)
    for i in range(n):
        part = pl.ds(i * (rows // n), rows // n)
        _remote(src.at[part], dst.at[part], send_sem, recv_sem, device).start()


def _all_gather_chips(name, shards, split):
    n = len(shards)

    def body(*refs):
        ins, outs = refs[:n], refs[n:2 * n]
        send_sems, recv_sems, local_sems = refs[2 * n:]
        x, y, c = _position()
        me = 2 * x + y
        chips = _other_chips(x, y)
        sibling = (x, y, 1 - c)
        sends, forwards = [], []
        for t in range(n):
            pltpu.make_async_copy(ins[t], outs[t].at[me], local_sems.at[t]).start()
        for t in range(n):
            r = ins[t].shape[0]
            rows = pl.ds(c * (r // 2), r // 2) if split[t] else pl.ds(0, r)
            for k, (cx, cy) in enumerate(chips):
                src, dst = ins[t].at[rows], outs[t].at[me, rows]
                _start_in_chunks(src, dst, send_sems.at[t, k], recv_sems.at[t, k], (cx, cy, c), want=4)
                sends.append(_remote(src, dst, send_sems.at[t, k], recv_sems.at[t, k], (cx, cy, c)))
        for t in range(n):
            r = ins[t].shape[0]
            rows = pl.ds(c * (r // 2), r // 2) if split[t] else pl.ds(0, r)
            for k, (cx, cy) in enumerate(chips):
                landed = outs[t].at[2 * cx + cy, rows]
                _remote(landed, landed, send_sems.at[t, k], recv_sems.at[t, k], (cx, cy, c)).wait_recv()
                if split[t]:
                    _start_in_chunks(landed, landed, send_sems.at[t, 3 + k], recv_sems.at[t, 3 + k], sibling, want=4)
                    forwards.append(_remote(landed, landed, send_sems.at[t, 3 + k], recv_sems.at[t, 3 + k], sibling))
        for t in range(n):
            if split[t]:
                r = ins[t].shape[0]
                other = pl.ds((1 - c) * (r // 2), r // 2)
                for k, (cx, cy) in enumerate(chips):
                    landed = outs[t].at[2 * cx + cy, other]
                    pltpu.make_async_remote_copy(
                        src_ref=landed, dst_ref=landed, send_sem=send_sems.at[t, 3 + k], recv_sem=recv_sems.at[t, 3 + k],
                        device_id=sibling, device_id_type=MESH).wait_recv()
        for cp in sends + forwards:
            cp.wait_send()
        for t in range(n):
            pltpu.make_async_copy(ins[t], outs[t].at[me], local_sems.at[t]).wait()

    out_shape = [_sds((N_CHIPS,) + s.shape, s.dtype) for s in shards]
    return _comm_call(name, body, n, out_shape,
                      [pltpu.SemaphoreType.DMA((n, 6)), pltpu.SemaphoreType.DMA((n, 6)), pltpu.SemaphoreType.DMA((n,))])(*shards)


def _swap_halves_with_sibling(name, blob):
    def body(b_ref, theirs_ref, send_sem, recv_sem):
        x, y, c = _position()
        sibling = (x, y, 1 - c)
        for k in range(b_ref.shape[1]):
            _start_in_chunks(b_ref.at[1 - c, k], theirs_ref.at[k], send_sem, recv_sem, sibling)
        _remote(b_ref.at[1 - c], theirs_ref, send_sem, recv_sem, sibling).wait()

    return _comm_call(name, body, 1, [_sds(blob.shape[1:], blob.dtype)],
                      [pltpu.SemaphoreType.DMA(()), pltpu.SemaphoreType.DMA(())])(blob)[0]


def _scatter_to_chips(name, parts):
    def body(p_ref, o_ref, send_sems, recv_sems, local_sems):
        x, y, c = _position()
        me = 2 * x + y
        rows = p_ref.shape[1]
        n_loc = _n_chunks(rows, 16, p_ref.dtype)
        locs = [pltpu.make_async_copy(p_ref.at[me, pl.ds(i * (rows // n_loc), rows // n_loc)],
                                      o_ref.at[me, pl.ds(i * (rows // n_loc), rows // n_loc)], local_sems.at[i])
                for i in range(n_loc)]
        for loc in locs:
            loc.start()
        sends = []
        for k, (cx, cy) in enumerate(_other_chips(x, y)):
            src, dst = p_ref.at[2 * cx + cy], o_ref.at[me]
            _start_in_chunks(src, dst, send_sems.at[k], recv_sems.at[k], (cx, cy, c))
            sends.append(_remote(src, dst, send_sems.at[k], recv_sems.at[k], (cx, cy, c)))
        for k, (cx, cy) in enumerate(_other_chips(x, y)):
            landed = o_ref.at[2 * cx + cy]
            _remote(landed, landed, send_sems.at[k], recv_sems.at[k], (cx, cy, c)).wait_recv()
        for cp in sends:
            cp.wait_send()
        for loc in locs:
            loc.wait()

    def local_sems_shape(rows):
        return pltpu.SemaphoreType.DMA((_n_chunks(rows, 16, parts.dtype),))

    return _comm_call(name, body, 1, [_sds(parts.shape, parts.dtype)],
                      [pltpu.SemaphoreType.DMA((3,)), pltpu.SemaphoreType.DMA((3,)), local_sems_shape(parts.shape[1])])(parts)[0]


def _share_with_sibling(name, mine):
    def body(m_ref, o_ref, send_sem, recv_sem):
        x, y, c = _position()
        sibling = (x, y, 1 - c)
        _start_in_chunks(m_ref, o_ref, send_sem, recv_sem, sibling, want=16)
        _remote(m_ref, o_ref, send_sem, recv_sem, sibling).wait()

    return _comm_call(name, body, 1, [_sds(mine.shape, mine.dtype)],
                      [pltpu.SemaphoreType.DMA(()), pltpu.SemaphoreType.DMA(())])(mine)[0]


def _row_tile(rows, cap=256):
    best = rows
    for t in range(16, min(rows, cap) + 1, 16):
        if rows % t == 0:
            best = t
    return best


def _add_my_half(name, blob, theirs, half_index, out_dtype):
    n, rows, cols = theirs.shape
    tr = _row_tile(rows)

    def body(c_ref, a_ref, b_ref, o_ref):
        o_ref[...] = (a_ref[0].astype(F32) + b_ref[...].astype(F32)).astype(out_dtype)

    spec = pl.BlockSpec((1, tr, cols), lambda k, i, c: (k, i, 0))
    grid_spec = pltpu.PrefetchScalarGridSpec(
        num_scalar_prefetch=1, grid=(n, rows // tr),
        in_specs=[pl.BlockSpec((1, 1, tr, cols), lambda k, i, c: (c[0], k, i, 0)), spec], out_specs=spec)
    return pl.pallas_call(
        body, name=name, grid_spec=grid_spec, out_shape=_sds(theirs.shape, out_dtype),
        compiler_params=pltpu.CompilerParams(dimension_semantics=("arbitrary", "arbitrary"),
                                             vmem_limit_bytes=V7X_VMEM_LIMIT))(half_index, blob, theirs)


def _sum_slots(name, parts):
    n, rows, cols = parts.shape
    tr = _row_tile(rows)

    def body(p_ref, o_ref):
        acc = p_ref[0].astype(F32)
        for k in range(1, n):
            acc = acc + p_ref[k].astype(F32)
        o_ref[...] = acc

    return _call(name, body, (rows // tr,), [pl.BlockSpec((n, tr, cols), lambda i: (0, i, 0))],
                 pl.BlockSpec((tr, cols), lambda i: (i, 0)), _sds((rows, cols), F32))(parts)


def _all_reduce_small(name, vec):
    rows, cols = vec.shape

    def body(v_ref, o_ref, buf_ref, send_sems, recv_sems):
        x, y, c = _position()
        me = 4 * x + 2 * y + c
        buf_ref[me] = v_ref[...]
        sends = []
        for dlt in range(1, N_DEV):
            tx = 1 - x if dlt & 4 else x
            ty = 1 - y if dlt & 2 else y
            tc = 1 - c if dlt & 1 else c
            cp = pltpu.make_async_remote_copy(src_ref=v_ref, dst_ref=buf_ref.at[me], send_sem=send_sems.at[dlt - 1],
                                              recv_sem=recv_sems.at[dlt - 1], device_id=(tx, ty, tc), device_id_type=MESH)
            cp.start()
            sends.append(cp)
        for dlt in range(1, N_DEV):
            tx = 1 - x if dlt & 4 else x
            ty = 1 - y if dlt & 2 else y
            tc = 1 - c if dlt & 1 else c
            landed = buf_ref.at[4 * tx + 2 * ty + tc]
            pltpu.make_async_remote_copy(src_ref=landed, dst_ref=landed, send_sem=send_sems.at[dlt - 1],
                                         recv_sem=recv_sems.at[dlt - 1], device_id=(tx, ty, tc), device_id_type=MESH).wait_recv()
        for cp in sends:
            cp.wait_send()
        acc = buf_ref[0]
        for k in range(1, N_DEV):
            acc = acc + buf_ref[k]
        o_ref[...] = acc

    return pl.pallas_call(
        body, name=name, in_specs=[_VMEM], out_specs=_VMEM, out_shape=_sds((rows, cols), F32),
        scratch_shapes=[pltpu.VMEM((N_DEV, rows, cols), F32), pltpu.SemaphoreType.DMA((N_DEV - 1,)),
                        pltpu.SemaphoreType.DMA((N_DEV - 1,))],
        compiler_params=pltpu.CompilerParams(has_side_effects=True, vmem_limit_bytes=V7X_VMEM_LIMIT))(vec)


def _pack(arrays):
    flat = []
    for a in arrays:
        v = a.reshape(-1).astype(F32)
        flat.append(jnp.pad(v, (0, (-v.shape[0]) % 128)))
    v = jnp.concatenate(flat)
    v = jnp.pad(v, (0, (-v.shape[0]) % (128 * SUBLANES)))
    return v.reshape(-1, 128)


def _unpack(packed, shapes):
    v = packed.reshape(-1)
    out, off = [], 0
    for s in shapes:
        n = math.prod(s)
        out.append(v[off:off + n].reshape(s))
        off += n + (-n) % 128
    return out


_BIG = ("ffn1_w_gate_up", "ffn1_w_down", "ffn2_w_gate_up", "ffn2_w_down", "ssm_w_in", "ssm_w_out", "w_kv", "attn_w_q", "attn_w_o")
_TRANSPOSED = ("ffn1_w_gate_up", "ffn2_w_gate_up")
_SMALL = ("meta_tokens", "ffn1_norm", "mix_norm", "ffn2_norm", "ssm_lambda_re", "ssm_lambda_im", "ssm_b_re", "ssm_b_im",
          "ssm_c_re", "ssm_c_im", "ssm_log_step", "ssm_d", "kv_norm", "k_norm", "q_norm", "attn_sinks")
_ORDER = ("meta_tokens", "ffn1_norm", "ffn1_w_gate_up", "ffn1_w_down", "mix_norm", "ffn2_norm", "ffn2_w_gate_up", "ffn2_w_down",
          "ssm_w_in", "ssm_lambda_re", "ssm_lambda_im", "ssm_b_re", "ssm_b_im", "ssm_c_re", "ssm_c_im", "ssm_log_step", "ssm_d",
          "ssm_w_out", "kv_norm", "w_kv", "k_norm", "attn_w_q", "q_norm", "attn_sinks", "attn_w_o")


def _step(x, target, w, m, v):
    n_ex, seq, d = x.shape
    n_meta = w["meta_tokens"].shape[0]
    n_main = n_ex * seq
    n_all = n_main + n_ex * META_BLOCK
    n_g, n_p, n_c = w["ssm_b_re"].shape[1:]
    hd = w["k_norm"].shape[0]
    n_kv = w["w_kv"].shape[1] // (2 * hd)
    n_q = w["attn_w_q"].shape[2] // hd
    qpk = n_q // n_kv
    px, py, pc = _position()
    chip = 2 * px + py

    items, split = [], []
    for name in ("ffn1_w_gate_up", "ffn1_w_down", "ffn2_w_gate_up", "ffn2_w_down"):
        for layer in range(2):
            items.append(w[name][layer].astype(BF16))
            split.append(True)
    for name in ("ssm_w_in", "ssm_w_out", "attn_w_q", "attn_w_o"):
        items.append(w[name][0].astype(BF16))
        split.append(True)
    items.append(w["w_kv"].astype(BF16))
    split.append(True)
    items += [w["meta_tokens"], w["ssm_d"]]
    split += [False, False]
    gathered = _all_gather_chips("gather_weights", items, split)
    wgu = {("ffn1", 0): gathered[0], ("ffn1", 1): gathered[1], ("ffn2", 0): gathered[4], ("ffn2", 1): gathered[5]}
    wd = {("ffn1", 0): gathered[2], ("ffn1", 1): gathered[3], ("ffn2", 0): gathered[6], ("ffn2", 1): gathered[7]}
    wd = {key: a.reshape(-1, d) for key, a in wd.items()}
    w_in = gathered[8].reshape(d, -1)
    wout4 = gathered[9]
    w_q = gathered[10].reshape(d, -1)
    w_o = gathered[11].reshape(-1, d)
    w_kv = gathered[12].reshape(d, -1)
    meta_full = jnp.transpose(gathered[13], (1, 0, 2)).reshape(n_meta, d)
    dskip = gathered[14].reshape(1, -1)

    row1 = lambda a: a.reshape(1, -1)
    ssm_args = tuple(w[k][0] for k in ("ssm_lambda_re", "ssm_lambda_im", "ssm_log_step", "ssm_b_re", "ssm_b_im", "ssm_c_re", "ssm_c_im"))
    (bb, cb, a_re, a_im), ssm_vjp = jax.vjp(_ssm_mats, *ssm_args)
    bb16, cb16 = bb.astype(BF16), cb.astype(BF16)
    a_re_s, a_im_s = lax.stop_gradient(a_re), lax.stop_gradient(a_im)
    half = n_g * n_p // 4
    tabs_f = _scan_tables(a_re_s, a_im_s, False)
    tabs_b = _scan_tables(a_re_s, a_im_s, True)

    freqs = ROPE_THETA ** (-jnp.arange(0, hd // 2, dtype=F32) * 2.0 / hd)
    pos_main = jnp.tile(n_meta + jnp.arange(seq), n_ex)
    pos_meta = jnp.tile(jnp.maximum(jnp.arange(META_BLOCK) - (META_BLOCK - n_meta), 0), n_ex)
    ang = jnp.concatenate([pos_main, pos_meta]).astype(F32)[:, None] * freqs[None, :]
    cos = jnp.concatenate([jnp.cos(ang), jnp.cos(ang)], axis=1)
    sin_s = jnp.concatenate([-jnp.sin(ang), jnp.sin(ang)], axis=1)

    meta_block = jnp.concatenate([jnp.zeros((META_BLOCK - n_meta, d), F32), meta_full], axis=0)
    h0 = jnp.concatenate([x.reshape(n_main, d)] + [meta_block] * n_ex, axis=0)

    g = lambda name, layer: row1(w[name][layer])
    h1, gu1 = _ffn_fwd("l0_ffn1", h0, g("ffn1_norm", 0), wgu["ffn1", 0], wd["ffn1", 0], n_all)
    u, bu = _ssm_in("ssm_in", h1, g("mix_norm", 0), w_in, bb16, n_all)
    xs = _scan_fwd("ssm_scan", bu, tabs_f, n_ex, seq)
    h2, y = _ssm_out("ssm_out", xs, u, dskip, cb16, wout4, h1, n_all)
    h3, gu2 = _ffn_fwd("l0_ffn2", h2, g("ffn2_norm", 0), wgu["ffn2", 0], wd["ffn2", 0], n_all)
    kvraw, k, vv = _kv_proj("kv_proj", h3, row1(w["kv_norm"]), w_kv, row1(w["k_norm"]), cos, sin_s, n_all, n_kv, hd)
    h4, gu3 = _ffn_fwd("l1_ffn1", h3, g("ffn1_norm", 1), wgu["ffn1", 1], wd["ffn1", 1], n_main)
    qraw, q = _q_proj("q_proj", h4, g("mix_norm", 1), w_q, row1(w["q_norm"][0]), cos, sin_s, n_main, n_q, hd)
    sinks = row1(w["attn_sinks"][0])
    o, lse = _attn_fwd("attn_fwd", q, k, vv, sinks, n_ex, seq, n_meta, n_kv, qpk, hd)
    h5 = _attn_out("attn_out", o, h4, w_o, n_main)
    h6, gu4 = _ffn_fwd("l1_ffn2", h5, g("ffn2_norm", 1), wgu["ffn2", 1], wd["ffn2", 1], n_main)
    dh6, loss_tile = _loss_head("loss_head", h6, target.reshape(n_main, d), n_main)

    big, small = {}, {}
    dh5, dg_f2l1, dwgu_f2l1, dwd_f2l1 = _ffn_bwd("l1_ffn2", dh6, h5, g("ffn2_norm", 1), gu4, wgu["ffn2", 1], wd["ffn2", 1], n_main)
    do = _attn_out_bwd("attn_out_bwd", dh5, w_o, n_main)
    big["attn_w_o"] = _tn_plain("attn_dwo", o, dh5, 1, o.shape[1], d, n_main).reshape(N_CHIPS, -1, d)
    dq, dk_main, dv_main, dk_meta, dv_meta, dsinks = _attn_bwd("attn_bwd", q, k, vv, sinks, o, lse, do, n_ex, seq, n_meta, n_kv, qpk, hd)
    dqraw, dh4, dq_gain, dg_mix1 = _q_bwd("q_bwd", dq, qraw, row1(w["q_norm"][0]), cos, sin_s, w_q, h4, g("mix_norm", 1), dh5, n_main, n_q, hd)
    big["attn_w_q"] = _tn_rms("attn_dwq", h4, g("mix_norm", 1), dqraw, n_main).reshape(N_CHIPS, -1, dqraw.shape[1])
    dh3m, dg_f1l1, dwgu_f1l1, dwd_f1l1 = _ffn_bwd("l1_ffn1", dh4, h3, g("ffn1_norm", 1), gu3, wgu["ffn1", 1], wd["ffn1", 1], n_main)

    def with_meta(main, meta):
        blocks = [jnp.pad(meta[b * n_meta:(b + 1) * n_meta], ((META_BLOCK - n_meta, 0), (0, 0))) for b in range(n_ex)]
        return jnp.concatenate([main] + blocks, axis=0)

    dkvraw, dh3, dk_gain, dg_kv = _kv_bwd("kv_bwd", with_meta(dk_main, dk_meta), with_meta(dv_main, dv_meta), kvraw, row1(w["k_norm"]),
                                          cos, sin_s, w_kv, h3, row1(w["kv_norm"]), dh3m, n_all, n_main, n_kv, hd)
    big["w_kv"] = _tn_rms("kv_dw", h3, row1(w["kv_norm"]), dkvraw, n_all).reshape(N_CHIPS, -1, dkvraw.shape[1])
    dh2, dg_f2l0, dwgu_f2l0, dwd_f2l0 = _ffn_bwd("l0_ffn2", dh3, h2, g("ffn2_norm", 0), gu2, wgu["ffn2", 0], wd["ffn2", 0], n_all)

    dy, dz, gx, dd = _ssm_out_bwd("ssm_out_bwd", dh2, y, u, cb16, wout4, n_all)
    hw = y.shape[1]
    oc = wout4.shape[2]
    big["ssm_w_out"] = _tn_plain("ssm_dwout", y, dz, wout4.shape[0], hw, oc, n_all, a_fn=_gelu)
    dcb = _tn_plain("ssm_dcb", xs, dy, 4, xs.shape[1] // 4, hw // 4, n_all)
    gbu, da = _scan_bwd("ssm_scan_bwd", gx, xs, tabs_b, n_ex, seq)
    du, dh1, dg_mix0 = _ssm_in_bwd("ssm_in_bwd", gbu, dy, dskip, bb16, w_in, h1, g("mix_norm", 0), dh2, n_all)
    dbb = _tn_plain("ssm_dbb", u, gbu, 4, hw // 4, gbu.shape[1] // 4, n_all)
    big["ssm_w_in"] = _tn_rms("ssm_dwin", h1, g("mix_norm", 0), du, n_all).reshape(N_CHIPS, -1, hw)
    dh0, dg_f1l0, dwgu_f1l0, dwd_f1l0 = _ffn_bwd("l0_ffn1", dh1, h0, g("ffn1_norm", 0), gu1, wgu["ffn1", 0], wd["ffn1", 0], n_all)

    grad_x = dh0[:n_main].reshape(n_ex, seq, d)
    da_sum = jnp.sum(da, axis=(0, 1)).reshape(4, 2, half)
    d_ssm = ssm_vjp((dbb, dcb, da_sum[:, 0].reshape(-1), da_sum[:, 1].reshape(-1)))
    for key, val in zip(("ssm_lambda_re", "ssm_lambda_im", "ssm_log_step", "ssm_b_re", "ssm_b_im", "ssm_c_re", "ssm_c_im"), d_ssm):
        small[key] = val[None]
    small["meta_tokens"] = sum(dh0[n_main + META_BLOCK * (b + 1) - n_meta:n_main + META_BLOCK * (b + 1)] for b in range(n_ex))
    small["ffn1_norm"] = jnp.concatenate([dg_f1l0, dg_f1l1], axis=0)
    small["ffn2_norm"] = jnp.concatenate([dg_f2l0, dg_f2l1], axis=0)
    small["mix_norm"] = jnp.concatenate([dg_mix0, dg_mix1], axis=0)
    small["ssm_d"] = dd
    small["kv_norm"] = dg_kv.reshape(-1)
    small["k_norm"] = dk_gain.reshape(-1)
    small["q_norm"] = dq_gain
    small["attn_sinks"] = dsinks
    big["ffn1_w_gate_up"] = (dwgu_f1l0, dwgu_f1l1)
    big["ffn1_w_down"] = (dwd_f1l0, dwd_f1l1)
    big["ffn2_w_gate_up"] = (dwgu_f2l0, dwgu_f2l1)
    big["ffn2_w_down"] = (dwd_f2l0, dwd_f2l1)

    small_shapes = [small[k].shape for k in _SMALL] + [(SUBLANES * 128,)]
    reduced = _unpack(_all_reduce_small("reduce_small", _pack([small[k] for k in _SMALL] + [loss_tile])), small_shapes)
    loss = jnp.sum(reduced[-1])
    grads = dict(zip(_SMALL, reduced[:-1]))
    cols = d // N_CHIPS
    grads["meta_tokens"] = lax.dynamic_slice_in_dim(grads["meta_tokens"], chip * cols, cols, axis=1)
    dcols = hw // N_CHIPS
    grads["ssm_d"] = lax.dynamic_slice_in_dim(grads["ssm_d"], chip * dcols, dcols, axis=1)

    lanes = 1024
    pieces = []
    for name in _BIG:
        parts = big[name] if isinstance(big[name], tuple) else (big[name],)
        pieces += [p.reshape(N_CHIPS, 2, -1, lanes) for p in parts]
    blob = jnp.transpose(jnp.concatenate(pieces, axis=2), (1, 0, 2, 3)).astype(BF16)
    theirs = _swap_halves_with_sibling("rs_swap", blob)
    chip_sum = _add_my_half("rs_chip_sum", blob, theirs, jnp.reshape(pc, (1,)).astype(jnp.int32), BF16)
    landed = _scatter_to_chips("rs_scatter", chip_sum)
    total = _sum_slots("rs_sum", landed)
    other = _share_with_sibling("rs_share", total)
    halves = (jnp.where(pc == 0, total, other), jnp.where(pc == 0, other, total))
    off = 0
    for name in _BIG:
        shard_shape = w[name].shape
        n_layers = 2 if isinstance(big[name], tuple) else 1
        per_layer = math.prod(shard_shape) // n_layers // lanes // 2
        layers = []
        for _ in range(n_layers):
            flat = jnp.concatenate([hv[off:off + per_layer].reshape(-1) for hv in halves])
            if name in _TRANSPOSED:
                flat = flat.reshape(shard_shape[-1], shard_shape[-2]).T
            layers.append(flat.reshape(-1))
            off += per_layer
        grads[name] = jnp.concatenate(layers).reshape(shard_shape)

    deltas, new_m, new_v = {}, {}, {}
    for name in _BIG:
        shape = w[name].shape
        two_d = lambda a: a.reshape(-1, shape[-1])
        dl, nm, nv = _adamw("adamw_" + name, two_d(w[name]), two_d(grads[name]), two_d(m[name]), two_d(v[name]))
        deltas[name], new_m[name], new_v[name] = dl.reshape(shape), nm.reshape(shape), nv.reshape(shape)
    packed = [_pack([t[k] for k in _SMALL]) for t in (w, grads, m, v)]
    outs = _adamw("adamw_small", *packed)
    shapes = [w[k].shape for k in _SMALL]
    for tgt, arr in zip((deltas, new_m, new_v), outs):
        for k, val in zip(_SMALL, _unpack(arr, shapes)):
            tgt[k] = val
    grads = {k: grads[k].reshape(w[k].shape) for k in _ORDER}
    return (loss, grad_x, *[grads[k] for k in _ORDER], *[deltas[k] for k in _ORDER], *[new_m[k] for k in _ORDER],
            *[new_v[k] for k in _ORDER])


def kernel(x, meta_tokens, ffn1_norm, ffn1_w_gate_up, ffn1_w_down, mix_norm, ffn2_norm, ffn2_w_gate_up, ffn2_w_down, ssm_w_in, ssm_lambda_re, ssm_lambda_im, ssm_b_re, ssm_b_im, ssm_c_re, ssm_c_im, ssm_log_step, ssm_d, ssm_w_out, kv_norm, w_kv, k_norm, attn_w_q, q_norm, attn_sinks, attn_w_o, loss_target, m_meta_tokens, m_ffn1_norm, m_ffn1_w_gate_up, m_ffn1_w_down, m_mix_norm, m_ffn2_norm, m_ffn2_w_gate_up, m_ffn2_w_down, m_ssm_w_in, m_ssm_lambda_re, m_ssm_lambda_im, m_ssm_b_re, m_ssm_b_im, m_ssm_c_re, m_ssm_c_im, m_ssm_log_step, m_ssm_d, m_ssm_w_out, m_kv_norm, m_w_kv, m_k_norm, m_attn_w_q, m_q_norm, m_attn_sinks, m_attn_w_o, v_meta_tokens, v_ffn1_norm, v_ffn1_w_gate_up, v_ffn1_w_down, v_mix_norm, v_ffn2_norm, v_ffn2_w_gate_up, v_ffn2_w_down, v_ssm_w_in, v_ssm_lambda_re, v_ssm_lambda_im, v_ssm_b_re, v_ssm_b_im, v_ssm_c_re, v_ssm_c_im, v_ssm_log_step, v_ssm_d, v_ssm_w_out, v_kv_norm, v_w_kv, v_k_norm, v_attn_w_q, v_q_norm, v_attn_sinks, v_attn_w_o):
    args = locals()
    w = {k: args[k] for k in _ORDER}
    m = {k: args["m_" + k] for k in _ORDER}
    v = {k: args["v_" + k] for k in _ORDER}
    return _step(x, loss_target, w, m, v)
```

```python
import functools
import math

import jax
import jax.numpy as jnp
from jax import lax
from jax.experimental import pallas as pl
from jax.experimental.pallas import tpu as pltpu
from jax.experimental.pallas import tpu_sc as plsc

F32 = jnp.float32
BF16 = jnp.bfloat16
MESH = pl.DeviceIdType.MESH

EPS = 1e-6
NEG_INF = -1e30
ROPE_THETA = 10000.0
WINDOW = 128
META_BLOCK = 128
ROW_TILE = 256
SUBLANES = 8
V7X_VMEM_LIMIT = 56 * 2**20
N_CHIPS = 4
N_DEV = 8

ADAM_LR = 0.001
ADAM_B1 = 0.9
ADAM_B2 = 0.999
ADAM_EPS = 1e-08
ADAM_WD = 0.01
ADAM_STEP = 10

_HBM = pl.BlockSpec(memory_space=pltpu.HBM)
_VMEM = pl.BlockSpec(memory_space=pltpu.VMEM)


def _call(name, body, grid, in_specs, out_specs, out_shape, scratch=()):
    return pl.pallas_call(
        body, name=name, grid=grid, in_specs=in_specs, out_specs=out_specs, out_shape=out_shape,
        scratch_shapes=list(scratch),
        compiler_params=pltpu.CompilerParams(dimension_semantics=("arbitrary",) * len(grid),
                                             vmem_limit_bytes=V7X_VMEM_LIMIT))


def _sds(shape, dtype):
    return jax.ShapeDtypeStruct(tuple(shape), dtype)


def _dot(a, b):
    return jnp.dot(a.astype(BF16), b.astype(BF16), preferred_element_type=F32)


def _dot_nt(a, b):
    return lax.dot_general(a.astype(BF16), b.astype(BF16), (((1,), (1,)), ((), ())), preferred_element_type=F32)


def _dot_tn(a, b):
    return lax.dot_general(a.astype(BF16), b.astype(BF16), (((0,), (0,)), ((), ())), preferred_element_type=F32)


def _rms(h, g):
    return h * lax.rsqrt(jnp.mean(h * h, axis=-1, keepdims=True) + EPS) * g


def _rms_bwd(h, g, dn):
    r = lax.rsqrt(jnp.mean(h * h, axis=-1, keepdims=True) + EPS)
    xh = h * r
    dxh = dn * g
    dg = jnp.sum(dn * xh, axis=0, keepdims=True)
    dh = r * (dxh - xh * jnp.mean(dxh * xh, axis=-1, keepdims=True))
    return dh, dg


def _sigmoid(x):
    return 0.5 * jnp.tanh(0.5 * x) + 0.5


def _gelu(y):
    k = math.sqrt(2.0 / math.pi)
    return 0.5 * y * (1.0 + jnp.tanh(k * (y + 0.044715 * y * y * y)))


def _gelu_grad(y):
    k = math.sqrt(2.0 / math.pi)
    t = jnp.tanh(k * (y + 0.044715 * y * y * y))
    return 0.5 * (1.0 + t) + 0.5 * y * (1.0 - t * t) * k * (1.0 + 3.0 * 0.044715 * y * y)


def _swap_halves(x):
    half = x.shape[-1] // 2
    return jnp.concatenate([x[:, half:], x[:, :half]], axis=1)


def _head_prep(x, gain, cos, sin_s, n_heads, hd):
    out = []
    for h in range(n_heads):
        xh = x[:, h * hd:(h + 1) * hd]
        y = xh * lax.rsqrt(jnp.mean(xh * xh, axis=-1, keepdims=True) + EPS) * gain
        out.append(y * cos + _swap_halves(y) * sin_s)
    return out


def _head_prep_bwd(x, gain, cos, sin_s, d_out, n_heads, hd):
    dxs = []
    dgain = jnp.zeros((1, hd), F32)
    for h in range(n_heads):
        xh = x[:, h * hd:(h + 1) * hd]
        do = d_out[:, h * hd:(h + 1) * hd]
        r = lax.rsqrt(jnp.mean(xh * xh, axis=-1, keepdims=True) + EPS)
        xhat = xh * r
        dy = do * cos + _swap_halves(do * sin_s)
        dgain = dgain + jnp.sum(dy * xhat, axis=0, keepdims=True)
        dxh = dy * gain
        dxs.append(r * (dxh - xhat * jnp.mean(dxh * xhat, axis=-1, keepdims=True)))
    return dxs, dgain


def _acc_out(ref, val, first):
    @pl.when(first)
    def _():
        ref[...] = jnp.zeros_like(ref)
    ref[...] += val


def _ffn_up(name, h, g, w4, n_rows):
    nj, d, fc = w4.shape
    tm = ROW_TILE

    def body(h_ref, g_ref, w_ref, o_ref, n_ref):
        n = _rms(h_ref[...], g_ref[...]).astype(BF16)
        n_ref[...] = n
        for j in range(nj):
            o_ref[:, j * fc:(j + 1) * fc] = _dot(n, w_ref[j]).astype(BF16)

    return _call(name, body, (n_rows // tm,),
                 [pl.BlockSpec((tm, d), lambda i: (i, 0)), pl.BlockSpec((1, d), lambda i: (0, 0)),
                  pl.BlockSpec((nj, d, fc), lambda i: (0, 0, 0))],
                 [pl.BlockSpec((tm, nj * fc), lambda i: (i, 0)), pl.BlockSpec((tm, d), lambda i: (i, 0))],
                 [_sds((n_rows, nj * fc), BF16), _sds((n_rows, d), BF16)])(h, g, w4)


def _ffn_down(name, gu, h, wd, n_rows):
    f, d = wd.shape
    tm = ROW_TILE

    def body(gu_ref, h_ref, w_ref, o_ref, s_ref):
        a = gu_ref[:, :f].astype(F32)
        b = gu_ref[:, f:].astype(F32)
        s = (a * _sigmoid(a) * b).astype(BF16)
        s_ref[...] = s
        o_ref[...] = h_ref[...] + 0.5 * _dot(s, w_ref[...])

    return _call(name, body, (n_rows // tm,),
                 [pl.BlockSpec((tm, 2 * f), lambda i: (i, 0)), pl.BlockSpec((tm, d), lambda i: (i, 0)),
                  pl.BlockSpec((f, d), lambda i: (0, 0))],
                 [pl.BlockSpec((tm, d), lambda i: (i, 0)), pl.BlockSpec((tm, f), lambda i: (i, 0))],
                 [_sds((n_rows, d), F32), _sds((n_rows, f), BF16)])(gu, h, wd)


def _ffn_dgu(name, dh, gu, wd, n_rows):
    f, d = wd.shape
    tm = ROW_TILE

    def body(dh_ref, gu_ref, w_ref, o_ref):
        ds = _dot_nt(0.5 * dh_ref[...], w_ref[...])
        a = gu_ref[:, :f].astype(F32)
        b = gu_ref[:, f:].astype(F32)
        sg = _sigmoid(a)
        o_ref[:, :f] = (ds * b * (sg * (1.0 + a * (1.0 - sg)))).astype(BF16)
        o_ref[:, f:] = (ds * (a * sg)).astype(BF16)

    return _call(name, body, (n_rows // tm,),
                 [pl.BlockSpec((tm, d), lambda i: (i, 0)), pl.BlockSpec((tm, 2 * f), lambda i: (i, 0)),
                  pl.BlockSpec((f, d), lambda i: (0, 0))],
                 pl.BlockSpec((tm, 2 * f), lambda i: (i, 0)), _sds((n_rows, 2 * f), BF16))(dh, gu, wd)


def _ffn_dh(name, dgu, h, g, dh, w4, n_rows):
    nj, d, fc = w4.shape
    tm = ROW_TILE

    def body(dgu_ref, h_ref, g_ref, dh_ref, w_ref, o_ref, dg_ref):
        dn = _dot_nt(dgu_ref[:, 0:fc], w_ref[0])
        for j in range(1, nj):
            dn = dn + _dot_nt(dgu_ref[:, j * fc:(j + 1) * fc], w_ref[j])
        dhn, dg = _rms_bwd(h_ref[...], g_ref[...], dn)
        o_ref[...] = dh_ref[...] + dhn
        _acc_out(dg_ref, dg, pl.program_id(0) == 0)

    return _call(name, body, (n_rows // tm,),
                 [pl.BlockSpec((tm, nj * fc), lambda i: (i, 0)), pl.BlockSpec((tm, d), lambda i: (i, 0)),
                  pl.BlockSpec((1, d), lambda i: (0, 0)), pl.BlockSpec((tm, d), lambda i: (i, 0)),
                  pl.BlockSpec((nj, d, fc), lambda i: (0, 0, 0))],
                 [pl.BlockSpec((tm, d), lambda i: (i, 0)), pl.BlockSpec((1, d), lambda i: (0, 0))],
                 [_sds((n_rows, d), F32), _sds((1, d), F32)])(dgu, h, g, dh, w4)


def _contract_tile(n_rows, cap=1024):
    best = ROW_TILE
    for t in range(ROW_TILE, cap + 1, ROW_TILE):
        if n_rows % t == 0:
            best = t
    return best


def _tn(name, operands, in_specs, prologue, nj, ma, nb, n_rows, tk):
    def body(*refs):
        o_ref = refs[-1]
        a, b = prologue(pl.program_id(0), *refs[:-1])
        _acc_out(o_ref, _dot_tn(a, b)[None], pl.program_id(1) == 0)

    return _call(name, body, (nj, n_rows // tk), in_specs, pl.BlockSpec((1, ma, nb), lambda j, k: (j, 0, 0)),
                 _sds((nj, ma, nb), F32))(*operands)


def _ffn_dwgu(name, n, dgu, nj, n_rows):
    d = n.shape[1]
    fc = dgu.shape[1] // nj
    tk = _contract_tile(n_rows)
    return _tn(name, (dgu, n),
               [pl.BlockSpec((tk, fc), lambda j, k: (k, j)), pl.BlockSpec((tk, d), lambda j, k: (k, 0))],
               lambda j, a_ref, b_ref: (a_ref[...], b_ref[...]), nj, fc, d, n_rows, tk)


def _ffn_dwd(name, s, dh, n_rows):
    f = s.shape[1]
    d = dh.shape[1]
    tk = _contract_tile(n_rows)
    return _tn(name, (s, dh),
               [pl.BlockSpec((tk, f), lambda j, k: (k, 0)), pl.BlockSpec((tk, d), lambda j, k: (k, 0))],
               lambda j, s_ref, dh_ref: (s_ref[...], 0.5 * dh_ref[...]), 1, f, d, n_rows, tk)


def _ffn_fwd(tag, h, g, w4, wd, n_rows):
    gu, n = _ffn_up(tag + "_up", h, g, w4, n_rows)
    h_out, s = _ffn_down(tag + "_down", gu, h, wd, n_rows)
    return h_out, (gu, n, s)


def _ffn_bwd(tag, dh_out, h, g, saved, w4, wd, n_rows):
    gu, n, s = saved
    nj = w4.shape[0]
    dgu = _ffn_dgu(tag + "_dgu", dh_out, gu, wd, n_rows)
    dwd = _ffn_dwd(tag + "_dwd", s, dh_out, n_rows)
    dh_in, dg = _ffn_dh(tag + "_dh", dgu, h, g, dh_out, w4, n_rows)
    dwgu = _ffn_dwgu(tag + "_dwgu", n, dgu, nj, n_rows)
    f, d = wd.shape
    return dh_in, dg, dwgu, dwd.reshape(N_CHIPS, f // N_CHIPS, d)


def _ssm_in(name, h, g, w_in, bb, n_rows):
    d, hw = w_in.shape
    nj, uc, xc = bb.shape
    tm = ROW_TILE

    def body(h_ref, g_ref, w_ref, bb_ref, u_ref, bu_ref):
        u = _dot(_rms(h_ref[...], g_ref[...]), w_ref[...])
        u_ref[...] = u
        for j in range(nj):
            bu_ref[:, j * xc:(j + 1) * xc] = _dot(u[:, j * uc:(j + 1) * uc], bb_ref[j])

    return _call(name, body, (n_rows // tm,),
                 [pl.BlockSpec((tm, d), lambda i: (i, 0)), pl.BlockSpec((1, d), lambda i: (0, 0)),
                  pl.BlockSpec((d, hw), lambda i: (0, 0)), pl.BlockSpec((nj, uc, xc), lambda i: (0, 0, 0))],
                 [pl.BlockSpec((tm, hw), lambda i: (i, 0)), pl.BlockSpec((tm, nj * xc), lambda i: (i, 0))],
                 [_sds((n_rows, hw), F32), _sds((n_rows, nj * xc), F32)])(h, g, w_in, bb)


def _cmul_add(xr, xi, ar, ai, sr, si):
    return xr + ar * sr - ai * si, xi + ar * si + ai * sr


def _scan_row_block(n_main_blocks, seq_blocks):
    return lambda b, i: jnp.where(i == 0, n_main_blocks + b, b * seq_blocks + i - 1)


def _scan_fwd(name, bu, tabs, n_ex, seq):
    n_rows, width = bu.shape
    nj = 4
    cw = width // nj
    half = cw // 2
    tq = META_BLOCK
    seq_blocks = seq // tq
    rb = _scan_row_block(n_ex * seq_blocks, seq_blocks)

    def body(bu_ref, tab_ref, x_ref, carry_ref):
        @pl.when(pl.program_id(2) == 0)
        def _():
            carry_ref[...] = jnp.zeros_like(carry_ref)
        def blk(k, c):
            t = [tab_ref[n * SUBLANES:(n + 1) * SUBLANES, :] for n in range(8)]
            r0 = pl.multiple_of(k * SUBLANES, SUBLANES)
            xr = bu_ref[pl.ds(r0, SUBLANES), 0:half]
            xi = bu_ref[pl.ds(r0, SUBLANES), half:cw]
            for s, d in enumerate((1, 2, 4)):
                xr, xi = _cmul_add(xr, xi, t[2 * s], t[2 * s + 1], pltpu.roll(xr, d, 0), pltpu.roll(xi, d, 0))
            xr, xi = _cmul_add(xr, xi, t[6], t[7], c[0], c[1])
            x_ref[pl.ds(r0, SUBLANES), 0:half] = xr
            x_ref[pl.ds(r0, SUBLANES), half:cw] = xi
            last = SUBLANES - 1
            return (jnp.broadcast_to(xr[last:last + 1, :], xr.shape), jnp.broadcast_to(xi[last:last + 1, :], xi.shape))

        c = lax.fori_loop(0, tq // SUBLANES, blk, (carry_ref[0], carry_ref[1]))
        carry_ref[0] = c[0]
        carry_ref[1] = c[1]

    return _call(name, body, (n_ex, nj, seq_blocks + 1),
                 [pl.BlockSpec((tq, cw), lambda b, j, i: (rb(b, i), j)), pl.BlockSpec((8 * SUBLANES, half), lambda b, j, i: (0, j))],
                 pl.BlockSpec((tq, cw), lambda b, j, i: (rb(b, i), j)), _sds((n_rows, width), F32),
                 scratch=[pltpu.VMEM((2, SUBLANES, half), F32)])(bu, tabs)


def _scan_bwd(name, gx, x, tabs, n_ex, seq):
    n_rows, width = gx.shape
    nj = 4
    cw = width // nj
    half = cw // 2
    tq = META_BLOCK
    seq_blocks = seq // tq
    n_steps = seq_blocks + 1
    rb = _scan_row_block(n_ex * seq_blocks, seq_blocks)
    rbr = lambda b, i: rb(b, n_steps - 1 - i)

    def body(gx_ref, x_ref, tab_ref, g_ref, da_ref, carry_ref):
        @pl.when(pl.program_id(2) == 0)
        def _():
            carry_ref[...] = jnp.zeros_like(carry_ref)
            da_ref[...] = jnp.zeros_like(da_ref)
        row = lax.broadcasted_iota(jnp.int32, (SUBLANES, half), 0)
        n_blk = tq // SUBLANES

        def blk(kk, st):
            t = [tab_ref[n * SUBLANES:(n + 1) * SUBLANES, :] for n in range(8)]
            cr, ci, dar, dai = st
            r0 = pl.multiple_of((n_blk - 1 - kk) * SUBLANES, SUBLANES)
            gr = gx_ref[pl.ds(r0, SUBLANES), 0:half]
            gi = gx_ref[pl.ds(r0, SUBLANES), half:cw]
            for s, d in enumerate((1, 2, 4)):
                gr, gi = _cmul_add(gr, gi, t[2 * s], t[2 * s + 1],
                                   pltpu.roll(gr, SUBLANES - d, 0), pltpu.roll(gi, SUBLANES - d, 0))
            gr, gi = _cmul_add(gr, gi, t[6], t[7], cr, ci)
            g_ref[pl.ds(r0, SUBLANES), 0:half] = gr.astype(BF16)
            g_ref[pl.ds(r0, SUBLANES), half:cw] = gi.astype(BF16)
            hr = jnp.where(row == SUBLANES - 1, cr, pltpu.roll(gr, SUBLANES - 1, 0))
            hi = jnp.where(row == SUBLANES - 1, ci, pltpu.roll(gi, SUBLANES - 1, 0))
            xr = x_ref[pl.ds(r0, SUBLANES), 0:half]
            xi = x_ref[pl.ds(r0, SUBLANES), half:cw]
            dar = dar + xr * hr + xi * hi
            dai = dai + xr * hi - xi * hr
            return (jnp.broadcast_to(gr[0:1, :], gr.shape), jnp.broadcast_to(gi[0:1, :], gi.shape), dar, dai)

        st = lax.fori_loop(0, n_blk, blk, (carry_ref[0], carry_ref[1], da_ref[0, :, 0:half], da_ref[0, :, half:cw]))
        carry_ref[0] = st[0]
        carry_ref[1] = st[1]
        da_ref[0, :, 0:half] = st[2]
        da_ref[0, :, half:cw] = st[3]

    return _call(name, body, (n_ex, nj, n_steps),
                 [pl.BlockSpec((tq, cw), lambda b, j, i: (rbr(b, i), j)), pl.BlockSpec((tq, cw), lambda b, j, i: (rbr(b, i), j)),
                  pl.BlockSpec((8 * SUBLANES, half), lambda b, j, i: (0, j))],
                 [pl.BlockSpec((tq, cw), lambda b, j, i: (rbr(b, i), j)), pl.BlockSpec((1, SUBLANES, cw), lambda b, j, i: (b, 0, j))],
                 [_sds((n_rows, width), BF16), _sds((n_ex, SUBLANES, width), F32)],
                 scratch=[pltpu.VMEM((2, SUBLANES, half), F32)])(gx, x, tabs)


def _ssm_z(gy, wout_ref, nj):
    return jnp.concatenate([_dot(gy, wout_ref[j]) for j in range(nj)], axis=1)


def _ssm_out(name, x, u, dskip, cb, wout4, h, n_rows):
    nj, xc, uc = cb.shape
    no, hw, oc = wout4.shape
    d = h.shape[1]
    tm = ROW_TILE

    def body(x_ref, u_ref, ds_ref, cb_ref, w_ref, h_ref, o_ref, y_ref):
        y = jnp.concatenate([_dot(x_ref[:, j * xc:(j + 1) * xc], cb_ref[j]) for j in range(nj)], axis=1)
        y = y + ds_ref[...] * u_ref[...]
        y_ref[...] = y
        z = _ssm_z(_gelu(y), w_ref, no)
        o_ref[...] = h_ref[...] + z[:, :d] * _sigmoid(z[:, d:])

    return _call(name, body, (n_rows // tm,),
                 [pl.BlockSpec((tm, nj * xc), lambda i: (i, 0)), pl.BlockSpec((tm, hw), lambda i: (i, 0)),
                  pl.BlockSpec((1, hw), lambda i: (0, 0)), pl.BlockSpec((nj, xc, uc), lambda i: (0, 0, 0)),
                  pl.BlockSpec((no, hw, oc), lambda i: (0, 0, 0)), pl.BlockSpec((tm, d), lambda i: (i, 0))],
                 [pl.BlockSpec((tm, d), lambda i: (i, 0)), pl.BlockSpec((tm, hw), lambda i: (i, 0))],
                 [_sds((n_rows, d), F32), _sds((n_rows, hw), F32)])(x, u, dskip, cb, wout4, h)


def _ssm_out_bwd(name, dh, y, u, cb, wout4, n_rows):
    nj, xc, uc = cb.shape
    no, hw, oc = wout4.shape
    d = dh.shape[1]
    tm = ROW_TILE

    def body(dh_ref, y_ref, u_ref, cb_ref, w_ref, dy_ref, dz_ref, gx_ref, dd_ref):
        y = y_ref[...]
        z = _ssm_z(_gelu(y), w_ref, no)
        za = z[:, :d]
        sg = _sigmoid(z[:, d:])
        dmix = dh_ref[...]
        dz = jnp.concatenate([dmix * sg, dmix * za * sg * (1.0 - sg)], axis=1).astype(BF16)
        dz_ref[...] = dz
        dgy = _dot_nt(dz[:, 0:oc], w_ref[0])
        for j in range(1, no):
            dgy = dgy + _dot_nt(dz[:, j * oc:(j + 1) * oc], w_ref[j])
        dy = dgy * _gelu_grad(y)
        dy_ref[...] = dy
        _acc_out(dd_ref, jnp.sum(dy * u_ref[...], axis=0, keepdims=True), pl.program_id(0) == 0)
        for j in range(nj):
            gx_ref[:, j * xc:(j + 1) * xc] = _dot_nt(dy[:, j * uc:(j + 1) * uc], cb_ref[j])

    return _call(name, body, (n_rows // tm,),
                 [pl.BlockSpec((tm, d), lambda i: (i, 0)), pl.BlockSpec((tm, hw), lambda i: (i, 0)),
                  pl.BlockSpec((tm, hw), lambda i: (i, 0)), pl.BlockSpec((nj, xc, uc), lambda i: (0, 0, 0)),
                  pl.BlockSpec((no, hw, oc), lambda i: (0, 0, 0))],
                 [pl.BlockSpec((tm, hw), lambda i: (i, 0)), pl.BlockSpec((tm, no * oc), lambda i: (i, 0)),
                  pl.BlockSpec((tm, nj * xc), lambda i: (i, 0)), pl.BlockSpec((1, hw), lambda i: (0, 0))],
                 [_sds((n_rows, hw), F32), _sds((n_rows, no * oc), BF16), _sds((n_rows, nj * xc), F32),
                  _sds((1, hw), F32)])(dh, y, u, cb, wout4)


def _ssm_in_bwd(name, gbu, dy, dskip, bb, w_in, h, g, dh, n_rows):
    nj, uc, xc = bb.shape
    d, hw = w_in.shape
    tm = ROW_TILE

    def body(gb_ref, dy_ref, ds_ref, bb_ref, w_ref, h_ref, g_ref, dh_ref, du_ref, o_ref, dg_ref):
        du = jnp.concatenate([_dot_nt(gb_ref[:, j * xc:(j + 1) * xc], bb_ref[j]) for j in range(nj)], axis=1)
        du = du + dy_ref[...] * ds_ref[...]
        du_ref[...] = du.astype(BF16)
        dhn, dg = _rms_bwd(h_ref[...], g_ref[...], _dot_nt(du, w_ref[...]))
        o_ref[...] = dh_ref[...] + dhn
        _acc_out(dg_ref, dg, pl.program_id(0) == 0)

    return _call(name, body, (n_rows // tm,),
                 [pl.BlockSpec((tm, nj * xc), lambda i: (i, 0)), pl.BlockSpec((tm, hw), lambda i: (i, 0)),
                  pl.BlockSpec((1, hw), lambda i: (0, 0)), pl.BlockSpec((nj, uc, xc), lambda i: (0, 0, 0)),
                  pl.BlockSpec((d, hw), lambda i: (0, 0)), pl.BlockSpec((tm, d), lambda i: (i, 0)),
                  pl.BlockSpec((1, d), lambda i: (0, 0)), pl.BlockSpec((tm, d), lambda i: (i, 0))],
                 [pl.BlockSpec((tm, hw), lambda i: (i, 0)), pl.BlockSpec((tm, d), lambda i: (i, 0)),
                  pl.BlockSpec((1, d), lambda i: (0, 0))],
                 [_sds((n_rows, hw), BF16), _sds((n_rows, d), F32), _sds((1, d), F32)])(gbu, dy, dskip, bb, w_in, h, g, dh)


def _discretize(lam_re, lam_im, log_step, b_re, b_im):
    step = jnp.exp(log_step)[:, None]
    mag = jnp.exp(lam_re * step)
    ar = mag * jnp.cos(lam_im * step)
    ai = mag * jnp.sin(lam_im * step)
    den = lam_re * lam_re + lam_im * lam_im
    nr, ni = ar - 1.0, ai
    cr = (nr * lam_re + ni * lam_im) / den
    ci = (ni * lam_re - nr * lam_im) / den
    bbar_r = cr[..., None] * b_re - ci[..., None] * b_im
    bbar_i = cr[..., None] * b_im + ci[..., None] * b_re
    return ar, ai, bbar_r, bbar_i


def _ssm_mats(lam_re, lam_im, log_step, b_re, b_im, c_re, c_im):
    n_g, n_p, n_c = b_re.shape
    gpc = n_g // 4
    ar, ai, bbar_r, bbar_i = _discretize(lam_re, lam_im, log_step, b_re, b_im)
    eye = jnp.eye(gpc, dtype=F32)

    def in_map(bbar):
        return jnp.einsum('jgpc,gh->jgchp', bbar.reshape(4, gpc, n_p, n_c), eye).reshape(4, gpc * n_c, gpc * n_p)

    def out_map(c):
        return jnp.einsum('jgcp,gh->jgphc', c.reshape(4, gpc, n_c, n_p), eye).reshape(4, gpc * n_p, gpc * n_c)

    bb = jnp.concatenate([in_map(bbar_r), in_map(bbar_i)], axis=2)
    cb = jnp.concatenate([out_map(c_re), -out_map(c_im)], axis=1)
    return bb, cb, ar.reshape(-1), ai.reshape(-1)


def _chunked(v, half):
    return v.reshape(v.shape[:-1] + (4, half))


def _scan_tables(ar, ai, reverse):
    if reverse:
        ai = -ai
    pr, pi = [ar], [ai]
    for _ in range(SUBLANES - 1):
        pr, pi = pr + [pr[-1] * ar - pi[-1] * ai], pi + [pr[-1] * ai + pi[-1] * ar]
    row = jnp.arange(SUBLANES)[:, None]
    tabs = []
    for d in (1, 2, 4):
        keep = (row <= SUBLANES - 1 - d) if reverse else (row >= d)
        tabs += [jnp.where(keep, pr[d - 1][None, :], 0.0), jnp.where(keep, pi[d - 1][None, :], 0.0)]
    order = list(range(SUBLANES))[::-1] if reverse else list(range(SUBLANES))
    tabs += [jnp.stack([pr[k] for k in order]), jnp.stack([pi[k] for k in order])]
    return jnp.concatenate(tabs, axis=0)


def _kv_proj(name, h, g, w_kv, k_gain, cos, sin_s, n_rows, n_kv, hd):
    d, kvw = w_kv.shape
    kw = n_kv * hd
    tm = ROW_TILE

    def body(h_ref, g_ref, w_ref, kg_ref, c_ref, s_ref, raw_ref, k_ref, v_ref):
        raw = _dot(_rms(h_ref[...], g_ref[...]), w_ref[...])
        raw_ref[...] = raw
        ks = _head_prep(raw[:, :kw], kg_ref[...], c_ref[...], s_ref[...], n_kv, hd)
        k_ref[...] = jnp.concatenate(ks, axis=1).astype(BF16)
        v_ref[...] = raw[:, kw:].astype(BF16)

    return _call(name, body, (n_rows // tm,),
                 [pl.BlockSpec((tm, d), lambda i: (i, 0)), pl.BlockSpec((1, d), lambda i: (0, 0)),
                  pl.BlockSpec((d, kvw), lambda i: (0, 0)), pl.BlockSpec((1, hd), lambda i: (0, 0)),
                  pl.BlockSpec((tm, hd), lambda i: (i, 0)), pl.BlockSpec((tm, hd), lambda i: (i, 0))],
                 [pl.BlockSpec((tm, kvw), lambda i: (i, 0)), pl.BlockSpec((tm, kw), lambda i: (i, 0)),
                  pl.BlockSpec((tm, kw), lambda i: (i, 0))],
                 [_sds((n_rows, kvw), F32), _sds((n_rows, kw), BF16), _sds((n_rows, kw), BF16)])(
                     h, g, w_kv, k_gain, cos, sin_s)


def _q_proj(name, h, g, w_q, q_gain, cos, sin_s, n_rows, n_q, hd):
    d, qw = w_q.shape
    tm = ROW_TILE

    def body(h_ref, g_ref, w_ref, qg_ref, c_ref, s_ref, raw_ref, q_ref):
        raw = _dot(_rms(h_ref[...], g_ref[...]), w_ref[...])
        raw_ref[...] = raw
        qs = _head_prep(raw, qg_ref[...], c_ref[...], s_ref[...], n_q, hd)
        q_ref[...] = jnp.concatenate(qs, axis=1).astype(BF16)

    return _call(name, body, (n_rows // tm,),
                 [pl.BlockSpec((tm, d), lambda i: (i, 0)), pl.BlockSpec((1, d), lambda i: (0, 0)),
                  pl.BlockSpec((d, qw), lambda i: (0, 0)), pl.BlockSpec((1, hd), lambda i: (0, 0)),
                  pl.BlockSpec((tm, hd), lambda i: (i, 0)), pl.BlockSpec((tm, hd), lambda i: (i, 0))],
                 [pl.BlockSpec((tm, qw), lambda i: (i, 0)), pl.BlockSpec((tm, qw), lambda i: (i, 0))],
                 [_sds((n_rows, qw), F32), _sds((n_rows, qw), BF16)])(h, g, w_q, q_gain, cos, sin_s)


def _attn_specs(seq, n_ex, n_meta, kw):
    nb = seq // WINDOW
    meta_blk = lambda b: (n_ex * seq + META_BLOCK * b + META_BLOCK - n_meta) // n_meta
    return [pl.BlockSpec((WINDOW, kw), lambda b, n: (b * nb + jnp.maximum(n - 1, 0), 0)),
            pl.BlockSpec((WINDOW, kw), lambda b, n: (b * nb + n, 0)),
            pl.BlockSpec((n_meta, kw), lambda b, n: (meta_blk(b), 0))]


def _attn_mask(n, qpk, n_keys):
    rows = qpk * WINDOW
    qi = lax.broadcasted_iota(jnp.int32, (rows, n_keys), 0) & (WINDOW - 1)
    kj = lax.broadcasted_iota(jnp.int32, (rows, n_keys), 1)
    rel = qi + WINDOW - kj
    band = (rel >= 0) & (rel < WINDOW) & ((n > 0) | (kj >= WINDOW))
    return band | (kj >= 2 * WINDOW)


def _stack_heads(ref, h, qpk, hd, dtype=None):
    parts = [ref[:, (h * qpk + gq) * hd:(h * qpk + gq + 1) * hd] for gq in range(qpk)]
    out = jnp.concatenate(parts, axis=0)
    return out if dtype is None else out.astype(dtype)


def _col(tile, c):
    lane = lax.broadcasted_iota(jnp.int32, tile.shape, 1)
    return jnp.sum(jnp.where(lane == c, tile, 0.0), axis=-1, keepdims=True)


def _put_col(col, c, n):
    lane = lax.broadcasted_iota(jnp.int32, (col.shape[0], n), 1)
    return jnp.where(lane == c, col, 0.0)


def _stack_cols(tile, h, qpk):
    return jnp.concatenate([_col(tile, h * qpk + gq) for gq in range(qpk)], axis=0)


def _sink_col(sinks, h, qpk):
    return jnp.concatenate([jnp.broadcast_to(_col(sinks, h * qpk + gq), (WINDOW, 1)) for gq in range(qpk)], axis=0)


def _attn_fwd(name, q, k, v, sinks, n_ex, seq, n_meta, n_kv, qpk, hd):
    nb = seq // WINDOW
    n_q = n_kv * qpk
    kw = n_kv * hd
    qw = n_q * hd
    n_keys = 2 * WINDOW + n_meta
    scale = hd ** -0.5

    def body(q_ref, kp_ref, kc_ref, km_ref, vp_ref, vc_ref, vm_ref, sk_ref, o_ref, lse_ref):
        valid = _attn_mask(pl.program_id(1), qpk, n_keys)
        sinks_v = sk_ref[...]
        o_parts = []
        lse_all = jnp.zeros((WINDOW, n_q), F32)
        for h in range(n_kv):
            hs = slice(h * hd, (h + 1) * hd)
            kb = jnp.concatenate([kp_ref[:, hs], kc_ref[:, hs], km_ref[:, hs]], axis=0)
            vb = jnp.concatenate([vp_ref[:, hs], vc_ref[:, hs], vm_ref[:, hs]], axis=0)
            s = jnp.where(valid, _dot_nt(_stack_heads(q_ref, h, qpk, hd), kb) * scale, NEG_INF)
            skc = _sink_col(sinks_v, h, qpk)
            m = jnp.maximum(jnp.max(s, axis=-1, keepdims=True), skc)
            p = jnp.exp(s - m)
            den = jnp.sum(p, axis=-1, keepdims=True) + jnp.exp(skc - m)
            o = _dot(p, vb) / den
            lse = m + jnp.log(den)
            for gq in range(qpk):
                o_parts.append(o[gq * WINDOW:(gq + 1) * WINDOW])
                lse_all = lse_all + _put_col(lse[gq * WINDOW:(gq + 1) * WINDOW], h * qpk + gq, n_q)
        o_ref[...] = jnp.concatenate(o_parts, axis=1)
        lse_ref[...] = lse_all

    qspec = pl.BlockSpec((WINDOW, qw), lambda b, n: (b * nb + n, 0))
    return _call(name, body, (n_ex, nb),
                 [qspec] + _attn_specs(seq, n_ex, n_meta, kw) + _attn_specs(seq, n_ex, n_meta, kw)
                 + [pl.BlockSpec((1, n_q), lambda b, n: (0, 0))],
                 [qspec, pl.BlockSpec((WINDOW, n_q), lambda b, n: (b * nb + n, 0))],
                 [_sds((n_ex * seq, qw), F32), _sds((n_ex * seq, n_q), F32)])(q, k, k, k, v, v, v, sinks)


def _attn_bwd(name, q, k, v, sinks, o, lse, do, n_ex, seq, n_meta, n_kv, qpk, hd):
    nb = seq // WINDOW
    n_q = n_kv * qpk
    kw = n_kv * hd
    qw = n_q * hd
    n_keys = 2 * WINDOW + n_meta
    scale = hd ** -0.5

    def body(q_ref, kp_ref, kc_ref, km_ref, vp_ref, vc_ref, vm_ref, sk_ref, o_ref, lse_ref, do_ref,
             dq_ref, dk_ref, dv_ref, dkm_ref, dvm_ref, dsk_ref):
        n = pl.program_id(1)

        @pl.when(n == 0)
        def _():
            dk_ref[...] = jnp.zeros_like(dk_ref)
            dv_ref[...] = jnp.zeros_like(dv_ref)
            dkm_ref[...] = jnp.zeros_like(dkm_ref)
            dvm_ref[...] = jnp.zeros_like(dvm_ref)

        @pl.when((n == 0) & (pl.program_id(0) == 0))
        def _():
            dsk_ref[...] = jnp.zeros_like(dsk_ref)

        valid = _attn_mask(n, qpk, n_keys)
        sinks_v = sk_ref[...]
        lse_v = lse_ref[...]
        dq_parts, dk_parts, dv_parts = [], [], []
        dsk = jnp.zeros((1, n_q), F32)
        for h in range(n_kv):
            hs = slice(h * hd, (h + 1) * hd)
            kb = jnp.concatenate([kp_ref[:, hs], kc_ref[:, hs], km_ref[:, hs]], axis=0)
            vb = jnp.concatenate([vp_ref[:, hs], vc_ref[:, hs], vm_ref[:, hs]], axis=0)
            qs = _stack_heads(q_ref, h, qpk, hd)
            dos = _stack_heads(do_ref, h, qpk, hd)
            delta = jnp.sum(dos * _stack_heads(o_ref, h, qpk, hd), axis=-1, keepdims=True)
            lse_c = _stack_cols(lse_v, h, qpk)
            s = jnp.where(valid, _dot_nt(qs, kb) * scale, NEG_INF)
            p = jnp.exp(s - lse_c)
            ds = p * (_dot_nt(dos, vb) - delta)
            dqs = _dot(ds, kb) * scale
            dk_parts.append(_dot_tn(ds, qs) * scale)
            dv_parts.append(_dot_tn(p, dos))
            dsink = -jnp.exp(_sink_col(sinks_v, h, qpk) - lse_c) * delta
            for gq in range(qpk):
                dq_parts.append(dqs[gq * WINDOW:(gq + 1) * WINDOW])
                dsk = dsk + _put_col(jnp.sum(dsink[gq * WINDOW:(gq + 1) * WINDOW], axis=0, keepdims=True), h * qpk + gq, n_q)
        dq_ref[...] = jnp.concatenate(dq_parts, axis=1)
        dsk_ref[...] += dsk
        dkb = jnp.concatenate(dk_parts, axis=1)
        dvb = jnp.concatenate(dv_parts, axis=1)
        prev = pl.ds(pl.multiple_of(jnp.maximum(n - 1, 0) * WINDOW, WINDOW), WINDOW)
        cur = pl.ds(pl.multiple_of(n * WINDOW, WINDOW), WINDOW)
        dk_ref[prev, :] += dkb[0:WINDOW]
        dv_ref[prev, :] += dvb[0:WINDOW]
        dk_ref[cur, :] += dkb[WINDOW:2 * WINDOW]
        dv_ref[cur, :] += dvb[WINDOW:2 * WINDOW]
        dkm_ref[...] += dkb[2 * WINDOW:]
        dvm_ref[...] += dvb[2 * WINDOW:]

    qspec = pl.BlockSpec((WINDOW, qw), lambda b, n: (b * nb + n, 0))
    exspec = pl.BlockSpec((seq, kw), lambda b, n: (b, 0))
    mspec = pl.BlockSpec((n_meta, kw), lambda b, n: (b, 0))
    return _call(name, body, (n_ex, nb),
                 [qspec] + _attn_specs(seq, n_ex, n_meta, kw) + _attn_specs(seq, n_ex, n_meta, kw)
                 + [pl.BlockSpec((1, n_q), lambda b, n: (0, 0)), qspec,
                    pl.BlockSpec((WINDOW, n_q), lambda b, n: (b * nb + n, 0)), qspec],
                 [qspec, exspec, exspec, mspec, mspec, pl.BlockSpec((1, n_q), lambda b, n: (0, 0))],
                 [_sds((n_ex * seq, qw), F32), _sds((n_ex * seq, kw), F32), _sds((n_ex * seq, kw), F32),
                  _sds((n_ex * n_meta, kw), F32), _sds((n_ex * n_meta, kw), F32), _sds((1, n_q), F32)])(
                      q, k, k, k, v, v, v, sinks, o, lse, do)


def _attn_out(name, o, h, w_o, n_rows):
    qw, d = w_o.shape
    tm = ROW_TILE

    def body(o_ref, h_ref, w_ref, out_ref):
        out_ref[...] = h_ref[...] + _dot(o_ref[...], w_ref[...])

    return _call(name, body, (n_rows // tm,),
                 [pl.BlockSpec((tm, qw), lambda i: (i, 0)), pl.BlockSpec((tm, d), lambda i: (i, 0)),
                  pl.BlockSpec((qw, d), lambda i: (0, 0))],
                 pl.BlockSpec((tm, d), lambda i: (i, 0)), _sds((n_rows, d), F32))(o, h, w_o)


def _attn_out_bwd(name, dh, w_o, n_rows):
    qw, d = w_o.shape
    tm = ROW_TILE

    def body(dh_ref, w_ref, do_ref):
        do_ref[...] = _dot_nt(dh_ref[...], w_ref[...])

    return _call(name, body, (n_rows // tm,),
                 [pl.BlockSpec((tm, d), lambda i: (i, 0)), pl.BlockSpec((qw, d), lambda i: (0, 0))],
                 pl.BlockSpec((tm, qw), lambda i: (i, 0)), _sds((n_rows, qw), F32))(dh, w_o)


def _q_bwd(name, dq, qraw, q_gain, cos, sin_s, w_q, h, g, dh, n_rows, n_q, hd):
    d, qw = w_q.shape
    tm = ROW_TILE

    def body(dq_ref, raw_ref, qg_ref, c_ref, s_ref, w_ref, h_ref, g_ref, dh_ref, draw_ref, o_ref, dqg_ref, dg_ref):
        dxs, dgain = _head_prep_bwd(raw_ref[...], qg_ref[...], c_ref[...], s_ref[...], dq_ref[...], n_q, hd)
        draw = jnp.concatenate(dxs, axis=1).astype(BF16)
        draw_ref[...] = draw
        dhn, dg = _rms_bwd(h_ref[...], g_ref[...], _dot_nt(draw, w_ref[...]))
        o_ref[...] = dh_ref[...] + dhn
        first = pl.program_id(0) == 0
        _acc_out(dqg_ref, dgain, first)
        _acc_out(dg_ref, dg, first)

    row = lambda w: pl.BlockSpec((tm, w), lambda i: (i, 0))
    one = lambda w: pl.BlockSpec((1, w), lambda i: (0, 0))
    return _call(name, body, (n_rows // tm,),
                 [row(qw), row(qw), one(hd), row(hd), row(hd), pl.BlockSpec((d, qw), lambda i: (0, 0)), row(d), one(d), row(d)],
                 [row(qw), row(d), one(hd), one(d)],
                 [_sds((n_rows, qw), BF16), _sds((n_rows, d), F32), _sds((1, hd), F32), _sds((1, d), F32)])(
                     dq, qraw, q_gain, cos, sin_s, w_q, h, g, dh)


def _kv_bwd(name, dk, dv, kvraw, k_gain, cos, sin_s, w_kv, h, g, dh_main, n_rows, n_main, n_kv, hd):
    d, kvw = w_kv.shape
    kw = n_kv * hd
    tm = ROW_TILE
    n_main_tiles = n_main // tm

    def body(dk_ref, dv_ref, raw_ref, kg_ref, c_ref, s_ref, w_ref, h_ref, g_ref, dh_ref, draw_ref, o_ref, dkg_ref, dg_ref):
        i = pl.program_id(0)
        dxs, dgain = _head_prep_bwd(raw_ref[:, :kw], kg_ref[...], c_ref[...], s_ref[...], dk_ref[...], n_kv, hd)
        draw = jnp.concatenate(dxs + [dv_ref[...]], axis=1).astype(BF16)
        draw_ref[...] = draw
        dhn, dg = _rms_bwd(h_ref[...], g_ref[...], _dot_nt(draw, w_ref[...]))
        o_ref[...] = jnp.where(i < n_main_tiles, dh_ref[...], 0.0) + dhn
        _acc_out(dkg_ref, dgain, i == 0)
        _acc_out(dg_ref, dg, i == 0)

    row = lambda w: pl.BlockSpec((tm, w), lambda i: (i, 0))
    one = lambda w: pl.BlockSpec((1, w), lambda i: (0, 0))
    return _call(name, body, (n_rows // tm,),
                 [row(kw), row(kw), row(kvw), one(hd), row(hd), row(hd), pl.BlockSpec((d, kvw), lambda i: (0, 0)), row(d),
                  one(d), pl.BlockSpec((tm, d), lambda i: (jnp.minimum(i, n_main_tiles - 1), 0))],
                 [row(kvw), row(d), one(hd), one(d)],
                 [_sds((n_rows, kvw), BF16), _sds((n_rows, d), F32), _sds((1, hd), F32), _sds((1, d), F32)])(
                     dk, dv, kvraw, k_gain, cos, sin_s, w_kv, h, g, dh_main)


def _tn_rms(name, h, g, b, n_rows):
    d = h.shape[1]
    nb = b.shape[1]
    tk = _contract_tile(n_rows)
    return _tn(name, (h, g, b),
               [pl.BlockSpec((tk, d), lambda j, k: (k, 0)), pl.BlockSpec((1, d), lambda j, k: (0, 0)),
                pl.BlockSpec((tk, nb), lambda j, k: (k, 0))],
               lambda j, h_ref, g_ref, b_ref: (_rms(h_ref[...], g_ref[...]), b_ref[...]), 1, d, nb, n_rows, tk)


def _tn_plain(name, a, b, nj, a_cols, b_cols, n_rows, a_fn=None):
    tk = _contract_tile(n_rows)
    fa = (lambda v: v) if a_fn is None else a_fn
    a_map = (lambda j, k: (k, j)) if a.shape[1] != a_cols else (lambda j, k: (k, 0))
    b_map = (lambda j, k: (k, j)) if b.shape[1] != b_cols else (lambda j, k: (k, 0))
    return _tn(name, (a, b), [pl.BlockSpec((tk, a_cols), a_map), pl.BlockSpec((tk, b_cols), b_map)],
               lambda j, a_ref, b_ref: (fa(a_ref[...]), b_ref[...]), nj, a_cols, b_cols, n_rows, tk)


def _loss_head(name, y, target, n_rows):
    d = y.shape[1]
    tm = ROW_TILE

    def body(y_ref, t_ref, dy_ref, l_ref):
        e = y_ref[...] - t_ref[...]
        dy_ref[...] = e * (1.0 / d)
        e2 = jnp.sum((e * e).reshape(tm // SUBLANES, SUBLANES, d), axis=0)
        part = e2[:, 0:128]
        for k in range(1, d // 128):
            part = part + e2[:, k * 128:(k + 1) * 128]
        _acc_out(l_ref, part * (0.5 / d), pl.program_id(0) == 0)

    return _call(name, body, (n_rows // tm,),
                 [pl.BlockSpec((tm, d), lambda i: (i, 0)), pl.BlockSpec((tm, d), lambda i: (i, 0))],
                 [pl.BlockSpec((tm, d), lambda i: (i, 0)), pl.BlockSpec((SUBLANES, 128), lambda i: (0, 0))],
                 [_sds((n_rows, d), F32), _sds((SUBLANES, 128), F32)])(y, target)


def _adamw_math(w, g, m, v):
    c1 = 1.0 - ADAM_B1 ** ADAM_STEP
    c2 = 1.0 - ADAM_B2 ** ADAM_STEP
    nm = ADAM_B1 * m + (1.0 - ADAM_B1) * g
    nv = ADAM_B2 * v + (1.0 - ADAM_B2) * (g * g)
    return -ADAM_LR * ((nm / c1) / (jnp.sqrt(nv / c2) + ADAM_EPS) + ADAM_WD * w), nm, nv


def _adamw(name, w, g, m, v):
    rows, cols = w.shape
    tr = 128 if rows % 128 == 0 else rows

    def body(w_ref, g_ref, m_ref, v_ref, d_ref, nm_ref, nv_ref):
        d_ref[...], nm_ref[...], nv_ref[...] = _adamw_math(w_ref[...], g_ref[...], m_ref[...], v_ref[...])

    spec = pl.BlockSpec((tr, cols), lambda i: (i, 0))
    return _call(name, body, (rows // tr,), [spec] * 4, [spec] * 3, [_sds((rows, cols), F32)] * 3)(w, g, m, v)


def _position():
    return lax.axis_index("x"), lax.axis_index("y"), lax.axis_index("c")


def _other_chips(x, y):
    return [(1 - x, y), (x, 1 - y), (1 - x, 1 - y)]


def _peers_chips(x, y, c):
    return [(cx, cy, c) for cx, cy in _other_chips(x, y)]


def _peers_sibling(x, y, c):
    return [(x, y, 1 - c)]


def _peers_chips_and_sibling(x, y, c):
    return _peers_chips(x, y, c) + _peers_sibling(x, y, c)


def _comm_call(name, body, n_in, out_shape, scratch, sequencer=None):
    if sequencer is None:
        return pl.pallas_call(
            body, name=name, in_specs=[_HBM] * n_in, out_specs=[_HBM] * len(out_shape), out_shape=out_shape,
            scratch_shapes=list(scratch),
            compiler_params=pltpu.CompilerParams(has_side_effects=True, vmem_limit_bytes=V7X_VMEM_LIMIT))
    collective_id, peers = sequencer

    def seq_body(*refs):
        barrier = pltpu.get_barrier_semaphore()
        plist = peers(*_position())
        for peer in plist:
            pl.semaphore_signal(barrier, inc=1, device_id=peer, device_id_type=MESH)
        pl.semaphore_wait(barrier, len(plist))
        body(*refs)

    return pl.kernel(seq_body, out_type=out_shape, mesh=plsc.ScalarSubcoreMesh(axis_name="sequencer", num_cores=1), name=name,
                     scratch_types=list(scratch), compiler_params=pltpu.CompilerParams(collective_id=collective_id))


def _n_chunks(rows, want, dtype):
    align = 16 if dtype == BF16 else 8
    n = want
    while n > 1 and (rows % n or (rows // n) % align):
        n -= 1
    return n


def _remote(src, dst, send_sem, recv_sem, device):
    return pltpu.make_async_remote_copy(src_ref=src, dst_ref=dst, send_sem=send_sem, recv_sem=recv_sem,
                                        device_id=device, device_id_type=MESH)


def _start_in_chunks(src, dst, send_sem, recv_sem, device, want=8):
    rows = src.shape[0]
    n = _n_chunks(rows, want, src.dtype)
    for i in range(n):
        part = pl.ds(i * (rows // n), rows // n)
        _remote(src.at[part], dst.at[part], send_sem, recv_sem, device).start()


def _all_gather_chips(name, shards, split, collective_id=None):
    n = len(shards)

    def body(*refs):
        ins, outs = refs[:n], refs[n:2 * n]
        send_sems, recv_sems, local_sems = refs[2 * n:]
        x, y, c = _position()
        me = 2 * x + y
        chips = _other_chips(x, y)
        sibling = (x, y, 1 - c)
        sends, forwards = [], []
        for t in range(n):
            pltpu.make_async_copy(ins[t], outs[t].at[me], local_sems.at[t]).start()
        for t in range(n):
            r = ins[t].shape[0]
            rows = pl.ds(c * (r // 2), r // 2) if split[t] else pl.ds(0, r)
            for k, (cx, cy) in enumerate(chips):
                src, dst = ins[t].at[rows], outs[t].at[me, rows]
                _start_in_chunks(src, dst, send_sems.at[t, k], recv_sems.at[t, k], (cx, cy, c), want=4)
                sends.append(_remote(src, dst, send_sems.at[t, k], recv_sems.at[t, k], (cx, cy, c)))
        for t in range(n):
            r = ins[t].shape[0]
            rows = pl.ds(c * (r // 2), r // 2) if split[t] else pl.ds(0, r)
            for k, (cx, cy) in enumerate(chips):
                landed = outs[t].at[2 * cx + cy, rows]
                _remote(landed, landed, send_sems.at[t, k], recv_sems.at[t, k], (cx, cy, c)).wait_recv()
                if split[t]:
                    _start_in_chunks(landed, landed, send_sems.at[t, 3 + k], recv_sems.at[t, 3 + k], sibling, want=4)
                    forwards.append(_remote(landed, landed, send_sems.at[t, 3 + k], recv_sems.at[t, 3 + k], sibling))
        for t in range(n):
            if split[t]:
                r = ins[t].shape[0]
                other = pl.ds((1 - c) * (r // 2), r // 2)
                for k, (cx, cy) in enumerate(chips):
                    landed = outs[t].at[2 * cx + cy, other]
                    pltpu.make_async_remote_copy(
                        src_ref=landed, dst_ref=landed, send_sem=send_sems.at[t, 3 + k], recv_sem=recv_sems.at[t, 3 + k],
                        device_id=sibling, device_id_type=MESH).wait_recv()
        for cp in sends + forwards:
            cp.wait_send()
        for t in range(n):
            pltpu.make_async_copy(ins[t], outs[t].at[me], local_sems.at[t]).wait()

    out_shape = [_sds((N_CHIPS,) + s.shape, s.dtype) for s in shards]
    sequencer = None if collective_id is None else (collective_id, _peers_chips_and_sibling)
    return _comm_call(name, body, n, out_shape,
                      [pltpu.SemaphoreType.DMA((n, 6)), pltpu.SemaphoreType.DMA((n, 6)), pltpu.SemaphoreType.DMA((n,))],
                      sequencer)(*shards)


def _swap_halves_with_sibling(name, blob, collective_id=None):
    def body(b_ref, theirs_ref, send_sem, recv_sem):
        x, y, c = _position()
        sibling = (x, y, 1 - c)
        for k in range(b_ref.shape[1]):
            _start_in_chunks(b_ref.at[1 - c, k], theirs_ref.at[k], send_sem, recv_sem, sibling)
        _remote(b_ref.at[1 - c], theirs_ref, send_sem, recv_sem, sibling).wait()

    return _comm_call(name, body, 1, [_sds(blob.shape[1:], blob.dtype)],
                      [pltpu.SemaphoreType.DMA(()), pltpu.SemaphoreType.DMA(())],
                      None if collective_id is None else (collective_id, _peers_sibling))(blob)[0]


def _scatter_to_chips(name, parts, collective_id=None):
    def body(p_ref, o_ref, send_sems, recv_sems, local_sems):
        x, y, c = _position()
        me = 2 * x + y
        rows = p_ref.shape[1]
        n_loc = _n_chunks(rows, 16, p_ref.dtype)
        locs = [pltpu.make_async_copy(p_ref.at[me, pl.ds(i * (rows // n_loc), rows // n_loc)],
                                      o_ref.at[me, pl.ds(i * (rows // n_loc), rows // n_loc)], local_sems.at[i])
                for i in range(n_loc)]
        for loc in locs:
            loc.start()
        sends = []
        for k, (cx, cy) in enumerate(_other_chips(x, y)):
            src, dst = p_ref.at[2 * cx + cy], o_ref.at[me]
            _start_in_chunks(src, dst, send_sems.at[k], recv_sems.at[k], (cx, cy, c))
            sends.append(_remote(src, dst, send_sems.at[k], recv_sems.at[k], (cx, cy, c)))
        for k, (cx, cy) in enumerate(_other_chips(x, y)):
            landed = o_ref.at[2 * cx + cy]
            _remote(landed, landed, send_sems.at[k], recv_sems.at[k], (cx, cy, c)).wait_recv()
        for cp in sends:
            cp.wait_send()
        for loc in locs:
            loc.wait()

    def local_sems_shape(rows):
        return pltpu.SemaphoreType.DMA((_n_chunks(rows, 16, parts.dtype),))

    return _comm_call(name, body, 1, [_sds(parts.shape, parts.dtype)],
                      [pltpu.SemaphoreType.DMA((3,)), pltpu.SemaphoreType.DMA((3,)), local_sems_shape(parts.shape[1])],
                      None if collective_id is None else (collective_id, _peers_chips))(parts)[0]


def _share_with_sibling(name, mine, collective_id=None):
    def body(m_ref, o_ref, send_sem, recv_sem):
        x, y, c = _position()
        sibling = (x, y, 1 - c)
        _start_in_chunks(m_ref, o_ref, send_sem, recv_sem, sibling, want=16)
        _remote(m_ref, o_ref, send_sem, recv_sem, sibling).wait()

    return _comm_call(name, body, 1, [_sds(mine.shape, mine.dtype)],
                      [pltpu.SemaphoreType.DMA(()), pltpu.SemaphoreType.DMA(())],
                      None if collective_id is None else (collective_id, _peers_sibling))(mine)[0]


def _row_tile(rows, cap=640):
    best = rows
    for t in range(16, min(rows, cap) + 1, 16):
        if rows % t == 0:
            best = t
    return best


def _add_my_half(name, blob, theirs, half_index, out_dtype):
    n, rows, cols = theirs.shape
    tr = _row_tile(rows)

    def body(c_ref, a_ref, b_ref, o_ref):
        o_ref[...] = (a_ref[0].astype(F32) + b_ref[...].astype(F32)).astype(out_dtype)

    spec = pl.BlockSpec((1, tr, cols), lambda k, i, c: (k, i, 0))
    grid_spec = pltpu.PrefetchScalarGridSpec(
        num_scalar_prefetch=1, grid=(n, rows // tr),
        in_specs=[pl.BlockSpec((1, 1, tr, cols), lambda k, i, c: (c[0], k, i, 0)), spec], out_specs=spec)
    return pl.pallas_call(
        body, name=name, grid_spec=grid_spec, out_shape=_sds(theirs.shape, out_dtype),
        compiler_params=pltpu.CompilerParams(dimension_semantics=("arbitrary", "arbitrary"),
                                             vmem_limit_bytes=V7X_VMEM_LIMIT))(half_index, blob, theirs)


def _sum_slots(name, parts):
    n, rows, cols = parts.shape
    tr = _row_tile(rows)

    def body(p_ref, o_ref):
        acc = p_ref[0].astype(F32)
        for k in range(1, n):
            acc = acc + p_ref[k].astype(F32)
        o_ref[...] = acc

    return _call(name, body, (rows // tr,), [pl.BlockSpec((n, tr, cols), lambda i: (0, i, 0))],
                 pl.BlockSpec((tr, cols), lambda i: (i, 0)), _sds((rows, cols), F32))(parts)


def _reduce_small_adamw(name, grads, loss_tile, ws, ms, vs):
    n = len(grads)
    srcs = list(grads) + [loss_tile]

    def body(*refs):
        g_in, w_in, m_in, v_in = refs[:n + 1], refs[n + 1:2 * n + 1], refs[2 * n + 1:3 * n + 1], refs[3 * n + 1:4 * n + 1]
        outs = refs[4 * n + 1:8 * n + 2]
        g_out, d_out, nm_out, nv_out, loss_out = outs[:n], outs[n:2 * n], outs[2 * n:3 * n], outs[3 * n:4 * n], outs[4 * n]
        bufs = refs[8 * n + 2:9 * n + 3]
        send_sems, recv_sems = refs[9 * n + 3:]
        x, y, c = _position()
        me = 4 * x + 2 * y + c
        chip = 2 * x + y
        peers = [(1 - x if dlt & 4 else x, 1 - y if dlt & 2 else y, 1 - c if dlt & 1 else c) for dlt in range(1, N_DEV)]
        sends = []
        for t in range(n + 1):
            bufs[t][me] = g_in[t][...]
            for k, peer in enumerate(peers):
                cp = _remote(g_in[t], bufs[t].at[me], send_sems.at[t, k], recv_sems.at[t, k], peer)
                cp.start()
                sends.append(cp)
        for t in range(n + 1):
            for k, (tx, ty, tc) in enumerate(peers):
                landed = bufs[t].at[4 * tx + 2 * ty + tc]
                _remote(landed, landed, send_sems.at[t, k], recv_sems.at[t, k], (tx, ty, tc)).wait_recv()
        for cp in sends:
            cp.wait_send()
        for t in range(n + 1):
            total = bufs[t][0]
            for k in range(1, N_DEV):
                total = total + bufs[t][k]
            if t == n:
                loss_out[...] = total
                continue
            cols = w_in[t].shape[1]
            if cols == total.shape[1]:
                g_out[t][...] = total
                d_out[t][...], nm_out[t][...], nv_out[t][...] = _adamw_math(w_in[t][...], total, m_in[t][...], v_in[t][...])
            else:
                for j in range(N_CHIPS):
                    @pl.when(chip == j)
                    def _(t=t, j=j, cols=cols, total=total):
                        mine = total[:, j * cols:(j + 1) * cols]
                        g_out[t][...] = mine
                        d_out[t][...], nm_out[t][...], nv_out[t][...] = _adamw_math(w_in[t][...], mine, m_in[t][...], v_in[t][...])

    w_shapes = [_sds(a.shape, F32) for a in ws]
    return pl.pallas_call(
        body, name=name, in_specs=[_VMEM] * (4 * n + 1), out_specs=[_VMEM] * (4 * n + 1),
        out_shape=w_shapes * 4 + [_sds(loss_tile.shape, F32)],
        scratch_shapes=[pltpu.VMEM((N_DEV,) + a.shape, F32) for a in srcs]
        + [pltpu.SemaphoreType.DMA((n + 1, N_DEV - 1)), pltpu.SemaphoreType.DMA((n + 1, N_DEV - 1))],
        compiler_params=pltpu.CompilerParams(has_side_effects=True, vmem_limit_bytes=V7X_VMEM_LIMIT))(
            *srcs, *ws, *ms, *vs)


_BIG = ("ffn1_w_gate_up", "ffn1_w_down", "ffn2_w_gate_up", "ffn2_w_down", "ssm_w_in", "ssm_w_out", "w_kv", "attn_w_q", "attn_w_o")
_TRANSPOSED = ("ffn1_w_gate_up", "ffn2_w_gate_up")
_SMALL = ("meta_tokens", "ffn1_norm", "mix_norm", "ffn2_norm", "ssm_lambda_re", "ssm_lambda_im", "ssm_b_re", "ssm_b_im",
          "ssm_c_re", "ssm_c_im", "ssm_log_step", "ssm_d", "kv_norm", "k_norm", "q_norm", "attn_sinks")
_ORDER = ("meta_tokens", "ffn1_norm", "ffn1_w_gate_up", "ffn1_w_down", "mix_norm", "ffn2_norm", "ffn2_w_gate_up", "ffn2_w_down",
          "ssm_w_in", "ssm_lambda_re", "ssm_lambda_im", "ssm_b_re", "ssm_b_im", "ssm_c_re", "ssm_c_im", "ssm_log_step", "ssm_d",
          "ssm_w_out", "kv_norm", "w_kv", "k_norm", "attn_w_q", "q_norm", "attn_sinks", "attn_w_o")


def _step(x, target, w, m, v):
    n_ex, seq, d = x.shape
    n_meta = w["meta_tokens"].shape[0]
    n_main = n_ex * seq
    n_all = n_main + n_ex * META_BLOCK
    n_g, n_p, n_c = w["ssm_b_re"].shape[1:]
    hd = w["k_norm"].shape[0]
    n_kv = w["w_kv"].shape[1] // (2 * hd)
    n_q = w["attn_w_q"].shape[2] // hd
    qpk = n_q // n_kv
    px, py, pc = _position()
    chip = 2 * px + py

    cast = lambda a: a.astype(BF16)
    first = [cast(w["ffn1_w_gate_up"][0]), cast(w["ffn1_w_down"][0]), cast(w["ssm_w_in"][0]), cast(w["ssm_w_out"][0]),
             w["meta_tokens"], w["ssm_d"]]
    g_a = _all_gather_chips("gather_first", first, [True, True, True, True, False, False])
    second = [cast(w["ffn2_w_gate_up"][0]), cast(w["ffn2_w_down"][0]), cast(w["w_kv"])]
    g_b = _all_gather_chips("gather_second", second, [True] * 3, collective_id=1)
    third = [cast(w["ffn1_w_gate_up"][1]), cast(w["ffn1_w_down"][1]), cast(w["attn_w_q"][0]), cast(w["attn_w_o"][0]),
             cast(w["ffn2_w_gate_up"][1]), cast(w["ffn2_w_down"][1])]
    g_c = _all_gather_chips("gather_third", third, [True] * 6, collective_id=2)
    wgu = {("ffn1", 0): g_a[0], ("ffn1", 1): g_c[0], ("ffn2", 0): g_b[0], ("ffn2", 1): g_c[4]}
    wd = {("ffn1", 0): g_a[1], ("ffn1", 1): g_c[1], ("ffn2", 0): g_b[1], ("ffn2", 1): g_c[5]}
    wd = {key: a.reshape(-1, d) for key, a in wd.items()}
    w_in = g_a[2].reshape(d, -1)
    wout4 = g_a[3]
    w_q = g_c[2].reshape(d, -1)
    w_o = g_c[3].reshape(-1, d)
    w_kv = g_b[2].reshape(d, -1)
    meta_full = jnp.transpose(g_a[4], (1, 0, 2)).reshape(n_meta, d)
    dskip = g_a[5].reshape(1, -1)

    row1 = lambda a: a.reshape(1, -1)
    ssm_args = tuple(w[k][0] for k in ("ssm_lambda_re", "ssm_lambda_im", "ssm_log_step", "ssm_b_re", "ssm_b_im", "ssm_c_re", "ssm_c_im"))
    (bb, cb, a_re, a_im), ssm_vjp = jax.vjp(_ssm_mats, *ssm_args)
    bb16, cb16 = bb.astype(BF16), cb.astype(BF16)
    a_re_s, a_im_s = lax.stop_gradient(a_re), lax.stop_gradient(a_im)
    half = n_g * n_p // 4
    tabs_f = _scan_tables(a_re_s, a_im_s, False)
    tabs_b = _scan_tables(a_re_s, a_im_s, True)

    freqs = ROPE_THETA ** (-jnp.arange(0, hd // 2, dtype=F32) * 2.0 / hd)
    pos_main = jnp.tile(n_meta + jnp.arange(seq), n_ex)
    pos_meta = jnp.tile(jnp.maximum(jnp.arange(META_BLOCK) - (META_BLOCK - n_meta), 0), n_ex)
    ang = jnp.concatenate([pos_main, pos_meta]).astype(F32)[:, None] * freqs[None, :]
    cos = jnp.concatenate([jnp.cos(ang), jnp.cos(ang)], axis=1)
    sin_s = jnp.concatenate([-jnp.sin(ang), jnp.sin(ang)], axis=1)

    meta_block = jnp.concatenate([jnp.zeros((META_BLOCK - n_meta, d), F32), meta_full], axis=0)
    h0 = jnp.concatenate([x.reshape(n_main, d)] + [meta_block] * n_ex, axis=0)

    g = lambda name, layer: row1(w[name][layer])
    h1, gu1 = _ffn_fwd("l0_ffn1", h0, g("ffn1_norm", 0), wgu["ffn1", 0], wd["ffn1", 0], n_all)
    u, bu = _ssm_in("ssm_in", h1, g("mix_norm", 0), w_in, bb16, n_all)
    xs = _scan_fwd("ssm_scan", bu, tabs_f, n_ex, seq)
    h2, y = _ssm_out("ssm_out", xs, u, dskip, cb16, wout4, h1, n_all)
    h3, gu2 = _ffn_fwd("l0_ffn2", h2, g("ffn2_norm", 0), wgu["ffn2", 0], wd["ffn2", 0], n_all)
    kvraw, k, vv = _kv_proj("kv_proj", h3, row1(w["kv_norm"]), w_kv, row1(w["k_norm"]), cos, sin_s, n_all, n_kv, hd)
    h4, gu3 = _ffn_fwd("l1_ffn1", h3, g("ffn1_norm", 1), wgu["ffn1", 1], wd["ffn1", 1], n_main)
    qraw, q = _q_proj("q_proj", h4, g("mix_norm", 1), w_q, row1(w["q_norm"][0]), cos, sin_s, n_main, n_q, hd)
    sinks = row1(w["attn_sinks"][0])
    o, lse = _attn_fwd("attn_fwd", q, k, vv, sinks, n_ex, seq, n_meta, n_kv, qpk, hd)
    h5 = _attn_out("attn_out", o, h4, w_o, n_main)
    h6, gu4 = _ffn_fwd("l1_ffn2", h5, g("ffn2_norm", 1), wgu["ffn2", 1], wd["ffn2", 1], n_main)
    dh6, loss_tile = _loss_head("loss_head", h6, target.reshape(n_main, d), n_main)

    lanes = 1024

    def reduce_scatter(tag, entries, collective_ids):
        ids = collective_ids or (None, None, None)
        pieces = [gr.reshape(N_CHIPS, 2, -1, lanes) for _, _, gr in entries]
        blob = jnp.transpose(jnp.concatenate(pieces, axis=2), (1, 0, 2, 3)).astype(BF16)
        theirs = _swap_halves_with_sibling(tag + "_swap", blob, ids[0])
        chip_sum = _add_my_half(tag + "_chip_sum", blob, theirs, jnp.reshape(pc, (1,)).astype(jnp.int32), BF16)
        landed = _scatter_to_chips(tag + "_scatter", chip_sum, ids[1])
        total = _sum_slots(tag + "_sum", landed)
        other = _share_with_sibling(tag + "_share", total, ids[2])
        halves = (jnp.where(pc == 0, total, other), jnp.where(pc == 0, other, total))
        out, off = {}, 0
        for name, layer, gr in entries:
            rows = gr.shape[1] * gr.shape[2] // lanes // 2
            flat = jnp.concatenate([hv[off:off + rows].reshape(-1) for hv in halves])
            if name in _TRANSPOSED:
                flat = flat.reshape(gr.shape[1], gr.shape[2]).T.reshape(-1)
            out[name, layer] = flat
            off += rows
        return out

    small = {}
    dh5, dg_f2l1, dwgu_f2l1, dwd_f2l1 = _ffn_bwd("l1_ffn2", dh6, h5, g("ffn2_norm", 1), gu4, wgu["ffn2", 1], wd["ffn2", 1], n_main)
    do = _attn_out_bwd("attn_out_bwd", dh5, w_o, n_main)
    dw_o = _tn_plain("attn_dwo", o, dh5, 1, o.shape[1], d, n_main).reshape(N_CHIPS, -1, d)
    dq, dk_main, dv_main, dk_meta, dv_meta, dsinks = _attn_bwd("attn_bwd", q, k, vv, sinks, o, lse, do, n_ex, seq, n_meta, n_kv, qpk, hd)
    dqraw, dh4, dq_gain, dg_mix1 = _q_bwd("q_bwd", dq, qraw, row1(w["q_norm"][0]), cos, sin_s, w_q, h4, g("mix_norm", 1), dh5, n_main, n_q, hd)
    dw_q = _tn_rms("attn_dwq", h4, g("mix_norm", 1), dqraw, n_main).reshape(N_CHIPS, -1, dqraw.shape[1])
    dh3m, dg_f1l1, dwgu_f1l1, dwd_f1l1 = _ffn_bwd("l1_ffn1", dh4, h3, g("ffn1_norm", 1), gu3, wgu["ffn1", 1], wd["ffn1", 1], n_main)
    reduced = reduce_scatter("rs1", [("ffn2_w_gate_up", 1, dwgu_f2l1), ("ffn2_w_down", 1, dwd_f2l1), ("attn_w_o", 0, dw_o),
                                     ("attn_w_q", 0, dw_q), ("ffn1_w_gate_up", 1, dwgu_f1l1), ("ffn1_w_down", 1, dwd_f1l1)], (3, 4, 5))

    def with_meta(main, meta):
        blocks = [jnp.pad(meta[b * n_meta:(b + 1) * n_meta], ((META_BLOCK - n_meta, 0), (0, 0))) for b in range(n_ex)]
        return jnp.concatenate([main] + blocks, axis=0)

    dkvraw, dh3, dk_gain, dg_kv = _kv_bwd("kv_bwd", with_meta(dk_main, dk_meta), with_meta(dv_main, dv_meta), kvraw, row1(w["k_norm"]),
                                          cos, sin_s, w_kv, h3, row1(w["kv_norm"]), dh3m, n_all, n_main, n_kv, hd)
    dw_kv = _tn_rms("kv_dw", h3, row1(w["kv_norm"]), dkvraw, n_all).reshape(N_CHIPS, -1, dkvraw.shape[1])
    dh2, dg_f2l0, dwgu_f2l0, dwd_f2l0 = _ffn_bwd("l0_ffn2", dh3, h2, g("ffn2_norm", 0), gu2, wgu["ffn2", 0], wd["ffn2", 0], n_all)

    dy, dz, gx, dd = _ssm_out_bwd("ssm_out_bwd", dh2, y, u, cb16, wout4, n_all)
    hw = y.shape[1]
    oc = wout4.shape[2]
    dw_out = _tn_plain("ssm_dwout", y, dz, wout4.shape[0], hw, oc, n_all, a_fn=_gelu)
    dcb = _tn_plain("ssm_dcb", xs, dy, 4, xs.shape[1] // 4, hw // 4, n_all)
    gbu, da = _scan_bwd("ssm_scan_bwd", gx, xs, tabs_b, n_ex, seq)
    du, dh1, dg_mix0 = _ssm_in_bwd("ssm_in_bwd", gbu, dy, dskip, bb16, w_in, h1, g("mix_norm", 0), dh2, n_all)
    dbb = _tn_plain("ssm_dbb", u, gbu, 4, hw // 4, gbu.shape[1] // 4, n_all)
    dw_in = _tn_rms("ssm_dwin", h1, g("mix_norm", 0), du, n_all).reshape(N_CHIPS, -1, hw)
    dh0, dg_f1l0, dwgu_f1l0, dwd_f1l0 = _ffn_bwd("l0_ffn1", dh1, h0, g("ffn1_norm", 0), gu1, wgu["ffn1", 0], wd["ffn1", 0], n_all)

    grad_x = dh0[:n_main].reshape(n_ex, seq, d)
    da_sum = jnp.sum(da, axis=(0, 1)).reshape(4, 2, half)
    d_ssm = ssm_vjp((dbb, dcb, da_sum[:, 0].reshape(-1), da_sum[:, 1].reshape(-1)))
    for key, val in zip(("ssm_lambda_re", "ssm_lambda_im", "ssm_log_step", "ssm_b_re", "ssm_b_im", "ssm_c_re", "ssm_c_im"), d_ssm):
        small[key] = val[None]
    small["meta_tokens"] = sum(dh0[n_main + META_BLOCK * (b + 1) - n_meta:n_main + META_BLOCK * (b + 1)] for b in range(n_ex))
    small["ffn1_norm"] = jnp.concatenate([dg_f1l0, dg_f1l1], axis=0)
    small["ffn2_norm"] = jnp.concatenate([dg_f2l0, dg_f2l1], axis=0)
    small["mix_norm"] = jnp.concatenate([dg_mix0, dg_mix1], axis=0)
    small["ssm_d"] = dd
    small["kv_norm"] = dg_kv.reshape(-1)
    small["k_norm"] = dk_gain.reshape(-1)
    small["q_norm"] = dq_gain
    small["attn_sinks"] = dsinks
    reduced.update(reduce_scatter("rs0", [("w_kv", 0, dw_kv), ("ffn2_w_gate_up", 0, dwgu_f2l0), ("ffn2_w_down", 0, dwd_f2l0),
                                          ("ssm_w_out", 0, dw_out), ("ssm_w_in", 0, dw_in), ("ffn1_w_gate_up", 0, dwgu_f1l0),
                                          ("ffn1_w_down", 0, dwd_f1l0)], None))

    def view(name, a):
        if name in ("ssm_b_re", "ssm_b_im"):
            return a.reshape(-1, 128)
        return a.reshape(1, -1) if a.ndim == 1 else a.reshape(-1, a.shape[-1])

    tail = _reduce_small_adamw("small_tail", [view(k, small[k]) for k in _SMALL], loss_tile,
                               *[[view(k, t[k]) for k in _SMALL] for t in (w, m, v)])
    n_small = len(_SMALL)
    grads = {k: tail[i].reshape(w[k].shape) for i, k in enumerate(_SMALL)}
    deltas = {k: tail[n_small + i].reshape(w[k].shape) for i, k in enumerate(_SMALL)}
    new_m = {k: tail[2 * n_small + i].reshape(w[k].shape) for i, k in enumerate(_SMALL)}
    new_v = {k: tail[3 * n_small + i].reshape(w[k].shape) for i, k in enumerate(_SMALL)}
    loss = jnp.sum(tail[-1])

    for name in _BIG:
        layers = [reduced[name, layer] for layer in range(2) if (name, layer) in reduced]
        grads[name] = jnp.concatenate(layers).reshape(w[name].shape)
    for name in _BIG:
        shape = w[name].shape
        two_d = lambda a: a.reshape(-1, shape[-1])
        dl, nm, nv = _adamw("adamw_" + name, two_d(w[name]), two_d(grads[name]), two_d(m[name]), two_d(v[name]))
        deltas[name], new_m[name], new_v[name] = dl.reshape(shape), nm.reshape(shape), nv.reshape(shape)
    return (loss, grad_x, *[grads[k] for k in _ORDER], *[deltas[k] for k in _ORDER], *[new_m[k] for k in _ORDER],
            *[new_v[k] for k in _ORDER])


def kernel(x, meta_tokens, ffn1_norm, ffn1_w_gate_up, ffn1_w_down, mix_norm, ffn2_norm, ffn2_w_gate_up, ffn2_w_down, ssm_w_in, ssm_lambda_re, ssm_lambda_im, ssm_b_re, ssm_b_im, ssm_c_re, ssm_c_im, ssm_log_step, ssm_d, ssm_w_out, kv_norm, w_kv, k_norm, attn_w_q, q_norm, attn_sinks, attn_w_o, loss_target, m_meta_tokens, m_ffn1_norm, m_ffn1_w_gate_up, m_ffn1_w_down, m_mix_norm, m_ffn2_norm, m_ffn2_w_gate_up, m_ffn2_w_down, m_ssm_w_in, m_ssm_lambda_re, m_ssm_lambda_im, m_ssm_b_re, m_ssm_b_im, m_ssm_c_re, m_ssm_c_im, m_ssm_log_step, m_ssm_d, m_ssm_w_out, m_kv_norm, m_w_kv, m_k_norm, m_attn_w_q, m_q_norm, m_attn_sinks, m_attn_w_o, v_meta_tokens, v_ffn1_norm, v_ffn1_w_gate_up, v_ffn1_w_down, v_mix_norm, v_ffn2_norm, v_ffn2_w_gate_up, v_ffn2_w_down, v_ssm_w_in, v_ssm_lambda_re, v_ssm_lambda_im, v_ssm_b_re, v_ssm_b_im, v_ssm_c_re, v_ssm_c_im, v_ssm_log_step, v_ssm_d, v_ssm_w_out, v_kv_norm, v_w_kv, v_k_norm, v_attn_w_q, v_q_norm, v_attn_sinks, v_attn_w_o):
    args = locals()
    w = {k: args[k] for k in _ORDER}
    m = {k: args["m_" + k] for k in _ORDER}
    v = {k: args["v_" + k] for k in _ORDER}
    return _step(x, loss_target, w, m, v)
```

```python
import functools
import math

import jax
import jax.numpy as jnp
from jax import lax
from jax.experimental import pallas as pl
from jax.experimental.pallas import tpu as pltpu
from jax.experimental.pallas import tpu_sc as plsc

F32 = jnp.float32
BF16 = jnp.bfloat16
MESH = pl.DeviceIdType.MESH

EPS = 1e-6
NEG_INF = -1e30
ROPE_THETA = 10000.0
WINDOW = 128
META_BLOCK = 128
ROW_TILE = 256
SUBLANES = 8
V7X_VMEM_LIMIT = 56 * 2**20
N_CHIPS = 4
N_DEV = 8

ADAM_LR = 0.001
ADAM_B1 = 0.9
ADAM_B2 = 0.999
ADAM_EPS = 1e-08
ADAM_WD = 0.01
ADAM_STEP = 10

_HBM = pl.BlockSpec(memory_space=pltpu.HBM)
_VMEM = pl.BlockSpec(memory_space=pltpu.VMEM)


def _call(name, body, grid, in_specs, out_specs, out_shape, scratch=()):
    return pl.pallas_call(
        body, name=name, grid=grid, in_specs=in_specs, out_specs=out_specs, out_shape=out_shape,
        scratch_shapes=list(scratch),
        compiler_params=pltpu.CompilerParams(dimension_semantics=("arbitrary",) * len(grid),
                                             vmem_limit_bytes=V7X_VMEM_LIMIT))


def _sds(shape, dtype):
    return jax.ShapeDtypeStruct(tuple(shape), dtype)


def _dot(a, b):
    return jnp.dot(a.astype(BF16), b.astype(BF16), preferred_element_type=F32)


def _dot_nt(a, b):
    return lax.dot_general(a.astype(BF16), b.astype(BF16), (((1,), (1,)), ((), ())), preferred_element_type=F32)


def _dot_tn(a, b):
    return lax.dot_general(a.astype(BF16), b.astype(BF16), (((0,), (0,)), ((), ())), preferred_element_type=F32)


def _rms(h, g):
    return h * lax.rsqrt(jnp.mean(h * h, axis=-1, keepdims=True) + EPS) * g


def _rms_bwd(h, g, dn):
    r = lax.rsqrt(jnp.mean(h * h, axis=-1, keepdims=True) + EPS)
    xh = h * r
    dxh = dn * g
    dg = jnp.sum(dn * xh, axis=0, keepdims=True)
    dh = r * (dxh - xh * jnp.mean(dxh * xh, axis=-1, keepdims=True))
    return dh, dg


def _sigmoid(x):
    return 0.5 * jnp.tanh(0.5 * x) + 0.5


def _gelu(y):
    k = math.sqrt(2.0 / math.pi)
    return 0.5 * y * (1.0 + jnp.tanh(k * (y + 0.044715 * y * y * y)))


def _gelu_grad(y):
    k = math.sqrt(2.0 / math.pi)
    t = jnp.tanh(k * (y + 0.044715 * y * y * y))
    return 0.5 * (1.0 + t) + 0.5 * y * (1.0 - t * t) * k * (1.0 + 3.0 * 0.044715 * y * y)


def _swap_halves(x):
    half = x.shape[-1] // 2
    return jnp.concatenate([x[:, half:], x[:, :half]], axis=1)


def _head_prep(x, gain, cos, sin_s, n_heads, hd):
    out = []
    for h in range(n_heads):
        xh = x[:, h * hd:(h + 1) * hd]
        y = xh * lax.rsqrt(jnp.mean(xh * xh, axis=-1, keepdims=True) + EPS) * gain
        out.append(y * cos + _swap_halves(y) * sin_s)
    return out


def _head_prep_bwd(x, gain, cos, sin_s, d_out, n_heads, hd):
    dxs = []
    dgain = jnp.zeros((1, hd), F32)
    for h in range(n_heads):
        xh = x[:, h * hd:(h + 1) * hd]
        do = d_out[:, h * hd:(h + 1) * hd]
        r = lax.rsqrt(jnp.mean(xh * xh, axis=-1, keepdims=True) + EPS)
        xhat = xh * r
        dy = do * cos + _swap_halves(do * sin_s)
        dgain = dgain + jnp.sum(dy * xhat, axis=0, keepdims=True)
        dxh = dy * gain
        dxs.append(r * (dxh - xhat * jnp.mean(dxh * xhat, axis=-1, keepdims=True)))
    return dxs, dgain


def _acc_out(ref, val, first):
    @pl.when(first)
    def _():
        ref[...] = jnp.zeros_like(ref)
    ref[...] += val


def _ffn_up(name, h, g, w4, n_rows):
    nj, d, fc = w4.shape
    tm = ROW_TILE

    def body(h_ref, g_ref, w_ref, o_ref, n_ref):
        n = _rms(h_ref[...], g_ref[...]).astype(BF16)
        n_ref[...] = n
        for j in range(nj):
            o_ref[:, j * fc:(j + 1) * fc] = _dot(n, w_ref[j]).astype(BF16)

    return _call(name, body, (n_rows // tm,),
                 [pl.BlockSpec((tm, d), lambda i: (i, 0)), pl.BlockSpec((1, d), lambda i: (0, 0)),
                  pl.BlockSpec((nj, d, fc), lambda i: (0, 0, 0))],
                 [pl.BlockSpec((tm, nj * fc), lambda i: (i, 0)), pl.BlockSpec((tm, d), lambda i: (i, 0))],
                 [_sds((n_rows, nj * fc), BF16), _sds((n_rows, d), BF16)])(h, g, w4)


def _ffn_down(name, gu, h, wd, n_rows):
    f, d = wd.shape
    tm = ROW_TILE

    def body(gu_ref, h_ref, w_ref, o_ref, s_ref):
        a = gu_ref[:, :f].astype(F32)
        b = gu_ref[:, f:].astype(F32)
        s = (a * _sigmoid(a) * b).astype(BF16)
        s_ref[...] = s
        o_ref[...] = h_ref[...] + 0.5 * _dot(s, w_ref[...])

    return _call(name, body, (n_rows // tm,),
                 [pl.BlockSpec((tm, 2 * f), lambda i: (i, 0)), pl.BlockSpec((tm, d), lambda i: (i, 0)),
                  pl.BlockSpec((f, d), lambda i: (0, 0))],
                 [pl.BlockSpec((tm, d), lambda i: (i, 0)), pl.BlockSpec((tm, f), lambda i: (i, 0))],
                 [_sds((n_rows, d), F32), _sds((n_rows, f), BF16)])(gu, h, wd)


def _ffn_dgu(name, dh, gu, wd, n_rows):
    f, d = wd.shape
    tm = ROW_TILE

    def body(dh_ref, gu_ref, w_ref, o_ref):
        ds = _dot_nt(0.5 * dh_ref[...], w_ref[...])
        a = gu_ref[:, :f].astype(F32)
        b = gu_ref[:, f:].astype(F32)
        sg = _sigmoid(a)
        o_ref[:, :f] = (ds * b * (sg * (1.0 + a * (1.0 - sg)))).astype(BF16)
        o_ref[:, f:] = (ds * (a * sg)).astype(BF16)

    return _call(name, body, (n_rows // tm,),
                 [pl.BlockSpec((tm, d), lambda i: (i, 0)), pl.BlockSpec((tm, 2 * f), lambda i: (i, 0)),
                  pl.BlockSpec((f, d), lambda i: (0, 0))],
                 pl.BlockSpec((tm, 2 * f), lambda i: (i, 0)), _sds((n_rows, 2 * f), BF16))(dh, gu, wd)


def _ffn_dh(name, dgu, h, g, dh, w4, n_rows, n_main=None):
    nj, d, fc = w4.shape
    tm = ROW_TILE
    n_first = (n_rows if n_main is None else n_main) // tm

    def body(dgu_ref, h_ref, g_ref, dh_ref, w_ref, o_ref, *rest):
        dg_ref = rest[-1]
        i = pl.program_id(0)
        dn = _dot_nt(dgu_ref[:, 0:fc], w_ref[0])
        for j in range(1, nj):
            dn = dn + _dot_nt(dgu_ref[:, j * fc:(j + 1) * fc], w_ref[j])
        dhn, dg = _rms_bwd(h_ref[...], g_ref[...], dn)
        val = dh_ref[...] + dhn
        if n_main is None:
            o_ref[...] = val
        else:
            @pl.when(i < n_first)
            def _():
                o_ref[...] = val

            @pl.when(i >= n_first)
            def _():
                rest[0][...] = val
        _acc_out(dg_ref, dg, i == 0)

    out_specs = [pl.BlockSpec((tm, d), lambda i: (jnp.minimum(i, n_first - 1), 0))]
    out_shape = [_sds((n_first * tm, d), F32)]
    if n_main is not None:
        out_specs.append(pl.BlockSpec((tm, d), lambda i: (jnp.maximum(i - n_first, 0), 0)))
        out_shape.append(_sds((n_rows - n_main, d), F32))
    return _call(name, body, (n_rows // tm,),
                 [pl.BlockSpec((tm, nj * fc), lambda i: (i, 0)), pl.BlockSpec((tm, d), lambda i: (i, 0)),
                  pl.BlockSpec((1, d), lambda i: (0, 0)), pl.BlockSpec((tm, d), lambda i: (i, 0)),
                  pl.BlockSpec((nj, d, fc), lambda i: (0, 0, 0))],
                 out_specs + [pl.BlockSpec((1, d), lambda i: (0, 0))],
                 out_shape + [_sds((1, d), F32)])(dgu, h, g, dh, w4)


def _contract_tile(n_rows, cap=1024):
    best = ROW_TILE
    for t in range(ROW_TILE, cap + 1, ROW_TILE):
        if n_rows % t == 0:
            best = t
    return best


def _tn(name, operands, in_specs, prologue, nj, ma, nb, n_rows, tk):
    def body(*refs):
        o_ref = refs[-1]
        a, b = prologue(pl.program_id(0), *refs[:-1])
        _acc_out(o_ref, _dot_tn(a, b)[None], pl.program_id(1) == 0)

    return _call(name, body, (nj, n_rows // tk), in_specs, pl.BlockSpec((1, ma, nb), lambda j, k: (j, 0, 0)),
                 _sds((nj, ma, nb), F32))(*operands)


def _ffn_dwgu(name, n, dgu, nj, n_rows):
    d = n.shape[1]
    fc = dgu.shape[1] // nj
    tk = _contract_tile(n_rows)
    return _tn(name, (dgu, n),
               [pl.BlockSpec((tk, fc), lambda j, k: (k, j)), pl.BlockSpec((tk, d), lambda j, k: (k, 0))],
               lambda j, a_ref, b_ref: (a_ref[...], b_ref[...]), nj, fc, d, n_rows, tk)


def _ffn_dwd(name, s, dh, n_rows):
    f = s.shape[1]
    d = dh.shape[1]
    tk = _contract_tile(n_rows)
    return _tn(name, (s, dh),
               [pl.BlockSpec((tk, f), lambda j, k: (k, 0)), pl.BlockSpec((tk, d), lambda j, k: (k, 0))],
               lambda j, s_ref, dh_ref: (s_ref[...], 0.5 * dh_ref[...]), 1, f, d, n_rows, tk)


def _ffn_fwd(tag, h, g, w4, wd, n_rows):
    gu, n = _ffn_up(tag + "_up", h, g, w4, n_rows)
    h_out, s = _ffn_down(tag + "_down", gu, h, wd, n_rows)
    return h_out, (gu, n, s)


def _ffn_bwd(tag, dh_out, h, g, saved, w4, wd, n_rows, n_main=None):
    gu, n, s = saved
    nj = w4.shape[0]
    dgu = _ffn_dgu(tag + "_dgu", dh_out, gu, wd, n_rows)
    dwd = _ffn_dwd(tag + "_dwd", s, dh_out, n_rows)
    *dh_parts, dg = _ffn_dh(tag + "_dh", dgu, h, g, dh_out, w4, n_rows, n_main)
    dh_in = dh_parts[0] if n_main is None else tuple(dh_parts)
    dwgu = _ffn_dwgu(tag + "_dwgu", n, dgu, nj, n_rows)
    f, d = wd.shape
    return dh_in, dg, dwgu, dwd.reshape(N_CHIPS, f // N_CHIPS, d)


def _ssm_in(name, h, g, w_in, bb, n_rows):
    d, hw = w_in.shape
    nj, uc, xc = bb.shape
    tm = ROW_TILE

    def body(h_ref, g_ref, w_ref, bb_ref, u_ref, bu_ref):
        u = _dot(_rms(h_ref[...], g_ref[...]), w_ref[...])
        u_ref[...] = u
        for j in range(nj):
            bu_ref[:, j * xc:(j + 1) * xc] = _dot(u[:, j * uc:(j + 1) * uc], bb_ref[j])

    return _call(name, body, (n_rows // tm,),
                 [pl.BlockSpec((tm, d), lambda i: (i, 0)), pl.BlockSpec((1, d), lambda i: (0, 0)),
                  pl.BlockSpec((d, hw), lambda i: (0, 0)), pl.BlockSpec((nj, uc, xc), lambda i: (0, 0, 0))],
                 [pl.BlockSpec((tm, hw), lambda i: (i, 0)), pl.BlockSpec((tm, nj * xc), lambda i: (i, 0))],
                 [_sds((n_rows, hw), F32), _sds((n_rows, nj * xc), F32)])(h, g, w_in, bb)


def _cmul_add(xr, xi, ar, ai, sr, si):
    return xr + ar * sr - ai * si, xi + ar * si + ai * sr


def _scan_row_block(n_main_blocks, seq_blocks):
    return lambda b, i: jnp.where(i == 0, n_main_blocks + b, b * seq_blocks + i - 1)


def _scan_fwd(name, bu, tabs, n_ex, seq):
    n_rows, width = bu.shape
    nj = 4
    cw = width // nj
    half = cw // 2
    tq = META_BLOCK
    seq_blocks = seq // tq
    rb = _scan_row_block(n_ex * seq_blocks, seq_blocks)

    def body(bu_ref, tab_ref, x_ref, carry_ref):
        @pl.when(pl.program_id(2) == 0)
        def _():
            carry_ref[...] = jnp.zeros_like(carry_ref)
        def blk(k, c):
            t = [tab_ref[n * SUBLANES:(n + 1) * SUBLANES, :] for n in range(8)]
            r0 = pl.multiple_of(k * SUBLANES, SUBLANES)
            xr = bu_ref[pl.ds(r0, SUBLANES), 0:half]
            xi = bu_ref[pl.ds(r0, SUBLANES), half:cw]
            for s, d in enumerate((1, 2, 4)):
                xr, xi = _cmul_add(xr, xi, t[2 * s], t[2 * s + 1], pltpu.roll(xr, d, 0), pltpu.roll(xi, d, 0))
            xr, xi = _cmul_add(xr, xi, t[6], t[7], c[0], c[1])
            x_ref[pl.ds(r0, SUBLANES), 0:half] = xr
            x_ref[pl.ds(r0, SUBLANES), half:cw] = xi
            last = SUBLANES - 1
            return (jnp.broadcast_to(xr[last:last + 1, :], xr.shape), jnp.broadcast_to(xi[last:last + 1, :], xi.shape))

        c = lax.fori_loop(0, tq // SUBLANES, blk, (carry_ref[0], carry_ref[1]))
        carry_ref[0] = c[0]
        carry_ref[1] = c[1]

    return _call(name, body, (n_ex, nj, seq_blocks + 1),
                 [pl.BlockSpec((tq, cw), lambda b, j, i: (rb(b, i), j)), pl.BlockSpec((8 * SUBLANES, half), lambda b, j, i: (0, j))],
                 pl.BlockSpec((tq, cw), lambda b, j, i: (rb(b, i), j)), _sds((n_rows, width), F32),
                 scratch=[pltpu.VMEM((2, SUBLANES, half), F32)])(bu, tabs)


def _scan_bwd(name, gx, x, tabs, n_ex, seq):
    n_rows, width = gx.shape
    nj = 4
    cw = width // nj
    half = cw // 2
    tq = META_BLOCK
    seq_blocks = seq // tq
    n_steps = seq_blocks + 1
    rb = _scan_row_block(n_ex * seq_blocks, seq_blocks)
    rbr = lambda b, i: rb(b, n_steps - 1 - i)

    def body(gx_ref, x_ref, tab_ref, g_ref, da_ref, carry_ref):
        @pl.when(pl.program_id(2) == 0)
        def _():
            carry_ref[...] = jnp.zeros_like(carry_ref)
            da_ref[...] = jnp.zeros_like(da_ref)
        row = lax.broadcasted_iota(jnp.int32, (SUBLANES, half), 0)
        n_blk = tq // SUBLANES

        def blk(kk, st):
            t = [tab_ref[n * SUBLANES:(n + 1) * SUBLANES, :] for n in range(8)]
            cr, ci, dar, dai = st
            r0 = pl.multiple_of((n_blk - 1 - kk) * SUBLANES, SUBLANES)
            gr = gx_ref[pl.ds(r0, SUBLANES), 0:half]
            gi = gx_ref[pl.ds(r0, SUBLANES), half:cw]
            for s, d in enumerate((1, 2, 4)):
                gr, gi = _cmul_add(gr, gi, t[2 * s], t[2 * s + 1],
                                   pltpu.roll(gr, SUBLANES - d, 0), pltpu.roll(gi, SUBLANES - d, 0))
            gr, gi = _cmul_add(gr, gi, t[6], t[7], cr, ci)
            g_ref[pl.ds(r0, SUBLANES), 0:half] = gr.astype(BF16)
            g_ref[pl.ds(r0, SUBLANES), half:cw] = gi.astype(BF16)
            hr = jnp.where(row == SUBLANES - 1, cr, pltpu.roll(gr, SUBLANES - 1, 0))
            hi = jnp.where(row == SUBLANES - 1, ci, pltpu.roll(gi, SUBLANES - 1, 0))
            xr = x_ref[pl.ds(r0, SUBLANES), 0:half]
            xi = x_ref[pl.ds(r0, SUBLANES), half:cw]
            dar = dar + xr * hr + xi * hi
            dai = dai + xr * hi - xi * hr
            return (jnp.broadcast_to(gr[0:1, :], gr.shape), jnp.broadcast_to(gi[0:1, :], gi.shape), dar, dai)

        st = lax.fori_loop(0, n_blk, blk, (carry_ref[0], carry_ref[1], da_ref[0, :, 0:half], da_ref[0, :, half:cw]))
        carry_ref[0] = st[0]
        carry_ref[1] = st[1]
        da_ref[0, :, 0:half] = st[2]
        da_ref[0, :, half:cw] = st[3]

    return _call(name, body, (n_ex, nj, n_steps),
                 [pl.BlockSpec((tq, cw), lambda b, j, i: (rbr(b, i), j)), pl.BlockSpec((tq, cw), lambda b, j, i: (rbr(b, i), j)),
                  pl.BlockSpec((8 * SUBLANES, half), lambda b, j, i: (0, j))],
                 [pl.BlockSpec((tq, cw), lambda b, j, i: (rbr(b, i), j)), pl.BlockSpec((1, SUBLANES, cw), lambda b, j, i: (b, 0, j))],
                 [_sds((n_rows, width), BF16), _sds((n_ex, SUBLANES, width), F32)],
                 scratch=[pltpu.VMEM((2, SUBLANES, half), F32)])(gx, x, tabs)


def _ssm_z(gy, wout_ref, nj):
    return jnp.concatenate([_dot(gy, wout_ref[j]) for j in range(nj)], axis=1)


def _ssm_out(name, x, u, dskip, cb, wout4, h, n_rows):
    nj, xc, uc = cb.shape
    no, hw, oc = wout4.shape
    d = h.shape[1]
    tm = ROW_TILE

    def body(x_ref, u_ref, ds_ref, cb_ref, w_ref, h_ref, o_ref, y_ref):
        y = jnp.concatenate([_dot(x_ref[:, j * xc:(j + 1) * xc], cb_ref[j]) for j in range(nj)], axis=1)
        y = y + ds_ref[...] * u_ref[...]
        y_ref[...] = y
        z = _ssm_z(_gelu(y), w_ref, no)
        o_ref[...] = h_ref[...] + z[:, :d] * _sigmoid(z[:, d:])

    return _call(name, body, (n_rows // tm,),
                 [pl.BlockSpec((tm, nj * xc), lambda i: (i, 0)), pl.BlockSpec((tm, hw), lambda i: (i, 0)),
                  pl.BlockSpec((1, hw), lambda i: (0, 0)), pl.BlockSpec((nj, xc, uc), lambda i: (0, 0, 0)),
                  pl.BlockSpec((no, hw, oc), lambda i: (0, 0, 0)), pl.BlockSpec((tm, d), lambda i: (i, 0))],
                 [pl.BlockSpec((tm, d), lambda i: (i, 0)), pl.BlockSpec((tm, hw), lambda i: (i, 0))],
                 [_sds((n_rows, d), F32), _sds((n_rows, hw), F32)])(x, u, dskip, cb, wout4, h)


def _ssm_out_bwd(name, dh, y, u, cb, wout4, n_rows):
    nj, xc, uc = cb.shape
    no, hw, oc = wout4.shape
    d = dh.shape[1]
    tm = ROW_TILE

    def body(dh_ref, y_ref, u_ref, cb_ref, w_ref, dy_ref, dz_ref, gx_ref, dd_ref):
        y = y_ref[...]
        z = _ssm_z(_gelu(y), w_ref, no)
        za = z[:, :d]
        sg = _sigmoid(z[:, d:])
        dmix = dh_ref[...]
        dz = jnp.concatenate([dmix * sg, dmix * za * sg * (1.0 - sg)], axis=1).astype(BF16)
        dz_ref[...] = dz
        dgy = _dot_nt(dz[:, 0:oc], w_ref[0])
        for j in range(1, no):
            dgy = dgy + _dot_nt(dz[:, j * oc:(j + 1) * oc], w_ref[j])
        dy = dgy * _gelu_grad(y)
        dy_ref[...] = dy
        _acc_out(dd_ref, jnp.sum(dy * u_ref[...], axis=0, keepdims=True), pl.program_id(0) == 0)
        for j in range(nj):
            gx_ref[:, j * xc:(j + 1) * xc] = _dot_nt(dy[:, j * uc:(j + 1) * uc], cb_ref[j])

    return _call(name, body, (n_rows // tm,),
                 [pl.BlockSpec((tm, d), lambda i: (i, 0)), pl.BlockSpec((tm, hw), lambda i: (i, 0)),
                  pl.BlockSpec((tm, hw), lambda i: (i, 0)), pl.BlockSpec((nj, xc, uc), lambda i: (0, 0, 0)),
                  pl.BlockSpec((no, hw, oc), lambda i: (0, 0, 0))],
                 [pl.BlockSpec((tm, hw), lambda i: (i, 0)), pl.BlockSpec((tm, no * oc), lambda i: (i, 0)),
                  pl.BlockSpec((tm, nj * xc), lambda i: (i, 0)), pl.BlockSpec((1, hw), lambda i: (0, 0))],
                 [_sds((n_rows, hw), F32), _sds((n_rows, no * oc), BF16), _sds((n_rows, nj * xc), F32),
                  _sds((1, hw), F32)])(dh, y, u, cb, wout4)


def _ssm_in_bwd(name, gbu, dy, dskip, bb, w_in, h, g, dh, n_rows):
    nj, uc, xc = bb.shape
    d, hw = w_in.shape
    tm = ROW_TILE

    def body(gb_ref, dy_ref, ds_ref, bb_ref, w_ref, h_ref, g_ref, dh_ref, du_ref, o_ref, dg_ref):
        du = jnp.concatenate([_dot_nt(gb_ref[:, j * xc:(j + 1) * xc], bb_ref[j]) for j in range(nj)], axis=1)
        du = du + dy_ref[...] * ds_ref[...]
        du_ref[...] = du.astype(BF16)
        dhn, dg = _rms_bwd(h_ref[...], g_ref[...], _dot_nt(du, w_ref[...]))
        o_ref[...] = dh_ref[...] + dhn
        _acc_out(dg_ref, dg, pl.program_id(0) == 0)

    return _call(name, body, (n_rows // tm,),
                 [pl.BlockSpec((tm, nj * xc), lambda i: (i, 0)), pl.BlockSpec((tm, hw), lambda i: (i, 0)),
                  pl.BlockSpec((1, hw), lambda i: (0, 0)), pl.BlockSpec((nj, uc, xc), lambda i: (0, 0, 0)),
                  pl.BlockSpec((d, hw), lambda i: (0, 0)), pl.BlockSpec((tm, d), lambda i: (i, 0)),
                  pl.BlockSpec((1, d), lambda i: (0, 0)), pl.BlockSpec((tm, d), lambda i: (i, 0))],
                 [pl.BlockSpec((tm, hw), lambda i: (i, 0)), pl.BlockSpec((tm, d), lambda i: (i, 0)),
                  pl.BlockSpec((1, d), lambda i: (0, 0))],
                 [_sds((n_rows, hw), BF16), _sds((n_rows, d), F32), _sds((1, d), F32)])(gbu, dy, dskip, bb, w_in, h, g, dh)


def _discretize(lam_re, lam_im, log_step, b_re, b_im):
    step = jnp.exp(log_step)[:, None]
    mag = jnp.exp(lam_re * step)
    ar = mag * jnp.cos(lam_im * step)
    ai = mag * jnp.sin(lam_im * step)
    den = lam_re * lam_re + lam_im * lam_im
    nr, ni = ar - 1.0, ai
    cr = (nr * lam_re + ni * lam_im) / den
    ci = (ni * lam_re - nr * lam_im) / den
    bbar_r = cr[..., None] * b_re - ci[..., None] * b_im
    bbar_i = cr[..., None] * b_im + ci[..., None] * b_re
    return ar, ai, bbar_r, bbar_i


def _ssm_mats(lam_re, lam_im, log_step, b_re, b_im, c_re, c_im):
    n_g, n_p, n_c = b_re.shape
    gpc = n_g // 4
    ar, ai, bbar_r, bbar_i = _discretize(lam_re, lam_im, log_step, b_re, b_im)
    eye = jnp.eye(gpc, dtype=F32)

    def in_map(bbar):
        return jnp.einsum('jgpc,gh->jgchp', bbar.reshape(4, gpc, n_p, n_c), eye).reshape(4, gpc * n_c, gpc * n_p)

    def out_map(c):
        return jnp.einsum('jgcp,gh->jgphc', c.reshape(4, gpc, n_c, n_p), eye).reshape(4, gpc * n_p, gpc * n_c)

    bb = jnp.concatenate([in_map(bbar_r), in_map(bbar_i)], axis=2)
    cb = jnp.concatenate([out_map(c_re), -out_map(c_im)], axis=1)
    return bb, cb, ar.reshape(-1), ai.reshape(-1)


def _chunked(v, half):
    return v.reshape(v.shape[:-1] + (4, half))


def _scan_tables(ar, ai, reverse):
    if reverse:
        ai = -ai
    pr, pi = [ar], [ai]
    for _ in range(SUBLANES - 1):
        pr, pi = pr + [pr[-1] * ar - pi[-1] * ai], pi + [pr[-1] * ai + pi[-1] * ar]
    row = jnp.arange(SUBLANES)[:, None]
    tabs = []
    for d in (1, 2, 4):
        keep = (row <= SUBLANES - 1 - d) if reverse else (row >= d)
        tabs += [jnp.where(keep, pr[d - 1][None, :], 0.0), jnp.where(keep, pi[d - 1][None, :], 0.0)]
    order = list(range(SUBLANES))[::-1] if reverse else list(range(SUBLANES))
    tabs += [jnp.stack([pr[k] for k in order]), jnp.stack([pi[k] for k in order])]
    return jnp.concatenate(tabs, axis=0)


def _kv_proj(name, h, g, w_kv, k_gain, cos, sin_s, n_rows, n_kv, hd):
    d, kvw = w_kv.shape
    kw = n_kv * hd
    tm = ROW_TILE

    def body(h_ref, g_ref, w_ref, kg_ref, c_ref, s_ref, raw_ref, k_ref, v_ref):
        raw = _dot(_rms(h_ref[...], g_ref[...]), w_ref[...])
        raw_ref[...] = raw
        ks = _head_prep(raw[:, :kw], kg_ref[...], c_ref[...], s_ref[...], n_kv, hd)
        k_ref[...] = jnp.concatenate(ks, axis=1).astype(BF16)
        v_ref[...] = raw[:, kw:].astype(BF16)

    return _call(name, body, (n_rows // tm,),
                 [pl.BlockSpec((tm, d), lambda i: (i, 0)), pl.BlockSpec((1, d), lambda i: (0, 0)),
                  pl.BlockSpec((d, kvw), lambda i: (0, 0)), pl.BlockSpec((1, hd), lambda i: (0, 0)),
                  pl.BlockSpec((tm, hd), lambda i: (i, 0)), pl.BlockSpec((tm, hd), lambda i: (i, 0))],
                 [pl.BlockSpec((tm, kvw), lambda i: (i, 0)), pl.BlockSpec((tm, kw), lambda i: (i, 0)),
                  pl.BlockSpec((tm, kw), lambda i: (i, 0))],
                 [_sds((n_rows, kvw), F32), _sds((n_rows, kw), BF16), _sds((n_rows, kw), BF16)])(
                     h, g, w_kv, k_gain, cos, sin_s)


def _q_proj(name, h, g, w_q, q_gain, cos, sin_s, n_rows, n_q, hd):
    d, qw = w_q.shape
    tm = ROW_TILE

    def body(h_ref, g_ref, w_ref, qg_ref, c_ref, s_ref, raw_ref, q_ref):
        raw = _dot(_rms(h_ref[...], g_ref[...]), w_ref[...])
        raw_ref[...] = raw
        qs = _head_prep(raw, qg_ref[...], c_ref[...], s_ref[...], n_q, hd)
        q_ref[...] = jnp.concatenate(qs, axis=1).astype(BF16)

    return _call(name, body, (n_rows // tm,),
                 [pl.BlockSpec((tm, d), lambda i: (i, 0)), pl.BlockSpec((1, d), lambda i: (0, 0)),
                  pl.BlockSpec((d, qw), lambda i: (0, 0)), pl.BlockSpec((1, hd), lambda i: (0, 0)),
                  pl.BlockSpec((tm, hd), lambda i: (i, 0)), pl.BlockSpec((tm, hd), lambda i: (i, 0))],
                 [pl.BlockSpec((tm, qw), lambda i: (i, 0)), pl.BlockSpec((tm, qw), lambda i: (i, 0))],
                 [_sds((n_rows, qw), F32), _sds((n_rows, qw), BF16)])(h, g, w_q, q_gain, cos, sin_s)


def _attn_specs(seq, n_ex, n_meta, kw):
    nb = seq // WINDOW
    meta_blk = lambda b: (n_ex * seq + META_BLOCK * b + META_BLOCK - n_meta) // n_meta
    return [pl.BlockSpec((WINDOW, kw), lambda b, n: (b * nb + jnp.maximum(n - 1, 0), 0)),
            pl.BlockSpec((WINDOW, kw), lambda b, n: (b * nb + n, 0)),
            pl.BlockSpec((n_meta, kw), lambda b, n: (meta_blk(b), 0))]


def _attn_mask(n, qpk, n_keys):
    rows = qpk * WINDOW
    qi = lax.broadcasted_iota(jnp.int32, (rows, n_keys), 0) & (WINDOW - 1)
    kj = lax.broadcasted_iota(jnp.int32, (rows, n_keys), 1)
    rel = qi + WINDOW - kj
    band = (rel >= 0) & (rel < WINDOW) & ((n > 0) | (kj >= WINDOW))
    return band | (kj >= 2 * WINDOW)


def _stack_heads(ref, h, qpk, hd, dtype=None):
    parts = [ref[:, (h * qpk + gq) * hd:(h * qpk + gq + 1) * hd] for gq in range(qpk)]
    out = jnp.concatenate(parts, axis=0)
    return out if dtype is None else out.astype(dtype)


def _col(tile, c):
    lane = lax.broadcasted_iota(jnp.int32, tile.shape, 1)
    return jnp.sum(jnp.where(lane == c, tile, 0.0), axis=-1, keepdims=True)


def _put_col(col, c, n):
    lane = lax.broadcasted_iota(jnp.int32, (col.shape[0], n), 1)
    return jnp.where(lane == c, col, 0.0)


def _stack_cols(tile, h, qpk):
    return jnp.concatenate([_col(tile, h * qpk + gq) for gq in range(qpk)], axis=0)


def _sink_col(sinks, h, qpk):
    return jnp.concatenate([jnp.broadcast_to(_col(sinks, h * qpk + gq), (WINDOW, 1)) for gq in range(qpk)], axis=0)


def _attn_fwd(name, q, k, v, sinks, n_ex, seq, n_meta, n_kv, qpk, hd):
    nb = seq // WINDOW
    n_q = n_kv * qpk
    kw = n_kv * hd
    qw = n_q * hd
    n_keys = 2 * WINDOW + n_meta
    scale = hd ** -0.5

    def body(q_ref, kp_ref, kc_ref, km_ref, vp_ref, vc_ref, vm_ref, sk_ref, o_ref, lse_ref):
        valid = _attn_mask(pl.program_id(1), qpk, n_keys)
        sinks_v = sk_ref[...]
        o_parts = []
        lse_all = jnp.zeros((WINDOW, n_q), F32)
        for h in range(n_kv):
            hs = slice(h * hd, (h + 1) * hd)
            kb = jnp.concatenate([kp_ref[:, hs], kc_ref[:, hs], km_ref[:, hs]], axis=0)
            vb = jnp.concatenate([vp_ref[:, hs], vc_ref[:, hs], vm_ref[:, hs]], axis=0)
            s = jnp.where(valid, _dot_nt(_stack_heads(q_ref, h, qpk, hd), kb) * scale, NEG_INF)
            skc = _sink_col(sinks_v, h, qpk)
            m = jnp.maximum(jnp.max(s, axis=-1, keepdims=True), skc)
            p = jnp.exp(s - m)
            den = jnp.sum(p, axis=-1, keepdims=True) + jnp.exp(skc - m)
            o = _dot(p, vb) / den
            lse = m + jnp.log(den)
            for gq in range(qpk):
                o_parts.append(o[gq * WINDOW:(gq + 1) * WINDOW])
                lse_all = lse_all + _put_col(lse[gq * WINDOW:(gq + 1) * WINDOW], h * qpk + gq, n_q)
        o_ref[...] = jnp.concatenate(o_parts, axis=1)
        lse_ref[...] = lse_all

    qspec = pl.BlockSpec((WINDOW, qw), lambda b, n: (b * nb + n, 0))
    return _call(name, body, (n_ex, nb),
                 [qspec] + _attn_specs(seq, n_ex, n_meta, kw) + _attn_specs(seq, n_ex, n_meta, kw)
                 + [pl.BlockSpec((1, n_q), lambda b, n: (0, 0))],
                 [qspec, pl.BlockSpec((WINDOW, n_q), lambda b, n: (b * nb + n, 0))],
                 [_sds((n_ex * seq, qw), F32), _sds((n_ex * seq, n_q), F32)])(q, k, k, k, v, v, v, sinks)


def _attn_bwd(name, q, k, v, sinks, o, lse, do, n_ex, seq, n_meta, n_kv, qpk, hd):
    nb = seq // WINDOW
    n_q = n_kv * qpk
    kw = n_kv * hd
    qw = n_q * hd
    n_keys = 2 * WINDOW + n_meta
    scale = hd ** -0.5

    def body(q_ref, kp_ref, kc_ref, km_ref, vp_ref, vc_ref, vm_ref, sk_ref, o_ref, lse_ref, do_ref,
             dq_ref, dk_ref, dv_ref, dkm_ref, dvm_ref, dsk_ref):
        n = pl.program_id(1)

        @pl.when(n == 0)
        def _():
            dk_ref[...] = jnp.zeros_like(dk_ref)
            dv_ref[...] = jnp.zeros_like(dv_ref)
            dkm_ref[...] = jnp.zeros_like(dkm_ref)
            dvm_ref[...] = jnp.zeros_like(dvm_ref)

        @pl.when((n == 0) & (pl.program_id(0) == 0))
        def _():
            dsk_ref[...] = jnp.zeros_like(dsk_ref)

        valid = _attn_mask(n, qpk, n_keys)
        sinks_v = sk_ref[...]
        lse_v = lse_ref[...]
        dq_parts, dk_parts, dv_parts = [], [], []
        dsk = jnp.zeros((1, n_q), F32)
        for h in range(n_kv):
            hs = slice(h * hd, (h + 1) * hd)
            kb = jnp.concatenate([kp_ref[:, hs], kc_ref[:, hs], km_ref[:, hs]], axis=0)
            vb = jnp.concatenate([vp_ref[:, hs], vc_ref[:, hs], vm_ref[:, hs]], axis=0)
            qs = _stack_heads(q_ref, h, qpk, hd)
            dos = _stack_heads(do_ref, h, qpk, hd)
            delta = jnp.sum(dos * _stack_heads(o_ref, h, qpk, hd), axis=-1, keepdims=True)
            lse_c = _stack_cols(lse_v, h, qpk)
            s = jnp.where(valid, _dot_nt(qs, kb) * scale, NEG_INF)
            p = jnp.exp(s - lse_c)
            ds = p * (_dot_nt(dos, vb) - delta)
            dqs = _dot(ds, kb) * scale
            dk_parts.append(_dot_tn(ds, qs) * scale)
            dv_parts.append(_dot_tn(p, dos))
            dsink = -jnp.exp(_sink_col(sinks_v, h, qpk) - lse_c) * delta
            for gq in range(qpk):
                dq_parts.append(dqs[gq * WINDOW:(gq + 1) * WINDOW])
                dsk = dsk + _put_col(jnp.sum(dsink[gq * WINDOW:(gq + 1) * WINDOW], axis=0, keepdims=True), h * qpk + gq, n_q)
        dq_ref[...] = jnp.concatenate(dq_parts, axis=1)
        dsk_ref[...] += dsk
        dkb = jnp.concatenate(dk_parts, axis=1)
        dvb = jnp.concatenate(dv_parts, axis=1)
        prev = pl.ds(pl.multiple_of(jnp.maximum(n - 1, 0) * WINDOW, WINDOW), WINDOW)
        cur = pl.ds(pl.multiple_of(n * WINDOW, WINDOW), WINDOW)
        dk_ref[prev, :] += dkb[0:WINDOW]
        dv_ref[prev, :] += dvb[0:WINDOW]
        dk_ref[cur, :] += dkb[WINDOW:2 * WINDOW]
        dv_ref[cur, :] += dvb[WINDOW:2 * WINDOW]
        dkm_ref[...] += dkb[2 * WINDOW:]
        dvm_ref[...] += dvb[2 * WINDOW:]

    qspec = pl.BlockSpec((WINDOW, qw), lambda b, n: (b * nb + n, 0))
    exspec = pl.BlockSpec((seq, kw), lambda b, n: (b, 0))
    mspec = pl.BlockSpec((n_meta, kw), lambda b, n: (b, 0))
    return _call(name, body, (n_ex, nb),
                 [qspec] + _attn_specs(seq, n_ex, n_meta, kw) + _attn_specs(seq, n_ex, n_meta, kw)
                 + [pl.BlockSpec((1, n_q), lambda b, n: (0, 0)), qspec,
                    pl.BlockSpec((WINDOW, n_q), lambda b, n: (b * nb + n, 0)), qspec],
                 [qspec, exspec, exspec, mspec, mspec, pl.BlockSpec((1, n_q), lambda b, n: (0, 0))],
                 [_sds((n_ex * seq, qw), F32), _sds((n_ex * seq, kw), F32), _sds((n_ex * seq, kw), F32),
                  _sds((n_ex * n_meta, kw), F32), _sds((n_ex * n_meta, kw), F32), _sds((1, n_q), F32)])(
                      q, k, k, k, v, v, v, sinks, o, lse, do)


def _attn_out(name, o, h, w_o, n_rows):
    qw, d = w_o.shape
    tm = ROW_TILE

    def body(o_ref, h_ref, w_ref, out_ref):
        out_ref[...] = h_ref[...] + _dot(o_ref[...], w_ref[...])

    return _call(name, body, (n_rows // tm,),
                 [pl.BlockSpec((tm, qw), lambda i: (i, 0)), pl.BlockSpec((tm, d), lambda i: (i, 0)),
                  pl.BlockSpec((qw, d), lambda i: (0, 0))],
                 pl.BlockSpec((tm, d), lambda i: (i, 0)), _sds((n_rows, d), F32))(o, h, w_o)


def _attn_out_bwd(name, dh, w_o, n_rows):
    qw, d = w_o.shape
    tm = ROW_TILE

    def body(dh_ref, w_ref, do_ref):
        do_ref[...] = _dot_nt(dh_ref[...], w_ref[...])

    return _call(name, body, (n_rows // tm,),
                 [pl.BlockSpec((tm, d), lambda i: (i, 0)), pl.BlockSpec((qw, d), lambda i: (0, 0))],
                 pl.BlockSpec((tm, qw), lambda i: (i, 0)), _sds((n_rows, qw), F32))(dh, w_o)


def _q_bwd(name, dq, qraw, q_gain, cos, sin_s, w_q, h, g, dh, n_rows, n_q, hd):
    d, qw = w_q.shape
    tm = ROW_TILE

    def body(dq_ref, raw_ref, qg_ref, c_ref, s_ref, w_ref, h_ref, g_ref, dh_ref, draw_ref, o_ref, dqg_ref, dg_ref):
        dxs, dgain = _head_prep_bwd(raw_ref[...], qg_ref[...], c_ref[...], s_ref[...], dq_ref[...], n_q, hd)
        draw = jnp.concatenate(dxs, axis=1).astype(BF16)
        draw_ref[...] = draw
        dhn, dg = _rms_bwd(h_ref[...], g_ref[...], _dot_nt(draw, w_ref[...]))
        o_ref[...] = dh_ref[...] + dhn
        first = pl.program_id(0) == 0
        _acc_out(dqg_ref, dgain, first)
        _acc_out(dg_ref, dg, first)

    row = lambda w: pl.BlockSpec((tm, w), lambda i: (i, 0))
    one = lambda w: pl.BlockSpec((1, w), lambda i: (0, 0))
    return _call(name, body, (n_rows // tm,),
                 [row(qw), row(qw), one(hd), row(hd), row(hd), pl.BlockSpec((d, qw), lambda i: (0, 0)), row(d), one(d), row(d)],
                 [row(qw), row(d), one(hd), one(d)],
                 [_sds((n_rows, qw), BF16), _sds((n_rows, d), F32), _sds((1, hd), F32), _sds((1, d), F32)])(
                     dq, qraw, q_gain, cos, sin_s, w_q, h, g, dh)


def _kv_bwd(name, dk, dv, kvraw, k_gain, cos, sin_s, w_kv, h, g, dh_main, n_rows, n_main, n_kv, hd):
    d, kvw = w_kv.shape
    kw = n_kv * hd
    tm = ROW_TILE
    n_main_tiles = n_main // tm

    def body(dk_ref, dv_ref, raw_ref, kg_ref, c_ref, s_ref, w_ref, h_ref, g_ref, dh_ref, draw_ref, o_ref, dkg_ref, dg_ref):
        i = pl.program_id(0)
        dxs, dgain = _head_prep_bwd(raw_ref[:, :kw], kg_ref[...], c_ref[...], s_ref[...], dk_ref[...], n_kv, hd)
        draw = jnp.concatenate(dxs + [dv_ref[...]], axis=1).astype(BF16)
        draw_ref[...] = draw
        dhn, dg = _rms_bwd(h_ref[...], g_ref[...], _dot_nt(draw, w_ref[...]))
        o_ref[...] = jnp.where(i < n_main_tiles, dh_ref[...], 0.0) + dhn
        _acc_out(dkg_ref, dgain, i == 0)
        _acc_out(dg_ref, dg, i == 0)

    row = lambda w: pl.BlockSpec((tm, w), lambda i: (i, 0))
    one = lambda w: pl.BlockSpec((1, w), lambda i: (0, 0))
    return _call(name, body, (n_rows // tm,),
                 [row(kw), row(kw), row(kvw), one(hd), row(hd), row(hd), pl.BlockSpec((d, kvw), lambda i: (0, 0)), row(d),
                  one(d), pl.BlockSpec((tm, d), lambda i: (jnp.minimum(i, n_main_tiles - 1), 0))],
                 [row(kvw), row(d), one(hd), one(d)],
                 [_sds((n_rows, kvw), BF16), _sds((n_rows, d), F32), _sds((1, hd), F32), _sds((1, d), F32)])(
                     dk, dv, kvraw, k_gain, cos, sin_s, w_kv, h, g, dh_main)


def _tn_rms(name, h, g, b, n_rows):
    d = h.shape[1]
    nb = b.shape[1]
    tk = _contract_tile(n_rows)
    return _tn(name, (h, g, b),
               [pl.BlockSpec((tk, d), lambda j, k: (k, 0)), pl.BlockSpec((1, d), lambda j, k: (0, 0)),
                pl.BlockSpec((tk, nb), lambda j, k: (k, 0))],
               lambda j, h_ref, g_ref, b_ref: (_rms(h_ref[...], g_ref[...]), b_ref[...]), 1, d, nb, n_rows, tk)


def _tn_plain(name, a, b, nj, a_cols, b_cols, n_rows, a_fn=None):
    tk = _contract_tile(n_rows)
    fa = (lambda v: v) if a_fn is None else a_fn
    a_map = (lambda j, k: (k, j)) if a.shape[1] != a_cols else (lambda j, k: (k, 0))
    b_map = (lambda j, k: (k, j)) if b.shape[1] != b_cols else (lambda j, k: (k, 0))
    return _tn(name, (a, b), [pl.BlockSpec((tk, a_cols), a_map), pl.BlockSpec((tk, b_cols), b_map)],
               lambda j, a_ref, b_ref: (fa(a_ref[...]), b_ref[...]), nj, a_cols, b_cols, n_rows, tk)


def _loss_head(name, y, target, n_rows):
    d = y.shape[1]
    tm = ROW_TILE

    def body(y_ref, t_ref, dy_ref, l_ref):
        e = y_ref[...] - t_ref[...]
        dy_ref[...] = e * (1.0 / d)
        e2 = jnp.sum((e * e).reshape(tm // SUBLANES, SUBLANES, d), axis=0)
        part = e2[:, 0:128]
        for k in range(1, d // 128):
            part = part + e2[:, k * 128:(k + 1) * 128]
        _acc_out(l_ref, part * (0.5 / d), pl.program_id(0) == 0)

    return _call(name, body, (n_rows // tm,),
                 [pl.BlockSpec((tm, d), lambda i: (i, 0)), pl.BlockSpec((tm, d), lambda i: (i, 0))],
                 [pl.BlockSpec((tm, d), lambda i: (i, 0)), pl.BlockSpec((SUBLANES, 128), lambda i: (0, 0))],
                 [_sds((n_rows, d), F32), _sds((SUBLANES, 128), F32)])(y, target)


def _cast_layer(name, a, layer):
    _, r, c = a.shape
    tr = _row_tile(r, 256)

    def body(a_ref, o_ref):
        o_ref[...] = a_ref[0].astype(BF16)

    return _call(name, body, (r // tr,), [pl.BlockSpec((1, tr, c), lambda i: (layer, i, 0))],
                 pl.BlockSpec((tr, c), lambda i: (i, 0)), _sds((r, c), BF16))(a)


def _adamw_math(w, g, m, v):
    c1 = 1.0 - ADAM_B1 ** ADAM_STEP
    c2 = 1.0 - ADAM_B2 ** ADAM_STEP
    nm = ADAM_B1 * m + (1.0 - ADAM_B1) * g
    nv = ADAM_B2 * v + (1.0 - ADAM_B2) * (g * g)
    return -ADAM_LR * ((nm / c1) / (jnp.sqrt(nv / c2) + ADAM_EPS) + ADAM_WD * w), nm, nv


def _adamw(name, w, g, m, v):
    rows, cols = w.shape
    tr = 128 if rows % 128 == 0 else rows

    def body(w_ref, g_ref, m_ref, v_ref, d_ref, nm_ref, nv_ref):
        d_ref[...], nm_ref[...], nv_ref[...] = _adamw_math(w_ref[...], g_ref[...], m_ref[...], v_ref[...])

    spec = pl.BlockSpec((tr, cols), lambda i: (i, 0))
    return _call(name, body, (rows // tr,), [spec] * 4, [spec] * 3, [_sds((rows, cols), F32)] * 3)(w, g, m, v)


def _position():
    return lax.axis_index("x"), lax.axis_index("y"), lax.axis_index("c")


def _other_chips(x, y):
    return [(1 - x, y), (x, 1 - y), (1 - x, 1 - y)]


def _peers_chips(x, y, c):
    return [(cx, cy, c) for cx, cy in _other_chips(x, y)]


def _peers_sibling(x, y, c):
    return [(x, y, 1 - c)]


def _peers_chips_and_sibling(x, y, c):
    return _peers_chips(x, y, c) + _peers_sibling(x, y, c)


def _comm_call(name, body, n_in, out_shape, scratch, sequencer=None):
    if sequencer is None:
        return pl.pallas_call(
            body, name=name, in_specs=[_HBM] * n_in, out_specs=[_HBM] * len(out_shape), out_shape=out_shape,
            scratch_shapes=list(scratch),
            compiler_params=pltpu.CompilerParams(has_side_effects=True, vmem_limit_bytes=V7X_VMEM_LIMIT))
    collective_id, peers = sequencer

    def seq_body(*refs):
        barrier = pltpu.get_barrier_semaphore()
        plist = peers(*_position())
        for peer in plist:
            pl.semaphore_signal(barrier, inc=1, device_id=peer, device_id_type=MESH)
        pl.semaphore_wait(barrier, len(plist))
        body(*refs)

    return pl.kernel(seq_body, out_type=out_shape, mesh=plsc.ScalarSubcoreMesh(axis_name="sequencer", num_cores=1), name=name,
                     scratch_types=list(scratch), compiler_params=pltpu.CompilerParams(collective_id=collective_id))


def _n_chunks(rows, want, dtype):
    align = 16 if dtype == BF16 else 8
    n = want
    while n > 1 and (rows % n or (rows // n) % align):
        n -= 1
    return n


def _remote(src, dst, send_sem, recv_sem, device):
    return pltpu.make_async_remote_copy(src_ref=src, dst_ref=dst, send_sem=send_sem, recv_sem=recv_sem,
                                        device_id=device, device_id_type=MESH)


def _start_in_chunks(src, dst, send_sem, recv_sem, device, want=8):
    rows = src.shape[0]
    n = _n_chunks(rows, want, src.dtype)
    for i in range(n):
        part = pl.ds(i * (rows // n), rows // n)
        _remote(src.at[part], dst.at[part], send_sem, recv_sem, device).start()


def _all_gather_chips(name, shards, split, collective_id=None):
    n = len(shards)

    def body(*refs):
        ins, outs = refs[:n], refs[n:2 * n]
        send_sems, recv_sems, local_sems = refs[2 * n:]
        x, y, c = _position()
        me = 2 * x + y
        chips = _other_chips(x, y)
        sibling = (x, y, 1 - c)
        sends, forwards = [], []
        for t in range(n):
            pltpu.make_async_copy(ins[t], outs[t].at[me], local_sems.at[t]).start()
        for t in range(n):
            r = ins[t].shape[0]
            rows = pl.ds(c * (r // 2), r // 2) if split[t] else pl.ds(0, r)
            for k, (cx, cy) in enumerate(chips):
                src, dst = ins[t].at[rows], outs[t].at[me, rows]
                _start_in_chunks(src, dst, send_sems.at[t, k], recv_sems.at[t, k], (cx, cy, c), want=4)
                sends.append(_remote(src, dst, send_sems.at[t, k], recv_sems.at[t, k], (cx, cy, c)))
        for t in range(n):
            r = ins[t].shape[0]
            rows = pl.ds(c * (r // 2), r // 2) if split[t] else pl.ds(0, r)
            for k, (cx, cy) in enumerate(chips):
                landed = outs[t].at[2 * cx + cy, rows]
                _remote(landed, landed, send_sems.at[t, k], recv_sems.at[t, k], (cx, cy, c)).wait_recv()
                if split[t]:
                    _start_in_chunks(landed, landed, send_sems.at[t, 3 + k], recv_sems.at[t, 3 + k], sibling, want=4)
                    forwards.append(_remote(landed, landed, send_sems.at[t, 3 + k], recv_sems.at[t, 3 + k], sibling))
        for t in range(n):
            if split[t]:
                r = ins[t].shape[0]
                other = pl.ds((1 - c) * (r // 2), r // 2)
                for k, (cx, cy) in enumerate(chips):
                    landed = outs[t].at[2 * cx + cy, other]
                    pltpu.make_async_remote_copy(
                        src_ref=landed, dst_ref=landed, send_sem=send_sems.at[t, 3 + k], recv_sem=recv_sems.at[t, 3 + k],
                        device_id=sibling, device_id_type=MESH).wait_recv()
        for cp in sends + forwards:
            cp.wait_send()
        for t in range(n):
            pltpu.make_async_copy(ins[t], outs[t].at[me], local_sems.at[t]).wait()

    out_shape = [_sds((N_CHIPS,) + s.shape, s.dtype) for s in shards]
    sequencer = None if collective_id is None else (collective_id, _peers_chips_and_sibling)
    return _comm_call(name, body, n, out_shape,
                      [pltpu.SemaphoreType.DMA((n, 6)), pltpu.SemaphoreType.DMA((n, 6)), pltpu.SemaphoreType.DMA((n,))],
                      sequencer)(*shards)


def _swap_halves_with_sibling(name, blob, collective_id=None):
    def body(b_ref, theirs_ref, send_sem, recv_sem):
        x, y, c = _position()
        sibling = (x, y, 1 - c)
        for k in range(b_ref.shape[1]):
            _start_in_chunks(b_ref.at[1 - c, k], theirs_ref.at[k], send_sem, recv_sem, sibling)
        _remote(b_ref.at[1 - c], theirs_ref, send_sem, recv_sem, sibling).wait()

    return _comm_call(name, body, 1, [_sds(blob.shape[1:], blob.dtype)],
                      [pltpu.SemaphoreType.DMA(()), pltpu.SemaphoreType.DMA(())],
                      None if collective_id is None else (collective_id, _peers_sibling))(blob)[0]


def _scatter_to_chips(name, parts, collective_id=None):
    def body(p_ref, o_ref, send_sems, recv_sems, local_sems):
        x, y, c = _position()
        me = 2 * x + y
        rows = p_ref.shape[1]
        n_loc = _n_chunks(rows, 16, p_ref.dtype)
        locs = [pltpu.make_async_copy(p_ref.at[me, pl.ds(i * (rows // n_loc), rows // n_loc)],
                                      o_ref.at[me, pl.ds(i * (rows // n_loc), rows // n_loc)], local_sems.at[i])
                for i in range(n_loc)]
        for loc in locs:
            loc.start()
        sends = []
        for k, (cx, cy) in enumerate(_other_chips(x, y)):
            src, dst = p_ref.at[2 * cx + cy], o_ref.at[me]
            _start_in_chunks(src, dst, send_sems.at[k], recv_sems.at[k], (cx, cy, c))
            sends.append(_remote(src, dst, send_sems.at[k], recv_sems.at[k], (cx, cy, c)))
        for k, (cx, cy) in enumerate(_other_chips(x, y)):
            landed = o_ref.at[2 * cx + cy]
            _remote(landed, landed, send_sems.at[k], recv_sems.at[k], (cx, cy, c)).wait_recv()
        for cp in sends:
            cp.wait_send()
        for loc in locs:
            loc.wait()

    def local_sems_shape(rows):
        return pltpu.SemaphoreType.DMA((_n_chunks(rows, 16, parts.dtype),))

    return _comm_call(name, body, 1, [_sds(parts.shape, parts.dtype)],
                      [pltpu.SemaphoreType.DMA((3,)), pltpu.SemaphoreType.DMA((3,)), local_sems_shape(parts.shape[1])],
                      None if collective_id is None else (collective_id, _peers_chips))(parts)[0]


def _share_with_sibling(name, mine, collective_id=None):
    def body(m_ref, o_ref, send_sem, recv_sem):
        x, y, c = _position()
        sibling = (x, y, 1 - c)
        _start_in_chunks(m_ref, o_ref, send_sem, recv_sem, sibling, want=16)
        _remote(m_ref, o_ref, send_sem, recv_sem, sibling).wait()

    return _comm_call(name, body, 1, [_sds(mine.shape, mine.dtype)],
                      [pltpu.SemaphoreType.DMA(()), pltpu.SemaphoreType.DMA(())],
                      None if collective_id is None else (collective_id, _peers_sibling))(mine)[0]


def _row_tile(rows, cap=640):
    best = rows
    for t in range(16, min(rows, cap) + 1, 16):
        if rows % t == 0:
            best = t
    return best


def _add_my_half(name, blob, theirs, half_index, out_dtype):
    n, rows, cols = theirs.shape
    tr = _row_tile(rows)

    def body(c_ref, a_ref, b_ref, o_ref):
        o_ref[...] = (a_ref[0].astype(F32) + b_ref[...].astype(F32)).astype(out_dtype)

    spec = pl.BlockSpec((1, tr, cols), lambda k, i, c: (k, i, 0))
    grid_spec = pltpu.PrefetchScalarGridSpec(
        num_scalar_prefetch=1, grid=(n, rows // tr),
        in_specs=[pl.BlockSpec((1, 1, tr, cols), lambda k, i, c: (c[0], k, i, 0)), spec], out_specs=spec)
    return pl.pallas_call(
        body, name=name, grid_spec=grid_spec, out_shape=_sds(theirs.shape, out_dtype),
        compiler_params=pltpu.CompilerParams(dimension_semantics=("arbitrary", "arbitrary"),
                                             vmem_limit_bytes=V7X_VMEM_LIMIT))(half_index, blob, theirs)


def _sum_slots(name, parts):
    n, rows, cols = parts.shape
    tr = _row_tile(rows)

    def body(p_ref, o_ref):
        acc = p_ref[0].astype(F32)
        for k in range(1, n):
            acc = acc + p_ref[k].astype(F32)
        o_ref[...] = acc

    return _call(name, body, (rows // tr,), [pl.BlockSpec((n, tr, cols), lambda i: (0, i, 0))],
                 pl.BlockSpec((tr, cols), lambda i: (i, 0)), _sds((rows, cols), F32))(parts)


def _reduce_small_adamw(name, grads, loss_tile, ws, ms, vs):
    n = len(grads)
    srcs = list(grads) + [loss_tile]

    def body(*refs):
        g_in, w_in, m_in, v_in = refs[:n + 1], refs[n + 1:2 * n + 1], refs[2 * n + 1:3 * n + 1], refs[3 * n + 1:4 * n + 1]
        outs = refs[4 * n + 1:8 * n + 2]
        g_out, d_out, nm_out, nv_out, loss_out = outs[:n], outs[n:2 * n], outs[2 * n:3 * n], outs[3 * n:4 * n], outs[4 * n]
        bufs = refs[8 * n + 2:9 * n + 3]
        send_sems, recv_sems = refs[9 * n + 3:]
        x, y, c = _position()
        me = 4 * x + 2 * y + c
        chip = 2 * x + y
        peers = [(1 - x if dlt & 4 else x, 1 - y if dlt & 2 else y, 1 - c if dlt & 1 else c) for dlt in range(1, N_DEV)]
        sends = []
        for t in range(n + 1):
            bufs[t][me] = g_in[t][...]
            for k, peer in enumerate(peers):
                cp = _remote(g_in[t], bufs[t].at[me], send_sems.at[t, k], recv_sems.at[t, k], peer)
                cp.start()
                sends.append(cp)
        for t in range(n + 1):
            for k, (tx, ty, tc) in enumerate(peers):
                landed = bufs[t].at[4 * tx + 2 * ty + tc]
                _remote(landed, landed, send_sems.at[t, k], recv_sems.at[t, k], (tx, ty, tc)).wait_recv()
        for cp in sends:
            cp.wait_send()
        for t in range(n + 1):
            total = bufs[t][0]
            for k in range(1, N_DEV):
                total = total + bufs[t][k]
            if t == n:
                loss_out[...] = total
                continue
            cols = w_in[t].shape[1]
            if cols == total.shape[1]:
                g_out[t][...] = total
                d_out[t][...], nm_out[t][...], nv_out[t][...] = _adamw_math(w_in[t][...], total, m_in[t][...], v_in[t][...])
            else:
                for j in range(N_CHIPS):
                    @pl.when(chip == j)
                    def _(t=t, j=j, cols=cols, total=total):
                        mine = total[:, j * cols:(j + 1) * cols]
                        g_out[t][...] = mine
                        d_out[t][...], nm_out[t][...], nv_out[t][...] = _adamw_math(w_in[t][...], mine, m_in[t][...], v_in[t][...])

    w_shapes = [_sds(a.shape, F32) for a in ws]
    return pl.pallas_call(
        body, name=name, in_specs=[_VMEM] * (4 * n + 1), out_specs=[_VMEM] * (4 * n + 1),
        out_shape=w_shapes * 4 + [_sds(loss_tile.shape, F32)],
        scratch_shapes=[pltpu.VMEM((N_DEV,) + a.shape, F32) for a in srcs]
        + [pltpu.SemaphoreType.DMA((n + 1, N_DEV - 1)), pltpu.SemaphoreType.DMA((n + 1, N_DEV - 1))],
        compiler_params=pltpu.CompilerParams(has_side_effects=True, vmem_limit_bytes=V7X_VMEM_LIMIT))(
            *srcs, *ws, *ms, *vs)


_BIG = ("ffn1_w_gate_up", "ffn1_w_down", "ffn2_w_gate_up", "ffn2_w_down", "ssm_w_in", "ssm_w_out", "w_kv", "attn_w_q", "attn_w_o")
_TRANSPOSED = ("ffn1_w_gate_up", "ffn2_w_gate_up")
_SMALL = ("meta_tokens", "ffn1_norm", "mix_norm", "ffn2_norm", "ssm_lambda_re", "ssm_lambda_im", "ssm_b_re", "ssm_b_im",
          "ssm_c_re", "ssm_c_im", "ssm_log_step", "ssm_d", "kv_norm", "k_norm", "q_norm", "attn_sinks")
_ORDER = ("meta_tokens", "ffn1_norm", "ffn1_w_gate_up", "ffn1_w_down", "mix_norm", "ffn2_norm", "ffn2_w_gate_up", "ffn2_w_down",
          "ssm_w_in", "ssm_lambda_re", "ssm_lambda_im", "ssm_b_re", "ssm_b_im", "ssm_c_re", "ssm_c_im", "ssm_log_step", "ssm_d",
          "ssm_w_out", "kv_norm", "w_kv", "k_norm", "attn_w_q", "q_norm", "attn_sinks", "attn_w_o")


def _step(x, target, w, m, v):
    n_ex, seq, d = x.shape
    n_meta = w["meta_tokens"].shape[0]
    n_main = n_ex * seq
    n_all = n_main + n_ex * META_BLOCK
    n_g, n_p, n_c = w["ssm_b_re"].shape[1:]
    hd = w["k_norm"].shape[0]
    n_kv = w["w_kv"].shape[1] // (2 * hd)
    n_q = w["attn_w_q"].shape[2] // hd
    qpk = n_q // n_kv
    px, py, pc = _position()
    chip = 2 * px + py

    def cast(name, layer=0):
        a = w[name]
        return _cast_layer(f"cast_{name}_{layer}", a if a.ndim == 3 else a[None], layer)

    first = [cast("ffn1_w_gate_up"), cast("ffn1_w_down"), cast("ssm_w_in"), cast("ssm_w_out"), w["meta_tokens"], w["ssm_d"]]
    g_a = _all_gather_chips("gather_first", first, [True, True, True, True, False, False], collective_id=12)
    second = [cast("ffn2_w_gate_up"), cast("ffn2_w_down"), cast("w_kv")]
    g_b = _all_gather_chips("gather_second", second, [True] * 3, collective_id=1)
    third = [cast("ffn1_w_gate_up", 1), cast("ffn1_w_down", 1), cast("attn_w_q"), cast("attn_w_o"),
             cast("ffn2_w_gate_up", 1), cast("ffn2_w_down", 1)]
    g_c = _all_gather_chips("gather_third", third, [True] * 6, collective_id=2)
    wgu = {("ffn1", 0): g_a[0], ("ffn1", 1): g_c[0], ("ffn2", 0): g_b[0], ("ffn2", 1): g_c[4]}
    wd = {("ffn1", 0): g_a[1], ("ffn1", 1): g_c[1], ("ffn2", 0): g_b[1], ("ffn2", 1): g_c[5]}
    wd = {key: a.reshape(-1, d) for key, a in wd.items()}
    w_in = g_a[2].reshape(d, -1)
    wout4 = g_a[3]
    w_q = g_c[2].reshape(d, -1)
    w_o = g_c[3].reshape(-1, d)
    w_kv = g_b[2].reshape(d, -1)
    meta_full = jnp.transpose(g_a[4], (1, 0, 2)).reshape(n_meta, d)
    dskip = g_a[5].reshape(1, -1)

    row1 = lambda a: a.reshape(1, -1)
    ssm_args = tuple(w[k][0] for k in ("ssm_lambda_re", "ssm_lambda_im", "ssm_log_step", "ssm_b_re", "ssm_b_im", "ssm_c_re", "ssm_c_im"))
    (bb, cb, a_re, a_im), ssm_vjp = jax.vjp(_ssm_mats, *ssm_args)
    bb16, cb16 = bb.astype(BF16), cb.astype(BF16)
    a_re_s, a_im_s = lax.stop_gradient(a_re), lax.stop_gradient(a_im)
    half = n_g * n_p // 4
    tabs_f = _scan_tables(a_re_s, a_im_s, False)
    tabs_b = _scan_tables(a_re_s, a_im_s, True)

    freqs = ROPE_THETA ** (-jnp.arange(0, hd // 2, dtype=F32) * 2.0 / hd)
    pos_main = jnp.tile(n_meta + jnp.arange(seq), n_ex)
    pos_meta = jnp.tile(jnp.maximum(jnp.arange(META_BLOCK) - (META_BLOCK - n_meta), 0), n_ex)
    ang = jnp.concatenate([pos_main, pos_meta]).astype(F32)[:, None] * freqs[None, :]
    cos = jnp.concatenate([jnp.cos(ang), jnp.cos(ang)], axis=1)
    sin_s = jnp.concatenate([-jnp.sin(ang), jnp.sin(ang)], axis=1)

    meta_block = jnp.concatenate([jnp.zeros((META_BLOCK - n_meta, d), F32), meta_full], axis=0)
    h0 = jnp.concatenate([x.reshape(n_main, d)] + [meta_block] * n_ex, axis=0)

    g = lambda name, layer: row1(w[name][layer])
    h1, gu1 = _ffn_fwd("l0_ffn1", h0, g("ffn1_norm", 0), wgu["ffn1", 0], wd["ffn1", 0], n_all)
    u, bu = _ssm_in("ssm_in", h1, g("mix_norm", 0), w_in, bb16, n_all)
    xs = _scan_fwd("ssm_scan", bu, tabs_f, n_ex, seq)
    h2, y = _ssm_out("ssm_out", xs, u, dskip, cb16, wout4, h1, n_all)
    h3, gu2 = _ffn_fwd("l0_ffn2", h2, g("ffn2_norm", 0), wgu["ffn2", 0], wd["ffn2", 0], n_all)
    kvraw, k, vv = _kv_proj("kv_proj", h3, row1(w["kv_norm"]), w_kv, row1(w["k_norm"]), cos, sin_s, n_all, n_kv, hd)
    h4, gu3 = _ffn_fwd("l1_ffn1", h3, g("ffn1_norm", 1), wgu["ffn1", 1], wd["ffn1", 1], n_main)
    qraw, q = _q_proj("q_proj", h4, g("mix_norm", 1), w_q, row1(w["q_norm"][0]), cos, sin_s, n_main, n_q, hd)
    sinks = row1(w["attn_sinks"][0])
    o, lse = _attn_fwd("attn_fwd", q, k, vv, sinks, n_ex, seq, n_meta, n_kv, qpk, hd)
    h5 = _attn_out("attn_out", o, h4, w_o, n_main)
    h6, gu4 = _ffn_fwd("l1_ffn2", h5, g("ffn2_norm", 1), wgu["ffn2", 1], wd["ffn2", 1], n_main)
    dh6, loss_tile = _loss_head("loss_head", h6, target.reshape(n_main, d), n_main)

    lanes = 1024

    def reduce_scatter(tag, entries, collective_ids):
        ids = collective_ids or (None, None, None)
        pieces = [gr.reshape(N_CHIPS, 2, -1, lanes) for _, _, gr in entries]
        blob = jnp.transpose(jnp.concatenate(pieces, axis=2), (1, 0, 2, 3)).astype(BF16)
        theirs = _swap_halves_with_sibling(tag + "_swap", blob, ids[0])
        chip_sum = _add_my_half(tag + "_chip_sum", blob, theirs, jnp.reshape(pc, (1,)).astype(jnp.int32), BF16)
        landed = _scatter_to_chips(tag + "_scatter", chip_sum, ids[1])
        total = _sum_slots(tag + "_sum", landed)
        other = _share_with_sibling(tag + "_share", total, ids[2])
        halves = (jnp.where(pc == 0, total, other), jnp.where(pc == 0, other, total))
        out, off = {}, 0
        for name, layer, gr in entries:
            rows = gr.shape[1] * gr.shape[2] // lanes // 2
            flat = jnp.concatenate([hv[off:off + rows].reshape(-1) for hv in halves])
            if name in _TRANSPOSED:
                flat = flat.reshape(gr.shape[1], gr.shape[2]).T.reshape(-1)
            out[name, layer] = flat
            off += rows
        return out

    small = {}
    dh5, dg_f2l1, dwgu_f2l1, dwd_f2l1 = _ffn_bwd("l1_ffn2", dh6, h5, g("ffn2_norm", 1), gu4, wgu["ffn2", 1], wd["ffn2", 1], n_main)
    do = _attn_out_bwd("attn_out_bwd", dh5, w_o, n_main)
    dw_o = _tn_plain("attn_dwo", o, dh5, 1, o.shape[1], d, n_main).reshape(N_CHIPS, -1, d)
    dq, dk_main, dv_main, dk_meta, dv_meta, dsinks = _attn_bwd("attn_bwd", q, k, vv, sinks, o, lse, do, n_ex, seq, n_meta, n_kv, qpk, hd)
    dqraw, dh4, dq_gain, dg_mix1 = _q_bwd("q_bwd", dq, qraw, row1(w["q_norm"][0]), cos, sin_s, w_q, h4, g("mix_norm", 1), dh5, n_main, n_q, hd)
    dw_q = _tn_rms("attn_dwq", h4, g("mix_norm", 1), dqraw, n_main).reshape(N_CHIPS, -1, dqraw.shape[1])
    dh3m, dg_f1l1, dwgu_f1l1, dwd_f1l1 = _ffn_bwd("l1_ffn1", dh4, h3, g("ffn1_norm", 1), gu3, wgu["ffn1", 1], wd["ffn1", 1], n_main)
    reduced = reduce_scatter("rs1", [("ffn2_w_gate_up", 1, dwgu_f2l1), ("ffn2_w_down", 1, dwd_f2l1), ("attn_w_o", 0, dw_o),
                                     ("attn_w_q", 0, dw_q), ("ffn1_w_gate_up", 1, dwgu_f1l1), ("ffn1_w_down", 1, dwd_f1l1)], (3, 4, 5))

    def with_meta(main, meta):
        blocks = [jnp.pad(meta[b * n_meta:(b + 1) * n_meta], ((META_BLOCK - n_meta, 0), (0, 0))) for b in range(n_ex)]
        return jnp.concatenate([main] + blocks, axis=0)

    dkvraw, dh3, dk_gain, dg_kv = _kv_bwd("kv_bwd", with_meta(dk_main, dk_meta), with_meta(dv_main, dv_meta), kvraw, row1(w["k_norm"]),
                                          cos, sin_s, w_kv, h3, row1(w["kv_norm"]), dh3m, n_all, n_main, n_kv, hd)
    dw_kv = _tn_rms("kv_dw", h3, row1(w["kv_norm"]), dkvraw, n_all).reshape(N_CHIPS, -1, dkvraw.shape[1])
    dh2, dg_f2l0, dwgu_f2l0, dwd_f2l0 = _ffn_bwd("l0_ffn2", dh3, h2, g("ffn2_norm", 0), gu2, wgu["ffn2", 0], wd["ffn2", 0], n_all)
    reduced.update(reduce_scatter("rs0a", [("w_kv", 0, dw_kv), ("ffn2_w_gate_up", 0, dwgu_f2l0), ("ffn2_w_down", 0, dwd_f2l0)], (6, 7, 8)))

    dy, dz, gx, dd = _ssm_out_bwd("ssm_out_bwd", dh2, y, u, cb16, wout4, n_all)
    hw = y.shape[1]
    oc = wout4.shape[2]
    dw_out = _tn_plain("ssm_dwout", y, dz, wout4.shape[0], hw, oc, n_all, a_fn=_gelu)
    dcb = _tn_plain("ssm_dcb", xs, dy, 4, xs.shape[1] // 4, hw // 4, n_all)
    gbu, da = _scan_bwd("ssm_scan_bwd", gx, xs, tabs_b, n_ex, seq)
    du, dh1, dg_mix0 = _ssm_in_bwd("ssm_in_bwd", gbu, dy, dskip, bb16, w_in, h1, g("mix_norm", 0), dh2, n_all)
    dbb = _tn_plain("ssm_dbb", u, gbu, 4, hw // 4, gbu.shape[1] // 4, n_all)
    dw_in = _tn_rms("ssm_dwin", h1, g("mix_norm", 0), du, n_all).reshape(N_CHIPS, -1, hw)
    (dh0, dh0_meta), dg_f1l0, dwgu_f1l0, dwd_f1l0 = _ffn_bwd("l0_ffn1", dh1, h0, g("ffn1_norm", 0), gu1, wgu["ffn1", 0],
                                                             wd["ffn1", 0], n_all, n_main)
    reduced.update(reduce_scatter("rs0b", [("ssm_w_out", 0, dw_out), ("ssm_w_in", 0, dw_in), ("ffn1_w_gate_up", 0, dwgu_f1l0),
                                           ("ffn1_w_down", 0, dwd_f1l0)], (9, 10, 11)))

    grad_x = dh0.reshape(n_ex, seq, d)
    da_sum = jnp.sum(da, axis=(0, 1)).reshape(4, 2, half)
    d_ssm = ssm_vjp((dbb, dcb, da_sum[:, 0].reshape(-1), da_sum[:, 1].reshape(-1)))
    for key, val in zip(("ssm_lambda_re", "ssm_lambda_im", "ssm_log_step", "ssm_b_re", "ssm_b_im", "ssm_c_re", "ssm_c_im"), d_ssm):
        small[key] = val[None]
    small["meta_tokens"] = sum(dh0_meta[META_BLOCK * (b + 1) - n_meta:META_BLOCK * (b + 1)] for b in range(n_ex))
    small["ffn1_norm"] = jnp.concatenate([dg_f1l0, dg_f1l1], axis=0)
    small["ffn2_norm"] = jnp.concatenate([dg_f2l0, dg_f2l1], axis=0)
    small["mix_norm"] = jnp.concatenate([dg_mix0, dg_mix1], axis=0)
    small["ssm_d"] = dd
    small["kv_norm"] = dg_kv.reshape(-1)
    small["k_norm"] = dk_gain.reshape(-1)
    small["q_norm"] = dq_gain
    small["attn_sinks"] = dsinks

    def view(name, a):
        if name in ("ssm_b_re", "ssm_b_im"):
            return a.reshape(-1, 128)
        return a.reshape(1, -1) if a.ndim == 1 else a.reshape(-1, a.shape[-1])

    tail = _reduce_small_adamw("small_tail", [view(k, small[k]) for k in _SMALL], loss_tile,
                               *[[view(k, t[k]) for k in _SMALL] for t in (w, m, v)])
    n_small = len(_SMALL)
    grads = {k: tail[i].reshape(w[k].shape) for i, k in enumerate(_SMALL)}
    deltas = {k: tail[n_small + i].reshape(w[k].shape) for i, k in enumerate(_SMALL)}
    new_m = {k: tail[2 * n_small + i].reshape(w[k].shape) for i, k in enumerate(_SMALL)}
    new_v = {k: tail[3 * n_small + i].reshape(w[k].shape) for i, k in enumerate(_SMALL)}
    loss = jnp.sum(tail[-1])

    for name in _BIG:
        layers = [reduced[name, layer] for layer in range(2) if (name, layer) in reduced]
        grads[name] = jnp.concatenate(layers).reshape(w[name].shape)
    for name in _BIG:
        shape = w[name].shape
        two_d = lambda a: a.reshape(-1, shape[-1])
        dl, nm, nv = _adamw("adamw_" + name, two_d(w[name]), two_d(grads[name]), two_d(m[name]), two_d(v[name]))
        deltas[name], new_m[name], new_v[name] = dl.reshape(shape), nm.reshape(shape), nv.reshape(shape)
    return (loss, grad_x, *[grads[k] for k in _ORDER], *[deltas[k] for k in _ORDER], *[new_m[k] for k in _ORDER],
            *[new_v[k] for k in _ORDER])


def kernel(x, meta_tokens, ffn1_norm, ffn1_w_gate_up, ffn1_w_down, mix_norm, ffn2_norm, ffn2_w_gate_up, ffn2_w_down, ssm_w_in, ssm_lambda_re, ssm_lambda_im, ssm_b_re, ssm_b_im, ssm_c_re, ssm_c_im, ssm_log_step, ssm_d, ssm_w_out, kv_norm, w_kv, k_norm, attn_w_q, q_norm, attn_sinks, attn_w_o, loss_target, m_meta_tokens, m_ffn1_norm, m_ffn1_w_gate_up, m_ffn1_w_down, m_mix_norm, m_ffn2_norm, m_ffn2_w_gate_up, m_ffn2_w_down, m_ssm_w_in, m_ssm_lambda_re, m_ssm_lambda_im, m_ssm_b_re, m_ssm_b_im, m_ssm_c_re, m_ssm_c_im, m_ssm_log_step, m_ssm_d, m_ssm_w_out, m_kv_norm, m_w_kv, m_k_norm, m_attn_w_q, m_q_norm, m_attn_sinks, m_attn_w_o, v_meta_tokens, v_ffn1_norm, v_ffn1_w_gate_up, v_ffn1_w_down, v_mix_norm, v_ffn2_norm, v_ffn2_w_gate_up, v_ffn2_w_down, v_ssm_w_in, v_ssm_lambda_re, v_ssm_lambda_im, v_ssm_b_re, v_ssm_b_im, v_ssm_c_re, v_ssm_c_im, v_ssm_log_step, v_ssm_d, v_ssm_w_out, v_kv_norm, v_w_kv, v_k_norm, v_attn_w_q, v_q_norm, v_attn_sinks, v_attn_w_o):
    args = locals()
    w = {k: args[k] for k in _ORDER}
    m = {k: args["m_" + k] for k in _ORDER}
    v = {k: args["v_" + k] for k in _ORDER}
    return _step(x, loss_target, w, m, v)
```

```python
import functools
import math

import jax
import jax.numpy as jnp
from jax import lax
from jax.experimental import pallas as pl
from jax.experimental.pallas import tpu as pltpu
from jax.experimental.pallas import tpu_sc as plsc

F32 = jnp.float32
BF16 = jnp.bfloat16
MESH = pl.DeviceIdType.MESH

EPS = 1e-6
NEG_INF = -1e30
ROPE_THETA = 10000.0
WINDOW = 128
META_BLOCK = 128
ROW_TILE = 256
SUBLANES = 8
V7X_VMEM_LIMIT = 56 * 2**20
N_CHIPS = 4
N_DEV = 8

ADAM_LR = 0.001
ADAM_B1 = 0.9
ADAM_B2 = 0.999
ADAM_EPS = 1e-08
ADAM_WD = 0.01
ADAM_STEP = 10

_HBM = pl.BlockSpec(memory_space=pltpu.HBM)
_VMEM = pl.BlockSpec(memory_space=pltpu.VMEM)


def _call(name, body, grid, in_specs, out_specs, out_shape, scratch=(), after=()):
    after = tuple(after)
    n_in = len(in_specs)

    def wrapped(*refs):
        return body(*refs[:n_in], *refs[n_in + len(after):])

    call = pl.pallas_call(
        wrapped, name=name, grid=grid, in_specs=list(in_specs) + [pl.BlockSpec(memory_space=pl.ANY)] * len(after),
        out_specs=out_specs, out_shape=out_shape, scratch_shapes=list(scratch),
        compiler_params=pltpu.CompilerParams(dimension_semantics=("arbitrary",) * len(grid),
                                             vmem_limit_bytes=V7X_VMEM_LIMIT))
    return lambda *operands: call(*operands, *after)


def _sds(shape, dtype):
    return jax.ShapeDtypeStruct(tuple(shape), dtype)


def _dot(a, b):
    return jnp.dot(a.astype(BF16), b.astype(BF16), preferred_element_type=F32)


def _dot_nt(a, b):
    return lax.dot_general(a.astype(BF16), b.astype(BF16), (((1,), (1,)), ((), ())), preferred_element_type=F32)


def _dot_tn(a, b):
    return lax.dot_general(a.astype(BF16), b.astype(BF16), (((0,), (0,)), ((), ())), preferred_element_type=F32)


def _rms(h, g):
    return h * lax.rsqrt(jnp.mean(h * h, axis=-1, keepdims=True) + EPS) * g


def _rms_bwd(h, g, dn):
    r = lax.rsqrt(jnp.mean(h * h, axis=-1, keepdims=True) + EPS)
    xh = h * r
    dxh = dn * g
    dg = jnp.sum(dn * xh, axis=0, keepdims=True)
    dh = r * (dxh - xh * jnp.mean(dxh * xh, axis=-1, keepdims=True))
    return dh, dg


def _sigmoid(x):
    return 0.5 * jnp.tanh(0.5 * x) + 0.5


def _gelu(y):
    k = math.sqrt(2.0 / math.pi)
    return 0.5 * y * (1.0 + jnp.tanh(k * (y + 0.044715 * y * y * y)))


def _gelu_grad(y):
    k = math.sqrt(2.0 / math.pi)
    t = jnp.tanh(k * (y + 0.044715 * y * y * y))
    return 0.5 * (1.0 + t) + 0.5 * y * (1.0 - t * t) * k * (1.0 + 3.0 * 0.044715 * y * y)


def _swap_halves(x):
    half = x.shape[-1] // 2
    return jnp.concatenate([x[:, half:], x[:, :half]], axis=1)


def _head_prep(x, gain, cos, sin_s, n_heads, hd):
    out = []
    for h in range(n_heads):
        xh = x[:, h * hd:(h + 1) * hd]
        y = xh * lax.rsqrt(jnp.mean(xh * xh, axis=-1, keepdims=True) + EPS) * gain
        out.append(y * cos + _swap_halves(y) * sin_s)
    return out


def _head_prep_bwd(x, gain, cos, sin_s, d_out, n_heads, hd):
    dxs = []
    dgain = jnp.zeros((1, hd), F32)
    for h in range(n_heads):
        xh = x[:, h * hd:(h + 1) * hd]
        do = d_out[:, h * hd:(h + 1) * hd]
        r = lax.rsqrt(jnp.mean(xh * xh, axis=-1, keepdims=True) + EPS)
        xhat = xh * r
        dy = do * cos + _swap_halves(do * sin_s)
        dgain = dgain + jnp.sum(dy * xhat, axis=0, keepdims=True)
        dxh = dy * gain
        dxs.append(r * (dxh - xhat * jnp.mean(dxh * xhat, axis=-1, keepdims=True)))
    return dxs, dgain


def _acc_out(ref, val, first):
    @pl.when(first)
    def _():
        ref[...] = jnp.zeros_like(ref)
    ref[...] += val


def _ffn_up(name, h, g, w4, n_rows):
    nj, d, fc = w4.shape
    tm = ROW_TILE

    def body(h_ref, g_ref, w_ref, o_ref, n_ref):
        n = _rms(h_ref[...], g_ref[...]).astype(BF16)
        n_ref[...] = n
        for j in range(nj):
            o_ref[:, j * fc:(j + 1) * fc] = _dot(n, w_ref[j]).astype(BF16)

    return _call(name, body, (n_rows // tm,),
                 [pl.BlockSpec((tm, d), lambda i: (i, 0)), pl.BlockSpec((1, d), lambda i: (0, 0)),
                  pl.BlockSpec((nj, d, fc), lambda i: (0, 0, 0))],
                 [pl.BlockSpec((tm, nj * fc), lambda i: (i, 0)), pl.BlockSpec((tm, d), lambda i: (i, 0))],
                 [_sds((n_rows, nj * fc), BF16), _sds((n_rows, d), BF16)])(h, g, w4)


def _ffn_down(name, gu, h, wd, n_rows):
    f, d = wd.shape
    tm = ROW_TILE

    def body(gu_ref, h_ref, w_ref, o_ref, s_ref):
        a = gu_ref[:, :f].astype(F32)
        b = gu_ref[:, f:].astype(F32)
        s = (a * _sigmoid(a) * b).astype(BF16)
        s_ref[...] = s
        o_ref[...] = h_ref[...] + 0.5 * _dot(s, w_ref[...])

    return _call(name, body, (n_rows // tm,),
                 [pl.BlockSpec((tm, 2 * f), lambda i: (i, 0)), pl.BlockSpec((tm, d), lambda i: (i, 0)),
                  pl.BlockSpec((f, d), lambda i: (0, 0))],
                 [pl.BlockSpec((tm, d), lambda i: (i, 0)), pl.BlockSpec((tm, f), lambda i: (i, 0))],
                 [_sds((n_rows, d), F32), _sds((n_rows, f), BF16)])(gu, h, wd)


def _ffn_dgu(name, dh, gu, wd, n_rows, after=()):
    f, d = wd.shape
    tm = ROW_TILE

    def body(dh_ref, gu_ref, w_ref, o_ref):
        ds = _dot_nt(0.5 * dh_ref[...], w_ref[...])
        a = gu_ref[:, :f].astype(F32)
        b = gu_ref[:, f:].astype(F32)
        sg = _sigmoid(a)
        o_ref[:, :f] = (ds * b * (sg * (1.0 + a * (1.0 - sg)))).astype(BF16)
        o_ref[:, f:] = (ds * (a * sg)).astype(BF16)

    return _call(name, body, (n_rows // tm,),
                 [pl.BlockSpec((tm, d), lambda i: (i, 0)), pl.BlockSpec((tm, 2 * f), lambda i: (i, 0)),
                  pl.BlockSpec((f, d), lambda i: (0, 0))],
                 pl.BlockSpec((tm, 2 * f), lambda i: (i, 0)), _sds((n_rows, 2 * f), BF16), after=after)(dh, gu, wd)


def _ffn_dh(name, dgu, h, g, dh, w4, n_rows, n_main=None, after=()):
    nj, d, fc = w4.shape
    tm = ROW_TILE
    n_first = (n_rows if n_main is None else n_main) // tm

    def body(dgu_ref, h_ref, g_ref, dh_ref, w_ref, o_ref, *rest):
        dg_ref = rest[-1]
        i = pl.program_id(0)
        dn = _dot_nt(dgu_ref[:, 0:fc], w_ref[0])
        for j in range(1, nj):
            dn = dn + _dot_nt(dgu_ref[:, j * fc:(j + 1) * fc], w_ref[j])
        dhn, dg = _rms_bwd(h_ref[...], g_ref[...], dn)
        val = dh_ref[...] + dhn
        if n_main is None:
            o_ref[...] = val
        else:
            @pl.when(i < n_first)
            def _():
                o_ref[...] = val

            @pl.when(i >= n_first)
            def _():
                rest[0][...] = val
        _acc_out(dg_ref, dg, i == 0)

    out_specs = [pl.BlockSpec((tm, d), lambda i: (jnp.minimum(i, n_first - 1), 0))]
    out_shape = [_sds((n_first * tm, d), F32)]
    if n_main is not None:
        out_specs.append(pl.BlockSpec((tm, d), lambda i: (jnp.maximum(i - n_first, 0), 0)))
        out_shape.append(_sds((n_rows - n_main, d), F32))
    return _call(name, body, (n_rows // tm,),
                 [pl.BlockSpec((tm, nj * fc), lambda i: (i, 0)), pl.BlockSpec((tm, d), lambda i: (i, 0)),
                  pl.BlockSpec((1, d), lambda i: (0, 0)), pl.BlockSpec((tm, d), lambda i: (i, 0)),
                  pl.BlockSpec((nj, d, fc), lambda i: (0, 0, 0))],
                 out_specs + [pl.BlockSpec((1, d), lambda i: (0, 0))],
                 out_shape + [_sds((1, d), F32)], after=after)(dgu, h, g, dh, w4)


def _contract_tile(n_rows, cap=1024):
    best = ROW_TILE
    for t in range(ROW_TILE, cap + 1, ROW_TILE):
        if n_rows % t == 0:
            best = t
    return best


def _tn(name, operands, in_specs, prologue, nj, ma, nb, n_rows, tk):
    def body(*refs):
        o_ref = refs[-1]
        a, b = prologue(pl.program_id(0), *refs[:-1])
        _acc_out(o_ref, _dot_tn(a, b)[None], pl.program_id(1) == 0)

    return _call(name, body, (nj, n_rows // tk), in_specs, pl.BlockSpec((1, ma, nb), lambda j, k: (j, 0, 0)),
                 _sds((nj, ma, nb), F32))(*operands)


def _ffn_dwgu(name, n, dgu, nj, n_rows):
    d = n.shape[1]
    fc = dgu.shape[1] // nj
    tk = _contract_tile(n_rows)
    return _tn(name, (dgu, n),
               [pl.BlockSpec((tk, fc), lambda j, k: (k, j)), pl.BlockSpec((tk, d), lambda j, k: (k, 0))],
               lambda j, a_ref, b_ref: (a_ref[...], b_ref[...]), nj, fc, d, n_rows, tk)


def _ffn_dwd(name, s, dh, n_rows):
    f = s.shape[1]
    d = dh.shape[1]
    tk = _contract_tile(n_rows)
    return _tn(name, (s, dh),
               [pl.BlockSpec((tk, f), lambda j, k: (k, 0)), pl.BlockSpec((tk, d), lambda j, k: (k, 0))],
               lambda j, s_ref, dh_ref: (s_ref[...], 0.5 * dh_ref[...]), 1, f, d, n_rows, tk)


def _ffn_fwd(tag, h, g, w4, wd, n_rows):
    gu, n = _ffn_up(tag + "_up", h, g, w4, n_rows)
    h_out, s = _ffn_down(tag + "_down", gu, h, wd, n_rows)
    return h_out, (gu, n, s)


def _ffn_bwd(tag, dh_out, h, g, saved, w4, wd, n_rows, n_main=None, after=(), before_dh=None):
    gu, n, s = saved
    nj = w4.shape[0]
    f, d = wd.shape
    dgu = _ffn_dgu(tag + "_dgu", dh_out, gu, wd, n_rows, after=after)
    dwd = _ffn_dwd(tag + "_dwd", s, dh_out, n_rows).reshape(N_CHIPS, f // N_CHIPS, d)
    dwgu = _ffn_dwgu(tag + "_dwgu", n, dgu, nj, n_rows)
    dh_after = () if before_dh is None else before_dh(dwgu, dwd)
    *dh_parts, dg = _ffn_dh(tag + "_dh", dgu, h, g, dh_out, w4, n_rows, n_main, after=dh_after)
    dh_in = dh_parts[0] if n_main is None else tuple(dh_parts)
    return dh_in, dg, dwgu, dwd


def _ssm_in(name, h, g, w_in, bb, n_rows):
    d, hw = w_in.shape
    nj, uc, xc = bb.shape
    tm = ROW_TILE

    def body(h_ref, g_ref, w_ref, bb_ref, u_ref, bu_ref):
        u = _dot(_rms(h_ref[...], g_ref[...]), w_ref[...])
        u_ref[...] = u
        for j in range(nj):
            bu_ref[:, j * xc:(j + 1) * xc] = _dot(u[:, j * uc:(j + 1) * uc], bb_ref[j])

    return _call(name, body, (n_rows // tm,),
                 [pl.BlockSpec((tm, d), lambda i: (i, 0)), pl.BlockSpec((1, d), lambda i: (0, 0)),
                  pl.BlockSpec((d, hw), lambda i: (0, 0)), pl.BlockSpec((nj, uc, xc), lambda i: (0, 0, 0))],
                 [pl.BlockSpec((tm, hw), lambda i: (i, 0)), pl.BlockSpec((tm, nj * xc), lambda i: (i, 0))],
                 [_sds((n_rows, hw), F32), _sds((n_rows, nj * xc), F32)])(h, g, w_in, bb)


def _cmul_add(xr, xi, ar, ai, sr, si):
    return xr + ar * sr - ai * si, xi + ar * si + ai * sr


def _scan_row_block(n_main_blocks, seq_blocks):
    return lambda b, i: jnp.where(i == 0, n_main_blocks + b, b * seq_blocks + i - 1)


def _scan_fwd(name, bu, tabs, n_ex, seq):
    n_rows, width = bu.shape
    nj = 4
    cw = width // nj
    half = cw // 2
    tq = META_BLOCK
    seq_blocks = seq // tq
    rb = _scan_row_block(n_ex * seq_blocks, seq_blocks)

    def body(bu_ref, tab_ref, x_ref, carry_ref):
        @pl.when(pl.program_id(2) == 0)
        def _():
            carry_ref[...] = jnp.zeros_like(carry_ref)
        def blk(k, c):
            t = [tab_ref[n * SUBLANES:(n + 1) * SUBLANES, :] for n in range(8)]
            r0 = pl.multiple_of(k * SUBLANES, SUBLANES)
            xr = bu_ref[pl.ds(r0, SUBLANES), 0:half]
            xi = bu_ref[pl.ds(r0, SUBLANES), half:cw]
            for s, d in enumerate((1, 2, 4)):
                xr, xi = _cmul_add(xr, xi, t[2 * s], t[2 * s + 1], pltpu.roll(xr, d, 0), pltpu.roll(xi, d, 0))
            xr, xi = _cmul_add(xr, xi, t[6], t[7], c[0], c[1])
            x_ref[pl.ds(r0, SUBLANES), 0:half] = xr
            x_ref[pl.ds(r0, SUBLANES), half:cw] = xi
            last = SUBLANES - 1
            return (jnp.broadcast_to(xr[last:last + 1, :], xr.shape), jnp.broadcast_to(xi[last:last + 1, :], xi.shape))

        c = lax.fori_loop(0, tq // SUBLANES, blk, (carry_ref[0], carry_ref[1]))
        carry_ref[0] = c[0]
        carry_ref[1] = c[1]

    return _call(name, body, (n_ex, nj, seq_blocks + 1),
                 [pl.BlockSpec((tq, cw), lambda b, j, i: (rb(b, i), j)), pl.BlockSpec((8 * SUBLANES, half), lambda b, j, i: (0, j))],
                 pl.BlockSpec((tq, cw), lambda b, j, i: (rb(b, i), j)), _sds((n_rows, width), F32),
                 scratch=[pltpu.VMEM((2, SUBLANES, half), F32)])(bu, tabs)


def _scan_bwd(name, gx, x, tabs, n_ex, seq, after=()):
    n_rows, width = gx.shape
    nj = 4
    cw = width // nj
    half = cw // 2
    tq = META_BLOCK
    seq_blocks = seq // tq
    n_steps = seq_blocks + 1
    rb = _scan_row_block(n_ex * seq_blocks, seq_blocks)
    rbr = lambda b, i: rb(b, n_steps - 1 - i)

    def body(gx_ref, x_ref, tab_ref, g_ref, da_ref, carry_ref):
        @pl.when(pl.program_id(2) == 0)
        def _():
            carry_ref[...] = jnp.zeros_like(carry_ref)
            da_ref[...] = jnp.zeros_like(da_ref)
        row = lax.broadcasted_iota(jnp.int32, (SUBLANES, half), 0)
        n_blk = tq // SUBLANES

        def blk(kk, st):
            t = [tab_ref[n * SUBLANES:(n + 1) * SUBLANES, :] for n in range(8)]
            cr, ci, dar, dai = st
            r0 = pl.multiple_of((n_blk - 1 - kk) * SUBLANES, SUBLANES)
            gr = gx_ref[pl.ds(r0, SUBLANES), 0:half]
            gi = gx_ref[pl.ds(r0, SUBLANES), half:cw]
            for s, d in enumerate((1, 2, 4)):
                gr, gi = _cmul_add(gr, gi, t[2 * s], t[2 * s + 1],
                                   pltpu.roll(gr, SUBLANES - d, 0), pltpu.roll(gi, SUBLANES - d, 0))
            gr, gi = _cmul_add(gr, gi, t[6], t[7], cr, ci)
            g_ref[pl.ds(r0, SUBLANES), 0:half] = gr.astype(BF16)
            g_ref[pl.ds(r0, SUBLANES), half:cw] = gi.astype(BF16)
            hr = jnp.where(row == SUBLANES - 1, cr, pltpu.roll(gr, SUBLANES - 1, 0))
            hi = jnp.where(row == SUBLANES - 1, ci, pltpu.roll(gi, SUBLANES - 1, 0))
            xr = x_ref[pl.ds(r0, SUBLANES), 0:half]
            xi = x_ref[pl.ds(r0, SUBLANES), half:cw]
            dar = dar + xr * hr + xi * hi
            dai = dai + xr * hi - xi * hr
            return (jnp.broadcast_to(gr[0:1, :], gr.shape), jnp.broadcast_to(gi[0:1, :], gi.shape), dar, dai)

        st = lax.fori_loop(0, n_blk, blk, (carry_ref[0], carry_ref[1], da_ref[0, :, 0:half], da_ref[0, :, half:cw]))
        carry_ref[0] = st[0]
        carry_ref[1] = st[1]
        da_ref[0, :, 0:half] = st[2]
        da_ref[0, :, half:cw] = st[3]

    return _call(name, body, (n_ex, nj, n_steps),
                 [pl.BlockSpec((tq, cw), lambda b, j, i: (rbr(b, i), j)), pl.BlockSpec((tq, cw), lambda b, j, i: (rbr(b, i), j)),
                  pl.BlockSpec((8 * SUBLANES, half), lambda b, j, i: (0, j))],
                 [pl.BlockSpec((tq, cw), lambda b, j, i: (rbr(b, i), j)), pl.BlockSpec((1, SUBLANES, cw), lambda b, j, i: (b, 0, j))],
                 [_sds((n_rows, width), BF16), _sds((n_ex, SUBLANES, width), F32)],
                 scratch=[pltpu.VMEM((2, SUBLANES, half), F32)], after=after)(gx, x, tabs)


def _ssm_z(gy, wout_ref, nj):
    return jnp.concatenate([_dot(gy, wout_ref[j]) for j in range(nj)], axis=1)


def _ssm_out(name, x, u, dskip, cb, wout4, h, n_rows):
    nj, xc, uc = cb.shape
    no, hw, oc = wout4.shape
    d = h.shape[1]
    tm = ROW_TILE

    def body(x_ref, u_ref, ds_ref, cb_ref, w_ref, h_ref, o_ref, y_ref):
        y = jnp.concatenate([_dot(x_ref[:, j * xc:(j + 1) * xc], cb_ref[j]) for j in range(nj)], axis=1)
        y = y + ds_ref[...] * u_ref[...]
        y_ref[...] = y
        z = _ssm_z(_gelu(y), w_ref, no)
        o_ref[...] = h_ref[...] + z[:, :d] * _sigmoid(z[:, d:])

    return _call(name, body, (n_rows // tm,),
                 [pl.BlockSpec((tm, nj * xc), lambda i: (i, 0)), pl.BlockSpec((tm, hw), lambda i: (i, 0)),
                  pl.BlockSpec((1, hw), lambda i: (0, 0)), pl.BlockSpec((nj, xc, uc), lambda i: (0, 0, 0)),
                  pl.BlockSpec((no, hw, oc), lambda i: (0, 0, 0)), pl.BlockSpec((tm, d), lambda i: (i, 0))],
                 [pl.BlockSpec((tm, d), lambda i: (i, 0)), pl.BlockSpec((tm, hw), lambda i: (i, 0))],
                 [_sds((n_rows, d), F32), _sds((n_rows, hw), F32)])(x, u, dskip, cb, wout4, h)


def _ssm_out_bwd(name, dh, y, u, cb, wout4, n_rows, after=()):
    nj, xc, uc = cb.shape
    no, hw, oc = wout4.shape
    d = dh.shape[1]
    tm = ROW_TILE

    def body(dh_ref, y_ref, u_ref, cb_ref, w_ref, dy_ref, dz_ref, gx_ref, dd_ref):
        y = y_ref[...]
        z = _ssm_z(_gelu(y), w_ref, no)
        za = z[:, :d]
        sg = _sigmoid(z[:, d:])
        dmix = dh_ref[...]
        dz = jnp.concatenate([dmix * sg, dmix * za * sg * (1.0 - sg)], axis=1).astype(BF16)
        dz_ref[...] = dz
        dgy = _dot_nt(dz[:, 0:oc], w_ref[0])
        for j in range(1, no):
            dgy = dgy + _dot_nt(dz[:, j * oc:(j + 1) * oc], w_ref[j])
        dy = dgy * _gelu_grad(y)
        dy_ref[...] = dy
        _acc_out(dd_ref, jnp.sum(dy * u_ref[...], axis=0, keepdims=True), pl.program_id(0) == 0)
        for j in range(nj):
            gx_ref[:, j * xc:(j + 1) * xc] = _dot_nt(dy[:, j * uc:(j + 1) * uc], cb_ref[j])

    return _call(name, body, (n_rows // tm,),
                 [pl.BlockSpec((tm, d), lambda i: (i, 0)), pl.BlockSpec((tm, hw), lambda i: (i, 0)),
                  pl.BlockSpec((tm, hw), lambda i: (i, 0)), pl.BlockSpec((nj, xc, uc), lambda i: (0, 0, 0)),
                  pl.BlockSpec((no, hw, oc), lambda i: (0, 0, 0))],
                 [pl.BlockSpec((tm, hw), lambda i: (i, 0)), pl.BlockSpec((tm, no * oc), lambda i: (i, 0)),
                  pl.BlockSpec((tm, nj * xc), lambda i: (i, 0)), pl.BlockSpec((1, hw), lambda i: (0, 0))],
                 [_sds((n_rows, hw), F32), _sds((n_rows, no * oc), BF16), _sds((n_rows, nj * xc), F32),
                  _sds((1, hw), F32)], after=after)(dh, y, u, cb, wout4)


def _ssm_in_bwd(name, gbu, dy, dskip, bb, w_in, h, g, dh, n_rows):
    nj, uc, xc = bb.shape
    d, hw = w_in.shape
    tm = ROW_TILE

    def body(gb_ref, dy_ref, ds_ref, bb_ref, w_ref, h_ref, g_ref, dh_ref, du_ref, o_ref, dg_ref):
        du = jnp.concatenate([_dot_nt(gb_ref[:, j * xc:(j + 1) * xc], bb_ref[j]) for j in range(nj)], axis=1)
        du = du + dy_ref[...] * ds_ref[...]
        du_ref[...] = du.astype(BF16)
        dhn, dg = _rms_bwd(h_ref[...], g_ref[...], _dot_nt(du, w_ref[...]))
        o_ref[...] = dh_ref[...] + dhn
        _acc_out(dg_ref, dg, pl.program_id(0) == 0)

    return _call(name, body, (n_rows // tm,),
                 [pl.BlockSpec((tm, nj * xc), lambda i: (i, 0)), pl.BlockSpec((tm, hw), lambda i: (i, 0)),
                  pl.BlockSpec((1, hw), lambda i: (0, 0)), pl.BlockSpec((nj, uc, xc), lambda i: (0, 0, 0)),
                  pl.BlockSpec((d, hw), lambda i: (0, 0)), pl.BlockSpec((tm, d), lambda i: (i, 0)),
                  pl.BlockSpec((1, d), lambda i: (0, 0)), pl.BlockSpec((tm, d), lambda i: (i, 0))],
                 [pl.BlockSpec((tm, hw), lambda i: (i, 0)), pl.BlockSpec((tm, d), lambda i: (i, 0)),
                  pl.BlockSpec((1, d), lambda i: (0, 0))],
                 [_sds((n_rows, hw), BF16), _sds((n_rows, d), F32), _sds((1, d), F32)])(gbu, dy, dskip, bb, w_in, h, g, dh)


def _discretize(lam_re, lam_im, log_step, b_re, b_im):
    step = jnp.exp(log_step)[:, None]
    mag = jnp.exp(lam_re * step)
    ar = mag * jnp.cos(lam_im * step)
    ai = mag * jnp.sin(lam_im * step)
    den = lam_re * lam_re + lam_im * lam_im
    nr, ni = ar - 1.0, ai
    cr = (nr * lam_re + ni * lam_im) / den
    ci = (ni * lam_re - nr * lam_im) / den
    bbar_r = cr[..., None] * b_re - ci[..., None] * b_im
    bbar_i = cr[..., None] * b_im + ci[..., None] * b_re
    return ar, ai, bbar_r, bbar_i


def _ssm_mats(lam_re, lam_im, log_step, b_re, b_im, c_re, c_im):
    n_g, n_p, n_c = b_re.shape
    gpc = n_g // 4
    ar, ai, bbar_r, bbar_i = _discretize(lam_re, lam_im, log_step, b_re, b_im)
    eye = jnp.eye(gpc, dtype=F32)

    def in_map(bbar):
        return jnp.einsum('jgpc,gh->jgchp', bbar.reshape(4, gpc, n_p, n_c), eye).reshape(4, gpc * n_c, gpc * n_p)

    def out_map(c):
        return jnp.einsum('jgcp,gh->jgphc', c.reshape(4, gpc, n_c, n_p), eye).reshape(4, gpc * n_p, gpc * n_c)

    bb = jnp.concatenate([in_map(bbar_r), in_map(bbar_i)], axis=2)
    cb = jnp.concatenate([out_map(c_re), -out_map(c_im)], axis=1)
    return bb, cb, ar.reshape(-1), ai.reshape(-1)


def _chunked(v, half):
    return v.reshape(v.shape[:-1] + (4, half))


def _scan_tables(ar, ai, reverse):
    if reverse:
        ai = -ai
    pr, pi = [ar], [ai]
    for _ in range(SUBLANES - 1):
        pr, pi = pr + [pr[-1] * ar - pi[-1] * ai], pi + [pr[-1] * ai + pi[-1] * ar]
    row = jnp.arange(SUBLANES)[:, None]
    tabs = []
    for d in (1, 2, 4):
        keep = (row <= SUBLANES - 1 - d) if reverse else (row >= d)
        tabs += [jnp.where(keep, pr[d - 1][None, :], 0.0), jnp.where(keep, pi[d - 1][None, :], 0.0)]
    order = list(range(SUBLANES))[::-1] if reverse else list(range(SUBLANES))
    tabs += [jnp.stack([pr[k] for k in order]), jnp.stack([pi[k] for k in order])]
    return jnp.concatenate(tabs, axis=0)


def _kv_proj(name, h, g, w_kv, k_gain, cos, sin_s, n_rows, n_kv, hd):
    d, kvw = w_kv.shape
    kw = n_kv * hd
    tm = ROW_TILE

    def body(h_ref, g_ref, w_ref, kg_ref, c_ref, s_ref, raw_ref, k_ref, v_ref):
        raw = _dot(_rms(h_ref[...], g_ref[...]), w_ref[...])
        raw_ref[...] = raw
        ks = _head_prep(raw[:, :kw], kg_ref[...], c_ref[...], s_ref[...], n_kv, hd)
        k_ref[...] = jnp.concatenate(ks, axis=1).astype(BF16)
        v_ref[...] = raw[:, kw:].astype(BF16)

    return _call(name, body, (n_rows // tm,),
                 [pl.BlockSpec((tm, d), lambda i: (i, 0)), pl.BlockSpec((1, d), lambda i: (0, 0)),
                  pl.BlockSpec((d, kvw), lambda i: (0, 0)), pl.BlockSpec((1, hd), lambda i: (0, 0)),
                  pl.BlockSpec((tm, hd), lambda i: (i, 0)), pl.BlockSpec((tm, hd), lambda i: (i, 0))],
                 [pl.BlockSpec((tm, kvw), lambda i: (i, 0)), pl.BlockSpec((tm, kw), lambda i: (i, 0)),
                  pl.BlockSpec((tm, kw), lambda i: (i, 0))],
                 [_sds((n_rows, kvw), F32), _sds((n_rows, kw), BF16), _sds((n_rows, kw), BF16)])(
                     h, g, w_kv, k_gain, cos, sin_s)


def _q_proj(name, h, g, w_q, q_gain, cos, sin_s, n_rows, n_q, hd):
    d, qw = w_q.shape
    tm = ROW_TILE

    def body(h_ref, g_ref, w_ref, qg_ref, c_ref, s_ref, raw_ref, q_ref):
        raw = _dot(_rms(h_ref[...], g_ref[...]), w_ref[...])
        raw_ref[...] = raw
        qs = _head_prep(raw, qg_ref[...], c_ref[...], s_ref[...], n_q, hd)
        q_ref[...] = jnp.concatenate(qs, axis=1).astype(BF16)

    return _call(name, body, (n_rows // tm,),
                 [pl.BlockSpec((tm, d), lambda i: (i, 0)), pl.BlockSpec((1, d), lambda i: (0, 0)),
                  pl.BlockSpec((d, qw), lambda i: (0, 0)), pl.BlockSpec((1, hd), lambda i: (0, 0)),
                  pl.BlockSpec((tm, hd), lambda i: (i, 0)), pl.BlockSpec((tm, hd), lambda i: (i, 0))],
                 [pl.BlockSpec((tm, qw), lambda i: (i, 0)), pl.BlockSpec((tm, qw), lambda i: (i, 0))],
                 [_sds((n_rows, qw), F32), _sds((n_rows, qw), BF16)])(h, g, w_q, q_gain, cos, sin_s)


def _attn_specs(seq, n_ex, n_meta, kw):
    nb = seq // WINDOW
    meta_blk = lambda b: (n_ex * seq + META_BLOCK * b + META_BLOCK - n_meta) // n_meta
    return [pl.BlockSpec((WINDOW, kw), lambda b, n: (b * nb + jnp.maximum(n - 1, 0), 0)),
            pl.BlockSpec((WINDOW, kw), lambda b, n: (b * nb + n, 0)),
            pl.BlockSpec((n_meta, kw), lambda b, n: (meta_blk(b), 0))]


def _attn_mask(n, qpk, n_keys):
    rows = qpk * WINDOW
    qi = lax.broadcasted_iota(jnp.int32, (rows, n_keys), 0) & (WINDOW - 1)
    kj = lax.broadcasted_iota(jnp.int32, (rows, n_keys), 1)
    rel = qi + WINDOW - kj
    band = (rel >= 0) & (rel < WINDOW) & ((n > 0) | (kj >= WINDOW))
    return band | (kj >= 2 * WINDOW)


def _stack_heads(ref, h, qpk, hd, dtype=None):
    parts = [ref[:, (h * qpk + gq) * hd:(h * qpk + gq + 1) * hd] for gq in range(qpk)]
    out = jnp.concatenate(parts, axis=0)
    return out if dtype is None else out.astype(dtype)


def _col(tile, c):
    lane = lax.broadcasted_iota(jnp.int32, tile.shape, 1)
    return jnp.sum(jnp.where(lane == c, tile, 0.0), axis=-1, keepdims=True)


def _put_col(col, c, n):
    lane = lax.broadcasted_iota(jnp.int32, (col.shape[0], n), 1)
    return jnp.where(lane == c, col, 0.0)


def _stack_cols(tile, h, qpk):
    return jnp.concatenate([_col(tile, h * qpk + gq) for gq in range(qpk)], axis=0)


def _sink_col(sinks, h, qpk):
    return jnp.concatenate([jnp.broadcast_to(_col(sinks, h * qpk + gq), (WINDOW, 1)) for gq in range(qpk)], axis=0)


def _attn_fwd(name, q, k, v, sinks, n_ex, seq, n_meta, n_kv, qpk, hd):
    nb = seq // WINDOW
    n_q = n_kv * qpk
    kw = n_kv * hd
    qw = n_q * hd
    n_keys = 2 * WINDOW + n_meta
    scale = hd ** -0.5

    def body(q_ref, kp_ref, kc_ref, km_ref, vp_ref, vc_ref, vm_ref, sk_ref, o_ref, lse_ref):
        valid = _attn_mask(pl.program_id(1), qpk, n_keys)
        sinks_v = sk_ref[...]
        o_parts = []
        lse_all = jnp.zeros((WINDOW, n_q), F32)
        for h in range(n_kv):
            hs = slice(h * hd, (h + 1) * hd)
            kb = jnp.concatenate([kp_ref[:, hs], kc_ref[:, hs], km_ref[:, hs]], axis=0)
            vb = jnp.concatenate([vp_ref[:, hs], vc_ref[:, hs], vm_ref[:, hs]], axis=0)
            s = jnp.where(valid, _dot_nt(_stack_heads(q_ref, h, qpk, hd), kb) * scale, NEG_INF)
            skc = _sink_col(sinks_v, h, qpk)
            m = jnp.maximum(jnp.max(s, axis=-1, keepdims=True), skc)
            p = jnp.exp(s - m)
            den = jnp.sum(p, axis=-1, keepdims=True) + jnp.exp(skc - m)
            o = _dot(p, vb) / den
            lse = m + jnp.log(den)
            for gq in range(qpk):
                o_parts.append(o[gq * WINDOW:(gq + 1) * WINDOW])
                lse_all = lse_all + _put_col(lse[gq * WINDOW:(gq + 1) * WINDOW], h * qpk + gq, n_q)
        o_ref[...] = jnp.concatenate(o_parts, axis=1)
        lse_ref[...] = lse_all

    qspec = pl.BlockSpec((WINDOW, qw), lambda b, n: (b * nb + n, 0))
    return _call(name, body, (n_ex, nb),
                 [qspec] + _attn_specs(seq, n_ex, n_meta, kw) + _attn_specs(seq, n_ex, n_meta, kw)
                 + [pl.BlockSpec((1, n_q), lambda b, n: (0, 0))],
                 [qspec, pl.BlockSpec((WINDOW, n_q), lambda b, n: (b * nb + n, 0))],
                 [_sds((n_ex * seq, qw), F32), _sds((n_ex * seq, n_q), F32)])(q, k, k, k, v, v, v, sinks)


def _attn_bwd(name, q, k, v, sinks, o, lse, do, n_ex, seq, n_meta, n_kv, qpk, hd):
    nb = seq // WINDOW
    n_q = n_kv * qpk
    kw = n_kv * hd
    qw = n_q * hd
    n_keys = 2 * WINDOW + n_meta
    scale = hd ** -0.5

    def body(q_ref, kp_ref, kc_ref, km_ref, vp_ref, vc_ref, vm_ref, sk_ref, o_ref, lse_ref, do_ref,
             dq_ref, dk_ref, dv_ref, dkm_ref, dvm_ref, dsk_ref):
        n = pl.program_id(1)

        @pl.when(n == 0)
        def _():
            dk_ref[...] = jnp.zeros_like(dk_ref)
            dv_ref[...] = jnp.zeros_like(dv_ref)
            dkm_ref[...] = jnp.zeros_like(dkm_ref)
            dvm_ref[...] = jnp.zeros_like(dvm_ref)

        @pl.when((n == 0) & (pl.program_id(0) == 0))
        def _():
            dsk_ref[...] = jnp.zeros_like(dsk_ref)

        valid = _attn_mask(n, qpk, n_keys)
        sinks_v = sk_ref[...]
        lse_v = lse_ref[...]
        dq_parts, dk_parts, dv_parts = [], [], []
        dsk = jnp.zeros((1, n_q), F32)
        for h in range(n_kv):
            hs = slice(h * hd, (h + 1) * hd)
            kb = jnp.concatenate([kp_ref[:, hs], kc_ref[:, hs], km_ref[:, hs]], axis=0)
            vb = jnp.concatenate([vp_ref[:, hs], vc_ref[:, hs], vm_ref[:, hs]], axis=0)
            qs = _stack_heads(q_ref, h, qpk, hd)
            dos = _stack_heads(do_ref, h, qpk, hd)
            delta = jnp.sum(dos * _stack_heads(o_ref, h, qpk, hd), axis=-1, keepdims=True)
            lse_c = _stack_cols(lse_v, h, qpk)
            s = jnp.where(valid, _dot_nt(qs, kb) * scale, NEG_INF)
            p = jnp.exp(s - lse_c)
            ds = p * (_dot_nt(dos, vb) - delta)
            dqs = _dot(ds, kb) * scale
            dk_parts.append(_dot_tn(ds, qs) * scale)
            dv_parts.append(_dot_tn(p, dos))
            dsink = -jnp.exp(_sink_col(sinks_v, h, qpk) - lse_c) * delta
            for gq in range(qpk):
                dq_parts.append(dqs[gq * WINDOW:(gq + 1) * WINDOW])
                dsk = dsk + _put_col(jnp.sum(dsink[gq * WINDOW:(gq + 1) * WINDOW], axis=0, keepdims=True), h * qpk + gq, n_q)
        dq_ref[...] = jnp.concatenate(dq_parts, axis=1)
        dsk_ref[...] += dsk
        dkb = jnp.concatenate(dk_parts, axis=1)
        dvb = jnp.concatenate(dv_parts, axis=1)
        prev = pl.ds(pl.multiple_of(jnp.maximum(n - 1, 0) * WINDOW, WINDOW), WINDOW)
        cur = pl.ds(pl.multiple_of(n * WINDOW, WINDOW), WINDOW)
        dk_ref[prev, :] += dkb[0:WINDOW]
        dv_ref[prev, :] += dvb[0:WINDOW]
        dk_ref[cur, :] += dkb[WINDOW:2 * WINDOW]
        dv_ref[cur, :] += dvb[WINDOW:2 * WINDOW]
        dkm_ref[...] += dkb[2 * WINDOW:]
        dvm_ref[...] += dvb[2 * WINDOW:]

    qspec = pl.BlockSpec((WINDOW, qw), lambda b, n: (b * nb + n, 0))
    exspec = pl.BlockSpec((seq, kw), lambda b, n: (b, 0))
    mspec = pl.BlockSpec((n_meta, kw), lambda b, n: (b, 0))
    return _call(name, body, (n_ex, nb),
                 [qspec] + _attn_specs(seq, n_ex, n_meta, kw) + _attn_specs(seq, n_ex, n_meta, kw)
                 + [pl.BlockSpec((1, n_q), lambda b, n: (0, 0)), qspec,
                    pl.BlockSpec((WINDOW, n_q), lambda b, n: (b * nb + n, 0)), qspec],
                 [qspec, exspec, exspec, mspec, mspec, pl.BlockSpec((1, n_q), lambda b, n: (0, 0))],
                 [_sds((n_ex * seq, qw), F32), _sds((n_ex * seq, kw), F32), _sds((n_ex * seq, kw), F32),
                  _sds((n_ex * n_meta, kw), F32), _sds((n_ex * n_meta, kw), F32), _sds((1, n_q), F32)])(
                      q, k, k, k, v, v, v, sinks, o, lse, do)


def _attn_out(name, o, h, w_o, n_rows):
    qw, d = w_o.shape
    tm = ROW_TILE

    def body(o_ref, h_ref, w_ref, out_ref):
        out_ref[...] = h_ref[...] + _dot(o_ref[...], w_ref[...])

    return _call(name, body, (n_rows // tm,),
                 [pl.BlockSpec((tm, qw), lambda i: (i, 0)), pl.BlockSpec((tm, d), lambda i: (i, 0)),
                  pl.BlockSpec((qw, d), lambda i: (0, 0))],
                 pl.BlockSpec((tm, d), lambda i: (i, 0)), _sds((n_rows, d), F32))(o, h, w_o)


def _attn_out_bwd(name, dh, w_o, n_rows):
    qw, d = w_o.shape
    tm = ROW_TILE

    def body(dh_ref, w_ref, do_ref):
        do_ref[...] = _dot_nt(dh_ref[...], w_ref[...])

    return _call(name, body, (n_rows // tm,),
                 [pl.BlockSpec((tm, d), lambda i: (i, 0)), pl.BlockSpec((qw, d), lambda i: (0, 0))],
                 pl.BlockSpec((tm, qw), lambda i: (i, 0)), _sds((n_rows, qw), F32))(dh, w_o)


def _q_bwd(name, dq, qraw, q_gain, cos, sin_s, w_q, h, g, dh, n_rows, n_q, hd):
    d, qw = w_q.shape
    tm = ROW_TILE

    def body(dq_ref, raw_ref, qg_ref, c_ref, s_ref, w_ref, h_ref, g_ref, dh_ref, draw_ref, o_ref, dqg_ref, dg_ref):
        dxs, dgain = _head_prep_bwd(raw_ref[...], qg_ref[...], c_ref[...], s_ref[...], dq_ref[...], n_q, hd)
        draw = jnp.concatenate(dxs, axis=1).astype(BF16)
        draw_ref[...] = draw
        dhn, dg = _rms_bwd(h_ref[...], g_ref[...], _dot_nt(draw, w_ref[...]))
        o_ref[...] = dh_ref[...] + dhn
        first = pl.program_id(0) == 0
        _acc_out(dqg_ref, dgain, first)
        _acc_out(dg_ref, dg, first)

    row = lambda w: pl.BlockSpec((tm, w), lambda i: (i, 0))
    one = lambda w: pl.BlockSpec((1, w), lambda i: (0, 0))
    return _call(name, body, (n_rows // tm,),
                 [row(qw), row(qw), one(hd), row(hd), row(hd), pl.BlockSpec((d, qw), lambda i: (0, 0)), row(d), one(d), row(d)],
                 [row(qw), row(d), one(hd), one(d)],
                 [_sds((n_rows, qw), BF16), _sds((n_rows, d), F32), _sds((1, hd), F32), _sds((1, d), F32)])(
                     dq, qraw, q_gain, cos, sin_s, w_q, h, g, dh)


def _kv_bwd(name, dk, dv, kvraw, k_gain, cos, sin_s, w_kv, h, g, dh_main, n_rows, n_main, n_kv, hd, after=()):
    d, kvw = w_kv.shape
    kw = n_kv * hd
    tm = ROW_TILE
    n_main_tiles = n_main // tm

    def body(dk_ref, dv_ref, raw_ref, kg_ref, c_ref, s_ref, w_ref, h_ref, g_ref, dh_ref, draw_ref, o_ref, dkg_ref, dg_ref):
        i = pl.program_id(0)
        dxs, dgain = _head_prep_bwd(raw_ref[:, :kw], kg_ref[...], c_ref[...], s_ref[...], dk_ref[...], n_kv, hd)
        draw = jnp.concatenate(dxs + [dv_ref[...]], axis=1).astype(BF16)
        draw_ref[...] = draw
        dhn, dg = _rms_bwd(h_ref[...], g_ref[...], _dot_nt(draw, w_ref[...]))
        o_ref[...] = jnp.where(i < n_main_tiles, dh_ref[...], 0.0) + dhn
        _acc_out(dkg_ref, dgain, i == 0)
        _acc_out(dg_ref, dg, i == 0)

    row = lambda w: pl.BlockSpec((tm, w), lambda i: (i, 0))
    one = lambda w: pl.BlockSpec((1, w), lambda i: (0, 0))
    return _call(name, body, (n_rows // tm,),
                 [row(kw), row(kw), row(kvw), one(hd), row(hd), row(hd), pl.BlockSpec((d, kvw), lambda i: (0, 0)), row(d),
                  one(d), pl.BlockSpec((tm, d), lambda i: (jnp.minimum(i, n_main_tiles - 1), 0))],
                 [row(kvw), row(d), one(hd), one(d)],
                 [_sds((n_rows, kvw), BF16), _sds((n_rows, d), F32), _sds((1, hd), F32), _sds((1, d), F32)], after=after)(
                     dk, dv, kvraw, k_gain, cos, sin_s, w_kv, h, g, dh_main)


def _tn_rms(name, h, g, b, n_rows):
    d = h.shape[1]
    nb = b.shape[1]
    tk = _contract_tile(n_rows)
    return _tn(name, (h, g, b),
               [pl.BlockSpec((tk, d), lambda j, k: (k, 0)), pl.BlockSpec((1, d), lambda j, k: (0, 0)),
                pl.BlockSpec((tk, nb), lambda j, k: (k, 0))],
               lambda j, h_ref, g_ref, b_ref: (_rms(h_ref[...], g_ref[...]), b_ref[...]), 1, d, nb, n_rows, tk)


def _tn_plain(name, a, b, nj, a_cols, b_cols, n_rows, a_fn=None):
    tk = _contract_tile(n_rows)
    fa = (lambda v: v) if a_fn is None else a_fn
    a_map = (lambda j, k: (k, j)) if a.shape[1] != a_cols else (lambda j, k: (k, 0))
    b_map = (lambda j, k: (k, j)) if b.shape[1] != b_cols else (lambda j, k: (k, 0))
    return _tn(name, (a, b), [pl.BlockSpec((tk, a_cols), a_map), pl.BlockSpec((tk, b_cols), b_map)],
               lambda j, a_ref, b_ref: (fa(a_ref[...]), b_ref[...]), nj, a_cols, b_cols, n_rows, tk)


def _loss_head(name, y, target, n_rows):
    d = y.shape[1]
    tm = ROW_TILE

    def body(y_ref, t_ref, dy_ref, l_ref):
        e = y_ref[...] - t_ref[...]
        dy_ref[...] = e * (1.0 / d)
        e2 = jnp.sum((e * e).reshape(tm // SUBLANES, SUBLANES, d), axis=0)
        part = e2[:, 0:128]
        for k in range(1, d // 128):
            part = part + e2[:, k * 128:(k + 1) * 128]
        _acc_out(l_ref, part * (0.5 / d), pl.program_id(0) == 0)

    return _call(name, body, (n_rows // tm,),
                 [pl.BlockSpec((tm, d), lambda i: (i, 0)), pl.BlockSpec((tm, d), lambda i: (i, 0))],
                 [pl.BlockSpec((tm, d), lambda i: (i, 0)), pl.BlockSpec((SUBLANES, 128), lambda i: (0, 0))],
                 [_sds((n_rows, d), F32), _sds((SUBLANES, 128), F32)])(y, target)


def _cast_layer(name, a, layer):
    _, r, c = a.shape
    tr = _row_tile(r, 256)

    def body(a_ref, o_ref):
        o_ref[...] = a_ref[0].astype(BF16)

    return _call(name, body, (r // tr,), [pl.BlockSpec((1, tr, c), lambda i: (layer, i, 0))],
                 pl.BlockSpec((tr, c), lambda i: (i, 0)), _sds((r, c), BF16))(a)


def _adamw_math(w, g, m, v):
    c1 = 1.0 - ADAM_B1 ** ADAM_STEP
    c2 = 1.0 - ADAM_B2 ** ADAM_STEP
    nm = ADAM_B1 * m + (1.0 - ADAM_B1) * g
    nv = ADAM_B2 * v + (1.0 - ADAM_B2) * (g * g)
    return -ADAM_LR * ((nm / c1) / (jnp.sqrt(nv / c2) + ADAM_EPS) + ADAM_WD * w), nm, nv


def _adamw(name, w, g, m, v, after=()):
    rows, cols = w.shape
    tr = 128 if rows % 128 == 0 else rows

    def body(w_ref, g_ref, m_ref, v_ref, d_ref, nm_ref, nv_ref):
        d_ref[...], nm_ref[...], nv_ref[...] = _adamw_math(w_ref[...], g_ref[...], m_ref[...], v_ref[...])

    spec = pl.BlockSpec((tr, cols), lambda i: (i, 0))
    return _call(name, body, (rows // tr,), [spec] * 4, [spec] * 3, [_sds((rows, cols), F32)] * 3, after=after)(w, g, m, v)


def _position():
    return lax.axis_index("x"), lax.axis_index("y"), lax.axis_index("c")


def _other_chips(x, y):
    return [(1 - x, y), (x, 1 - y), (1 - x, 1 - y)]


def _peers_chips(x, y, c):
    return [(cx, cy, c) for cx, cy in _other_chips(x, y)]


def _peers_sibling(x, y, c):
    return [(x, y, 1 - c)]


def _peers_chips_and_sibling(x, y, c):
    return _peers_chips(x, y, c) + _peers_sibling(x, y, c)


def _comm_call(name, body, n_in, out_shape, scratch, sequencer=None):
    if sequencer is None:
        return pl.pallas_call(
            body, name=name, in_specs=[_HBM] * n_in, out_specs=[_HBM] * len(out_shape), out_shape=out_shape,
            scratch_shapes=list(scratch),
            compiler_params=pltpu.CompilerParams(has_side_effects=True, vmem_limit_bytes=V7X_VMEM_LIMIT))
    collective_id, peers = sequencer

    def seq_body(*refs):
        barrier = pltpu.get_barrier_semaphore()
        plist = peers(*_position())
        for peer in plist:
            pl.semaphore_signal(barrier, inc=1, device_id=peer, device_id_type=MESH)
        pl.semaphore_wait(barrier, len(plist))
        body(*refs)

    return pl.kernel(seq_body, out_type=out_shape, mesh=plsc.ScalarSubcoreMesh(axis_name="sequencer", num_cores=1), name=name,
                     scratch_types=list(scratch), compiler_params=pltpu.CompilerParams(collective_id=collective_id))


def _n_chunks(rows, want, dtype):
    align = 16 if dtype == BF16 else 8
    n = want
    while n > 1 and (rows % n or (rows // n) % align):
        n -= 1
    return n


def _remote(src, dst, send_sem, recv_sem, device):
    return pltpu.make_async_remote_copy(src_ref=src, dst_ref=dst, send_sem=send_sem, recv_sem=recv_sem,
                                        device_id=device, device_id_type=MESH)


def _start_in_chunks(src, dst, send_sem, recv_sem, device, want=8):
    rows = src.shape[0]
    n = _n_chunks(rows, want, src.dtype)
    for i in range(n):
        part = pl.ds(i * (rows // n), rows // n)
        _remote(src.at[part], dst.at[part], send_sem, recv_sem, device).start()


def _all_gather_chips(name, shards, split, collective_id=None):
    n = len(shards)

    def body(*refs):
        ins, outs = refs[:n], refs[n:2 * n]
        send_sems, recv_sems, local_sems = refs[2 * n:]
        x, y, c = _position()
        me = 2 * x + y
        chips = _other_chips(x, y)
        sibling = (x, y, 1 - c)
        sends, forwards = [], []
        for t in range(n):
            pltpu.make_async_copy(ins[t], outs[t].at[me], local_sems.at[t]).start()
        for t in range(n):
            r = ins[t].shape[0]
            rows = pl.ds(c * (r // 2), r // 2) if split[t] else pl.ds(0, r)
            for k, (cx, cy) in enumerate(chips):
                src, dst = ins[t].at[rows], outs[t].at[me, rows]
                _start_in_chunks(src, dst, send_sems.at[t, k], recv_sems.at[t, k], (cx, cy, c), want=4)
                sends.append(_remote(src, dst, send_sems.at[t, k], recv_sems.at[t, k], (cx, cy, c)))
        for t in range(n):
            r = ins[t].shape[0]
            rows = pl.ds(c * (r // 2), r // 2) if split[t] else pl.ds(0, r)
            for k, (cx, cy) in enumerate(chips):
                landed = outs[t].at[2 * cx + cy, rows]
                _remote(landed, landed, send_sems.at[t, k], recv_sems.at[t, k], (cx, cy, c)).wait_recv()
                if split[t]:
                    _start_in_chunks(landed, landed, send_sems.at[t, 3 + k], recv_sems.at[t, 3 + k], sibling, want=4)
                    forwards.append(_remote(landed, landed, send_sems.at[t, 3 + k], recv_sems.at[t, 3 + k], sibling))
        for t in range(n):
            if split[t]:
                r = ins[t].shape[0]
                other = pl.ds((1 - c) * (r // 2), r // 2)
                for k, (cx, cy) in enumerate(chips):
                    landed = outs[t].at[2 * cx + cy, other]
                    pltpu.make_async_remote_copy(
                        src_ref=landed, dst_ref=landed, send_sem=send_sems.at[t, 3 + k], recv_sem=recv_sems.at[t, 3 + k],
                        device_id=sibling, device_id_type=MESH).wait_recv()
        for cp in sends + forwards:
            cp.wait_send()
        for t in range(n):
            pltpu.make_async_copy(ins[t], outs[t].at[me], local_sems.at[t]).wait()

    out_shape = [_sds((N_CHIPS,) + s.shape, s.dtype) for s in shards]
    sequencer = None if collective_id is None else (collective_id, _peers_chips_and_sibling)
    return _comm_call(name, body, n, out_shape,
                      [pltpu.SemaphoreType.DMA((n, 6)), pltpu.SemaphoreType.DMA((n, 6)), pltpu.SemaphoreType.DMA((n,))],
                      sequencer)(*shards)


def _swap_halves_with_sibling(name, blob, collective_id=None):
    def body(b_ref, theirs_ref, send_sem, recv_sem):
        x, y, c = _position()
        sibling = (x, y, 1 - c)
        for k in range(b_ref.shape[1]):
            _start_in_chunks(b_ref.at[1 - c, k], theirs_ref.at[k], send_sem, recv_sem, sibling)
        _remote(b_ref.at[1 - c], theirs_ref, send_sem, recv_sem, sibling).wait()

    return _comm_call(name, body, 1, [_sds(blob.shape[1:], blob.dtype)],
                      [pltpu.SemaphoreType.DMA(()), pltpu.SemaphoreType.DMA(())],
                      None if collective_id is None else (collective_id, _peers_sibling))(blob)[0]


def _scatter_to_chips(name, parts, collective_id=None):
    def body(p_ref, o_ref, send_sems, recv_sems, local_sems):
        x, y, c = _position()
        me = 2 * x + y
        rows = p_ref.shape[1]
        n_loc = _n_chunks(rows, 16, p_ref.dtype)
        locs = [pltpu.make_async_copy(p_ref.at[me, pl.ds(i * (rows // n_loc), rows // n_loc)],
                                      o_ref.at[me, pl.ds(i * (rows // n_loc), rows // n_loc)], local_sems.at[i])
                for i in range(n_loc)]
        for loc in locs:
            loc.start()
        sends = []
        for k, (cx, cy) in enumerate(_other_chips(x, y)):
            src, dst = p_ref.at[2 * cx + cy], o_ref.at[me]
            _start_in_chunks(src, dst, send_sems.at[k], recv_sems.at[k], (cx, cy, c))
            sends.append(_remote(src, dst, send_sems.at[k], recv_sems.at[k], (cx, cy, c)))
        for k, (cx, cy) in enumerate(_other_chips(x, y)):
            landed = o_ref.at[2 * cx + cy]
            _remote(landed, landed, send_sems.at[k], recv_sems.at[k], (cx, cy, c)).wait_recv()
        for cp in sends:
            cp.wait_send()
        for loc in locs:
            loc.wait()

    def local_sems_shape(rows):
        return pltpu.SemaphoreType.DMA((_n_chunks(rows, 16, parts.dtype),))

    return _comm_call(name, body, 1, [_sds(parts.shape, parts.dtype)],
                      [pltpu.SemaphoreType.DMA((3,)), pltpu.SemaphoreType.DMA((3,)), local_sems_shape(parts.shape[1])],
                      None if collective_id is None else (collective_id, _peers_chips))(parts)[0]


def _share_with_sibling(name, mine, collective_id=None):
    def body(m_ref, o_ref, send_sem, recv_sem):
        x, y, c = _position()
        sibling = (x, y, 1 - c)
        _start_in_chunks(m_ref, o_ref, send_sem, recv_sem, sibling, want=16)
        _remote(m_ref, o_ref, send_sem, recv_sem, sibling).wait()

    return _comm_call(name, body, 1, [_sds(mine.shape, mine.dtype)],
                      [pltpu.SemaphoreType.DMA(()), pltpu.SemaphoreType.DMA(())],
                      None if collective_id is None else (collective_id, _peers_sibling))(mine)[0]


def _row_tile(rows, cap=640):
    best = rows
    for t in range(16, min(rows, cap) + 1, 16):
        if rows % t == 0:
            best = t
    return best


_ANY = pl.BlockSpec(memory_space=pl.ANY)


def _add_my_half(name, blob, theirs, half_index, out_dtype, after):
    n, rows, cols = theirs.shape
    tr = _row_tile(rows)

    def body(c_ref, a_ref, b_ref, after_ref, o_ref):
        o_ref[...] = (a_ref[0].astype(F32) + b_ref[...].astype(F32)).astype(out_dtype)

    spec = pl.BlockSpec((1, tr, cols), lambda k, i, c: (k, i, 0))
    grid_spec = pltpu.PrefetchScalarGridSpec(
        num_scalar_prefetch=1, grid=(n, rows // tr),
        in_specs=[pl.BlockSpec((1, 1, tr, cols), lambda k, i, c: (c[0], k, i, 0)), spec, _ANY], out_specs=spec)
    return pl.pallas_call(
        body, name=name, grid_spec=grid_spec, out_shape=_sds(theirs.shape, out_dtype),
        compiler_params=pltpu.CompilerParams(dimension_semantics=("arbitrary", "arbitrary"),
                                             vmem_limit_bytes=V7X_VMEM_LIMIT))(half_index, blob, theirs, after)


def _sum_slots(name, parts, after):
    n, rows, cols = parts.shape
    tr = _row_tile(rows)

    def body(p_ref, o_ref):
        acc = p_ref[0].astype(F32)
        for k in range(1, n):
            acc = acc + p_ref[k].astype(F32)
        o_ref[...] = acc

    return _call(name, body, (rows // tr,), [pl.BlockSpec((n, tr, cols), lambda i: (0, i, 0))],
                 pl.BlockSpec((tr, cols), lambda i: (i, 0)), _sds((rows, cols), F32), after=(after,))(parts)


def _reduce_small_adamw(name, grads, loss_tile, ws, ms, vs, after=()):
    n = len(grads)
    srcs = list(grads) + [loss_tile]
    after = tuple(after)

    def body(*refs):
        refs = refs[:4 * n + 1] + refs[4 * n + 1 + len(after):]
        g_in, w_in, m_in, v_in = refs[:n + 1], refs[n + 1:2 * n + 1], refs[2 * n + 1:3 * n + 1], refs[3 * n + 1:4 * n + 1]
        outs = refs[4 * n + 1:8 * n + 2]
        g_out, d_out, nm_out, nv_out, loss_out = outs[:n], outs[n:2 * n], outs[2 * n:3 * n], outs[3 * n:4 * n], outs[4 * n]
        bufs = refs[8 * n + 2:9 * n + 3]
        send_sems, recv_sems = refs[9 * n + 3:]
        x, y, c = _position()
        me = 4 * x + 2 * y + c
        chip = 2 * x + y
        peers = [(1 - x if dlt & 4 else x, 1 - y if dlt & 2 else y, 1 - c if dlt & 1 else c) for dlt in range(1, N_DEV)]
        sends = []
        for t in range(n + 1):
            bufs[t][me] = g_in[t][...]
            for k, peer in enumerate(peers):
                cp = _remote(g_in[t], bufs[t].at[me], send_sems.at[t, k], recv_sems.at[t, k], peer)
                cp.start()
                sends.append(cp)
        for t in range(n + 1):
            for k, (tx, ty, tc) in enumerate(peers):
                landed = bufs[t].at[4 * tx + 2 * ty + tc]
                _remote(landed, landed, send_sems.at[t, k], recv_sems.at[t, k], (tx, ty, tc)).wait_recv()
        for cp in sends:
            cp.wait_send()
        for t in range(n + 1):
            total = bufs[t][0]
            for k in range(1, N_DEV):
                total = total + bufs[t][k]
            if t == n:
                loss_out[...] = total
                continue
            cols = w_in[t].shape[1]
            if cols == total.shape[1]:
                g_out[t][...] = total
                d_out[t][...], nm_out[t][...], nv_out[t][...] = _adamw_math(w_in[t][...], total, m_in[t][...], v_in[t][...])
            else:
                for j in range(N_CHIPS):
                    @pl.when(chip == j)
                    def _(t=t, j=j, cols=cols, total=total):
                        mine = total[:, j * cols:(j + 1) * cols]
                        g_out[t][...] = mine
                        d_out[t][...], nm_out[t][...], nv_out[t][...] = _adamw_math(w_in[t][...], mine, m_in[t][...], v_in[t][...])

    w_shapes = [_sds(a.shape, F32) for a in ws]
    return pl.pallas_call(
        body, name=name, in_specs=[_VMEM] * (4 * n + 1) + [_ANY] * len(after), out_specs=[_VMEM] * (4 * n + 1),
        out_shape=w_shapes * 4 + [_sds(loss_tile.shape, F32)],
        scratch_shapes=[pltpu.VMEM((N_DEV,) + a.shape, F32) for a in srcs]
        + [pltpu.SemaphoreType.DMA((n + 1, N_DEV - 1)), pltpu.SemaphoreType.DMA((n + 1, N_DEV - 1))],
        compiler_params=pltpu.CompilerParams(has_side_effects=True, vmem_limit_bytes=V7X_VMEM_LIMIT))(
            *srcs, *ws, *ms, *vs, *after)


_BIG = ("ffn1_w_gate_up", "ffn1_w_down", "ffn2_w_gate_up", "ffn2_w_down", "ssm_w_in", "ssm_w_out", "w_kv", "attn_w_q", "attn_w_o")
_TRANSPOSED = ("ffn1_w_gate_up", "ffn2_w_gate_up")
_SMALL = ("meta_tokens", "ffn1_norm", "mix_norm", "ffn2_norm", "ssm_lambda_re", "ssm_lambda_im", "ssm_b_re", "ssm_b_im",
          "ssm_c_re", "ssm_c_im", "ssm_log_step", "ssm_d", "kv_norm", "k_norm", "q_norm", "attn_sinks")
_ORDER = ("meta_tokens", "ffn1_norm", "ffn1_w_gate_up", "ffn1_w_down", "mix_norm", "ffn2_norm", "ffn2_w_gate_up", "ffn2_w_down",
          "ssm_w_in", "ssm_lambda_re", "ssm_lambda_im", "ssm_b_re", "ssm_b_im", "ssm_c_re", "ssm_c_im", "ssm_log_step", "ssm_d",
          "ssm_w_out", "kv_norm", "w_kv", "k_norm", "attn_w_q", "q_norm", "attn_sinks", "attn_w_o")


def _step(x, target, w, m, v):
    n_ex, seq, d = x.shape
    n_meta = w["meta_tokens"].shape[0]
    n_main = n_ex * seq
    n_all = n_main + n_ex * META_BLOCK
    n_g, n_p, n_c = w["ssm_b_re"].shape[1:]
    hd = w["k_norm"].shape[0]
    n_kv = w["w_kv"].shape[1] // (2 * hd)
    n_q = w["attn_w_q"].shape[2] // hd
    qpk = n_q // n_kv
    px, py, pc = _position()
    chip = 2 * px + py

    def cast(name, layer=0):
        a = w[name]
        return _cast_layer(f"cast_{name}_{layer}", a if a.ndim == 3 else a[None], layer)

    first = [cast("ffn1_w_gate_up"), cast("ffn1_w_down"), cast("ssm_w_in"), cast("ssm_w_out"), w["meta_tokens"], w["ssm_d"]]
    g_a = _all_gather_chips("gather_first", first, [True, True, True, True, False, False], collective_id=12)
    second = [cast("ffn2_w_gate_up"), cast("ffn2_w_down"), cast("w_kv")]
    g_b = _all_gather_chips("gather_second", second, [True] * 3, collective_id=1)
    third = [cast("ffn1_w_gate_up", 1), cast("ffn1_w_down", 1), cast("attn_w_q"), cast("attn_w_o"),
             cast("ffn2_w_gate_up", 1), cast("ffn2_w_down", 1)]
    g_c = _all_gather_chips("gather_third", third, [True] * 6, collective_id=2)
    wgu = {("ffn1", 0): g_a[0], ("ffn1", 1): g_c[0], ("ffn2", 0): g_b[0], ("ffn2", 1): g_c[4]}
    wd = {("ffn1", 0): g_a[1], ("ffn1", 1): g_c[1], ("ffn2", 0): g_b[1], ("ffn2", 1): g_c[5]}
    wd = {key: a.reshape(-1, d) for key, a in wd.items()}
    w_in = g_a[2].reshape(d, -1)
    wout4 = g_a[3]
    w_q = g_c[2].reshape(d, -1)
    w_o = g_c[3].reshape(-1, d)
    w_kv = g_b[2].reshape(d, -1)
    meta_full = jnp.transpose(g_a[4], (1, 0, 2)).reshape(n_meta, d)
    dskip = g_a[5].reshape(1, -1)

    row1 = lambda a: a.reshape(1, -1)
    ssm_args = tuple(w[k][0] for k in ("ssm_lambda_re", "ssm_lambda_im", "ssm_log_step", "ssm_b_re", "ssm_b_im", "ssm_c_re", "ssm_c_im"))
    (bb, cb, a_re, a_im), ssm_vjp = jax.vjp(_ssm_mats, *ssm_args)
    bb16, cb16 = bb.astype(BF16), cb.astype(BF16)
    a_re_s, a_im_s = lax.stop_gradient(a_re), lax.stop_gradient(a_im)
    half = n_g * n_p // 4
    tabs_f = _scan_tables(a_re_s, a_im_s, False)
    tabs_b = _scan_tables(a_re_s, a_im_s, True)

    freqs = ROPE_THETA ** (-jnp.arange(0, hd // 2, dtype=F32) * 2.0 / hd)
    pos_main = jnp.tile(n_meta + jnp.arange(seq), n_ex)
    pos_meta = jnp.tile(jnp.maximum(jnp.arange(META_BLOCK) - (META_BLOCK - n_meta), 0), n_ex)
    ang = jnp.concatenate([pos_main, pos_meta]).astype(F32)[:, None] * freqs[None, :]
    cos = jnp.concatenate([jnp.cos(ang), jnp.cos(ang)], axis=1)
    sin_s = jnp.concatenate([-jnp.sin(ang), jnp.sin(ang)], axis=1)

    meta_block = jnp.concatenate([jnp.zeros((META_BLOCK - n_meta, d), F32), meta_full], axis=0)
    h0 = jnp.concatenate([x.reshape(n_main, d)] + [meta_block] * n_ex, axis=0)

    g = lambda name, layer: row1(w[name][layer])
    h1, gu1 = _ffn_fwd("l0_ffn1", h0, g("ffn1_norm", 0), wgu["ffn1", 0], wd["ffn1", 0], n_all)
    u, bu = _ssm_in("ssm_in", h1, g("mix_norm", 0), w_in, bb16, n_all)
    xs = _scan_fwd("ssm_scan", bu, tabs_f, n_ex, seq)
    h2, y = _ssm_out("ssm_out", xs, u, dskip, cb16, wout4, h1, n_all)
    h3, gu2 = _ffn_fwd("l0_ffn2", h2, g("ffn2_norm", 0), wgu["ffn2", 0], wd["ffn2", 0], n_all)
    kvraw, k, vv = _kv_proj("kv_proj", h3, row1(w["kv_norm"]), w_kv, row1(w["k_norm"]), cos, sin_s, n_all, n_kv, hd)
    h4, gu3 = _ffn_fwd("l1_ffn1", h3, g("ffn1_norm", 1), wgu["ffn1", 1], wd["ffn1", 1], n_main)
    qraw, q = _q_proj("q_proj", h4, g("mix_norm", 1), w_q, row1(w["q_norm"][0]), cos, sin_s, n_main, n_q, hd)
    sinks = row1(w["attn_sinks"][0])
    o, lse = _attn_fwd("attn_fwd", q, k, vv, sinks, n_ex, seq, n_meta, n_kv, qpk, hd)
    h5 = _attn_out("attn_out", o, h4, w_o, n_main)
    h6, gu4 = _ffn_fwd("l1_ffn2", h5, g("ffn2_norm", 1), wgu["ffn2", 1], wd["ffn2", 1], n_main)
    dh6, loss_tile = _loss_head("loss_head", h6, target.reshape(n_main, d), n_main)

    lanes = 1024

    def rs_start(tag, entries, ids):
        pieces = [gr.reshape(N_CHIPS, 2, -1, lanes) for _, _, gr in entries]
        blob = jnp.transpose(jnp.concatenate(pieces, axis=2), (1, 0, 2, 3)).astype(BF16)
        return dict(tag=tag, entries=entries, ids=ids, blob=blob, theirs=_swap_halves_with_sibling(tag + "_swap", blob, ids[0]))

    def rs_scatter(st, after):
        chip_sum = _add_my_half(st["tag"] + "_chip_sum", st["blob"], st["theirs"], jnp.reshape(pc, (1,)).astype(jnp.int32), BF16, after)
        st["chip_sum"] = chip_sum
        st["landed"] = _scatter_to_chips(st["tag"] + "_scatter", chip_sum, st["ids"][1])

    def rs_finish(st, after):
        total = _sum_slots(st["tag"] + "_sum", st["landed"], after)
        st["total"] = total
        other = _share_with_sibling(st["tag"] + "_share", total, st["ids"][2])
        halves = (jnp.where(pc == 0, total, other), jnp.where(pc == 0, other, total))
        out, off = {}, 0
        for name, layer, gr in st["entries"]:
            rows = gr.shape[1] * gr.shape[2] // lanes // 2
            flat = jnp.concatenate([hv[off:off + rows].reshape(-1) for hv in halves])
            if name in _TRANSPOSED:
                flat = flat.reshape(gr.shape[1], gr.shape[2]).T.reshape(-1)
            out[name, layer] = flat
            off += rows
        return out

    small = {}
    dh5, dg_f2l1, dwgu_f2l1, dwd_f2l1 = _ffn_bwd("l1_ffn2", dh6, h5, g("ffn2_norm", 1), gu4, wgu["ffn2", 1], wd["ffn2", 1], n_main)
    do = _attn_out_bwd("attn_out_bwd", dh5, w_o, n_main)
    dw_o = _tn_plain("attn_dwo", o, dh5, 1, o.shape[1], d, n_main).reshape(N_CHIPS, -1, d)
    dq, dk_main, dv_main, dk_meta, dv_meta, dsinks = _attn_bwd("attn_bwd", q, k, vv, sinks, o, lse, do, n_ex, seq, n_meta, n_kv, qpk, hd)
    dqraw, dh4, dq_gain, dg_mix1 = _q_bwd("q_bwd", dq, qraw, row1(w["q_norm"][0]), cos, sin_s, w_q, h4, g("mix_norm", 1), dh5, n_main, n_q, hd)
    dw_q = _tn_rms("attn_dwq", h4, g("mix_norm", 1), dqraw, n_main).reshape(N_CHIPS, -1, dqraw.shape[1])
    dh3m, dg_f1l1, dwgu_f1l1, dwd_f1l1 = _ffn_bwd("l1_ffn1", dh4, h3, g("ffn1_norm", 1), gu3, wgu["ffn1", 1], wd["ffn1", 1], n_main)
    rs1 = rs_start("rs1", [("ffn2_w_gate_up", 1, dwgu_f2l1), ("ffn2_w_down", 1, dwd_f2l1), ("attn_w_o", 0, dw_o),
                           ("attn_w_q", 0, dw_q), ("ffn1_w_gate_up", 1, dwgu_f1l1), ("ffn1_w_down", 1, dwd_f1l1)], (3, 4, 5))

    def with_meta(main, meta):
        blocks = [jnp.pad(meta[b * n_meta:(b + 1) * n_meta], ((META_BLOCK - n_meta, 0), (0, 0))) for b in range(n_ex)]
        return jnp.concatenate([main] + blocks, axis=0)

    dkvraw, dh3, dk_gain, dg_kv = _kv_bwd("kv_bwd", with_meta(dk_main, dk_meta), with_meta(dv_main, dv_meta), kvraw, row1(w["k_norm"]),
                                          cos, sin_s, w_kv, h3, row1(w["kv_norm"]), dh3m, n_all, n_main, n_kv, hd,
                                          after=(rs1["blob"],))
    rs_scatter(rs1, after=dh3)
    dw_kv = _tn_rms("kv_dw", h3, row1(w["kv_norm"]), dkvraw, n_all).reshape(N_CHIPS, -1, dkvraw.shape[1])
    dh2, dg_f2l0, dwgu_f2l0, dwd_f2l0 = _ffn_bwd("l0_ffn2", dh3, h2, g("ffn2_norm", 0), gu2, wgu["ffn2", 0], wd["ffn2", 0], n_all,
                                                 after=(rs1["chip_sum"],))
    reduced = rs_finish(rs1, after=dh2)
    rs0a = rs_start("rs0a", [("w_kv", 0, dw_kv), ("ffn2_w_gate_up", 0, dwgu_f2l0), ("ffn2_w_down", 0, dwd_f2l0)], (6, 7, 8))

    dy, dz, gx, dd = _ssm_out_bwd("ssm_out_bwd", dh2, y, u, cb16, wout4, n_all, after=(rs1["total"], rs0a["blob"]))
    rs_scatter(rs0a, after=dy)
    hw = y.shape[1]
    oc = wout4.shape[2]
    dw_out = _tn_plain("ssm_dwout", y, dz, wout4.shape[0], hw, oc, n_all, a_fn=_gelu)
    dcb = _tn_plain("ssm_dcb", xs, dy, 4, xs.shape[1] // 4, hw // 4, n_all)
    gbu, da = _scan_bwd("ssm_scan_bwd", gx, xs, tabs_b, n_ex, seq, after=(rs0a["chip_sum"],))
    du, dh1, dg_mix0 = _ssm_in_bwd("ssm_in_bwd", gbu, dy, dskip, bb16, w_in, h1, g("mix_norm", 0), dh2, n_all)
    reduced.update(rs_finish(rs0a, after=dh1))
    dbb = _tn_plain("ssm_dbb", u, gbu, 4, hw // 4, gbu.shape[1] // 4, n_all)
    dw_in = _tn_rms("ssm_dwin", h1, g("mix_norm", 0), du, n_all).reshape(N_CHIPS, -1, hw)
    rs0b = {}

    def start_last_group(dwgu, dwd):
        rs0b.update(rs_start("rs0b", [("ssm_w_out", 0, dw_out), ("ssm_w_in", 0, dw_in), ("ffn1_w_gate_up", 0, dwgu),
                                      ("ffn1_w_down", 0, dwd)], (9, 10, 11)))
        return (rs0b["blob"],)

    (dh0, dh0_meta), dg_f1l0, _, _ = _ffn_bwd("l0_ffn1", dh1, h0, g("ffn1_norm", 0), gu1, wgu["ffn1", 0], wd["ffn1", 0], n_all, n_main,
                                              after=(rs0a["total"],), before_dh=start_last_group)
    rs_scatter(rs0b, after=dh0)

    grad_x = dh0.reshape(n_ex, seq, d)
    da_sum = jnp.sum(da, axis=(0, 1)).reshape(4, 2, half)
    d_ssm = ssm_vjp((dbb, dcb, da_sum[:, 0].reshape(-1), da_sum[:, 1].reshape(-1)))
    for key, val in zip(("ssm_lambda_re", "ssm_lambda_im", "ssm_log_step", "ssm_b_re", "ssm_b_im", "ssm_c_re", "ssm_c_im"), d_ssm):
        small[key] = val[None]
    small["meta_tokens"] = sum(dh0_meta[META_BLOCK * (b + 1) - n_meta:META_BLOCK * (b + 1)] for b in range(n_ex))
    small["ffn1_norm"] = jnp.concatenate([dg_f1l0, dg_f1l1], axis=0)
    small["ffn2_norm"] = jnp.concatenate([dg_f2l0, dg_f2l1], axis=0)
    small["mix_norm"] = jnp.concatenate([dg_mix0, dg_mix1], axis=0)
    small["ssm_d"] = dd
    small["kv_norm"] = dg_kv.reshape(-1)
    small["k_norm"] = dk_gain.reshape(-1)
    small["q_norm"] = dq_gain
    small["attn_sinks"] = dsinks

    def view(name, a):
        if name in ("ssm_b_re", "ssm_b_im"):
            return a.reshape(-1, 128)
        return a.reshape(1, -1) if a.ndim == 1 else a.reshape(-1, a.shape[-1])

    grads, deltas, new_m, new_v = {}, {}, {}, {}

    def adamw_matrix(name, after=()):
        shape = w[name].shape
        layers = [reduced[name, layer] for layer in range(2) if (name, layer) in reduced]
        grads[name] = jnp.concatenate(layers).reshape(shape)
        two_d = lambda a: a.reshape(-1, shape[-1])
        dl, nm, nv = _adamw("adamw_" + name, two_d(w[name]), two_d(grads[name]), two_d(m[name]), two_d(v[name]), after=after)
        deltas[name], new_m[name], new_v[name] = dl.reshape(shape), nm.reshape(shape), nv.reshape(shape)
        return nv

    placed = (rs0b["chip_sum"],)
    for name in ("ffn2_w_down", "attn_w_o", "attn_w_q", "w_kv"):
        placed = (adamw_matrix(name, after=placed),)
    tail = _reduce_small_adamw("small_tail", [view(k, small[k]) for k in _SMALL], loss_tile,
                               *[[view(k, t[k]) for k in _SMALL] for t in (w, m, v)], after=placed)
    n_small = len(_SMALL)
    for i, k in enumerate(_SMALL):
        grads[k], deltas[k] = tail[i].reshape(w[k].shape), tail[n_small + i].reshape(w[k].shape)
        new_m[k], new_v[k] = tail[2 * n_small + i].reshape(w[k].shape), tail[3 * n_small + i].reshape(w[k].shape)
    loss = jnp.sum(tail[-1])
    reduced.update(rs_finish(rs0b, after=tail[-1]))
    adamw_matrix("ffn2_w_gate_up", after=(rs0b["total"],))
    for name in ("ffn1_w_gate_up", "ffn1_w_down", "ssm_w_in", "ssm_w_out"):
        adamw_matrix(name)
    return (loss, grad_x, *[grads[k] for k in _ORDER], *[deltas[k] for k in _ORDER], *[new_m[k] for k in _ORDER],
            *[new_v[k] for k in _ORDER])


def kernel(x, meta_tokens, ffn1_norm, ffn1_w_gate_up, ffn1_w_down, mix_norm, ffn2_norm, ffn2_w_gate_up, ffn2_w_down, ssm_w_in, ssm_lambda_re, ssm_lambda_im, ssm_b_re, ssm_b_im, ssm_c_re, ssm_c_im, ssm_log_step, ssm_d, ssm_w_out, kv_norm, w_kv, k_norm, attn_w_q, q_norm, attn_sinks, attn_w_o, loss_target, m_meta_tokens, m_ffn1_norm, m_ffn1_w_gate_up, m_ffn1_w_down, m_mix_norm, m_ffn2_norm, m_ffn2_w_gate_up, m_ffn2_w_down, m_ssm_w_in, m_ssm_lambda_re, m_ssm_lambda_im, m_ssm_b_re, m_ssm_b_im, m_ssm_c_re, m_ssm_c_im, m_ssm_log_step, m_ssm_d, m_ssm_w_out, m_kv_norm, m_w_kv, m_k_norm, m_attn_w_q, m_q_norm, m_attn_sinks, m_attn_w_o, v_meta_tokens, v_ffn1_norm, v_ffn1_w_gate_up, v_ffn1_w_down, v_mix_norm, v_ffn2_norm, v_ffn2_w_gate_up, v_ffn2_w_down, v_ssm_w_in, v_ssm_lambda_re, v_ssm_lambda_im, v_ssm_b_re, v_ssm_b_im, v_ssm_c_re, v_ssm_c_im, v_ssm_log_step, v_ssm_d, v_ssm_w_out, v_kv_norm, v_w_kv, v_k_norm, v_attn_w_q, v_q_norm, v_attn_sinks, v_attn_w_o):
    args = locals()
    w = {k: args[k] for k in _ORDER}
    m = {k: args["m_" + k] for k in _ORDER}
    v = {k: args["v_" + k] for k in _ORDER}
    return _step(x, loss_target, w, m, v)
```

```python
import functools
import math

import jax
import jax.numpy as jnp
from jax import lax
from jax.experimental import pallas as pl
from jax.experimental.pallas import tpu as pltpu
from jax.experimental.pallas import tpu_sc as plsc

F32 = jnp.float32
BF16 = jnp.bfloat16
MESH = pl.DeviceIdType.MESH

EPS = 1e-6
NEG_INF = -1e30
ROPE_THETA = 10000.0
WINDOW = 128
META_BLOCK = 128
ROW_TILE = 256
SUBLANES = 8
V7X_VMEM_LIMIT = 56 * 2**20
N_CHIPS = 4
N_DEV = 8

ADAM_LR = 0.001
ADAM_B1 = 0.9
ADAM_B2 = 0.999
ADAM_EPS = 1e-08
ADAM_WD = 0.01
ADAM_STEP = 10

_HBM = pl.BlockSpec(memory_space=pltpu.HBM)
_VMEM = pl.BlockSpec(memory_space=pltpu.VMEM)


def _call(name, body, grid, in_specs, out_specs, out_shape, scratch=(), after=()):
    after = tuple(after)
    n_in = len(in_specs)

    def wrapped(*refs):
        return body(*refs[:n_in], *refs[n_in + len(after):])

    call = pl.pallas_call(
        wrapped, name=name, grid=grid, in_specs=list(in_specs) + [pl.BlockSpec(memory_space=pl.ANY)] * len(after),
        out_specs=out_specs, out_shape=out_shape, scratch_shapes=list(scratch),
        compiler_params=pltpu.CompilerParams(dimension_semantics=("arbitrary",) * len(grid),
                                             vmem_limit_bytes=V7X_VMEM_LIMIT))
    return lambda *operands: call(*operands, *after)


def _sds(shape, dtype):
    return jax.ShapeDtypeStruct(tuple(shape), dtype)


def _dot(a, b):
    return jnp.dot(a.astype(BF16), b.astype(BF16), preferred_element_type=F32)


def _dot_nt(a, b):
    return lax.dot_general(a.astype(BF16), b.astype(BF16), (((1,), (1,)), ((), ())), preferred_element_type=F32)


def _dot_tn(a, b):
    return lax.dot_general(a.astype(BF16), b.astype(BF16), (((0,), (0,)), ((), ())), preferred_element_type=F32)


def _rms(h, g):
    return h * lax.rsqrt(jnp.mean(h * h, axis=-1, keepdims=True) + EPS) * g


def _rms_bwd(h, g, dn):
    r = lax.rsqrt(jnp.mean(h * h, axis=-1, keepdims=True) + EPS)
    xh = h * r
    dxh = dn * g
    dg = jnp.sum(dn * xh, axis=0, keepdims=True)
    dh = r * (dxh - xh * jnp.mean(dxh * xh, axis=-1, keepdims=True))
    return dh, dg


def _sigmoid(x):
    return 0.5 * jnp.tanh(0.5 * x) + 0.5


def _gelu(y):
    k = math.sqrt(2.0 / math.pi)
    return 0.5 * y * (1.0 + jnp.tanh(k * (y + 0.044715 * y * y * y)))


def _gelu_grad(y):
    k = math.sqrt(2.0 / math.pi)
    t = jnp.tanh(k * (y + 0.044715 * y * y * y))
    return 0.5 * (1.0 + t) + 0.5 * y * (1.0 - t * t) * k * (1.0 + 3.0 * 0.044715 * y * y)


def _partner(x, lane, d):
    width = x.shape[-1]
    return jnp.where((lane & d) == 0, pltpu.roll(x, width - d, 1), pltpu.roll(x, d, 1))


def _split_bf16(x):
    hi = x.astype(BF16)
    return hi, (x - hi.astype(F32)).astype(BF16)


def _head_sums(x, sel):
    hi, lo = _split_bf16(x)
    return jnp.dot(hi, sel, preferred_element_type=F32) + jnp.dot(lo, sel, preferred_element_type=F32)


def _head_expand(v, sel_t):
    hi, lo = _split_bf16(v)
    return jnp.dot(hi, sel_t, preferred_element_type=F32) + jnp.dot(lo, sel_t, preferred_element_type=F32)


def _tile_lanes(t, width):
    return jnp.concatenate([t] * (width // t.shape[-1]), axis=1)


def _head_prep(x, gain_t, cos2, sin2, sel, sel_t, hd):
    width = x.shape[-1]
    lane = lax.broadcasted_iota(jnp.int32, x.shape, 1)
    r = _head_expand(lax.rsqrt(_head_sums(x * x, sel) * (1.0 / hd) + EPS), sel_t)
    y = x * r * gain_t
    return y * _tile_lanes(cos2, width) + _partner(y, lane, hd // 2) * _tile_lanes(sin2, width)


def _head_prep_bwd(x, gain_t, cos2, sin2, sel, sel_t, d_out, hd):
    width = x.shape[-1]
    lane = lax.broadcasted_iota(jnp.int32, x.shape, 1)
    r = _head_expand(lax.rsqrt(_head_sums(x * x, sel) * (1.0 / hd) + EPS), sel_t)
    xhat = x * r
    dy = d_out * _tile_lanes(cos2, width) + _partner(d_out * _tile_lanes(sin2, width), lane, hd // 2)
    dgain = jnp.sum(dy * xhat, axis=0, keepdims=True)
    dxh = dy * gain_t
    mean = _head_expand(_head_sums(dxh * xhat, sel) * (1.0 / hd), sel_t)
    return r * (dxh - xhat * mean), dgain


def _head_selectors(n_heads, hd):
    sel = (jnp.arange(n_heads * hd)[:, None] // hd == jnp.arange(128)[None, :]).astype(BF16)
    return sel, sel.T


def _acc_out(ref, val, first):
    @pl.when(first)
    def _():
        ref[...] = jnp.zeros_like(ref)
    ref[...] += val


def _ffn_up(name, h, g, w4, n_rows):
    nj, d, fc = w4.shape
    tm = ROW_TILE

    def body(h_ref, g_ref, w_ref, o_ref, n_ref):
        n = _rms(h_ref[...], g_ref[...]).astype(BF16)
        n_ref[...] = n
        for j in range(nj):
            o_ref[:, j * fc:(j + 1) * fc] = _dot(n, w_ref[j]).astype(BF16)

    return _call(name, body, (n_rows // tm,),
                 [pl.BlockSpec((tm, d), lambda i: (i, 0)), pl.BlockSpec((1, d), lambda i: (0, 0)),
                  pl.BlockSpec((nj, d, fc), lambda i: (0, 0, 0))],
                 [pl.BlockSpec((tm, nj * fc), lambda i: (i, 0)), pl.BlockSpec((tm, d), lambda i: (i, 0))],
                 [_sds((n_rows, nj * fc), BF16), _sds((n_rows, d), BF16)])(h, g, w4)


def _ffn_down(name, gu, h, wd, n_rows):
    f, d = wd.shape
    tm = ROW_TILE

    def body(gu_ref, h_ref, w_ref, o_ref, s_ref):
        a = gu_ref[:, :f].astype(F32)
        b = gu_ref[:, f:].astype(F32)
        s = (a * _sigmoid(a) * b).astype(BF16)
        s_ref[...] = s
        o_ref[...] = h_ref[...] + 0.5 * _dot(s, w_ref[...])

    return _call(name, body, (n_rows // tm,),
                 [pl.BlockSpec((tm, 2 * f), lambda i: (i, 0)), pl.BlockSpec((tm, d), lambda i: (i, 0)),
                  pl.BlockSpec((f, d), lambda i: (0, 0))],
                 [pl.BlockSpec((tm, d), lambda i: (i, 0)), pl.BlockSpec((tm, f), lambda i: (i, 0))],
                 [_sds((n_rows, d), F32), _sds((n_rows, f), BF16)])(gu, h, wd)


def _ffn_dgu(name, dh, gu, wd, n_rows, after=()):
    f, d = wd.shape
    tm = ROW_TILE

    def body(dh_ref, gu_ref, w_ref, o_ref):
        ds = _dot_nt(0.5 * dh_ref[...], w_ref[...])
        a = gu_ref[:, :f].astype(F32)
        b = gu_ref[:, f:].astype(F32)
        sg = _sigmoid(a)
        o_ref[:, :f] = (ds * b * (sg * (1.0 + a * (1.0 - sg)))).astype(BF16)
        o_ref[:, f:] = (ds * (a * sg)).astype(BF16)

    return _call(name, body, (n_rows // tm,),
                 [pl.BlockSpec((tm, d), lambda i: (i, 0)), pl.BlockSpec((tm, 2 * f), lambda i: (i, 0)),
                  pl.BlockSpec((f, d), lambda i: (0, 0))],
                 pl.BlockSpec((tm, 2 * f), lambda i: (i, 0)), _sds((n_rows, 2 * f), BF16), after=after)(dh, gu, wd)


def _ffn_dh(name, dgu, h, g, dh, w4, n_rows, n_main=None, after=()):
    nj, d, fc = w4.shape
    tm = ROW_TILE
    n_first = (n_rows if n_main is None else n_main) // tm

    def body(dgu_ref, h_ref, g_ref, dh_ref, w_ref, o_ref, *rest):
        dg_ref = rest[-1]
        i = pl.program_id(0)
        dn = _dot_nt(dgu_ref[:, 0:fc], w_ref[0])
        for j in range(1, nj):
            dn = dn + _dot_nt(dgu_ref[:, j * fc:(j + 1) * fc], w_ref[j])
        dhn, dg = _rms_bwd(h_ref[...], g_ref[...], dn)
        val = dh_ref[...] + dhn
        if n_main is None:
            o_ref[...] = val
        else:
            @pl.when(i < n_first)
            def _():
                o_ref[...] = val

            @pl.when(i >= n_first)
            def _():
                rest[0][...] = val
        _acc_out(dg_ref, dg, i == 0)

    out_specs = [pl.BlockSpec((tm, d), lambda i: (jnp.minimum(i, n_first - 1), 0))]
    out_shape = [_sds((n_first * tm, d), F32)]
    if n_main is not None:
        out_specs.append(pl.BlockSpec((tm, d), lambda i: (jnp.maximum(i - n_first, 0), 0)))
        out_shape.append(_sds((n_rows - n_main, d), F32))
    return _call(name, body, (n_rows // tm,),
                 [pl.BlockSpec((tm, nj * fc), lambda i: (i, 0)), pl.BlockSpec((tm, d), lambda i: (i, 0)),
                  pl.BlockSpec((1, d), lambda i: (0, 0)), pl.BlockSpec((tm, d), lambda i: (i, 0)),
                  pl.BlockSpec((nj, d, fc), lambda i: (0, 0, 0))],
                 out_specs + [pl.BlockSpec((1, d), lambda i: (0, 0))],
                 out_shape + [_sds((1, d), F32)], after=after)(dgu, h, g, dh, w4)


def _contract_tile(n_rows, cap=1024):
    best = ROW_TILE
    for t in range(ROW_TILE, cap + 1, ROW_TILE):
        if n_rows % t == 0:
            best = t
    return best


def _tn(name, operands, in_specs, prologue, nj, ma, nb, n_rows, tk):
    def body(*refs):
        o_ref = refs[-1]
        a, b = prologue(pl.program_id(0), *refs[:-1])
        _acc_out(o_ref, _dot_tn(a, b)[None], pl.program_id(1) == 0)

    return _call(name, body, (nj, n_rows // tk), in_specs, pl.BlockSpec((1, ma, nb), lambda j, k: (j, 0, 0)),
                 _sds((nj, ma, nb), F32))(*operands)


def _ffn_dwgu(name, n, dgu, nj, n_rows):
    d = n.shape[1]
    fc = dgu.shape[1] // nj
    tk = _contract_tile(n_rows)
    return _tn(name, (dgu, n),
               [pl.BlockSpec((tk, fc), lambda j, k: (k, j)), pl.BlockSpec((tk, d), lambda j, k: (k, 0))],
               lambda j, a_ref, b_ref: (a_ref[...], b_ref[...]), nj, fc, d, n_rows, tk)


def _ffn_dwd(name, s, dh, n_rows):
    f = s.shape[1]
    d = dh.shape[1]
    tk = _contract_tile(n_rows)
    return _tn(name, (s, dh),
               [pl.BlockSpec((tk, f), lambda j, k: (k, 0)), pl.BlockSpec((tk, d), lambda j, k: (k, 0))],
               lambda j, s_ref, dh_ref: (s_ref[...], 0.5 * dh_ref[...]), 1, f, d, n_rows, tk)


def _ffn_fwd(tag, h, g, w4, wd, n_rows):
    gu, n = _ffn_up(tag + "_up", h, g, w4, n_rows)
    h_out, s = _ffn_down(tag + "_down", gu, h, wd, n_rows)
    return h_out, (gu, n, s)


def _ffn_bwd(tag, dh_out, h, g, saved, w4, wd, n_rows, n_main=None, after=(), before_dh=None):
    gu, n, s = saved
    nj = w4.shape[0]
    f, d = wd.shape
    dgu = _ffn_dgu(tag + "_dgu", dh_out, gu, wd, n_rows, after=after)
    dwd = _ffn_dwd(tag + "_dwd", s, dh_out, n_rows).reshape(N_CHIPS, f // N_CHIPS, d)
    dwgu = _ffn_dwgu(tag + "_dwgu", n, dgu, nj, n_rows)
    dh_after = () if before_dh is None else before_dh(dwgu, dwd)
    *dh_parts, dg = _ffn_dh(tag + "_dh", dgu, h, g, dh_out, w4, n_rows, n_main, after=dh_after)
    dh_in = dh_parts[0] if n_main is None else tuple(dh_parts)
    return dh_in, dg, dwgu, dwd


def _ssm_in(name, h, g, w_in, bb, n_rows):
    d, hw = w_in.shape
    nj, uc, xc = bb.shape
    tm = ROW_TILE

    def body(h_ref, g_ref, w_ref, bb_ref, u_ref, bu_ref):
        u = _dot(_rms(h_ref[...], g_ref[...]), w_ref[...])
        u_ref[...] = u
        for j in range(nj):
            bu_ref[:, j * xc:(j + 1) * xc] = _dot(u[:, j * uc:(j + 1) * uc], bb_ref[j])

    return _call(name, body, (n_rows // tm,),
                 [pl.BlockSpec((tm, d), lambda i: (i, 0)), pl.BlockSpec((1, d), lambda i: (0, 0)),
                  pl.BlockSpec((d, hw), lambda i: (0, 0)), pl.BlockSpec((nj, uc, xc), lambda i: (0, 0, 0))],
                 [pl.BlockSpec((tm, hw), lambda i: (i, 0)), pl.BlockSpec((tm, nj * xc), lambda i: (i, 0))],
                 [_sds((n_rows, hw), F32), _sds((n_rows, nj * xc), F32)])(h, g, w_in, bb)


def _cmul_add(xr, xi, ar, ai, sr, si):
    return xr + ar * sr - ai * si, xi + ar * si + ai * sr


def _scan_row_block(n_main_blocks, seq_blocks):
    return lambda b, i: jnp.where(i == 0, n_main_blocks + b, b * seq_blocks + i - 1)


def _scan_fwd(name, bu, tabs, n_ex, seq):
    n_rows, width = bu.shape
    nj = 4
    cw = width // nj
    half = cw // 2
    tq = META_BLOCK
    seq_blocks = seq // tq
    rb = _scan_row_block(n_ex * seq_blocks, seq_blocks)

    def body(bu_ref, tab_ref, x_ref, carry_ref):
        @pl.when(pl.program_id(1) == 0)
        def _():
            carry_ref[...] = jnp.zeros_like(carry_ref)

        for j in range(nj):
            re, im = slice(j * cw, j * cw + half), slice(j * cw + half, (j + 1) * cw)
            ch = slice(j * half, (j + 1) * half)

            def blk(k, c, re=re, im=im, ch=ch):
                t = [tab_ref[n * SUBLANES:(n + 1) * SUBLANES, ch] for n in range(8)]
                r0 = pl.multiple_of(k * SUBLANES, SUBLANES)
                xr = bu_ref[pl.ds(r0, SUBLANES), re]
                xi = bu_ref[pl.ds(r0, SUBLANES), im]
                for s, d in enumerate((1, 2, 4)):
                    xr, xi = _cmul_add(xr, xi, t[2 * s], t[2 * s + 1], pltpu.roll(xr, d, 0), pltpu.roll(xi, d, 0))
                xr, xi = _cmul_add(xr, xi, t[6], t[7], c[0], c[1])
                x_ref[pl.ds(r0, SUBLANES), re] = xr
                x_ref[pl.ds(r0, SUBLANES), im] = xi
                last = SUBLANES - 1
                return (jnp.broadcast_to(xr[last:last + 1, :], xr.shape), jnp.broadcast_to(xi[last:last + 1, :], xi.shape))

            c = lax.fori_loop(0, tq // SUBLANES, blk, (carry_ref[0, :, ch], carry_ref[1, :, ch]))
            carry_ref[0, :, ch] = c[0]
            carry_ref[1, :, ch] = c[1]

    return _call(name, body, (n_ex, seq_blocks + 1),
                 [pl.BlockSpec((tq, width), lambda b, i: (rb(b, i), 0)), pl.BlockSpec((8 * SUBLANES, nj * half), lambda b, i: (0, 0))],
                 pl.BlockSpec((tq, width), lambda b, i: (rb(b, i), 0)), _sds((n_rows, width), F32),
                 scratch=[pltpu.VMEM((2, SUBLANES, nj * half), F32)])(bu, tabs)


def _scan_bwd(name, gx, x, tabs, n_ex, seq, after=()):
    n_rows, width = gx.shape
    nj = 4
    cw = width // nj
    half = cw // 2
    tq = META_BLOCK
    seq_blocks = seq // tq
    n_steps = seq_blocks + 1
    rb = _scan_row_block(n_ex * seq_blocks, seq_blocks)
    rbr = lambda b, i: rb(b, n_steps - 1 - i)

    def body(gx_ref, x_ref, tab_ref, g_ref, da_ref, carry_ref):
        @pl.when(pl.program_id(1) == 0)
        def _():
            carry_ref[...] = jnp.zeros_like(carry_ref)
            da_ref[...] = jnp.zeros_like(da_ref)
        row = lax.broadcasted_iota(jnp.int32, (SUBLANES, half), 0)
        n_blk = tq // SUBLANES

        for j in range(nj):
            re, im = slice(j * cw, j * cw + half), slice(j * cw + half, (j + 1) * cw)
            ch = slice(j * half, (j + 1) * half)

            def blk(kk, st, re=re, im=im, ch=ch):
                t = [tab_ref[n * SUBLANES:(n + 1) * SUBLANES, ch] for n in range(8)]
                cr, ci, dar, dai = st
                r0 = pl.multiple_of((n_blk - 1 - kk) * SUBLANES, SUBLANES)
                gr = gx_ref[pl.ds(r0, SUBLANES), re]
                gi = gx_ref[pl.ds(r0, SUBLANES), im]
                for s, d in enumerate((1, 2, 4)):
                    gr, gi = _cmul_add(gr, gi, t[2 * s], t[2 * s + 1],
                                       pltpu.roll(gr, SUBLANES - d, 0), pltpu.roll(gi, SUBLANES - d, 0))
                gr, gi = _cmul_add(gr, gi, t[6], t[7], cr, ci)
                g_ref[pl.ds(r0, SUBLANES), re] = gr.astype(BF16)
                g_ref[pl.ds(r0, SUBLANES), im] = gi.astype(BF16)
                hr = jnp.where(row == SUBLANES - 1, cr, pltpu.roll(gr, SUBLANES - 1, 0))
                hi = jnp.where(row == SUBLANES - 1, ci, pltpu.roll(gi, SUBLANES - 1, 0))
                xr = x_ref[pl.ds(r0, SUBLANES), re]
                xi = x_ref[pl.ds(r0, SUBLANES), im]
                dar = dar + xr * hr + xi * hi
                dai = dai + xr * hi - xi * hr
                return (jnp.broadcast_to(gr[0:1, :], gr.shape), jnp.broadcast_to(gi[0:1, :], gi.shape), dar, dai)

            st = lax.fori_loop(0, n_blk, blk, (carry_ref[0, :, ch], carry_ref[1, :, ch], da_ref[0, :, re], da_ref[0, :, im]))
            carry_ref[0, :, ch] = st[0]
            carry_ref[1, :, ch] = st[1]
            da_ref[0, :, re] = st[2]
            da_ref[0, :, im] = st[3]

    return _call(name, body, (n_ex, n_steps),
                 [pl.BlockSpec((tq, width), lambda b, i: (rbr(b, i), 0)), pl.BlockSpec((tq, width), lambda b, i: (rbr(b, i), 0)),
                  pl.BlockSpec((8 * SUBLANES, nj * half), lambda b, i: (0, 0))],
                 [pl.BlockSpec((tq, width), lambda b, i: (rbr(b, i), 0)), pl.BlockSpec((1, SUBLANES, width), lambda b, i: (b, 0, 0))],
                 [_sds((n_rows, width), BF16), _sds((n_ex, SUBLANES, width), F32)],
                 scratch=[pltpu.VMEM((2, SUBLANES, nj * half), F32)], after=after)(gx, x, tabs)


def _ssm_z(gy, wout_ref, nj):
    return jnp.concatenate([_dot(gy, wout_ref[j]) for j in range(nj)], axis=1)


def _ssm_out(name, x, u, dskip, cb, wout4, h, n_rows):
    nj, xc, uc = cb.shape
    no, hw, oc = wout4.shape
    d = h.shape[1]
    tm = ROW_TILE

    def body(x_ref, u_ref, ds_ref, cb_ref, w_ref, h_ref, o_ref, y_ref):
        y = jnp.concatenate([_dot(x_ref[:, j * xc:(j + 1) * xc], cb_ref[j]) for j in range(nj)], axis=1)
        y = y + ds_ref[...] * u_ref[...]
        y_ref[...] = y
        z = _ssm_z(_gelu(y), w_ref, no)
        o_ref[...] = h_ref[...] + z[:, :d] * _sigmoid(z[:, d:])

    return _call(name, body, (n_rows // tm,),
                 [pl.BlockSpec((tm, nj * xc), lambda i: (i, 0)), pl.BlockSpec((tm, hw), lambda i: (i, 0)),
                  pl.BlockSpec((1, hw), lambda i: (0, 0)), pl.BlockSpec((nj, xc, uc), lambda i: (0, 0, 0)),
                  pl.BlockSpec((no, hw, oc), lambda i: (0, 0, 0)), pl.BlockSpec((tm, d), lambda i: (i, 0))],
                 [pl.BlockSpec((tm, d), lambda i: (i, 0)), pl.BlockSpec((tm, hw), lambda i: (i, 0))],
                 [_sds((n_rows, d), F32), _sds((n_rows, hw), F32)])(x, u, dskip, cb, wout4, h)


def _ssm_out_bwd(name, dh, y, u, cb, wout4, n_rows, after=()):
    nj, xc, uc = cb.shape
    no, hw, oc = wout4.shape
    d = dh.shape[1]
    tm = ROW_TILE

    def body(dh_ref, y_ref, u_ref, cb_ref, w_ref, dy_ref, dz_ref, gx_ref, dd_ref):
        y = y_ref[...]
        z = _ssm_z(_gelu(y), w_ref, no)
        za = z[:, :d]
        sg = _sigmoid(z[:, d:])
        dmix = dh_ref[...]
        dz = jnp.concatenate([dmix * sg, dmix * za * sg * (1.0 - sg)], axis=1).astype(BF16)
        dz_ref[...] = dz
        dgy = _dot_nt(dz[:, 0:oc], w_ref[0])
        for j in range(1, no):
            dgy = dgy + _dot_nt(dz[:, j * oc:(j + 1) * oc], w_ref[j])
        dy = dgy * _gelu_grad(y)
        dy_ref[...] = dy
        _acc_out(dd_ref, jnp.sum(dy * u_ref[...], axis=0, keepdims=True), pl.program_id(0) == 0)
        for j in range(nj):
            gx_ref[:, j * xc:(j + 1) * xc] = _dot_nt(dy[:, j * uc:(j + 1) * uc], cb_ref[j])

    return _call(name, body, (n_rows // tm,),
                 [pl.BlockSpec((tm, d), lambda i: (i, 0)), pl.BlockSpec((tm, hw), lambda i: (i, 0)),
                  pl.BlockSpec((tm, hw), lambda i: (i, 0)), pl.BlockSpec((nj, xc, uc), lambda i: (0, 0, 0)),
                  pl.BlockSpec((no, hw, oc), lambda i: (0, 0, 0))],
                 [pl.BlockSpec((tm, hw), lambda i: (i, 0)), pl.BlockSpec((tm, no * oc), lambda i: (i, 0)),
                  pl.BlockSpec((tm, nj * xc), lambda i: (i, 0)), pl.BlockSpec((1, hw), lambda i: (0, 0))],
                 [_sds((n_rows, hw), F32), _sds((n_rows, no * oc), BF16), _sds((n_rows, nj * xc), F32),
                  _sds((1, hw), F32)], after=after)(dh, y, u, cb, wout4)


def _ssm_in_bwd(name, gbu, dy, dskip, bb, w_in, h, g, dh, n_rows):
    nj, uc, xc = bb.shape
    d, hw = w_in.shape
    tm = ROW_TILE

    def body(gb_ref, dy_ref, ds_ref, bb_ref, w_ref, h_ref, g_ref, dh_ref, du_ref, o_ref, dg_ref):
        du = jnp.concatenate([_dot_nt(gb_ref[:, j * xc:(j + 1) * xc], bb_ref[j]) for j in range(nj)], axis=1)
        du = du + dy_ref[...] * ds_ref[...]
        du_ref[...] = du.astype(BF16)
        dhn, dg = _rms_bwd(h_ref[...], g_ref[...], _dot_nt(du, w_ref[...]))
        o_ref[...] = dh_ref[...] + dhn
        _acc_out(dg_ref, dg, pl.program_id(0) == 0)

    return _call(name, body, (n_rows // tm,),
                 [pl.BlockSpec((tm, nj * xc), lambda i: (i, 0)), pl.BlockSpec((tm, hw), lambda i: (i, 0)),
                  pl.BlockSpec((1, hw), lambda i: (0, 0)), pl.BlockSpec((nj, uc, xc), lambda i: (0, 0, 0)),
                  pl.BlockSpec((d, hw), lambda i: (0, 0)), pl.BlockSpec((tm, d), lambda i: (i, 0)),
                  pl.BlockSpec((1, d), lambda i: (0, 0)), pl.BlockSpec((tm, d), lambda i: (i, 0))],
                 [pl.BlockSpec((tm, hw), lambda i: (i, 0)), pl.BlockSpec((tm, d), lambda i: (i, 0)),
                  pl.BlockSpec((1, d), lambda i: (0, 0))],
                 [_sds((n_rows, hw), BF16), _sds((n_rows, d), F32), _sds((1, d), F32)])(gbu, dy, dskip, bb, w_in, h, g, dh)


def _discretize(lam_re, lam_im, log_step, b_re, b_im):
    step = jnp.exp(log_step)[:, None]
    mag = jnp.exp(lam_re * step)
    ar = mag * jnp.cos(lam_im * step)
    ai = mag * jnp.sin(lam_im * step)
    den = lam_re * lam_re + lam_im * lam_im
    nr, ni = ar - 1.0, ai
    cr = (nr * lam_re + ni * lam_im) / den
    ci = (ni * lam_re - nr * lam_im) / den
    bbar_r = cr[..., None] * b_re - ci[..., None] * b_im
    bbar_i = cr[..., None] * b_im + ci[..., None] * b_re
    return ar, ai, bbar_r, bbar_i


def _ssm_mats(lam_re, lam_im, log_step, b_re, b_im, c_re, c_im):
    n_g, n_p, n_c = b_re.shape
    gpc = n_g // 4
    ar, ai, bbar_r, bbar_i = _discretize(lam_re, lam_im, log_step, b_re, b_im)
    eye = jnp.eye(gpc, dtype=F32)

    def in_map(bbar):
        return jnp.einsum('jgpc,gh->jgchp', bbar.reshape(4, gpc, n_p, n_c), eye).reshape(4, gpc * n_c, gpc * n_p)

    def out_map(c):
        return jnp.einsum('jgcp,gh->jgphc', c.reshape(4, gpc, n_c, n_p), eye).reshape(4, gpc * n_p, gpc * n_c)

    bb = jnp.concatenate([in_map(bbar_r), in_map(bbar_i)], axis=2)
    cb = jnp.concatenate([out_map(c_re), -out_map(c_im)], axis=1)
    return bb, cb, ar.reshape(-1), ai.reshape(-1)


def _chunked(v, half):
    return v.reshape(v.shape[:-1] + (4, half))


def _scan_tables(ar, ai, reverse):
    if reverse:
        ai = -ai
    pr, pi = [ar], [ai]
    for _ in range(SUBLANES - 1):
        pr, pi = pr + [pr[-1] * ar - pi[-1] * ai], pi + [pr[-1] * ai + pi[-1] * ar]
    row = jnp.arange(SUBLANES)[:, None]
    tabs = []
    for d in (1, 2, 4):
        keep = (row <= SUBLANES - 1 - d) if reverse else (row >= d)
        tabs += [jnp.where(keep, pr[d - 1][None, :], 0.0), jnp.where(keep, pi[d - 1][None, :], 0.0)]
    order = list(range(SUBLANES))[::-1] if reverse else list(range(SUBLANES))
    tabs += [jnp.stack([pr[k] for k in order]), jnp.stack([pi[k] for k in order])]
    return jnp.concatenate(tabs, axis=0)


def _kv_proj(name, h, g, w_kv, k_gain_t, cos2, sin2, n_rows, n_kv, hd):
    d, kvw = w_kv.shape
    kw = n_kv * hd
    tm = ROW_TILE

    sel, sel_t = _head_selectors(n_kv, hd)

    def body(h_ref, g_ref, w_ref, kg_ref, c_ref, s_ref, e_ref, et_ref, raw_ref, k_ref, v_ref):
        raw = _dot(_rms(h_ref[...], g_ref[...]), w_ref[...])
        raw_ref[...] = raw
        k_ref[...] = _head_prep(raw[:, :kw], kg_ref[...], c_ref[...], s_ref[...], e_ref[...], et_ref[...], hd).astype(BF16)
        v_ref[...] = raw[:, kw:].astype(BF16)

    return _call(name, body, (n_rows // tm,),
                 [pl.BlockSpec((tm, d), lambda i: (i, 0)), pl.BlockSpec((1, d), lambda i: (0, 0)),
                  pl.BlockSpec((d, kvw), lambda i: (0, 0)), pl.BlockSpec((1, kw), lambda i: (0, 0)),
                  pl.BlockSpec((tm, 2 * hd), lambda i: (i, 0)), pl.BlockSpec((tm, 2 * hd), lambda i: (i, 0)),
                  pl.BlockSpec(sel.shape, lambda i: (0, 0)), pl.BlockSpec(sel_t.shape, lambda i: (0, 0))],
                 [pl.BlockSpec((tm, kvw), lambda i: (i, 0)), pl.BlockSpec((tm, kw), lambda i: (i, 0)),
                  pl.BlockSpec((tm, kw), lambda i: (i, 0))],
                 [_sds((n_rows, kvw), F32), _sds((n_rows, kw), BF16), _sds((n_rows, kw), BF16)])(
                     h, g, w_kv, k_gain_t, cos2, sin2, sel, sel_t)


def _q_proj(name, h, g, w_q, q_gain_t, cos2, sin2, n_rows, n_q, hd):
    d, qw = w_q.shape
    tm = ROW_TILE

    sel, sel_t = _head_selectors(n_q, hd)

    def body(h_ref, g_ref, w_ref, qg_ref, c_ref, s_ref, e_ref, et_ref, raw_ref, q_ref):
        raw = _dot(_rms(h_ref[...], g_ref[...]), w_ref[...])
        raw_ref[...] = raw
        q_ref[...] = _head_prep(raw, qg_ref[...], c_ref[...], s_ref[...], e_ref[...], et_ref[...], hd).astype(BF16)

    return _call(name, body, (n_rows // tm,),
                 [pl.BlockSpec((tm, d), lambda i: (i, 0)), pl.BlockSpec((1, d), lambda i: (0, 0)),
                  pl.BlockSpec((d, qw), lambda i: (0, 0)), pl.BlockSpec((1, qw), lambda i: (0, 0)),
                  pl.BlockSpec((tm, 2 * hd), lambda i: (i, 0)), pl.BlockSpec((tm, 2 * hd), lambda i: (i, 0)),
                  pl.BlockSpec(sel.shape, lambda i: (0, 0)), pl.BlockSpec(sel_t.shape, lambda i: (0, 0))],
                 [pl.BlockSpec((tm, qw), lambda i: (i, 0)), pl.BlockSpec((tm, qw), lambda i: (i, 0))],
                 [_sds((n_rows, qw), F32), _sds((n_rows, qw), BF16)])(h, g, w_q, q_gain_t, cos2, sin2, sel, sel_t)


def _attn_specs(seq, n_ex, n_meta, kw):
    nb = seq // WINDOW
    meta_blk = lambda b: (n_ex * seq + META_BLOCK * b + META_BLOCK - n_meta) // n_meta
    return [pl.BlockSpec((WINDOW, kw), lambda b, n: (b * nb + jnp.maximum(n - 1, 0), 0)),
            pl.BlockSpec((WINDOW, kw), lambda b, n: (b * nb + n, 0)),
            pl.BlockSpec((n_meta, kw), lambda b, n: (meta_blk(b), 0))]


def _attn_mask(n, qpk, n_keys):
    rows = qpk * WINDOW
    qi = lax.broadcasted_iota(jnp.int32, (rows, n_keys), 0) & (WINDOW - 1)
    kj = lax.broadcasted_iota(jnp.int32, (rows, n_keys), 1)
    rel = qi + WINDOW - kj
    band = (rel >= 0) & (rel < WINDOW) & ((n > 0) | (kj >= WINDOW))
    return band | (kj >= 2 * WINDOW)


def _stack_heads(ref, h, qpk, hd, dtype=None):
    parts = [ref[:, (h * qpk + gq) * hd:(h * qpk + gq + 1) * hd] for gq in range(qpk)]
    out = jnp.concatenate(parts, axis=0)
    return out if dtype is None else out.astype(dtype)


def _col(tile, c):
    lane = lax.broadcasted_iota(jnp.int32, tile.shape, 1)
    return jnp.sum(jnp.where(lane == c, tile, 0.0), axis=-1, keepdims=True)


def _put_col(col, c, n):
    lane = lax.broadcasted_iota(jnp.int32, (col.shape[0], n), 1)
    return jnp.where(lane == c, col, 0.0)


def _stack_cols(tile, h, qpk):
    return jnp.concatenate([_col(tile, h * qpk + gq) for gq in range(qpk)], axis=0)


def _sink_col(sinks, h, qpk):
    return jnp.concatenate([jnp.broadcast_to(_col(sinks, h * qpk + gq), (WINDOW, 1)) for gq in range(qpk)], axis=0)


def _attn_fwd(name, q, k, v, sinks, n_ex, seq, n_meta, n_kv, qpk, hd):
    nb = seq // WINDOW
    n_q = n_kv * qpk
    kw = n_kv * hd
    qw = n_q * hd
    n_keys = 2 * WINDOW + n_meta
    scale = hd ** -0.5

    def body(q_ref, kp_ref, kc_ref, km_ref, vp_ref, vc_ref, vm_ref, sk_ref, o_ref, lse_ref):
        valid = _attn_mask(pl.program_id(1), qpk, n_keys)
        sinks_v = sk_ref[...]
        o_parts = []
        lse_all = jnp.zeros((WINDOW, n_q), F32)
        for h in range(n_kv):
            hs = slice(h * hd, (h + 1) * hd)
            kb = jnp.concatenate([kp_ref[:, hs], kc_ref[:, hs], km_ref[:, hs]], axis=0)
            vb = jnp.concatenate([vp_ref[:, hs], vc_ref[:, hs], vm_ref[:, hs]], axis=0)
            s = jnp.where(valid, _dot_nt(_stack_heads(q_ref, h, qpk, hd), kb) * scale, NEG_INF)
            skc = _sink_col(sinks_v, h, qpk)
            m = jnp.maximum(jnp.max(s, axis=-1, keepdims=True), skc)
            p = jnp.exp(s - m)
            den = jnp.sum(p, axis=-1, keepdims=True) + jnp.exp(skc - m)
            o = _dot(p, vb) / den
            lse = m + jnp.log(den)
            for gq in range(qpk):
                o_parts.append(o[gq * WINDOW:(gq + 1) * WINDOW])
                lse_all = lse_all + _put_col(lse[gq * WINDOW:(gq + 1) * WINDOW], h * qpk + gq, n_q)
        o_ref[...] = jnp.concatenate(o_parts, axis=1)
        lse_ref[...] = lse_all

    qspec = pl.BlockSpec((WINDOW, qw), lambda b, n: (b * nb + n, 0))
    return _call(name, body, (n_ex, nb),
                 [qspec] + _attn_specs(seq, n_ex, n_meta, kw) + _attn_specs(seq, n_ex, n_meta, kw)
                 + [pl.BlockSpec((1, n_q), lambda b, n: (0, 0))],
                 [qspec, pl.BlockSpec((WINDOW, n_q), lambda b, n: (b * nb + n, 0))],
                 [_sds((n_ex * seq, qw), F32), _sds((n_ex * seq, n_q), F32)])(q, k, k, k, v, v, v, sinks)


def _attn_bwd(name, q, k, v, sinks, o, lse, do, n_ex, seq, n_meta, n_kv, qpk, hd):
    nb = seq // WINDOW
    n_q = n_kv * qpk
    kw = n_kv * hd
    qw = n_q * hd
    n_keys = 2 * WINDOW + n_meta
    scale = hd ** -0.5

    def body(q_ref, kp_ref, kc_ref, km_ref, vp_ref, vc_ref, vm_ref, sk_ref, o_ref, lse_ref, do_ref,
             dq_ref, dk_ref, dv_ref, dkm_ref, dvm_ref, dsk_ref):
        n = pl.program_id(1)

        @pl.when(n == 0)
        def _():
            dk_ref[...] = jnp.zeros_like(dk_ref)
            dv_ref[...] = jnp.zeros_like(dv_ref)
            dkm_ref[...] = jnp.zeros_like(dkm_ref)
            dvm_ref[...] = jnp.zeros_like(dvm_ref)

        @pl.when((n == 0) & (pl.program_id(0) == 0))
        def _():
            dsk_ref[...] = jnp.zeros_like(dsk_ref)

        valid = _attn_mask(n, qpk, n_keys)
        sinks_v = sk_ref[...]
        lse_v = lse_ref[...]
        dq_parts, dk_parts, dv_parts = [], [], []
        dsk = jnp.zeros((1, n_q), F32)
        for h in range(n_kv):
            hs = slice(h * hd, (h + 1) * hd)
            kb = jnp.concatenate([kp_ref[:, hs], kc_ref[:, hs], km_ref[:, hs]], axis=0)
            vb = jnp.concatenate([vp_ref[:, hs], vc_ref[:, hs], vm_ref[:, hs]], axis=0)
            qs = _stack_heads(q_ref, h, qpk, hd)
            dos = _stack_heads(do_ref, h, qpk, hd)
            delta = jnp.sum(dos * _stack_heads(o_ref, h, qpk, hd), axis=-1, keepdims=True)
            lse_c = _stack_cols(lse_v, h, qpk)
            s = jnp.where(valid, _dot_nt(qs, kb) * scale, NEG_INF)
            p = jnp.exp(s - lse_c)
            ds = p * (_dot_nt(dos, vb) - delta)
            dqs = _dot(ds, kb) * scale
            dk_parts.append(_dot_tn(ds, qs) * scale)
            dv_parts.append(_dot_tn(p, dos))
            dsink = -jnp.exp(_sink_col(sinks_v, h, qpk) - lse_c) * delta
            for gq in range(qpk):
                dq_parts.append(dqs[gq * WINDOW:(gq + 1) * WINDOW])
                dsk = dsk + _put_col(jnp.sum(dsink[gq * WINDOW:(gq + 1) * WINDOW], axis=0, keepdims=True), h * qpk + gq, n_q)
        dq_ref[...] = jnp.concatenate(dq_parts, axis=1)
        dsk_ref[...] += dsk
        dkb = jnp.concatenate(dk_parts, axis=1)
        dvb = jnp.concatenate(dv_parts, axis=1)
        prev = pl.ds(pl.multiple_of(jnp.maximum(n - 1, 0) * WINDOW, WINDOW), WINDOW)
        cur = pl.ds(pl.multiple_of(n * WINDOW, WINDOW), WINDOW)
        dk_ref[prev, :] += dkb[0:WINDOW]
        dv_ref[prev, :] += dvb[0:WINDOW]
        dk_ref[cur, :] += dkb[WINDOW:2 * WINDOW]
        dv_ref[cur, :] += dvb[WINDOW:2 * WINDOW]
        dkm_ref[...] += dkb[2 * WINDOW:]
        dvm_ref[...] += dvb[2 * WINDOW:]

    qspec = pl.BlockSpec((WINDOW, qw), lambda b, n: (b * nb + n, 0))
    exspec = pl.BlockSpec((seq, kw), lambda b, n: (b, 0))
    mspec = pl.BlockSpec((n_meta, kw), lambda b, n: (b, 0))
    return _call(name, body, (n_ex, nb),
                 [qspec] + _attn_specs(seq, n_ex, n_meta, kw) + _attn_specs(seq, n_ex, n_meta, kw)
                 + [pl.BlockSpec((1, n_q), lambda b, n: (0, 0)), qspec,
                    pl.BlockSpec((WINDOW, n_q), lambda b, n: (b * nb + n, 0)), qspec],
                 [qspec, exspec, exspec, mspec, mspec, pl.BlockSpec((1, n_q), lambda b, n: (0, 0))],
                 [_sds((n_ex * seq, qw), F32), _sds((n_ex * seq, kw), F32), _sds((n_ex * seq, kw), F32),
                  _sds((n_ex * n_meta, kw), F32), _sds((n_ex * n_meta, kw), F32), _sds((1, n_q), F32)])(
                      q, k, k, k, v, v, v, sinks, o, lse, do)


def _attn_out(name, o, h, w_o, n_rows):
    qw, d = w_o.shape
    tm = ROW_TILE

    def body(o_ref, h_ref, w_ref, out_ref):
        out_ref[...] = h_ref[...] + _dot(o_ref[...], w_ref[...])

    return _call(name, body, (n_rows // tm,),
                 [pl.BlockSpec((tm, qw), lambda i: (i, 0)), pl.BlockSpec((tm, d), lambda i: (i, 0)),
                  pl.BlockSpec((qw, d), lambda i: (0, 0))],
                 pl.BlockSpec((tm, d), lambda i: (i, 0)), _sds((n_rows, d), F32))(o, h, w_o)


def _attn_out_bwd(name, dh, w_o, n_rows):
    qw, d = w_o.shape
    tm = ROW_TILE

    def body(dh_ref, w_ref, do_ref):
        do_ref[...] = _dot_nt(dh_ref[...], w_ref[...])

    return _call(name, body, (n_rows // tm,),
                 [pl.BlockSpec((tm, d), lambda i: (i, 0)), pl.BlockSpec((qw, d), lambda i: (0, 0))],
                 pl.BlockSpec((tm, qw), lambda i: (i, 0)), _sds((n_rows, qw), F32))(dh, w_o)


def _q_bwd(name, dq, qraw, q_gain_t, cos2, sin2, w_q, h, g, dh, n_rows, n_q, hd):
    d, qw = w_q.shape
    tm = ROW_TILE

    sel, sel_t = _head_selectors(n_q, hd)

    def body(dq_ref, raw_ref, qg_ref, c_ref, s_ref, e_ref, et_ref, w_ref, h_ref, g_ref, dh_ref, draw_ref, o_ref, dqg_ref, dg_ref):
        dx, dgain = _head_prep_bwd(raw_ref[...], qg_ref[...], c_ref[...], s_ref[...], e_ref[...], et_ref[...], dq_ref[...], hd)
        draw = dx.astype(BF16)
        draw_ref[...] = draw
        dhn, dg = _rms_bwd(h_ref[...], g_ref[...], _dot_nt(draw, w_ref[...]))
        o_ref[...] = dh_ref[...] + dhn
        first = pl.program_id(0) == 0
        _acc_out(dqg_ref, dgain, first)
        _acc_out(dg_ref, dg, first)

    row = lambda w: pl.BlockSpec((tm, w), lambda i: (i, 0))
    one = lambda w: pl.BlockSpec((1, w), lambda i: (0, 0))
    return _call(name, body, (n_rows // tm,),
                 [row(qw), row(qw), one(qw), row(2 * hd), row(2 * hd), pl.BlockSpec(sel.shape, lambda i: (0, 0)),
                  pl.BlockSpec(sel_t.shape, lambda i: (0, 0)), pl.BlockSpec((d, qw), lambda i: (0, 0)), row(d), one(d), row(d)],
                 [row(qw), row(d), one(qw), one(d)],
                 [_sds((n_rows, qw), BF16), _sds((n_rows, d), F32), _sds((1, qw), F32), _sds((1, d), F32)])(
                     dq, qraw, q_gain_t, cos2, sin2, sel, sel_t, w_q, h, g, dh)


def _kv_bwd(name, dk, dv, kvraw, k_gain_t, cos2, sin2, w_kv, h, g, dh_main, n_rows, n_main, n_kv, hd, after=()):
    d, kvw = w_kv.shape
    kw = n_kv * hd
    tm = ROW_TILE
    n_main_tiles = n_main // tm

    sel, sel_t = _head_selectors(n_kv, hd)

    def body(dk_ref, dv_ref, raw_ref, kg_ref, c_ref, s_ref, e_ref, et_ref, w_ref, h_ref, g_ref, dh_ref, draw_ref, o_ref, dkg_ref,
             dg_ref):
        i = pl.program_id(0)
        dx, dgain = _head_prep_bwd(raw_ref[:, :kw], kg_ref[...], c_ref[...], s_ref[...], e_ref[...], et_ref[...], dk_ref[...], hd)
        draw = jnp.concatenate([dx, dv_ref[...]], axis=1).astype(BF16)
        draw_ref[...] = draw
        dhn, dg = _rms_bwd(h_ref[...], g_ref[...], _dot_nt(draw, w_ref[...]))
        o_ref[...] = jnp.where(i < n_main_tiles, dh_ref[...], 0.0) + dhn
        _acc_out(dkg_ref, dgain, i == 0)
        _acc_out(dg_ref, dg, i == 0)

    row = lambda w: pl.BlockSpec((tm, w), lambda i: (i, 0))
    one = lambda w: pl.BlockSpec((1, w), lambda i: (0, 0))
    return _call(name, body, (n_rows // tm,),
                 [row(kw), row(kw), row(kvw), one(kw), row(2 * hd), row(2 * hd), pl.BlockSpec(sel.shape, lambda i: (0, 0)),
                  pl.BlockSpec(sel_t.shape, lambda i: (0, 0)), pl.BlockSpec((d, kvw), lambda i: (0, 0)), row(d),
                  one(d), pl.BlockSpec((tm, d), lambda i: (jnp.minimum(i, n_main_tiles - 1), 0))],
                 [row(kvw), row(d), one(kw), one(d)],
                 [_sds((n_rows, kvw), BF16), _sds((n_rows, d), F32), _sds((1, kw), F32), _sds((1, d), F32)], after=after)(
                     dk, dv, kvraw, k_gain_t, cos2, sin2, sel, sel_t, w_kv, h, g, dh_main)


def _tn_rms(name, h, g, b, n_rows):
    d = h.shape[1]
    nb = b.shape[1]
    tk = _contract_tile(n_rows)
    return _tn(name, (h, g, b),
               [pl.BlockSpec((tk, d), lambda j, k: (k, 0)), pl.BlockSpec((1, d), lambda j, k: (0, 0)),
                pl.BlockSpec((tk, nb), lambda j, k: (k, 0))],
               lambda j, h_ref, g_ref, b_ref: (_rms(h_ref[...], g_ref[...]), b_ref[...]), 1, d, nb, n_rows, tk)


def _tn_plain(name, a, b, nj, a_cols, b_cols, n_rows, a_fn=None):
    tk = _contract_tile(n_rows)
    fa = (lambda v: v) if a_fn is None else a_fn
    a_map = (lambda j, k: (k, j)) if a.shape[1] != a_cols else (lambda j, k: (k, 0))
    b_map = (lambda j, k: (k, j)) if b.shape[1] != b_cols else (lambda j, k: (k, 0))
    return _tn(name, (a, b), [pl.BlockSpec((tk, a_cols), a_map), pl.BlockSpec((tk, b_cols), b_map)],
               lambda j, a_ref, b_ref: (fa(a_ref[...]), b_ref[...]), nj, a_cols, b_cols, n_rows, tk)


def _loss_head(name, y, target, n_rows):
    d = y.shape[1]
    tm = ROW_TILE

    def body(y_ref, t_ref, dy_ref, l_ref):
        e = y_ref[...] - t_ref[...]
        dy_ref[...] = e * (1.0 / d)
        e2 = jnp.sum((e * e).reshape(tm // SUBLANES, SUBLANES, d), axis=0)
        part = e2[:, 0:128]
        for k in range(1, d // 128):
            part = part + e2[:, k * 128:(k + 1) * 128]
        _acc_out(l_ref, part * (0.5 / d), pl.program_id(0) == 0)

    return _call(name, body, (n_rows // tm,),
                 [pl.BlockSpec((tm, d), lambda i: (i, 0)), pl.BlockSpec((tm, d), lambda i: (i, 0))],
                 [pl.BlockSpec((tm, d), lambda i: (i, 0)), pl.BlockSpec((SUBLANES, 128), lambda i: (0, 0))],
                 [_sds((n_rows, d), F32), _sds((SUBLANES, 128), F32)])(y, target)


def _cast_layer(name, a, layer):
    _, r, c = a.shape
    tr = _row_tile(r, 256)

    def body(a_ref, o_ref):
        o_ref[...] = a_ref[0].astype(BF16)

    return _call(name, body, (r // tr,), [pl.BlockSpec((1, tr, c), lambda i: (layer, i, 0))],
                 pl.BlockSpec((tr, c), lambda i: (i, 0)), _sds((r, c), BF16))(a)


def _adamw_math(w, g, m, v):
    c1 = 1.0 - ADAM_B1 ** ADAM_STEP
    c2 = 1.0 - ADAM_B2 ** ADAM_STEP
    nm = ADAM_B1 * m + (1.0 - ADAM_B1) * g
    nv = ADAM_B2 * v + (1.0 - ADAM_B2) * (g * g)
    return -ADAM_LR * ((nm / c1) / (jnp.sqrt(nv / c2) + ADAM_EPS) + ADAM_WD * w), nm, nv


def _adamw(name, w, g, m, v, after=()):
    rows, cols = w.shape
    tr = 128 if rows % 128 == 0 else rows

    def body(w_ref, g_ref, m_ref, v_ref, d_ref, nm_ref, nv_ref):
        d_ref[...], nm_ref[...], nv_ref[...] = _adamw_math(w_ref[...], g_ref[...], m_ref[...], v_ref[...])

    spec = pl.BlockSpec((tr, cols), lambda i: (i, 0))
    return _call(name, body, (rows // tr,), [spec] * 4, [spec] * 3, [_sds((rows, cols), F32)] * 3, after=after)(w, g, m, v)


def _position():
    return lax.axis_index("x"), lax.axis_index("y"), lax.axis_index("c")


def _other_chips(x, y):
    return [(1 - x, y), (x, 1 - y), (1 - x, 1 - y)]


def _peers_chips(x, y, c):
    return [(cx, cy, c) for cx, cy in _other_chips(x, y)]


def _peers_sibling(x, y, c):
    return [(x, y, 1 - c)]


def _peers_chips_and_sibling(x, y, c):
    return _peers_chips(x, y, c) + _peers_sibling(x, y, c)


def _comm_call(name, body, n_in, out_shape, scratch, sequencer=None):
    if sequencer is None:
        return pl.pallas_call(
            body, name=name, in_specs=[_HBM] * n_in, out_specs=[_HBM] * len(out_shape), out_shape=out_shape,
            scratch_shapes=list(scratch),
            compiler_params=pltpu.CompilerParams(has_side_effects=True, vmem_limit_bytes=V7X_VMEM_LIMIT))
    collective_id, peers = sequencer

    def seq_body(*refs):
        barrier = pltpu.get_barrier_semaphore()
        plist = peers(*_position())
        for peer in plist:
            pl.semaphore_signal(barrier, inc=1, device_id=peer, device_id_type=MESH)
        pl.semaphore_wait(barrier, len(plist))
        body(*refs)

    return pl.kernel(seq_body, out_type=out_shape, mesh=plsc.ScalarSubcoreMesh(axis_name="sequencer", num_cores=1), name=name,
                     scratch_types=list(scratch), compiler_params=pltpu.CompilerParams(collective_id=collective_id))


def _n_chunks(rows, want, dtype):
    align = 16 if dtype == BF16 else 8
    n = want
    while n > 1 and (rows % n or (rows // n) % align):
        n -= 1
    return n


def _remote(src, dst, send_sem, recv_sem, device):
    return pltpu.make_async_remote_copy(src_ref=src, dst_ref=dst, send_sem=send_sem, recv_sem=recv_sem,
                                        device_id=device, device_id_type=MESH)


def _start_in_chunks(src, dst, send_sem, recv_sem, device, want=8):
    rows = src.shape[0]
    n = _n_chunks(rows, want, src.dtype)
    for i in range(n):
        part = pl.ds(i * (rows // n), rows // n)
        _remote(src.at[part], dst.at[part], send_sem, recv_sem, device).start()


def _all_gather_chips(name, shards, split, collective_id=None):
    n = len(shards)

    def body(*refs):
        ins, outs = refs[:n], refs[n:2 * n]
        send_sems, recv_sems, local_sems = refs[2 * n:]
        x, y, c = _position()
        me = 2 * x + y
        chips = _other_chips(x, y)
        sibling = (x, y, 1 - c)
        sends, forwards = [], []
        for t in range(n):
            pltpu.make_async_copy(ins[t], outs[t].at[me], local_sems.at[t]).start()
        for t in range(n):
            r = ins[t].shape[0]
            rows = pl.ds(c * (r // 2), r // 2) if split[t] else pl.ds(0, r)
            for k, (cx, cy) in enumerate(chips):
                src, dst = ins[t].at[rows], outs[t].at[me, rows]
                _start_in_chunks(src, dst, send_sems.at[t, k], recv_sems.at[t, k], (cx, cy, c), want=4)
                sends.append(_remote(src, dst, send_sems.at[t, k], recv_sems.at[t, k], (cx, cy, c)))
        for t in range(n):
            r = ins[t].shape[0]
            rows = pl.ds(c * (r // 2), r // 2) if split[t] else pl.ds(0, r)
            for k, (cx, cy) in enumerate(chips):
                landed = outs[t].at[2 * cx + cy, rows]
                _remote(landed, landed, send_sems.at[t, k], recv_sems.at[t, k], (cx, cy, c)).wait_recv()
                if split[t]:
                    _start_in_chunks(landed, landed, send_sems.at[t, 3 + k], recv_sems.at[t, 3 + k], sibling, want=4)
                    forwards.append(_remote(landed, landed, send_sems.at[t, 3 + k], recv_sems.at[t, 3 + k], sibling))
        for t in range(n):
            if split[t]:
                r = ins[t].shape[0]
                other = pl.ds((1 - c) * (r // 2), r // 2)
                for k, (cx, cy) in enumerate(chips):
                    landed = outs[t].at[2 * cx + cy, other]
                    pltpu.make_async_remote_copy(
                        src_ref=landed, dst_ref=landed, send_sem=send_sems.at[t, 3 + k], recv_sem=recv_sems.at[t, 3 + k],
                        device_id=sibling, device_id_type=MESH).wait_recv()
        for cp in sends + forwards:
            cp.wait_send()
        for t in range(n):
            pltpu.make_async_copy(ins[t], outs[t].at[me], local_sems.at[t]).wait()

    out_shape = [_sds((N_CHIPS,) + s.shape, s.dtype) for s in shards]
    sequencer = None if collective_id is None else (collective_id, _peers_chips_and_sibling)
    return _comm_call(name, body, n, out_shape,
                      [pltpu.SemaphoreType.DMA((n, 6)), pltpu.SemaphoreType.DMA((n, 6)), pltpu.SemaphoreType.DMA((n,))],
                      sequencer)(*shards)


def _swap_halves_with_sibling(name, blob, collective_id=None):
    def body(b_ref, theirs_ref, send_sem, recv_sem):
        x, y, c = _position()
        sibling = (x, y, 1 - c)
        for k in range(b_ref.shape[1]):
            _start_in_chunks(b_ref.at[1 - c, k], theirs_ref.at[k], send_sem, recv_sem, sibling)
        _remote(b_ref.at[1 - c], theirs_ref, send_sem, recv_sem, sibling).wait()

    return _comm_call(name, body, 1, [_sds(blob.shape[1:], blob.dtype)],
                      [pltpu.SemaphoreType.DMA(()), pltpu.SemaphoreType.DMA(())],
                      None if collective_id is None else (collective_id, _peers_sibling))(blob)[0]


def _scatter_to_chips(name, parts, collective_id=None):
    def body(p_ref, o_ref, send_sems, recv_sems, local_sems):
        x, y, c = _position()
        me = 2 * x + y
        rows = p_ref.shape[1]
        n_loc = _n_chunks(rows, 16, p_ref.dtype)
        locs = [pltpu.make_async_copy(p_ref.at[me, pl.ds(i * (rows // n_loc), rows // n_loc)],
                                      o_ref.at[me, pl.ds(i * (rows // n_loc), rows // n_loc)], local_sems.at[i])
                for i in range(n_loc)]
        for loc in locs:
            loc.start()
        sends = []
        for k, (cx, cy) in enumerate(_other_chips(x, y)):
            src, dst = p_ref.at[2 * cx + cy], o_ref.at[me]
            _start_in_chunks(src, dst, send_sems.at[k], recv_sems.at[k], (cx, cy, c))
            sends.append(_remote(src, dst, send_sems.at[k], recv_sems.at[k], (cx, cy, c)))
        for k, (cx, cy) in enumerate(_other_chips(x, y)):
            landed = o_ref.at[2 * cx + cy]
            _remote(landed, landed, send_sems.at[k], recv_sems.at[k], (cx, cy, c)).wait_recv()
        for cp in sends:
            cp.wait_send()
        for loc in locs:
            loc.wait()

    def local_sems_shape(rows):
        return pltpu.SemaphoreType.DMA((_n_chunks(rows, 16, parts.dtype),))

    return _comm_call(name, body, 1, [_sds(parts.shape, parts.dtype)],
                      [pltpu.SemaphoreType.DMA((3,)), pltpu.SemaphoreType.DMA((3,)), local_sems_shape(parts.shape[1])],
                      None if collective_id is None else (collective_id, _peers_chips))(parts)[0]


def _share_with_sibling(name, mine, collective_id=None):
    def body(m_ref, o_ref, send_sem, recv_sem):
        x, y, c = _position()
        sibling = (x, y, 1 - c)
        _start_in_chunks(m_ref, o_ref, send_sem, recv_sem, sibling, want=16)
        _remote(m_ref, o_ref, send_sem, recv_sem, sibling).wait()

    return _comm_call(name, body, 1, [_sds(mine.shape, mine.dtype)],
                      [pltpu.SemaphoreType.DMA(()), pltpu.SemaphoreType.DMA(())],
                      None if collective_id is None else (collective_id, _peers_sibling))(mine)[0]


def _row_tile(rows, cap=640):
    best = rows
    for t in range(16, min(rows, cap) + 1, 16):
        if rows % t == 0:
            best = t
    return best


_ANY = pl.BlockSpec(memory_space=pl.ANY)


def _add_my_half(name, blob, theirs, half_index, out_dtype, after):
    n, rows, cols = theirs.shape
    tr = _row_tile(rows)

    def body(c_ref, a_ref, b_ref, after_ref, o_ref):
        o_ref[...] = (a_ref[0].astype(F32) + b_ref[...].astype(F32)).astype(out_dtype)

    spec = pl.BlockSpec((1, tr, cols), lambda k, i, c: (k, i, 0))
    grid_spec = pltpu.PrefetchScalarGridSpec(
        num_scalar_prefetch=1, grid=(n, rows // tr),
        in_specs=[pl.BlockSpec((1, 1, tr, cols), lambda k, i, c: (c[0], k, i, 0)), spec, _ANY], out_specs=spec)
    return pl.pallas_call(
        body, name=name, grid_spec=grid_spec, out_shape=_sds(theirs.shape, out_dtype),
        compiler_params=pltpu.CompilerParams(dimension_semantics=("arbitrary", "arbitrary"),
                                             vmem_limit_bytes=V7X_VMEM_LIMIT))(half_index, blob, theirs, after)


def _sum_slots(name, parts, after):
    n, rows, cols = parts.shape
    tr = _row_tile(rows)

    def body(p_ref, o_ref):
        acc = p_ref[0].astype(F32)
        for k in range(1, n):
            acc = acc + p_ref[k].astype(F32)
        o_ref[...] = acc

    return _call(name, body, (rows // tr,), [pl.BlockSpec((n, tr, cols), lambda i: (0, i, 0))],
                 pl.BlockSpec((tr, cols), lambda i: (i, 0)), _sds((rows, cols), F32), after=(after,))(parts)


def _reduce_small_adamw(name, grads, loss_tile, ws, ms, vs, after=()):
    n = len(grads)
    srcs = list(grads) + [loss_tile]
    after = tuple(after)

    def body(*refs):
        refs = refs[:4 * n + 1] + refs[4 * n + 1 + len(after):]
        g_in, w_in, m_in, v_in = refs[:n + 1], refs[n + 1:2 * n + 1], refs[2 * n + 1:3 * n + 1], refs[3 * n + 1:4 * n + 1]
        outs = refs[4 * n + 1:8 * n + 2]
        g_out, d_out, nm_out, nv_out, loss_out = outs[:n], outs[n:2 * n], outs[2 * n:3 * n], outs[3 * n:4 * n], outs[4 * n]
        bufs = refs[8 * n + 2:9 * n + 3]
        send_sems, recv_sems = refs[9 * n + 3:]
        x, y, c = _position()
        me = 4 * x + 2 * y + c
        chip = 2 * x + y
        peers = [(1 - x if dlt & 4 else x, 1 - y if dlt & 2 else y, 1 - c if dlt & 1 else c) for dlt in range(1, N_DEV)]
        sends = []
        for t in range(n + 1):
            bufs[t][me] = g_in[t][...]
            for k, peer in enumerate(peers):
                cp = _remote(g_in[t], bufs[t].at[me], send_sems.at[t, k], recv_sems.at[t, k], peer)
                cp.start()
                sends.append(cp)
        for t in range(n + 1):
            for k, (tx, ty, tc) in enumerate(peers):
                landed = bufs[t].at[4 * tx + 2 * ty + tc]
                _remote(landed, landed, send_sems.at[t, k], recv_sems.at[t, k], (tx, ty, tc)).wait_recv()
        for cp in sends:
            cp.wait_send()
        for t in range(n + 1):
            total = bufs[t][0]
            for k in range(1, N_DEV):
                total = total + bufs[t][k]
            if t == n:
                loss_out[...] = total
                continue
            cols = w_in[t].shape[1]
            if cols == total.shape[1]:
                g_out[t][...] = total
                d_out[t][...], nm_out[t][...], nv_out[t][...] = _adamw_math(w_in[t][...], total, m_in[t][...], v_in[t][...])
            else:
                for j in range(N_CHIPS):
                    @pl.when(chip == j)
                    def _(t=t, j=j, cols=cols, total=total):
                        mine = total[:, j * cols:(j + 1) * cols]
                        g_out[t][...] = mine
                        d_out[t][...], nm_out[t][...], nv_out[t][...] = _adamw_math(w_in[t][...], mine, m_in[t][...], v_in[t][...])

    w_shapes = [_sds(a.shape, F32) for a in ws]
    return pl.pallas_call(
        body, name=name, in_specs=[_VMEM] * (4 * n + 1) + [_ANY] * len(after), out_specs=[_VMEM] * (4 * n + 1),
        out_shape=w_shapes * 4 + [_sds(loss_tile.shape, F32)],
        scratch_shapes=[pltpu.VMEM((N_DEV,) + a.shape, F32) for a in srcs]
        + [pltpu.SemaphoreType.DMA((n + 1, N_DEV - 1)), pltpu.SemaphoreType.DMA((n + 1, N_DEV - 1))],
        compiler_params=pltpu.CompilerParams(has_side_effects=True, vmem_limit_bytes=V7X_VMEM_LIMIT))(
            *srcs, *ws, *ms, *vs, *after)


_BIG = ("ffn1_w_gate_up", "ffn1_w_down", "ffn2_w_gate_up", "ffn2_w_down", "ssm_w_in", "ssm_w_out", "w_kv", "attn_w_q", "attn_w_o")
_TRANSPOSED = ("ffn1_w_gate_up", "ffn2_w_gate_up")
_SMALL = ("meta_tokens", "ffn1_norm", "mix_norm", "ffn2_norm", "ssm_lambda_re", "ssm_lambda_im", "ssm_b_re", "ssm_b_im",
          "ssm_c_re", "ssm_c_im", "ssm_log_step", "ssm_d", "kv_norm", "k_norm", "q_norm", "attn_sinks")
_ORDER = ("meta_tokens", "ffn1_norm", "ffn1_w_gate_up", "ffn1_w_down", "mix_norm", "ffn2_norm", "ffn2_w_gate_up", "ffn2_w_down",
          "ssm_w_in", "ssm_lambda_re", "ssm_lambda_im", "ssm_b_re", "ssm_b_im", "ssm_c_re", "ssm_c_im", "ssm_log_step", "ssm_d",
          "ssm_w_out", "kv_norm", "w_kv", "k_norm", "attn_w_q", "q_norm", "attn_sinks", "attn_w_o")


def _step(x, target, w, m, v):
    n_ex, seq, d = x.shape
    n_meta = w["meta_tokens"].shape[0]
    n_main = n_ex * seq
    n_all = n_main + n_ex * META_BLOCK
    n_g, n_p, n_c = w["ssm_b_re"].shape[1:]
    hd = w["k_norm"].shape[0]
    n_kv = w["w_kv"].shape[1] // (2 * hd)
    n_q = w["attn_w_q"].shape[2] // hd
    qpk = n_q // n_kv
    px, py, pc = _position()
    chip = 2 * px + py

    def cast(name, layer=0):
        a = w[name]
        return _cast_layer(f"cast_{name}_{layer}", a if a.ndim == 3 else a[None], layer)

    first = [cast("ffn1_w_gate_up"), cast("ffn1_w_down"), cast("ssm_w_in"), cast("ssm_w_out"), w["meta_tokens"], w["ssm_d"]]
    g_a = _all_gather_chips("gather_first", first, [True, True, True, True, False, False], collective_id=12)
    second = [cast("ffn2_w_gate_up"), cast("ffn2_w_down"), cast("w_kv")]
    g_b = _all_gather_chips("gather_second", second, [True] * 3, collective_id=1)
    third = [cast("ffn1_w_gate_up", 1), cast("ffn1_w_down", 1), cast("attn_w_q"), cast("attn_w_o"),
             cast("ffn2_w_gate_up", 1), cast("ffn2_w_down", 1)]
    g_c = _all_gather_chips("gather_third", third, [True] * 6, collective_id=2)
    wgu = {("ffn1", 0): g_a[0], ("ffn1", 1): g_c[0], ("ffn2", 0): g_b[0], ("ffn2", 1): g_c[4]}
    wd = {("ffn1", 0): g_a[1], ("ffn1", 1): g_c[1], ("ffn2", 0): g_b[1], ("ffn2", 1): g_c[5]}
    wd = {key: a.reshape(-1, d) for key, a in wd.items()}
    w_in = g_a[2].reshape(d, -1)
    wout4 = g_a[3]
    w_q = g_c[2].reshape(d, -1)
    w_o = g_c[3].reshape(-1, d)
    w_kv = g_b[2].reshape(d, -1)
    meta_full = jnp.transpose(g_a[4], (1, 0, 2)).reshape(n_meta, d)
    dskip = g_a[5].reshape(1, -1)

    row1 = lambda a: a.reshape(1, -1)
    ssm_args = tuple(w[k][0] for k in ("ssm_lambda_re", "ssm_lambda_im", "ssm_log_step", "ssm_b_re", "ssm_b_im", "ssm_c_re", "ssm_c_im"))
    (bb, cb, a_re, a_im), ssm_vjp = jax.vjp(_ssm_mats, *ssm_args)
    bb16, cb16 = bb.astype(BF16), cb.astype(BF16)
    a_re_s, a_im_s = lax.stop_gradient(a_re), lax.stop_gradient(a_im)
    half = n_g * n_p // 4
    tabs_f = _scan_tables(a_re_s, a_im_s, False)
    tabs_b = _scan_tables(a_re_s, a_im_s, True)

    freqs = ROPE_THETA ** (-jnp.arange(0, hd // 2, dtype=F32) * 2.0 / hd)
    pos_main = jnp.tile(n_meta + jnp.arange(seq), n_ex)
    pos_meta = jnp.tile(jnp.maximum(jnp.arange(META_BLOCK) - (META_BLOCK - n_meta), 0), n_ex)
    ang = jnp.concatenate([pos_main, pos_meta]).astype(F32)[:, None] * freqs[None, :]
    cos = jnp.concatenate([jnp.cos(ang), jnp.cos(ang)] * 2, axis=1)
    sin_s = jnp.concatenate([-jnp.sin(ang), jnp.sin(ang)] * 2, axis=1)
    k_gain_t = jnp.tile(row1(w["k_norm"]), (1, n_kv))
    q_gain_t = jnp.tile(row1(w["q_norm"][0]), (1, n_q))

    meta_block = jnp.concatenate([jnp.zeros((META_BLOCK - n_meta, d), F32), meta_full], axis=0)
    h0 = jnp.concatenate([x.reshape(n_main, d)] + [meta_block] * n_ex, axis=0)

    g = lambda name, layer: row1(w[name][layer])
    h1, gu1 = _ffn_fwd("l0_ffn1", h0, g("ffn1_norm", 0), wgu["ffn1", 0], wd["ffn1", 0], n_all)
    u, bu = _ssm_in("ssm_in", h1, g("mix_norm", 0), w_in, bb16, n_all)
    xs = _scan_fwd("ssm_scan", bu, tabs_f, n_ex, seq)
    h2, y = _ssm_out("ssm_out", xs, u, dskip, cb16, wout4, h1, n_all)
    h3, gu2 = _ffn_fwd("l0_ffn2", h2, g("ffn2_norm", 0), wgu["ffn2", 0], wd["ffn2", 0], n_all)
    kvraw, k, vv = _kv_proj("kv_proj", h3, row1(w["kv_norm"]), w_kv, k_gain_t, cos, sin_s, n_all, n_kv, hd)
    h4, gu3 = _ffn_fwd("l1_ffn1", h3, g("ffn1_norm", 1), wgu["ffn1", 1], wd["ffn1", 1], n_main)
    qraw, q = _q_proj("q_proj", h4, g("mix_norm", 1), w_q, q_gain_t, cos, sin_s, n_main, n_q, hd)
    sinks = row1(w["attn_sinks"][0])
    o, lse = _attn_fwd("attn_fwd", q, k, vv, sinks, n_ex, seq, n_meta, n_kv, qpk, hd)
    h5 = _attn_out("attn_out", o, h4, w_o, n_main)
    h6, gu4 = _ffn_fwd("l1_ffn2", h5, g("ffn2_norm", 1), wgu["ffn2", 1], wd["ffn2", 1], n_main)
    dh6, loss_tile = _loss_head("loss_head", h6, target.reshape(n_main, d), n_main)

    lanes = 1024

    def rs_start(tag, entries, ids):
        pieces = [gr.reshape(N_CHIPS, 2, -1, lanes) for _, _, gr in entries]
        blob = jnp.transpose(jnp.concatenate(pieces, axis=2), (1, 0, 2, 3)).astype(BF16)
        return dict(tag=tag, entries=entries, ids=ids, blob=blob, theirs=_swap_halves_with_sibling(tag + "_swap", blob, ids[0]))

    def rs_scatter(st, after):
        chip_sum = _add_my_half(st["tag"] + "_chip_sum", st["blob"], st["theirs"], jnp.reshape(pc, (1,)).astype(jnp.int32), BF16, after)
        st["chip_sum"] = chip_sum
        st["landed"] = _scatter_to_chips(st["tag"] + "_scatter", chip_sum, st["ids"][1])

    def rs_finish(st, after):
        total = _sum_slots(st["tag"] + "_sum", st["landed"], after)
        st["total"] = total
        other = _share_with_sibling(st["tag"] + "_share", total, st["ids"][2])
        halves = (jnp.where(pc == 0, total, other), jnp.where(pc == 0, other, total))
        out, off = {}, 0
        for name, layer, gr in st["entries"]:
            rows = gr.shape[1] * gr.shape[2] // lanes // 2
            flat = jnp.concatenate([hv[off:off + rows].reshape(-1) for hv in halves])
            if name in _TRANSPOSED:
                flat = flat.reshape(gr.shape[1], gr.shape[2]).T.reshape(-1)
            out[name, layer] = flat
            off += rows
        return out

    small = {}
    dh5, dg_f2l1, dwgu_f2l1, dwd_f2l1 = _ffn_bwd("l1_ffn2", dh6, h5, g("ffn2_norm", 1), gu4, wgu["ffn2", 1], wd["ffn2", 1], n_main)
    do = _attn_out_bwd("attn_out_bwd", dh5, w_o, n_main)
    dw_o = _tn_plain("attn_dwo", o, dh5, 1, o.shape[1], d, n_main).reshape(N_CHIPS, -1, d)
    dq, dk_main, dv_main, dk_meta, dv_meta, dsinks = _attn_bwd("attn_bwd", q, k, vv, sinks, o, lse, do, n_ex, seq, n_meta, n_kv, qpk, hd)
    dqraw, dh4, dq_gain, dg_mix1 = _q_bwd("q_bwd", dq, qraw, q_gain_t, cos, sin_s, w_q, h4, g("mix_norm", 1), dh5, n_main, n_q, hd)
    dw_q = _tn_rms("attn_dwq", h4, g("mix_norm", 1), dqraw, n_main).reshape(N_CHIPS, -1, dqraw.shape[1])
    dh3m, dg_f1l1, dwgu_f1l1, dwd_f1l1 = _ffn_bwd("l1_ffn1", dh4, h3, g("ffn1_norm", 1), gu3, wgu["ffn1", 1], wd["ffn1", 1], n_main)
    rs1 = rs_start("rs1", [("ffn2_w_gate_up", 1, dwgu_f2l1), ("ffn2_w_down", 1, dwd_f2l1), ("attn_w_o", 0, dw_o),
                           ("attn_w_q", 0, dw_q), ("ffn1_w_gate_up", 1, dwgu_f1l1), ("ffn1_w_down", 1, dwd_f1l1)], (3, 4, 5))

    def with_meta(main, meta):
        blocks = [jnp.pad(meta[b * n_meta:(b + 1) * n_meta], ((META_BLOCK - n_meta, 0), (0, 0))) for b in range(n_ex)]
        return jnp.concatenate([main] + blocks, axis=0)

    dkvraw, dh3, dk_gain, dg_kv = _kv_bwd("kv_bwd", with_meta(dk_main, dk_meta), with_meta(dv_main, dv_meta), kvraw, k_gain_t,
                                          cos, sin_s, w_kv, h3, row1(w["kv_norm"]), dh3m, n_all, n_main, n_kv, hd,
                                          after=(rs1["blob"],))
    rs_scatter(rs1, after=dh3)
    dw_kv = _tn_rms("kv_dw", h3, row1(w["kv_norm"]), dkvraw, n_all).reshape(N_CHIPS, -1, dkvraw.shape[1])
    dh2, dg_f2l0, dwgu_f2l0, dwd_f2l0 = _ffn_bwd("l0_ffn2", dh3, h2, g("ffn2_norm", 0), gu2, wgu["ffn2", 0], wd["ffn2", 0], n_all,
                                                 after=(rs1["chip_sum"],))
    reduced = rs_finish(rs1, after=dh2)
    rs0a = rs_start("rs0a", [("w_kv", 0, dw_kv), ("ffn2_w_gate_up", 0, dwgu_f2l0), ("ffn2_w_down", 0, dwd_f2l0)], (6, 7, 8))

    dy, dz, gx, dd = _ssm_out_bwd("ssm_out_bwd", dh2, y, u, cb16, wout4, n_all, after=(rs1["total"], rs0a["blob"]))
    rs_scatter(rs0a, after=dy)
    hw = y.shape[1]
    oc = wout4.shape[2]
    dw_out = _tn_plain("ssm_dwout", y, dz, wout4.shape[0], hw, oc, n_all, a_fn=_gelu)
    dcb = _tn_plain("ssm_dcb", xs, dy, 4, xs.shape[1] // 4, hw // 4, n_all)
    gbu, da = _scan_bwd("ssm_scan_bwd", gx, xs, tabs_b, n_ex, seq, after=(rs0a["chip_sum"],))
    du, dh1, dg_mix0 = _ssm_in_bwd("ssm_in_bwd", gbu, dy, dskip, bb16, w_in, h1, g("mix_norm", 0), dh2, n_all)
    reduced.update(rs_finish(rs0a, after=dh1))
    dbb = _tn_plain("ssm_dbb", u, gbu, 4, hw // 4, gbu.shape[1] // 4, n_all)
    dw_in = _tn_rms("ssm_dwin", h1, g("mix_norm", 0), du, n_all).reshape(N_CHIPS, -1, hw)
    rs0b = {}

    def start_last_group(dwgu, dwd):
        rs0b.update(rs_start("rs0b", [("ssm_w_out", 0, dw_out), ("ssm_w_in", 0, dw_in), ("ffn1_w_gate_up", 0, dwgu),
                                      ("ffn1_w_down", 0, dwd)], (9, 10, 11)))
        return (rs0b["blob"],)

    (dh0, dh0_meta), dg_f1l0, _, _ = _ffn_bwd("l0_ffn1", dh1, h0, g("ffn1_norm", 0), gu1, wgu["ffn1", 0], wd["ffn1", 0], n_all, n_main,
                                              after=(rs0a["total"],), before_dh=start_last_group)
    rs_scatter(rs0b, after=dh0)

    grad_x = dh0.reshape(n_ex, seq, d)
    da_sum = jnp.sum(da, axis=(0, 1)).reshape(4, 2, half)
    d_ssm = ssm_vjp((dbb, dcb, da_sum[:, 0].reshape(-1), da_sum[:, 1].reshape(-1)))
    for key, val in zip(("ssm_lambda_re", "ssm_lambda_im", "ssm_log_step", "ssm_b_re", "ssm_b_im", "ssm_c_re", "ssm_c_im"), d_ssm):
        small[key] = val[None]
    small["meta_tokens"] = sum(dh0_meta[META_BLOCK * (b + 1) - n_meta:META_BLOCK * (b + 1)] for b in range(n_ex))
    small["ffn1_norm"] = jnp.concatenate([dg_f1l0, dg_f1l1], axis=0)
    small["ffn2_norm"] = jnp.concatenate([dg_f2l0, dg_f2l1], axis=0)
    small["mix_norm"] = jnp.concatenate([dg_mix0, dg_mix1], axis=0)
    small["ssm_d"] = dd
    small["kv_norm"] = dg_kv.reshape(-1)
    small["k_norm"] = jnp.sum(dk_gain.reshape(n_kv, hd), axis=0)
    small["q_norm"] = jnp.sum(dq_gain.reshape(n_q, hd), axis=0, keepdims=True)
    small["attn_sinks"] = dsinks

    def view(name, a):
        if name in ("ssm_b_re", "ssm_b_im"):
            return a.reshape(-1, 128)
        return a.reshape(1, -1) if a.ndim == 1 else a.reshape(-1, a.shape[-1])

    grads, deltas, new_m, new_v = {}, {}, {}, {}

    def adamw_matrix(name, after=()):
        shape = w[name].shape
        layers = [reduced[name, layer] for layer in range(2) if (name, layer) in reduced]
        grads[name] = jnp.concatenate(layers).reshape(shape)
        two_d = lambda a: a.reshape(-1, shape[-1])
        dl, nm, nv = _adamw("adamw_" + name, two_d(w[name]), two_d(grads[name]), two_d(m[name]), two_d(v[name]), after=after)
        deltas[name], new_m[name], new_v[name] = dl.reshape(shape), nm.reshape(shape), nv.reshape(shape)
        return nv

    placed = (rs0b["chip_sum"],)
    for name in ("ffn2_w_down", "attn_w_o", "attn_w_q", "w_kv"):
        placed = (adamw_matrix(name, after=placed),)
    tail = _reduce_small_adamw("small_tail", [view(k, small[k]) for k in _SMALL], loss_tile,
                               *[[view(k, t[k]) for k in _SMALL] for t in (w, m, v)], after=placed)
    n_small = len(_SMALL)
    for i, k in enumerate(_SMALL):
        grads[k], deltas[k] = tail[i].reshape(w[k].shape), tail[n_small + i].reshape(w[k].shape)
        new_m[k], new_v[k] = tail[2 * n_small + i].reshape(w[k].shape), tail[3 * n_small + i].reshape(w[k].shape)
    loss = jnp.sum(tail[-1])
    reduced.update(rs_finish(rs0b, after=tail[-1]))
    adamw_matrix("ffn2_w_gate_up", after=(rs0b["total"],))
    for name in ("ffn1_w_gate_up", "ffn1_w_down", "ssm_w_in", "ssm_w_out"):
        adamw_matrix(name)
    return (loss, grad_x, *[grads[k] for k in _ORDER], *[deltas[k] for k in _ORDER], *[new_m[k] for k in _ORDER],
            *[new_v[k] for k in _ORDER])


def kernel(x, meta_tokens, ffn1_norm, ffn1_w_gate_up, ffn1_w_down, mix_norm, ffn2_norm, ffn2_w_gate_up, ffn2_w_down, ssm_w_in, ssm_lambda_re, ssm_lambda_im, ssm_b_re, ssm_b_im, ssm_c_re, ssm_c_im, ssm_log_step, ssm_d, ssm_w_out, kv_norm, w_kv, k_norm, attn_w_q, q_norm, attn_sinks, attn_w_o, loss_target, m_meta_tokens, m_ffn1_norm, m_ffn1_w_gate_up, m_ffn1_w_down, m_mix_norm, m_ffn2_norm, m_ffn2_w_gate_up, m_ffn2_w_down, m_ssm_w_in, m_ssm_lambda_re, m_ssm_lambda_im, m_ssm_b_re, m_ssm_b_im, m_ssm_c_re, m_ssm_c_im, m_ssm_log_step, m_ssm_d, m_ssm_w_out, m_kv_norm, m_w_kv, m_k_norm, m_attn_w_q, m_q_norm, m_attn_sinks, m_attn_w_o, v_meta_tokens, v_ffn1_norm, v_ffn1_w_gate_up, v_ffn1_w_down, v_mix_norm, v_ffn2_norm, v_ffn2_w_gate_up, v_ffn2_w_down, v_ssm_w_in, v_ssm_lambda_re, v_ssm_lambda_im, v_ssm_b_re, v_ssm_b_im, v_ssm_c_re, v_ssm_c_im, v_ssm_log_step, v_ssm_d, v_ssm_w_out, v_kv_norm, v_w_kv, v_k_norm, v_attn_w_q, v_q_norm, v_attn_sinks, v_attn_w_o):
    args = locals()
    w = {k: args[k] for k in _ORDER}
    m = {k: args["m_" + k] for k in _ORDER}
    v = {k: args["v_" + k] for k in _ORDER}
    return _step(x, loss_target, w, m, v)
```

```python
import functools
import math

import jax
import jax.numpy as jnp
from jax import lax
from jax.experimental import pallas as pl
from jax.experimental.pallas import tpu as pltpu
from jax.experimental.pallas import tpu_sc as plsc

F32 = jnp.float32
BF16 = jnp.bfloat16
MESH = pl.DeviceIdType.MESH

EPS = 1e-6
NEG_INF = -1e30
ROPE_THETA = 10000.0
WINDOW = 128
META_BLOCK = 128
ROW_TILE = 256
SUBLANES = 8
V7X_VMEM_LIMIT = 56 * 2**20
N_CHIPS = 4
N_DEV = 8

ADAM_LR = 0.001
ADAM_B1 = 0.9
ADAM_B2 = 0.999
ADAM_EPS = 1e-08
ADAM_WD = 0.01
ADAM_STEP = 10

_HBM = pl.BlockSpec(memory_space=pltpu.HBM)
_VMEM = pl.BlockSpec(memory_space=pltpu.VMEM)


def _call(name, body, grid, in_specs, out_specs, out_shape, scratch=(), after=()):
    after = tuple(after)
    n_in = len(in_specs)

    def wrapped(*refs):
        return body(*refs[:n_in], *refs[n_in + len(after):])

    call = pl.pallas_call(
        wrapped, name=name, grid=grid, in_specs=list(in_specs) + [pl.BlockSpec(memory_space=pl.ANY)] * len(after),
        out_specs=out_specs, out_shape=out_shape, scratch_shapes=list(scratch),
        compiler_params=pltpu.CompilerParams(dimension_semantics=("arbitrary",) * len(grid),
                                             vmem_limit_bytes=V7X_VMEM_LIMIT))
    return lambda *operands: call(*operands, *after)


def _sds(shape, dtype):
    return jax.ShapeDtypeStruct(tuple(shape), dtype)


def _dot(a, b):
    return jnp.dot(a.astype(BF16), b.astype(BF16), preferred_element_type=F32)


def _dot_nt(a, b):
    return lax.dot_general(a.astype(BF16), b.astype(BF16), (((1,), (1,)), ((), ())), preferred_element_type=F32)


def _dot_tn(a, b):
    return lax.dot_general(a.astype(BF16), b.astype(BF16), (((0,), (0,)), ((), ())), preferred_element_type=F32)


def _rms(h, g):
    return h * lax.rsqrt(jnp.mean(h * h, axis=-1, keepdims=True) + EPS) * g


def _rms_bwd(h, g, dn):
    r = lax.rsqrt(jnp.mean(h * h, axis=-1, keepdims=True) + EPS)
    xh = h * r
    dxh = dn * g
    dg = jnp.sum(dn * xh, axis=0, keepdims=True)
    dh = r * (dxh - xh * jnp.mean(dxh * xh, axis=-1, keepdims=True))
    return dh, dg


def _sigmoid(x):
    return 0.5 * jnp.tanh(0.5 * x) + 0.5


def _gelu(y):
    k = math.sqrt(2.0 / math.pi)
    return 0.5 * y * (1.0 + jnp.tanh(k * (y + 0.044715 * y * y * y)))


def _gelu_grad(y):
    k = math.sqrt(2.0 / math.pi)
    t = jnp.tanh(k * (y + 0.044715 * y * y * y))
    return 0.5 * (1.0 + t) + 0.5 * y * (1.0 - t * t) * k * (1.0 + 3.0 * 0.044715 * y * y)


def _partner(x, lane, d):
    width = x.shape[-1]
    return jnp.where((lane & d) == 0, pltpu.roll(x, width - d, 1), pltpu.roll(x, d, 1))


def _split_bf16(x):
    hi = x.astype(BF16)
    return hi, (x - hi.astype(F32)).astype(BF16)


def _head_sums(x, sel):
    hi, lo = _split_bf16(x)
    return jnp.dot(hi, sel, preferred_element_type=F32) + jnp.dot(lo, sel, preferred_element_type=F32)


def _head_expand(v, sel_t):
    hi, lo = _split_bf16(v)
    return jnp.dot(hi, sel_t, preferred_element_type=F32) + jnp.dot(lo, sel_t, preferred_element_type=F32)


def _tile_lanes(t, width):
    return jnp.concatenate([t] * (width // t.shape[-1]), axis=1)


def _head_prep(x, gain_t, cos2, sin2, sel, sel_t, hd):
    width = x.shape[-1]
    lane = lax.broadcasted_iota(jnp.int32, x.shape, 1)
    r = _head_expand(lax.rsqrt(_head_sums(x * x, sel) * (1.0 / hd) + EPS), sel_t)
    y = x * r * gain_t
    return y * _tile_lanes(cos2, width) + _partner(y, lane, hd // 2) * _tile_lanes(sin2, width)


def _head_prep_bwd(x, gain_t, cos2, sin2, sel, sel_t, d_out, hd):
    width = x.shape[-1]
    lane = lax.broadcasted_iota(jnp.int32, x.shape, 1)
    r = _head_expand(lax.rsqrt(_head_sums(x * x, sel) * (1.0 / hd) + EPS), sel_t)
    xhat = x * r
    dy = d_out * _tile_lanes(cos2, width) + _partner(d_out * _tile_lanes(sin2, width), lane, hd // 2)
    dgain = jnp.sum(dy * xhat, axis=0, keepdims=True)
    dxh = dy * gain_t
    mean = _head_expand(_head_sums(dxh * xhat, sel) * (1.0 / hd), sel_t)
    return r * (dxh - xhat * mean), dgain


def _head_selectors(n_heads, hd):
    sel = (jnp.arange(n_heads * hd)[:, None] // hd == jnp.arange(128)[None, :]).astype(BF16)
    return sel, sel.T


def _acc_out(ref, val, first):
    @pl.when(first)
    def _():
        ref[...] = jnp.zeros_like(ref)
    ref[...] += val


def _ffn_up(name, h, g, w4, n_rows):
    nj, d, fc = w4.shape
    tm = ROW_TILE

    def body(h_ref, g_ref, w_ref, o_ref, n_ref):
        n = _rms(h_ref[...], g_ref[...]).astype(BF16)
        n_ref[...] = n
        for j in range(nj):
            o_ref[:, j * fc:(j + 1) * fc] = _dot(n, w_ref[j]).astype(BF16)

    return _call(name, body, (n_rows // tm,),
                 [pl.BlockSpec((tm, d), lambda i: (i, 0)), pl.BlockSpec((1, d), lambda i: (0, 0)),
                  pl.BlockSpec((nj, d, fc), lambda i: (0, 0, 0))],
                 [pl.BlockSpec((tm, nj * fc), lambda i: (i, 0)), pl.BlockSpec((tm, d), lambda i: (i, 0))],
                 [_sds((n_rows, nj * fc), BF16), _sds((n_rows, d), BF16)])(h, g, w4)


def _ffn_down(name, gu, h, wd, n_rows):
    f, d = wd.shape
    tm = ROW_TILE

    def body(gu_ref, h_ref, w_ref, o_ref, s_ref):
        half_a = gu_ref[:, :f] * 0.5
        s = (half_a + half_a * jnp.tanh(half_a)) * gu_ref[:, f:]
        s_ref[...] = s
        o_ref[...] = h_ref[...] + 0.5 * _dot(s, w_ref[...])

    return _call(name, body, (n_rows // tm,),
                 [pl.BlockSpec((tm, 2 * f), lambda i: (i, 0)), pl.BlockSpec((tm, d), lambda i: (i, 0)),
                  pl.BlockSpec((f, d), lambda i: (0, 0))],
                 [pl.BlockSpec((tm, d), lambda i: (i, 0)), pl.BlockSpec((tm, f), lambda i: (i, 0))],
                 [_sds((n_rows, d), F32), _sds((n_rows, f), BF16)])(gu, h, wd)


def _ffn_dgu(name, dh, gu, wd, n_rows, after=()):
    f, d = wd.shape
    tm = ROW_TILE

    def body(dh_ref, gu_ref, w_ref, o_ref):
        ds = _dot_nt(0.5 * dh_ref[...], w_ref[...]).astype(BF16)
        half_a = gu_ref[:, :f] * 0.5
        t = jnp.tanh(half_a)
        o_ref[:, :f] = ds * gu_ref[:, f:] * ((1.0 + t + half_a * (1.0 - t * t)) * 0.5)
        o_ref[:, f:] = ds * (half_a + half_a * t)

    return _call(name, body, (n_rows // tm,),
                 [pl.BlockSpec((tm, d), lambda i: (i, 0)), pl.BlockSpec((tm, 2 * f), lambda i: (i, 0)),
                  pl.BlockSpec((f, d), lambda i: (0, 0))],
                 pl.BlockSpec((tm, 2 * f), lambda i: (i, 0)), _sds((n_rows, 2 * f), BF16), after=after)(dh, gu, wd)


def _ffn_dh(name, dgu, h, g, dh, w4, n_rows, n_main=None, after=()):
    nj, d, fc = w4.shape
    tm = ROW_TILE
    n_first = (n_rows if n_main is None else n_main) // tm

    def body(dgu_ref, h_ref, g_ref, dh_ref, w_ref, o_ref, *rest):
        dg_ref = rest[-1]
        i = pl.program_id(0)
        dn = _dot_nt(dgu_ref[:, 0:fc], w_ref[0])
        for j in range(1, nj):
            dn = dn + _dot_nt(dgu_ref[:, j * fc:(j + 1) * fc], w_ref[j])
        dhn, dg = _rms_bwd(h_ref[...], g_ref[...], dn)
        val = dh_ref[...] + dhn
        if n_main is None:
            o_ref[...] = val
        else:
            @pl.when(i < n_first)
            def _():
                o_ref[...] = val

            @pl.when(i >= n_first)
            def _():
                rest[0][...] = val
        _acc_out(dg_ref, dg, i == 0)

    out_specs = [pl.BlockSpec((tm, d), lambda i: (jnp.minimum(i, n_first - 1), 0))]
    out_shape = [_sds((n_first * tm, d), F32)]
    if n_main is not None:
        out_specs.append(pl.BlockSpec((tm, d), lambda i: (jnp.maximum(i - n_first, 0), 0)))
        out_shape.append(_sds((n_rows - n_main, d), F32))
    return _call(name, body, (n_rows // tm,),
                 [pl.BlockSpec((tm, nj * fc), lambda i: (i, 0)), pl.BlockSpec((tm, d), lambda i: (i, 0)),
                  pl.BlockSpec((1, d), lambda i: (0, 0)), pl.BlockSpec((tm, d), lambda i: (i, 0)),
                  pl.BlockSpec((nj, d, fc), lambda i: (0, 0, 0))],
                 out_specs + [pl.BlockSpec((1, d), lambda i: (0, 0))],
                 out_shape + [_sds((1, d), F32)], after=after)(dgu, h, g, dh, w4)


def _contract_tile(n_rows, cap=1024):
    best = ROW_TILE
    for t in range(ROW_TILE, cap + 1, ROW_TILE):
        if n_rows % t == 0:
            best = t
    return best


def _tn(name, operands, in_specs, prologue, nj, ma, nb, n_rows, tk, out_dtype=F32):
    n_k = n_rows // tk
    out_spec = pl.BlockSpec((1, ma, nb), lambda j, k: (j, 0, 0))
    if out_dtype == F32:
        def body(*refs):
            o_ref = refs[-1]
            a, b = prologue(pl.program_id(0), *refs[:-1])
            _acc_out(o_ref, _dot_tn(a, b)[None], pl.program_id(1) == 0)

        return _call(name, body, (nj, n_k), in_specs, out_spec, _sds((nj, ma, nb), F32))(*operands)

    def body_rounded(*refs):
        o_ref, acc_ref = refs[-2:]
        a, b = prologue(pl.program_id(0), *refs[:-2])
        _acc_out(acc_ref, _dot_tn(a, b), pl.program_id(1) == 0)

        @pl.when(pl.program_id(1) == n_k - 1)
        def _():
            o_ref[0] = acc_ref[...].astype(out_dtype)

    return _call(name, body_rounded, (nj, n_k), in_specs, out_spec, _sds((nj, ma, nb), out_dtype),
                 scratch=[pltpu.VMEM((ma, nb), F32)])(*operands)


def _ffn_dwgu(name, n, dgu, nj, n_rows):
    d = n.shape[1]
    fc = dgu.shape[1] // nj
    tk = _contract_tile(n_rows)
    return _tn(name, (dgu, n),
               [pl.BlockSpec((tk, fc), lambda j, k: (k, j)), pl.BlockSpec((tk, d), lambda j, k: (k, 0))],
               lambda j, a_ref, b_ref: (a_ref[...], b_ref[...]), nj, fc, d, n_rows, tk, out_dtype=BF16)


def _ffn_dwd(name, s, dh, n_rows):
    f = s.shape[1]
    d = dh.shape[1]
    tk = _contract_tile(n_rows)
    return _tn(name, (s, dh),
               [pl.BlockSpec((tk, f), lambda j, k: (k, 0)), pl.BlockSpec((tk, d), lambda j, k: (k, 0))],
               lambda j, s_ref, dh_ref: (s_ref[...], 0.5 * dh_ref[...]), 1, f, d, n_rows, tk, out_dtype=BF16)


def _ffn_fwd(tag, h, g, w4, wd, n_rows):
    gu, n = _ffn_up(tag + "_up", h, g, w4, n_rows)
    h_out, s = _ffn_down(tag + "_down", gu, h, wd, n_rows)
    return h_out, (gu, n, s)


def _ffn_bwd(tag, dh_out, h, g, saved, w4, wd, n_rows, n_main=None, after=(), before_dh=None):
    gu, n, s = saved
    nj = w4.shape[0]
    f, d = wd.shape
    dgu = _ffn_dgu(tag + "_dgu", dh_out, gu, wd, n_rows, after=after)
    dwd = _ffn_dwd(tag + "_dwd", s, dh_out, n_rows).reshape(N_CHIPS, f // N_CHIPS, d)
    dwgu = _ffn_dwgu(tag + "_dwgu", n, dgu, nj, n_rows)
    dh_after = () if before_dh is None else before_dh(dwgu, dwd)
    *dh_parts, dg = _ffn_dh(tag + "_dh", dgu, h, g, dh_out, w4, n_rows, n_main, after=dh_after)
    dh_in = dh_parts[0] if n_main is None else tuple(dh_parts)
    return dh_in, dg, dwgu, dwd


def _ssm_in(name, h, g, w_in, bb, n_rows):
    d, hw = w_in.shape
    nj, uc, xc = bb.shape
    tm = ROW_TILE

    def body(h_ref, g_ref, w_ref, bb_ref, u_ref, bu_ref):
        u = _dot(_rms(h_ref[...], g_ref[...]), w_ref[...])
        u_ref[...] = u
        for j in range(nj):
            bu_ref[:, j * xc:(j + 1) * xc] = _dot(u[:, j * uc:(j + 1) * uc], bb_ref[j])

    return _call(name, body, (n_rows // tm,),
                 [pl.BlockSpec((tm, d), lambda i: (i, 0)), pl.BlockSpec((1, d), lambda i: (0, 0)),
                  pl.BlockSpec((d, hw), lambda i: (0, 0)), pl.BlockSpec((nj, uc, xc), lambda i: (0, 0, 0))],
                 [pl.BlockSpec((tm, hw), lambda i: (i, 0)), pl.BlockSpec((tm, nj * xc), lambda i: (i, 0))],
                 [_sds((n_rows, hw), F32), _sds((n_rows, nj * xc), F32)])(h, g, w_in, bb)


def _cmul_add(xr, xi, ar, ai, sr, si):
    return xr + ar * sr - ai * si, xi + ar * si + ai * sr


def _scan_row_block(n_main_blocks, seq_blocks):
    return lambda b, i: jnp.where(i == 0, n_main_blocks + b, b * seq_blocks + i - 1)


def _scan_fwd(name, bu, tabs, n_ex, seq):
    n_rows, width = bu.shape
    nj = 4
    cw = width // nj
    half = cw // 2
    tq = META_BLOCK
    seq_blocks = seq // tq
    rb = _scan_row_block(n_ex * seq_blocks, seq_blocks)

    def body(bu_ref, tab_ref, x_ref, carry_ref):
        @pl.when(pl.program_id(1) == 0)
        def _():
            carry_ref[...] = jnp.zeros_like(carry_ref)

        for j in range(nj):
            re, im = slice(j * cw, j * cw + half), slice(j * cw + half, (j + 1) * cw)
            ch = slice(j * half, (j + 1) * half)

            def blk(k, c, re=re, im=im, ch=ch):
                t = [tab_ref[n * SUBLANES:(n + 1) * SUBLANES, ch] for n in range(8)]
                r0 = pl.multiple_of(k * SUBLANES, SUBLANES)
                xr = bu_ref[pl.ds(r0, SUBLANES), re]
                xi = bu_ref[pl.ds(r0, SUBLANES), im]
                for s, d in enumerate((1, 2, 4)):
                    xr, xi = _cmul_add(xr, xi, t[2 * s], t[2 * s + 1], pltpu.roll(xr, d, 0), pltpu.roll(xi, d, 0))
                xr, xi = _cmul_add(xr, xi, t[6], t[7], c[0], c[1])
                x_ref[pl.ds(r0, SUBLANES), re] = xr
                x_ref[pl.ds(r0, SUBLANES), im] = xi
                last = SUBLANES - 1
                return (jnp.broadcast_to(xr[last:last + 1, :], xr.shape), jnp.broadcast_to(xi[last:last + 1, :], xi.shape))

            c = lax.fori_loop(0, tq // SUBLANES, blk, (carry_ref[0, :, ch], carry_ref[1, :, ch]))
            carry_ref[0, :, ch] = c[0]
            carry_ref[1, :, ch] = c[1]

    return _call(name, body, (n_ex, seq_blocks + 1),
                 [pl.BlockSpec((tq, width), lambda b, i: (rb(b, i), 0)), pl.BlockSpec((8 * SUBLANES, nj * half), lambda b, i: (0, 0))],
                 pl.BlockSpec((tq, width), lambda b, i: (rb(b, i), 0)), _sds((n_rows, width), F32),
                 scratch=[pltpu.VMEM((2, SUBLANES, nj * half), F32)])(bu, tabs)


def _scan_bwd(name, gx, x, tabs, n_ex, seq, after=()):
    n_rows, width = gx.shape
    nj = 4
    cw = width // nj
    half = cw // 2
    tq = META_BLOCK
    seq_blocks = seq // tq
    n_steps = seq_blocks + 1
    rb = _scan_row_block(n_ex * seq_blocks, seq_blocks)
    rbr = lambda b, i: rb(b, n_steps - 1 - i)

    def body(gx_ref, x_ref, tab_ref, g_ref, da_ref, carry_ref):
        @pl.when(pl.program_id(1) == 0)
        def _():
            carry_ref[...] = jnp.zeros_like(carry_ref)
            da_ref[...] = jnp.zeros_like(da_ref)
        row = lax.broadcasted_iota(jnp.int32, (SUBLANES, half), 0)
        n_blk = tq // SUBLANES

        for j in range(nj):
            re, im = slice(j * cw, j * cw + half), slice(j * cw + half, (j + 1) * cw)
            ch = slice(j * half, (j + 1) * half)

            def blk(kk, st, re=re, im=im, ch=ch):
                t = [tab_ref[n * SUBLANES:(n + 1) * SUBLANES, ch] for n in range(8)]
                cr, ci, dar, dai = st
                r0 = pl.multiple_of((n_blk - 1 - kk) * SUBLANES, SUBLANES)
                gr = gx_ref[pl.ds(r0, SUBLANES), re]
                gi = gx_ref[pl.ds(r0, SUBLANES), im]
                for s, d in enumerate((1, 2, 4)):
                    gr, gi = _cmul_add(gr, gi, t[2 * s], t[2 * s + 1],
                                       pltpu.roll(gr, SUBLANES - d, 0), pltpu.roll(gi, SUBLANES - d, 0))
                gr, gi = _cmul_add(gr, gi, t[6], t[7], cr, ci)
                g_ref[pl.ds(r0, SUBLANES), re] = gr.astype(BF16)
                g_ref[pl.ds(r0, SUBLANES), im] = gi.astype(BF16)
                hr = jnp.where(row == SUBLANES - 1, cr, pltpu.roll(gr, SUBLANES - 1, 0))
                hi = jnp.where(row == SUBLANES - 1, ci, pltpu.roll(gi, SUBLANES - 1, 0))
                xr = x_ref[pl.ds(r0, SUBLANES), re]
                xi = x_ref[pl.ds(r0, SUBLANES), im]
                dar = dar + xr * hr + xi * hi
                dai = dai + xr * hi - xi * hr
                return (jnp.broadcast_to(gr[0:1, :], gr.shape), jnp.broadcast_to(gi[0:1, :], gi.shape), dar, dai)

            st = lax.fori_loop(0, n_blk, blk, (carry_ref[0, :, ch], carry_ref[1, :, ch], da_ref[0, :, re], da_ref[0, :, im]))
            carry_ref[0, :, ch] = st[0]
            carry_ref[1, :, ch] = st[1]
            da_ref[0, :, re] = st[2]
            da_ref[0, :, im] = st[3]

    return _call(name, body, (n_ex, n_steps),
                 [pl.BlockSpec((tq, width), lambda b, i: (rbr(b, i), 0)), pl.BlockSpec((tq, width), lambda b, i: (rbr(b, i), 0)),
                  pl.BlockSpec((8 * SUBLANES, nj * half), lambda b, i: (0, 0))],
                 [pl.BlockSpec((tq, width), lambda b, i: (rbr(b, i), 0)), pl.BlockSpec((1, SUBLANES, width), lambda b, i: (b, 0, 0))],
                 [_sds((n_rows, width), BF16), _sds((n_ex, SUBLANES, width), F32)],
                 scratch=[pltpu.VMEM((2, SUBLANES, nj * half), F32)], after=after)(gx, x, tabs)


def _ssm_z(gy, wout_ref, nj):
    return jnp.concatenate([_dot(gy, wout_ref[j]) for j in range(nj)], axis=1)


def _ssm_out(name, x, u, dskip, cb, wout4, h, n_rows):
    nj, xc, uc = cb.shape
    no, hw, oc = wout4.shape
    d = h.shape[1]
    tm = ROW_TILE

    def body(x_ref, u_ref, ds_ref, cb_ref, w_ref, h_ref, o_ref, y_ref):
        y = jnp.concatenate([_dot(x_ref[:, j * xc:(j + 1) * xc], cb_ref[j]) for j in range(nj)], axis=1)
        y = y + ds_ref[...] * u_ref[...]
        y_ref[...] = y
        z = _ssm_z(_gelu(y), w_ref, no)
        o_ref[...] = h_ref[...] + z[:, :d] * _sigmoid(z[:, d:])

    return _call(name, body, (n_rows // tm,),
                 [pl.BlockSpec((tm, nj * xc), lambda i: (i, 0)), pl.BlockSpec((tm, hw), lambda i: (i, 0)),
                  pl.BlockSpec((1, hw), lambda i: (0, 0)), pl.BlockSpec((nj, xc, uc), lambda i: (0, 0, 0)),
                  pl.BlockSpec((no, hw, oc), lambda i: (0, 0, 0)), pl.BlockSpec((tm, d), lambda i: (i, 0))],
                 [pl.BlockSpec((tm, d), lambda i: (i, 0)), pl.BlockSpec((tm, hw), lambda i: (i, 0))],
                 [_sds((n_rows, d), F32), _sds((n_rows, hw), F32)])(x, u, dskip, cb, wout4, h)


def _ssm_out_bwd(name, dh, y, u, cb, wout4, n_rows, after=()):
    nj, xc, uc = cb.shape
    no, hw, oc = wout4.shape
    d = dh.shape[1]
    tm = ROW_TILE

    def body(dh_ref, y_ref, u_ref, cb_ref, w_ref, dy_ref, dz_ref, gx_ref, dd_ref):
        y = y_ref[...]
        z = _ssm_z(_gelu(y), w_ref, no)
        za = z[:, :d]
        sg = _sigmoid(z[:, d:])
        dmix = dh_ref[...]
        dz = jnp.concatenate([dmix * sg, dmix * za * sg * (1.0 - sg)], axis=1).astype(BF16)
        dz_ref[...] = dz
        dgy = _dot_nt(dz[:, 0:oc], w_ref[0])
        for j in range(1, no):
            dgy = dgy + _dot_nt(dz[:, j * oc:(j + 1) * oc], w_ref[j])
        dy = dgy * _gelu_grad(y)
        dy_ref[...] = dy
        _acc_out(dd_ref, jnp.sum(dy * u_ref[...], axis=0, keepdims=True), pl.program_id(0) == 0)
        for j in range(nj):
            gx_ref[:, j * xc:(j + 1) * xc] = _dot_nt(dy[:, j * uc:(j + 1) * uc], cb_ref[j])

    return _call(name, body, (n_rows // tm,),
                 [pl.BlockSpec((tm, d), lambda i: (i, 0)), pl.BlockSpec((tm, hw), lambda i: (i, 0)),
                  pl.BlockSpec((tm, hw), lambda i: (i, 0)), pl.BlockSpec((nj, xc, uc), lambda i: (0, 0, 0)),
                  pl.BlockSpec((no, hw, oc), lambda i: (0, 0, 0))],
                 [pl.BlockSpec((tm, hw), lambda i: (i, 0)), pl.BlockSpec((tm, no * oc), lambda i: (i, 0)),
                  pl.BlockSpec((tm, nj * xc), lambda i: (i, 0)), pl.BlockSpec((1, hw), lambda i: (0, 0))],
                 [_sds((n_rows, hw), F32), _sds((n_rows, no * oc), BF16), _sds((n_rows, nj * xc), F32),
                  _sds((1, hw), F32)], after=after)(dh, y, u, cb, wout4)


def _ssm_in_bwd(name, gbu, dy, dskip, bb, w_in, h, g, dh, n_rows):
    nj, uc, xc = bb.shape
    d, hw = w_in.shape
    tm = ROW_TILE

    def body(gb_ref, dy_ref, ds_ref, bb_ref, w_ref, h_ref, g_ref, dh_ref, du_ref, o_ref, dg_ref):
        du = jnp.concatenate([_dot_nt(gb_ref[:, j * xc:(j + 1) * xc], bb_ref[j]) for j in range(nj)], axis=1)
        du = du + dy_ref[...] * ds_ref[...]
        du_ref[...] = du.astype(BF16)
        dhn, dg = _rms_bwd(h_ref[...], g_ref[...], _dot_nt(du, w_ref[...]))
        o_ref[...] = dh_ref[...] + dhn
        _acc_out(dg_ref, dg, pl.program_id(0) == 0)

    return _call(name, body, (n_rows // tm,),
                 [pl.BlockSpec((tm, nj * xc), lambda i: (i, 0)), pl.BlockSpec((tm, hw), lambda i: (i, 0)),
                  pl.BlockSpec((1, hw), lambda i: (0, 0)), pl.BlockSpec((nj, uc, xc), lambda i: (0, 0, 0)),
                  pl.BlockSpec((d, hw), lambda i: (0, 0)), pl.BlockSpec((tm, d), lambda i: (i, 0)),
                  pl.BlockSpec((1, d), lambda i: (0, 0)), pl.BlockSpec((tm, d), lambda i: (i, 0))],
                 [pl.BlockSpec((tm, hw), lambda i: (i, 0)), pl.BlockSpec((tm, d), lambda i: (i, 0)),
                  pl.BlockSpec((1, d), lambda i: (0, 0))],
                 [_sds((n_rows, hw), BF16), _sds((n_rows, d), F32), _sds((1, d), F32)])(gbu, dy, dskip, bb, w_in, h, g, dh)


def _discretize(lam_re, lam_im, log_step, b_re, b_im):
    step = jnp.exp(log_step)[:, None]
    mag = jnp.exp(lam_re * step)
    ar = mag * jnp.cos(lam_im * step)
    ai = mag * jnp.sin(lam_im * step)
    den = lam_re * lam_re + lam_im * lam_im
    nr, ni = ar - 1.0, ai
    cr = (nr * lam_re + ni * lam_im) / den
    ci = (ni * lam_re - nr * lam_im) / den
    bbar_r = cr[..., None] * b_re - ci[..., None] * b_im
    bbar_i = cr[..., None] * b_im + ci[..., None] * b_re
    return ar, ai, bbar_r, bbar_i


def _ssm_mats(lam_re, lam_im, log_step, b_re, b_im, c_re, c_im):
    n_g, n_p, n_c = b_re.shape
    gpc = n_g // 4
    ar, ai, bbar_r, bbar_i = _discretize(lam_re, lam_im, log_step, b_re, b_im)
    eye = jnp.eye(gpc, dtype=F32)

    def in_map(bbar):
        return jnp.einsum('jgpc,gh->jgchp', bbar.reshape(4, gpc, n_p, n_c), eye).reshape(4, gpc * n_c, gpc * n_p)

    def out_map(c):
        return jnp.einsum('jgcp,gh->jgphc', c.reshape(4, gpc, n_c, n_p), eye).reshape(4, gpc * n_p, gpc * n_c)

    bb = jnp.concatenate([in_map(bbar_r), in_map(bbar_i)], axis=2)
    cb = jnp.concatenate([out_map(c_re), -out_map(c_im)], axis=1)
    return bb, cb, ar.reshape(-1), ai.reshape(-1)


def _chunked(v, half):
    return v.reshape(v.shape[:-1] + (4, half))


def _scan_tables(ar, ai, reverse):
    if reverse:
        ai = -ai
    pr, pi = [ar], [ai]
    for _ in range(SUBLANES - 1):
        pr, pi = pr + [pr[-1] * ar - pi[-1] * ai], pi + [pr[-1] * ai + pi[-1] * ar]
    row = jnp.arange(SUBLANES)[:, None]
    tabs = []
    for d in (1, 2, 4):
        keep = (row <= SUBLANES - 1 - d) if reverse else (row >= d)
        tabs += [jnp.where(keep, pr[d - 1][None, :], 0.0), jnp.where(keep, pi[d - 1][None, :], 0.0)]
    order = list(range(SUBLANES))[::-1] if reverse else list(range(SUBLANES))
    tabs += [jnp.stack([pr[k] for k in order]), jnp.stack([pi[k] for k in order])]
    return jnp.concatenate(tabs, axis=0)


def _kv_proj(name, h, g, w_kv, k_gain_t, cos2, sin2, n_rows, n_kv, hd):
    d, kvw = w_kv.shape
    kw = n_kv * hd
    tm = ROW_TILE

    sel, sel_t = _head_selectors(n_kv, hd)

    def body(h_ref, g_ref, w_ref, kg_ref, c_ref, s_ref, e_ref, et_ref, raw_ref, k_ref, v_ref):
        raw = _dot(_rms(h_ref[...], g_ref[...]), w_ref[...])
        raw_ref[...] = raw
        k_ref[...] = _head_prep(raw[:, :kw], kg_ref[...], c_ref[...], s_ref[...], e_ref[...], et_ref[...], hd).astype(BF16)
        v_ref[...] = raw[:, kw:].astype(BF16)

    return _call(name, body, (n_rows // tm,),
                 [pl.BlockSpec((tm, d), lambda i: (i, 0)), pl.BlockSpec((1, d), lambda i: (0, 0)),
                  pl.BlockSpec((d, kvw), lambda i: (0, 0)), pl.BlockSpec((1, kw), lambda i: (0, 0)),
                  pl.BlockSpec((tm, 2 * hd), lambda i: (i, 0)), pl.BlockSpec((tm, 2 * hd), lambda i: (i, 0)),
                  pl.BlockSpec(sel.shape, lambda i: (0, 0)), pl.BlockSpec(sel_t.shape, lambda i: (0, 0))],
                 [pl.BlockSpec((tm, kvw), lambda i: (i, 0)), pl.BlockSpec((tm, kw), lambda i: (i, 0)),
                  pl.BlockSpec((tm, kw), lambda i: (i, 0))],
                 [_sds((n_rows, kvw), F32), _sds((n_rows, kw), BF16), _sds((n_rows, kw), BF16)])(
                     h, g, w_kv, k_gain_t, cos2, sin2, sel, sel_t)


def _q_proj(name, h, g, w_q, q_gain_t, cos2, sin2, n_rows, n_q, hd):
    d, qw = w_q.shape
    tm = ROW_TILE

    sel, sel_t = _head_selectors(n_q, hd)

    def body(h_ref, g_ref, w_ref, qg_ref, c_ref, s_ref, e_ref, et_ref, raw_ref, q_ref):
        raw = _dot(_rms(h_ref[...], g_ref[...]), w_ref[...])
        raw_ref[...] = raw
        q_ref[...] = _head_prep(raw, qg_ref[...], c_ref[...], s_ref[...], e_ref[...], et_ref[...], hd).astype(BF16)

    return _call(name, body, (n_rows // tm,),
                 [pl.BlockSpec((tm, d), lambda i: (i, 0)), pl.BlockSpec((1, d), lambda i: (0, 0)),
                  pl.BlockSpec((d, qw), lambda i: (0, 0)), pl.BlockSpec((1, qw), lambda i: (0, 0)),
                  pl.BlockSpec((tm, 2 * hd), lambda i: (i, 0)), pl.BlockSpec((tm, 2 * hd), lambda i: (i, 0)),
                  pl.BlockSpec(sel.shape, lambda i: (0, 0)), pl.BlockSpec(sel_t.shape, lambda i: (0, 0))],
                 [pl.BlockSpec((tm, qw), lambda i: (i, 0)), pl.BlockSpec((tm, qw), lambda i: (i, 0))],
                 [_sds((n_rows, qw), F32), _sds((n_rows, qw), BF16)])(h, g, w_q, q_gain_t, cos2, sin2, sel, sel_t)


def _attn_specs(seq, n_ex, n_meta, kw):
    nb = seq // WINDOW
    meta_blk = lambda b: (n_ex * seq + META_BLOCK * b + META_BLOCK - n_meta) // n_meta
    return [pl.BlockSpec((WINDOW, kw), lambda b, n: (b * nb + jnp.maximum(n - 1, 0), 0)),
            pl.BlockSpec((WINDOW, kw), lambda b, n: (b * nb + n, 0)),
            pl.BlockSpec((n_meta, kw), lambda b, n: (meta_blk(b), 0))]


def _attn_mask(n, qpk, n_keys):
    rows = qpk * WINDOW
    qi = lax.broadcasted_iota(jnp.int32, (rows, n_keys), 0) & (WINDOW - 1)
    kj = lax.broadcasted_iota(jnp.int32, (rows, n_keys), 1)
    rel = qi + WINDOW - kj
    band = (rel >= 0) & (rel < WINDOW) & ((n > 0) | (kj >= WINDOW))
    return band | (kj >= 2 * WINDOW)


def _stack_heads(ref, h, qpk, hd, dtype=None):
    parts = [ref[:, (h * qpk + gq) * hd:(h * qpk + gq + 1) * hd] for gq in range(qpk)]
    out = jnp.concatenate(parts, axis=0)
    return out if dtype is None else out.astype(dtype)


def _col(tile, c):
    lane = lax.broadcasted_iota(jnp.int32, tile.shape, 1)
    return jnp.sum(jnp.where(lane == c, tile, 0.0), axis=-1, keepdims=True)


def _put_col(col, c, n):
    lane = lax.broadcasted_iota(jnp.int32, (col.shape[0], n), 1)
    return jnp.where(lane == c, col, 0.0)


def _stack_cols(tile, h, qpk):
    return jnp.concatenate([_col(tile, h * qpk + gq) for gq in range(qpk)], axis=0)


def _sink_col(sinks, h, qpk):
    return jnp.concatenate([jnp.broadcast_to(_col(sinks, h * qpk + gq), (WINDOW, 1)) for gq in range(qpk)], axis=0)


def _attn_fwd(name, q, k, v, sinks, n_ex, seq, n_meta, n_kv, qpk, hd):
    nb = seq // WINDOW
    n_q = n_kv * qpk
    kw = n_kv * hd
    qw = n_q * hd
    n_keys = 2 * WINDOW + n_meta
    scale = hd ** -0.5

    def body(q_ref, kp_ref, kc_ref, km_ref, vp_ref, vc_ref, vm_ref, sk_ref, o_ref, lse_ref):
        valid = _attn_mask(pl.program_id(1), qpk, n_keys)
        sinks_v = sk_ref[...]
        o_parts = []
        lse_all = jnp.zeros((WINDOW, n_q), F32)
        for h in range(n_kv):
            hs = slice(h * hd, (h + 1) * hd)
            kb = jnp.concatenate([kp_ref[:, hs], kc_ref[:, hs], km_ref[:, hs]], axis=0)
            vb = jnp.concatenate([vp_ref[:, hs], vc_ref[:, hs], vm_ref[:, hs]], axis=0)
            s = jnp.where(valid, _dot_nt(_stack_heads(q_ref, h, qpk, hd), kb) * scale, NEG_INF)
            skc = _sink_col(sinks_v, h, qpk)
            m = jnp.maximum(jnp.max(s, axis=-1, keepdims=True), skc)
            p = jnp.exp(s - m)
            den = jnp.sum(p, axis=-1, keepdims=True) + jnp.exp(skc - m)
            o = _dot(p, vb) / den
            lse = m + jnp.log(den)
            for gq in range(qpk):
                o_parts.append(o[gq * WINDOW:(gq + 1) * WINDOW])
                lse_all = lse_all + _put_col(lse[gq * WINDOW:(gq + 1) * WINDOW], h * qpk + gq, n_q)
        o_ref[...] = jnp.concatenate(o_parts, axis=1)
        lse_ref[...] = lse_all

    qspec = pl.BlockSpec((WINDOW, qw), lambda b, n: (b * nb + n, 0))
    return _call(name, body, (n_ex, nb),
                 [qspec] + _attn_specs(seq, n_ex, n_meta, kw) + _attn_specs(seq, n_ex, n_meta, kw)
                 + [pl.BlockSpec((1, n_q), lambda b, n: (0, 0))],
                 [qspec, pl.BlockSpec((WINDOW, n_q), lambda b, n: (b * nb + n, 0))],
                 [_sds((n_ex * seq, qw), F32), _sds((n_ex * seq, n_q), F32)])(q, k, k, k, v, v, v, sinks)


def _attn_bwd(name, q, k, v, sinks, o, lse, do, n_ex, seq, n_meta, n_kv, qpk, hd):
    nb = seq // WINDOW
    n_q = n_kv * qpk
    kw = n_kv * hd
    qw = n_q * hd
    n_keys = 2 * WINDOW + n_meta
    scale = hd ** -0.5

    def body(q_ref, kp_ref, kc_ref, km_ref, vp_ref, vc_ref, vm_ref, sk_ref, o_ref, lse_ref, do_ref,
             dq_ref, dk_ref, dv_ref, dkm_ref, dvm_ref, dsk_ref):
        n = pl.program_id(1)

        @pl.when(n == 0)
        def _():
            dk_ref[...] = jnp.zeros_like(dk_ref)
            dv_ref[...] = jnp.zeros_like(dv_ref)
            dkm_ref[...] = jnp.zeros_like(dkm_ref)
            dvm_ref[...] = jnp.zeros_like(dvm_ref)

        @pl.when((n == 0) & (pl.program_id(0) == 0))
        def _():
            dsk_ref[...] = jnp.zeros_like(dsk_ref)

        valid = _attn_mask(n, qpk, n_keys)
        sinks_v = sk_ref[...]
        lse_v = lse_ref[...]
        dq_parts, dk_parts, dv_parts = [], [], []
        dsk = jnp.zeros((1, n_q), F32)
        for h in range(n_kv):
            hs = slice(h * hd, (h + 1) * hd)
            kb = jnp.concatenate([kp_ref[:, hs], kc_ref[:, hs], km_ref[:, hs]], axis=0)
            vb = jnp.concatenate([vp_ref[:, hs], vc_ref[:, hs], vm_ref[:, hs]], axis=0)
            qs = _stack_heads(q_ref, h, qpk, hd)
            dos = _stack_heads(do_ref, h, qpk, hd)
            delta = jnp.sum(dos * _stack_heads(o_ref, h, qpk, hd), axis=-1, keepdims=True)
            lse_c = _stack_cols(lse_v, h, qpk)
            s = jnp.where(valid, _dot_nt(qs, kb) * scale, NEG_INF)
            p = jnp.exp(s - lse_c)
            ds = p * (_dot_nt(dos, vb) - delta)
            dqs = _dot(ds, kb) * scale
            dk_parts.append(_dot_tn(ds, qs) * scale)
            dv_parts.append(_dot_tn(p, dos))
            dsink = -jnp.exp(_sink_col(sinks_v, h, qpk) - lse_c) * delta
            for gq in range(qpk):
                dq_parts.append(dqs[gq * WINDOW:(gq + 1) * WINDOW])
                dsk = dsk + _put_col(jnp.sum(dsink[gq * WINDOW:(gq + 1) * WINDOW], axis=0, keepdims=True), h * qpk + gq, n_q)
        dq_ref[...] = jnp.concatenate(dq_parts, axis=1)
        dsk_ref[...] += dsk
        dkb = jnp.concatenate(dk_parts, axis=1)
        dvb = jnp.concatenate(dv_parts, axis=1)
        prev = pl.ds(pl.multiple_of(jnp.maximum(n - 1, 0) * WINDOW, WINDOW), WINDOW)
        cur = pl.ds(pl.multiple_of(n * WINDOW, WINDOW), WINDOW)
        dk_ref[prev, :] += dkb[0:WINDOW]
        dv_ref[prev, :] += dvb[0:WINDOW]
        dk_ref[cur, :] += dkb[WINDOW:2 * WINDOW]
        dv_ref[cur, :] += dvb[WINDOW:2 * WINDOW]
        dkm_ref[...] += dkb[2 * WINDOW:]
        dvm_ref[...] += dvb[2 * WINDOW:]

    qspec = pl.BlockSpec((WINDOW, qw), lambda b, n: (b * nb + n, 0))
    exspec = pl.BlockSpec((seq, kw), lambda b, n: (b, 0))
    mspec = pl.BlockSpec((n_meta, kw), lambda b, n: (b, 0))
    return _call(name, body, (n_ex, nb),
                 [qspec] + _attn_specs(seq, n_ex, n_meta, kw) + _attn_specs(seq, n_ex, n_meta, kw)
                 + [pl.BlockSpec((1, n_q), lambda b, n: (0, 0)), qspec,
                    pl.BlockSpec((WINDOW, n_q), lambda b, n: (b * nb + n, 0)), qspec],
                 [qspec, exspec, exspec, mspec, mspec, pl.BlockSpec((1, n_q), lambda b, n: (0, 0))],
                 [_sds((n_ex * seq, qw), F32), _sds((n_ex * seq, kw), F32), _sds((n_ex * seq, kw), F32),
                  _sds((n_ex * n_meta, kw), F32), _sds((n_ex * n_meta, kw), F32), _sds((1, n_q), F32)])(
                      q, k, k, k, v, v, v, sinks, o, lse, do)


def _attn_out(name, o, h, w_o, n_rows):
    qw, d = w_o.shape
    tm = ROW_TILE

    def body(o_ref, h_ref, w_ref, out_ref):
        out_ref[...] = h_ref[...] + _dot(o_ref[...], w_ref[...])

    return _call(name, body, (n_rows // tm,),
                 [pl.BlockSpec((tm, qw), lambda i: (i, 0)), pl.BlockSpec((tm, d), lambda i: (i, 0)),
                  pl.BlockSpec((qw, d), lambda i: (0, 0))],
                 pl.BlockSpec((tm, d), lambda i: (i, 0)), _sds((n_rows, d), F32))(o, h, w_o)


def _attn_out_bwd(name, dh, w_o, n_rows):
    qw, d = w_o.shape
    tm = ROW_TILE

    def body(dh_ref, w_ref, do_ref):
        do_ref[...] = _dot_nt(dh_ref[...], w_ref[...])

    return _call(name, body, (n_rows // tm,),
                 [pl.BlockSpec((tm, d), lambda i: (i, 0)), pl.BlockSpec((qw, d), lambda i: (0, 0))],
                 pl.BlockSpec((tm, qw), lambda i: (i, 0)), _sds((n_rows, qw), F32))(dh, w_o)


def _q_bwd(name, dq, qraw, q_gain_t, cos2, sin2, w_q, h, g, dh, n_rows, n_q, hd):
    d, qw = w_q.shape
    tm = ROW_TILE

    sel, sel_t = _head_selectors(n_q, hd)

    def body(dq_ref, raw_ref, qg_ref, c_ref, s_ref, e_ref, et_ref, w_ref, h_ref, g_ref, dh_ref, draw_ref, o_ref, dqg_ref, dg_ref):
        dx, dgain = _head_prep_bwd(raw_ref[...], qg_ref[...], c_ref[...], s_ref[...], e_ref[...], et_ref[...], dq_ref[...], hd)
        draw = dx.astype(BF16)
        draw_ref[...] = draw
        dhn, dg = _rms_bwd(h_ref[...], g_ref[...], _dot_nt(draw, w_ref[...]))
        o_ref[...] = dh_ref[...] + dhn
        first = pl.program_id(0) == 0
        _acc_out(dqg_ref, dgain, first)
        _acc_out(dg_ref, dg, first)

    row = lambda w: pl.BlockSpec((tm, w), lambda i: (i, 0))
    one = lambda w: pl.BlockSpec((1, w), lambda i: (0, 0))
    return _call(name, body, (n_rows // tm,),
                 [row(qw), row(qw), one(qw), row(2 * hd), row(2 * hd), pl.BlockSpec(sel.shape, lambda i: (0, 0)),
                  pl.BlockSpec(sel_t.shape, lambda i: (0, 0)), pl.BlockSpec((d, qw), lambda i: (0, 0)), row(d), one(d), row(d)],
                 [row(qw), row(d), one(qw), one(d)],
                 [_sds((n_rows, qw), BF16), _sds((n_rows, d), F32), _sds((1, qw), F32), _sds((1, d), F32)])(
                     dq, qraw, q_gain_t, cos2, sin2, sel, sel_t, w_q, h, g, dh)


def _kv_bwd(name, dk, dv, kvraw, k_gain_t, cos2, sin2, w_kv, h, g, dh_main, n_rows, n_main, n_kv, hd, after=()):
    d, kvw = w_kv.shape
    kw = n_kv * hd
    tm = ROW_TILE
    n_main_tiles = n_main // tm

    sel, sel_t = _head_selectors(n_kv, hd)

    def body(dk_ref, dv_ref, raw_ref, kg_ref, c_ref, s_ref, e_ref, et_ref, w_ref, h_ref, g_ref, dh_ref, draw_ref, o_ref, dkg_ref,
             dg_ref):
        i = pl.program_id(0)
        dx, dgain = _head_prep_bwd(raw_ref[:, :kw], kg_ref[...], c_ref[...], s_ref[...], e_ref[...], et_ref[...], dk_ref[...], hd)
        draw = jnp.concatenate([dx, dv_ref[...]], axis=1).astype(BF16)
        draw_ref[...] = draw
        dhn, dg = _rms_bwd(h_ref[...], g_ref[...], _dot_nt(draw, w_ref[...]))
        o_ref[...] = jnp.where(i < n_main_tiles, dh_ref[...], 0.0) + dhn
        _acc_out(dkg_ref, dgain, i == 0)
        _acc_out(dg_ref, dg, i == 0)

    row = lambda w: pl.BlockSpec((tm, w), lambda i: (i, 0))
    one = lambda w: pl.BlockSpec((1, w), lambda i: (0, 0))
    return _call(name, body, (n_rows // tm,),
                 [row(kw), row(kw), row(kvw), one(kw), row(2 * hd), row(2 * hd), pl.BlockSpec(sel.shape, lambda i: (0, 0)),
                  pl.BlockSpec(sel_t.shape, lambda i: (0, 0)), pl.BlockSpec((d, kvw), lambda i: (0, 0)), row(d),
                  one(d), pl.BlockSpec((tm, d), lambda i: (jnp.minimum(i, n_main_tiles - 1), 0))],
                 [row(kvw), row(d), one(kw), one(d)],
                 [_sds((n_rows, kvw), BF16), _sds((n_rows, d), F32), _sds((1, kw), F32), _sds((1, d), F32)], after=after)(
                     dk, dv, kvraw, k_gain_t, cos2, sin2, sel, sel_t, w_kv, h, g, dh_main)


def _tn_rms(name, h, g, b, n_rows, out_dtype=F32):
    d = h.shape[1]
    nb = b.shape[1]
    tk = _contract_tile(n_rows)
    return _tn(name, (h, g, b),
               [pl.BlockSpec((tk, d), lambda j, k: (k, 0)), pl.BlockSpec((1, d), lambda j, k: (0, 0)),
                pl.BlockSpec((tk, nb), lambda j, k: (k, 0))],
               lambda j, h_ref, g_ref, b_ref: (_rms(h_ref[...], g_ref[...]), b_ref[...]), 1, d, nb, n_rows, tk, out_dtype=out_dtype)


def _tn_plain(name, a, b, nj, a_cols, b_cols, n_rows, a_fn=None, out_dtype=F32):
    tk = _contract_tile(n_rows)
    fa = (lambda v: v) if a_fn is None else a_fn
    a_map = (lambda j, k: (k, j)) if a.shape[1] != a_cols else (lambda j, k: (k, 0))
    b_map = (lambda j, k: (k, j)) if b.shape[1] != b_cols else (lambda j, k: (k, 0))
    return _tn(name, (a, b), [pl.BlockSpec((tk, a_cols), a_map), pl.BlockSpec((tk, b_cols), b_map)],
               lambda j, a_ref, b_ref: (fa(a_ref[...]), b_ref[...]), nj, a_cols, b_cols, n_rows, tk, out_dtype=out_dtype)


def _loss_head(name, y, target, n_rows):
    d = y.shape[1]
    tm = ROW_TILE

    def body(y_ref, t_ref, dy_ref, l_ref):
        e = y_ref[...] - t_ref[...]
        dy_ref[...] = e * (1.0 / d)
        e2 = jnp.sum((e * e).reshape(tm // SUBLANES, SUBLANES, d), axis=0)
        part = e2[:, 0:128]
        for k in range(1, d // 128):
            part = part + e2[:, k * 128:(k + 1) * 128]
        _acc_out(l_ref, part * (0.5 / d), pl.program_id(0) == 0)

    return _call(name, body, (n_rows // tm,),
                 [pl.BlockSpec((tm, d), lambda i: (i, 0)), pl.BlockSpec((tm, d), lambda i: (i, 0))],
                 [pl.BlockSpec((tm, d), lambda i: (i, 0)), pl.BlockSpec((SUBLANES, 128), lambda i: (0, 0))],
                 [_sds((n_rows, d), F32), _sds((SUBLANES, 128), F32)])(y, target)


def _cast_layer(name, a, layer):
    _, r, c = a.shape
    tr = _row_tile(r, 256)

    def body(a_ref, o_ref):
        o_ref[...] = a_ref[0].astype(BF16)

    return _call(name, body, (r // tr,), [pl.BlockSpec((1, tr, c), lambda i: (layer, i, 0))],
                 pl.BlockSpec((tr, c), lambda i: (i, 0)), _sds((r, c), BF16))(a)


def _adamw_math(w, g, m, v):
    c1 = 1.0 - ADAM_B1 ** ADAM_STEP
    c2 = 1.0 - ADAM_B2 ** ADAM_STEP
    nm = ADAM_B1 * m + (1.0 - ADAM_B1) * g
    nv = ADAM_B2 * v + (1.0 - ADAM_B2) * (g * g)
    return -ADAM_LR * ((nm / c1) / (jnp.sqrt(nv / c2) + ADAM_EPS) + ADAM_WD * w), nm, nv


def _adamw(name, w, g, m, v, after=()):
    rows, cols = w.shape
    tr = 128 if rows % 128 == 0 else rows

    def body(w_ref, g_ref, m_ref, v_ref, d_ref, nm_ref, nv_ref):
        d_ref[...], nm_ref[...], nv_ref[...] = _adamw_math(w_ref[...], g_ref[...], m_ref[...], v_ref[...])

    spec = pl.BlockSpec((tr, cols), lambda i: (i, 0))
    return _call(name, body, (rows // tr,), [spec] * 4, [spec] * 3, [_sds((rows, cols), F32)] * 3, after=after)(w, g, m, v)


def _position():
    return lax.axis_index("x"), lax.axis_index("y"), lax.axis_index("c")


def _other_chips(x, y):
    return [(1 - x, y), (x, 1 - y), (1 - x, 1 - y)]


def _peers_chips(x, y, c):
    return [(cx, cy, c) for cx, cy in _other_chips(x, y)]


def _peers_sibling(x, y, c):
    return [(x, y, 1 - c)]


def _peers_chips_and_sibling(x, y, c):
    return _peers_chips(x, y, c) + _peers_sibling(x, y, c)


def _comm_call(name, body, n_in, out_shape, scratch, sequencer=None):
    if sequencer is None:
        return pl.pallas_call(
            body, name=name, in_specs=[_HBM] * n_in, out_specs=[_HBM] * len(out_shape), out_shape=out_shape,
            scratch_shapes=list(scratch),
            compiler_params=pltpu.CompilerParams(has_side_effects=True, vmem_limit_bytes=V7X_VMEM_LIMIT))
    collective_id, peers = sequencer

    def seq_body(*refs):
        barrier = pltpu.get_barrier_semaphore()
        plist = peers(*_position())
        for peer in plist:
            pl.semaphore_signal(barrier, inc=1, device_id=peer, device_id_type=MESH)
        pl.semaphore_wait(barrier, len(plist))
        body(*refs)

    return pl.kernel(seq_body, out_type=out_shape, mesh=plsc.ScalarSubcoreMesh(axis_name="sequencer", num_cores=1), name=name,
                     scratch_types=list(scratch), compiler_params=pltpu.CompilerParams(collective_id=collective_id))


def _n_chunks(rows, want, dtype):
    align = 16 if dtype == BF16 else 8
    n = want
    while n > 1 and (rows % n or (rows // n) % align):
        n -= 1
    return n


def _remote(src, dst, send_sem, recv_sem, device):
    return pltpu.make_async_remote_copy(src_ref=src, dst_ref=dst, send_sem=send_sem, recv_sem=recv_sem,
                                        device_id=device, device_id_type=MESH)


def _start_in_chunks(src, dst, send_sem, recv_sem, device, want=8):
    rows = src.shape[0]
    n = _n_chunks(rows, want, src.dtype)
    for i in range(n):
        part = pl.ds(i * (rows // n), rows // n)
        _remote(src.at[part], dst.at[part], send_sem, recv_sem, device).start()


def _all_gather_chips(name, shards, split, collective_id=None):
    n = len(shards)

    def body(*refs):
        ins, outs = refs[:n], refs[n:2 * n]
        send_sems, recv_sems, local_sems = refs[2 * n:]
        x, y, c = _position()
        me = 2 * x + y
        chips = _other_chips(x, y)
        sibling = (x, y, 1 - c)
        sends, forwards = [], []
        for t in range(n):
            pltpu.make_async_copy(ins[t], outs[t].at[me], local_sems.at[t]).start()
        for t in range(n):
            r = ins[t].shape[0]
            rows = pl.ds(c * (r // 2), r // 2) if split[t] else pl.ds(0, r)
            for k, (cx, cy) in enumerate(chips):
                src, dst = ins[t].at[rows], outs[t].at[me, rows]
                _start_in_chunks(src, dst, send_sems.at[t, k], recv_sems.at[t, k], (cx, cy, c), want=4)
                sends.append(_remote(src, dst, send_sems.at[t, k], recv_sems.at[t, k], (cx, cy, c)))
        for t in range(n):
            r = ins[t].shape[0]
            rows = pl.ds(c * (r // 2), r // 2) if split[t] else pl.ds(0, r)
            for k, (cx, cy) in enumerate(chips):
                landed = outs[t].at[2 * cx + cy, rows]
                _remote(landed, landed, send_sems.at[t, k], recv_sems.at[t, k], (cx, cy, c)).wait_recv()
                if split[t]:
                    _start_in_chunks(landed, landed, send_sems.at[t, 3 + k], recv_sems.at[t, 3 + k], sibling, want=4)
                    forwards.append(_remote(landed, landed, send_sems.at[t, 3 + k], recv_sems.at[t, 3 + k], sibling))
        for t in range(n):
            if split[t]:
                r = ins[t].shape[0]
                other = pl.ds((1 - c) * (r // 2), r // 2)
                for k, (cx, cy) in enumerate(chips):
                    landed = outs[t].at[2 * cx + cy, other]
                    pltpu.make_async_remote_copy(
                        src_ref=landed, dst_ref=landed, send_sem=send_sems.at[t, 3 + k], recv_sem=recv_sems.at[t, 3 + k],
                        device_id=sibling, device_id_type=MESH).wait_recv()
        for cp in sends + forwards:
            cp.wait_send()
        for t in range(n):
            pltpu.make_async_copy(ins[t], outs[t].at[me], local_sems.at[t]).wait()

    out_shape = [_sds((N_CHIPS,) + s.shape, s.dtype) for s in shards]
    sequencer = None if collective_id is None else (collective_id, _peers_chips_and_sibling)
    return _comm_call(name, body, n, out_shape,
                      [pltpu.SemaphoreType.DMA((n, 6)), pltpu.SemaphoreType.DMA((n, 6)), pltpu.SemaphoreType.DMA((n,))],
                      sequencer)(*shards)


def _swap_halves_with_sibling(name, blob, collective_id=None):
    def body(b_ref, theirs_ref, send_sem, recv_sem):
        x, y, c = _position()
        sibling = (x, y, 1 - c)
        for k in range(b_ref.shape[1]):
            _start_in_chunks(b_ref.at[1 - c, k], theirs_ref.at[k], send_sem, recv_sem, sibling)
        _remote(b_ref.at[1 - c], theirs_ref, send_sem, recv_sem, sibling).wait()

    return _comm_call(name, body, 1, [_sds(blob.shape[1:], blob.dtype)],
                      [pltpu.SemaphoreType.DMA(()), pltpu.SemaphoreType.DMA(())],
                      None if collective_id is None else (collective_id, _peers_sibling))(blob)[0]


def _scatter_to_chips(name, parts, collective_id=None):
    def body(p_ref, o_ref, send_sems, recv_sems, local_sems):
        x, y, c = _position()
        me = 2 * x + y
        rows = p_ref.shape[1]
        n_loc = _n_chunks(rows, 16, p_ref.dtype)
        locs = [pltpu.make_async_copy(p_ref.at[me, pl.ds(i * (rows // n_loc), rows // n_loc)],
                                      o_ref.at[me, pl.ds(i * (rows // n_loc), rows // n_loc)], local_sems.at[i])
                for i in range(n_loc)]
        for loc in locs:
            loc.start()
        sends = []
        for k, (cx, cy) in enumerate(_other_chips(x, y)):
            src, dst = p_ref.at[2 * cx + cy], o_ref.at[me]
            _start_in_chunks(src, dst, send_sems.at[k], recv_sems.at[k], (cx, cy, c))
            sends.append(_remote(src, dst, send_sems.at[k], recv_sems.at[k], (cx, cy, c)))
        for k, (cx, cy) in enumerate(_other_chips(x, y)):
            landed = o_ref.at[2 * cx + cy]
            _remote(landed, landed, send_sems.at[k], recv_sems.at[k], (cx, cy, c)).wait_recv()
        for cp in sends:
            cp.wait_send()
        for loc in locs:
            loc.wait()

    def local_sems_shape(rows):
        return pltpu.SemaphoreType.DMA((_n_chunks(rows, 16, parts.dtype),))

    return _comm_call(name, body, 1, [_sds(parts.shape, parts.dtype)],
                      [pltpu.SemaphoreType.DMA((3,)), pltpu.SemaphoreType.DMA((3,)), local_sems_shape(parts.shape[1])],
                      None if collective_id is None else (collective_id, _peers_chips))(parts)[0]


def _share_with_sibling(name, mine, collective_id=None):
    def body(m_ref, o_ref, send_sem, recv_sem):
        x, y, c = _position()
        sibling = (x, y, 1 - c)
        _start_in_chunks(m_ref, o_ref, send_sem, recv_sem, sibling, want=16)
        _remote(m_ref, o_ref, send_sem, recv_sem, sibling).wait()

    return _comm_call(name, body, 1, [_sds(mine.shape, mine.dtype)],
                      [pltpu.SemaphoreType.DMA(()), pltpu.SemaphoreType.DMA(())],
                      None if collective_id is None else (collective_id, _peers_sibling))(mine)[0]


def _row_tile(rows, cap=640):
    best = rows
    for t in range(16, min(rows, cap) + 1, 16):
        if rows % t == 0:
            best = t
    return best


_ANY = pl.BlockSpec(memory_space=pl.ANY)


def _add_my_half(name, blob, theirs, half_index, out_dtype, after):
    n, rows, cols = theirs.shape
    tr = _row_tile(rows)

    def body(c_ref, a_ref, b_ref, after_ref, o_ref):
        o_ref[...] = (a_ref[0].astype(F32) + b_ref[...].astype(F32)).astype(out_dtype)

    spec = pl.BlockSpec((1, tr, cols), lambda k, i, c: (k, i, 0))
    grid_spec = pltpu.PrefetchScalarGridSpec(
        num_scalar_prefetch=1, grid=(n, rows // tr),
        in_specs=[pl.BlockSpec((1, 1, tr, cols), lambda k, i, c: (c[0], k, i, 0)), spec, _ANY], out_specs=spec)
    return pl.pallas_call(
        body, name=name, grid_spec=grid_spec, out_shape=_sds(theirs.shape, out_dtype),
        compiler_params=pltpu.CompilerParams(dimension_semantics=("arbitrary", "arbitrary"),
                                             vmem_limit_bytes=V7X_VMEM_LIMIT))(half_index, blob, theirs, after)


def _sum_slots(name, parts, after):
    n, rows, cols = parts.shape
    tr = _row_tile(rows)

    def body(p_ref, o_ref):
        acc = p_ref[0].astype(F32)
        for k in range(1, n):
            acc = acc + p_ref[k].astype(F32)
        o_ref[...] = acc

    return _call(name, body, (rows // tr,), [pl.BlockSpec((n, tr, cols), lambda i: (0, i, 0))],
                 pl.BlockSpec((tr, cols), lambda i: (i, 0)), _sds((rows, cols), F32), after=(after,))(parts)


def _reduce_small_adamw(name, grads, loss_tile, ws, ms, vs, after=()):
    n = len(grads)
    srcs = list(grads) + [loss_tile]
    after = tuple(after)

    def body(*refs):
        refs = refs[:4 * n + 1] + refs[4 * n + 1 + len(after):]
        g_in, w_in, m_in, v_in = refs[:n + 1], refs[n + 1:2 * n + 1], refs[2 * n + 1:3 * n + 1], refs[3 * n + 1:4 * n + 1]
        outs = refs[4 * n + 1:8 * n + 2]
        g_out, d_out, nm_out, nv_out, loss_out = outs[:n], outs[n:2 * n], outs[2 * n:3 * n], outs[3 * n:4 * n], outs[4 * n]
        bufs = refs[8 * n + 2:9 * n + 3]
        send_sems, recv_sems = refs[9 * n + 3:]
        x, y, c = _position()
        me = 4 * x + 2 * y + c
        chip = 2 * x + y
        peers = [(1 - x if dlt & 4 else x, 1 - y if dlt & 2 else y, 1 - c if dlt & 1 else c) for dlt in range(1, N_DEV)]
        sends = []
        for t in range(n + 1):
            bufs[t][me] = g_in[t][...]
            for k, peer in enumerate(peers):
                cp = _remote(g_in[t], bufs[t].at[me], send_sems.at[t, k], recv_sems.at[t, k], peer)
                cp.start()
                sends.append(cp)
        for t in range(n + 1):
            for k, (tx, ty, tc) in enumerate(peers):
                landed = bufs[t].at[4 * tx + 2 * ty + tc]
                _remote(landed, landed, send_sems.at[t, k], recv_sems.at[t, k], (tx, ty, tc)).wait_recv()
        for cp in sends:
            cp.wait_send()
        for t in range(n + 1):
            total = bufs[t][0]
            for k in range(1, N_DEV):
                total = total + bufs[t][k]
            if t == n:
                loss_out[...] = total
                continue
            cols = w_in[t].shape[1]
            if cols == total.shape[1]:
                g_out[t][...] = total
                d_out[t][...], nm_out[t][...], nv_out[t][...] = _adamw_math(w_in[t][...], total, m_in[t][...], v_in[t][...])
            else:
                for j in range(N_CHIPS):
                    @pl.when(chip == j)
                    def _(t=t, j=j, cols=cols, total=total):
                        mine = total[:, j * cols:(j + 1) * cols]
                        g_out[t][...] = mine
                        d_out[t][...], nm_out[t][...], nv_out[t][...] = _adamw_math(w_in[t][...], mine, m_in[t][...], v_in[t][...])

    w_shapes = [_sds(a.shape, F32) for a in ws]
    return pl.pallas_call(
        body, name=name, in_specs=[_VMEM] * (4 * n + 1) + [_ANY] * len(after), out_specs=[_VMEM] * (4 * n + 1),
        out_shape=w_shapes * 4 + [_sds(loss_tile.shape, F32)],
        scratch_shapes=[pltpu.VMEM((N_DEV,) + a.shape, F32) for a in srcs]
        + [pltpu.SemaphoreType.DMA((n + 1, N_DEV - 1)), pltpu.SemaphoreType.DMA((n + 1, N_DEV - 1))],
        compiler_params=pltpu.CompilerParams(has_side_effects=True, vmem_limit_bytes=V7X_VMEM_LIMIT))(
            *srcs, *ws, *ms, *vs, *after)


_BIG = ("ffn1_w_gate_up", "ffn1_w_down", "ffn2_w_gate_up", "ffn2_w_down", "ssm_w_in", "ssm_w_out", "w_kv", "attn_w_q", "attn_w_o")
_TRANSPOSED = ("ffn1_w_gate_up", "ffn2_w_gate_up")
_SMALL = ("meta_tokens", "ffn1_norm", "mix_norm", "ffn2_norm", "ssm_lambda_re", "ssm_lambda_im", "ssm_b_re", "ssm_b_im",
          "ssm_c_re", "ssm_c_im", "ssm_log_step", "ssm_d", "kv_norm", "k_norm", "q_norm", "attn_sinks")
_ORDER = ("meta_tokens", "ffn1_norm", "ffn1_w_gate_up", "ffn1_w_down", "mix_norm", "ffn2_norm", "ffn2_w_gate_up", "ffn2_w_down",
          "ssm_w_in", "ssm_lambda_re", "ssm_lambda_im", "ssm_b_re", "ssm_b_im", "ssm_c_re", "ssm_c_im", "ssm_log_step", "ssm_d",
          "ssm_w_out", "kv_norm", "w_kv", "k_norm", "attn_w_q", "q_norm", "attn_sinks", "attn_w_o")


def _step(x, target, w, m, v):
    n_ex, seq, d = x.shape
    n_meta = w["meta_tokens"].shape[0]
    n_main = n_ex * seq
    n_all = n_main + n_ex * META_BLOCK
    n_g, n_p, n_c = w["ssm_b_re"].shape[1:]
    hd = w["k_norm"].shape[0]
    n_kv = w["w_kv"].shape[1] // (2 * hd)
    n_q = w["attn_w_q"].shape[2] // hd
    qpk = n_q // n_kv
    px, py, pc = _position()
    chip = 2 * px + py

    def cast(name, layer=0):
        a = w[name]
        return _cast_layer(f"cast_{name}_{layer}", a if a.ndim == 3 else a[None], layer)

    first = [cast("ffn1_w_gate_up"), cast("ffn1_w_down"), cast("ssm_w_in"), cast("ssm_w_out"), w["meta_tokens"], w["ssm_d"]]
    g_a = _all_gather_chips("gather_first", first, [True, True, True, True, False, False], collective_id=12)
    second = [cast("ffn2_w_gate_up"), cast("ffn2_w_down"), cast("w_kv")]
    g_b = _all_gather_chips("gather_second", second, [True] * 3, collective_id=1)
    third = [cast("ffn1_w_gate_up", 1), cast("ffn1_w_down", 1), cast("attn_w_q"), cast("attn_w_o"),
             cast("ffn2_w_gate_up", 1), cast("ffn2_w_down", 1)]
    g_c = _all_gather_chips("gather_third", third, [True] * 6, collective_id=2)
    wgu = {("ffn1", 0): g_a[0], ("ffn1", 1): g_c[0], ("ffn2", 0): g_b[0], ("ffn2", 1): g_c[4]}
    wd = {("ffn1", 0): g_a[1], ("ffn1", 1): g_c[1], ("ffn2", 0): g_b[1], ("ffn2", 1): g_c[5]}
    wd = {key: a.reshape(-1, d) for key, a in wd.items()}
    w_in = g_a[2].reshape(d, -1)
    wout4 = g_a[3]
    w_q = g_c[2].reshape(d, -1)
    w_o = g_c[3].reshape(-1, d)
    w_kv = g_b[2].reshape(d, -1)
    meta_full = jnp.transpose(g_a[4], (1, 0, 2)).reshape(n_meta, d)
    dskip = g_a[5].reshape(1, -1)

    row1 = lambda a: a.reshape(1, -1)
    ssm_args = tuple(w[k][0] for k in ("ssm_lambda_re", "ssm_lambda_im", "ssm_log_step", "ssm_b_re", "ssm_b_im", "ssm_c_re", "ssm_c_im"))
    (bb, cb, a_re, a_im), ssm_vjp = jax.vjp(_ssm_mats, *ssm_args)
    bb16, cb16 = bb.astype(BF16), cb.astype(BF16)
    a_re_s, a_im_s = lax.stop_gradient(a_re), lax.stop_gradient(a_im)
    half = n_g * n_p // 4
    tabs_f = _scan_tables(a_re_s, a_im_s, False)
    tabs_b = _scan_tables(a_re_s, a_im_s, True)

    freqs = ROPE_THETA ** (-jnp.arange(0, hd // 2, dtype=F32) * 2.0 / hd)
    pos_main = jnp.tile(n_meta + jnp.arange(seq), n_ex)
    pos_meta = jnp.tile(jnp.maximum(jnp.arange(META_BLOCK) - (META_BLOCK - n_meta), 0), n_ex)
    ang = jnp.concatenate([pos_main, pos_meta]).astype(F32)[:, None] * freqs[None, :]
    cos = jnp.concatenate([jnp.cos(ang), jnp.cos(ang)] * 2, axis=1)
    sin_s = jnp.concatenate([-jnp.sin(ang), jnp.sin(ang)] * 2, axis=1)
    k_gain_t = jnp.tile(row1(w["k_norm"]), (1, n_kv))
    q_gain_t = jnp.tile(row1(w["q_norm"][0]), (1, n_q))

    meta_block = jnp.concatenate([jnp.zeros((META_BLOCK - n_meta, d), F32), meta_full], axis=0)
    h0 = jnp.concatenate([x.reshape(n_main, d)] + [meta_block] * n_ex, axis=0)

    g = lambda name, layer: row1(w[name][layer])
    h1, gu1 = _ffn_fwd("l0_ffn1", h0, g("ffn1_norm", 0), wgu["ffn1", 0], wd["ffn1", 0], n_all)
    u, bu = _ssm_in("ssm_in", h1, g("mix_norm", 0), w_in, bb16, n_all)
    xs = _scan_fwd("ssm_scan", bu, tabs_f, n_ex, seq)
    h2, y = _ssm_out("ssm_out", xs, u, dskip, cb16, wout4, h1, n_all)
    h3, gu2 = _ffn_fwd("l0_ffn2", h2, g("ffn2_norm", 0), wgu["ffn2", 0], wd["ffn2", 0], n_all)
    kvraw, k, vv = _kv_proj("kv_proj", h3, row1(w["kv_norm"]), w_kv, k_gain_t, cos, sin_s, n_all, n_kv, hd)
    h4, gu3 = _ffn_fwd("l1_ffn1", h3, g("ffn1_norm", 1), wgu["ffn1", 1], wd["ffn1", 1], n_main)
    qraw, q = _q_proj("q_proj", h4, g("mix_norm", 1), w_q, q_gain_t, cos, sin_s, n_main, n_q, hd)
    sinks = row1(w["attn_sinks"][0])
    o, lse = _attn_fwd("attn_fwd", q, k, vv, sinks, n_ex, seq, n_meta, n_kv, qpk, hd)
    h5 = _attn_out("attn_out", o, h4, w_o, n_main)
    h6, gu4 = _ffn_fwd("l1_ffn2", h5, g("ffn2_norm", 1), wgu["ffn2", 1], wd["ffn2", 1], n_main)
    dh6, loss_tile = _loss_head("loss_head", h6, target.reshape(n_main, d), n_main)

    lanes = 1024

    def rs_start(tag, entries, ids):
        pieces = [gr.reshape(N_CHIPS, 2, -1, lanes) for _, _, gr in entries]
        blob = jnp.transpose(jnp.concatenate(pieces, axis=2), (1, 0, 2, 3)).astype(BF16)
        return dict(tag=tag, entries=entries, ids=ids, blob=blob, theirs=_swap_halves_with_sibling(tag + "_swap", blob, ids[0]))

    def rs_scatter(st, after):
        chip_sum = _add_my_half(st["tag"] + "_chip_sum", st["blob"], st["theirs"], jnp.reshape(pc, (1,)).astype(jnp.int32), BF16, after)
        st["chip_sum"] = chip_sum
        st["landed"] = _scatter_to_chips(st["tag"] + "_scatter", chip_sum, st["ids"][1])

    def rs_finish(st, after):
        total = _sum_slots(st["tag"] + "_sum", st["landed"], after)
        st["total"] = total
        other = _share_with_sibling(st["tag"] + "_share", total, st["ids"][2])
        halves = (jnp.where(pc == 0, total, other), jnp.where(pc == 0, other, total))
        out, off = {}, 0
        for name, layer, gr in st["entries"]:
            rows = gr.shape[1] * gr.shape[2] // lanes // 2
            flat = jnp.concatenate([hv[off:off + rows].reshape(-1) for hv in halves])
            if name in _TRANSPOSED:
                flat = flat.reshape(gr.shape[1], gr.shape[2]).T.reshape(-1)
            out[name, layer] = flat
            off += rows
        return out

    small = {}
    dh5, dg_f2l1, dwgu_f2l1, dwd_f2l1 = _ffn_bwd("l1_ffn2", dh6, h5, g("ffn2_norm", 1), gu4, wgu["ffn2", 1], wd["ffn2", 1], n_main)
    do = _attn_out_bwd("attn_out_bwd", dh5, w_o, n_main)
    dw_o = _tn_plain("attn_dwo", o, dh5, 1, o.shape[1], d, n_main, out_dtype=BF16).reshape(N_CHIPS, -1, d)
    dq, dk_main, dv_main, dk_meta, dv_meta, dsinks = _attn_bwd("attn_bwd", q, k, vv, sinks, o, lse, do, n_ex, seq, n_meta, n_kv, qpk, hd)
    dqraw, dh4, dq_gain, dg_mix1 = _q_bwd("q_bwd", dq, qraw, q_gain_t, cos, sin_s, w_q, h4, g("mix_norm", 1), dh5, n_main, n_q, hd)
    dw_q = _tn_rms("attn_dwq", h4, g("mix_norm", 1), dqraw, n_main, out_dtype=BF16).reshape(N_CHIPS, -1, dqraw.shape[1])
    dh3m, dg_f1l1, dwgu_f1l1, dwd_f1l1 = _ffn_bwd("l1_ffn1", dh4, h3, g("ffn1_norm", 1), gu3, wgu["ffn1", 1], wd["ffn1", 1], n_main)
    rs1 = rs_start("rs1", [("ffn2_w_gate_up", 1, dwgu_f2l1), ("ffn2_w_down", 1, dwd_f2l1), ("attn_w_o", 0, dw_o),
                           ("attn_w_q", 0, dw_q), ("ffn1_w_gate_up", 1, dwgu_f1l1), ("ffn1_w_down", 1, dwd_f1l1)], (3, 4, 5))

    def with_meta(main, meta):
        blocks = [jnp.pad(meta[b * n_meta:(b + 1) * n_meta], ((META_BLOCK - n_meta, 0), (0, 0))) for b in range(n_ex)]
        return jnp.concatenate([main] + blocks, axis=0)

    dkvraw, dh3, dk_gain, dg_kv = _kv_bwd("kv_bwd", with_meta(dk_main, dk_meta), with_meta(dv_main, dv_meta), kvraw, k_gain_t,
                                          cos, sin_s, w_kv, h3, row1(w["kv_norm"]), dh3m, n_all, n_main, n_kv, hd,
                                          after=(rs1["blob"],))
    rs_scatter(rs1, after=dh3)
    dw_kv = _tn_rms("kv_dw", h3, row1(w["kv_norm"]), dkvraw, n_all, out_dtype=BF16).reshape(N_CHIPS, -1, dkvraw.shape[1])
    dh2, dg_f2l0, dwgu_f2l0, dwd_f2l0 = _ffn_bwd("l0_ffn2", dh3, h2, g("ffn2_norm", 0), gu2, wgu["ffn2", 0], wd["ffn2", 0], n_all,
                                                 after=(rs1["chip_sum"],))
    reduced = rs_finish(rs1, after=dh2)
    rs0a = rs_start("rs0a", [("w_kv", 0, dw_kv), ("ffn2_w_gate_up", 0, dwgu_f2l0), ("ffn2_w_down", 0, dwd_f2l0)], (6, 7, 8))

    dy, dz, gx, dd = _ssm_out_bwd("ssm_out_bwd", dh2, y, u, cb16, wout4, n_all, after=(rs1["total"], rs0a["blob"]))
    rs_scatter(rs0a, after=dy)
    hw = y.shape[1]
    oc = wout4.shape[2]
    dw_out = _tn_plain("ssm_dwout", y, dz, wout4.shape[0], hw, oc, n_all, a_fn=_gelu, out_dtype=BF16)
    dcb = _tn_plain("ssm_dcb", xs, dy, 4, xs.shape[1] // 4, hw // 4, n_all)
    gbu, da = _scan_bwd("ssm_scan_bwd", gx, xs, tabs_b, n_ex, seq, after=(rs0a["chip_sum"],))
    du, dh1, dg_mix0 = _ssm_in_bwd("ssm_in_bwd", gbu, dy, dskip, bb16, w_in, h1, g("mix_norm", 0), dh2, n_all)
    reduced.update(rs_finish(rs0a, after=dh1))
    dbb = _tn_plain("ssm_dbb", u, gbu, 4, hw // 4, gbu.shape[1] // 4, n_all)
    dw_in = _tn_rms("ssm_dwin", h1, g("mix_norm", 0), du, n_all, out_dtype=BF16).reshape(N_CHIPS, -1, hw)
    rs0b = {}

    def start_last_group(dwgu, dwd):
        rs0b.update(rs_start("rs0b", [("ssm_w_out", 0, dw_out), ("ssm_w_in", 0, dw_in), ("ffn1_w_gate_up", 0, dwgu),
                                      ("ffn1_w_down", 0, dwd)], (9, 10, 11)))
        return (rs0b["blob"],)

    (dh0, dh0_meta), dg_f1l0, _, _ = _ffn_bwd("l0_ffn1", dh1, h0, g("ffn1_norm", 0), gu1, wgu["ffn1", 0], wd["ffn1", 0], n_all, n_main,
                                              after=(rs0a["total"],), before_dh=start_last_group)
    rs_scatter(rs0b, after=dh0)

    grad_x = dh0.reshape(n_ex, seq, d)
    da_sum = jnp.sum(da, axis=(0, 1)).reshape(4, 2, half)
    d_ssm = ssm_vjp((dbb, dcb, da_sum[:, 0].reshape(-1), da_sum[:, 1].reshape(-1)))
    for key, val in zip(("ssm_lambda_re", "ssm_lambda_im", "ssm_log_step", "ssm_b_re", "ssm_b_im", "ssm_c_re", "ssm_c_im"), d_ssm):
        small[key] = val[None]
    small["meta_tokens"] = sum(dh0_meta[META_BLOCK * (b + 1) - n_meta:META_BLOCK * (b + 1)] for b in range(n_ex))
    small["ffn1_norm"] = jnp.concatenate([dg_f1l0, dg_f1l1], axis=0)
    small["ffn2_norm"] = jnp.concatenate([dg_f2l0, dg_f2l1], axis=0)
    small["mix_norm"] = jnp.concatenate([dg_mix0, dg_mix1], axis=0)
    small["ssm_d"] = dd
    small["kv_norm"] = dg_kv.reshape(-1)
    small["k_norm"] = jnp.sum(dk_gain.reshape(n_kv, hd), axis=0)
    small["q_norm"] = jnp.sum(dq_gain.reshape(n_q, hd), axis=0, keepdims=True)
    small["attn_sinks"] = dsinks

    def view(name, a):
        if name in ("ssm_b_re", "ssm_b_im"):
            return a.reshape(-1, 128)
        return a.reshape(1, -1) if a.ndim == 1 else a.reshape(-1, a.shape[-1])

    grads, deltas, new_m, new_v = {}, {}, {}, {}

    def adamw_matrix(name, after=()):
        shape = w[name].shape
        layers = [reduced[name, layer] for layer in range(2) if (name, layer) in reduced]
        grads[name] = jnp.concatenate(layers).reshape(shape)
        two_d = lambda a: a.reshape(-1, shape[-1])
        dl, nm, nv = _adamw("adamw_" + name, two_d(w[name]), two_d(grads[name]), two_d(m[name]), two_d(v[name]), after=after)
        deltas[name], new_m[name], new_v[name] = dl.reshape(shape), nm.reshape(shape), nv.reshape(shape)
        return nv

    placed = (rs0b["chip_sum"],)
    for name in ("ffn2_w_down", "attn_w_o", "attn_w_q", "w_kv"):
        placed = (adamw_matrix(name, after=placed),)
    tail = _reduce_small_adamw("small_tail", [view(k, small[k]) for k in _SMALL], loss_tile,
                               *[[view(k, t[k]) for k in _SMALL] for t in (w, m, v)], after=placed)
    n_small = len(_SMALL)
    for i, k in enumerate(_SMALL):
        grads[k], deltas[k] = tail[i].reshape(w[k].shape), tail[n_small + i].reshape(w[k].shape)
        new_m[k], new_v[k] = tail[2 * n_small + i].reshape(w[k].shape), tail[3 * n_small + i].reshape(w[k].shape)
    loss = jnp.sum(tail[-1])
    reduced.update(rs_finish(rs0b, after=tail[-1]))
    adamw_matrix("ffn2_w_gate_up", after=(rs0b["total"],))
    for name in ("ffn1_w_gate_up", "ffn1_w_down", "ssm_w_in", "ssm_w_out"):
        adamw_matrix(name)
    return (loss, grad_x, *[grads[k] for k in _ORDER], *[deltas[k] for k in _ORDER], *[new_m[k] for k in _ORDER],
            *[new_v[k] for k in _ORDER])


def kernel(x, meta_tokens, ffn1_norm, ffn1_w_gate_up, ffn1_w_down, mix_norm, ffn2_norm, ffn2_w_gate_up, ffn2_w_down, ssm_w_in, ssm_lambda_re, ssm_lambda_im, ssm_b_re, ssm_b_im, ssm_c_re, ssm_c_im, ssm_log_step, ssm_d, ssm_w_out, kv_norm, w_kv, k_norm, attn_w_q, q_norm, attn_sinks, attn_w_o, loss_target, m_meta_tokens, m_ffn1_norm, m_ffn1_w_gate_up, m_ffn1_w_down, m_mix_norm, m_ffn2_norm, m_ffn2_w_gate_up, m_ffn2_w_down, m_ssm_w_in, m_ssm_lambda_re, m_ssm_lambda_im, m_ssm_b_re, m_ssm_b_im, m_ssm_c_re, m_ssm_c_im, m_ssm_log_step, m_ssm_d, m_ssm_w_out, m_kv_norm, m_w_kv, m_k_norm, m_attn_w_q, m_q_norm, m_attn_sinks, m_attn_w_o, v_meta_tokens, v_ffn1_norm, v_ffn1_w_gate_up, v_ffn1_w_down, v_mix_norm, v_ffn2_norm, v_ffn2_w_gate_up, v_ffn2_w_down, v_ssm_w_in, v_ssm_lambda_re, v_ssm_lambda_im, v_ssm_b_re, v_ssm_b_im, v_ssm_c_re, v_ssm_c_im, v_ssm_log_step, v_ssm_d, v_ssm_w_out, v_kv_norm, v_w_kv, v_k_norm, v_attn_w_q, v_q_norm, v_attn_sinks, v_attn_w_o):
    args = locals()
    w = {k: args[k] for k in _ORDER}
    m = {k: args["m_" + k] for k in _ORDER}
    v = {k: args["v_" + k] for k in _ORDER}
    return _step(x, loss_target, w, m, v)
```

```python
import functools
import math

import jax
import jax.numpy as jnp
from jax import lax
from jax.experimental import pallas as pl
from jax.experimental.pallas import tpu as pltpu
from jax.experimental.pallas import tpu_sc as plsc

F32 = jnp.float32
BF16 = jnp.bfloat16
MESH = pl.DeviceIdType.MESH

EPS = 1e-6
NEG_INF = -1e30
ROPE_THETA = 10000.0
WINDOW = 128
META_BLOCK = 128
ROW_TILE = 256
SUBLANES = 8
V7X_VMEM_LIMIT = 56 * 2**20
N_CHIPS = 4
N_DEV = 8

ADAM_LR = 0.001
ADAM_B1 = 0.9
ADAM_B2 = 0.999
ADAM_EPS = 1e-08
ADAM_WD = 0.01
ADAM_STEP = 10

_HBM = pl.BlockSpec(memory_space=pltpu.HBM)
_VMEM = pl.BlockSpec(memory_space=pltpu.VMEM)


def _call(name, body, grid, in_specs, out_specs, out_shape, scratch=(), after=()):
    after = tuple(after)
    n_in = len(in_specs)

    def wrapped(*refs):
        return body(*refs[:n_in], *refs[n_in + len(after):])

    call = pl.pallas_call(
        wrapped, name=name, grid=grid, in_specs=list(in_specs) + [pl.BlockSpec(memory_space=pl.ANY)] * len(after),
        out_specs=out_specs, out_shape=out_shape, scratch_shapes=list(scratch),
        compiler_params=pltpu.CompilerParams(dimension_semantics=("arbitrary",) * len(grid),
                                             vmem_limit_bytes=V7X_VMEM_LIMIT))
    return lambda *operands: call(*operands, *after)


def _sds(shape, dtype):
    return jax.ShapeDtypeStruct(tuple(shape), dtype)


def _dot(a, b):
    return jnp.dot(a.astype(BF16), b.astype(BF16), preferred_element_type=F32)


def _dot_nt(a, b):
    return lax.dot_general(a.astype(BF16), b.astype(BF16), (((1,), (1,)), ((), ())), preferred_element_type=F32)


def _dot_tn(a, b):
    return lax.dot_general(a.astype(BF16), b.astype(BF16), (((0,), (0,)), ((), ())), preferred_element_type=F32)


def _rms(h, g):
    return h * lax.rsqrt(jnp.mean(h * h, axis=-1, keepdims=True) + EPS) * g


def _rms_bwd(h, g, dn):
    r = lax.rsqrt(jnp.mean(h * h, axis=-1, keepdims=True) + EPS)
    xh = h * r
    dxh = dn * g
    dg = jnp.sum(dn * xh, axis=0, keepdims=True)
    dh = r * (dxh - xh * jnp.mean(dxh * xh, axis=-1, keepdims=True))
    return dh, dg


def _sigmoid(x):
    return 0.5 * jnp.tanh(0.5 * x) + 0.5


def _gelu(y):
    k = math.sqrt(2.0 / math.pi)
    return 0.5 * y * (1.0 + jnp.tanh(k * (y + 0.044715 * y * y * y)))


def _gelu_grad(y):
    k = math.sqrt(2.0 / math.pi)
    t = jnp.tanh(k * (y + 0.044715 * y * y * y))
    return 0.5 * (1.0 + t) + 0.5 * y * (1.0 - t * t) * k * (1.0 + 3.0 * 0.044715 * y * y)


def _partner(x, lane, d):
    width = x.shape[-1]
    return jnp.where((lane & d) == 0, pltpu.roll(x, width - d, 1), pltpu.roll(x, d, 1))


def _split_bf16(x):
    hi = x.astype(BF16)
    return hi, (x - hi.astype(F32)).astype(BF16)


def _head_sums(x, sel):
    hi, lo = _split_bf16(x)
    return jnp.dot(hi, sel, preferred_element_type=F32) + jnp.dot(lo, sel, preferred_element_type=F32)


def _head_expand(v, sel_t):
    hi, lo = _split_bf16(v)
    return jnp.dot(hi, sel_t, preferred_element_type=F32) + jnp.dot(lo, sel_t, preferred_element_type=F32)


def _tile_lanes(t, width):
    return jnp.concatenate([t] * (width // t.shape[-1]), axis=1)


def _head_prep(x, gain_t, cos2, sin2, sel, sel_t, hd):
    width = x.shape[-1]
    lane = lax.broadcasted_iota(jnp.int32, x.shape, 1)
    r = _head_expand(lax.rsqrt(_head_sums(x * x, sel) * (1.0 / hd) + EPS), sel_t)
    y = x * r * gain_t
    return y * _tile_lanes(cos2, width) + _partner(y, lane, hd // 2) * _tile_lanes(sin2, width)


def _head_prep_bwd(x, gain_t, cos2, sin2, sel, sel_t, d_out, hd):
    width = x.shape[-1]
    lane = lax.broadcasted_iota(jnp.int32, x.shape, 1)
    r = _head_expand(lax.rsqrt(_head_sums(x * x, sel) * (1.0 / hd) + EPS), sel_t)
    xhat = x * r
    dy = d_out * _tile_lanes(cos2, width) + _partner(d_out * _tile_lanes(sin2, width), lane, hd // 2)
    dgain = jnp.sum(dy * xhat, axis=0, keepdims=True)
    dxh = dy * gain_t
    mean = _head_expand(_head_sums(dxh * xhat, sel) * (1.0 / hd), sel_t)
    return r * (dxh - xhat * mean), dgain


def _head_selectors(n_heads, hd):
    sel = (jnp.arange(n_heads * hd)[:, None] // hd == jnp.arange(128)[None, :]).astype(BF16)
    return sel, sel.T


def _acc_out(ref, val, first):
    @pl.when(first)
    def _():
        ref[...] = jnp.zeros_like(ref)
    ref[...] += val


def _ffn_up(name, h, g, w4, n_rows):
    nj, d, fc = w4.shape
    tm = ROW_TILE

    def body(h_ref, g_ref, w_ref, o_ref, n_ref):
        n = _rms(h_ref[...], g_ref[...]).astype(BF16)
        n_ref[...] = n
        for j in range(nj):
            o_ref[:, j * fc:(j + 1) * fc] = _dot(n, w_ref[j]).astype(BF16)

    return _call(name, body, (n_rows // tm,),
                 [pl.BlockSpec((tm, d), lambda i: (i, 0)), pl.BlockSpec((1, d), lambda i: (0, 0)),
                  pl.BlockSpec((nj, d, fc), lambda i: (0, 0, 0))],
                 [pl.BlockSpec((tm, nj * fc), lambda i: (i, 0)), pl.BlockSpec((tm, d), lambda i: (i, 0))],
                 [_sds((n_rows, nj * fc), BF16), _sds((n_rows, d), BF16)])(h, g, w4)


def _ffn_down(name, gu, h, wd, n_rows, target=None):
    f, d = wd.shape
    tm = ROW_TILE

    def body(gu_ref, h_ref, w_ref, *rest):
        half_a = gu_ref[:, :f] * 0.5
        s = (half_a + half_a * jnp.tanh(half_a)) * gu_ref[:, f:]
        y = h_ref[...] + 0.5 * _dot(s, w_ref[...])
        if target is None:
            o_ref, s_ref = rest
            o_ref[...] = y
        else:
            t_ref, dy_ref, l_ref, s_ref = rest
            e = y - t_ref[...]
            dy_ref[...] = e * (1.0 / d)
            e2 = jnp.sum((e * e).reshape(tm // SUBLANES, SUBLANES, d), axis=0)
            part = e2[:, 0:128]
            for k in range(1, d // 128):
                part = part + e2[:, k * 128:(k + 1) * 128]
            _acc_out(l_ref, part * (0.5 / d), pl.program_id(0) == 0)
        s_ref[...] = s

    row = lambda width: pl.BlockSpec((tm, width), lambda i: (i, 0))
    in_specs = [row(2 * f), row(d), pl.BlockSpec((f, d), lambda i: (0, 0))]
    if target is None:
        return _call(name, body, (n_rows // tm,), in_specs, [row(d), row(f)],
                     [_sds((n_rows, d), F32), _sds((n_rows, f), BF16)])(gu, h, wd)
    return _call(name, body, (n_rows // tm,), in_specs + [row(d)],
                 [row(d), pl.BlockSpec((SUBLANES, 128), lambda i: (0, 0)), row(f)],
                 [_sds((n_rows, d), F32), _sds((SUBLANES, 128), F32), _sds((n_rows, f), BF16)])(gu, h, wd, target)


def _ffn_dgu(name, dh, gu, wd, n_rows, after=()):
    f, d = wd.shape
    tm = ROW_TILE

    def body(dh_ref, gu_ref, w_ref, o_ref):
        ds = _dot_nt(0.5 * dh_ref[...], w_ref[...]).astype(BF16)
        half_a = gu_ref[:, :f] * 0.5
        t = jnp.tanh(half_a)
        o_ref[:, :f] = ds * gu_ref[:, f:] * ((1.0 + t + half_a * (1.0 - t * t)) * 0.5)
        o_ref[:, f:] = ds * (half_a + half_a * t)

    return _call(name, body, (n_rows // tm,),
                 [pl.BlockSpec((tm, d), lambda i: (i, 0)), pl.BlockSpec((tm, 2 * f), lambda i: (i, 0)),
                  pl.BlockSpec((f, d), lambda i: (0, 0))],
                 pl.BlockSpec((tm, 2 * f), lambda i: (i, 0)), _sds((n_rows, 2 * f), BF16), after=after)(dh, gu, wd)


def _ffn_dh(name, dgu, h, g, dh, w4, n_rows, n_main=None, after=()):
    nj, d, fc = w4.shape
    tm = ROW_TILE
    n_first = (n_rows if n_main is None else n_main) // tm

    def body(dgu_ref, h_ref, g_ref, dh_ref, w_ref, o_ref, *rest):
        dg_ref = rest[-1]
        i = pl.program_id(0)
        dn = _dot_nt(dgu_ref[:, 0:fc], w_ref[0])
        for j in range(1, nj):
            dn = dn + _dot_nt(dgu_ref[:, j * fc:(j + 1) * fc], w_ref[j])
        dhn, dg = _rms_bwd(h_ref[...], g_ref[...], dn)
        val = dh_ref[...] + dhn
        if n_main is None:
            o_ref[...] = val
        else:
            @pl.when(i < n_first)
            def _():
                o_ref[...] = val

            @pl.when(i >= n_first)
            def _():
                rest[0][...] = val
        _acc_out(dg_ref, dg, i == 0)

    out_specs = [pl.BlockSpec((tm, d), lambda i: (jnp.minimum(i, n_first - 1), 0))]
    out_shape = [_sds((n_first * tm, d), F32)]
    if n_main is not None:
        out_specs.append(pl.BlockSpec((tm, d), lambda i: (jnp.maximum(i - n_first, 0), 0)))
        out_shape.append(_sds((n_rows - n_main, d), F32))
    return _call(name, body, (n_rows // tm,),
                 [pl.BlockSpec((tm, nj * fc), lambda i: (i, 0)), pl.BlockSpec((tm, d), lambda i: (i, 0)),
                  pl.BlockSpec((1, d), lambda i: (0, 0)), pl.BlockSpec((tm, d), lambda i: (i, 0)),
                  pl.BlockSpec((nj, d, fc), lambda i: (0, 0, 0))],
                 out_specs + [pl.BlockSpec((1, d), lambda i: (0, 0))],
                 out_shape + [_sds((1, d), F32)], after=after)(dgu, h, g, dh, w4)


def _contract_tile(n_rows, cap=1024):
    best = ROW_TILE
    for t in range(ROW_TILE, cap + 1, ROW_TILE):
        if n_rows % t == 0:
            best = t
    return best


def _tn(name, operands, in_specs, prologue, nj, ma, nb, n_rows, tk, out_dtype=F32, after=()):
    n_k = n_rows // tk
    out_spec = pl.BlockSpec((1, ma, nb), lambda j, k: (j, 0, 0))
    if out_dtype == F32:
        def body(*refs):
            o_ref = refs[-1]
            a, b = prologue(pl.program_id(0), *refs[:-1])
            _acc_out(o_ref, _dot_tn(a, b)[None], pl.program_id(1) == 0)

        return _call(name, body, (nj, n_k), in_specs, out_spec, _sds((nj, ma, nb), F32), after=after)(*operands)

    def body_rounded(*refs):
        o_ref, acc_ref = refs[-2:]
        a, b = prologue(pl.program_id(0), *refs[:-2])
        _acc_out(acc_ref, _dot_tn(a, b), pl.program_id(1) == 0)

        @pl.when(pl.program_id(1) == n_k - 1)
        def _():
            o_ref[0] = acc_ref[...].astype(out_dtype)

    return _call(name, body_rounded, (nj, n_k), in_specs, out_spec, _sds((nj, ma, nb), out_dtype),
                 scratch=[pltpu.VMEM((ma, nb), F32)], after=after)(*operands)


def _ffn_dwgu(name, n, dgu, nj, n_rows):
    d = n.shape[1]
    fc = dgu.shape[1] // nj
    tk = _contract_tile(n_rows)
    return _tn(name, (dgu, n),
               [pl.BlockSpec((tk, fc), lambda j, k: (k, j)), pl.BlockSpec((tk, d), lambda j, k: (k, 0))],
               lambda j, a_ref, b_ref: (a_ref[...], b_ref[...]), nj, fc, d, n_rows, tk, out_dtype=BF16)


def _ffn_dwd(name, s, dh, n_rows):
    f = s.shape[1]
    d = dh.shape[1]
    tk = _contract_tile(n_rows)
    return _tn(name, (s, dh),
               [pl.BlockSpec((tk, f), lambda j, k: (k, 0)), pl.BlockSpec((tk, d), lambda j, k: (k, 0))],
               lambda j, s_ref, dh_ref: (s_ref[...], 0.5 * dh_ref[...]), 1, f, d, n_rows, tk, out_dtype=BF16)


def _ffn_fwd(tag, h, g, w4, wd, n_rows, target=None):
    gu, n = _ffn_up(tag + "_up", h, g, w4, n_rows)
    *out, s = _ffn_down(tag + "_down", gu, h, wd, n_rows, target)
    return (out[0] if target is None else tuple(out)), (gu, n, s)


def _ffn_bwd(tag, dh_out, h, g, saved, w4, wd, n_rows, n_main=None, after=()):
    gu, n, s = saved
    nj = w4.shape[0]
    f, d = wd.shape
    dgu = _ffn_dgu(tag + "_dgu", dh_out, gu, wd, n_rows, after=after)
    dwd = _ffn_dwd(tag + "_dwd", s, dh_out, n_rows).reshape(N_CHIPS, f // N_CHIPS, d)
    *dh_parts, dg = _ffn_dh(tag + "_dh", dgu, h, g, dh_out, w4, n_rows, n_main)
    dwgu = _ffn_dwgu(tag + "_dwgu", n, dgu, nj, n_rows)
    dh_in = dh_parts[0] if n_main is None else tuple(dh_parts)
    return dh_in, dg, dwgu, dwd


def _ssm_in(name, h, g, w_in, bb, n_rows):
    d, hw = w_in.shape
    nj, uc, xc = bb.shape
    tm = ROW_TILE

    def body(h_ref, g_ref, w_ref, bb_ref, u_ref, bu_ref):
        u = _dot(_rms(h_ref[...], g_ref[...]), w_ref[...])
        u_ref[...] = u
        for j in range(nj):
            bu_ref[:, j * xc:(j + 1) * xc] = _dot(u[:, j * uc:(j + 1) * uc], bb_ref[j])

    return _call(name, body, (n_rows // tm,),
                 [pl.BlockSpec((tm, d), lambda i: (i, 0)), pl.BlockSpec((1, d), lambda i: (0, 0)),
                  pl.BlockSpec((d, hw), lambda i: (0, 0)), pl.BlockSpec((nj, uc, xc), lambda i: (0, 0, 0))],
                 [pl.BlockSpec((tm, hw), lambda i: (i, 0)), pl.BlockSpec((tm, nj * xc), lambda i: (i, 0))],
                 [_sds((n_rows, hw), F32), _sds((n_rows, nj * xc), F32)])(h, g, w_in, bb)


def _cmul_add(xr, xi, ar, ai, sr, si):
    return xr + ar * sr - ai * si, xi + ar * si + ai * sr


def _scan_row_block(n_main_blocks, seq_blocks):
    return lambda b, i: jnp.where(i == 0, n_main_blocks + b, b * seq_blocks + i - 1)


def _scan_fwd(name, bu, tabs, n_ex, seq):
    n_rows, width = bu.shape
    nj = 4
    cw = width // nj
    half = cw // 2
    tq = META_BLOCK
    seq_blocks = seq // tq
    rb = _scan_row_block(n_ex * seq_blocks, seq_blocks)

    def body(bu_ref, tab_ref, x_ref, carry_ref):
        @pl.when(pl.program_id(1) == 0)
        def _():
            carry_ref[...] = jnp.zeros_like(carry_ref)

        for j in range(nj):
            re, im = slice(j * cw, j * cw + half), slice(j * cw + half, (j + 1) * cw)
            ch = slice(j * half, (j + 1) * half)

            def blk(k, c, re=re, im=im, ch=ch):
                t = [tab_ref[n * SUBLANES:(n + 1) * SUBLANES, ch] for n in range(8)]
                r0 = pl.multiple_of(k * SUBLANES, SUBLANES)
                xr = bu_ref[pl.ds(r0, SUBLANES), re]
                xi = bu_ref[pl.ds(r0, SUBLANES), im]
                for s, d in enumerate((1, 2, 4)):
                    xr, xi = _cmul_add(xr, xi, t[2 * s], t[2 * s + 1], pltpu.roll(xr, d, 0), pltpu.roll(xi, d, 0))
                xr, xi = _cmul_add(xr, xi, t[6], t[7], c[0], c[1])
                x_ref[pl.ds(r0, SUBLANES), re] = xr
                x_ref[pl.ds(r0, SUBLANES), im] = xi
                last = SUBLANES - 1
                return (jnp.broadcast_to(xr[last:last + 1, :], xr.shape), jnp.broadcast_to(xi[last:last + 1, :], xi.shape))

            c = lax.fori_loop(0, tq // SUBLANES, blk, (carry_ref[0, :, ch], carry_ref[1, :, ch]))
            carry_ref[0, :, ch] = c[0]
            carry_ref[1, :, ch] = c[1]

    return _call(name, body, (n_ex, seq_blocks + 1),
                 [pl.BlockSpec((tq, width), lambda b, i: (rb(b, i), 0)), pl.BlockSpec((8 * SUBLANES, nj * half), lambda b, i: (0, 0))],
                 pl.BlockSpec((tq, width), lambda b, i: (rb(b, i), 0)), _sds((n_rows, width), F32),
                 scratch=[pltpu.VMEM((2, SUBLANES, nj * half), F32)])(bu, tabs)


def _scan_bwd(name, gx, x, tabs, n_ex, seq, after=()):
    n_rows, width = gx.shape
    nj = 4
    cw = width // nj
    half = cw // 2
    tq = META_BLOCK
    seq_blocks = seq // tq
    n_steps = seq_blocks + 1
    rb = _scan_row_block(n_ex * seq_blocks, seq_blocks)
    rbr = lambda b, i: rb(b, n_steps - 1 - i)

    def body(gx_ref, x_ref, tab_ref, g_ref, da_ref, carry_ref):
        @pl.when(pl.program_id(1) == 0)
        def _():
            carry_ref[...] = jnp.zeros_like(carry_ref)
            da_ref[...] = jnp.zeros_like(da_ref)
        row = lax.broadcasted_iota(jnp.int32, (SUBLANES, half), 0)
        n_blk = tq // SUBLANES

        for j in range(nj):
            re, im = slice(j * cw, j * cw + half), slice(j * cw + half, (j + 1) * cw)
            ch = slice(j * half, (j + 1) * half)

            def blk(kk, st, re=re, im=im, ch=ch):
                t = [tab_ref[n * SUBLANES:(n + 1) * SUBLANES, ch] for n in range(8)]
                cr, ci, dar, dai = st
                r0 = pl.multiple_of((n_blk - 1 - kk) * SUBLANES, SUBLANES)
                gr = gx_ref[pl.ds(r0, SUBLANES), re]
                gi = gx_ref[pl.ds(r0, SUBLANES), im]
                for s, d in enumerate((1, 2, 4)):
                    gr, gi = _cmul_add(gr, gi, t[2 * s], t[2 * s + 1],
                                       pltpu.roll(gr, SUBLANES - d, 0), pltpu.roll(gi, SUBLANES - d, 0))
                gr, gi = _cmul_add(gr, gi, t[6], t[7], cr, ci)
                g_ref[pl.ds(r0, SUBLANES), re] = gr.astype(BF16)
                g_ref[pl.ds(r0, SUBLANES), im] = gi.astype(BF16)
                hr = jnp.where(row == SUBLANES - 1, cr, pltpu.roll(gr, SUBLANES - 1, 0))
                hi = jnp.where(row == SUBLANES - 1, ci, pltpu.roll(gi, SUBLANES - 1, 0))
                xr = x_ref[pl.ds(r0, SUBLANES), re]
                xi = x_ref[pl.ds(r0, SUBLANES), im]
                dar = dar + xr * hr + xi * hi
                dai = dai + xr * hi - xi * hr
                return (jnp.broadcast_to(gr[0:1, :], gr.shape), jnp.broadcast_to(gi[0:1, :], gi.shape), dar, dai)

            st = lax.fori_loop(0, n_blk, blk, (carry_ref[0, :, ch], carry_ref[1, :, ch], da_ref[0, :, re], da_ref[0, :, im]))
            carry_ref[0, :, ch] = st[0]
            carry_ref[1, :, ch] = st[1]
            da_ref[0, :, re] = st[2]
            da_ref[0, :, im] = st[3]

    return _call(name, body, (n_ex, n_steps),
                 [pl.BlockSpec((tq, width), lambda b, i: (rbr(b, i), 0)), pl.BlockSpec((tq, width), lambda b, i: (rbr(b, i), 0)),
                  pl.BlockSpec((8 * SUBLANES, nj * half), lambda b, i: (0, 0))],
                 [pl.BlockSpec((tq, width), lambda b, i: (rbr(b, i), 0)), pl.BlockSpec((1, SUBLANES, width), lambda b, i: (b, 0, 0))],
                 [_sds((n_rows, width), BF16), _sds((n_ex, SUBLANES, width), F32)],
                 scratch=[pltpu.VMEM((2, SUBLANES, nj * half), F32)], after=after)(gx, x, tabs)


def _ssm_z(gy, wout_ref, nj):
    return jnp.concatenate([_dot(gy, wout_ref[j]) for j in range(nj)], axis=1)


def _ssm_out(name, x, u, dskip, cb, wout4, h, n_rows):
    nj, xc, uc = cb.shape
    no, hw, oc = wout4.shape
    d = h.shape[1]
    tm = ROW_TILE

    def body(x_ref, u_ref, ds_ref, cb_ref, w_ref, h_ref, o_ref, y_ref):
        y = jnp.concatenate([_dot(x_ref[:, j * xc:(j + 1) * xc], cb_ref[j]) for j in range(nj)], axis=1)
        y = y + ds_ref[...] * u_ref[...]
        y_ref[...] = y
        z = _ssm_z(_gelu(y), w_ref, no)
        o_ref[...] = h_ref[...] + z[:, :d] * _sigmoid(z[:, d:])

    return _call(name, body, (n_rows // tm,),
                 [pl.BlockSpec((tm, nj * xc), lambda i: (i, 0)), pl.BlockSpec((tm, hw), lambda i: (i, 0)),
                  pl.BlockSpec((1, hw), lambda i: (0, 0)), pl.BlockSpec((nj, xc, uc), lambda i: (0, 0, 0)),
                  pl.BlockSpec((no, hw, oc), lambda i: (0, 0, 0)), pl.BlockSpec((tm, d), lambda i: (i, 0))],
                 [pl.BlockSpec((tm, d), lambda i: (i, 0)), pl.BlockSpec((tm, hw), lambda i: (i, 0))],
                 [_sds((n_rows, d), F32), _sds((n_rows, hw), F32)])(x, u, dskip, cb, wout4, h)


def _ssm_out_bwd(name, dh, y, u, cb, wout4, n_rows, after=()):
    nj, xc, uc = cb.shape
    no, hw, oc = wout4.shape
    d = dh.shape[1]
    tm = ROW_TILE

    def body(dh_ref, y_ref, u_ref, cb_ref, w_ref, dy_ref, dz_ref, gx_ref, dd_ref):
        y = y_ref[...]
        z = _ssm_z(_gelu(y), w_ref, no)
        za = z[:, :d]
        sg = _sigmoid(z[:, d:])
        dmix = dh_ref[...]
        dz = jnp.concatenate([dmix * sg, dmix * za * sg * (1.0 - sg)], axis=1).astype(BF16)
        dz_ref[...] = dz
        dgy = _dot_nt(dz[:, 0:oc], w_ref[0])
        for j in range(1, no):
            dgy = dgy + _dot_nt(dz[:, j * oc:(j + 1) * oc], w_ref[j])
        dy = dgy * _gelu_grad(y)
        dy_ref[...] = dy
        _acc_out(dd_ref, jnp.sum(dy * u_ref[...], axis=0, keepdims=True), pl.program_id(0) == 0)
        for j in range(nj):
            gx_ref[:, j * xc:(j + 1) * xc] = _dot_nt(dy[:, j * uc:(j + 1) * uc], cb_ref[j])

    return _call(name, body, (n_rows // tm,),
                 [pl.BlockSpec((tm, d), lambda i: (i, 0)), pl.BlockSpec((tm, hw), lambda i: (i, 0)),
                  pl.BlockSpec((tm, hw), lambda i: (i, 0)), pl.BlockSpec((nj, xc, uc), lambda i: (0, 0, 0)),
                  pl.BlockSpec((no, hw, oc), lambda i: (0, 0, 0))],
                 [pl.BlockSpec((tm, hw), lambda i: (i, 0)), pl.BlockSpec((tm, no * oc), lambda i: (i, 0)),
                  pl.BlockSpec((tm, nj * xc), lambda i: (i, 0)), pl.BlockSpec((1, hw), lambda i: (0, 0))],
                 [_sds((n_rows, hw), F32), _sds((n_rows, no * oc), BF16), _sds((n_rows, nj * xc), F32),
                  _sds((1, hw), F32)], after=after)(dh, y, u, cb, wout4)


def _ssm_in_bwd(name, gbu, dy, dskip, bb, w_in, h, g, dh, n_rows):
    nj, uc, xc = bb.shape
    d, hw = w_in.shape
    tm = ROW_TILE

    def body(gb_ref, dy_ref, ds_ref, bb_ref, w_ref, h_ref, g_ref, dh_ref, du_ref, o_ref, dg_ref):
        du = jnp.concatenate([_dot_nt(gb_ref[:, j * xc:(j + 1) * xc], bb_ref[j]) for j in range(nj)], axis=1)
        du = du + dy_ref[...] * ds_ref[...]
        du_ref[...] = du.astype(BF16)
        dhn, dg = _rms_bwd(h_ref[...], g_ref[...], _dot_nt(du, w_ref[...]))
        o_ref[...] = dh_ref[...] + dhn
        _acc_out(dg_ref, dg, pl.program_id(0) == 0)

    return _call(name, body, (n_rows // tm,),
                 [pl.BlockSpec((tm, nj * xc), lambda i: (i, 0)), pl.BlockSpec((tm, hw), lambda i: (i, 0)),
                  pl.BlockSpec((1, hw), lambda i: (0, 0)), pl.BlockSpec((nj, uc, xc), lambda i: (0, 0, 0)),
                  pl.BlockSpec((d, hw), lambda i: (0, 0)), pl.BlockSpec((tm, d), lambda i: (i, 0)),
                  pl.BlockSpec((1, d), lambda i: (0, 0)), pl.BlockSpec((tm, d), lambda i: (i, 0))],
                 [pl.BlockSpec((tm, hw), lambda i: (i, 0)), pl.BlockSpec((tm, d), lambda i: (i, 0)),
                  pl.BlockSpec((1, d), lambda i: (0, 0))],
                 [_sds((n_rows, hw), BF16), _sds((n_rows, d), F32), _sds((1, d), F32)])(gbu, dy, dskip, bb, w_in, h, g, dh)


def _discretize(lam_re, lam_im, log_step, b_re, b_im):
    step = jnp.exp(log_step)[:, None]
    mag = jnp.exp(lam_re * step)
    ar = mag * jnp.cos(lam_im * step)
    ai = mag * jnp.sin(lam_im * step)
    den = lam_re * lam_re + lam_im * lam_im
    nr, ni = ar - 1.0, ai
    cr = (nr * lam_re + ni * lam_im) / den
    ci = (ni * lam_re - nr * lam_im) / den
    bbar_r = cr[..., None] * b_re - ci[..., None] * b_im
    bbar_i = cr[..., None] * b_im + ci[..., None] * b_re
    return ar, ai, bbar_r, bbar_i


def _ssm_mats(lam_re, lam_im, log_step, b_re, b_im, c_re, c_im):
    n_g, n_p, n_c = b_re.shape
    gpc = n_g // 4
    ar, ai, bbar_r, bbar_i = _discretize(lam_re, lam_im, log_step, b_re, b_im)
    eye = jnp.eye(gpc, dtype=F32)

    def in_map(bbar):
        return jnp.einsum('jgpc,gh->jgchp', bbar.reshape(4, gpc, n_p, n_c), eye).reshape(4, gpc * n_c, gpc * n_p)

    def out_map(c):
        return jnp.einsum('jgcp,gh->jgphc', c.reshape(4, gpc, n_c, n_p), eye).reshape(4, gpc * n_p, gpc * n_c)

    bb = jnp.concatenate([in_map(bbar_r), in_map(bbar_i)], axis=2)
    cb = jnp.concatenate([out_map(c_re), -out_map(c_im)], axis=1)
    return bb, cb, ar.reshape(-1), ai.reshape(-1)


def _chunked(v, half):
    return v.reshape(v.shape[:-1] + (4, half))


def _scan_tables(ar, ai, reverse):
    if reverse:
        ai = -ai
    pr, pi = [ar], [ai]
    for _ in range(SUBLANES - 1):
        pr, pi = pr + [pr[-1] * ar - pi[-1] * ai], pi + [pr[-1] * ai + pi[-1] * ar]
    row = jnp.arange(SUBLANES)[:, None]
    tabs = []
    for d in (1, 2, 4):
        keep = (row <= SUBLANES - 1 - d) if reverse else (row >= d)
        tabs += [jnp.where(keep, pr[d - 1][None, :], 0.0), jnp.where(keep, pi[d - 1][None, :], 0.0)]
    order = list(range(SUBLANES))[::-1] if reverse else list(range(SUBLANES))
    tabs += [jnp.stack([pr[k] for k in order]), jnp.stack([pi[k] for k in order])]
    return jnp.concatenate(tabs, axis=0)


def _kv_proj(name, h, g, w_kv, k_gain_t, cos2, sin2, n_rows, n_kv, hd):
    d, kvw = w_kv.shape
    kw = n_kv * hd
    tm = ROW_TILE

    sel, sel_t = _head_selectors(n_kv, hd)

    def body(h_ref, g_ref, w_ref, kg_ref, c_ref, s_ref, e_ref, et_ref, raw_ref, k_ref, v_ref):
        raw = _dot(_rms(h_ref[...], g_ref[...]), w_ref[...])
        raw_ref[...] = raw
        k_ref[...] = _head_prep(raw[:, :kw], kg_ref[...], c_ref[...], s_ref[...], e_ref[...], et_ref[...], hd).astype(BF16)
        v_ref[...] = raw[:, kw:].astype(BF16)

    return _call(name, body, (n_rows // tm,),
                 [pl.BlockSpec((tm, d), lambda i: (i, 0)), pl.BlockSpec((1, d), lambda i: (0, 0)),
                  pl.BlockSpec((d, kvw), lambda i: (0, 0)), pl.BlockSpec((1, kw), lambda i: (0, 0)),
                  pl.BlockSpec((tm, 2 * hd), lambda i: (i, 0)), pl.BlockSpec((tm, 2 * hd), lambda i: (i, 0)),
                  pl.BlockSpec(sel.shape, lambda i: (0, 0)), pl.BlockSpec(sel_t.shape, lambda i: (0, 0))],
                 [pl.BlockSpec((tm, kvw), lambda i: (i, 0)), pl.BlockSpec((tm, kw), lambda i: (i, 0)),
                  pl.BlockSpec((tm, kw), lambda i: (i, 0))],
                 [_sds((n_rows, kvw), F32), _sds((n_rows, kw), BF16), _sds((n_rows, kw), BF16)])(
                     h, g, w_kv, k_gain_t, cos2, sin2, sel, sel_t)


def _q_proj(name, h, g, w_q, q_gain_t, cos2, sin2, n_rows, n_q, hd):
    d, qw = w_q.shape
    tm = ROW_TILE

    sel, sel_t = _head_selectors(n_q, hd)

    def body(h_ref, g_ref, w_ref, qg_ref, c_ref, s_ref, e_ref, et_ref, raw_ref, q_ref):
        raw = _dot(_rms(h_ref[...], g_ref[...]), w_ref[...])
        raw_ref[...] = raw
        q_ref[...] = _head_prep(raw, qg_ref[...], c_ref[...], s_ref[...], e_ref[...], et_ref[...], hd).astype(BF16)

    return _call(name, body, (n_rows // tm,),
                 [pl.BlockSpec((tm, d), lambda i: (i, 0)), pl.BlockSpec((1, d), lambda i: (0, 0)),
                  pl.BlockSpec((d, qw), lambda i: (0, 0)), pl.BlockSpec((1, qw), lambda i: (0, 0)),
                  pl.BlockSpec((tm, 2 * hd), lambda i: (i, 0)), pl.BlockSpec((tm, 2 * hd), lambda i: (i, 0)),
                  pl.BlockSpec(sel.shape, lambda i: (0, 0)), pl.BlockSpec(sel_t.shape, lambda i: (0, 0))],
                 [pl.BlockSpec((tm, qw), lambda i: (i, 0)), pl.BlockSpec((tm, qw), lambda i: (i, 0))],
                 [_sds((n_rows, qw), F32), _sds((n_rows, qw), BF16)])(h, g, w_q, q_gain_t, cos2, sin2, sel, sel_t)


def _attn_specs(seq, n_ex, n_meta, kw):
    nb = seq // WINDOW
    meta_blk = lambda b: (n_ex * seq + META_BLOCK * b + META_BLOCK - n_meta) // n_meta
    return [pl.BlockSpec((WINDOW, kw), lambda b, n: (b * nb + jnp.maximum(n - 1, 0), 0)),
            pl.BlockSpec((WINDOW, kw), lambda b, n: (b * nb + n, 0)),
            pl.BlockSpec((n_meta, kw), lambda b, n: (meta_blk(b), 0))]


def _attn_mask(n, qpk, n_keys):
    rows = qpk * WINDOW
    qi = lax.broadcasted_iota(jnp.int32, (rows, n_keys), 0) & (WINDOW - 1)
    kj = lax.broadcasted_iota(jnp.int32, (rows, n_keys), 1)
    rel = qi + WINDOW - kj
    band = (rel >= 0) & (rel < WINDOW) & ((n > 0) | (kj >= WINDOW))
    return band | (kj >= 2 * WINDOW)


def _stack_heads(ref, h, qpk, hd, dtype=None):
    parts = [ref[:, (h * qpk + gq) * hd:(h * qpk + gq + 1) * hd] for gq in range(qpk)]
    out = jnp.concatenate(parts, axis=0)
    return out if dtype is None else out.astype(dtype)


def _col(tile, c):
    lane = lax.broadcasted_iota(jnp.int32, tile.shape, 1)
    return jnp.sum(jnp.where(lane == c, tile, 0.0), axis=-1, keepdims=True)


def _put_col(col, c, n):
    lane = lax.broadcasted_iota(jnp.int32, (col.shape[0], n), 1)
    return jnp.where(lane == c, col, 0.0)


def _stack_cols(tile, h, qpk):
    return jnp.concatenate([_col(tile, h * qpk + gq) for gq in range(qpk)], axis=0)


def _sink_col(sinks, h, qpk):
    return jnp.concatenate([jnp.broadcast_to(_col(sinks, h * qpk + gq), (WINDOW, 1)) for gq in range(qpk)], axis=0)


def _attn_fwd(name, q, k, v, sinks, n_ex, seq, n_meta, n_kv, qpk, hd):
    nb = seq // WINDOW
    n_q = n_kv * qpk
    kw = n_kv * hd
    qw = n_q * hd
    n_keys = 2 * WINDOW + n_meta
    scale = hd ** -0.5

    def body(q_ref, kp_ref, kc_ref, km_ref, vp_ref, vc_ref, vm_ref, sk_ref, o_ref, lse_ref):
        valid = _attn_mask(pl.program_id(1), qpk, n_keys)
        sinks_v = sk_ref[...]
        o_parts = []
        lse_all = jnp.zeros((WINDOW, n_q), F32)
        for h in range(n_kv):
            hs = slice(h * hd, (h + 1) * hd)
            kb = jnp.concatenate([kp_ref[:, hs], kc_ref[:, hs], km_ref[:, hs]], axis=0)
            vb = jnp.concatenate([vp_ref[:, hs], vc_ref[:, hs], vm_ref[:, hs]], axis=0)
            s = jnp.where(valid, _dot_nt(_stack_heads(q_ref, h, qpk, hd), kb) * scale, NEG_INF)
            skc = _sink_col(sinks_v, h, qpk)
            m = jnp.maximum(jnp.max(s, axis=-1, keepdims=True), skc)
            p = jnp.exp(s - m)
            den = jnp.sum(p, axis=-1, keepdims=True) + jnp.exp(skc - m)
            o = _dot(p, vb) / den
            lse = m + jnp.log(den)
            for gq in range(qpk):
                o_parts.append(o[gq * WINDOW:(gq + 1) * WINDOW])
                lse_all = lse_all + _put_col(lse[gq * WINDOW:(gq + 1) * WINDOW], h * qpk + gq, n_q)
        o_ref[...] = jnp.concatenate(o_parts, axis=1)
        lse_ref[...] = lse_all

    qspec = pl.BlockSpec((WINDOW, qw), lambda b, n: (b * nb + n, 0))
    return _call(name, body, (n_ex, nb),
                 [qspec] + _attn_specs(seq, n_ex, n_meta, kw) + _attn_specs(seq, n_ex, n_meta, kw)
                 + [pl.BlockSpec((1, n_q), lambda b, n: (0, 0))],
                 [qspec, pl.BlockSpec((WINDOW, n_q), lambda b, n: (b * nb + n, 0))],
                 [_sds((n_ex * seq, qw), F32), _sds((n_ex * seq, n_q), F32)])(q, k, k, k, v, v, v, sinks)


def _attn_bwd(name, q, k, v, sinks, o, lse, do, n_ex, seq, n_meta, n_kv, qpk, hd):
    nb = seq // WINDOW
    n_q = n_kv * qpk
    kw = n_kv * hd
    qw = n_q * hd
    n_keys = 2 * WINDOW + n_meta
    scale = hd ** -0.5

    def body(q_ref, kp_ref, kc_ref, km_ref, vp_ref, vc_ref, vm_ref, sk_ref, o_ref, lse_ref, do_ref,
             dq_ref, dk_ref, dv_ref, dkm_ref, dvm_ref, dsk_ref):
        n = pl.program_id(1)

        @pl.when(n == 0)
        def _():
            dk_ref[...] = jnp.zeros_like(dk_ref)
            dv_ref[...] = jnp.zeros_like(dv_ref)
            dkm_ref[...] = jnp.zeros_like(dkm_ref)
            dvm_ref[...] = jnp.zeros_like(dvm_ref)

        @pl.when((n == 0) & (pl.program_id(0) == 0))
        def _():
            dsk_ref[...] = jnp.zeros_like(dsk_ref)

        valid = _attn_mask(n, qpk, n_keys)
        sinks_v = sk_ref[...]
        lse_v = lse_ref[...]
        dq_parts, dk_parts, dv_parts = [], [], []
        dsk = jnp.zeros((1, n_q), F32)
        for h in range(n_kv):
            hs = slice(h * hd, (h + 1) * hd)
            kb = jnp.concatenate([kp_ref[:, hs], kc_ref[:, hs], km_ref[:, hs]], axis=0)
            vb = jnp.concatenate([vp_ref[:, hs], vc_ref[:, hs], vm_ref[:, hs]], axis=0)
            qs = _stack_heads(q_ref, h, qpk, hd)
            dos = _stack_heads(do_ref, h, qpk, hd)
            delta = jnp.sum(dos * _stack_heads(o_ref, h, qpk, hd), axis=-1, keepdims=True)
            lse_c = _stack_cols(lse_v, h, qpk)
            s = jnp.where(valid, _dot_nt(qs, kb) * scale, NEG_INF)
            p = jnp.exp(s - lse_c)
            ds = p * (_dot_nt(dos, vb) - delta)
            dqs = _dot(ds, kb) * scale
            dk_parts.append(_dot_tn(ds, qs) * scale)
            dv_parts.append(_dot_tn(p, dos))
            dsink = -jnp.exp(_sink_col(sinks_v, h, qpk) - lse_c) * delta
            for gq in range(qpk):
                dq_parts.append(dqs[gq * WINDOW:(gq + 1) * WINDOW])
                dsk = dsk + _put_col(jnp.sum(dsink[gq * WINDOW:(gq + 1) * WINDOW], axis=0, keepdims=True), h * qpk + gq, n_q)
        dq_ref[...] = jnp.concatenate(dq_parts, axis=1)
        dsk_ref[...] += dsk
        dkb = jnp.concatenate(dk_parts, axis=1)
        dvb = jnp.concatenate(dv_parts, axis=1)
        prev = pl.ds(pl.multiple_of(jnp.maximum(n - 1, 0) * WINDOW, WINDOW), WINDOW)
        cur = pl.ds(pl.multiple_of(n * WINDOW, WINDOW), WINDOW)
        dk_ref[prev, :] += dkb[0:WINDOW]
        dv_ref[prev, :] += dvb[0:WINDOW]
        dk_ref[cur, :] += dkb[WINDOW:2 * WINDOW]
        dv_ref[cur, :] += dvb[WINDOW:2 * WINDOW]
        dkm_ref[...] += dkb[2 * WINDOW:]
        dvm_ref[...] += dvb[2 * WINDOW:]

    qspec = pl.BlockSpec((WINDOW, qw), lambda b, n: (b * nb + n, 0))
    exspec = pl.BlockSpec((seq, kw), lambda b, n: (b, 0))
    mspec = pl.BlockSpec((n_meta, kw), lambda b, n: (b, 0))
    return _call(name, body, (n_ex, nb),
                 [qspec] + _attn_specs(seq, n_ex, n_meta, kw) + _attn_specs(seq, n_ex, n_meta, kw)
                 + [pl.BlockSpec((1, n_q), lambda b, n: (0, 0)), qspec,
                    pl.BlockSpec((WINDOW, n_q), lambda b, n: (b * nb + n, 0)), qspec],
                 [qspec, exspec, exspec, mspec, mspec, pl.BlockSpec((1, n_q), lambda b, n: (0, 0))],
                 [_sds((n_ex * seq, qw), F32), _sds((n_ex * seq, kw), F32), _sds((n_ex * seq, kw), F32),
                  _sds((n_ex * n_meta, kw), F32), _sds((n_ex * n_meta, kw), F32), _sds((1, n_q), F32)])(
                      q, k, k, k, v, v, v, sinks, o, lse, do)


def _attn_out(name, o, h, w_o, n_rows):
    qw, d = w_o.shape
    tm = ROW_TILE

    def body(o_ref, h_ref, w_ref, out_ref):
        out_ref[...] = h_ref[...] + _dot(o_ref[...], w_ref[...])

    return _call(name, body, (n_rows // tm,),
                 [pl.BlockSpec((tm, qw), lambda i: (i, 0)), pl.BlockSpec((tm, d), lambda i: (i, 0)),
                  pl.BlockSpec((qw, d), lambda i: (0, 0))],
                 pl.BlockSpec((tm, d), lambda i: (i, 0)), _sds((n_rows, d), F32))(o, h, w_o)


def _attn_out_bwd(name, dh, w_o, n_rows):
    qw, d = w_o.shape
    tm = ROW_TILE

    def body(dh_ref, w_ref, do_ref):
        do_ref[...] = _dot_nt(dh_ref[...], w_ref[...])

    return _call(name, body, (n_rows // tm,),
                 [pl.BlockSpec((tm, d), lambda i: (i, 0)), pl.BlockSpec((qw, d), lambda i: (0, 0))],
                 pl.BlockSpec((tm, qw), lambda i: (i, 0)), _sds((n_rows, qw), F32))(dh, w_o)


def _q_bwd(name, dq, qraw, q_gain_t, cos2, sin2, w_q, h, g, dh, n_rows, n_q, hd):
    d, qw = w_q.shape
    tm = ROW_TILE

    sel, sel_t = _head_selectors(n_q, hd)

    def body(dq_ref, raw_ref, qg_ref, c_ref, s_ref, e_ref, et_ref, w_ref, h_ref, g_ref, dh_ref, draw_ref, o_ref, dqg_ref, dg_ref):
        dx, dgain = _head_prep_bwd(raw_ref[...], qg_ref[...], c_ref[...], s_ref[...], e_ref[...], et_ref[...], dq_ref[...], hd)
        draw = dx.astype(BF16)
        draw_ref[...] = draw
        dhn, dg = _rms_bwd(h_ref[...], g_ref[...], _dot_nt(draw, w_ref[...]))
        o_ref[...] = dh_ref[...] + dhn
        first = pl.program_id(0) == 0
        _acc_out(dqg_ref, dgain, first)
        _acc_out(dg_ref, dg, first)

    row = lambda w: pl.BlockSpec((tm, w), lambda i: (i, 0))
    one = lambda w: pl.BlockSpec((1, w), lambda i: (0, 0))
    return _call(name, body, (n_rows // tm,),
                 [row(qw), row(qw), one(qw), row(2 * hd), row(2 * hd), pl.BlockSpec(sel.shape, lambda i: (0, 0)),
                  pl.BlockSpec(sel_t.shape, lambda i: (0, 0)), pl.BlockSpec((d, qw), lambda i: (0, 0)), row(d), one(d), row(d)],
                 [row(qw), row(d), one(qw), one(d)],
                 [_sds((n_rows, qw), BF16), _sds((n_rows, d), F32), _sds((1, qw), F32), _sds((1, d), F32)])(
                     dq, qraw, q_gain_t, cos2, sin2, sel, sel_t, w_q, h, g, dh)


def _kv_bwd(name, dk, dv, kvraw, k_gain_t, cos2, sin2, w_kv, h, g, dh_main, n_rows, n_main, n_kv, hd, after=()):
    d, kvw = w_kv.shape
    kw = n_kv * hd
    tm = ROW_TILE
    n_main_tiles = n_main // tm

    sel, sel_t = _head_selectors(n_kv, hd)

    def body(dk_ref, dv_ref, raw_ref, kg_ref, c_ref, s_ref, e_ref, et_ref, w_ref, h_ref, g_ref, dh_ref, draw_ref, o_ref, dkg_ref,
             dg_ref):
        i = pl.program_id(0)
        dx, dgain = _head_prep_bwd(raw_ref[:, :kw], kg_ref[...], c_ref[...], s_ref[...], e_ref[...], et_ref[...], dk_ref[...], hd)
        draw = jnp.concatenate([dx, dv_ref[...]], axis=1).astype(BF16)
        draw_ref[...] = draw
        dhn, dg = _rms_bwd(h_ref[...], g_ref[...], _dot_nt(draw, w_ref[...]))
        o_ref[...] = jnp.where(i < n_main_tiles, dh_ref[...], 0.0) + dhn
        _acc_out(dkg_ref, dgain, i == 0)
        _acc_out(dg_ref, dg, i == 0)

    row = lambda w: pl.BlockSpec((tm, w), lambda i: (i, 0))
    one = lambda w: pl.BlockSpec((1, w), lambda i: (0, 0))
    return _call(name, body, (n_rows // tm,),
                 [row(kw), row(kw), row(kvw), one(kw), row(2 * hd), row(2 * hd), pl.BlockSpec(sel.shape, lambda i: (0, 0)),
                  pl.BlockSpec(sel_t.shape, lambda i: (0, 0)), pl.BlockSpec((d, kvw), lambda i: (0, 0)), row(d),
                  one(d), pl.BlockSpec((tm, d), lambda i: (jnp.minimum(i, n_main_tiles - 1), 0))],
                 [row(kvw), row(d), one(kw), one(d)],
                 [_sds((n_rows, kvw), BF16), _sds((n_rows, d), F32), _sds((1, kw), F32), _sds((1, d), F32)], after=after)(
                     dk, dv, kvraw, k_gain_t, cos2, sin2, sel, sel_t, w_kv, h, g, dh_main)


def _tn_rms(name, h, g, b, n_rows, out_dtype=F32):
    d = h.shape[1]
    nb = b.shape[1]
    tk = _contract_tile(n_rows)
    return _tn(name, (h, g, b),
               [pl.BlockSpec((tk, d), lambda j, k: (k, 0)), pl.BlockSpec((1, d), lambda j, k: (0, 0)),
                pl.BlockSpec((tk, nb), lambda j, k: (k, 0))],
               lambda j, h_ref, g_ref, b_ref: (_rms(h_ref[...], g_ref[...]), b_ref[...]), 1, d, nb, n_rows, tk, out_dtype=out_dtype)


def _tn_plain(name, a, b, nj, a_cols, b_cols, n_rows, a_fn=None, out_dtype=F32, after=()):
    tk = _contract_tile(n_rows)
    fa = (lambda v: v) if a_fn is None else a_fn
    a_map = (lambda j, k: (k, j)) if a.shape[1] != a_cols else (lambda j, k: (k, 0))
    b_map = (lambda j, k: (k, j)) if b.shape[1] != b_cols else (lambda j, k: (k, 0))
    return _tn(name, (a, b), [pl.BlockSpec((tk, a_cols), a_map), pl.BlockSpec((tk, b_cols), b_map)],
               lambda j, a_ref, b_ref: (fa(a_ref[...]), b_ref[...]), nj, a_cols, b_cols, n_rows, tk, out_dtype=out_dtype,
               after=after)


def _cast_layer(name, a, layer):
    _, r, c = a.shape
    tr = _row_tile(r, 256)

    def body(a_ref, o_ref):
        o_ref[...] = a_ref[0].astype(BF16)

    return _call(name, body, (r // tr,), [pl.BlockSpec((1, tr, c), lambda i: (layer, i, 0))],
                 pl.BlockSpec((tr, c), lambda i: (i, 0)), _sds((r, c), BF16))(a)


def _adamw_math(w, g, m, v):
    c1 = 1.0 - ADAM_B1 ** ADAM_STEP
    c2 = 1.0 - ADAM_B2 ** ADAM_STEP
    nm = ADAM_B1 * m + (1.0 - ADAM_B1) * g
    nv = ADAM_B2 * v + (1.0 - ADAM_B2) * (g * g)
    return -ADAM_LR * ((nm / c1) / (jnp.sqrt(nv / c2) + ADAM_EPS) + ADAM_WD * w), nm, nv


def _adamw(name, w, g, m, v, after=()):
    rows, cols = w.shape
    tr = 128 if rows % 128 == 0 else rows

    def body(w_ref, g_ref, m_ref, v_ref, d_ref, nm_ref, nv_ref):
        d_ref[...], nm_ref[...], nv_ref[...] = _adamw_math(w_ref[...], g_ref[...], m_ref[...], v_ref[...])

    spec = pl.BlockSpec((tr, cols), lambda i: (i, 0))
    return _call(name, body, (rows // tr,), [spec] * 4, [spec] * 3, [_sds((rows, cols), F32)] * 3, after=after)(w, g, m, v)


def _adamw_from_halves(name, w, m, v, sources, half_index, transposed, after=()):
    n_layers, r, c = w.shape
    lanes = 1024
    after = tuple(after)
    if transposed:
        rows_half, tr = c // 2, 128
        grid = (n_layers, r // tr)
        w_spec = pl.BlockSpec((1, tr, c), lambda l, i, s: (l, i, 0))
        g_spec = lambda off: pl.BlockSpec((rows_half, tr), lambda l, i, s: (off // rows_half, i))
    else:
        rows_half = r // 2
        grid = (n_layers, 2)
        w_spec = pl.BlockSpec((1, rows_half, c), lambda l, k, s: (l, k, 0))
        g_spec = lambda off: pl.BlockSpec((rows_half, lanes), lambda l, k, s: (off // rows_half, 0))

    def body(s_ref, w_ref, m_ref, v_ref, t0_ref, o0_ref, t1_ref, o1_ref, *rest):
        g_ref, d_ref, nm_ref, nv_ref = rest[len(after):]
        layer, k, mine = pl.program_id(0), pl.program_id(1), s_ref[0]
        tot = jnp.where(layer == 0, t0_ref[...], t1_ref[...])
        oth = jnp.where(layer == 0, o0_ref[...], o1_ref[...])
        if transposed:
            g = jnp.concatenate([jnp.where(mine == 0, tot, oth), jnp.where(mine == 0, oth, tot)], axis=0).T
        else:
            g = jnp.where(k == mine, tot, oth)
        g_ref[0] = g
        d_ref[0], nm_ref[0], nv_ref[0] = _adamw_math(w_ref[0], g, m_ref[0], v_ref[0])

    (t0, o0, off0), (t1, o1, off1) = sources
    grid_spec = pltpu.PrefetchScalarGridSpec(
        num_scalar_prefetch=1, grid=grid,
        in_specs=[w_spec] * 3 + [g_spec(off0), g_spec(off0), g_spec(off1), g_spec(off1)] + [_ANY] * len(after),
        out_specs=[w_spec] * 4)
    return pl.pallas_call(
        body, name=name, grid_spec=grid_spec, out_shape=[_sds(w.shape, F32)] * 4,
        compiler_params=pltpu.CompilerParams(dimension_semantics=("arbitrary", "arbitrary"),
                                             vmem_limit_bytes=V7X_VMEM_LIMIT))(half_index, w, m, v, t0, o0, t1, o1, *after)


def _position():
    return lax.axis_index("x"), lax.axis_index("y"), lax.axis_index("c")


def _other_chips(x, y):
    return [(1 - x, y), (x, 1 - y), (1 - x, 1 - y)]


def _peers_chips(x, y, c):
    return [(cx, cy, c) for cx, cy in _other_chips(x, y)]


def _peers_sibling(x, y, c):
    return [(x, y, 1 - c)]


def _peers_chips_and_sibling(x, y, c):
    return _peers_chips(x, y, c) + _peers_sibling(x, y, c)


def _comm_call(name, body, n_in, out_shape, scratch, sequencer=None):
    if sequencer is None:
        return pl.pallas_call(
            body, name=name, in_specs=[_HBM] * n_in, out_specs=[_HBM] * len(out_shape), out_shape=out_shape,
            scratch_shapes=list(scratch),
            compiler_params=pltpu.CompilerParams(has_side_effects=True, vmem_limit_bytes=V7X_VMEM_LIMIT))
    collective_id, peers = sequencer

    def seq_body(*refs):
        barrier = pltpu.get_barrier_semaphore()
        plist = peers(*_position())
        for peer in plist:
            pl.semaphore_signal(barrier, inc=1, device_id=peer, device_id_type=MESH)
        pl.semaphore_wait(barrier, len(plist))
        body(*refs)

    return pl.kernel(seq_body, out_type=out_shape, mesh=plsc.ScalarSubcoreMesh(axis_name="sequencer", num_cores=1), name=name,
                     scratch_types=list(scratch), compiler_params=pltpu.CompilerParams(collective_id=collective_id))


def _n_chunks(rows, want, dtype):
    align = 16 if dtype == BF16 else 8
    n = want
    while n > 1 and (rows % n or (rows // n) % align):
        n -= 1
    return n


def _remote(src, dst, send_sem, recv_sem, device):
    return pltpu.make_async_remote_copy(src_ref=src, dst_ref=dst, send_sem=send_sem, recv_sem=recv_sem,
                                        device_id=device, device_id_type=MESH)


def _start_in_chunks(src, dst, send_sem, recv_sem, device, want=8):
    rows = src.shape[0]
    n = _n_chunks(rows, want, src.dtype)
    for i in range(n):
        part = pl.ds(i * (rows // n), rows // n)
        _remote(src.at[part], dst.at[part], send_sem, recv_sem, device).start()


def _all_gather_chips(name, shards, split, collective_id=None):
    n = len(shards)

    def body(*refs):
        ins, outs = refs[:n], refs[n:2 * n]
        send_sems, recv_sems, local_sems = refs[2 * n:]
        x, y, c = _position()
        me = 2 * x + y
        chips = _other_chips(x, y)
        sibling = (x, y, 1 - c)
        sends, forwards = [], []
        for t in range(n):
            pltpu.make_async_copy(ins[t], outs[t].at[me], local_sems.at[t]).start()
        for t in range(n):
            r = ins[t].shape[0]
            rows = pl.ds(c * (r // 2), r // 2) if split[t] else pl.ds(0, r)
            for k, (cx, cy) in enumerate(chips):
                src, dst = ins[t].at[rows], outs[t].at[me, rows]
                _start_in_chunks(src, dst, send_sems.at[t, k], recv_sems.at[t, k], (cx, cy, c), want=4)
                sends.append(_remote(src, dst, send_sems.at[t, k], recv_sems.at[t, k], (cx, cy, c)))
        for t in range(n):
            r = ins[t].shape[0]
            rows = pl.ds(c * (r // 2), r // 2) if split[t] else pl.ds(0, r)
            for k, (cx, cy) in enumerate(chips):
                landed = outs[t].at[2 * cx + cy, rows]
                _remote(landed, landed, send_sems.at[t, k], recv_sems.at[t, k], (cx, cy, c)).wait_recv()
                if split[t]:
                    _start_in_chunks(landed, landed, send_sems.at[t, 3 + k], recv_sems.at[t, 3 + k], sibling, want=4)
                    forwards.append(_remote(landed, landed, send_sems.at[t, 3 + k], recv_sems.at[t, 3 + k], sibling))
        for t in range(n):
            if split[t]:
                r = ins[t].shape[0]
                other = pl.ds((1 - c) * (r // 2), r // 2)
                for k, (cx, cy) in enumerate(chips):
                    landed = outs[t].at[2 * cx + cy, other]
                    pltpu.make_async_remote_copy(
                        src_ref=landed, dst_ref=landed, send_sem=send_sems.at[t, 3 + k], recv_sem=recv_sems.at[t, 3 + k],
                        device_id=sibling, device_id_type=MESH).wait_recv()
        for cp in sends + forwards:
            cp.wait_send()
        for t in range(n):
            pltpu.make_async_copy(ins[t], outs[t].at[me], local_sems.at[t]).wait()

    out_shape = [_sds((N_CHIPS,) + s.shape, s.dtype) for s in shards]
    sequencer = None if collective_id is None else (collective_id, _peers_chips_and_sibling)
    return _comm_call(name, body, n, out_shape,
                      [pltpu.SemaphoreType.DMA((n, 6)), pltpu.SemaphoreType.DMA((n, 6)), pltpu.SemaphoreType.DMA((n,))],
                      sequencer)(*shards)


def _swap_halves_with_sibling(name, blob, collective_id=None):
    def body(b_ref, theirs_ref, send_sem, recv_sem):
        x, y, c = _position()
        sibling = (x, y, 1 - c)
        for k in range(b_ref.shape[1]):
            _start_in_chunks(b_ref.at[1 - c, k], theirs_ref.at[k], send_sem, recv_sem, sibling)
        _remote(b_ref.at[1 - c], theirs_ref, send_sem, recv_sem, sibling).wait()

    return _comm_call(name, body, 1, [_sds(blob.shape[1:], blob.dtype)],
                      [pltpu.SemaphoreType.DMA(()), pltpu.SemaphoreType.DMA(())],
                      None if collective_id is None else (collective_id, _peers_sibling))(blob)[0]


def _scatter_to_chips(name, parts, collective_id=None):
    def body(p_ref, o_ref, send_sems, recv_sems, local_sems):
        x, y, c = _position()
        me = 2 * x + y
        rows = p_ref.shape[1]
        n_loc = _n_chunks(rows, 16, p_ref.dtype)
        locs = [pltpu.make_async_copy(p_ref.at[me, pl.ds(i * (rows // n_loc), rows // n_loc)],
                                      o_ref.at[me, pl.ds(i * (rows // n_loc), rows // n_loc)], local_sems.at[i])
                for i in range(n_loc)]
        for loc in locs:
            loc.start()
        sends = []
        for k, (cx, cy) in enumerate(_other_chips(x, y)):
            src, dst = p_ref.at[2 * cx + cy], o_ref.at[me]
            _start_in_chunks(src, dst, send_sems.at[k], recv_sems.at[k], (cx, cy, c))
            sends.append(_remote(src, dst, send_sems.at[k], recv_sems.at[k], (cx, cy, c)))
        for k, (cx, cy) in enumerate(_other_chips(x, y)):
            landed = o_ref.at[2 * cx + cy]
            _remote(landed, landed, send_sems.at[k], recv_sems.at[k], (cx, cy, c)).wait_recv()
        for cp in sends:
            cp.wait_send()
        for loc in locs:
            loc.wait()

    def local_sems_shape(rows):
        return pltpu.SemaphoreType.DMA((_n_chunks(rows, 16, parts.dtype),))

    return _comm_call(name, body, 1, [_sds(parts.shape, parts.dtype)],
                      [pltpu.SemaphoreType.DMA((3,)), pltpu.SemaphoreType.DMA((3,)), local_sems_shape(parts.shape[1])],
                      None if collective_id is None else (collective_id, _peers_chips))(parts)[0]


def _share_with_sibling(name, mine, collective_id=None):
    def body(m_ref, o_ref, send_sem, recv_sem):
        x, y, c = _position()
        sibling = (x, y, 1 - c)
        _start_in_chunks(m_ref, o_ref, send_sem, recv_sem, sibling, want=16)
        _remote(m_ref, o_ref, send_sem, recv_sem, sibling).wait()

    return _comm_call(name, body, 1, [_sds(mine.shape, mine.dtype)],
                      [pltpu.SemaphoreType.DMA(()), pltpu.SemaphoreType.DMA(())],
                      None if collective_id is None else (collective_id, _peers_sibling))(mine)[0]


def _row_tile(rows, cap=640):
    best = rows
    for t in range(16, min(rows, cap) + 1, 16):
        if rows % t == 0:
            best = t
    return best


_ANY = pl.BlockSpec(memory_space=pl.ANY)


def _add_my_half(name, blob, theirs, half_index, out_dtype, after):
    n, rows, cols = theirs.shape
    tr = _row_tile(rows)
    after = tuple(after)

    def body(c_ref, a_ref, b_ref, *rest):
        o_ref = rest[-1]
        o_ref[...] = (a_ref[0].astype(F32) + b_ref[...].astype(F32)).astype(out_dtype)

    spec = pl.BlockSpec((1, tr, cols), lambda k, i, c: (k, i, 0))
    grid_spec = pltpu.PrefetchScalarGridSpec(
        num_scalar_prefetch=1, grid=(n, rows // tr),
        in_specs=[pl.BlockSpec((1, 1, tr, cols), lambda k, i, c: (c[0], k, i, 0)), spec] + [_ANY] * len(after), out_specs=spec)
    return pl.pallas_call(
        body, name=name, grid_spec=grid_spec, out_shape=_sds(theirs.shape, out_dtype),
        compiler_params=pltpu.CompilerParams(dimension_semantics=("arbitrary", "arbitrary"),
                                             vmem_limit_bytes=V7X_VMEM_LIMIT))(half_index, blob, theirs, *after)


def _sum_slots(name, parts, after):
    n, rows, cols = parts.shape
    tr = _row_tile(rows)

    def body(p_ref, o_ref):
        acc = p_ref[0].astype(F32)
        for k in range(1, n):
            acc = acc + p_ref[k].astype(F32)
        o_ref[...] = acc

    return _call(name, body, (rows // tr,), [pl.BlockSpec((n, tr, cols), lambda i: (0, i, 0))],
                 pl.BlockSpec((tr, cols), lambda i: (i, 0)), _sds((rows, cols), F32), after=after)(parts)


def _reduce_small_adamw(name, grads, loss_tile, ws, ms, vs, after=()):
    n = len(grads)
    srcs = list(grads) + [loss_tile]
    after = tuple(after)

    def body(*refs):
        refs = refs[:4 * n + 1] + refs[4 * n + 1 + len(after):]
        g_in, w_in, m_in, v_in = refs[:n + 1], refs[n + 1:2 * n + 1], refs[2 * n + 1:3 * n + 1], refs[3 * n + 1:4 * n + 1]
        outs = refs[4 * n + 1:8 * n + 2]
        g_out, d_out, nm_out, nv_out, loss_out = outs[:n], outs[n:2 * n], outs[2 * n:3 * n], outs[3 * n:4 * n], outs[4 * n]
        bufs = refs[8 * n + 2:9 * n + 3]
        send_sems, recv_sems = refs[9 * n + 3:]
        x, y, c = _position()
        me = 4 * x + 2 * y + c
        chip = 2 * x + y
        peers = [(1 - x if dlt & 4 else x, 1 - y if dlt & 2 else y, 1 - c if dlt & 1 else c) for dlt in range(1, N_DEV)]
        sends = []
        for t in range(n + 1):
            bufs[t][me] = g_in[t][...]
            for k, peer in enumerate(peers):
                cp = _remote(g_in[t], bufs[t].at[me], send_sems.at[t, k], recv_sems.at[t, k], peer)
                cp.start()
                sends.append(cp)
        for t in range(n + 1):
            for k, (tx, ty, tc) in enumerate(peers):
                landed = bufs[t].at[4 * tx + 2 * ty + tc]
                _remote(landed, landed, send_sems.at[t, k], recv_sems.at[t, k], (tx, ty, tc)).wait_recv()
        for cp in sends:
            cp.wait_send()
        for t in range(n + 1):
            total = bufs[t][0]
            for k in range(1, N_DEV):
                total = total + bufs[t][k]
            if t == n:
                loss_out[...] = total
                continue
            cols = w_in[t].shape[1]
            if cols == total.shape[1]:
                g_out[t][...] = total
                d_out[t][...], nm_out[t][...], nv_out[t][...] = _adamw_math(w_in[t][...], total, m_in[t][...], v_in[t][...])
            else:
                for j in range(N_CHIPS):
                    @pl.when(chip == j)
                    def _(t=t, j=j, cols=cols, total=total):
                        mine = total[:, j * cols:(j + 1) * cols]
                        g_out[t][...] = mine
                        d_out[t][...], nm_out[t][...], nv_out[t][...] = _adamw_math(w_in[t][...], mine, m_in[t][...], v_in[t][...])

    w_shapes = [_sds(a.shape, F32) for a in ws]
    return pl.pallas_call(
        body, name=name, in_specs=[_VMEM] * (4 * n + 1) + [_ANY] * len(after), out_specs=[_VMEM] * (4 * n + 1),
        out_shape=w_shapes * 4 + [_sds(loss_tile.shape, F32)],
        scratch_shapes=[pltpu.VMEM((N_DEV,) + a.shape, F32) for a in srcs]
        + [pltpu.SemaphoreType.DMA((n + 1, N_DEV - 1)), pltpu.SemaphoreType.DMA((n + 1, N_DEV - 1))],
        compiler_params=pltpu.CompilerParams(has_side_effects=True, vmem_limit_bytes=V7X_VMEM_LIMIT))(
            *srcs, *ws, *ms, *vs, *after)


_BIG = ("ffn1_w_gate_up", "ffn1_w_down", "ffn2_w_gate_up", "ffn2_w_down", "ssm_w_in", "ssm_w_out", "w_kv", "attn_w_q", "attn_w_o")
_TRANSPOSED = ("ffn1_w_gate_up", "ffn2_w_gate_up")
_FROM_HALVES = _TRANSPOSED + ("ffn1_w_down", "ffn2_w_down")
_SMALL = ("meta_tokens", "ffn1_norm", "mix_norm", "ffn2_norm", "ssm_lambda_re", "ssm_lambda_im", "ssm_b_re", "ssm_b_im",
          "ssm_c_re", "ssm_c_im", "ssm_log_step", "ssm_d", "kv_norm", "k_norm", "q_norm", "attn_sinks")
_ORDER = ("meta_tokens", "ffn1_norm", "ffn1_w_gate_up", "ffn1_w_down", "mix_norm", "ffn2_norm", "ffn2_w_gate_up", "ffn2_w_down",
          "ssm_w_in", "ssm_lambda_re", "ssm_lambda_im", "ssm_b_re", "ssm_b_im", "ssm_c_re", "ssm_c_im", "ssm_log_step", "ssm_d",
          "ssm_w_out", "kv_norm", "w_kv", "k_norm", "attn_w_q", "q_norm", "attn_sinks", "attn_w_o")


def _step(x, target, w, m, v):
    n_ex, seq, d = x.shape
    n_meta = w["meta_tokens"].shape[0]
    n_main = n_ex * seq
    n_all = n_main + n_ex * META_BLOCK
    n_g, n_p, n_c = w["ssm_b_re"].shape[1:]
    hd = w["k_norm"].shape[0]
    n_kv = w["w_kv"].shape[1] // (2 * hd)
    n_q = w["attn_w_q"].shape[2] // hd
    qpk = n_q // n_kv
    px, py, pc = _position()
    chip = 2 * px + py

    def cast(name, layer=0):
        a = w[name]
        return _cast_layer(f"cast_{name}_{layer}", a if a.ndim == 3 else a[None], layer)

    first = [cast("ffn1_w_gate_up"), cast("ffn1_w_down"), cast("ssm_w_in"), cast("ssm_w_out"), w["meta_tokens"], w["ssm_d"]]
    g_a = _all_gather_chips("gather_first", first, [True, True, True, True, False, False], collective_id=12)
    second = [cast("ffn2_w_gate_up"), cast("ffn2_w_down"), cast("w_kv")]
    g_b = _all_gather_chips("gather_second", second, [True] * 3, collective_id=1)
    third = [cast("ffn1_w_gate_up", 1), cast("ffn1_w_down", 1), cast("attn_w_q"), cast("attn_w_o"),
             cast("ffn2_w_gate_up", 1), cast("ffn2_w_down", 1)]
    g_c = _all_gather_chips("gather_third", third, [True] * 6, collective_id=2)
    wgu = {("ffn1", 0): g_a[0], ("ffn1", 1): g_c[0], ("ffn2", 0): g_b[0], ("ffn2", 1): g_c[4]}
    wd = {("ffn1", 0): g_a[1], ("ffn1", 1): g_c[1], ("ffn2", 0): g_b[1], ("ffn2", 1): g_c[5]}
    wd = {key: a.reshape(-1, d) for key, a in wd.items()}
    w_in = g_a[2].reshape(d, -1)
    wout4 = g_a[3]
    w_q = g_c[2].reshape(d, -1)
    w_o = g_c[3].reshape(-1, d)
    w_kv = g_b[2].reshape(d, -1)
    meta_full = jnp.transpose(g_a[4], (1, 0, 2)).reshape(n_meta, d)
    dskip = g_a[5].reshape(1, -1)

    row1 = lambda a: a.reshape(1, -1)
    ssm_args = tuple(w[k][0] for k in ("ssm_lambda_re", "ssm_lambda_im", "ssm_log_step", "ssm_b_re", "ssm_b_im", "ssm_c_re", "ssm_c_im"))
    (bb, cb, a_re, a_im), ssm_vjp = jax.vjp(_ssm_mats, *ssm_args)
    bb16, cb16 = bb.astype(BF16), cb.astype(BF16)
    a_re_s, a_im_s = lax.stop_gradient(a_re), lax.stop_gradient(a_im)
    half = n_g * n_p // 4
    tabs_f = _scan_tables(a_re_s, a_im_s, False)
    tabs_b = _scan_tables(a_re_s, a_im_s, True)

    freqs = ROPE_THETA ** (-jnp.arange(0, hd // 2, dtype=F32) * 2.0 / hd)
    pos_main = jnp.tile(n_meta + jnp.arange(seq), n_ex)
    pos_meta = jnp.tile(jnp.maximum(jnp.arange(META_BLOCK) - (META_BLOCK - n_meta), 0), n_ex)
    ang = jnp.concatenate([pos_main, pos_meta]).astype(F32)[:, None] * freqs[None, :]
    cos = jnp.concatenate([jnp.cos(ang), jnp.cos(ang)] * 2, axis=1)
    sin_s = jnp.concatenate([-jnp.sin(ang), jnp.sin(ang)] * 2, axis=1)
    k_gain_t = jnp.tile(row1(w["k_norm"]), (1, n_kv))
    q_gain_t = jnp.tile(row1(w["q_norm"][0]), (1, n_q))

    meta_block = jnp.concatenate([jnp.zeros((META_BLOCK - n_meta, d), F32), meta_full], axis=0)
    h0 = jnp.concatenate([x.reshape(n_main, d)] + [meta_block] * n_ex, axis=0)

    g = lambda name, layer: row1(w[name][layer])
    h1, gu1 = _ffn_fwd("l0_ffn1", h0, g("ffn1_norm", 0), wgu["ffn1", 0], wd["ffn1", 0], n_all)
    u, bu = _ssm_in("ssm_in", h1, g("mix_norm", 0), w_in, bb16, n_all)
    xs = _scan_fwd("ssm_scan", bu, tabs_f, n_ex, seq)
    h2, y = _ssm_out("ssm_out", xs, u, dskip, cb16, wout4, h1, n_all)
    h3, gu2 = _ffn_fwd("l0_ffn2", h2, g("ffn2_norm", 0), wgu["ffn2", 0], wd["ffn2", 0], n_all)
    kvraw, k, vv = _kv_proj("kv_proj", h3, row1(w["kv_norm"]), w_kv, k_gain_t, cos, sin_s, n_all, n_kv, hd)
    h4, gu3 = _ffn_fwd("l1_ffn1", h3, g("ffn1_norm", 1), wgu["ffn1", 1], wd["ffn1", 1], n_main)
    qraw, q = _q_proj("q_proj", h4, g("mix_norm", 1), w_q, q_gain_t, cos, sin_s, n_main, n_q, hd)
    sinks = row1(w["attn_sinks"][0])
    o, lse = _attn_fwd("attn_fwd", q, k, vv, sinks, n_ex, seq, n_meta, n_kv, qpk, hd)
    h5 = _attn_out("attn_out", o, h4, w_o, n_main)
    (dh6, loss_tile), gu4 = _ffn_fwd("l1_ffn2", h5, g("ffn2_norm", 1), wgu["ffn2", 1], wd["ffn2", 1], n_main,
                                     target=target.reshape(n_main, d))

    lanes = 1024

    def rs_start(tag, entries, ids):
        pieces = [gr.reshape(N_CHIPS, 2, -1, lanes) for _, _, gr in entries]
        blob = jnp.transpose(jnp.concatenate(pieces, axis=2), (1, 0, 2, 3)).astype(BF16)
        return dict(tag=tag, entries=entries, ids=ids, blob=blob, theirs=_swap_halves_with_sibling(tag + "_swap", blob, ids[0]))

    def rs_scatter(st, after):
        chip_sum = _add_my_half(st["tag"] + "_chip_sum", st["blob"], st["theirs"], jnp.reshape(pc, (1,)).astype(jnp.int32), BF16, after)
        st["chip_sum"] = chip_sum
        st["landed"] = _scatter_to_chips(st["tag"] + "_scatter", chip_sum, st["ids"][1])

    def rs_finish(st, after):
        total = _sum_slots(st["tag"] + "_sum", st["landed"], after)
        st["total"] = total
        other = _share_with_sibling(st["tag"] + "_share", total, st["ids"][2])
        halves = (jnp.where(pc == 0, total, other), jnp.where(pc == 0, other, total))
        out, off = {}, 0
        for name, layer, gr in st["entries"]:
            rows = gr.shape[1] * gr.shape[2] // lanes // 2
            if name in _FROM_HALVES:
                out[name, layer] = (total, other, off)
            else:
                out[name, layer] = jnp.concatenate([hv[off:off + rows].reshape(-1) for hv in halves])
            off += rows
        return out

    small = {}
    dh5, dg_f2l1, dwgu_f2l1, dwd_f2l1 = _ffn_bwd("l1_ffn2", dh6, h5, g("ffn2_norm", 1), gu4, wgu["ffn2", 1], wd["ffn2", 1], n_main)
    do = _attn_out_bwd("attn_out_bwd", dh5, w_o, n_main)
    dw_o = _tn_plain("attn_dwo", o, dh5, 1, o.shape[1], d, n_main, out_dtype=BF16).reshape(N_CHIPS, -1, d)
    dq, dk_main, dv_main, dk_meta, dv_meta, dsinks = _attn_bwd("attn_bwd", q, k, vv, sinks, o, lse, do, n_ex, seq, n_meta, n_kv, qpk, hd)
    dqraw, dh4, dq_gain, dg_mix1 = _q_bwd("q_bwd", dq, qraw, q_gain_t, cos, sin_s, w_q, h4, g("mix_norm", 1), dh5, n_main, n_q, hd)
    dw_q = _tn_rms("attn_dwq", h4, g("mix_norm", 1), dqraw, n_main, out_dtype=BF16).reshape(N_CHIPS, -1, dqraw.shape[1])
    dh3m, dg_f1l1, dwgu_f1l1, dwd_f1l1 = _ffn_bwd("l1_ffn1", dh4, h3, g("ffn1_norm", 1), gu3, wgu["ffn1", 1], wd["ffn1", 1], n_main)
    rs1 = rs_start("rs1", [("ffn2_w_gate_up", 1, dwgu_f2l1), ("ffn1_w_gate_up", 1, dwgu_f1l1), ("ffn2_w_down", 1, dwd_f2l1),
                           ("ffn1_w_down", 1, dwd_f1l1), ("attn_w_o", 0, dw_o), ("attn_w_q", 0, dw_q)], (3, 4, 5))

    def with_meta(main, meta):
        blocks = [jnp.pad(meta[b * n_meta:(b + 1) * n_meta], ((META_BLOCK - n_meta, 0), (0, 0))) for b in range(n_ex)]
        return jnp.concatenate([main] + blocks, axis=0)

    dkvraw, dh3, dk_gain, dg_kv = _kv_bwd("kv_bwd", with_meta(dk_main, dk_meta), with_meta(dv_main, dv_meta), kvraw, k_gain_t,
                                          cos, sin_s, w_kv, h3, row1(w["kv_norm"]), dh3m, n_all, n_main, n_kv, hd,
                                          after=(rs1["blob"],))
    rs_scatter(rs1, after=(dh3,))
    dw_kv = _tn_rms("kv_dw", h3, row1(w["kv_norm"]), dkvraw, n_all, out_dtype=BF16).reshape(N_CHIPS, -1, dkvraw.shape[1])
    dh2, dg_f2l0, dwgu_f2l0, dwd_f2l0 = _ffn_bwd("l0_ffn2", dh3, h2, g("ffn2_norm", 0), gu2, wgu["ffn2", 0], wd["ffn2", 0], n_all,
                                                 after=(rs1["chip_sum"],))
    reduced = rs_finish(rs1, after=(dh2, dwgu_f2l0, dwd_f2l0, dw_kv))
    rs0a = rs_start("rs0a", [("ffn2_w_gate_up", 0, dwgu_f2l0), ("ffn2_w_down", 0, dwd_f2l0), ("w_kv", 0, dw_kv)], (6, 7, 8))

    dy, dz, gx, dd = _ssm_out_bwd("ssm_out_bwd", dh2, y, u, cb16, wout4, n_all, after=(rs1["total"], rs0a["blob"]))
    rs_scatter(rs0a, after=(dy,))
    hw = y.shape[1]
    oc = wout4.shape[2]
    dw_out = _tn_plain("ssm_dwout", y, dz, wout4.shape[0], hw, oc, n_all, a_fn=_gelu, out_dtype=BF16)
    gbu, da = _scan_bwd("ssm_scan_bwd", gx, xs, tabs_b, n_ex, seq, after=(rs0a["chip_sum"],))
    du, dh1, dg_mix0 = _ssm_in_bwd("ssm_in_bwd", gbu, dy, dskip, bb16, w_in, h1, g("mix_norm", 0), dh2, n_all)
    reduced.update(rs_finish(rs0a, after=(dh1,)))
    dw_in = _tn_rms("ssm_dwin", h1, g("mix_norm", 0), du, n_all, out_dtype=BF16).reshape(N_CHIPS, -1, hw)
    (dh0, dh0_meta), dg_f1l0, dwgu_f1l0, dwd_f1l0 = _ffn_bwd("l0_ffn1", dh1, h0, g("ffn1_norm", 0), gu1, wgu["ffn1", 0], wd["ffn1", 0],
                                                             n_all, n_main, after=(rs0a["total"],))
    rs0b = rs_start("rs0b", [("ffn1_w_gate_up", 0, dwgu_f1l0), ("ffn1_w_down", 0, dwd_f1l0), ("ssm_w_out", 0, dw_out),
                             ("ssm_w_in", 0, dw_in)], (9, 10, 11))
    dcb = _tn_plain("ssm_dcb", xs, dy, 4, xs.shape[1] // 4, hw // 4, n_all, after=(rs0b["blob"],))
    rs_scatter(rs0b, after=(dcb,))
    dbb = _tn_plain("ssm_dbb", u, gbu, 4, hw // 4, gbu.shape[1] // 4, n_all, after=(rs0b["chip_sum"],))

    grad_x = dh0.reshape(n_ex, seq, d)
    da_sum = jnp.sum(da, axis=(0, 1)).reshape(4, 2, half)
    d_ssm = ssm_vjp((dbb, dcb, da_sum[:, 0].reshape(-1), da_sum[:, 1].reshape(-1)))
    for key, val in zip(("ssm_lambda_re", "ssm_lambda_im", "ssm_log_step", "ssm_b_re", "ssm_b_im", "ssm_c_re", "ssm_c_im"), d_ssm):
        small[key] = val[None]
    small["meta_tokens"] = sum(dh0_meta[META_BLOCK * (b + 1) - n_meta:META_BLOCK * (b + 1)] for b in range(n_ex))
    small["ffn1_norm"] = jnp.concatenate([dg_f1l0, dg_f1l1], axis=0)
    small["ffn2_norm"] = jnp.concatenate([dg_f2l0, dg_f2l1], axis=0)
    small["mix_norm"] = jnp.concatenate([dg_mix0, dg_mix1], axis=0)
    small["ssm_d"] = dd
    small["kv_norm"] = dg_kv.reshape(-1)
    small["k_norm"] = jnp.sum(dk_gain.reshape(n_kv, hd), axis=0)
    small["q_norm"] = jnp.sum(dq_gain.reshape(n_q, hd), axis=0, keepdims=True)
    small["attn_sinks"] = dsinks

    def view(name, a):
        if name in ("ssm_b_re", "ssm_b_im"):
            return a.reshape(-1, 128)
        return a.reshape(1, -1) if a.ndim == 1 else a.reshape(-1, a.shape[-1])

    grads, deltas, new_m, new_v = {}, {}, {}, {}

    def adamw_matrix(name, after=()):
        shape = w[name].shape
        if name in _FROM_HALVES:
            grads[name], deltas[name], new_m[name], new_v[name] = _adamw_from_halves(
                "adamw_" + name, w[name], m[name], v[name], [reduced[name, 0], reduced[name, 1]],
                jnp.reshape(pc, (1,)).astype(jnp.int32), name in _TRANSPOSED, after=after)
            return new_v[name]
        layers = [reduced[name, layer] for layer in range(2) if (name, layer) in reduced]
        grads[name] = jnp.concatenate(layers).reshape(shape)
        two_d = lambda a: a.reshape(-1, shape[-1])
        dl, nm, nv = _adamw("adamw_" + name, two_d(w[name]), two_d(grads[name]), two_d(m[name]), two_d(v[name]), after=after)
        deltas[name], new_m[name], new_v[name] = dl.reshape(shape), nm.reshape(shape), nv.reshape(shape)
        return nv

    placed = (rs0b["chip_sum"],)
    for name in ("ffn2_w_down", "attn_w_o", "attn_w_q", "w_kv"):
        placed = (adamw_matrix(name, after=placed),)
    tail = _reduce_small_adamw("small_tail", [view(k, small[k]) for k in _SMALL], loss_tile,
                               *[[view(k, t[k]) for k in _SMALL] for t in (w, m, v)], after=placed)
    n_small = len(_SMALL)
    for i, k in enumerate(_SMALL):
        grads[k], deltas[k] = tail[i].reshape(w[k].shape), tail[n_small + i].reshape(w[k].shape)
        new_m[k], new_v[k] = tail[2 * n_small + i].reshape(w[k].shape), tail[3 * n_small + i].reshape(w[k].shape)
    loss = jnp.sum(tail[-1])
    reduced.update(rs_finish(rs0b, after=(tail[-1],)))
    adamw_matrix("ffn2_w_gate_up", after=(rs0b["total"],))
    for name in ("ffn1_w_gate_up", "ffn1_w_down", "ssm_w_in", "ssm_w_out"):
        adamw_matrix(name)
    return (loss, grad_x, *[grads[k] for k in _ORDER], *[deltas[k] for k in _ORDER], *[new_m[k] for k in _ORDER],
            *[new_v[k] for k in _ORDER])


def kernel(x, meta_tokens, ffn1_norm, ffn1_w_gate_up, ffn1_w_down, mix_norm, ffn2_norm, ffn2_w_gate_up, ffn2_w_down, ssm_w_in, ssm_lambda_re, ssm_lambda_im, ssm_b_re, ssm_b_im, ssm_c_re, ssm_c_im, ssm_log_step, ssm_d, ssm_w_out, kv_norm, w_kv, k_norm, attn_w_q, q_norm, attn_sinks, attn_w_o, loss_target, m_meta_tokens, m_ffn1_norm, m_ffn1_w_gate_up, m_ffn1_w_down, m_mix_norm, m_ffn2_norm, m_ffn2_w_gate_up, m_ffn2_w_down, m_ssm_w_in, m_ssm_lambda_re, m_ssm_lambda_im, m_ssm_b_re, m_ssm_b_im, m_ssm_c_re, m_ssm_c_im, m_ssm_log_step, m_ssm_d, m_ssm_w_out, m_kv_norm, m_w_kv, m_k_norm, m_attn_w_q, m_q_norm, m_attn_sinks, m_attn_w_o, v_meta_tokens, v_ffn1_norm, v_ffn1_w_gate_up, v_ffn1_w_down, v_mix_norm, v_ffn2_norm, v_ffn2_w_gate_up, v_ffn2_w_down, v_ssm_w_in, v_ssm_lambda_re, v_ssm_lambda_im, v_ssm_b_re, v_ssm_b_im, v_ssm_c_re, v_ssm_c_im, v_ssm_log_step, v_ssm_d, v_ssm_w_out, v_kv_norm, v_w_kv, v_k_norm, v_attn_w_q, v_q_norm, v_attn_sinks, v_attn_w_o):
    args = locals()
    w = {k: args[k] for k in _ORDER}
    m = {k: args["m_" + k] for k in _ORDER}
    v = {k: args["v_" + k] for k in _ORDER}
    return _step(x, loss_target, w, m, v)
```

```python
import functools
import math

import jax
import jax.numpy as jnp
from jax import lax
from jax.experimental import pallas as pl
from jax.experimental.pallas import tpu as pltpu
from jax.experimental.pallas import tpu_sc as plsc

F32 = jnp.float32
BF16 = jnp.bfloat16
MESH = pl.DeviceIdType.MESH

EPS = 1e-6
NEG_INF = -1e30
ROPE_THETA = 10000.0
WINDOW = 128
META_BLOCK = 128
ROW_TILE = 256
SUBLANES = 8
V7X_VMEM_LIMIT = 56 * 2**20
N_CHIPS = 4
N_DEV = 8

ADAM_LR = 0.001
ADAM_B1 = 0.9
ADAM_B2 = 0.999
ADAM_EPS = 1e-08
ADAM_WD = 0.01
ADAM_STEP = 10

_HBM = pl.BlockSpec(memory_space=pltpu.HBM)
_VMEM = pl.BlockSpec(memory_space=pltpu.VMEM)


def _call(name, body, grid, in_specs, out_specs, out_shape, scratch=(), after=()):
    after = tuple(after)
    n_in = len(in_specs)

    def wrapped(*refs):
        return body(*refs[:n_in], *refs[n_in + len(after):])

    call = pl.pallas_call(
        wrapped, name=name, grid=grid, in_specs=list(in_specs) + [pl.BlockSpec(memory_space=pl.ANY)] * len(after),
        out_specs=out_specs, out_shape=out_shape, scratch_shapes=list(scratch),
        compiler_params=pltpu.CompilerParams(dimension_semantics=("arbitrary",) * len(grid),
                                             vmem_limit_bytes=V7X_VMEM_LIMIT))
    return lambda *operands: call(*operands, *after)


def _sds(shape, dtype):
    return jax.ShapeDtypeStruct(tuple(shape), dtype)


def _dot(a, b):
    return jnp.dot(a.astype(BF16), b.astype(BF16), preferred_element_type=F32)


def _dot_nt(a, b):
    return lax.dot_general(a.astype(BF16), b.astype(BF16), (((1,), (1,)), ((), ())), preferred_element_type=F32)


def _dot_tn(a, b):
    return lax.dot_general(a.astype(BF16), b.astype(BF16), (((0,), (0,)), ((), ())), preferred_element_type=F32)


def _rms(h, g):
    return h * lax.rsqrt(jnp.mean(h * h, axis=-1, keepdims=True) + EPS) * g


def _rms_bwd(h, g, dn):
    r = lax.rsqrt(jnp.mean(h * h, axis=-1, keepdims=True) + EPS)
    xh = h * r
    dxh = dn * g
    dg = jnp.sum(dn * xh, axis=0, keepdims=True)
    dh = r * (dxh - xh * jnp.mean(dxh * xh, axis=-1, keepdims=True))
    return dh, dg


def _sigmoid(x):
    return 0.5 * jnp.tanh(0.5 * x) + 0.5


def _gelu(y):
    k = math.sqrt(2.0 / math.pi)
    return 0.5 * y * (1.0 + jnp.tanh(k * (y + 0.044715 * y * y * y)))


def _gelu_grad(y):
    k = math.sqrt(2.0 / math.pi)
    t = jnp.tanh(k * (y + 0.044715 * y * y * y))
    return 0.5 * (1.0 + t) + 0.5 * y * (1.0 - t * t) * k * (1.0 + 3.0 * 0.044715 * y * y)


def _partner(x, lane, d):
    width = x.shape[-1]
    return jnp.where((lane & d) == 0, pltpu.roll(x, width - d, 1), pltpu.roll(x, d, 1))


def _split_bf16(x):
    hi = x.astype(BF16)
    return hi, (x - hi.astype(F32)).astype(BF16)


def _head_sums(x, sel):
    hi, lo = _split_bf16(x)
    return jnp.dot(hi, sel, preferred_element_type=F32) + jnp.dot(lo, sel, preferred_element_type=F32)


def _head_expand(v, sel_t):
    hi, lo = _split_bf16(v)
    return jnp.dot(hi, sel_t, preferred_element_type=F32) + jnp.dot(lo, sel_t, preferred_element_type=F32)


def _tile_lanes(t, width):
    return jnp.concatenate([t] * (width // t.shape[-1]), axis=1)


def _head_prep(x, gain_t, cos2, sin2, sel, sel_t, hd):
    width = x.shape[-1]
    lane = lax.broadcasted_iota(jnp.int32, x.shape, 1)
    r = _head_expand(lax.rsqrt(_head_sums(x * x, sel) * (1.0 / hd) + EPS), sel_t)
    y = x * r * gain_t
    return y * _tile_lanes(cos2, width) + _partner(y, lane, hd // 2) * _tile_lanes(sin2, width)


def _head_prep_bwd(x, gain_t, cos2, sin2, sel, sel_t, d_out, hd):
    width = x.shape[-1]
    lane = lax.broadcasted_iota(jnp.int32, x.shape, 1)
    r = _head_expand(lax.rsqrt(_head_sums(x * x, sel) * (1.0 / hd) + EPS), sel_t)
    xhat = x * r
    dy = d_out * _tile_lanes(cos2, width) + _partner(d_out * _tile_lanes(sin2, width), lane, hd // 2)
    dgain = jnp.sum(dy * xhat, axis=0, keepdims=True)
    dxh = dy * gain_t
    mean = _head_expand(_head_sums(dxh * xhat, sel) * (1.0 / hd), sel_t)
    return r * (dxh - xhat * mean), dgain


def _head_selectors(n_heads, hd):
    sel = (jnp.arange(n_heads * hd)[:, None] // hd == jnp.arange(128)[None, :]).astype(BF16)
    return sel, sel.T


def _acc_out(ref, val, first):
    @pl.when(first)
    def _():
        ref[...] = jnp.zeros_like(ref)
    ref[...] += val


def _ffn_up(name, h, g, w4, n_rows):
    nj, d, fc = w4.shape
    tm = ROW_TILE

    def body(h_ref, g_ref, w_ref, o_ref, n_ref):
        n = _rms(h_ref[...], g_ref[...]).astype(BF16)
        n_ref[...] = n
        for j in range(nj):
            o_ref[:, j * fc:(j + 1) * fc] = _dot(n, w_ref[j]).astype(BF16)

    return _call(name, body, (n_rows // tm,),
                 [pl.BlockSpec((tm, d), lambda i: (i, 0)), pl.BlockSpec((1, d), lambda i: (0, 0)),
                  pl.BlockSpec((nj, d, fc), lambda i: (0, 0, 0))],
                 [pl.BlockSpec((tm, nj * fc), lambda i: (i, 0)), pl.BlockSpec((tm, d), lambda i: (i, 0))],
                 [_sds((n_rows, nj * fc), BF16), _sds((n_rows, d), BF16)])(h, g, w4)


def _ffn_down(name, gu, h, wd, n_rows, target=None):
    f, d = wd.shape
    tm = ROW_TILE

    def body(gu_ref, h_ref, w_ref, *rest):
        half_a = gu_ref[:, :f] * 0.5
        s = (half_a + half_a * jnp.tanh(half_a)) * gu_ref[:, f:]
        y = h_ref[...] + 0.5 * _dot(s, w_ref[...])
        if target is None:
            o_ref, s_ref = rest
            o_ref[...] = y
        else:
            t_ref, dy_ref, l_ref, s_ref = rest
            e = y - t_ref[...]
            dy_ref[...] = e * (1.0 / d)
            e2 = jnp.sum((e * e).reshape(tm // SUBLANES, SUBLANES, d), axis=0)
            part = e2[:, 0:128]
            for k in range(1, d // 128):
                part = part + e2[:, k * 128:(k + 1) * 128]
            _acc_out(l_ref, part * (0.5 / d), pl.program_id(0) == 0)
        s_ref[...] = s

    row = lambda width: pl.BlockSpec((tm, width), lambda i: (i, 0))
    in_specs = [row(2 * f), row(d), pl.BlockSpec((f, d), lambda i: (0, 0))]
    if target is None:
        return _call(name, body, (n_rows // tm,), in_specs, [row(d), row(f)],
                     [_sds((n_rows, d), F32), _sds((n_rows, f), BF16)])(gu, h, wd)
    return _call(name, body, (n_rows // tm,), in_specs + [row(d)],
                 [row(d), pl.BlockSpec((SUBLANES, 128), lambda i: (0, 0)), row(f)],
                 [_sds((n_rows, d), F32), _sds((SUBLANES, 128), F32), _sds((n_rows, f), BF16)])(gu, h, wd, target)


def _ffn_dgu(name, dh, gu, wd, n_rows, after=()):
    f, d = wd.shape
    tm = ROW_TILE

    def body(dh_ref, gu_ref, w_ref, o_ref):
        ds = _dot_nt(0.5 * dh_ref[...], w_ref[...]).astype(BF16)
        half_a = gu_ref[:, :f] * 0.5
        t = jnp.tanh(half_a)
        o_ref[:, :f] = ds * gu_ref[:, f:] * ((1.0 + t + half_a * (1.0 - t * t)) * 0.5)
        o_ref[:, f:] = ds * (half_a + half_a * t)

    return _call(name, body, (n_rows // tm,),
                 [pl.BlockSpec((tm, d), lambda i: (i, 0)), pl.BlockSpec((tm, 2 * f), lambda i: (i, 0)),
                  pl.BlockSpec((f, d), lambda i: (0, 0))],
                 pl.BlockSpec((tm, 2 * f), lambda i: (i, 0)), _sds((n_rows, 2 * f), BF16), after=after)(dh, gu, wd)


def _ffn_dh(name, dgu, h, g, dh, w4, n_rows, n_main=None, after=()):
    nj, d, fc = w4.shape
    tm = ROW_TILE
    n_first = (n_rows if n_main is None else n_main) // tm

    def body(dgu_ref, h_ref, g_ref, dh_ref, w_ref, o_ref, *rest):
        dg_ref = rest[-1]
        i = pl.program_id(0)
        dn = _dot_nt(dgu_ref[:, 0:fc], w_ref[0])
        for j in range(1, nj):
            dn = dn + _dot_nt(dgu_ref[:, j * fc:(j + 1) * fc], w_ref[j])
        dhn, dg = _rms_bwd(h_ref[...], g_ref[...], dn)
        val = dh_ref[...] + dhn
        if n_main is None:
            o_ref[...] = val
        else:
            @pl.when(i < n_first)
            def _():
                o_ref[...] = val

            @pl.when(i >= n_first)
            def _():
                rest[0][...] = val
        _acc_out(dg_ref, dg, i == 0)

    out_specs = [pl.BlockSpec((tm, d), lambda i: (jnp.minimum(i, n_first - 1), 0))]
    out_shape = [_sds((n_first * tm, d), F32)]
    if n_main is not None:
        out_specs.append(pl.BlockSpec((tm, d), lambda i: (jnp.maximum(i - n_first, 0), 0)))
        out_shape.append(_sds((n_rows - n_main, d), F32))
    return _call(name, body, (n_rows // tm,),
                 [pl.BlockSpec((tm, nj * fc), lambda i: (i, 0)), pl.BlockSpec((tm, d), lambda i: (i, 0)),
                  pl.BlockSpec((1, d), lambda i: (0, 0)), pl.BlockSpec((tm, d), lambda i: (i, 0)),
                  pl.BlockSpec((nj, d, fc), lambda i: (0, 0, 0))],
                 out_specs + [pl.BlockSpec((1, d), lambda i: (0, 0))],
                 out_shape + [_sds((1, d), F32)], after=after)(dgu, h, g, dh, w4)


def _contract_tile(n_rows, cap=1024):
    best = ROW_TILE
    for t in range(ROW_TILE, cap + 1, ROW_TILE):
        if n_rows % t == 0:
            best = t
    return best


def _tn(name, operands, in_specs, prologue, nj, ma, nb, n_rows, tk, out_dtype=F32, after=()):
    n_k = n_rows // tk
    out_spec = pl.BlockSpec((1, ma, nb), lambda j, k: (j, 0, 0))
    if out_dtype == F32:
        def body(*refs):
            o_ref = refs[-1]
            a, b = prologue(pl.program_id(0), *refs[:-1])
            _acc_out(o_ref, _dot_tn(a, b)[None], pl.program_id(1) == 0)

        return _call(name, body, (nj, n_k), in_specs, out_spec, _sds((nj, ma, nb), F32), after=after)(*operands)

    def body_rounded(*refs):
        o_ref, acc_ref = refs[-2:]
        a, b = prologue(pl.program_id(0), *refs[:-2])
        _acc_out(acc_ref, _dot_tn(a, b), pl.program_id(1) == 0)

        @pl.when(pl.program_id(1) == n_k - 1)
        def _():
            o_ref[0] = acc_ref[...].astype(out_dtype)

    return _call(name, body_rounded, (nj, n_k), in_specs, out_spec, _sds((nj, ma, nb), out_dtype),
                 scratch=[pltpu.VMEM((ma, nb), F32)], after=after)(*operands)


def _ffn_dwgu(name, n, dgu, nj, n_rows):
    d = n.shape[1]
    fc = dgu.shape[1] // nj
    tk = _contract_tile(n_rows)
    return _tn(name, (dgu, n),
               [pl.BlockSpec((tk, fc), lambda j, k: (k, j)), pl.BlockSpec((tk, d), lambda j, k: (k, 0))],
               lambda j, a_ref, b_ref: (a_ref[...], b_ref[...]), nj, fc, d, n_rows, tk, out_dtype=BF16)


def _ffn_dwd(name, s, dh, n_rows):
    f = s.shape[1]
    d = dh.shape[1]
    tk = _contract_tile(n_rows)
    return _tn(name, (s, dh),
               [pl.BlockSpec((tk, f), lambda j, k: (k, 0)), pl.BlockSpec((tk, d), lambda j, k: (k, 0))],
               lambda j, s_ref, dh_ref: (s_ref[...], 0.5 * dh_ref[...]), 1, f, d, n_rows, tk, out_dtype=BF16)


def _ffn_fwd(tag, h, g, w4, wd, n_rows, target=None):
    gu, n = _ffn_up(tag + "_up", h, g, w4, n_rows)
    *out, s = _ffn_down(tag + "_down", gu, h, wd, n_rows, target)
    return (out[0] if target is None else tuple(out)), (gu, n, s)


def _ffn_bwd(tag, dh_out, h, g, saved, w4, wd, n_rows, n_main=None, after=()):
    gu, n, s = saved
    nj = w4.shape[0]
    f, d = wd.shape
    dgu = _ffn_dgu(tag + "_dgu", dh_out, gu, wd, n_rows, after=after)
    dwd = _ffn_dwd(tag + "_dwd", s, dh_out, n_rows).reshape(N_CHIPS, f // N_CHIPS, d)
    *dh_parts, dg = _ffn_dh(tag + "_dh", dgu, h, g, dh_out, w4, n_rows, n_main)
    dwgu = _ffn_dwgu(tag + "_dwgu", n, dgu, nj, n_rows)
    dh_in = dh_parts[0] if n_main is None else tuple(dh_parts)
    return dh_in, dg, dwgu, dwd


def _ssm_in(name, h, g, w_in, bb, n_rows):
    d, hw = w_in.shape
    nj, uc, xc = bb.shape
    tm = ROW_TILE

    def body(h_ref, g_ref, w_ref, bb_ref, u_ref, bu_ref):
        u = _dot(_rms(h_ref[...], g_ref[...]), w_ref[...])
        u_ref[...] = u
        for j in range(nj):
            bu_ref[:, j * xc:(j + 1) * xc] = _dot(u[:, j * uc:(j + 1) * uc], bb_ref[j]).astype(BF16)

    return _call(name, body, (n_rows // tm,),
                 [pl.BlockSpec((tm, d), lambda i: (i, 0)), pl.BlockSpec((1, d), lambda i: (0, 0)),
                  pl.BlockSpec((d, hw), lambda i: (0, 0)), pl.BlockSpec((nj, uc, xc), lambda i: (0, 0, 0))],
                 [pl.BlockSpec((tm, hw), lambda i: (i, 0)), pl.BlockSpec((tm, nj * xc), lambda i: (i, 0))],
                 [_sds((n_rows, hw), F32), _sds((n_rows, nj * xc), BF16)])(h, g, w_in, bb)


def _cmul_add(xr, xi, ar, ai, sr, si):
    return xr + ar * sr - ai * si, xi + ar * si + ai * sr


def _scan_row_block(n_main_blocks, seq_blocks):
    return lambda b, i: jnp.where(i == 0, n_main_blocks + b, b * seq_blocks + i - 1)


def _scan_fwd(name, bu, tabs, n_ex, seq):
    n_rows, width = bu.shape
    nj = 4
    cw = width // nj
    half = cw // 2
    tq = META_BLOCK
    seq_blocks = seq // tq
    rb = _scan_row_block(n_ex * seq_blocks, seq_blocks)

    def body(bu_ref, tab_ref, x_ref, carry_ref):
        @pl.when(pl.program_id(1) == 0)
        def _():
            carry_ref[...] = jnp.zeros_like(carry_ref)

        for j in range(nj):
            re, im = slice(j * cw, j * cw + half), slice(j * cw + half, (j + 1) * cw)
            ch = slice(j * half, (j + 1) * half)

            def blk(k, c, re=re, im=im, ch=ch):
                t = [tab_ref[n * SUBLANES:(n + 1) * SUBLANES, ch] for n in range(8)]
                r0 = pl.multiple_of(k * SUBLANES, SUBLANES)
                xr = bu_ref[pl.ds(r0, SUBLANES), re].astype(F32)
                xi = bu_ref[pl.ds(r0, SUBLANES), im].astype(F32)
                for s, d in enumerate((1, 2, 4)):
                    xr, xi = _cmul_add(xr, xi, t[2 * s], t[2 * s + 1], pltpu.roll(xr, d, 0), pltpu.roll(xi, d, 0))
                xr, xi = _cmul_add(xr, xi, t[6], t[7], c[0], c[1])
                x_ref[pl.ds(r0, SUBLANES), re] = xr.astype(BF16)
                x_ref[pl.ds(r0, SUBLANES), im] = xi.astype(BF16)
                last = SUBLANES - 1
                return (jnp.broadcast_to(xr[last:last + 1, :], xr.shape), jnp.broadcast_to(xi[last:last + 1, :], xi.shape))

            c = lax.fori_loop(0, tq // SUBLANES, blk, (carry_ref[0, :, ch], carry_ref[1, :, ch]), unroll=2)
            carry_ref[0, :, ch] = c[0]
            carry_ref[1, :, ch] = c[1]

    return _call(name, body, (n_ex, seq_blocks + 1),
                 [pl.BlockSpec((tq, width), lambda b, i: (rb(b, i), 0)), pl.BlockSpec((8 * SUBLANES, nj * half), lambda b, i: (0, 0))],
                 pl.BlockSpec((tq, width), lambda b, i: (rb(b, i), 0)), _sds((n_rows, width), BF16),
                 scratch=[pltpu.VMEM((2, SUBLANES, nj * half), F32)])(bu, tabs)


def _scan_bwd(name, gx, x, tabs, n_ex, seq, after=()):
    n_rows, width = gx.shape
    nj = 4
    cw = width // nj
    half = cw // 2
    tq = META_BLOCK
    seq_blocks = seq // tq
    n_steps = seq_blocks + 1
    rb = _scan_row_block(n_ex * seq_blocks, seq_blocks)
    rbr = lambda b, i: rb(b, n_steps - 1 - i)

    def body(gx_ref, x_ref, tab_ref, g_ref, da_ref, carry_ref):
        @pl.when(pl.program_id(1) == 0)
        def _():
            carry_ref[...] = jnp.zeros_like(carry_ref)
            da_ref[...] = jnp.zeros_like(da_ref)
        row = lax.broadcasted_iota(jnp.int32, (SUBLANES, half), 0)
        n_blk = tq // SUBLANES

        for j in range(nj):
            re, im = slice(j * cw, j * cw + half), slice(j * cw + half, (j + 1) * cw)
            ch = slice(j * half, (j + 1) * half)

            def blk(kk, st, re=re, im=im, ch=ch):
                t = [tab_ref[n * SUBLANES:(n + 1) * SUBLANES, ch] for n in range(8)]
                cr, ci, dar, dai = st
                r0 = pl.multiple_of((n_blk - 1 - kk) * SUBLANES, SUBLANES)
                gr = gx_ref[pl.ds(r0, SUBLANES), re].astype(F32)
                gi = gx_ref[pl.ds(r0, SUBLANES), im].astype(F32)
                for s, d in enumerate((1, 2, 4)):
                    gr, gi = _cmul_add(gr, gi, t[2 * s], t[2 * s + 1],
                                       pltpu.roll(gr, SUBLANES - d, 0), pltpu.roll(gi, SUBLANES - d, 0))
                gr, gi = _cmul_add(gr, gi, t[6], t[7], cr, ci)
                g_ref[pl.ds(r0, SUBLANES), re] = gr.astype(BF16)
                g_ref[pl.ds(r0, SUBLANES), im] = gi.astype(BF16)
                hr = jnp.where(row == SUBLANES - 1, cr, pltpu.roll(gr, SUBLANES - 1, 0))
                hi = jnp.where(row == SUBLANES - 1, ci, pltpu.roll(gi, SUBLANES - 1, 0))
                xr = x_ref[pl.ds(r0, SUBLANES), re].astype(F32)
                xi = x_ref[pl.ds(r0, SUBLANES), im].astype(F32)
                dar = dar + xr * hr + xi * hi
                dai = dai + xr * hi - xi * hr
                return (jnp.broadcast_to(gr[0:1, :], gr.shape), jnp.broadcast_to(gi[0:1, :], gi.shape), dar, dai)

            st = lax.fori_loop(0, n_blk, blk, (carry_ref[0, :, ch], carry_ref[1, :, ch], da_ref[0, :, re], da_ref[0, :, im]),
                               unroll=2)
            carry_ref[0, :, ch] = st[0]
            carry_ref[1, :, ch] = st[1]
            da_ref[0, :, re] = st[2]
            da_ref[0, :, im] = st[3]

    return _call(name, body, (n_ex, n_steps),
                 [pl.BlockSpec((tq, width), lambda b, i: (rbr(b, i), 0)), pl.BlockSpec((tq, width), lambda b, i: (rbr(b, i), 0)),
                  pl.BlockSpec((8 * SUBLANES, nj * half), lambda b, i: (0, 0))],
                 [pl.BlockSpec((tq, width), lambda b, i: (rbr(b, i), 0)), pl.BlockSpec((1, SUBLANES, width), lambda b, i: (b, 0, 0))],
                 [_sds((n_rows, width), BF16), _sds((n_ex, SUBLANES, width), F32)],
                 scratch=[pltpu.VMEM((2, SUBLANES, nj * half), F32)], after=after)(gx, x, tabs)


def _ssm_z(gy, wout_ref, nj):
    return jnp.concatenate([_dot(gy, wout_ref[j]) for j in range(nj)], axis=1)


def _ssm_out(name, x, u, dskip, cb, wout4, h, n_rows):
    nj, xc, uc = cb.shape
    no, hw, oc = wout4.shape
    d = h.shape[1]
    tm = ROW_TILE

    def body(x_ref, u_ref, ds_ref, cb_ref, w_ref, h_ref, o_ref, y_ref):
        y = jnp.concatenate([_dot(x_ref[:, j * xc:(j + 1) * xc], cb_ref[j]) for j in range(nj)], axis=1)
        y = y + ds_ref[...] * u_ref[...]
        y_ref[...] = y
        z = _ssm_z(_gelu(y), w_ref, no)
        o_ref[...] = h_ref[...] + z[:, :d] * _sigmoid(z[:, d:])

    return _call(name, body, (n_rows // tm,),
                 [pl.BlockSpec((tm, nj * xc), lambda i: (i, 0)), pl.BlockSpec((tm, hw), lambda i: (i, 0)),
                  pl.BlockSpec((1, hw), lambda i: (0, 0)), pl.BlockSpec((nj, xc, uc), lambda i: (0, 0, 0)),
                  pl.BlockSpec((no, hw, oc), lambda i: (0, 0, 0)), pl.BlockSpec((tm, d), lambda i: (i, 0))],
                 [pl.BlockSpec((tm, d), lambda i: (i, 0)), pl.BlockSpec((tm, hw), lambda i: (i, 0))],
                 [_sds((n_rows, d), F32), _sds((n_rows, hw), F32)])(x, u, dskip, cb, wout4, h)


def _ssm_out_bwd(name, dh, y, u, cb, wout4, n_rows, after=()):
    nj, xc, uc = cb.shape
    no, hw, oc = wout4.shape
    d = dh.shape[1]
    tm = ROW_TILE

    def body(dh_ref, y_ref, u_ref, cb_ref, w_ref, dy_ref, dz_ref, gx_ref, dd_ref):
        y = y_ref[...]
        z = _ssm_z(_gelu(y), w_ref, no)
        za = z[:, :d]
        sg = _sigmoid(z[:, d:])
        dmix = dh_ref[...]
        dz = jnp.concatenate([dmix * sg, dmix * za * sg * (1.0 - sg)], axis=1).astype(BF16)
        dz_ref[...] = dz
        dgy = _dot_nt(dz[:, 0:oc], w_ref[0])
        for j in range(1, no):
            dgy = dgy + _dot_nt(dz[:, j * oc:(j + 1) * oc], w_ref[j])
        dy = dgy * _gelu_grad(y)
        dy_ref[...] = dy
        _acc_out(dd_ref, jnp.sum(dy * u_ref[...], axis=0, keepdims=True), pl.program_id(0) == 0)
        for j in range(nj):
            gx_ref[:, j * xc:(j + 1) * xc] = _dot_nt(dy[:, j * uc:(j + 1) * uc], cb_ref[j]).astype(BF16)

    return _call(name, body, (n_rows // tm,),
                 [pl.BlockSpec((tm, d), lambda i: (i, 0)), pl.BlockSpec((tm, hw), lambda i: (i, 0)),
                  pl.BlockSpec((tm, hw), lambda i: (i, 0)), pl.BlockSpec((nj, xc, uc), lambda i: (0, 0, 0)),
                  pl.BlockSpec((no, hw, oc), lambda i: (0, 0, 0))],
                 [pl.BlockSpec((tm, hw), lambda i: (i, 0)), pl.BlockSpec((tm, no * oc), lambda i: (i, 0)),
                  pl.BlockSpec((tm, nj * xc), lambda i: (i, 0)), pl.BlockSpec((1, hw), lambda i: (0, 0))],
                 [_sds((n_rows, hw), F32), _sds((n_rows, no * oc), BF16), _sds((n_rows, nj * xc), BF16),
                  _sds((1, hw), F32)], after=after)(dh, y, u, cb, wout4)


def _ssm_in_bwd(name, gbu, dy, dskip, bb, w_in, h, g, dh, n_rows):
    nj, uc, xc = bb.shape
    d, hw = w_in.shape
    tm = ROW_TILE

    def body(gb_ref, dy_ref, ds_ref, bb_ref, w_ref, h_ref, g_ref, dh_ref, du_ref, o_ref, dg_ref):
        du = jnp.concatenate([_dot_nt(gb_ref[:, j * xc:(j + 1) * xc], bb_ref[j]) for j in range(nj)], axis=1)
        du = du + dy_ref[...] * ds_ref[...]
        du_ref[...] = du.astype(BF16)
        dhn, dg = _rms_bwd(h_ref[...], g_ref[...], _dot_nt(du, w_ref[...]))
        o_ref[...] = dh_ref[...] + dhn
        _acc_out(dg_ref, dg, pl.program_id(0) == 0)

    return _call(name, body, (n_rows // tm,),
                 [pl.BlockSpec((tm, nj * xc), lambda i: (i, 0)), pl.BlockSpec((tm, hw), lambda i: (i, 0)),
                  pl.BlockSpec((1, hw), lambda i: (0, 0)), pl.BlockSpec((nj, uc, xc), lambda i: (0, 0, 0)),
                  pl.BlockSpec((d, hw), lambda i: (0, 0)), pl.BlockSpec((tm, d), lambda i: (i, 0)),
                  pl.BlockSpec((1, d), lambda i: (0, 0)), pl.BlockSpec((tm, d), lambda i: (i, 0))],
                 [pl.BlockSpec((tm, hw), lambda i: (i, 0)), pl.BlockSpec((tm, d), lambda i: (i, 0)),
                  pl.BlockSpec((1, d), lambda i: (0, 0))],
                 [_sds((n_rows, hw), BF16), _sds((n_rows, d), F32), _sds((1, d), F32)])(gbu, dy, dskip, bb, w_in, h, g, dh)


def _discretize(lam_re, lam_im, log_step, b_re, b_im):
    step = jnp.exp(log_step)[:, None]
    mag = jnp.exp(lam_re * step)
    ar = mag * jnp.cos(lam_im * step)
    ai = mag * jnp.sin(lam_im * step)
    den = lam_re * lam_re + lam_im * lam_im
    nr, ni = ar - 1.0, ai
    cr = (nr * lam_re + ni * lam_im) / den
    ci = (ni * lam_re - nr * lam_im) / den
    bbar_r = cr[..., None] * b_re - ci[..., None] * b_im
    bbar_i = cr[..., None] * b_im + ci[..., None] * b_re
    return ar, ai, bbar_r, bbar_i


def _ssm_mats(lam_re, lam_im, log_step, b_re, b_im, c_re, c_im):
    n_g, n_p, n_c = b_re.shape
    gpc = n_g // 4
    ar, ai, bbar_r, bbar_i = _discretize(lam_re, lam_im, log_step, b_re, b_im)
    eye = jnp.eye(gpc, dtype=F32)

    def in_map(bbar):
        return jnp.einsum('jgpc,gh->jgchp', bbar.reshape(4, gpc, n_p, n_c), eye).reshape(4, gpc * n_c, gpc * n_p)

    def out_map(c):
        return jnp.einsum('jgcp,gh->jgphc', c.reshape(4, gpc, n_c, n_p), eye).reshape(4, gpc * n_p, gpc * n_c)

    bb = jnp.concatenate([in_map(bbar_r), in_map(bbar_i)], axis=2)
    cb = jnp.concatenate([out_map(c_re), -out_map(c_im)], axis=1)
    return bb, cb, ar.reshape(-1), ai.reshape(-1)


def _chunked(v, half):
    return v.reshape(v.shape[:-1] + (4, half))


def _scan_tables(ar, ai, reverse):
    if reverse:
        ai = -ai
    pr, pi = [ar], [ai]
    for _ in range(SUBLANES - 1):
        pr, pi = pr + [pr[-1] * ar - pi[-1] * ai], pi + [pr[-1] * ai + pi[-1] * ar]
    row = jnp.arange(SUBLANES)[:, None]
    tabs = []
    for d in (1, 2, 4):
        keep = (row <= SUBLANES - 1 - d) if reverse else (row >= d)
        tabs += [jnp.where(keep, pr[d - 1][None, :], 0.0), jnp.where(keep, pi[d - 1][None, :], 0.0)]
    order = list(range(SUBLANES))[::-1] if reverse else list(range(SUBLANES))
    tabs += [jnp.stack([pr[k] for k in order]), jnp.stack([pi[k] for k in order])]
    return jnp.concatenate(tabs, axis=0)


def _kv_proj(name, h, g, w_kv, k_gain_t, cos2, sin2, n_rows, n_kv, hd):
    d, kvw = w_kv.shape
    kw = n_kv * hd
    tm = ROW_TILE

    sel, sel_t = _head_selectors(n_kv, hd)

    def body(h_ref, g_ref, w_ref, kg_ref, c_ref, s_ref, e_ref, et_ref, raw_ref, k_ref, v_ref):
        raw = _dot(_rms(h_ref[...], g_ref[...]), w_ref[...])
        raw_ref[...] = raw
        k_ref[...] = _head_prep(raw[:, :kw], kg_ref[...], c_ref[...], s_ref[...], e_ref[...], et_ref[...], hd).astype(BF16)
        v_ref[...] = raw[:, kw:].astype(BF16)

    return _call(name, body, (n_rows // tm,),
                 [pl.BlockSpec((tm, d), lambda i: (i, 0)), pl.BlockSpec((1, d), lambda i: (0, 0)),
                  pl.BlockSpec((d, kvw), lambda i: (0, 0)), pl.BlockSpec((1, kw), lambda i: (0, 0)),
                  pl.BlockSpec((tm, 2 * hd), lambda i: (i, 0)), pl.BlockSpec((tm, 2 * hd), lambda i: (i, 0)),
                  pl.BlockSpec(sel.shape, lambda i: (0, 0)), pl.BlockSpec(sel_t.shape, lambda i: (0, 0))],
                 [pl.BlockSpec((tm, kvw), lambda i: (i, 0)), pl.BlockSpec((tm, kw), lambda i: (i, 0)),
                  pl.BlockSpec((tm, kw), lambda i: (i, 0))],
                 [_sds((n_rows, kvw), F32), _sds((n_rows, kw), BF16), _sds((n_rows, kw), BF16)])(
                     h, g, w_kv, k_gain_t, cos2, sin2, sel, sel_t)


def _q_proj(name, h, g, w_q, q_gain_t, cos2, sin2, n_rows, n_q, hd):
    d, qw = w_q.shape
    tm = ROW_TILE

    sel, sel_t = _head_selectors(n_q, hd)

    def body(h_ref, g_ref, w_ref, qg_ref, c_ref, s_ref, e_ref, et_ref, raw_ref, q_ref):
        raw = _dot(_rms(h_ref[...], g_ref[...]), w_ref[...])
        raw_ref[...] = raw
        q_ref[...] = _head_prep(raw, qg_ref[...], c_ref[...], s_ref[...], e_ref[...], et_ref[...], hd).astype(BF16)

    return _call(name, body, (n_rows // tm,),
                 [pl.BlockSpec((tm, d), lambda i: (i, 0)), pl.BlockSpec((1, d), lambda i: (0, 0)),
                  pl.BlockSpec((d, qw), lambda i: (0, 0)), pl.BlockSpec((1, qw), lambda i: (0, 0)),
                  pl.BlockSpec((tm, 2 * hd), lambda i: (i, 0)), pl.BlockSpec((tm, 2 * hd), lambda i: (i, 0)),
                  pl.BlockSpec(sel.shape, lambda i: (0, 0)), pl.BlockSpec(sel_t.shape, lambda i: (0, 0))],
                 [pl.BlockSpec((tm, qw), lambda i: (i, 0)), pl.BlockSpec((tm, qw), lambda i: (i, 0))],
                 [_sds((n_rows, qw), F32), _sds((n_rows, qw), BF16)])(h, g, w_q, q_gain_t, cos2, sin2, sel, sel_t)


def _attn_specs(seq, n_ex, n_meta, kw):
    nb = seq // WINDOW
    meta_blk = lambda b: (n_ex * seq + META_BLOCK * b + META_BLOCK - n_meta) // n_meta
    return [pl.BlockSpec((WINDOW, kw), lambda b, n: (b * nb + jnp.maximum(n - 1, 0), 0)),
            pl.BlockSpec((WINDOW, kw), lambda b, n: (b * nb + n, 0)),
            pl.BlockSpec((n_meta, kw), lambda b, n: (meta_blk(b), 0))]


def _attn_bias(qpk, n_keys):
    rows = qpk * WINDOW
    qi = jnp.arange(rows)[:, None] & (WINDOW - 1)
    kj = jnp.arange(n_keys)[None, :]
    rel = qi + WINDOW - kj
    band = (rel >= 0) & (rel < WINDOW)
    meta = kj >= 2 * WINDOW
    first = (band & (kj >= WINDOW)) | meta
    return jnp.where(jnp.stack([first, band | meta]), 0.0, NEG_INF).astype(F32)


def _stack_heads(ref, h, qpk, hd, dtype=None):
    parts = [ref[:, (h * qpk + gq) * hd:(h * qpk + gq + 1) * hd] for gq in range(qpk)]
    out = jnp.concatenate(parts, axis=0)
    return out if dtype is None else out.astype(dtype)


def _col(tile, c):
    lane = lax.broadcasted_iota(jnp.int32, tile.shape, 1)
    return jnp.sum(jnp.where(lane == c, tile, 0.0), axis=-1, keepdims=True)


def _put_col(col, c, n):
    lane = lax.broadcasted_iota(jnp.int32, (col.shape[0], n), 1)
    return jnp.where(lane == c, col, 0.0)


def _stack_cols(tile, h, qpk):
    return jnp.concatenate([_col(tile, h * qpk + gq) for gq in range(qpk)], axis=0)


def _sink_col(sinks, h, qpk):
    return jnp.concatenate([jnp.broadcast_to(_col(sinks, h * qpk + gq), (WINDOW, 1)) for gq in range(qpk)], axis=0)


def _attn_fwd(name, q, k, v, sinks, n_ex, seq, n_meta, n_kv, qpk, hd):
    nb = seq // WINDOW
    n_q = n_kv * qpk
    kw = n_kv * hd
    qw = n_q * hd
    n_keys = 2 * WINDOW + n_meta
    bias = _attn_bias(qpk, n_keys)

    def body(q_ref, kp_ref, kc_ref, km_ref, vp_ref, vc_ref, vm_ref, sk_ref, bias_ref, o_ref, lse_ref):
        sinks_v = sk_ref[...]
        o_parts = []
        lse_all = jnp.zeros((WINDOW, n_q), F32)
        for h in range(n_kv):
            hs = slice(h * hd, (h + 1) * hd)
            kb = jnp.concatenate([kp_ref[:, hs], kc_ref[:, hs], km_ref[:, hs]], axis=0)
            vb = jnp.concatenate([vp_ref[:, hs], vc_ref[:, hs], vm_ref[:, hs]], axis=0)
            s = _dot_nt(_stack_heads(q_ref, h, qpk, hd), kb) + bias_ref[0]
            skc = _sink_col(sinks_v, h, qpk)
            m = jnp.maximum(jnp.max(s, axis=-1, keepdims=True), skc)
            p = jnp.exp(s - m)
            den = jnp.sum(p, axis=-1, keepdims=True) + jnp.exp(skc - m)
            o = _dot(p, vb) / den
            lse = m + jnp.log(den)
            for gq in range(qpk):
                o_parts.append(o[gq * WINDOW:(gq + 1) * WINDOW])
                lse_all = lse_all + _put_col(lse[gq * WINDOW:(gq + 1) * WINDOW], h * qpk + gq, n_q)
        o_ref[...] = jnp.concatenate(o_parts, axis=1)
        lse_ref[...] = lse_all

    qspec = pl.BlockSpec((WINDOW, qw), lambda b, n: (b * nb + n, 0))
    return _call(name, body, (n_ex, nb),
                 [qspec] + _attn_specs(seq, n_ex, n_meta, kw) + _attn_specs(seq, n_ex, n_meta, kw)
                 + [pl.BlockSpec((1, n_q), lambda b, n: (0, 0)),
                    pl.BlockSpec((1,) + bias.shape[1:], lambda b, n: (jnp.minimum(n, 1), 0, 0))],
                 [qspec, pl.BlockSpec((WINDOW, n_q), lambda b, n: (b * nb + n, 0))],
                 [_sds((n_ex * seq, qw), F32), _sds((n_ex * seq, n_q), F32)])(q, k, k, k, v, v, v, sinks, bias)


def _attn_bwd(name, q, k, v, sinks, o, lse, do, n_ex, seq, n_meta, n_kv, qpk, hd):
    nb = seq // WINDOW
    n_q = n_kv * qpk
    kw = n_kv * hd
    qw = n_q * hd
    n_keys = 2 * WINDOW + n_meta
    bias = _attn_bias(qpk, n_keys)

    def body(q_ref, kp_ref, kc_ref, km_ref, vp_ref, vc_ref, vm_ref, sk_ref, o_ref, lse_ref, do_ref, bias_ref,
             dq_ref, dk_ref, dv_ref, dkm_ref, dvm_ref, dsk_ref):
        n = pl.program_id(1)

        @pl.when(n == 0)
        def _():
            dk_ref[...] = jnp.zeros_like(dk_ref)
            dv_ref[...] = jnp.zeros_like(dv_ref)
            dkm_ref[...] = jnp.zeros_like(dkm_ref)
            dvm_ref[...] = jnp.zeros_like(dvm_ref)

        @pl.when((n == 0) & (pl.program_id(0) == 0))
        def _():
            dsk_ref[...] = jnp.zeros_like(dsk_ref)

        sinks_v = sk_ref[...]
        lse_v = lse_ref[...]
        dq_parts, dk_parts, dv_parts = [], [], []
        dsk = jnp.zeros((1, n_q), F32)
        for h in range(n_kv):
            hs = slice(h * hd, (h + 1) * hd)
            kb = jnp.concatenate([kp_ref[:, hs], kc_ref[:, hs], km_ref[:, hs]], axis=0)
            vb = jnp.concatenate([vp_ref[:, hs], vc_ref[:, hs], vm_ref[:, hs]], axis=0)
            qs = _stack_heads(q_ref, h, qpk, hd)
            dos = _stack_heads(do_ref, h, qpk, hd)
            delta = jnp.sum(dos * _stack_heads(o_ref, h, qpk, hd), axis=-1, keepdims=True)
            lse_c = _stack_cols(lse_v, h, qpk)
            p = jnp.exp(_dot_nt(qs, kb) + bias_ref[0] - lse_c)
            ds = p * (_dot_nt(dos, vb) - delta)
            dqs = _dot(ds, kb)
            dk_parts.append(_dot_tn(ds, qs))
            dv_parts.append(_dot_tn(p, dos))
            dsink = -jnp.exp(_sink_col(sinks_v, h, qpk) - lse_c) * delta
            for gq in range(qpk):
                dq_parts.append(dqs[gq * WINDOW:(gq + 1) * WINDOW])
                dsk = dsk + _put_col(jnp.sum(dsink[gq * WINDOW:(gq + 1) * WINDOW], axis=0, keepdims=True), h * qpk + gq, n_q)
        dq_ref[...] = jnp.concatenate(dq_parts, axis=1)
        dsk_ref[...] += dsk
        dkb = jnp.concatenate(dk_parts, axis=1)
        dvb = jnp.concatenate(dv_parts, axis=1)
        prev = pl.ds(pl.multiple_of(jnp.maximum(n - 1, 0) * WINDOW, WINDOW), WINDOW)
        cur = pl.ds(pl.multiple_of(n * WINDOW, WINDOW), WINDOW)
        dk_ref[prev, :] += dkb[0:WINDOW]
        dv_ref[prev, :] += dvb[0:WINDOW]
        dk_ref[cur, :] += dkb[WINDOW:2 * WINDOW]
        dv_ref[cur, :] += dvb[WINDOW:2 * WINDOW]
        dkm_ref[...] += dkb[2 * WINDOW:]
        dvm_ref[...] += dvb[2 * WINDOW:]

    qspec = pl.BlockSpec((WINDOW, qw), lambda b, n: (b * nb + n, 0))
    exspec = pl.BlockSpec((seq, kw), lambda b, n: (b, 0))
    mspec = pl.BlockSpec((n_meta, kw), lambda b, n: (b, 0))
    return _call(name, body, (n_ex, nb),
                 [qspec] + _attn_specs(seq, n_ex, n_meta, kw) + _attn_specs(seq, n_ex, n_meta, kw)
                 + [pl.BlockSpec((1, n_q), lambda b, n: (0, 0)), qspec,
                    pl.BlockSpec((WINDOW, n_q), lambda b, n: (b * nb + n, 0)), qspec,
                    pl.BlockSpec((1,) + bias.shape[1:], lambda b, n: (jnp.minimum(n, 1), 0, 0))],
                 [qspec, exspec, exspec, mspec, mspec, pl.BlockSpec((1, n_q), lambda b, n: (0, 0))],
                 [_sds((n_ex * seq, qw), F32), _sds((n_ex * seq, kw), F32), _sds((n_ex * seq, kw), F32),
                  _sds((n_ex * n_meta, kw), F32), _sds((n_ex * n_meta, kw), F32), _sds((1, n_q), F32)])(
                      q, k, k, k, v, v, v, sinks, o, lse, do, bias)


def _attn_out(name, o, h, w_o, n_rows):
    qw, d = w_o.shape
    tm = ROW_TILE

    def body(o_ref, h_ref, w_ref, out_ref):
        out_ref[...] = h_ref[...] + _dot(o_ref[...], w_ref[...])

    return _call(name, body, (n_rows // tm,),
                 [pl.BlockSpec((tm, qw), lambda i: (i, 0)), pl.BlockSpec((tm, d), lambda i: (i, 0)),
                  pl.BlockSpec((qw, d), lambda i: (0, 0))],
                 pl.BlockSpec((tm, d), lambda i: (i, 0)), _sds((n_rows, d), F32))(o, h, w_o)


def _attn_out_bwd(name, dh, w_o, n_rows):
    qw, d = w_o.shape
    tm = ROW_TILE

    def body(dh_ref, w_ref, do_ref):
        do_ref[...] = _dot_nt(dh_ref[...], w_ref[...])

    return _call(name, body, (n_rows // tm,),
                 [pl.BlockSpec((tm, d), lambda i: (i, 0)), pl.BlockSpec((qw, d), lambda i: (0, 0))],
                 pl.BlockSpec((tm, qw), lambda i: (i, 0)), _sds((n_rows, qw), F32))(dh, w_o)


def _q_bwd(name, dq, qraw, q_gain_t, cos2, sin2, w_q, h, g, dh, n_rows, n_q, hd):
    d, qw = w_q.shape
    tm = ROW_TILE

    sel, sel_t = _head_selectors(n_q, hd)

    def body(dq_ref, raw_ref, qg_ref, c_ref, s_ref, e_ref, et_ref, w_ref, h_ref, g_ref, dh_ref, draw_ref, o_ref, dqg_ref, dg_ref):
        dx, dgain = _head_prep_bwd(raw_ref[...], qg_ref[...], c_ref[...], s_ref[...], e_ref[...], et_ref[...], dq_ref[...], hd)
        draw = dx.astype(BF16)
        draw_ref[...] = draw
        dhn, dg = _rms_bwd(h_ref[...], g_ref[...], _dot_nt(draw, w_ref[...]))
        o_ref[...] = dh_ref[...] + dhn
        first = pl.program_id(0) == 0
        _acc_out(dqg_ref, dgain, first)
        _acc_out(dg_ref, dg, first)

    row = lambda w: pl.BlockSpec((tm, w), lambda i: (i, 0))
    one = lambda w: pl.BlockSpec((1, w), lambda i: (0, 0))
    return _call(name, body, (n_rows // tm,),
                 [row(qw), row(qw), one(qw), row(2 * hd), row(2 * hd), pl.BlockSpec(sel.shape, lambda i: (0, 0)),
                  pl.BlockSpec(sel_t.shape, lambda i: (0, 0)), pl.BlockSpec((d, qw), lambda i: (0, 0)), row(d), one(d), row(d)],
                 [row(qw), row(d), one(qw), one(d)],
                 [_sds((n_rows, qw), BF16), _sds((n_rows, d), F32), _sds((1, qw), F32), _sds((1, d), F32)])(
                     dq, qraw, q_gain_t, cos2, sin2, sel, sel_t, w_q, h, g, dh)


def _kv_bwd(name, dk, dv, kvraw, k_gain_t, cos2, sin2, w_kv, h, g, dh_main, n_rows, n_main, n_kv, hd, after=()):
    d, kvw = w_kv.shape
    kw = n_kv * hd
    tm = ROW_TILE
    n_main_tiles = n_main // tm

    sel, sel_t = _head_selectors(n_kv, hd)

    def body(dk_ref, dv_ref, raw_ref, kg_ref, c_ref, s_ref, e_ref, et_ref, w_ref, h_ref, g_ref, dh_ref, draw_ref, o_ref, dkg_ref,
             dg_ref):
        i = pl.program_id(0)
        dx, dgain = _head_prep_bwd(raw_ref[:, :kw], kg_ref[...], c_ref[...], s_ref[...], e_ref[...], et_ref[...], dk_ref[...], hd)
        draw = jnp.concatenate([dx, dv_ref[...]], axis=1).astype(BF16)
        draw_ref[...] = draw
        dhn, dg = _rms_bwd(h_ref[...], g_ref[...], _dot_nt(draw, w_ref[...]))
        o_ref[...] = jnp.where(i < n_main_tiles, dh_ref[...], 0.0) + dhn
        _acc_out(dkg_ref, dgain, i == 0)
        _acc_out(dg_ref, dg, i == 0)

    row = lambda w: pl.BlockSpec((tm, w), lambda i: (i, 0))
    one = lambda w: pl.BlockSpec((1, w), lambda i: (0, 0))
    return _call(name, body, (n_rows // tm,),
                 [row(kw), row(kw), row(kvw), one(kw), row(2 * hd), row(2 * hd), pl.BlockSpec(sel.shape, lambda i: (0, 0)),
                  pl.BlockSpec(sel_t.shape, lambda i: (0, 0)), pl.BlockSpec((d, kvw), lambda i: (0, 0)), row(d),
                  one(d), pl.BlockSpec((tm, d), lambda i: (jnp.minimum(i, n_main_tiles - 1), 0))],
                 [row(kvw), row(d), one(kw), one(d)],
                 [_sds((n_rows, kvw), BF16), _sds((n_rows, d), F32), _sds((1, kw), F32), _sds((1, d), F32)], after=after)(
                     dk, dv, kvraw, k_gain_t, cos2, sin2, sel, sel_t, w_kv, h, g, dh_main)


def _tn_rms(name, h, g, b, n_rows, out_dtype=F32):
    d = h.shape[1]
    nb = b.shape[1]
    tk = _contract_tile(n_rows)
    return _tn(name, (h, g, b),
               [pl.BlockSpec((tk, d), lambda j, k: (k, 0)), pl.BlockSpec((1, d), lambda j, k: (0, 0)),
                pl.BlockSpec((tk, nb), lambda j, k: (k, 0))],
               lambda j, h_ref, g_ref, b_ref: (_rms(h_ref[...], g_ref[...]), b_ref[...]), 1, d, nb, n_rows, tk, out_dtype=out_dtype)


def _tn_plain(name, a, b, nj, a_cols, b_cols, n_rows, a_fn=None, out_dtype=F32, after=()):
    tk = _contract_tile(n_rows)
    fa = (lambda v: v) if a_fn is None else a_fn
    a_map = (lambda j, k: (k, j)) if a.shape[1] != a_cols else (lambda j, k: (k, 0))
    b_map = (lambda j, k: (k, j)) if b.shape[1] != b_cols else (lambda j, k: (k, 0))
    return _tn(name, (a, b), [pl.BlockSpec((tk, a_cols), a_map), pl.BlockSpec((tk, b_cols), b_map)],
               lambda j, a_ref, b_ref: (fa(a_ref[...]), b_ref[...]), nj, a_cols, b_cols, n_rows, tk, out_dtype=out_dtype,
               after=after)


def _cast_layer(name, a, layer):
    _, r, c = a.shape
    tr = _row_tile(r, 256)

    def body(a_ref, o_ref):
        o_ref[...] = a_ref[0].astype(BF16)

    return _call(name, body, (r // tr,), [pl.BlockSpec((1, tr, c), lambda i: (layer, i, 0))],
                 pl.BlockSpec((tr, c), lambda i: (i, 0)), _sds((r, c), BF16))(a)


def _adamw_math(w, g, m, v):
    c1 = 1.0 - ADAM_B1 ** ADAM_STEP
    c2 = 1.0 - ADAM_B2 ** ADAM_STEP
    nm = ADAM_B1 * m + (1.0 - ADAM_B1) * g
    nv = ADAM_B2 * v + (1.0 - ADAM_B2) * (g * g)
    return -ADAM_LR * ((nm / c1) / (jnp.sqrt(nv / c2) + ADAM_EPS) + ADAM_WD * w), nm, nv


def _adamw(name, w, g, m, v, after=()):
    rows, cols = w.shape
    tr = 128 if rows % 128 == 0 else rows

    def body(w_ref, g_ref, m_ref, v_ref, d_ref, nm_ref, nv_ref):
        d_ref[...], nm_ref[...], nv_ref[...] = _adamw_math(w_ref[...], g_ref[...], m_ref[...], v_ref[...])

    spec = pl.BlockSpec((tr, cols), lambda i: (i, 0))
    return _call(name, body, (rows // tr,), [spec] * 4, [spec] * 3, [_sds((rows, cols), F32)] * 3, after=after)(w, g, m, v)


def _adamw_from_halves(name, w, m, v, sources, half_index, transposed, after=()):
    n_layers, r, c = w.shape
    lanes = 1024
    after = tuple(after)
    if transposed:
        rows_half, tr = c // 2, 128
        grid = (n_layers, r // tr)
        w_spec = pl.BlockSpec((1, tr, c), lambda l, i, s: (l, i, 0))
        g_spec = lambda off: pl.BlockSpec((rows_half, tr), lambda l, i, s: (off // rows_half, i))
    else:
        rows_half = r // 2
        grid = (n_layers, 2)
        w_spec = pl.BlockSpec((1, rows_half, c), lambda l, k, s: (l, k, 0))
        g_spec = lambda off: pl.BlockSpec((rows_half, lanes), lambda l, k, s: (off // rows_half, 0))

    def body(s_ref, w_ref, m_ref, v_ref, t0_ref, o0_ref, t1_ref, o1_ref, *rest):
        g_ref, d_ref, nm_ref, nv_ref = rest[len(after):]
        layer, k, mine = pl.program_id(0), pl.program_id(1), s_ref[0]
        tot = jnp.where(layer == 0, t0_ref[...], t1_ref[...])
        oth = jnp.where(layer == 0, o0_ref[...], o1_ref[...])
        if transposed:
            g = jnp.concatenate([jnp.where(mine == 0, tot, oth), jnp.where(mine == 0, oth, tot)], axis=0).T
        else:
            g = jnp.where(k == mine, tot, oth)
        g_ref[0] = g
        d_ref[0], nm_ref[0], nv_ref[0] = _adamw_math(w_ref[0], g, m_ref[0], v_ref[0])

    (t0, o0, off0), (t1, o1, off1) = sources
    grid_spec = pltpu.PrefetchScalarGridSpec(
        num_scalar_prefetch=1, grid=grid,
        in_specs=[w_spec] * 3 + [g_spec(off0), g_spec(off0), g_spec(off1), g_spec(off1)] + [_ANY] * len(after),
        out_specs=[w_spec] * 4)
    return pl.pallas_call(
        body, name=name, grid_spec=grid_spec, out_shape=[_sds(w.shape, F32)] * 4,
        compiler_params=pltpu.CompilerParams(dimension_semantics=("arbitrary", "arbitrary"),
                                             vmem_limit_bytes=V7X_VMEM_LIMIT))(half_index, w, m, v, t0, o0, t1, o1, *after)


def _position():
    return lax.axis_index("x"), lax.axis_index("y"), lax.axis_index("c")


def _other_chips(x, y):
    return [(1 - x, y), (x, 1 - y), (1 - x, 1 - y)]


def _peers_chips(x, y, c):
    return [(cx, cy, c) for cx, cy in _other_chips(x, y)]


def _peers_sibling(x, y, c):
    return [(x, y, 1 - c)]


def _peers_chips_and_sibling(x, y, c):
    return _peers_chips(x, y, c) + _peers_sibling(x, y, c)


def _comm_call(name, body, n_in, out_shape, scratch, sequencer=None):
    if sequencer is None:
        return pl.pallas_call(
            body, name=name, in_specs=[_HBM] * n_in, out_specs=[_HBM] * len(out_shape), out_shape=out_shape,
            scratch_shapes=list(scratch),
            compiler_params=pltpu.CompilerParams(has_side_effects=True, vmem_limit_bytes=V7X_VMEM_LIMIT))
    collective_id, peers = sequencer

    def seq_body(*refs):
        barrier = pltpu.get_barrier_semaphore()
        plist = peers(*_position())
        for peer in plist:
            pl.semaphore_signal(barrier, inc=1, device_id=peer, device_id_type=MESH)
        pl.semaphore_wait(barrier, len(plist))
        body(*refs)

    return pl.kernel(seq_body, out_type=out_shape, mesh=plsc.ScalarSubcoreMesh(axis_name="sequencer", num_cores=1), name=name,
                     scratch_types=list(scratch), compiler_params=pltpu.CompilerParams(collective_id=collective_id))


def _n_chunks(rows, want, dtype):
    align = 16 if dtype == BF16 else 8
    n = want
    while n > 1 and (rows % n or (rows // n) % align):
        n -= 1
    return n


def _remote(src, dst, send_sem, recv_sem, device):
    return pltpu.make_async_remote_copy(src_ref=src, dst_ref=dst, send_sem=send_sem, recv_sem=recv_sem,
                                        device_id=device, device_id_type=MESH)


def _start_in_chunks(src, dst, send_sem, recv_sem, device, want=8):
    rows = src.shape[0]
    n = _n_chunks(rows, want, src.dtype)
    for i in range(n):
        part = pl.ds(i * (rows // n), rows // n)
        _remote(src.at[part], dst.at[part], send_sem, recv_sem, device).start()


def _all_gather_chips(name, shards, split, collective_id=None):
    n = len(shards)

    def body(*refs):
        ins, outs = refs[:n], refs[n:2 * n]
        send_sems, recv_sems, local_sems = refs[2 * n:]
        x, y, c = _position()
        me = 2 * x + y
        chips = _other_chips(x, y)
        sibling = (x, y, 1 - c)
        sends, forwards = [], []
        for t in range(n):
            pltpu.make_async_copy(ins[t], outs[t].at[me], local_sems.at[t]).start()
        for t in range(n):
            r = ins[t].shape[0]
            rows = pl.ds(c * (r // 2), r // 2) if split[t] else pl.ds(0, r)
            for k, (cx, cy) in enumerate(chips):
                src, dst = ins[t].at[rows], outs[t].at[me, rows]
                _start_in_chunks(src, dst, send_sems.at[t, k], recv_sems.at[t, k], (cx, cy, c), want=4)
                sends.append(_remote(src, dst, send_sems.at[t, k], recv_sems.at[t, k], (cx, cy, c)))
        for t in range(n):
            r = ins[t].shape[0]
            rows = pl.ds(c * (r // 2), r // 2) if split[t] else pl.ds(0, r)
            for k, (cx, cy) in enumerate(chips):
                landed = outs[t].at[2 * cx + cy, rows]
                _remote(landed, landed, send_sems.at[t, k], recv_sems.at[t, k], (cx, cy, c)).wait_recv()
                if split[t]:
                    _start_in_chunks(landed, landed, send_sems.at[t, 3 + k], recv_sems.at[t, 3 + k], sibling, want=4)
                    forwards.append(_remote(landed, landed, send_sems.at[t, 3 + k], recv_sems.at[t, 3 + k], sibling))
        for t in range(n):
            if split[t]:
                r = ins[t].shape[0]
                other = pl.ds((1 - c) * (r // 2), r // 2)
                for k, (cx, cy) in enumerate(chips):
                    landed = outs[t].at[2 * cx + cy, other]
                    pltpu.make_async_remote_copy(
                        src_ref=landed, dst_ref=landed, send_sem=send_sems.at[t, 3 + k], recv_sem=recv_sems.at[t, 3 + k],
                        device_id=sibling, device_id_type=MESH).wait_recv()
        for cp in sends + forwards:
            cp.wait_send()
        for t in range(n):
            pltpu.make_async_copy(ins[t], outs[t].at[me], local_sems.at[t]).wait()

    out_shape = [_sds((N_CHIPS,) + s.shape, s.dtype) for s in shards]
    sequencer = None if collective_id is None else (collective_id, _peers_chips_and_sibling)
    return _comm_call(name, body, n, out_shape,
                      [pltpu.SemaphoreType.DMA((n, 6)), pltpu.SemaphoreType.DMA((n, 6)), pltpu.SemaphoreType.DMA((n,))],
                      sequencer)(*shards)


def _swap_halves_with_sibling(name, blob, collective_id=None):
    def body(b_ref, theirs_ref, send_sem, recv_sem):
        x, y, c = _position()
        sibling = (x, y, 1 - c)
        for k in range(b_ref.shape[1]):
            _start_in_chunks(b_ref.at[1 - c, k], theirs_ref.at[k], send_sem, recv_sem, sibling)
        _remote(b_ref.at[1 - c], theirs_ref, send_sem, recv_sem, sibling).wait()

    return _comm_call(name, body, 1, [_sds(blob.shape[1:], blob.dtype)],
                      [pltpu.SemaphoreType.DMA(()), pltpu.SemaphoreType.DMA(())],
                      None if collective_id is None else (collective_id, _peers_sibling))(blob)[0]


def _scatter_to_chips(name, parts, collective_id=None):
    def body(p_ref, o_ref, send_sems, recv_sems, local_sems):
        x, y, c = _position()
        me = 2 * x + y
        rows = p_ref.shape[1]
        n_loc = _n_chunks(rows, 16, p_ref.dtype)
        locs = [pltpu.make_async_copy(p_ref.at[me, pl.ds(i * (rows // n_loc), rows // n_loc)],
                                      o_ref.at[me, pl.ds(i * (rows // n_loc), rows // n_loc)], local_sems.at[i])
                for i in range(n_loc)]
        for loc in locs:
            loc.start()
        sends = []
        for k, (cx, cy) in enumerate(_other_chips(x, y)):
            src, dst = p_ref.at[2 * cx + cy], o_ref.at[me]
            _start_in_chunks(src, dst, send_sems.at[k], recv_sems.at[k], (cx, cy, c))
            sends.append(_remote(src, dst, send_sems.at[k], recv_sems.at[k], (cx, cy, c)))
        for k, (cx, cy) in enumerate(_other_chips(x, y)):
            landed = o_ref.at[2 * cx + cy]
            _remote(landed, landed, send_sems.at[k], recv_sems.at[k], (cx, cy, c)).wait_recv()
        for cp in sends:
            cp.wait_send()
        for loc in locs:
            loc.wait()

    def local_sems_shape(rows):
        return pltpu.SemaphoreType.DMA((_n_chunks(rows, 16, parts.dtype),))

    return _comm_call(name, body, 1, [_sds(parts.shape, parts.dtype)],
                      [pltpu.SemaphoreType.DMA((3,)), pltpu.SemaphoreType.DMA((3,)), local_sems_shape(parts.shape[1])],
                      None if collective_id is None else (collective_id, _peers_chips))(parts)[0]


def _share_with_sibling(name, mine, collective_id=None):
    def body(m_ref, o_ref, send_sem, recv_sem):
        x, y, c = _position()
        sibling = (x, y, 1 - c)
        _start_in_chunks(m_ref, o_ref, send_sem, recv_sem, sibling, want=16)
        _remote(m_ref, o_ref, send_sem, recv_sem, sibling).wait()

    return _comm_call(name, body, 1, [_sds(mine.shape, mine.dtype)],
                      [pltpu.SemaphoreType.DMA(()), pltpu.SemaphoreType.DMA(())],
                      None if collective_id is None else (collective_id, _peers_sibling))(mine)[0]


def _row_tile(rows, cap=640):
    best = rows
    for t in range(16, min(rows, cap) + 1, 16):
        if rows % t == 0:
            best = t
    return best


_ANY = pl.BlockSpec(memory_space=pl.ANY)


def _add_my_half(name, blob, theirs, half_index, out_dtype, after):
    n, rows, cols = theirs.shape
    tr = _row_tile(rows)
    after = tuple(after)

    def body(c_ref, a_ref, b_ref, *rest):
        o_ref = rest[-1]
        o_ref[...] = (a_ref[0].astype(F32) + b_ref[...].astype(F32)).astype(out_dtype)

    spec = pl.BlockSpec((1, tr, cols), lambda k, i, c: (k, i, 0))
    grid_spec = pltpu.PrefetchScalarGridSpec(
        num_scalar_prefetch=1, grid=(n, rows // tr),
        in_specs=[pl.BlockSpec((1, 1, tr, cols), lambda k, i, c: (c[0], k, i, 0)), spec] + [_ANY] * len(after), out_specs=spec)
    return pl.pallas_call(
        body, name=name, grid_spec=grid_spec, out_shape=_sds(theirs.shape, out_dtype),
        compiler_params=pltpu.CompilerParams(dimension_semantics=("arbitrary", "arbitrary"),
                                             vmem_limit_bytes=V7X_VMEM_LIMIT))(half_index, blob, theirs, *after)


def _sum_slots(name, parts, after):
    n, rows, cols = parts.shape
    tr = _row_tile(rows)

    def body(p_ref, o_ref):
        acc = p_ref[0].astype(F32)
        for k in range(1, n):
            acc = acc + p_ref[k].astype(F32)
        o_ref[...] = acc

    return _call(name, body, (rows // tr,), [pl.BlockSpec((n, tr, cols), lambda i: (0, i, 0))],
                 pl.BlockSpec((tr, cols), lambda i: (i, 0)), _sds((rows, cols), F32), after=after)(parts)


def _reduce_small_adamw(name, grads, loss_tile, ws, ms, vs, after=()):
    n = len(grads)
    srcs = list(grads) + [loss_tile]
    after = tuple(after)

    def body(*refs):
        refs = refs[:4 * n + 1] + refs[4 * n + 1 + len(after):]
        g_in, w_in, m_in, v_in = refs[:n + 1], refs[n + 1:2 * n + 1], refs[2 * n + 1:3 * n + 1], refs[3 * n + 1:4 * n + 1]
        outs = refs[4 * n + 1:8 * n + 2]
        g_out, d_out, nm_out, nv_out, loss_out = outs[:n], outs[n:2 * n], outs[2 * n:3 * n], outs[3 * n:4 * n], outs[4 * n]
        bufs = refs[8 * n + 2:9 * n + 3]
        send_sems, recv_sems = refs[9 * n + 3:]
        x, y, c = _position()
        me = 4 * x + 2 * y + c
        chip = 2 * x + y
        peers = [(1 - x if dlt & 4 else x, 1 - y if dlt & 2 else y, 1 - c if dlt & 1 else c) for dlt in range(1, N_DEV)]
        sends = []
        for t in range(n + 1):
            bufs[t][me] = g_in[t][...]
            for k, peer in enumerate(peers):
                cp = _remote(g_in[t], bufs[t].at[me], send_sems.at[t, k], recv_sems.at[t, k], peer)
                cp.start()
                sends.append(cp)
        for t in range(n + 1):
            for k, (tx, ty, tc) in enumerate(peers):
                landed = bufs[t].at[4 * tx + 2 * ty + tc]
                _remote(landed, landed, send_sems.at[t, k], recv_sems.at[t, k], (tx, ty, tc)).wait_recv()
        for cp in sends:
            cp.wait_send()
        for t in range(n + 1):
            total = bufs[t][0]
            for k in range(1, N_DEV):
                total = total + bufs[t][k]
            if t == n:
                loss_out[...] = total
                continue
            cols = w_in[t].shape[1]
            if cols == total.shape[1]:
                g_out[t][...] = total
                d_out[t][...], nm_out[t][...], nv_out[t][...] = _adamw_math(w_in[t][...], total, m_in[t][...], v_in[t][...])
            else:
                for j in range(N_CHIPS):
                    @pl.when(chip == j)
                    def _(t=t, j=j, cols=cols, total=total):
                        mine = total[:, j * cols:(j + 1) * cols]
                        g_out[t][...] = mine
                        d_out[t][...], nm_out[t][...], nv_out[t][...] = _adamw_math(w_in[t][...], mine, m_in[t][...], v_in[t][...])

    w_shapes = [_sds(a.shape, F32) for a in ws]
    return pl.pallas_call(
        body, name=name, in_specs=[_VMEM] * (4 * n + 1) + [_ANY] * len(after), out_specs=[_VMEM] * (4 * n + 1),
        out_shape=w_shapes * 4 + [_sds(loss_tile.shape, F32)],
        scratch_shapes=[pltpu.VMEM((N_DEV,) + a.shape, F32) for a in srcs]
        + [pltpu.SemaphoreType.DMA((n + 1, N_DEV - 1)), pltpu.SemaphoreType.DMA((n + 1, N_DEV - 1))],
        compiler_params=pltpu.CompilerParams(has_side_effects=True, vmem_limit_bytes=V7X_VMEM_LIMIT))(
            *srcs, *ws, *ms, *vs, *after)


_BIG = ("ffn1_w_gate_up", "ffn1_w_down", "ffn2_w_gate_up", "ffn2_w_down", "ssm_w_in", "ssm_w_out", "w_kv", "attn_w_q", "attn_w_o")
_TRANSPOSED = ("ffn1_w_gate_up", "ffn2_w_gate_up")
_FROM_HALVES = _TRANSPOSED + ("ffn1_w_down", "ffn2_w_down")
_SMALL = ("meta_tokens", "ffn1_norm", "mix_norm", "ffn2_norm", "ssm_lambda_re", "ssm_lambda_im", "ssm_b_re", "ssm_b_im",
          "ssm_c_re", "ssm_c_im", "ssm_log_step", "ssm_d", "kv_norm", "k_norm", "q_norm", "attn_sinks")
_ORDER = ("meta_tokens", "ffn1_norm", "ffn1_w_gate_up", "ffn1_w_down", "mix_norm", "ffn2_norm", "ffn2_w_gate_up", "ffn2_w_down",
          "ssm_w_in", "ssm_lambda_re", "ssm_lambda_im", "ssm_b_re", "ssm_b_im", "ssm_c_re", "ssm_c_im", "ssm_log_step", "ssm_d",
          "ssm_w_out", "kv_norm", "w_kv", "k_norm", "attn_w_q", "q_norm", "attn_sinks", "attn_w_o")


def _step(x, target, w, m, v):
    n_ex, seq, d = x.shape
    n_meta = w["meta_tokens"].shape[0]
    n_main = n_ex * seq
    n_all = n_main + n_ex * META_BLOCK
    n_g, n_p, n_c = w["ssm_b_re"].shape[1:]
    hd = w["k_norm"].shape[0]
    n_kv = w["w_kv"].shape[1] // (2 * hd)
    n_q = w["attn_w_q"].shape[2] // hd
    qpk = n_q // n_kv
    px, py, pc = _position()
    chip = 2 * px + py

    def cast(name, layer=0):
        a = w[name]
        return _cast_layer(f"cast_{name}_{layer}", a if a.ndim == 3 else a[None], layer)

    first = [cast("ffn1_w_gate_up"), cast("ffn1_w_down"), cast("ssm_w_in"), cast("ssm_w_out"), w["meta_tokens"], w["ssm_d"]]
    g_a = _all_gather_chips("gather_first", first, [True, True, True, True, False, False], collective_id=12)
    second = [cast("ffn2_w_gate_up"), cast("ffn2_w_down"), cast("w_kv")]
    g_b = _all_gather_chips("gather_second", second, [True] * 3, collective_id=1)
    third = [cast("ffn1_w_gate_up", 1), cast("ffn1_w_down", 1), cast("attn_w_q"), cast("attn_w_o"),
             cast("ffn2_w_gate_up", 1), cast("ffn2_w_down", 1)]
    g_c = _all_gather_chips("gather_third", third, [True] * 6, collective_id=2)
    wgu = {("ffn1", 0): g_a[0], ("ffn1", 1): g_c[0], ("ffn2", 0): g_b[0], ("ffn2", 1): g_c[4]}
    wd = {("ffn1", 0): g_a[1], ("ffn1", 1): g_c[1], ("ffn2", 0): g_b[1], ("ffn2", 1): g_c[5]}
    wd = {key: a.reshape(-1, d) for key, a in wd.items()}
    w_in = g_a[2].reshape(d, -1)
    wout4 = g_a[3]
    w_q = g_c[2].reshape(d, -1)
    w_o = g_c[3].reshape(-1, d)
    w_kv = g_b[2].reshape(d, -1)
    meta_full = jnp.transpose(g_a[4], (1, 0, 2)).reshape(n_meta, d)
    dskip = g_a[5].reshape(1, -1)

    row1 = lambda a: a.reshape(1, -1)
    ssm_args = tuple(w[k][0] for k in ("ssm_lambda_re", "ssm_lambda_im", "ssm_log_step", "ssm_b_re", "ssm_b_im", "ssm_c_re", "ssm_c_im"))
    (bb, cb, a_re, a_im), ssm_vjp = jax.vjp(_ssm_mats, *ssm_args)
    bb16, cb16 = bb.astype(BF16), cb.astype(BF16)
    a_re_s, a_im_s = lax.stop_gradient(a_re), lax.stop_gradient(a_im)
    half = n_g * n_p // 4
    tabs_f = _scan_tables(a_re_s, a_im_s, False)
    tabs_b = _scan_tables(a_re_s, a_im_s, True)

    freqs = ROPE_THETA ** (-jnp.arange(0, hd // 2, dtype=F32) * 2.0 / hd)
    pos_main = jnp.tile(n_meta + jnp.arange(seq), n_ex)
    pos_meta = jnp.tile(jnp.maximum(jnp.arange(META_BLOCK) - (META_BLOCK - n_meta), 0), n_ex)
    ang = jnp.concatenate([pos_main, pos_meta]).astype(F32)[:, None] * freqs[None, :]
    cos = jnp.concatenate([jnp.cos(ang), jnp.cos(ang)] * 2, axis=1)
    sin_s = jnp.concatenate([-jnp.sin(ang), jnp.sin(ang)] * 2, axis=1)
    k_gain_t = jnp.tile(row1(w["k_norm"]), (1, n_kv))
    score_scale = hd ** -0.5
    q_gain_t = jnp.tile(row1(w["q_norm"][0]), (1, n_q)) * score_scale

    meta_block = jnp.concatenate([jnp.zeros((META_BLOCK - n_meta, d), F32), meta_full], axis=0)
    h0 = jnp.concatenate([x.reshape(n_main, d)] + [meta_block] * n_ex, axis=0)

    g = lambda name, layer: row1(w[name][layer])
    h1, gu1 = _ffn_fwd("l0_ffn1", h0, g("ffn1_norm", 0), wgu["ffn1", 0], wd["ffn1", 0], n_all)
    u, bu = _ssm_in("ssm_in", h1, g("mix_norm", 0), w_in, bb16, n_all)
    xs = _scan_fwd("ssm_scan", bu, tabs_f, n_ex, seq)
    h2, y = _ssm_out("ssm_out", xs, u, dskip, cb16, wout4, h1, n_all)
    h3, gu2 = _ffn_fwd("l0_ffn2", h2, g("ffn2_norm", 0), wgu["ffn2", 0], wd["ffn2", 0], n_all)
    kvraw, k, vv = _kv_proj("kv_proj", h3, row1(w["kv_norm"]), w_kv, k_gain_t, cos, sin_s, n_all, n_kv, hd)
    h4, gu3 = _ffn_fwd("l1_ffn1", h3, g("ffn1_norm", 1), wgu["ffn1", 1], wd["ffn1", 1], n_main)
    qraw, q = _q_proj("q_proj", h4, g("mix_norm", 1), w_q, q_gain_t, cos, sin_s, n_main, n_q, hd)
    sinks = row1(w["attn_sinks"][0])
    o, lse = _attn_fwd("attn_fwd", q, k, vv, sinks, n_ex, seq, n_meta, n_kv, qpk, hd)
    h5 = _attn_out("attn_out", o, h4, w_o, n_main)
    (dh6, loss_tile), gu4 = _ffn_fwd("l1_ffn2", h5, g("ffn2_norm", 1), wgu["ffn2", 1], wd["ffn2", 1], n_main,
                                     target=target.reshape(n_main, d))

    lanes = 1024

    def rs_start(tag, entries, ids):
        pieces = [gr.reshape(N_CHIPS, 2, -1, lanes) for _, _, gr in entries]
        blob = jnp.transpose(jnp.concatenate(pieces, axis=2), (1, 0, 2, 3)).astype(BF16)
        return dict(tag=tag, entries=entries, ids=ids, blob=blob, theirs=_swap_halves_with_sibling(tag + "_swap", blob, ids[0]))

    def rs_scatter(st, after):
        chip_sum = _add_my_half(st["tag"] + "_chip_sum", st["blob"], st["theirs"], jnp.reshape(pc, (1,)).astype(jnp.int32), BF16, after)
        st["chip_sum"] = chip_sum
        st["landed"] = _scatter_to_chips(st["tag"] + "_scatter", chip_sum, st["ids"][1])

    def rs_finish(st, after):
        total = _sum_slots(st["tag"] + "_sum", st["landed"], after)
        st["total"] = total
        other = _share_with_sibling(st["tag"] + "_share", total, st["ids"][2])
        halves = (jnp.where(pc == 0, total, other), jnp.where(pc == 0, other, total))
        out, off = {}, 0
        for name, layer, gr in st["entries"]:
            rows = gr.shape[1] * gr.shape[2] // lanes // 2
            if name in _FROM_HALVES:
                out[name, layer] = (total, other, off)
            else:
                out[name, layer] = jnp.concatenate([hv[off:off + rows].reshape(-1) for hv in halves])
            off += rows
        return out

    small = {}
    dh5, dg_f2l1, dwgu_f2l1, dwd_f2l1 = _ffn_bwd("l1_ffn2", dh6, h5, g("ffn2_norm", 1), gu4, wgu["ffn2", 1], wd["ffn2", 1], n_main)
    do = _attn_out_bwd("attn_out_bwd", dh5, w_o, n_main)
    dw_o = _tn_plain("attn_dwo", o, dh5, 1, o.shape[1], d, n_main, out_dtype=BF16).reshape(N_CHIPS, -1, d)
    dq, dk_main, dv_main, dk_meta, dv_meta, dsinks = _attn_bwd("attn_bwd", q, k, vv, sinks, o, lse, do, n_ex, seq, n_meta, n_kv, qpk, hd)
    dqraw, dh4, dq_gain, dg_mix1 = _q_bwd("q_bwd", dq, qraw, q_gain_t, cos, sin_s, w_q, h4, g("mix_norm", 1), dh5, n_main, n_q, hd)
    dw_q = _tn_rms("attn_dwq", h4, g("mix_norm", 1), dqraw, n_main, out_dtype=BF16).reshape(N_CHIPS, -1, dqraw.shape[1])
    dh3m, dg_f1l1, dwgu_f1l1, dwd_f1l1 = _ffn_bwd("l1_ffn1", dh4, h3, g("ffn1_norm", 1), gu3, wgu["ffn1", 1], wd["ffn1", 1], n_main)
    rs1 = rs_start("rs1", [("ffn2_w_gate_up", 1, dwgu_f2l1), ("ffn1_w_gate_up", 1, dwgu_f1l1), ("ffn2_w_down", 1, dwd_f2l1),
                           ("ffn1_w_down", 1, dwd_f1l1), ("attn_w_o", 0, dw_o), ("attn_w_q", 0, dw_q)], (3, 4, 5))

    def with_meta(main, meta):
        blocks = [jnp.pad(meta[b * n_meta:(b + 1) * n_meta], ((META_BLOCK - n_meta, 0), (0, 0))) for b in range(n_ex)]
        return jnp.concatenate([main] + blocks, axis=0)

    dkvraw, dh3, dk_gain, dg_kv = _kv_bwd("kv_bwd", with_meta(dk_main, dk_meta), with_meta(dv_main, dv_meta), kvraw, k_gain_t,
                                          cos, sin_s, w_kv, h3, row1(w["kv_norm"]), dh3m, n_all, n_main, n_kv, hd,
                                          after=(rs1["blob"],))
    rs_scatter(rs1, after=(dh3,))
    dw_kv = _tn_rms("kv_dw", h3, row1(w["kv_norm"]), dkvraw, n_all, out_dtype=BF16).reshape(N_CHIPS, -1, dkvraw.shape[1])
    dh2, dg_f2l0, dwgu_f2l0, dwd_f2l0 = _ffn_bwd("l0_ffn2", dh3, h2, g("ffn2_norm", 0), gu2, wgu["ffn2", 0], wd["ffn2", 0], n_all,
                                                 after=(rs1["chip_sum"],))
    reduced = rs_finish(rs1, after=(dh2, dwgu_f2l0, dwd_f2l0, dw_kv))
    rs0a = rs_start("rs0a", [("ffn2_w_gate_up", 0, dwgu_f2l0), ("ffn2_w_down", 0, dwd_f2l0), ("w_kv", 0, dw_kv)], (6, 7, 8))

    dy, dz, gx, dd = _ssm_out_bwd("ssm_out_bwd", dh2, y, u, cb16, wout4, n_all, after=(rs1["total"], rs0a["blob"]))
    rs_scatter(rs0a, after=(dy,))
    hw = y.shape[1]
    oc = wout4.shape[2]
    dw_out = _tn_plain("ssm_dwout", y, dz, wout4.shape[0], hw, oc, n_all, a_fn=_gelu, out_dtype=BF16)
    gbu, da = _scan_bwd("ssm_scan_bwd", gx, xs, tabs_b, n_ex, seq, after=(rs0a["chip_sum"],))
    du, dh1, dg_mix0 = _ssm_in_bwd("ssm_in_bwd", gbu, dy, dskip, bb16, w_in, h1, g("mix_norm", 0), dh2, n_all)
    reduced.update(rs_finish(rs0a, after=(dh1,)))
    dw_in = _tn_rms("ssm_dwin", h1, g("mix_norm", 0), du, n_all, out_dtype=BF16).reshape(N_CHIPS, -1, hw)
    (dh0, dh0_meta), dg_f1l0, dwgu_f1l0, dwd_f1l0 = _ffn_bwd("l0_ffn1", dh1, h0, g("ffn1_norm", 0), gu1, wgu["ffn1", 0], wd["ffn1", 0],
                                                             n_all, n_main, after=(rs0a["total"],))
    rs0b = rs_start("rs0b", [("ffn1_w_gate_up", 0, dwgu_f1l0), ("ffn1_w_down", 0, dwd_f1l0), ("ssm_w_out", 0, dw_out),
                             ("ssm_w_in", 0, dw_in)], (9, 10, 11))
    dcb = _tn_plain("ssm_dcb", xs, dy, 4, xs.shape[1] // 4, hw // 4, n_all, after=(rs0b["blob"],))
    rs_scatter(rs0b, after=(dcb,))
    dbb = _tn_plain("ssm_dbb", u, gbu, 4, hw // 4, gbu.shape[1] // 4, n_all, after=(rs0b["chip_sum"],))

    grad_x = dh0.reshape(n_ex, seq, d)
    da_sum = jnp.sum(da, axis=(0, 1)).reshape(4, 2, half)
    d_ssm = ssm_vjp((dbb, dcb, da_sum[:, 0].reshape(-1), da_sum[:, 1].reshape(-1)))
    for key, val in zip(("ssm_lambda_re", "ssm_lambda_im", "ssm_log_step", "ssm_b_re", "ssm_b_im", "ssm_c_re", "ssm_c_im"), d_ssm):
        small[key] = val[None]
    small["meta_tokens"] = sum(dh0_meta[META_BLOCK * (b + 1) - n_meta:META_BLOCK * (b + 1)] for b in range(n_ex))
    small["ffn1_norm"] = jnp.concatenate([dg_f1l0, dg_f1l1], axis=0)
    small["ffn2_norm"] = jnp.concatenate([dg_f2l0, dg_f2l1], axis=0)
    small["mix_norm"] = jnp.concatenate([dg_mix0, dg_mix1], axis=0)
    small["ssm_d"] = dd
    small["kv_norm"] = dg_kv.reshape(-1)
    small["k_norm"] = jnp.sum(dk_gain.reshape(n_kv, hd), axis=0)
    small["q_norm"] = jnp.sum(dq_gain.reshape(n_q, hd), axis=0, keepdims=True) * score_scale
    small["attn_sinks"] = dsinks

    def view(name, a):
        if name in ("ssm_b_re", "ssm_b_im"):
            return a.reshape(-1, 128)
        return a.reshape(1, -1) if a.ndim == 1 else a.reshape(-1, a.shape[-1])

    grads, deltas, new_m, new_v = {}, {}, {}, {}

    def adamw_matrix(name, after=()):
        shape = w[name].shape
        if name in _FROM_HALVES:
            grads[name], deltas[name], new_m[name], new_v[name] = _adamw_from_halves(
                "adamw_" + name, w[name], m[name], v[name], [reduced[name, 0], reduced[name, 1]],
                jnp.reshape(pc, (1,)).astype(jnp.int32), name in _TRANSPOSED, after=after)
            return new_v[name]
        layers = [reduced[name, layer] for layer in range(2) if (name, layer) in reduced]
        grads[name] = jnp.concatenate(layers).reshape(shape)
        two_d = lambda a: a.reshape(-1, shape[-1])
        dl, nm, nv = _adamw("adamw_" + name, two_d(w[name]), two_d(grads[name]), two_d(m[name]), two_d(v[name]), after=after)
        deltas[name], new_m[name], new_v[name] = dl.reshape(shape), nm.reshape(shape), nv.reshape(shape)
        return nv

    placed = (rs0b["chip_sum"],)
    for name in ("ffn2_w_down", "attn_w_o", "attn_w_q", "w_kv"):
        placed = (adamw_matrix(name, after=placed),)
    tail = _reduce_small_adamw("small_tail", [view(k, small[k]) for k in _SMALL], loss_tile,
                               *[[view(k, t[k]) for k in _SMALL] for t in (w, m, v)], after=placed)
    n_small = len(_SMALL)
    for i, k in enumerate(_SMALL):
        grads[k], deltas[k] = tail[i].reshape(w[k].shape), tail[n_small + i].reshape(w[k].shape)
        new_m[k], new_v[k] = tail[2 * n_small + i].reshape(w[k].shape), tail[3 * n_small + i].reshape(w[k].shape)
    loss = jnp.sum(tail[-1])
    reduced.update(rs_finish(rs0b, after=(tail[-1],)))
    adamw_matrix("ffn2_w_gate_up", after=(rs0b["total"],))
    for name in ("ffn1_w_gate_up", "ffn1_w_down", "ssm_w_in", "ssm_w_out"):
        adamw_matrix(name)
    return (loss, grad_x, *[grads[k] for k in _ORDER], *[deltas[k] for k in _ORDER], *[new_m[k] for k in _ORDER],
            *[new_v[k] for k in _ORDER])


def kernel(x, meta_tokens, ffn1_norm, ffn1_w_gate_up, ffn1_w_down, mix_norm, ffn2_norm, ffn2_w_gate_up, ffn2_w_down, ssm_w_in, ssm_lambda_re, ssm_lambda_im, ssm_b_re, ssm_b_im, ssm_c_re, ssm_c_im, ssm_log_step, ssm_d, ssm_w_out, kv_norm, w_kv, k_norm, attn_w_q, q_norm, attn_sinks, attn_w_o, loss_target, m_meta_tokens, m_ffn1_norm, m_ffn1_w_gate_up, m_ffn1_w_down, m_mix_norm, m_ffn2_norm, m_ffn2_w_gate_up, m_ffn2_w_down, m_ssm_w_in, m_ssm_lambda_re, m_ssm_lambda_im, m_ssm_b_re, m_ssm_b_im, m_ssm_c_re, m_ssm_c_im, m_ssm_log_step, m_ssm_d, m_ssm_w_out, m_kv_norm, m_w_kv, m_k_norm, m_attn_w_q, m_q_norm, m_attn_sinks, m_attn_w_o, v_meta_tokens, v_ffn1_norm, v_ffn1_w_gate_up, v_ffn1_w_down, v_mix_norm, v_ffn2_norm, v_ffn2_w_gate_up, v_ffn2_w_down, v_ssm_w_in, v_ssm_lambda_re, v_ssm_lambda_im, v_ssm_b_re, v_ssm_b_im, v_ssm_c_re, v_ssm_c_im, v_ssm_log_step, v_ssm_d, v_ssm_w_out, v_kv_norm, v_w_kv, v_k_norm, v_attn_w_q, v_q_norm, v_attn_sinks, v_attn_w_o):
    args = locals()
    w = {k: args[k] for k in _ORDER}
    m = {k: args["m_" + k] for k in _ORDER}
    v = {k: args["v_" + k] for k in _ORDER}
    return _step(x, loss_target, w, m, v)
```

```python
import functools
import math

import jax
import jax.numpy as jnp
from jax import lax
from jax.experimental import pallas as pl
from jax.experimental.pallas import tpu as pltpu
from jax.experimental.pallas import tpu_sc as plsc

F32 = jnp.float32
BF16 = jnp.bfloat16
MESH = pl.DeviceIdType.MESH

EPS = 1e-6
NEG_INF = -1e30
ROPE_THETA = 10000.0
WINDOW = 128
META_BLOCK = 128
ROW_TILE = 256
SUBLANES = 8
V7X_VMEM_LIMIT = 56 * 2**20
N_CHIPS = 4
N_DEV = 8

ADAM_LR = 0.001
ADAM_B1 = 0.9
ADAM_B2 = 0.999
ADAM_EPS = 1e-08
ADAM_WD = 0.01
ADAM_STEP = 10

_HBM = pl.BlockSpec(memory_space=pltpu.HBM)
_VMEM = pl.BlockSpec(memory_space=pltpu.VMEM)


def _call(name, body, grid, in_specs, out_specs, out_shape, scratch=(), after=()):
    after = tuple(after)
    n_in = len(in_specs)

    def wrapped(*refs):
        return body(*refs[:n_in], *refs[n_in + len(after):])

    call = pl.pallas_call(
        wrapped, name=name, grid=grid, in_specs=list(in_specs) + [pl.BlockSpec(memory_space=pl.ANY)] * len(after),
        out_specs=out_specs, out_shape=out_shape, scratch_shapes=list(scratch),
        compiler_params=pltpu.CompilerParams(dimension_semantics=("arbitrary",) * len(grid),
                                             vmem_limit_bytes=V7X_VMEM_LIMIT))
    return lambda *operands: call(*operands, *after)


def _sds(shape, dtype):
    return jax.ShapeDtypeStruct(tuple(shape), dtype)


def _dot(a, b):
    return jnp.dot(a.astype(BF16), b.astype(BF16), preferred_element_type=F32)


def _dot_nt(a, b):
    return lax.dot_general(a.astype(BF16), b.astype(BF16), (((1,), (1,)), ((), ())), preferred_element_type=F32)


def _dot_tn(a, b):
    return lax.dot_general(a.astype(BF16), b.astype(BF16), (((0,), (0,)), ((), ())), preferred_element_type=F32)


def _rms(h, g):
    return h * lax.rsqrt(jnp.mean(h * h, axis=-1, keepdims=True) + EPS) * g


def _rms_bwd(h, g, dn):
    r = lax.rsqrt(jnp.mean(h * h, axis=-1, keepdims=True) + EPS)
    xh = h * r
    dxh = dn * g
    dg = jnp.sum(dn * xh, axis=0, keepdims=True)
    dh = r * (dxh - xh * jnp.mean(dxh * xh, axis=-1, keepdims=True))
    return dh, dg


def _sigmoid(x):
    return 0.5 * jnp.tanh(0.5 * x) + 0.5


def _gelu(y):
    k = math.sqrt(2.0 / math.pi)
    return 0.5 * y * (1.0 + jnp.tanh(k * (y + 0.044715 * y * y * y)))


def _gelu_grad(y):
    k = math.sqrt(2.0 / math.pi)
    t = jnp.tanh(k * (y + 0.044715 * y * y * y))
    return 0.5 * (1.0 + t) + 0.5 * y * (1.0 - t * t) * k * (1.0 + 3.0 * 0.044715 * y * y)


def _partner(x, lane, d):
    width = x.shape[-1]
    return jnp.where((lane & d) == 0, pltpu.roll(x, width - d, 1), pltpu.roll(x, d, 1))


def _split_bf16(x):
    hi = x.astype(BF16)
    return hi, (x - hi.astype(F32)).astype(BF16)


def _head_sums(x, sel):
    hi, lo = _split_bf16(x)
    return jnp.dot(hi, sel, preferred_element_type=F32) + jnp.dot(lo, sel, preferred_element_type=F32)


def _head_expand(v, sel_t):
    hi, lo = _split_bf16(v)
    return jnp.dot(hi, sel_t, preferred_element_type=F32) + jnp.dot(lo, sel_t, preferred_element_type=F32)


def _tile_lanes(t, width):
    return jnp.concatenate([t] * (width // t.shape[-1]), axis=1)


def _head_prep(x, gain_t, cos2, sin2, sel, sel_t, hd):
    width = x.shape[-1]
    lane = lax.broadcasted_iota(jnp.int32, x.shape, 1)
    r = _head_expand(lax.rsqrt(_head_sums(x * x, sel) * (1.0 / hd) + EPS), sel_t)
    y = x * r * gain_t
    return y * _tile_lanes(cos2, width) + _partner(y, lane, hd // 2) * _tile_lanes(sin2, width)


def _head_prep_bwd(x, gain_t, cos2, sin2, sel, sel_t, d_out, hd):
    width = x.shape[-1]
    lane = lax.broadcasted_iota(jnp.int32, x.shape, 1)
    r = _head_expand(lax.rsqrt(_head_sums(x * x, sel) * (1.0 / hd) + EPS), sel_t)
    xhat = x * r
    dy = d_out * _tile_lanes(cos2, width) + _partner(d_out * _tile_lanes(sin2, width), lane, hd // 2)
    dgain = jnp.sum(dy * xhat, axis=0, keepdims=True)
    dxh = dy * gain_t
    mean = _head_expand(_head_sums(dxh * xhat, sel) * (1.0 / hd), sel_t)
    return r * (dxh - xhat * mean), dgain


def _head_selectors(n_heads, hd):
    sel = (jnp.arange(n_heads * hd)[:, None] // hd == jnp.arange(128)[None, :]).astype(BF16)
    return sel, sel.T


def _acc_out(ref, val, first):
    @pl.when(first)
    def _():
        ref[...] = jnp.zeros_like(ref)
    ref[...] += val


def _wide_row_tile(n_rows, cap=512):
    best = 128
    for t in range(128, cap + 1, 128):
        if n_rows % t == 0:
            best = t
    return best


def _ffn_up(name, h, g, w4, n_rows):
    nj, d, fc = w4.shape
    tm = _wide_row_tile(n_rows)

    def body(h_ref, g_ref, w_ref, o_ref, n_ref):
        n = _rms(h_ref[...], g_ref[...]).astype(BF16)
        n_ref[...] = n
        for j in range(nj):
            o_ref[:, j * fc:(j + 1) * fc] = _dot(n, w_ref[j]).astype(BF16)

    return _call(name, body, (n_rows // tm,),
                 [pl.BlockSpec((tm, d), lambda i: (i, 0)), pl.BlockSpec((1, d), lambda i: (0, 0)),
                  pl.BlockSpec((nj, d, fc), lambda i: (0, 0, 0))],
                 [pl.BlockSpec((tm, nj * fc), lambda i: (i, 0)), pl.BlockSpec((tm, d), lambda i: (i, 0))],
                 [_sds((n_rows, nj * fc), BF16), _sds((n_rows, d), BF16)])(h, g, w4)


def _ffn_down(name, gu, h, wd, n_rows, target=None):
    f, d = wd.shape
    tm = ROW_TILE

    def body(gu_ref, h_ref, w_ref, *rest):
        half_a = gu_ref[:, :f] * 0.5
        s = (half_a + half_a * jnp.tanh(half_a)) * gu_ref[:, f:]
        y = h_ref[...] + 0.5 * _dot(s, w_ref[...])
        if target is None:
            o_ref, s_ref = rest
            o_ref[...] = y
        else:
            t_ref, dy_ref, l_ref, s_ref = rest
            e = y - t_ref[...]
            dy_ref[...] = e * (1.0 / d)
            e2 = jnp.sum((e * e).reshape(tm // SUBLANES, SUBLANES, d), axis=0)
            part = e2[:, 0:128]
            for k in range(1, d // 128):
                part = part + e2[:, k * 128:(k + 1) * 128]
            _acc_out(l_ref, part * (0.5 / d), pl.program_id(0) == 0)
        s_ref[...] = s

    row = lambda width: pl.BlockSpec((tm, width), lambda i: (i, 0))
    in_specs = [row(2 * f), row(d), pl.BlockSpec((f, d), lambda i: (0, 0))]
    if target is None:
        return _call(name, body, (n_rows // tm,), in_specs, [row(d), row(f)],
                     [_sds((n_rows, d), F32), _sds((n_rows, f), BF16)])(gu, h, wd)
    return _call(name, body, (n_rows // tm,), in_specs + [row(d)],
                 [row(d), pl.BlockSpec((SUBLANES, 128), lambda i: (0, 0)), row(f)],
                 [_sds((n_rows, d), F32), _sds((SUBLANES, 128), F32), _sds((n_rows, f), BF16)])(gu, h, wd, target)


def _ffn_dgu(name, dh, gu, wd, n_rows, after=()):
    f, d = wd.shape
    tm = ROW_TILE

    def body(dh_ref, gu_ref, w_ref, o_ref):
        ds = _dot_nt(0.5 * dh_ref[...], w_ref[...]).astype(BF16)
        half_a = gu_ref[:, :f] * 0.5
        t = jnp.tanh(half_a)
        o_ref[:, :f] = ds * gu_ref[:, f:] * ((1.0 + t + half_a * (1.0 - t * t)) * 0.5)
        o_ref[:, f:] = ds * (half_a + half_a * t)

    return _call(name, body, (n_rows // tm,),
                 [pl.BlockSpec((tm, d), lambda i: (i, 0)), pl.BlockSpec((tm, 2 * f), lambda i: (i, 0)),
                  pl.BlockSpec((f, d), lambda i: (0, 0))],
                 pl.BlockSpec((tm, 2 * f), lambda i: (i, 0)), _sds((n_rows, 2 * f), BF16), after=after)(dh, gu, wd)


def _ffn_dh(name, dgu, h, g, dh, w4, n_rows, n_main=None, after=()):
    nj, d, fc = w4.shape
    tm = _wide_row_tile(n_rows) if n_main is None else ROW_TILE
    n_first = (n_rows if n_main is None else n_main) // tm

    def body(dgu_ref, h_ref, g_ref, dh_ref, w_ref, o_ref, *rest):
        dg_ref = rest[-1]
        i = pl.program_id(0)
        dn = _dot_nt(dgu_ref[:, 0:fc], w_ref[0])
        for j in range(1, nj):
            dn = dn + _dot_nt(dgu_ref[:, j * fc:(j + 1) * fc], w_ref[j])
        dhn, dg = _rms_bwd(h_ref[...], g_ref[...], dn)
        val = dh_ref[...] + dhn
        if n_main is None:
            o_ref[...] = val
        else:
            @pl.when(i < n_first)
            def _():
                o_ref[...] = val

            @pl.when(i >= n_first)
            def _():
                rest[0][...] = val
        _acc_out(dg_ref, dg, i == 0)

    out_specs = [pl.BlockSpec((tm, d), lambda i: (jnp.minimum(i, n_first - 1), 0))]
    out_shape = [_sds((n_first * tm, d), F32)]
    if n_main is not None:
        out_specs.append(pl.BlockSpec((tm, d), lambda i: (jnp.maximum(i - n_first, 0), 0)))
        out_shape.append(_sds((n_rows - n_main, d), F32))
    return _call(name, body, (n_rows // tm,),
                 [pl.BlockSpec((tm, nj * fc), lambda i: (i, 0)), pl.BlockSpec((tm, d), lambda i: (i, 0)),
                  pl.BlockSpec((1, d), lambda i: (0, 0)), pl.BlockSpec((tm, d), lambda i: (i, 0)),
                  pl.BlockSpec((nj, d, fc), lambda i: (0, 0, 0), pipeline_mode=pl.Buffered(1))],
                 out_specs + [pl.BlockSpec((1, d), lambda i: (0, 0))],
                 out_shape + [_sds((1, d), F32)], after=after)(dgu, h, g, dh, w4)


def _contract_tile(n_rows, cap=1024):
    best = ROW_TILE
    for t in range(ROW_TILE, cap + 1, ROW_TILE):
        if n_rows % t == 0:
            best = t
    return best


def _tn(name, operands, in_specs, prologue, nj, ma, nb, n_rows, tk, out_dtype=F32, after=(), side_by_side=False):
    n_k = n_rows // tk
    out_spec = pl.BlockSpec((1, ma, nb), lambda j, k: (j, 0, 0))
    if out_dtype == F32 and not side_by_side:
        def body(*refs):
            o_ref = refs[-1]
            a, b = prologue(pl.program_id(0), *refs[:-1])
            _acc_out(o_ref, _dot_tn(a, b)[None], pl.program_id(1) == 0)

        return _call(name, body, (nj, n_k), in_specs, out_spec, _sds((nj, ma, nb), F32), after=after)(*operands)

    def body_rounded(*refs):
        o_ref, acc_ref = refs[-2:]
        a, b = prologue(pl.program_id(0), *refs[:-2])
        _acc_out(acc_ref, _dot_tn(a, b), pl.program_id(1) == 0)

        @pl.when(pl.program_id(1) == n_k - 1)
        def _():
            o_ref[...] = acc_ref[...].astype(out_dtype).reshape(o_ref.shape)

    if side_by_side:
        out_spec, out_shape = pl.BlockSpec((ma, nb), lambda j, k: (0, j)), _sds((ma, nj * nb), out_dtype)
    else:
        out_shape = _sds((nj, ma, nb), out_dtype)
    return _call(name, body_rounded, (nj, n_k), in_specs, out_spec, out_shape,
                 scratch=[pltpu.VMEM((ma, nb), F32)], after=after)(*operands)


def _ffn_dwgu(name, n, dgu, nj, n_rows):
    d = n.shape[1]
    fc = dgu.shape[1] // nj
    tk = _contract_tile(n_rows, cap=2816)
    return _tn(name, (dgu, n),
               [pl.BlockSpec((tk, fc), lambda j, k: (k, j)), pl.BlockSpec((tk, d), lambda j, k: (k, 0))],
               lambda j, a_ref, b_ref: (a_ref[...], b_ref[...]), nj, fc, d, n_rows, tk, out_dtype=BF16)


def _ffn_dwd(name, s, dh, n_rows):
    f = s.shape[1]
    d = dh.shape[1]
    tk = _contract_tile(n_rows, cap=2048)
    halves = 2 if tk > 1024 else 1
    return _tn(name, (s, dh),
               [pl.BlockSpec((tk, f), lambda j, k: (k, 0)), pl.BlockSpec((tk, d // halves), lambda j, k: (k, j))],
               lambda j, s_ref, dh_ref: (s_ref[...], 0.5 * dh_ref[...]), halves, f, d // halves, n_rows, tk, out_dtype=BF16,
               side_by_side=True)


def _ffn_fwd(tag, h, g, w4, wd, n_rows, target=None):
    gu, n = _ffn_up(tag + "_up", h, g, w4, n_rows)
    *out, s = _ffn_down(tag + "_down", gu, h, wd, n_rows, target)
    return (out[0] if target is None else tuple(out)), (gu, n, s)


def _ffn_bwd(tag, dh_out, h, g, saved, w4, wd, n_rows, n_main=None, after=()):
    gu, n, s = saved
    nj = w4.shape[0]
    f, d = wd.shape
    dgu = _ffn_dgu(tag + "_dgu", dh_out, gu, wd, n_rows, after=after)
    dwd = _ffn_dwd(tag + "_dwd", s, dh_out, n_rows).reshape(N_CHIPS, f // N_CHIPS, d)
    *dh_parts, dg = _ffn_dh(tag + "_dh", dgu, h, g, dh_out, w4, n_rows, n_main)
    dwgu = _ffn_dwgu(tag + "_dwgu", n, dgu, nj, n_rows)
    dh_in = dh_parts[0] if n_main is None else tuple(dh_parts)
    return dh_in, dg, dwgu, dwd


def _ssm_in(name, h, g, w_in, bb, n_rows):
    d, hw = w_in.shape
    nj, uc, xc = bb.shape
    tm = ROW_TILE

    def body(h_ref, g_ref, w_ref, bb_ref, u_ref, bu_ref):
        u = _dot(_rms(h_ref[...], g_ref[...]), w_ref[...])
        u_ref[...] = u
        for j in range(nj):
            bu_ref[:, j * xc:(j + 1) * xc] = _dot(u[:, j * uc:(j + 1) * uc], bb_ref[j]).astype(BF16)

    return _call(name, body, (n_rows // tm,),
                 [pl.BlockSpec((tm, d), lambda i: (i, 0)), pl.BlockSpec((1, d), lambda i: (0, 0)),
                  pl.BlockSpec((d, hw), lambda i: (0, 0)), pl.BlockSpec((nj, uc, xc), lambda i: (0, 0, 0))],
                 [pl.BlockSpec((tm, hw), lambda i: (i, 0)), pl.BlockSpec((tm, nj * xc), lambda i: (i, 0))],
                 [_sds((n_rows, hw), F32), _sds((n_rows, nj * xc), BF16)])(h, g, w_in, bb)


def _cmul_add(xr, xi, ar, ai, sr, si):
    return xr + ar * sr - ai * si, xi + ar * si + ai * sr


def _scan_row_block(n_main_blocks, seq_blocks):
    return lambda b, i: jnp.where(i == 0, n_main_blocks + b, b * seq_blocks + i - 1)


def _scan_fwd(name, bu, tabs, n_ex, seq):
    n_rows, width = bu.shape
    nj = 4
    cw = width // nj
    half = cw // 2
    tq = META_BLOCK
    seq_blocks = seq // tq
    rb = _scan_row_block(n_ex * seq_blocks, seq_blocks)

    def body(bu_ref, tab_ref, x_ref, carry_ref):
        @pl.when(pl.program_id(1) == 0)
        def _():
            carry_ref[...] = jnp.zeros_like(carry_ref)

        for j in range(nj):
            re, im = slice(j * cw, j * cw + half), slice(j * cw + half, (j + 1) * cw)
            ch = slice(j * half, (j + 1) * half)

            def blk(k, c, re=re, im=im, ch=ch):
                t = [tab_ref[n * SUBLANES:(n + 1) * SUBLANES, ch] for n in range(8)]
                r0 = pl.multiple_of(k * SUBLANES, SUBLANES)
                xr = bu_ref[pl.ds(r0, SUBLANES), re].astype(F32)
                xi = bu_ref[pl.ds(r0, SUBLANES), im].astype(F32)
                for s, d in enumerate((1, 2, 4)):
                    xr, xi = _cmul_add(xr, xi, t[2 * s], t[2 * s + 1], pltpu.roll(xr, d, 0), pltpu.roll(xi, d, 0))
                xr, xi = _cmul_add(xr, xi, t[6], t[7], c[0], c[1])
                x_ref[pl.ds(r0, SUBLANES), re] = xr.astype(BF16)
                x_ref[pl.ds(r0, SUBLANES), im] = xi.astype(BF16)
                last = SUBLANES - 1
                return (jnp.broadcast_to(xr[last:last + 1, :], xr.shape), jnp.broadcast_to(xi[last:last + 1, :], xi.shape))

            c = lax.fori_loop(0, tq // SUBLANES, blk, (carry_ref[0, :, ch], carry_ref[1, :, ch]), unroll=2)
            carry_ref[0, :, ch] = c[0]
            carry_ref[1, :, ch] = c[1]

    return _call(name, body, (n_ex, seq_blocks + 1),
                 [pl.BlockSpec((tq, width), lambda b, i: (rb(b, i), 0)), pl.BlockSpec((8 * SUBLANES, nj * half), lambda b, i: (0, 0))],
                 pl.BlockSpec((tq, width), lambda b, i: (rb(b, i), 0)), _sds((n_rows, width), BF16),
                 scratch=[pltpu.VMEM((2, SUBLANES, nj * half), F32)])(bu, tabs)


def _scan_bwd(name, gx, x, tabs, n_ex, seq, after=()):
    n_rows, width = gx.shape
    nj = 4
    cw = width // nj
    half = cw // 2
    tq = META_BLOCK
    seq_blocks = seq // tq
    n_steps = seq_blocks + 1
    rb = _scan_row_block(n_ex * seq_blocks, seq_blocks)
    rbr = lambda b, i: rb(b, n_steps - 1 - i)

    def body(gx_ref, x_ref, tab_ref, g_ref, da_ref, carry_ref):
        @pl.when(pl.program_id(1) == 0)
        def _():
            carry_ref[...] = jnp.zeros_like(carry_ref)
            da_ref[...] = jnp.zeros_like(da_ref)
        row = lax.broadcasted_iota(jnp.int32, (SUBLANES, half), 0)
        n_blk = tq // SUBLANES

        for j in range(nj):
            re, im = slice(j * cw, j * cw + half), slice(j * cw + half, (j + 1) * cw)
            ch = slice(j * half, (j + 1) * half)

            def blk(kk, st, re=re, im=im, ch=ch):
                t = [tab_ref[n * SUBLANES:(n + 1) * SUBLANES, ch] for n in range(8)]
                cr, ci, dar, dai = st
                r0 = pl.multiple_of((n_blk - 1 - kk) * SUBLANES, SUBLANES)
                gr = gx_ref[pl.ds(r0, SUBLANES), re].astype(F32)
                gi = gx_ref[pl.ds(r0, SUBLANES), im].astype(F32)
                for s, d in enumerate((1, 2, 4)):
                    gr, gi = _cmul_add(gr, gi, t[2 * s], t[2 * s + 1],
                                       pltpu.roll(gr, SUBLANES - d, 0), pltpu.roll(gi, SUBLANES - d, 0))
                gr, gi = _cmul_add(gr, gi, t[6], t[7], cr, ci)
                g_ref[pl.ds(r0, SUBLANES), re] = gr.astype(BF16)
                g_ref[pl.ds(r0, SUBLANES), im] = gi.astype(BF16)
                hr = jnp.where(row == SUBLANES - 1, cr, pltpu.roll(gr, SUBLANES - 1, 0))
                hi = jnp.where(row == SUBLANES - 1, ci, pltpu.roll(gi, SUBLANES - 1, 0))
                xr = x_ref[pl.ds(r0, SUBLANES), re].astype(F32)
                xi = x_ref[pl.ds(r0, SUBLANES), im].astype(F32)
                dar = dar + xr * hr + xi * hi
                dai = dai + xr * hi - xi * hr
                return (jnp.broadcast_to(gr[0:1, :], gr.shape), jnp.broadcast_to(gi[0:1, :], gi.shape), dar, dai)

            st = lax.fori_loop(0, n_blk, blk, (carry_ref[0, :, ch], carry_ref[1, :, ch], da_ref[0, :, re], da_ref[0, :, im]),
                               unroll=2)
            carry_ref[0, :, ch] = st[0]
            carry_ref[1, :, ch] = st[1]
            da_ref[0, :, re] = st[2]
            da_ref[0, :, im] = st[3]

    return _call(name, body, (n_ex, n_steps),
                 [pl.BlockSpec((tq, width), lambda b, i: (rbr(b, i), 0)), pl.BlockSpec((tq, width), lambda b, i: (rbr(b, i), 0)),
                  pl.BlockSpec((8 * SUBLANES, nj * half), lambda b, i: (0, 0))],
                 [pl.BlockSpec((tq, width), lambda b, i: (rbr(b, i), 0)), pl.BlockSpec((1, SUBLANES, width), lambda b, i: (b, 0, 0))],
                 [_sds((n_rows, width), BF16), _sds((n_ex, SUBLANES, width), F32)],
                 scratch=[pltpu.VMEM((2, SUBLANES, nj * half), F32)], after=after)(gx, x, tabs)


def _ssm_z(gy, wout_ref, nj):
    return jnp.concatenate([_dot(gy, wout_ref[j]) for j in range(nj)], axis=1)


def _ssm_out(name, x, u, dskip, cb, wout4, h, n_rows):
    nj, xc, uc = cb.shape
    no, hw, oc = wout4.shape
    d = h.shape[1]
    tm = ROW_TILE

    def body(x_ref, u_ref, ds_ref, cb_ref, w_ref, h_ref, o_ref, y_ref):
        y = jnp.concatenate([_dot(x_ref[:, j * xc:(j + 1) * xc], cb_ref[j]) for j in range(nj)], axis=1)
        y = y + ds_ref[...] * u_ref[...]
        y_ref[...] = y
        z = _ssm_z(_gelu(y), w_ref, no)
        o_ref[...] = h_ref[...] + z[:, :d] * _sigmoid(z[:, d:])

    return _call(name, body, (n_rows // tm,),
                 [pl.BlockSpec((tm, nj * xc), lambda i: (i, 0)), pl.BlockSpec((tm, hw), lambda i: (i, 0)),
                  pl.BlockSpec((1, hw), lambda i: (0, 0)), pl.BlockSpec((nj, xc, uc), lambda i: (0, 0, 0)),
                  pl.BlockSpec((no, hw, oc), lambda i: (0, 0, 0)), pl.BlockSpec((tm, d), lambda i: (i, 0))],
                 [pl.BlockSpec((tm, d), lambda i: (i, 0)), pl.BlockSpec((tm, hw), lambda i: (i, 0))],
                 [_sds((n_rows, d), F32), _sds((n_rows, hw), F32)])(x, u, dskip, cb, wout4, h)


def _ssm_out_bwd(name, dh, y, u, cb, wout4, n_rows, after=()):
    nj, xc, uc = cb.shape
    no, hw, oc = wout4.shape
    d = dh.shape[1]
    tm = ROW_TILE

    def body(dh_ref, y_ref, u_ref, cb_ref, w_ref, dy_ref, dz_ref, gx_ref, dd_ref):
        y = y_ref[...]
        z = _ssm_z(_gelu(y), w_ref, no)
        za = z[:, :d]
        sg = _sigmoid(z[:, d:])
        dmix = dh_ref[...]
        dz = jnp.concatenate([dmix * sg, dmix * za * sg * (1.0 - sg)], axis=1).astype(BF16)
        dz_ref[...] = dz
        dgy = _dot_nt(dz[:, 0:oc], w_ref[0])
        for j in range(1, no):
            dgy = dgy + _dot_nt(dz[:, j * oc:(j + 1) * oc], w_ref[j])
        dy = dgy * _gelu_grad(y)
        dy_ref[...] = dy
        _acc_out(dd_ref, jnp.sum(dy * u_ref[...], axis=0, keepdims=True), pl.program_id(0) == 0)
        for j in range(nj):
            gx_ref[:, j * xc:(j + 1) * xc] = _dot_nt(dy[:, j * uc:(j + 1) * uc], cb_ref[j]).astype(BF16)

    return _call(name, body, (n_rows // tm,),
                 [pl.BlockSpec((tm, d), lambda i: (i, 0)), pl.BlockSpec((tm, hw), lambda i: (i, 0)),
                  pl.BlockSpec((tm, hw), lambda i: (i, 0)), pl.BlockSpec((nj, xc, uc), lambda i: (0, 0, 0)),
                  pl.BlockSpec((no, hw, oc), lambda i: (0, 0, 0))],
                 [pl.BlockSpec((tm, hw), lambda i: (i, 0)), pl.BlockSpec((tm, no * oc), lambda i: (i, 0)),
                  pl.BlockSpec((tm, nj * xc), lambda i: (i, 0)), pl.BlockSpec((1, hw), lambda i: (0, 0))],
                 [_sds((n_rows, hw), F32), _sds((n_rows, no * oc), BF16), _sds((n_rows, nj * xc), BF16),
                  _sds((1, hw), F32)], after=after)(dh, y, u, cb, wout4)


def _ssm_in_bwd(name, gbu, dy, dskip, bb, w_in, h, g, dh, n_rows):
    nj, uc, xc = bb.shape
    d, hw = w_in.shape
    tm = ROW_TILE

    def body(gb_ref, dy_ref, ds_ref, bb_ref, w_ref, h_ref, g_ref, dh_ref, du_ref, o_ref, dg_ref):
        du = jnp.concatenate([_dot_nt(gb_ref[:, j * xc:(j + 1) * xc], bb_ref[j]) for j in range(nj)], axis=1)
        du = du + dy_ref[...] * ds_ref[...]
        du_ref[...] = du.astype(BF16)
        dhn, dg = _rms_bwd(h_ref[...], g_ref[...], _dot_nt(du, w_ref[...]))
        o_ref[...] = dh_ref[...] + dhn
        _acc_out(dg_ref, dg, pl.program_id(0) == 0)

    return _call(name, body, (n_rows // tm,),
                 [pl.BlockSpec((tm, nj * xc), lambda i: (i, 0)), pl.BlockSpec((tm, hw), lambda i: (i, 0)),
                  pl.BlockSpec((1, hw), lambda i: (0, 0)), pl.BlockSpec((nj, uc, xc), lambda i: (0, 0, 0)),
                  pl.BlockSpec((d, hw), lambda i: (0, 0)), pl.BlockSpec((tm, d), lambda i: (i, 0)),
                  pl.BlockSpec((1, d), lambda i: (0, 0)), pl.BlockSpec((tm, d), lambda i: (i, 0))],
                 [pl.BlockSpec((tm, hw), lambda i: (i, 0)), pl.BlockSpec((tm, d), lambda i: (i, 0)),
                  pl.BlockSpec((1, d), lambda i: (0, 0))],
                 [_sds((n_rows, hw), BF16), _sds((n_rows, d), F32), _sds((1, d), F32)])(gbu, dy, dskip, bb, w_in, h, g, dh)


def _discretize(lam_re, lam_im, log_step, b_re, b_im):
    step = jnp.exp(log_step)[:, None]
    mag = jnp.exp(lam_re * step)
    ar = mag * jnp.cos(lam_im * step)
    ai = mag * jnp.sin(lam_im * step)
    den = lam_re * lam_re + lam_im * lam_im
    nr, ni = ar - 1.0, ai
    cr = (nr * lam_re + ni * lam_im) / den
    ci = (ni * lam_re - nr * lam_im) / den
    bbar_r = cr[..., None] * b_re - ci[..., None] * b_im
    bbar_i = cr[..., None] * b_im + ci[..., None] * b_re
    return ar, ai, bbar_r, bbar_i


def _ssm_mats(lam_re, lam_im, log_step, b_re, b_im, c_re, c_im):
    n_g, n_p, n_c = b_re.shape
    gpc = n_g // 4
    ar, ai, bbar_r, bbar_i = _discretize(lam_re, lam_im, log_step, b_re, b_im)
    eye = jnp.eye(gpc, dtype=F32)

    def in_map(bbar):
        return jnp.einsum('jgpc,gh->jgchp', bbar.reshape(4, gpc, n_p, n_c), eye).reshape(4, gpc * n_c, gpc * n_p)

    def out_map(c):
        return jnp.einsum('jgcp,gh->jgphc', c.reshape(4, gpc, n_c, n_p), eye).reshape(4, gpc * n_p, gpc * n_c)

    bb = jnp.concatenate([in_map(bbar_r), in_map(bbar_i)], axis=2)
    cb = jnp.concatenate([out_map(c_re), -out_map(c_im)], axis=1)
    return bb, cb, ar.reshape(-1), ai.reshape(-1)


def _chunked(v, half):
    return v.reshape(v.shape[:-1] + (4, half))


def _scan_tables(ar, ai, reverse):
    if reverse:
        ai = -ai
    pr, pi = [ar], [ai]
    for _ in range(SUBLANES - 1):
        pr, pi = pr + [pr[-1] * ar - pi[-1] * ai], pi + [pr[-1] * ai + pi[-1] * ar]
    row = jnp.arange(SUBLANES)[:, None]
    tabs = []
    for d in (1, 2, 4):
        keep = (row <= SUBLANES - 1 - d) if reverse else (row >= d)
        tabs += [jnp.where(keep, pr[d - 1][None, :], 0.0), jnp.where(keep, pi[d - 1][None, :], 0.0)]
    order = list(range(SUBLANES))[::-1] if reverse else list(range(SUBLANES))
    tabs += [jnp.stack([pr[k] for k in order]), jnp.stack([pi[k] for k in order])]
    return jnp.concatenate(tabs, axis=0)


def _kv_proj(name, h, g, w_kv, k_gain_t, cos2, sin2, n_rows, n_kv, hd):
    d, kvw = w_kv.shape
    kw = n_kv * hd
    tm = ROW_TILE

    sel, sel_t = _head_selectors(n_kv, hd)

    def body(h_ref, g_ref, w_ref, kg_ref, c_ref, s_ref, e_ref, et_ref, raw_ref, k_ref, v_ref):
        raw = _dot(_rms(h_ref[...], g_ref[...]), w_ref[...])
        raw_ref[...] = raw
        k_ref[...] = _head_prep(raw[:, :kw], kg_ref[...], c_ref[...], s_ref[...], e_ref[...], et_ref[...], hd).astype(BF16)
        v_ref[...] = raw[:, kw:].astype(BF16)

    return _call(name, body, (n_rows // tm,),
                 [pl.BlockSpec((tm, d), lambda i: (i, 0)), pl.BlockSpec((1, d), lambda i: (0, 0)),
                  pl.BlockSpec((d, kvw), lambda i: (0, 0)), pl.BlockSpec((1, kw), lambda i: (0, 0)),
                  pl.BlockSpec((tm, 2 * hd), lambda i: (i, 0)), pl.BlockSpec((tm, 2 * hd), lambda i: (i, 0)),
                  pl.BlockSpec(sel.shape, lambda i: (0, 0)), pl.BlockSpec(sel_t.shape, lambda i: (0, 0))],
                 [pl.BlockSpec((tm, kvw), lambda i: (i, 0)), pl.BlockSpec((tm, kw), lambda i: (i, 0)),
                  pl.BlockSpec((tm, kw), lambda i: (i, 0))],
                 [_sds((n_rows, kvw), F32), _sds((n_rows, kw), BF16), _sds((n_rows, kw), BF16)])(
                     h, g, w_kv, k_gain_t, cos2, sin2, sel, sel_t)


def _q_proj(name, h, g, w_q, q_gain_t, cos2, sin2, n_rows, n_q, hd):
    d, qw = w_q.shape
    tm = ROW_TILE

    sel, sel_t = _head_selectors(n_q, hd)

    def body(h_ref, g_ref, w_ref, qg_ref, c_ref, s_ref, e_ref, et_ref, raw_ref, q_ref):
        raw = _dot(_rms(h_ref[...], g_ref[...]), w_ref[...])
        raw_ref[...] = raw
        q_ref[...] = _head_prep(raw, qg_ref[...], c_ref[...], s_ref[...], e_ref[...], et_ref[...], hd).astype(BF16)

    return _call(name, body, (n_rows // tm,),
                 [pl.BlockSpec((tm, d), lambda i: (i, 0)), pl.BlockSpec((1, d), lambda i: (0, 0)),
                  pl.BlockSpec((d, qw), lambda i: (0, 0)), pl.BlockSpec((1, qw), lambda i: (0, 0)),
                  pl.BlockSpec((tm, 2 * hd), lambda i: (i, 0)), pl.BlockSpec((tm, 2 * hd), lambda i: (i, 0)),
                  pl.BlockSpec(sel.shape, lambda i: (0, 0)), pl.BlockSpec(sel_t.shape, lambda i: (0, 0))],
                 [pl.BlockSpec((tm, qw), lambda i: (i, 0)), pl.BlockSpec((tm, qw), lambda i: (i, 0))],
                 [_sds((n_rows, qw), F32), _sds((n_rows, qw), BF16)])(h, g, w_q, q_gain_t, cos2, sin2, sel, sel_t)


def _attn_specs(seq, n_ex, n_meta, kw):
    nb = seq // WINDOW
    meta_blk = lambda b: (n_ex * seq + META_BLOCK * b + META_BLOCK - n_meta) // n_meta
    return [pl.BlockSpec((WINDOW, kw), lambda b, n: (b * nb + jnp.maximum(n - 1, 0), 0)),
            pl.BlockSpec((WINDOW, kw), lambda b, n: (b * nb + n, 0)),
            pl.BlockSpec((n_meta, kw), lambda b, n: (meta_blk(b), 0))]


def _attn_bias(qpk, n_keys):
    rows = qpk * WINDOW
    qi = jnp.arange(rows)[:, None] & (WINDOW - 1)
    kj = jnp.arange(n_keys)[None, :]
    rel = qi + WINDOW - kj
    band = (rel >= 0) & (rel < WINDOW)
    meta = kj >= 2 * WINDOW
    first = (band & (kj >= WINDOW)) | meta
    return jnp.where(jnp.stack([first, band | meta]), 0.0, NEG_INF).astype(F32)


def _stack_heads(ref, h, qpk, hd, dtype=None):
    parts = [ref[:, (h * qpk + gq) * hd:(h * qpk + gq + 1) * hd] for gq in range(qpk)]
    out = jnp.concatenate(parts, axis=0)
    return out if dtype is None else out.astype(dtype)


def _col(tile, c):
    lane = lax.broadcasted_iota(jnp.int32, tile.shape, 1)
    return jnp.sum(jnp.where(lane == c, tile, 0.0), axis=-1, keepdims=True)


def _put_col(col, c, n):
    lane = lax.broadcasted_iota(jnp.int32, (col.shape[0], n), 1)
    return jnp.where(lane == c, col, 0.0)


def _stack_cols(tile, h, qpk):
    return jnp.concatenate([_col(tile, h * qpk + gq) for gq in range(qpk)], axis=0)


def _sink_col(sinks, h, qpk):
    return jnp.concatenate([jnp.broadcast_to(_col(sinks, h * qpk + gq), (WINDOW, 1)) for gq in range(qpk)], axis=0)


def _attn_fwd(name, q, k, v, sinks, n_ex, seq, n_meta, n_kv, qpk, hd):
    nb = seq // WINDOW
    n_q = n_kv * qpk
    kw = n_kv * hd
    qw = n_q * hd
    n_keys = 2 * WINDOW + n_meta
    bias = _attn_bias(qpk, n_keys)

    def body(q_ref, kp_ref, kc_ref, km_ref, vp_ref, vc_ref, vm_ref, sk_ref, bias_ref, o_ref, lse_ref):
        sinks_v = sk_ref[...]
        o_parts = []
        lse_all = jnp.zeros((WINDOW, n_q), F32)
        for h in range(n_kv):
            hs = slice(h * hd, (h + 1) * hd)
            kb = jnp.concatenate([kp_ref[:, hs], kc_ref[:, hs], km_ref[:, hs]], axis=0)
            vb = jnp.concatenate([vp_ref[:, hs], vc_ref[:, hs], vm_ref[:, hs]], axis=0)
            s = _dot_nt(_stack_heads(q_ref, h, qpk, hd), kb) + bias_ref[0]
            skc = _sink_col(sinks_v, h, qpk)
            m = jnp.maximum(jnp.max(s, axis=-1, keepdims=True), skc)
            p = jnp.exp(s - m)
            den = jnp.sum(p, axis=-1, keepdims=True) + jnp.exp(skc - m)
            o = _dot(p, vb) / den
            lse = m + jnp.log(den)
            for gq in range(qpk):
                o_parts.append(o[gq * WINDOW:(gq + 1) * WINDOW])
                lse_all = lse_all + _put_col(lse[gq * WINDOW:(gq + 1) * WINDOW], h * qpk + gq, n_q)
        o_ref[...] = jnp.concatenate(o_parts, axis=1)
        lse_ref[...] = lse_all

    qspec = pl.BlockSpec((WINDOW, qw), lambda b, n: (b * nb + n, 0))
    return _call(name, body, (n_ex, nb),
                 [qspec] + _attn_specs(seq, n_ex, n_meta, kw) + _attn_specs(seq, n_ex, n_meta, kw)
                 + [pl.BlockSpec((1, n_q), lambda b, n: (0, 0)),
                    pl.BlockSpec((1,) + bias.shape[1:], lambda b, n: (jnp.minimum(n, 1), 0, 0))],
                 [qspec, pl.BlockSpec((WINDOW, n_q), lambda b, n: (b * nb + n, 0))],
                 [_sds((n_ex * seq, qw), F32), _sds((n_ex * seq, n_q), F32)])(q, k, k, k, v, v, v, sinks, bias)


def _attn_bwd(name, q, k, v, sinks, o, lse, do, n_ex, seq, n_meta, n_kv, qpk, hd):
    nb = seq // WINDOW
    n_q = n_kv * qpk
    kw = n_kv * hd
    qw = n_q * hd
    n_keys = 2 * WINDOW + n_meta
    bias = _attn_bias(qpk, n_keys)

    def body(q_ref, kp_ref, kc_ref, km_ref, vp_ref, vc_ref, vm_ref, sk_ref, o_ref, lse_ref, do_ref, bias_ref,
             dq_ref, dk_ref, dv_ref, dkm_ref, dvm_ref, dsk_ref):
        n = pl.program_id(1)

        @pl.when(n == 0)
        def _():
            dk_ref[...] = jnp.zeros_like(dk_ref)
            dv_ref[...] = jnp.zeros_like(dv_ref)
            dkm_ref[...] = jnp.zeros_like(dkm_ref)
            dvm_ref[...] = jnp.zeros_like(dvm_ref)

        @pl.when((n == 0) & (pl.program_id(0) == 0))
        def _():
            dsk_ref[...] = jnp.zeros_like(dsk_ref)

        sinks_v = sk_ref[...]
        lse_v = lse_ref[...]
        dq_parts, dk_parts, dv_parts = [], [], []
        dsk = jnp.zeros((1, n_q), F32)
        for h in range(n_kv):
            hs = slice(h * hd, (h + 1) * hd)
            kb = jnp.concatenate([kp_ref[:, hs], kc_ref[:, hs], km_ref[:, hs]], axis=0)
            vb = jnp.concatenate([vp_ref[:, hs], vc_ref[:, hs], vm_ref[:, hs]], axis=0)
            qs = _stack_heads(q_ref, h, qpk, hd)
            dos = _stack_heads(do_ref, h, qpk, hd)
            delta = jnp.sum(dos * _stack_heads(o_ref, h, qpk, hd), axis=-1, keepdims=True)
            lse_c = _stack_cols(lse_v, h, qpk)
            p = jnp.exp(_dot_nt(qs, kb) + bias_ref[0] - lse_c)
            ds = p * (_dot_nt(dos, vb) - delta)
            dqs = _dot(ds, kb)
            dk_parts.append(_dot_tn(ds, qs))
            dv_parts.append(_dot_tn(p, dos))
            dsink = -jnp.exp(_sink_col(sinks_v, h, qpk) - lse_c) * delta
            for gq in range(qpk):
                dq_parts.append(dqs[gq * WINDOW:(gq + 1) * WINDOW])
                dsk = dsk + _put_col(jnp.sum(dsink[gq * WINDOW:(gq + 1) * WINDOW], axis=0, keepdims=True), h * qpk + gq, n_q)
        dq_ref[...] = jnp.concatenate(dq_parts, axis=1)
        dsk_ref[...] += dsk
        dkb = jnp.concatenate(dk_parts, axis=1)
        dvb = jnp.concatenate(dv_parts, axis=1)
        prev = pl.ds(pl.multiple_of(jnp.maximum(n - 1, 0) * WINDOW, WINDOW), WINDOW)
        cur = pl.ds(pl.multiple_of(n * WINDOW, WINDOW), WINDOW)
        dk_ref[prev, :] += dkb[0:WINDOW]
        dv_ref[prev, :] += dvb[0:WINDOW]
        dk_ref[cur, :] += dkb[WINDOW:2 * WINDOW]
        dv_ref[cur, :] += dvb[WINDOW:2 * WINDOW]
        dkm_ref[...] += dkb[2 * WINDOW:]
        dvm_ref[...] += dvb[2 * WINDOW:]

    qspec = pl.BlockSpec((WINDOW, qw), lambda b, n: (b * nb + n, 0))
    exspec = pl.BlockSpec((seq, kw), lambda b, n: (b, 0))
    mspec = pl.BlockSpec((n_meta, kw), lambda b, n: (b, 0))
    return _call(name, body, (n_ex, nb),
                 [qspec] + _attn_specs(seq, n_ex, n_meta, kw) + _attn_specs(seq, n_ex, n_meta, kw)
                 + [pl.BlockSpec((1, n_q), lambda b, n: (0, 0)), qspec,
                    pl.BlockSpec((WINDOW, n_q), lambda b, n: (b * nb + n, 0)), qspec,
                    pl.BlockSpec((1,) + bias.shape[1:], lambda b, n: (jnp.minimum(n, 1), 0, 0))],
                 [qspec, exspec, exspec, mspec, mspec, pl.BlockSpec((1, n_q), lambda b, n: (0, 0))],
                 [_sds((n_ex * seq, qw), F32), _sds((n_ex * seq, kw), F32), _sds((n_ex * seq, kw), F32),
                  _sds((n_ex * n_meta, kw), F32), _sds((n_ex * n_meta, kw), F32), _sds((1, n_q), F32)])(
                      q, k, k, k, v, v, v, sinks, o, lse, do, bias)


def _attn_out(name, o, h, w_o, n_rows):
    qw, d = w_o.shape
    tm = ROW_TILE

    def body(o_ref, h_ref, w_ref, out_ref):
        out_ref[...] = h_ref[...] + _dot(o_ref[...], w_ref[...])

    return _call(name, body, (n_rows // tm,),
                 [pl.BlockSpec((tm, qw), lambda i: (i, 0)), pl.BlockSpec((tm, d), lambda i: (i, 0)),
                  pl.BlockSpec((qw, d), lambda i: (0, 0))],
                 pl.BlockSpec((tm, d), lambda i: (i, 0)), _sds((n_rows, d), F32))(o, h, w_o)


def _attn_out_bwd(name, dh, w_o, n_rows):
    qw, d = w_o.shape
    tm = ROW_TILE

    def body(dh_ref, w_ref, do_ref):
        do_ref[...] = _dot_nt(dh_ref[...], w_ref[...])

    return _call(name, body, (n_rows // tm,),
                 [pl.BlockSpec((tm, d), lambda i: (i, 0)), pl.BlockSpec((qw, d), lambda i: (0, 0))],
                 pl.BlockSpec((tm, qw), lambda i: (i, 0)), _sds((n_rows, qw), F32))(dh, w_o)


def _q_bwd(name, dq, qraw, q_gain_t, cos2, sin2, w_q, h, g, dh, n_rows, n_q, hd):
    d, qw = w_q.shape
    tm = ROW_TILE

    sel, sel_t = _head_selectors(n_q, hd)

    def body(dq_ref, raw_ref, qg_ref, c_ref, s_ref, e_ref, et_ref, w_ref, h_ref, g_ref, dh_ref, draw_ref, o_ref, dqg_ref, dg_ref):
        dx, dgain = _head_prep_bwd(raw_ref[...], qg_ref[...], c_ref[...], s_ref[...], e_ref[...], et_ref[...], dq_ref[...], hd)
        draw = dx.astype(BF16)
        draw_ref[...] = draw
        dhn, dg = _rms_bwd(h_ref[...], g_ref[...], _dot_nt(draw, w_ref[...]))
        o_ref[...] = dh_ref[...] + dhn
        first = pl.program_id(0) == 0
        _acc_out(dqg_ref, dgain, first)
        _acc_out(dg_ref, dg, first)

    row = lambda w: pl.BlockSpec((tm, w), lambda i: (i, 0))
    one = lambda w: pl.BlockSpec((1, w), lambda i: (0, 0))
    return _call(name, body, (n_rows // tm,),
                 [row(qw), row(qw), one(qw), row(2 * hd), row(2 * hd), pl.BlockSpec(sel.shape, lambda i: (0, 0)),
                  pl.BlockSpec(sel_t.shape, lambda i: (0, 0)), pl.BlockSpec((d, qw), lambda i: (0, 0)), row(d), one(d), row(d)],
                 [row(qw), row(d), one(qw), one(d)],
                 [_sds((n_rows, qw), BF16), _sds((n_rows, d), F32), _sds((1, qw), F32), _sds((1, d), F32)])(
                     dq, qraw, q_gain_t, cos2, sin2, sel, sel_t, w_q, h, g, dh)


def _kv_bwd(name, dk, dv, kvraw, k_gain_t, cos2, sin2, w_kv, h, g, dh_main, n_rows, n_main, n_kv, hd, after=()):
    d, kvw = w_kv.shape
    kw = n_kv * hd
    tm = ROW_TILE
    n_main_tiles = n_main // tm

    sel, sel_t = _head_selectors(n_kv, hd)

    def body(dk_ref, dv_ref, raw_ref, kg_ref, c_ref, s_ref, e_ref, et_ref, w_ref, h_ref, g_ref, dh_ref, draw_ref, o_ref, dkg_ref,
             dg_ref):
        i = pl.program_id(0)
        dx, dgain = _head_prep_bwd(raw_ref[:, :kw], kg_ref[...], c_ref[...], s_ref[...], e_ref[...], et_ref[...], dk_ref[...], hd)
        draw = jnp.concatenate([dx, dv_ref[...]], axis=1).astype(BF16)
        draw_ref[...] = draw
        dhn, dg = _rms_bwd(h_ref[...], g_ref[...], _dot_nt(draw, w_ref[...]))
        o_ref[...] = jnp.where(i < n_main_tiles, dh_ref[...], 0.0) + dhn
        _acc_out(dkg_ref, dgain, i == 0)
        _acc_out(dg_ref, dg, i == 0)

    row = lambda w: pl.BlockSpec((tm, w), lambda i: (i, 0))
    one = lambda w: pl.BlockSpec((1, w), lambda i: (0, 0))
    return _call(name, body, (n_rows // tm,),
                 [row(kw), row(kw), row(kvw), one(kw), row(2 * hd), row(2 * hd), pl.BlockSpec(sel.shape, lambda i: (0, 0)),
                  pl.BlockSpec(sel_t.shape, lambda i: (0, 0)), pl.BlockSpec((d, kvw), lambda i: (0, 0)), row(d),
                  one(d), pl.BlockSpec((tm, d), lambda i: (jnp.minimum(i, n_main_tiles - 1), 0))],
                 [row(kvw), row(d), one(kw), one(d)],
                 [_sds((n_rows, kvw), BF16), _sds((n_rows, d), F32), _sds((1, kw), F32), _sds((1, d), F32)], after=after)(
                     dk, dv, kvraw, k_gain_t, cos2, sin2, sel, sel_t, w_kv, h, g, dh_main)


def _tn_rms(name, h, g, b, n_rows, out_dtype=F32):
    d = h.shape[1]
    nb = b.shape[1]
    tk = _contract_tile(n_rows)
    return _tn(name, (h, g, b),
               [pl.BlockSpec((tk, d), lambda j, k: (k, 0)), pl.BlockSpec((1, d), lambda j, k: (0, 0)),
                pl.BlockSpec((tk, nb), lambda j, k: (k, 0))],
               lambda j, h_ref, g_ref, b_ref: (_rms(h_ref[...], g_ref[...]), b_ref[...]), 1, d, nb, n_rows, tk, out_dtype=out_dtype)


def _tn_plain(name, a, b, nj, a_cols, b_cols, n_rows, a_fn=None, out_dtype=F32, after=()):
    tk = _contract_tile(n_rows)
    fa = (lambda v: v) if a_fn is None else a_fn
    a_map = (lambda j, k: (k, j)) if a.shape[1] != a_cols else (lambda j, k: (k, 0))
    b_map = (lambda j, k: (k, j)) if b.shape[1] != b_cols else (lambda j, k: (k, 0))
    return _tn(name, (a, b), [pl.BlockSpec((tk, a_cols), a_map), pl.BlockSpec((tk, b_cols), b_map)],
               lambda j, a_ref, b_ref: (fa(a_ref[...]), b_ref[...]), nj, a_cols, b_cols, n_rows, tk, out_dtype=out_dtype,
               after=after)


def _cast_layer(name, a, layer):
    _, r, c = a.shape
    tr = _row_tile(r, 256)

    def body(a_ref, o_ref):
        o_ref[...] = a_ref[0].astype(BF16)

    return _call(name, body, (r // tr,), [pl.BlockSpec((1, tr, c), lambda i: (layer, i, 0))],
                 pl.BlockSpec((tr, c), lambda i: (i, 0)), _sds((r, c), BF16))(a)


def _adamw_math(w, g, m, v):
    c1 = 1.0 - ADAM_B1 ** ADAM_STEP
    c2 = 1.0 - ADAM_B2 ** ADAM_STEP
    nm = ADAM_B1 * m + (1.0 - ADAM_B1) * g
    nv = ADAM_B2 * v + (1.0 - ADAM_B2) * (g * g)
    return -ADAM_LR * ((nm / c1) / (jnp.sqrt(nv / c2) + ADAM_EPS) + ADAM_WD * w), nm, nv


def _adamw(name, w, g, m, v, after=()):
    rows, cols = w.shape
    tr = 128 if rows % 128 == 0 else rows

    def body(w_ref, g_ref, m_ref, v_ref, d_ref, nm_ref, nv_ref):
        d_ref[...], nm_ref[...], nv_ref[...] = _adamw_math(w_ref[...], g_ref[...], m_ref[...], v_ref[...])

    spec = pl.BlockSpec((tr, cols), lambda i: (i, 0))
    return _call(name, body, (rows // tr,), [spec] * 4, [spec] * 3, [_sds((rows, cols), F32)] * 3, after=after)(w, g, m, v)


def _adamw_from_halves(name, w, m, v, sources, half_index, transposed, after=()):
    n_layers, r, c = w.shape
    lanes = 1024
    after = tuple(after)
    if transposed:
        rows_half, tr = c // 2, 128
        grid = (n_layers, r // tr)
        w_spec = pl.BlockSpec((1, tr, c), lambda l, i, s: (l, i, 0))
        g_spec = lambda off: pl.BlockSpec((rows_half, tr), lambda l, i, s: (off // rows_half, i))
    else:
        rows_half = r // 2
        grid = (n_layers, 2)
        w_spec = pl.BlockSpec((1, rows_half, c), lambda l, k, s: (l, k, 0))
        g_spec = lambda off: pl.BlockSpec((rows_half, lanes), lambda l, k, s: (off // rows_half, 0))

    def body(s_ref, w_ref, m_ref, v_ref, t0_ref, o0_ref, t1_ref, o1_ref, *rest):
        g_ref, d_ref, nm_ref, nv_ref = rest[len(after):]
        layer, k, mine = pl.program_id(0), pl.program_id(1), s_ref[0]
        tot = jnp.where(layer == 0, t0_ref[...], t1_ref[...])
        oth = jnp.where(layer == 0, o0_ref[...], o1_ref[...])
        if transposed:
            g = jnp.concatenate([jnp.where(mine == 0, tot, oth), jnp.where(mine == 0, oth, tot)], axis=0).T
        else:
            g = jnp.where(k == mine, tot, oth)
        g_ref[0] = g
        d_ref[0], nm_ref[0], nv_ref[0] = _adamw_math(w_ref[0], g, m_ref[0], v_ref[0])

    (t0, o0, off0), (t1, o1, off1) = sources
    grid_spec = pltpu.PrefetchScalarGridSpec(
        num_scalar_prefetch=1, grid=grid,
        in_specs=[w_spec] * 3 + [g_spec(off0), g_spec(off0), g_spec(off1), g_spec(off1)] + [_ANY] * len(after),
        out_specs=[w_spec] * 4)
    return pl.pallas_call(
        body, name=name, grid_spec=grid_spec, out_shape=[_sds(w.shape, F32)] * 4,
        compiler_params=pltpu.CompilerParams(dimension_semantics=("arbitrary", "arbitrary"),
                                             vmem_limit_bytes=V7X_VMEM_LIMIT))(half_index, w, m, v, t0, o0, t1, o1, *after)


def _position():
    return lax.axis_index("x"), lax.axis_index("y"), lax.axis_index("c")


def _other_chips(x, y):
    return [(1 - x, y), (x, 1 - y), (1 - x, 1 - y)]


def _peers_chips(x, y, c):
    return [(cx, cy, c) for cx, cy in _other_chips(x, y)]


def _peers_sibling(x, y, c):
    return [(x, y, 1 - c)]


def _peers_chips_and_sibling(x, y, c):
    return _peers_chips(x, y, c) + _peers_sibling(x, y, c)


def _comm_call(name, body, n_in, out_shape, scratch, sequencer=None):
    if sequencer is None:
        return pl.pallas_call(
            body, name=name, in_specs=[_HBM] * n_in, out_specs=[_HBM] * len(out_shape), out_shape=out_shape,
            scratch_shapes=list(scratch),
            compiler_params=pltpu.CompilerParams(has_side_effects=True, vmem_limit_bytes=V7X_VMEM_LIMIT))
    collective_id, peers = sequencer

    def seq_body(*refs):
        barrier = pltpu.get_barrier_semaphore()
        plist = peers(*_position())
        for peer in plist:
            pl.semaphore_signal(barrier, inc=1, device_id=peer, device_id_type=MESH)
        pl.semaphore_wait(barrier, len(plist))
        body(*refs)

    return pl.kernel(seq_body, out_type=out_shape, mesh=plsc.ScalarSubcoreMesh(axis_name="sequencer", num_cores=1), name=name,
                     scratch_types=list(scratch), compiler_params=pltpu.CompilerParams(collective_id=collective_id))


def _n_chunks(rows, want, dtype):
    align = 16 if dtype == BF16 else 8
    n = want
    while n > 1 and (rows % n or (rows // n) % align):
        n -= 1
    return n


def _remote(src, dst, send_sem, recv_sem, device):
    return pltpu.make_async_remote_copy(src_ref=src, dst_ref=dst, send_sem=send_sem, recv_sem=recv_sem,
                                        device_id=device, device_id_type=MESH)


def _start_in_chunks(src, dst, send_sem, recv_sem, device, want=8):
    rows = src.shape[0]
    n = _n_chunks(rows, want, src.dtype)
    for i in range(n):
        part = pl.ds(i * (rows // n), rows // n)
        _remote(src.at[part], dst.at[part], send_sem, recv_sem, device).start()


def _all_gather_chips(name, shards, split, collective_id=None):
    n = len(shards)

    def body(*refs):
        ins, outs = refs[:n], refs[n:2 * n]
        send_sems, recv_sems, local_sems = refs[2 * n:]
        x, y, c = _position()
        me = 2 * x + y
        chips = _other_chips(x, y)
        sibling = (x, y, 1 - c)
        sends, forwards = [], []
        for t in range(n):
            pltpu.make_async_copy(ins[t], outs[t].at[me], local_sems.at[t]).start()
        for t in range(n):
            r = ins[t].shape[0]
            rows = pl.ds(c * (r // 2), r // 2) if split[t] else pl.ds(0, r)
            for k, (cx, cy) in enumerate(chips):
                src, dst = ins[t].at[rows], outs[t].at[me, rows]
                _start_in_chunks(src, dst, send_sems.at[t, k], recv_sems.at[t, k], (cx, cy, c), want=4)
                sends.append(_remote(src, dst, send_sems.at[t, k], recv_sems.at[t, k], (cx, cy, c)))
        for t in range(n):
            r = ins[t].shape[0]
            rows = pl.ds(c * (r // 2), r // 2) if split[t] else pl.ds(0, r)
            for k, (cx, cy) in enumerate(chips):
                landed = outs[t].at[2 * cx + cy, rows]
                _remote(landed, landed, send_sems.at[t, k], recv_sems.at[t, k], (cx, cy, c)).wait_recv()
                if split[t]:
                    _start_in_chunks(landed, landed, send_sems.at[t, 3 + k], recv_sems.at[t, 3 + k], sibling, want=4)
                    forwards.append(_remote(landed, landed, send_sems.at[t, 3 + k], recv_sems.at[t, 3 + k], sibling))
        for t in range(n):
            if split[t]:
                r = ins[t].shape[0]
                other = pl.ds((1 - c) * (r // 2), r // 2)
                for k, (cx, cy) in enumerate(chips):
                    landed = outs[t].at[2 * cx + cy, other]
                    pltpu.make_async_remote_copy(
                        src_ref=landed, dst_ref=landed, send_sem=send_sems.at[t, 3 + k], recv_sem=recv_sems.at[t, 3 + k],
                        device_id=sibling, device_id_type=MESH).wait_recv()
        for cp in sends + forwards:
            cp.wait_send()
        for t in range(n):
            pltpu.make_async_copy(ins[t], outs[t].at[me], local_sems.at[t]).wait()

    out_shape = [_sds((N_CHIPS,) + s.shape, s.dtype) for s in shards]
    sequencer = None if collective_id is None else (collective_id, _peers_chips_and_sibling)
    return _comm_call(name, body, n, out_shape,
                      [pltpu.SemaphoreType.DMA((n, 6)), pltpu.SemaphoreType.DMA((n, 6)), pltpu.SemaphoreType.DMA((n,))],
                      sequencer)(*shards)


def _swap_halves_with_sibling(name, blob, collective_id=None):
    def body(b_ref, theirs_ref, send_sem, recv_sem):
        x, y, c = _position()
        sibling = (x, y, 1 - c)
        for k in range(b_ref.shape[1]):
            _start_in_chunks(b_ref.at[1 - c, k], theirs_ref.at[k], send_sem, recv_sem, sibling)
        _remote(b_ref.at[1 - c], theirs_ref, send_sem, recv_sem, sibling).wait()

    return _comm_call(name, body, 1, [_sds(blob.shape[1:], blob.dtype)],
                      [pltpu.SemaphoreType.DMA(()), pltpu.SemaphoreType.DMA(())],
                      None if collective_id is None else (collective_id, _peers_sibling))(blob)[0]


def _scatter_to_chips(name, parts, collective_id=None):
    def body(p_ref, o_ref, send_sems, recv_sems, local_sems):
        x, y, c = _position()
        me = 2 * x + y
        rows = p_ref.shape[1]
        n_loc = _n_chunks(rows, 16, p_ref.dtype)
        locs = [pltpu.make_async_copy(p_ref.at[me, pl.ds(i * (rows // n_loc), rows // n_loc)],
                                      o_ref.at[me, pl.ds(i * (rows // n_loc), rows // n_loc)], local_sems.at[i])
                for i in range(n_loc)]
        for loc in locs:
            loc.start()
        sends = []
        for k, (cx, cy) in enumerate(_other_chips(x, y)):
            src, dst = p_ref.at[2 * cx + cy], o_ref.at[me]
            _start_in_chunks(src, dst, send_sems.at[k], recv_sems.at[k], (cx, cy, c))
            sends.append(_remote(src, dst, send_sems.at[k], recv_sems.at[k], (cx, cy, c)))
        for k, (cx, cy) in enumerate(_other_chips(x, y)):
            landed = o_ref.at[2 * cx + cy]
            _remote(landed, landed, send_sems.at[k], recv_sems.at[k], (cx, cy, c)).wait_recv()
        for cp in sends:
            cp.wait_send()
        for loc in locs:
            loc.wait()

    def local_sems_shape(rows):
        return pltpu.SemaphoreType.DMA((_n_chunks(rows, 16, parts.dtype),))

    return _comm_call(name, body, 1, [_sds(parts.shape, parts.dtype)],
                      [pltpu.SemaphoreType.DMA((3,)), pltpu.SemaphoreType.DMA((3,)), local_sems_shape(parts.shape[1])],
                      None if collective_id is None else (collective_id, _peers_chips))(parts)[0]


def _share_with_sibling(name, mine, collective_id=None):
    def body(m_ref, o_ref, send_sem, recv_sem):
        x, y, c = _position()
        sibling = (x, y, 1 - c)
        _start_in_chunks(m_ref, o_ref, send_sem, recv_sem, sibling, want=16)
        _remote(m_ref, o_ref, send_sem, recv_sem, sibling).wait()

    return _comm_call(name, body, 1, [_sds(mine.shape, mine.dtype)],
                      [pltpu.SemaphoreType.DMA(()), pltpu.SemaphoreType.DMA(())],
                      None if collective_id is None else (collective_id, _peers_sibling))(mine)[0]


def _row_tile(rows, cap=640):
    best = rows
    for t in range(16, min(rows, cap) + 1, 16):
        if rows % t == 0:
            best = t
    return best


_ANY = pl.BlockSpec(memory_space=pl.ANY)


def _add_my_half(name, blob, theirs, half_index, out_dtype, after):
    n, rows, cols = theirs.shape
    tr = _row_tile(rows)
    after = tuple(after)

    def body(c_ref, a_ref, b_ref, *rest):
        o_ref = rest[-1]
        o_ref[...] = (a_ref[0].astype(F32) + b_ref[...].astype(F32)).astype(out_dtype)

    spec = pl.BlockSpec((1, tr, cols), lambda k, i, c: (k, i, 0))
    grid_spec = pltpu.PrefetchScalarGridSpec(
        num_scalar_prefetch=1, grid=(n, rows // tr),
        in_specs=[pl.BlockSpec((1, 1, tr, cols), lambda k, i, c: (c[0], k, i, 0)), spec] + [_ANY] * len(after), out_specs=spec)
    return pl.pallas_call(
        body, name=name, grid_spec=grid_spec, out_shape=_sds(theirs.shape, out_dtype),
        compiler_params=pltpu.CompilerParams(dimension_semantics=("arbitrary", "arbitrary"),
                                             vmem_limit_bytes=V7X_VMEM_LIMIT))(half_index, blob, theirs, *after)


def _sum_slots(name, parts, after):
    n, rows, cols = parts.shape
    tr = _row_tile(rows)

    def body(p_ref, o_ref):
        acc = p_ref[0].astype(F32)
        for k in range(1, n):
            acc = acc + p_ref[k].astype(F32)
        o_ref[...] = acc

    return _call(name, body, (rows // tr,), [pl.BlockSpec((n, tr, cols), lambda i: (0, i, 0))],
                 pl.BlockSpec((tr, cols), lambda i: (i, 0)), _sds((rows, cols), F32), after=after)(parts)


def _reduce_small_adamw(name, grads, loss_tile, ws, ms, vs, after=()):
    n = len(grads)
    srcs = list(grads) + [loss_tile]
    after = tuple(after)

    def body(*refs):
        refs = refs[:4 * n + 1] + refs[4 * n + 1 + len(after):]
        g_in, w_in, m_in, v_in = refs[:n + 1], refs[n + 1:2 * n + 1], refs[2 * n + 1:3 * n + 1], refs[3 * n + 1:4 * n + 1]
        outs = refs[4 * n + 1:8 * n + 2]
        g_out, d_out, nm_out, nv_out, loss_out = outs[:n], outs[n:2 * n], outs[2 * n:3 * n], outs[3 * n:4 * n], outs[4 * n]
        bufs = refs[8 * n + 2:9 * n + 3]
        send_sems, recv_sems = refs[9 * n + 3:]
        x, y, c = _position()
        me = 4 * x + 2 * y + c
        chip = 2 * x + y
        peers = [(1 - x if dlt & 4 else x, 1 - y if dlt & 2 else y, 1 - c if dlt & 1 else c) for dlt in range(1, N_DEV)]
        sends = []
        for t in range(n + 1):
            bufs[t][me] = g_in[t][...]
            for k, peer in enumerate(peers):
                cp = _remote(g_in[t], bufs[t].at[me], send_sems.at[t, k], recv_sems.at[t, k], peer)
                cp.start()
                sends.append(cp)
        for t in range(n + 1):
            for k, (tx, ty, tc) in enumerate(peers):
                landed = bufs[t].at[4 * tx + 2 * ty + tc]
                _remote(landed, landed, send_sems.at[t, k], recv_sems.at[t, k], (tx, ty, tc)).wait_recv()
        for cp in sends:
            cp.wait_send()
        for t in range(n + 1):
            total = bufs[t][0]
            for k in range(1, N_DEV):
                total = total + bufs[t][k]
            if t == n:
                loss_out[...] = total
                continue
            cols = w_in[t].shape[1]
            if cols == total.shape[1]:
                g_out[t][...] = total
                d_out[t][...], nm_out[t][...], nv_out[t][...] = _adamw_math(w_in[t][...], total, m_in[t][...], v_in[t][...])
            else:
                for j in range(N_CHIPS):
                    @pl.when(chip == j)
                    def _(t=t, j=j, cols=cols, total=total):
                        mine = total[:, j * cols:(j + 1) * cols]
                        g_out[t][...] = mine
                        d_out[t][...], nm_out[t][...], nv_out[t][...] = _adamw_math(w_in[t][...], mine, m_in[t][...], v_in[t][...])

    w_shapes = [_sds(a.shape, F32) for a in ws]
    return pl.pallas_call(
        body, name=name, in_specs=[_VMEM] * (4 * n + 1) + [_ANY] * len(after), out_specs=[_VMEM] * (4 * n + 1),
        out_shape=w_shapes * 4 + [_sds(loss_tile.shape, F32)],
        scratch_shapes=[pltpu.VMEM((N_DEV,) + a.shape, F32) for a in srcs]
        + [pltpu.SemaphoreType.DMA((n + 1, N_DEV - 1)), pltpu.SemaphoreType.DMA((n + 1, N_DEV - 1))],
        compiler_params=pltpu.CompilerParams(has_side_effects=True, vmem_limit_bytes=V7X_VMEM_LIMIT))(
            *srcs, *ws, *ms, *vs, *after)


_BIG = ("ffn1_w_gate_up", "ffn1_w_down", "ffn2_w_gate_up", "ffn2_w_down", "ssm_w_in", "ssm_w_out", "w_kv", "attn_w_q", "attn_w_o")
_TRANSPOSED = ("ffn1_w_gate_up", "ffn2_w_gate_up")
_FROM_HALVES = _TRANSPOSED + ("ffn1_w_down", "ffn2_w_down")
_SMALL = ("meta_tokens", "ffn1_norm", "mix_norm", "ffn2_norm", "ssm_lambda_re", "ssm_lambda_im", "ssm_b_re", "ssm_b_im",
          "ssm_c_re", "ssm_c_im", "ssm_log_step", "ssm_d", "kv_norm", "k_norm", "q_norm", "attn_sinks")
_ORDER = ("meta_tokens", "ffn1_norm", "ffn1_w_gate_up", "ffn1_w_down", "mix_norm", "ffn2_norm", "ffn2_w_gate_up", "ffn2_w_down",
          "ssm_w_in", "ssm_lambda_re", "ssm_lambda_im", "ssm_b_re", "ssm_b_im", "ssm_c_re", "ssm_c_im", "ssm_log_step", "ssm_d",
          "ssm_w_out", "kv_norm", "w_kv", "k_norm", "attn_w_q", "q_norm", "attn_sinks", "attn_w_o")


def _step(x, target, w, m, v):
    n_ex, seq, d = x.shape
    n_meta = w["meta_tokens"].shape[0]
    n_main = n_ex * seq
    n_all = n_main + n_ex * META_BLOCK
    n_g, n_p, n_c = w["ssm_b_re"].shape[1:]
    hd = w["k_norm"].shape[0]
    n_kv = w["w_kv"].shape[1] // (2 * hd)
    n_q = w["attn_w_q"].shape[2] // hd
    qpk = n_q // n_kv
    px, py, pc = _position()
    chip = 2 * px + py

    def cast(name, layer=0):
        a = w[name]
        return _cast_layer(f"cast_{name}_{layer}", a if a.ndim == 3 else a[None], layer)

    first = [cast("ffn1_w_gate_up"), cast("ffn1_w_down"), cast("ssm_w_in"), cast("ssm_w_out"), w["meta_tokens"], w["ssm_d"]]
    g_a = _all_gather_chips("gather_first", first, [True, True, True, True, False, False], collective_id=12)
    second = [cast("ffn2_w_gate_up"), cast("ffn2_w_down"), cast("w_kv")]
    g_b = _all_gather_chips("gather_second", second, [True] * 3, collective_id=1)
    third = [cast("ffn1_w_gate_up", 1), cast("ffn1_w_down", 1), cast("attn_w_q"), cast("attn_w_o"),
             cast("ffn2_w_gate_up", 1), cast("ffn2_w_down", 1)]
    g_c = _all_gather_chips("gather_third", third, [True] * 6, collective_id=2)
    wgu = {("ffn1", 0): g_a[0], ("ffn1", 1): g_c[0], ("ffn2", 0): g_b[0], ("ffn2", 1): g_c[4]}
    wd = {("ffn1", 0): g_a[1], ("ffn1", 1): g_c[1], ("ffn2", 0): g_b[1], ("ffn2", 1): g_c[5]}
    wd = {key: a.reshape(-1, d) for key, a in wd.items()}
    w_in = g_a[2].reshape(d, -1)
    wout4 = g_a[3]
    w_q = g_c[2].reshape(d, -1)
    w_o = g_c[3].reshape(-1, d)
    w_kv = g_b[2].reshape(d, -1)
    meta_full = jnp.transpose(g_a[4], (1, 0, 2)).reshape(n_meta, d)
    dskip = g_a[5].reshape(1, -1)

    row1 = lambda a: a.reshape(1, -1)
    ssm_args = tuple(w[k][0] for k in ("ssm_lambda_re", "ssm_lambda_im", "ssm_log_step", "ssm_b_re", "ssm_b_im", "ssm_c_re", "ssm_c_im"))
    (bb, cb, a_re, a_im), ssm_vjp = jax.vjp(_ssm_mats, *ssm_args)
    bb16, cb16 = bb.astype(BF16), cb.astype(BF16)
    a_re_s, a_im_s = lax.stop_gradient(a_re), lax.stop_gradient(a_im)
    half = n_g * n_p // 4
    tabs_f = _scan_tables(a_re_s, a_im_s, False)
    tabs_b = _scan_tables(a_re_s, a_im_s, True)

    freqs = ROPE_THETA ** (-jnp.arange(0, hd // 2, dtype=F32) * 2.0 / hd)
    pos_main = jnp.tile(n_meta + jnp.arange(seq), n_ex)
    pos_meta = jnp.tile(jnp.maximum(jnp.arange(META_BLOCK) - (META_BLOCK - n_meta), 0), n_ex)
    ang = jnp.concatenate([pos_main, pos_meta]).astype(F32)[:, None] * freqs[None, :]
    cos = jnp.concatenate([jnp.cos(ang), jnp.cos(ang)] * 2, axis=1)
    sin_s = jnp.concatenate([-jnp.sin(ang), jnp.sin(ang)] * 2, axis=1)
    k_gain_t = jnp.tile(row1(w["k_norm"]), (1, n_kv))
    score_scale = hd ** -0.5
    q_gain_t = jnp.tile(row1(w["q_norm"][0]), (1, n_q)) * score_scale

    meta_block = jnp.concatenate([jnp.zeros((META_BLOCK - n_meta, d), F32), meta_full], axis=0)
    h0 = jnp.concatenate([x.reshape(n_main, d)] + [meta_block] * n_ex, axis=0)

    g = lambda name, layer: row1(w[name][layer])
    h1, gu1 = _ffn_fwd("l0_ffn1", h0, g("ffn1_norm", 0), wgu["ffn1", 0], wd["ffn1", 0], n_all)
    u, bu = _ssm_in("ssm_in", h1, g("mix_norm", 0), w_in, bb16, n_all)
    xs = _scan_fwd("ssm_scan", bu, tabs_f, n_ex, seq)
    h2, y = _ssm_out("ssm_out", xs, u, dskip, cb16, wout4, h1, n_all)
    h3, gu2 = _ffn_fwd("l0_ffn2", h2, g("ffn2_norm", 0), wgu["ffn2", 0], wd["ffn2", 0], n_all)
    kvraw, k, vv = _kv_proj("kv_proj", h3, row1(w["kv_norm"]), w_kv, k_gain_t, cos, sin_s, n_all, n_kv, hd)
    h4, gu3 = _ffn_fwd("l1_ffn1", h3, g("ffn1_norm", 1), wgu["ffn1", 1], wd["ffn1", 1], n_main)
    qraw, q = _q_proj("q_proj", h4, g("mix_norm", 1), w_q, q_gain_t, cos, sin_s, n_main, n_q, hd)
    sinks = row1(w["attn_sinks"][0])
    o, lse = _attn_fwd("attn_fwd", q, k, vv, sinks, n_ex, seq, n_meta, n_kv, qpk, hd)
    h5 = _attn_out("attn_out", o, h4, w_o, n_main)
    (dh6, loss_tile), gu4 = _ffn_fwd("l1_ffn2", h5, g("ffn2_norm", 1), wgu["ffn2", 1], wd["ffn2", 1], n_main,
                                     target=target.reshape(n_main, d))

    lanes = 1024

    def rs_start(tag, entries, ids):
        pieces = [gr.reshape(N_CHIPS, 2, -1, lanes) for _, _, gr in entries]
        blob = jnp.transpose(jnp.concatenate(pieces, axis=2), (1, 0, 2, 3)).astype(BF16)
        return dict(tag=tag, entries=entries, ids=ids, blob=blob, theirs=_swap_halves_with_sibling(tag + "_swap", blob, ids[0]))

    def rs_scatter(st, after):
        chip_sum = _add_my_half(st["tag"] + "_chip_sum", st["blob"], st["theirs"], jnp.reshape(pc, (1,)).astype(jnp.int32), BF16, after)
        st["chip_sum"] = chip_sum
        st["landed"] = _scatter_to_chips(st["tag"] + "_scatter", chip_sum, st["ids"][1])

    def rs_finish(st, after):
        total = _sum_slots(st["tag"] + "_sum", st["landed"], after)
        st["total"] = total
        other = _share_with_sibling(st["tag"] + "_share", total, st["ids"][2])
        halves = (jnp.where(pc == 0, total, other), jnp.where(pc == 0, other, total))
        out, off = {}, 0
        for name, layer, gr in st["entries"]:
            rows = gr.shape[1] * gr.shape[2] // lanes // 2
            if name in _FROM_HALVES:
                out[name, layer] = (total, other, off)
            else:
                out[name, layer] = jnp.concatenate([hv[off:off + rows].reshape(-1) for hv in halves])
            off += rows
        return out

    small = {}
    dh5, dg_f2l1, dwgu_f2l1, dwd_f2l1 = _ffn_bwd("l1_ffn2", dh6, h5, g("ffn2_norm", 1), gu4, wgu["ffn2", 1], wd["ffn2", 1], n_main)
    do = _attn_out_bwd("attn_out_bwd", dh5, w_o, n_main)
    dw_o = _tn_plain("attn_dwo", o, dh5, 1, o.shape[1], d, n_main, out_dtype=BF16).reshape(N_CHIPS, -1, d)
    dq, dk_main, dv_main, dk_meta, dv_meta, dsinks = _attn_bwd("attn_bwd", q, k, vv, sinks, o, lse, do, n_ex, seq, n_meta, n_kv, qpk, hd)
    dqraw, dh4, dq_gain, dg_mix1 = _q_bwd("q_bwd", dq, qraw, q_gain_t, cos, sin_s, w_q, h4, g("mix_norm", 1), dh5, n_main, n_q, hd)
    dw_q = _tn_rms("attn_dwq", h4, g("mix_norm", 1), dqraw, n_main, out_dtype=BF16).reshape(N_CHIPS, -1, dqraw.shape[1])
    dh3m, dg_f1l1, dwgu_f1l1, dwd_f1l1 = _ffn_bwd("l1_ffn1", dh4, h3, g("ffn1_norm", 1), gu3, wgu["ffn1", 1], wd["ffn1", 1], n_main)
    rs1 = rs_start("rs1", [("ffn2_w_gate_up", 1, dwgu_f2l1), ("ffn1_w_gate_up", 1, dwgu_f1l1), ("ffn2_w_down", 1, dwd_f2l1),
                           ("ffn1_w_down", 1, dwd_f1l1), ("attn_w_o", 0, dw_o), ("attn_w_q", 0, dw_q)], (3, 4, 5))

    def with_meta(main, meta):
        blocks = [jnp.pad(meta[b * n_meta:(b + 1) * n_meta], ((META_BLOCK - n_meta, 0), (0, 0))) for b in range(n_ex)]
        return jnp.concatenate([main] + blocks, axis=0)

    dkvraw, dh3, dk_gain, dg_kv = _kv_bwd("kv_bwd", with_meta(dk_main, dk_meta), with_meta(dv_main, dv_meta), kvraw, k_gain_t,
                                          cos, sin_s, w_kv, h3, row1(w["kv_norm"]), dh3m, n_all, n_main, n_kv, hd,
                                          after=(rs1["blob"],))
    rs_scatter(rs1, after=(dh3,))
    dw_kv = _tn_rms("kv_dw", h3, row1(w["kv_norm"]), dkvraw, n_all, out_dtype=BF16).reshape(N_CHIPS, -1, dkvraw.shape[1])
    dh2, dg_f2l0, dwgu_f2l0, dwd_f2l0 = _ffn_bwd("l0_ffn2", dh3, h2, g("ffn2_norm", 0), gu2, wgu["ffn2", 0], wd["ffn2", 0], n_all,
                                                 after=(rs1["chip_sum"],))
    reduced = rs_finish(rs1, after=(dh2, dwgu_f2l0, dwd_f2l0, dw_kv))
    rs0a = rs_start("rs0a", [("ffn2_w_gate_up", 0, dwgu_f2l0), ("ffn2_w_down", 0, dwd_f2l0), ("w_kv", 0, dw_kv)], (6, 7, 8))

    dy, dz, gx, dd = _ssm_out_bwd("ssm_out_bwd", dh2, y, u, cb16, wout4, n_all, after=(rs1["total"], rs0a["blob"]))
    rs_scatter(rs0a, after=(dy,))
    hw = y.shape[1]
    oc = wout4.shape[2]
    dw_out = _tn_plain("ssm_dwout", y, dz, wout4.shape[0], hw, oc, n_all, a_fn=_gelu, out_dtype=BF16)
    gbu, da = _scan_bwd("ssm_scan_bwd", gx, xs, tabs_b, n_ex, seq, after=(rs0a["chip_sum"],))
    du, dh1, dg_mix0 = _ssm_in_bwd("ssm_in_bwd", gbu, dy, dskip, bb16, w_in, h1, g("mix_norm", 0), dh2, n_all)
    reduced.update(rs_finish(rs0a, after=(dh1,)))
    dw_in = _tn_rms("ssm_dwin", h1, g("mix_norm", 0), du, n_all, out_dtype=BF16).reshape(N_CHIPS, -1, hw)
    (dh0, dh0_meta), dg_f1l0, dwgu_f1l0, dwd_f1l0 = _ffn_bwd("l0_ffn1", dh1, h0, g("ffn1_norm", 0), gu1, wgu["ffn1", 0], wd["ffn1", 0],
                                                             n_all, n_main, after=(rs0a["total"],))
    rs0b = rs_start("rs0b", [("ffn1_w_gate_up", 0, dwgu_f1l0), ("ffn1_w_down", 0, dwd_f1l0), ("ssm_w_out", 0, dw_out),
                             ("ssm_w_in", 0, dw_in)], (9, 10, 11))
    dcb = _tn_plain("ssm_dcb", xs, dy, 4, xs.shape[1] // 4, hw // 4, n_all, after=(rs0b["blob"],))
    rs_scatter(rs0b, after=(dcb,))
    dbb = _tn_plain("ssm_dbb", u, gbu, 4, hw // 4, gbu.shape[1] // 4, n_all, after=(rs0b["chip_sum"],))

    grad_x = dh0.reshape(n_ex, seq, d)
    da_sum = jnp.sum(da, axis=(0, 1)).reshape(4, 2, half)
    d_ssm = ssm_vjp((dbb, dcb, da_sum[:, 0].reshape(-1), da_sum[:, 1].reshape(-1)))
    for key, val in zip(("ssm_lambda_re", "ssm_lambda_im", "ssm_log_step", "ssm_b_re", "ssm_b_im", "ssm_c_re", "ssm_c_im"), d_ssm):
        small[key] = val[None]
    small["meta_tokens"] = sum(dh0_meta[META_BLOCK * (b + 1) - n_meta:META_BLOCK * (b + 1)] for b in range(n_ex))
    small["ffn1_norm"] = jnp.concatenate([dg_f1l0, dg_f1l1], axis=0)
    small["ffn2_norm"] = jnp.concatenate([dg_f2l0, dg_f2l1], axis=0)
    small["mix_norm"] = jnp.concatenate([dg_mix0, dg_mix1], axis=0)
    small["ssm_d"] = dd
    small["kv_norm"] = dg_kv.reshape(-1)
    small["k_norm"] = jnp.sum(dk_gain.reshape(n_kv, hd), axis=0)
    small["q_norm"] = jnp.sum(dq_gain.reshape(n_q, hd), axis=0, keepdims=True) * score_scale
    small["attn_sinks"] = dsinks

    def view(name, a):
        if name in ("ssm_b_re", "ssm_b_im"):
            return a.reshape(-1, 128)
        return a.reshape(1, -1) if a.ndim == 1 else a.reshape(-1, a.shape[-1])

    grads, deltas, new_m, new_v = {}, {}, {}, {}

    def adamw_matrix(name, after=()):
        shape = w[name].shape
        if name in _FROM_HALVES:
            grads[name], deltas[name], new_m[name], new_v[name] = _adamw_from_halves(
                "adamw_" + name, w[name], m[name], v[name], [reduced[name, 0], reduced[name, 1]],
                jnp.reshape(pc, (1,)).astype(jnp.int32), name in _TRANSPOSED, after=after)
            return new_v[name]
        layers = [reduced[name, layer] for layer in range(2) if (name, layer) in reduced]
        grads[name] = jnp.concatenate(layers).reshape(shape)
        two_d = lambda a: a.reshape(-1, shape[-1])
        dl, nm, nv = _adamw("adamw_" + name, two_d(w[name]), two_d(grads[name]), two_d(m[name]), two_d(v[name]), after=after)
        deltas[name], new_m[name], new_v[name] = dl.reshape(shape), nm.reshape(shape), nv.reshape(shape)
        return nv

    placed = (rs0b["chip_sum"],)
    for name in ("ffn2_w_down", "attn_w_o", "attn_w_q", "w_kv"):
        placed = (adamw_matrix(name, after=placed),)
    tail = _reduce_small_adamw("small_tail", [view(k, small[k]) for k in _SMALL], loss_tile,
                               *[[view(k, t[k]) for k in _SMALL] for t in (w, m, v)], after=placed)
    n_small = len(_SMALL)
    for i, k in enumerate(_SMALL):
        grads[k], deltas[k] = tail[i].reshape(w[k].shape), tail[n_small + i].reshape(w[k].shape)
        new_m[k], new_v[k] = tail[2 * n_small + i].reshape(w[k].shape), tail[3 * n_small + i].reshape(w[k].shape)
    loss = jnp.sum(tail[-1])
    reduced.update(rs_finish(rs0b, after=(tail[-1],)))
    adamw_matrix("ffn2_w_gate_up", after=(rs0b["total"],))
    for name in ("ffn1_w_gate_up", "ffn1_w_down", "ssm_w_in", "ssm_w_out"):
        adamw_matrix(name)
    return (loss, grad_x, *[grads[k] for k in _ORDER], *[deltas[k] for k in _ORDER], *[new_m[k] for k in _ORDER],
            *[new_v[k] for k in _ORDER])


def kernel(x, meta_tokens, ffn1_norm, ffn1_w_gate_up, ffn1_w_down, mix_norm, ffn2_norm, ffn2_w_gate_up, ffn2_w_down, ssm_w_in, ssm_lambda_re, ssm_lambda_im, ssm_b_re, ssm_b_im, ssm_c_re, ssm_c_im, ssm_log_step, ssm_d, ssm_w_out, kv_norm, w_kv, k_norm, attn_w_q, q_norm, attn_sinks, attn_w_o, loss_target, m_meta_tokens, m_ffn1_norm, m_ffn1_w_gate_up, m_ffn1_w_down, m_mix_norm, m_ffn2_norm, m_ffn2_w_gate_up, m_ffn2_w_down, m_ssm_w_in, m_ssm_lambda_re, m_ssm_lambda_im, m_ssm_b_re, m_ssm_b_im, m_ssm_c_re, m_ssm_c_im, m_ssm_log_step, m_ssm_d, m_ssm_w_out, m_kv_norm, m_w_kv, m_k_norm, m_attn_w_q, m_q_norm, m_attn_sinks, m_attn_w_o, v_meta_tokens, v_ffn1_norm, v_ffn1_w_gate_up, v_ffn1_w_down, v_mix_norm, v_ffn2_norm, v_ffn2_w_gate_up, v_ffn2_w_down, v_ssm_w_in, v_ssm_lambda_re, v_ssm_lambda_im, v_ssm_b_re, v_ssm_b_im, v_ssm_c_re, v_ssm_c_im, v_ssm_log_step, v_ssm_d, v_ssm_w_out, v_kv_norm, v_w_kv, v_k_norm, v_attn_w_q, v_q_norm, v_attn_sinks, v_attn_w_o):
    args = locals()
    w = {k: args[k] for k in _ORDER}
    m = {k: args["m_" + k] for k in _ORDER}
    v = {k: args["v_" + k] for k in _ORDER}
    return _step(x, loss_target, w, m, v)
```

```python
import functools
import math

import jax
import jax.numpy as jnp
from jax import lax
from jax.experimental import pallas as pl
from jax.experimental.pallas import tpu as pltpu
from jax.experimental.pallas import tpu_sc as plsc

F32 = jnp.float32
BF16 = jnp.bfloat16
MESH = pl.DeviceIdType.MESH

EPS = 1e-6
NEG_INF = -1e30
ROPE_THETA = 10000.0
WINDOW = 128
META_BLOCK = 128
ROW_TILE = 256
SUBLANES = 8
V7X_VMEM_LIMIT = 56 * 2**20
N_CHIPS = 4
N_DEV = 8

ADAM_LR = 0.001
ADAM_B1 = 0.9
ADAM_B2 = 0.999
ADAM_EPS = 1e-08
ADAM_WD = 0.01
ADAM_STEP = 10

_HBM = pl.BlockSpec(memory_space=pltpu.HBM)
_VMEM = pl.BlockSpec(memory_space=pltpu.VMEM)


def _call(name, body, grid, in_specs, out_specs, out_shape, scratch=(), after=()):
    after = tuple(after)
    n_in = len(in_specs)

    def wrapped(*refs):
        return body(*refs[:n_in], *refs[n_in + len(after):])

    call = pl.pallas_call(
        wrapped, name=name, grid=grid, in_specs=list(in_specs) + [pl.BlockSpec(memory_space=pl.ANY)] * len(after),
        out_specs=out_specs, out_shape=out_shape, scratch_shapes=list(scratch),
        compiler_params=pltpu.CompilerParams(dimension_semantics=("arbitrary",) * len(grid),
                                             vmem_limit_bytes=V7X_VMEM_LIMIT))
    return lambda *operands: call(*operands, *after)


def _sds(shape, dtype):
    return jax.ShapeDtypeStruct(tuple(shape), dtype)


def _dot(a, b):
    return jnp.dot(a.astype(BF16), b.astype(BF16), preferred_element_type=F32)


def _dot_nt(a, b):
    return lax.dot_general(a.astype(BF16), b.astype(BF16), (((1,), (1,)), ((), ())), preferred_element_type=F32)


def _dot_tn(a, b):
    return lax.dot_general(a.astype(BF16), b.astype(BF16), (((0,), (0,)), ((), ())), preferred_element_type=F32)


def _rms(h, g):
    return h * lax.rsqrt(jnp.mean(h * h, axis=-1, keepdims=True) + EPS) * g


def _rms_bwd(h, g, dn):
    r = lax.rsqrt(jnp.mean(h * h, axis=-1, keepdims=True) + EPS)
    xh = h * r
    dxh = dn * g
    dg = jnp.sum(dn * xh, axis=0, keepdims=True)
    dh = r * (dxh - xh * jnp.mean(dxh * xh, axis=-1, keepdims=True))
    return dh, dg


def _sigmoid(x):
    return 0.5 * jnp.tanh(0.5 * x) + 0.5


def _gelu(y):
    k = math.sqrt(2.0 / math.pi)
    return 0.5 * y * (1.0 + jnp.tanh(k * (y + 0.044715 * y * y * y)))


def _gelu_grad(y):
    k = math.sqrt(2.0 / math.pi)
    t = jnp.tanh(k * (y + 0.044715 * y * y * y))
    return 0.5 * (1.0 + t) + 0.5 * y * (1.0 - t * t) * k * (1.0 + 3.0 * 0.044715 * y * y)


def _partner(x, lane, d):
    width = x.shape[-1]
    return jnp.where((lane & d) == 0, pltpu.roll(x, width - d, 1), pltpu.roll(x, d, 1))


def _split_bf16(x):
    hi = x.astype(BF16)
    return hi, (x - hi.astype(F32)).astype(BF16)


def _head_sums(x, sel):
    hi, lo = _split_bf16(x)
    return jnp.dot(hi, sel, preferred_element_type=F32) + jnp.dot(lo, sel, preferred_element_type=F32)


def _head_expand(v, sel_t):
    hi, lo = _split_bf16(v)
    return jnp.dot(hi, sel_t, preferred_element_type=F32) + jnp.dot(lo, sel_t, preferred_element_type=F32)


def _tile_lanes(t, width):
    return jnp.concatenate([t] * (width // t.shape[-1]), axis=1)


def _head_prep(x, gain_t, cos2, sin2, sel, sel_t, hd):
    width = x.shape[-1]
    lane = lax.broadcasted_iota(jnp.int32, x.shape, 1)
    r = _head_expand(lax.rsqrt(_head_sums(x * x, sel) * (1.0 / hd) + EPS), sel_t)
    y = x * r * gain_t
    return y * _tile_lanes(cos2, width) + _partner(y, lane, hd // 2) * _tile_lanes(sin2, width)


def _head_prep_bwd(x, gain_t, cos2, sin2, sel, sel_t, d_out, hd):
    width = x.shape[-1]
    lane = lax.broadcasted_iota(jnp.int32, x.shape, 1)
    r = _head_expand(lax.rsqrt(_head_sums(x * x, sel) * (1.0 / hd) + EPS), sel_t)
    xhat = x * r
    dy = d_out * _tile_lanes(cos2, width) + _partner(d_out * _tile_lanes(sin2, width), lane, hd // 2)
    dgain = jnp.sum(dy * xhat, axis=0, keepdims=True)
    dxh = dy * gain_t
    mean = _head_expand(_head_sums(dxh * xhat, sel) * (1.0 / hd), sel_t)
    return r * (dxh - xhat * mean), dgain


def _head_selectors(n_heads, hd):
    sel = (jnp.arange(n_heads * hd)[:, None] // hd == jnp.arange(128)[None, :]).astype(BF16)
    return sel, sel.T


def _acc_out(ref, val, first):
    @pl.when(first)
    def _():
        ref[...] = jnp.zeros_like(ref)
    ref[...] += val


def _wide_row_tile(n_rows, cap=512):
    best = 128
    for t in range(128, cap + 1, 128):
        if n_rows % t == 0:
            best = t
    return best


def _ffn_up(name, h, g, w4, n_rows):
    nj, d, fc = w4.shape
    tm = _wide_row_tile(n_rows)

    def body(h_ref, g_ref, w_ref, o_ref, n_ref):
        n = _rms(h_ref[...], g_ref[...]).astype(BF16)
        n_ref[...] = n
        for j in range(nj):
            o_ref[:, j * fc:(j + 1) * fc] = _dot(n, w_ref[j]).astype(BF16)

    return _call(name, body, (n_rows // tm,),
                 [pl.BlockSpec((tm, d), lambda i: (i, 0)), pl.BlockSpec((1, d), lambda i: (0, 0)),
                  pl.BlockSpec((nj, d, fc), lambda i: (0, 0, 0))],
                 [pl.BlockSpec((tm, nj * fc), lambda i: (i, 0)), pl.BlockSpec((tm, d), lambda i: (i, 0))],
                 [_sds((n_rows, nj * fc), BF16), _sds((n_rows, d), BF16)])(h, g, w4)


def _ffn_down(name, gu, h, wd, n_rows, target=None):
    f, d = wd.shape
    tm = ROW_TILE

    def body(gu_ref, h_ref, w_ref, *rest):
        half_a = gu_ref[:, :f] * 0.5
        s = (half_a + half_a * jnp.tanh(half_a)) * gu_ref[:, f:]
        y = h_ref[...] + 0.5 * _dot(s, w_ref[...])
        if target is None:
            o_ref, s_ref = rest
            o_ref[...] = y
        else:
            t_ref, dy_ref, l_ref, s_ref = rest
            e = y - t_ref[...]
            dy_ref[...] = e * (1.0 / d)
            e2 = jnp.sum((e * e).reshape(tm // SUBLANES, SUBLANES, d), axis=0)
            part = e2[:, 0:128]
            for k in range(1, d // 128):
                part = part + e2[:, k * 128:(k + 1) * 128]
            _acc_out(l_ref, part * (0.5 / d), pl.program_id(0) == 0)
        s_ref[...] = s

    row = lambda width: pl.BlockSpec((tm, width), lambda i: (i, 0))
    in_specs = [row(2 * f), row(d), pl.BlockSpec((f, d), lambda i: (0, 0))]
    if target is None:
        return _call(name, body, (n_rows // tm,), in_specs, [row(d), row(f)],
                     [_sds((n_rows, d), F32), _sds((n_rows, f), BF16)])(gu, h, wd)
    return _call(name, body, (n_rows // tm,), in_specs + [row(d)],
                 [row(d), pl.BlockSpec((SUBLANES, 128), lambda i: (0, 0)), row(f)],
                 [_sds((n_rows, d), F32), _sds((SUBLANES, 128), F32), _sds((n_rows, f), BF16)])(gu, h, wd, target)


def _ffn_dgu(name, dh, gu, wd, n_rows, after=()):
    f, d = wd.shape
    tm = ROW_TILE

    def body(dh_ref, gu_ref, w_ref, o_ref):
        ds = _dot_nt(0.5 * dh_ref[...], w_ref[...]).astype(BF16)
        half_a = gu_ref[:, :f] * 0.5
        t = jnp.tanh(half_a)
        o_ref[:, :f] = ds * gu_ref[:, f:] * ((1.0 + t + half_a * (1.0 - t * t)) * 0.5)
        o_ref[:, f:] = ds * (half_a + half_a * t)

    return _call(name, body, (n_rows // tm,),
                 [pl.BlockSpec((tm, d), lambda i: (i, 0)), pl.BlockSpec((tm, 2 * f), lambda i: (i, 0)),
                  pl.BlockSpec((f, d), lambda i: (0, 0))],
                 pl.BlockSpec((tm, 2 * f), lambda i: (i, 0)), _sds((n_rows, 2 * f), BF16), after=after)(dh, gu, wd)


def _ffn_dh(name, dgu, h, g, dh, w4, n_rows, n_main=None, after=()):
    nj, d, fc = w4.shape
    tm = _wide_row_tile(n_rows) if n_main is None else ROW_TILE
    n_first = (n_rows if n_main is None else n_main) // tm

    def body(dgu_ref, h_ref, g_ref, dh_ref, w_ref, o_ref, *rest):
        dg_ref = rest[-1]
        i = pl.program_id(0)
        dn = _dot_nt(dgu_ref[:, 0:fc], w_ref[0])
        for j in range(1, nj):
            dn = dn + _dot_nt(dgu_ref[:, j * fc:(j + 1) * fc], w_ref[j])
        dhn, dg = _rms_bwd(h_ref[...], g_ref[...], dn)
        val = dh_ref[...] + dhn
        if n_main is None:
            o_ref[...] = val
        else:
            @pl.when(i < n_first)
            def _():
                o_ref[...] = val

            @pl.when(i >= n_first)
            def _():
                rest[0][...] = val
        _acc_out(dg_ref, dg, i == 0)

    out_specs = [pl.BlockSpec((tm, d), lambda i: (jnp.minimum(i, n_first - 1), 0))]
    out_shape = [_sds((n_first * tm, d), F32)]
    if n_main is not None:
        out_specs.append(pl.BlockSpec((tm, d), lambda i: (jnp.maximum(i - n_first, 0), 0)))
        out_shape.append(_sds((n_rows - n_main, d), F32))
    return _call(name, body, (n_rows // tm,),
                 [pl.BlockSpec((tm, nj * fc), lambda i: (i, 0)), pl.BlockSpec((tm, d), lambda i: (i, 0)),
                  pl.BlockSpec((1, d), lambda i: (0, 0)), pl.BlockSpec((tm, d), lambda i: (i, 0)),
                  pl.BlockSpec((nj, d, fc), lambda i: (0, 0, 0), pipeline_mode=pl.Buffered(1))],
                 out_specs + [pl.BlockSpec((1, d), lambda i: (0, 0))],
                 out_shape + [_sds((1, d), F32)], after=after)(dgu, h, g, dh, w4)


def _contract_tile(n_rows, cap=1024):
    best = ROW_TILE
    for t in range(ROW_TILE, cap + 1, ROW_TILE):
        if n_rows % t == 0:
            best = t
    return best


def _tn(name, operands, in_specs, prologue, nj, ma, nb, n_rows, tk, out_dtype=F32, after=(), side_by_side=False):
    n_k = n_rows // tk
    out_spec = pl.BlockSpec((1, ma, nb), lambda j, k: (j, 0, 0))
    if out_dtype == F32 and not side_by_side:
        def body(*refs):
            o_ref = refs[-1]
            a, b = prologue(pl.program_id(0), *refs[:-1])
            _acc_out(o_ref, _dot_tn(a, b)[None], pl.program_id(1) == 0)

        return _call(name, body, (nj, n_k), in_specs, out_spec, _sds((nj, ma, nb), F32), after=after)(*operands)

    def body_rounded(*refs):
        o_ref, acc_ref = refs[-2:]
        a, b = prologue(pl.program_id(0), *refs[:-2])
        _acc_out(acc_ref, _dot_tn(a, b), pl.program_id(1) == 0)

        @pl.when(pl.program_id(1) == n_k - 1)
        def _():
            o_ref[...] = acc_ref[...].astype(out_dtype).reshape(o_ref.shape)

    if side_by_side:
        out_spec, out_shape = pl.BlockSpec((ma, nb), lambda j, k: (0, j)), _sds((ma, nj * nb), out_dtype)
    else:
        out_shape = _sds((nj, ma, nb), out_dtype)
    return _call(name, body_rounded, (nj, n_k), in_specs, out_spec, out_shape,
                 scratch=[pltpu.VMEM((ma, nb), F32)], after=after)(*operands)


def _ffn_dwgu(name, n, dgu, nj, n_rows):
    d = n.shape[1]
    fc = dgu.shape[1] // nj
    tk = _contract_tile(n_rows, cap=2816)
    return _tn(name, (dgu, n),
               [pl.BlockSpec((tk, fc), lambda j, k: (k, j)), pl.BlockSpec((tk, d), lambda j, k: (k, 0))],
               lambda j, a_ref, b_ref: (a_ref[...], b_ref[...]), nj, fc, d, n_rows, tk, out_dtype=BF16)


def _ffn_dwd(name, s, dh, n_rows):
    f = s.shape[1]
    d = dh.shape[1]
    tk = _contract_tile(n_rows, cap=2048)
    halves = 2 if tk > 1024 else 1
    return _tn(name, (s, dh),
               [pl.BlockSpec((tk, f), lambda j, k: (k, 0)), pl.BlockSpec((tk, d // halves), lambda j, k: (k, j))],
               lambda j, s_ref, dh_ref: (s_ref[...], 0.5 * dh_ref[...]), halves, f, d // halves, n_rows, tk, out_dtype=BF16,
               side_by_side=True)


def _ffn_fwd(tag, h, g, w4, wd, n_rows, target=None):
    gu, n = _ffn_up(tag + "_up", h, g, w4, n_rows)
    *out, s = _ffn_down(tag + "_down", gu, h, wd, n_rows, target)
    return (out[0] if target is None else tuple(out)), (gu, n, s)


def _ffn_bwd(tag, dh_out, h, g, saved, w4, wd, n_rows, n_main=None, after=()):
    gu, n, s = saved
    nj = w4.shape[0]
    f, d = wd.shape
    dgu = _ffn_dgu(tag + "_dgu", dh_out, gu, wd, n_rows, after=after)
    dwd = _ffn_dwd(tag + "_dwd", s, dh_out, n_rows).reshape(N_CHIPS, f // N_CHIPS, d)
    *dh_parts, dg = _ffn_dh(tag + "_dh", dgu, h, g, dh_out, w4, n_rows, n_main)
    dwgu = _ffn_dwgu(tag + "_dwgu", n, dgu, nj, n_rows)
    dh_in = dh_parts[0] if n_main is None else tuple(dh_parts)
    return dh_in, dg, dwgu, dwd


def _ssm_in(name, h, g, w_in, bb, n_rows):
    d, hw = w_in.shape
    nj, uc, xc = bb.shape
    tm = ROW_TILE

    def body(h_ref, g_ref, w_ref, bb_ref, u_ref, bu_ref):
        u = _dot(_rms(h_ref[...], g_ref[...]), w_ref[...])
        u_ref[...] = u
        for j in range(nj):
            bu_ref[:, j * xc:(j + 1) * xc] = _dot(u[:, j * uc:(j + 1) * uc], bb_ref[j]).astype(BF16)

    return _call(name, body, (n_rows // tm,),
                 [pl.BlockSpec((tm, d), lambda i: (i, 0)), pl.BlockSpec((1, d), lambda i: (0, 0)),
                  pl.BlockSpec((d, hw), lambda i: (0, 0)), pl.BlockSpec((nj, uc, xc), lambda i: (0, 0, 0))],
                 [pl.BlockSpec((tm, hw), lambda i: (i, 0)), pl.BlockSpec((tm, nj * xc), lambda i: (i, 0))],
                 [_sds((n_rows, hw), F32), _sds((n_rows, nj * xc), BF16)])(h, g, w_in, bb)


def _cmul_add(xr, xi, ar, ai, sr, si):
    return xr + ar * sr - ai * si, xi + ar * si + ai * sr


def _scan_row_block(n_main_blocks, seq_blocks):
    return lambda b, i: jnp.where(i == 0, n_main_blocks + b, b * seq_blocks + i - 1)


def _scan_fwd(name, bu, tabs, n_ex, seq):
    n_rows, width = bu.shape
    nj = 4
    cw = width // nj
    half = cw // 2
    tq = META_BLOCK
    seq_blocks = seq // tq
    rb = _scan_row_block(n_ex * seq_blocks, seq_blocks)

    def body(bu_ref, tab_ref, x_ref, carry_ref):
        @pl.when(pl.program_id(1) == 0)
        def _():
            carry_ref[...] = jnp.zeros_like(carry_ref)

        for j in range(nj):
            re, im = slice(j * cw, j * cw + half), slice(j * cw + half, (j + 1) * cw)
            ch = slice(j * half, (j + 1) * half)

            def blk(k, c, re=re, im=im, ch=ch):
                t = [tab_ref[n * SUBLANES:(n + 1) * SUBLANES, ch] for n in range(8)]
                r0 = pl.multiple_of(k * SUBLANES, SUBLANES)
                xr = bu_ref[pl.ds(r0, SUBLANES), re].astype(F32)
                xi = bu_ref[pl.ds(r0, SUBLANES), im].astype(F32)
                for s, d in enumerate((1, 2, 4)):
                    xr, xi = _cmul_add(xr, xi, t[2 * s], t[2 * s + 1], pltpu.roll(xr, d, 0), pltpu.roll(xi, d, 0))
                xr, xi = _cmul_add(xr, xi, t[6], t[7], c[0], c[1])
                x_ref[pl.ds(r0, SUBLANES), re] = xr.astype(BF16)
                x_ref[pl.ds(r0, SUBLANES), im] = xi.astype(BF16)
                last = SUBLANES - 1
                return (jnp.broadcast_to(xr[last:last + 1, :], xr.shape), jnp.broadcast_to(xi[last:last + 1, :], xi.shape))

            c = lax.fori_loop(0, tq // SUBLANES, blk, (carry_ref[0, :, ch], carry_ref[1, :, ch]), unroll=2)
            carry_ref[0, :, ch] = c[0]
            carry_ref[1, :, ch] = c[1]

    return _call(name, body, (n_ex, seq_blocks + 1),
                 [pl.BlockSpec((tq, width), lambda b, i: (rb(b, i), 0)), pl.BlockSpec((8 * SUBLANES, nj * half), lambda b, i: (0, 0))],
                 pl.BlockSpec((tq, width), lambda b, i: (rb(b, i), 0)), _sds((n_rows, width), BF16),
                 scratch=[pltpu.VMEM((2, SUBLANES, nj * half), F32)])(bu, tabs)


def _scan_bwd(name, gx, x, tabs, n_ex, seq, after=()):
    n_rows, width = gx.shape
    nj = 4
    cw = width // nj
    half = cw // 2
    tq = META_BLOCK
    seq_blocks = seq // tq
    n_steps = seq_blocks + 1
    rb = _scan_row_block(n_ex * seq_blocks, seq_blocks)
    rbr = lambda b, i: rb(b, n_steps - 1 - i)

    def body(gx_ref, x_ref, tab_ref, g_ref, da_ref, carry_ref):
        @pl.when(pl.program_id(1) == 0)
        def _():
            carry_ref[...] = jnp.zeros_like(carry_ref)
            da_ref[...] = jnp.zeros_like(da_ref)
        row = lax.broadcasted_iota(jnp.int32, (SUBLANES, half), 0)
        n_blk = tq // SUBLANES

        for j in range(nj):
            re, im = slice(j * cw, j * cw + half), slice(j * cw + half, (j + 1) * cw)
            ch = slice(j * half, (j + 1) * half)

            def blk(kk, st, re=re, im=im, ch=ch):
                t = [tab_ref[n * SUBLANES:(n + 1) * SUBLANES, ch] for n in range(8)]
                cr, ci, dar, dai = st
                r0 = pl.multiple_of((n_blk - 1 - kk) * SUBLANES, SUBLANES)
                gr = gx_ref[pl.ds(r0, SUBLANES), re].astype(F32)
                gi = gx_ref[pl.ds(r0, SUBLANES), im].astype(F32)
                for s, d in enumerate((1, 2, 4)):
                    gr, gi = _cmul_add(gr, gi, t[2 * s], t[2 * s + 1],
                                       pltpu.roll(gr, SUBLANES - d, 0), pltpu.roll(gi, SUBLANES - d, 0))
                gr, gi = _cmul_add(gr, gi, t[6], t[7], cr, ci)
                g_ref[pl.ds(r0, SUBLANES), re] = gr.astype(BF16)
                g_ref[pl.ds(r0, SUBLANES), im] = gi.astype(BF16)
                hr = jnp.where(row == SUBLANES - 1, cr, pltpu.roll(gr, SUBLANES - 1, 0))
                hi = jnp.where(row == SUBLANES - 1, ci, pltpu.roll(gi, SUBLANES - 1, 0))
                xr = x_ref[pl.ds(r0, SUBLANES), re].astype(F32)
                xi = x_ref[pl.ds(r0, SUBLANES), im].astype(F32)
                dar = dar + xr * hr + xi * hi
                dai = dai + xr * hi - xi * hr
                return (jnp.broadcast_to(gr[0:1, :], gr.shape), jnp.broadcast_to(gi[0:1, :], gi.shape), dar, dai)

            st = lax.fori_loop(0, n_blk, blk, (carry_ref[0, :, ch], carry_ref[1, :, ch], da_ref[0, :, re], da_ref[0, :, im]),
                               unroll=2)
            carry_ref[0, :, ch] = st[0]
            carry_ref[1, :, ch] = st[1]
            da_ref[0, :, re] = st[2]
            da_ref[0, :, im] = st[3]

    return _call(name, body, (n_ex, n_steps),
                 [pl.BlockSpec((tq, width), lambda b, i: (rbr(b, i), 0)), pl.BlockSpec((tq, width), lambda b, i: (rbr(b, i), 0)),
                  pl.BlockSpec((8 * SUBLANES, nj * half), lambda b, i: (0, 0))],
                 [pl.BlockSpec((tq, width), lambda b, i: (rbr(b, i), 0)), pl.BlockSpec((1, SUBLANES, width), lambda b, i: (b, 0, 0))],
                 [_sds((n_rows, width), BF16), _sds((n_ex, SUBLANES, width), F32)],
                 scratch=[pltpu.VMEM((2, SUBLANES, nj * half), F32)], after=after)(gx, x, tabs)


def _ssm_z(gy, wout_ref, nj):
    return jnp.concatenate([_dot(gy, wout_ref[j]) for j in range(nj)], axis=1)


def _ssm_out(name, x, u, dskip, cb, wout4, h, n_rows):
    nj, xc, uc = cb.shape
    no, hw, oc = wout4.shape
    d = h.shape[1]
    tm = ROW_TILE

    def body(x_ref, u_ref, ds_ref, cb_ref, w_ref, h_ref, o_ref, y_ref):
        y = jnp.concatenate([_dot(x_ref[:, j * xc:(j + 1) * xc], cb_ref[j]) for j in range(nj)], axis=1)
        y = y + ds_ref[...] * u_ref[...]
        y_ref[...] = y
        z = _ssm_z(_gelu(y), w_ref, no)
        o_ref[...] = h_ref[...] + z[:, :d] * _sigmoid(z[:, d:])

    return _call(name, body, (n_rows // tm,),
                 [pl.BlockSpec((tm, nj * xc), lambda i: (i, 0)), pl.BlockSpec((tm, hw), lambda i: (i, 0)),
                  pl.BlockSpec((1, hw), lambda i: (0, 0)), pl.BlockSpec((nj, xc, uc), lambda i: (0, 0, 0)),
                  pl.BlockSpec((no, hw, oc), lambda i: (0, 0, 0)), pl.BlockSpec((tm, d), lambda i: (i, 0))],
                 [pl.BlockSpec((tm, d), lambda i: (i, 0)), pl.BlockSpec((tm, hw), lambda i: (i, 0))],
                 [_sds((n_rows, d), F32), _sds((n_rows, hw), F32)])(x, u, dskip, cb, wout4, h)


def _ssm_out_bwd(name, dh, y, u, cb, wout4, n_rows, after=()):
    nj, xc, uc = cb.shape
    no, hw, oc = wout4.shape
    d = dh.shape[1]
    tm = ROW_TILE

    def body(dh_ref, y_ref, u_ref, cb_ref, w_ref, dy_ref, dz_ref, gx_ref, dd_ref):
        y = y_ref[...]
        z = _ssm_z(_gelu(y), w_ref, no)
        za = z[:, :d]
        sg = _sigmoid(z[:, d:])
        dmix = dh_ref[...]
        dz = jnp.concatenate([dmix * sg, dmix * za * sg * (1.0 - sg)], axis=1).astype(BF16)
        dz_ref[...] = dz
        dgy = _dot_nt(dz[:, 0:oc], w_ref[0])
        for j in range(1, no):
            dgy = dgy + _dot_nt(dz[:, j * oc:(j + 1) * oc], w_ref[j])
        dy = dgy * _gelu_grad(y)
        dy_ref[...] = dy
        _acc_out(dd_ref, jnp.sum(dy * u_ref[...], axis=0, keepdims=True), pl.program_id(0) == 0)
        for j in range(nj):
            gx_ref[:, j * xc:(j + 1) * xc] = _dot_nt(dy[:, j * uc:(j + 1) * uc], cb_ref[j]).astype(BF16)

    return _call(name, body, (n_rows // tm,),
                 [pl.BlockSpec((tm, d), lambda i: (i, 0)), pl.BlockSpec((tm, hw), lambda i: (i, 0)),
                  pl.BlockSpec((tm, hw), lambda i: (i, 0)), pl.BlockSpec((nj, xc, uc), lambda i: (0, 0, 0)),
                  pl.BlockSpec((no, hw, oc), lambda i: (0, 0, 0))],
                 [pl.BlockSpec((tm, hw), lambda i: (i, 0)), pl.BlockSpec((tm, no * oc), lambda i: (i, 0)),
                  pl.BlockSpec((tm, nj * xc), lambda i: (i, 0)), pl.BlockSpec((1, hw), lambda i: (0, 0))],
                 [_sds((n_rows, hw), F32), _sds((n_rows, no * oc), BF16), _sds((n_rows, nj * xc), BF16),
                  _sds((1, hw), F32)], after=after)(dh, y, u, cb, wout4)


def _ssm_in_bwd(name, gbu, dy, dskip, bb, w_in, h, g, dh, n_rows):
    nj, uc, xc = bb.shape
    d, hw = w_in.shape
    tm = ROW_TILE

    def body(gb_ref, dy_ref, ds_ref, bb_ref, w_ref, h_ref, g_ref, dh_ref, du_ref, o_ref, dg_ref):
        du = jnp.concatenate([_dot_nt(gb_ref[:, j * xc:(j + 1) * xc], bb_ref[j]) for j in range(nj)], axis=1)
        du = du + dy_ref[...] * ds_ref[...]
        du_ref[...] = du.astype(BF16)
        dhn, dg = _rms_bwd(h_ref[...], g_ref[...], _dot_nt(du, w_ref[...]))
        o_ref[...] = dh_ref[...] + dhn
        _acc_out(dg_ref, dg, pl.program_id(0) == 0)

    return _call(name, body, (n_rows // tm,),
                 [pl.BlockSpec((tm, nj * xc), lambda i: (i, 0)), pl.BlockSpec((tm, hw), lambda i: (i, 0)),
                  pl.BlockSpec((1, hw), lambda i: (0, 0)), pl.BlockSpec((nj, uc, xc), lambda i: (0, 0, 0)),
                  pl.BlockSpec((d, hw), lambda i: (0, 0)), pl.BlockSpec((tm, d), lambda i: (i, 0)),
                  pl.BlockSpec((1, d), lambda i: (0, 0)), pl.BlockSpec((tm, d), lambda i: (i, 0))],
                 [pl.BlockSpec((tm, hw), lambda i: (i, 0)), pl.BlockSpec((tm, d), lambda i: (i, 0)),
                  pl.BlockSpec((1, d), lambda i: (0, 0))],
                 [_sds((n_rows, hw), BF16), _sds((n_rows, d), F32), _sds((1, d), F32)])(gbu, dy, dskip, bb, w_in, h, g, dh)


def _discretize(lam_re, lam_im, log_step, b_re, b_im):
    step = jnp.exp(log_step)[:, None]
    mag = jnp.exp(lam_re * step)
    ar = mag * jnp.cos(lam_im * step)
    ai = mag * jnp.sin(lam_im * step)
    den = lam_re * lam_re + lam_im * lam_im
    nr, ni = ar - 1.0, ai
    cr = (nr * lam_re + ni * lam_im) / den
    ci = (ni * lam_re - nr * lam_im) / den
    bbar_r = cr[..., None] * b_re - ci[..., None] * b_im
    bbar_i = cr[..., None] * b_im + ci[..., None] * b_re
    return ar, ai, bbar_r, bbar_i


def _ssm_mats(lam_re, lam_im, log_step, b_re, b_im, c_re, c_im):
    n_g, n_p, n_c = b_re.shape
    gpc = n_g // 4
    ar, ai, bbar_r, bbar_i = _discretize(lam_re, lam_im, log_step, b_re, b_im)
    eye = jnp.eye(gpc, dtype=F32)

    def in_map(bbar):
        return jnp.einsum('jgpc,gh->jgchp', bbar.reshape(4, gpc, n_p, n_c), eye).reshape(4, gpc * n_c, gpc * n_p)

    def out_map(c):
        return jnp.einsum('jgcp,gh->jgphc', c.reshape(4, gpc, n_c, n_p), eye).reshape(4, gpc * n_p, gpc * n_c)

    bb = jnp.concatenate([in_map(bbar_r), in_map(bbar_i)], axis=2)
    cb = jnp.concatenate([out_map(c_re), -out_map(c_im)], axis=1)
    return bb, cb, ar.reshape(-1), ai.reshape(-1)


def _chunked(v, half):
    return v.reshape(v.shape[:-1] + (4, half))


def _scan_tables(ar, ai, reverse):
    if reverse:
        ai = -ai
    pr, pi = [ar], [ai]
    for _ in range(SUBLANES - 1):
        pr, pi = pr + [pr[-1] * ar - pi[-1] * ai], pi + [pr[-1] * ai + pi[-1] * ar]
    row = jnp.arange(SUBLANES)[:, None]
    tabs = []
    for d in (1, 2, 4):
        keep = (row <= SUBLANES - 1 - d) if reverse else (row >= d)
        tabs += [jnp.where(keep, pr[d - 1][None, :], 0.0), jnp.where(keep, pi[d - 1][None, :], 0.0)]
    order = list(range(SUBLANES))[::-1] if reverse else list(range(SUBLANES))
    tabs += [jnp.stack([pr[k] for k in order]), jnp.stack([pi[k] for k in order])]
    return jnp.concatenate(tabs, axis=0)


def _kv_proj(name, h, g, w_kv, k_gain_t, cos2, sin2, n_rows, n_kv, hd):
    d, kvw = w_kv.shape
    kw = n_kv * hd
    tm = ROW_TILE

    sel, sel_t = _head_selectors(n_kv, hd)

    def body(h_ref, g_ref, w_ref, kg_ref, c_ref, s_ref, e_ref, et_ref, raw_ref, k_ref, v_ref):
        raw = _dot(_rms(h_ref[...], g_ref[...]), w_ref[...])
        raw_ref[...] = raw
        k_ref[...] = _head_prep(raw[:, :kw], kg_ref[...], c_ref[...], s_ref[...], e_ref[...], et_ref[...], hd).astype(BF16)
        v_ref[...] = raw[:, kw:].astype(BF16)

    return _call(name, body, (n_rows // tm,),
                 [pl.BlockSpec((tm, d), lambda i: (i, 0)), pl.BlockSpec((1, d), lambda i: (0, 0)),
                  pl.BlockSpec((d, kvw), lambda i: (0, 0)), pl.BlockSpec((1, kw), lambda i: (0, 0)),
                  pl.BlockSpec((tm, 2 * hd), lambda i: (i, 0)), pl.BlockSpec((tm, 2 * hd), lambda i: (i, 0)),
                  pl.BlockSpec(sel.shape, lambda i: (0, 0)), pl.BlockSpec(sel_t.shape, lambda i: (0, 0))],
                 [pl.BlockSpec((tm, kvw), lambda i: (i, 0)), pl.BlockSpec((tm, kw), lambda i: (i, 0)),
                  pl.BlockSpec((tm, kw), lambda i: (i, 0))],
                 [_sds((n_rows, kvw), F32), _sds((n_rows, kw), BF16), _sds((n_rows, kw), BF16)])(
                     h, g, w_kv, k_gain_t, cos2, sin2, sel, sel_t)


def _q_proj(name, h, g, w_q, q_gain_t, cos2, sin2, n_rows, n_q, hd):
    d, qw = w_q.shape
    tm = ROW_TILE

    sel, sel_t = _head_selectors(n_q, hd)

    def body(h_ref, g_ref, w_ref, qg_ref, c_ref, s_ref, e_ref, et_ref, raw_ref, q_ref):
        raw = _dot(_rms(h_ref[...], g_ref[...]), w_ref[...])
        raw_ref[...] = raw
        q_ref[...] = _head_prep(raw, qg_ref[...], c_ref[...], s_ref[...], e_ref[...], et_ref[...], hd).astype(BF16)

    return _call(name, body, (n_rows // tm,),
                 [pl.BlockSpec((tm, d), lambda i: (i, 0)), pl.BlockSpec((1, d), lambda i: (0, 0)),
                  pl.BlockSpec((d, qw), lambda i: (0, 0)), pl.BlockSpec((1, qw), lambda i: (0, 0)),
                  pl.BlockSpec((tm, 2 * hd), lambda i: (i, 0)), pl.BlockSpec((tm, 2 * hd), lambda i: (i, 0)),
                  pl.BlockSpec(sel.shape, lambda i: (0, 0)), pl.BlockSpec(sel_t.shape, lambda i: (0, 0))],
                 [pl.BlockSpec((tm, qw), lambda i: (i, 0)), pl.BlockSpec((tm, qw), lambda i: (i, 0))],
                 [_sds((n_rows, qw), F32), _sds((n_rows, qw), BF16)])(h, g, w_q, q_gain_t, cos2, sin2, sel, sel_t)


def _attn_specs(seq, n_ex, n_meta, kw):
    nb = seq // WINDOW
    meta_blk = lambda b: (n_ex * seq + META_BLOCK * b + META_BLOCK - n_meta) // n_meta
    return [pl.BlockSpec((WINDOW, kw), lambda b, n: (b * nb + jnp.maximum(n - 1, 0), 0)),
            pl.BlockSpec((WINDOW, kw), lambda b, n: (b * nb + n, 0)),
            pl.BlockSpec((n_meta, kw), lambda b, n: (meta_blk(b), 0))]


def _attn_bias(qpk, n_keys):
    rows = qpk * WINDOW
    qi = jnp.arange(rows)[:, None] & (WINDOW - 1)
    kj = jnp.arange(n_keys)[None, :]
    rel = qi + WINDOW - kj
    band = (rel >= 0) & (rel < WINDOW)
    meta = kj >= 2 * WINDOW
    first = (band & (kj >= WINDOW)) | meta
    return jnp.where(jnp.stack([first, band | meta]), 0.0, NEG_INF).astype(F32)


def _stack_heads(ref, h, qpk, hd, dtype=None):
    parts = [ref[:, (h * qpk + gq) * hd:(h * qpk + gq + 1) * hd] for gq in range(qpk)]
    out = jnp.concatenate(parts, axis=0)
    return out if dtype is None else out.astype(dtype)


def _col(tile, c):
    lane = lax.broadcasted_iota(jnp.int32, tile.shape, 1)
    return jnp.sum(jnp.where(lane == c, tile, 0.0), axis=-1, keepdims=True)


def _put_col(col, c, n):
    lane = lax.broadcasted_iota(jnp.int32, (col.shape[0], n), 1)
    return jnp.where(lane == c, col, 0.0)


def _stack_cols(tile, h, qpk):
    return jnp.concatenate([_col(tile, h * qpk + gq) for gq in range(qpk)], axis=0)


def _sink_col(sinks, h, qpk):
    return jnp.concatenate([jnp.broadcast_to(_col(sinks, h * qpk + gq), (WINDOW, 1)) for gq in range(qpk)], axis=0)


def _attn_fwd(name, q, k, v, sinks, n_ex, seq, n_meta, n_kv, qpk, hd):
    nb = seq // WINDOW
    n_q = n_kv * qpk
    kw = n_kv * hd
    qw = n_q * hd
    n_keys = 2 * WINDOW + n_meta
    bias = _attn_bias(qpk, n_keys)

    def body(q_ref, kp_ref, kc_ref, km_ref, vp_ref, vc_ref, vm_ref, sk_ref, bias_ref, o_ref, lse_ref):
        sinks_v = sk_ref[...]
        o_parts = []
        lse_all = jnp.zeros((WINDOW, n_q), F32)
        for h in range(n_kv):
            hs = slice(h * hd, (h + 1) * hd)
            kb = jnp.concatenate([kp_ref[:, hs], kc_ref[:, hs], km_ref[:, hs]], axis=0)
            vb = jnp.concatenate([vp_ref[:, hs], vc_ref[:, hs], vm_ref[:, hs]], axis=0)
            s = _dot_nt(_stack_heads(q_ref, h, qpk, hd), kb) + bias_ref[0]
            skc = _sink_col(sinks_v, h, qpk)
            m = jnp.maximum(jnp.max(s, axis=-1, keepdims=True), skc)
            p = jnp.exp(s - m)
            den = jnp.sum(p, axis=-1, keepdims=True) + jnp.exp(skc - m)
            o = _dot(p, vb) / den
            lse = m + jnp.log(den)
            for gq in range(qpk):
                o_parts.append(o[gq * WINDOW:(gq + 1) * WINDOW])
                lse_all = lse_all + _put_col(lse[gq * WINDOW:(gq + 1) * WINDOW], h * qpk + gq, n_q)
        o_ref[...] = jnp.concatenate(o_parts, axis=1).astype(BF16)
        lse_ref[...] = lse_all

    qspec = pl.BlockSpec((WINDOW, qw), lambda b, n: (b * nb + n, 0))
    return _call(name, body, (n_ex, nb),
                 [qspec] + _attn_specs(seq, n_ex, n_meta, kw) + _attn_specs(seq, n_ex, n_meta, kw)
                 + [pl.BlockSpec((1, n_q), lambda b, n: (0, 0)),
                    pl.BlockSpec((1,) + bias.shape[1:], lambda b, n: (jnp.minimum(n, 1), 0, 0))],
                 [qspec, pl.BlockSpec((WINDOW, n_q), lambda b, n: (b * nb + n, 0))],
                 [_sds((n_ex * seq, qw), BF16), _sds((n_ex * seq, n_q), F32)])(q, k, k, k, v, v, v, sinks, bias)


def _attn_bwd(name, q, k, v, sinks, o, lse, do, n_ex, seq, n_meta, n_kv, qpk, hd):
    nb = seq // WINDOW
    n_q = n_kv * qpk
    kw = n_kv * hd
    qw = n_q * hd
    n_keys = 2 * WINDOW + n_meta
    bias = _attn_bias(qpk, n_keys)

    def body(q_ref, kp_ref, kc_ref, km_ref, vp_ref, vc_ref, vm_ref, sk_ref, o_ref, lse_ref, do_ref, bias_ref,
             dq_ref, dk_ref, dv_ref, dkm_ref, dvm_ref, dsk_ref):
        n = pl.program_id(1)

        @pl.when(n == 0)
        def _():
            dk_ref[...] = jnp.zeros_like(dk_ref)
            dv_ref[...] = jnp.zeros_like(dv_ref)
            dkm_ref[...] = jnp.zeros_like(dkm_ref)
            dvm_ref[...] = jnp.zeros_like(dvm_ref)

        @pl.when((n == 0) & (pl.program_id(0) == 0))
        def _():
            dsk_ref[...] = jnp.zeros_like(dsk_ref)

        sinks_v = sk_ref[...]
        lse_v = lse_ref[...]
        dq_parts, dk_parts, dv_parts = [], [], []
        dsk = jnp.zeros((1, n_q), F32)
        for h in range(n_kv):
            hs = slice(h * hd, (h + 1) * hd)
            kb = jnp.concatenate([kp_ref[:, hs], kc_ref[:, hs], km_ref[:, hs]], axis=0)
            vb = jnp.concatenate([vp_ref[:, hs], vc_ref[:, hs], vm_ref[:, hs]], axis=0)
            qs = _stack_heads(q_ref, h, qpk, hd)
            dos = _stack_heads(do_ref, h, qpk, hd)
            delta = jnp.sum(dos.astype(F32) * _stack_heads(o_ref, h, qpk, hd, F32), axis=-1, keepdims=True)
            lse_c = _stack_cols(lse_v, h, qpk)
            p = jnp.exp(_dot_nt(qs, kb) + bias_ref[0] - lse_c)
            ds = p * (_dot_nt(dos, vb) - delta)
            dqs = _dot(ds, kb)
            dk_parts.append(_dot_tn(ds, qs))
            dv_parts.append(_dot_tn(p, dos))
            dsink = -jnp.exp(_sink_col(sinks_v, h, qpk) - lse_c) * delta
            for gq in range(qpk):
                dq_parts.append(dqs[gq * WINDOW:(gq + 1) * WINDOW])
                dsk = dsk + _put_col(jnp.sum(dsink[gq * WINDOW:(gq + 1) * WINDOW], axis=0, keepdims=True), h * qpk + gq, n_q)
        dq_ref[...] = jnp.concatenate(dq_parts, axis=1).astype(BF16)
        dsk_ref[...] += dsk
        dkb = jnp.concatenate(dk_parts, axis=1)
        dvb = jnp.concatenate(dv_parts, axis=1)
        prev = pl.ds(pl.multiple_of(jnp.maximum(n - 1, 0) * WINDOW, WINDOW), WINDOW)
        cur = pl.ds(pl.multiple_of(n * WINDOW, WINDOW), WINDOW)
        dk_ref[prev, :] += dkb[0:WINDOW]
        dv_ref[prev, :] += dvb[0:WINDOW]
        dk_ref[cur, :] += dkb[WINDOW:2 * WINDOW]
        dv_ref[cur, :] += dvb[WINDOW:2 * WINDOW]
        dkm_ref[...] += dkb[2 * WINDOW:]
        dvm_ref[...] += dvb[2 * WINDOW:]

    qspec = pl.BlockSpec((WINDOW, qw), lambda b, n: (b * nb + n, 0))
    exspec = pl.BlockSpec((seq, kw), lambda b, n: (b, 0))
    mspec = pl.BlockSpec((n_meta, kw), lambda b, n: (b, 0))
    return _call(name, body, (n_ex, nb),
                 [qspec] + _attn_specs(seq, n_ex, n_meta, kw) + _attn_specs(seq, n_ex, n_meta, kw)
                 + [pl.BlockSpec((1, n_q), lambda b, n: (0, 0)), qspec,
                    pl.BlockSpec((WINDOW, n_q), lambda b, n: (b * nb + n, 0)), qspec,
                    pl.BlockSpec((1,) + bias.shape[1:], lambda b, n: (jnp.minimum(n, 1), 0, 0))],
                 [qspec, exspec, exspec, mspec, mspec, pl.BlockSpec((1, n_q), lambda b, n: (0, 0))],
                 [_sds((n_ex * seq, qw), BF16), _sds((n_ex * seq, kw), F32), _sds((n_ex * seq, kw), F32),
                  _sds((n_ex * n_meta, kw), F32), _sds((n_ex * n_meta, kw), F32), _sds((1, n_q), F32)])(
                      q, k, k, k, v, v, v, sinks, o, lse, do, bias)


def _attn_out(name, o, h, w_o, n_rows):
    qw, d = w_o.shape
    tm = ROW_TILE

    def body(o_ref, h_ref, w_ref, out_ref):
        out_ref[...] = h_ref[...] + _dot(o_ref[...], w_ref[...])

    return _call(name, body, (n_rows // tm,),
                 [pl.BlockSpec((tm, qw), lambda i: (i, 0)), pl.BlockSpec((tm, d), lambda i: (i, 0)),
                  pl.BlockSpec((qw, d), lambda i: (0, 0))],
                 pl.BlockSpec((tm, d), lambda i: (i, 0)), _sds((n_rows, d), F32))(o, h, w_o)


def _attn_out_bwd(name, dh, w_o, n_rows):
    qw, d = w_o.shape
    tm = ROW_TILE

    def body(dh_ref, w_ref, do_ref):
        do_ref[...] = _dot_nt(dh_ref[...], w_ref[...]).astype(BF16)

    return _call(name, body, (n_rows // tm,),
                 [pl.BlockSpec((tm, d), lambda i: (i, 0)), pl.BlockSpec((qw, d), lambda i: (0, 0))],
                 pl.BlockSpec((tm, qw), lambda i: (i, 0)), _sds((n_rows, qw), BF16))(dh, w_o)


def _q_bwd(name, dq, qraw, q_gain_t, cos2, sin2, w_q, h, g, dh, n_rows, n_q, hd):
    d, qw = w_q.shape
    tm = ROW_TILE

    sel, sel_t = _head_selectors(n_q, hd)

    def body(dq_ref, raw_ref, qg_ref, c_ref, s_ref, e_ref, et_ref, w_ref, h_ref, g_ref, dh_ref, draw_ref, o_ref, dqg_ref, dg_ref):
        dx, dgain = _head_prep_bwd(raw_ref[...], qg_ref[...], c_ref[...], s_ref[...], e_ref[...], et_ref[...],
                                   dq_ref[...].astype(F32), hd)
        draw = dx.astype(BF16)
        draw_ref[...] = draw
        dhn, dg = _rms_bwd(h_ref[...], g_ref[...], _dot_nt(draw, w_ref[...]))
        o_ref[...] = dh_ref[...] + dhn
        first = pl.program_id(0) == 0
        _acc_out(dqg_ref, dgain, first)
        _acc_out(dg_ref, dg, first)

    row = lambda w: pl.BlockSpec((tm, w), lambda i: (i, 0))
    one = lambda w: pl.BlockSpec((1, w), lambda i: (0, 0))
    return _call(name, body, (n_rows // tm,),
                 [row(qw), row(qw), one(qw), row(2 * hd), row(2 * hd), pl.BlockSpec(sel.shape, lambda i: (0, 0)),
                  pl.BlockSpec(sel_t.shape, lambda i: (0, 0)), pl.BlockSpec((d, qw), lambda i: (0, 0)), row(d), one(d), row(d)],
                 [row(qw), row(d), one(qw), one(d)],
                 [_sds((n_rows, qw), BF16), _sds((n_rows, d), F32), _sds((1, qw), F32), _sds((1, d), F32)])(
                     dq, qraw, q_gain_t, cos2, sin2, sel, sel_t, w_q, h, g, dh)


def _kv_bwd(name, dk, dv, kvraw, k_gain_t, cos2, sin2, w_kv, h, g, dh_main, n_rows, n_main, n_kv, hd, after=()):
    d, kvw = w_kv.shape
    kw = n_kv * hd
    tm = ROW_TILE
    n_main_tiles = n_main // tm

    sel, sel_t = _head_selectors(n_kv, hd)

    def body(dk_ref, dv_ref, raw_ref, kg_ref, c_ref, s_ref, e_ref, et_ref, w_ref, h_ref, g_ref, dh_ref, draw_ref, o_ref, dkg_ref,
             dg_ref):
        i = pl.program_id(0)
        dx, dgain = _head_prep_bwd(raw_ref[:, :kw], kg_ref[...], c_ref[...], s_ref[...], e_ref[...], et_ref[...], dk_ref[...], hd)
        draw = jnp.concatenate([dx, dv_ref[...]], axis=1).astype(BF16)
        draw_ref[...] = draw
        dhn, dg = _rms_bwd(h_ref[...], g_ref[...], _dot_nt(draw, w_ref[...]))
        o_ref[...] = jnp.where(i < n_main_tiles, dh_ref[...], 0.0) + dhn
        _acc_out(dkg_ref, dgain, i == 0)
        _acc_out(dg_ref, dg, i == 0)

    row = lambda w: pl.BlockSpec((tm, w), lambda i: (i, 0))
    one = lambda w: pl.BlockSpec((1, w), lambda i: (0, 0))
    return _call(name, body, (n_rows // tm,),
                 [row(kw), row(kw), row(kvw), one(kw), row(2 * hd), row(2 * hd), pl.BlockSpec(sel.shape, lambda i: (0, 0)),
                  pl.BlockSpec(sel_t.shape, lambda i: (0, 0)), pl.BlockSpec((d, kvw), lambda i: (0, 0)), row(d),
                  one(d), pl.BlockSpec((tm, d), lambda i: (jnp.minimum(i, n_main_tiles - 1), 0))],
                 [row(kvw), row(d), one(kw), one(d)],
                 [_sds((n_rows, kvw), BF16), _sds((n_rows, d), F32), _sds((1, kw), F32), _sds((1, d), F32)], after=after)(
                     dk, dv, kvraw, k_gain_t, cos2, sin2, sel, sel_t, w_kv, h, g, dh_main)


def _tn_rms(name, h, g, b, n_rows, out_dtype=F32):
    d = h.shape[1]
    nb = b.shape[1]
    tk = _contract_tile(n_rows)
    return _tn(name, (h, g, b),
               [pl.BlockSpec((tk, d), lambda j, k: (k, 0)), pl.BlockSpec((1, d), lambda j, k: (0, 0)),
                pl.BlockSpec((tk, nb), lambda j, k: (k, 0))],
               lambda j, h_ref, g_ref, b_ref: (_rms(h_ref[...], g_ref[...]), b_ref[...]), 1, d, nb, n_rows, tk, out_dtype=out_dtype)


def _tn_plain(name, a, b, nj, a_cols, b_cols, n_rows, a_fn=None, out_dtype=F32, after=()):
    tk = _contract_tile(n_rows)
    fa = (lambda v: v) if a_fn is None else a_fn
    a_map = (lambda j, k: (k, j)) if a.shape[1] != a_cols else (lambda j, k: (k, 0))
    b_map = (lambda j, k: (k, j)) if b.shape[1] != b_cols else (lambda j, k: (k, 0))
    return _tn(name, (a, b), [pl.BlockSpec((tk, a_cols), a_map), pl.BlockSpec((tk, b_cols), b_map)],
               lambda j, a_ref, b_ref: (fa(a_ref[...]), b_ref[...]), nj, a_cols, b_cols, n_rows, tk, out_dtype=out_dtype,
               after=after)


def _cast_layer(name, a, layer):
    _, r, c = a.shape
    tr = _row_tile(r, 256)

    def body(a_ref, o_ref):
        o_ref[...] = a_ref[0].astype(BF16)

    return _call(name, body, (r // tr,), [pl.BlockSpec((1, tr, c), lambda i: (layer, i, 0))],
                 pl.BlockSpec((tr, c), lambda i: (i, 0)), _sds((r, c), BF16))(a)


def _adamw_math(w, g, m, v):
    c1 = 1.0 - ADAM_B1 ** ADAM_STEP
    c2 = 1.0 - ADAM_B2 ** ADAM_STEP
    nm = ADAM_B1 * m + (1.0 - ADAM_B1) * g
    nv = ADAM_B2 * v + (1.0 - ADAM_B2) * (g * g)
    return -ADAM_LR * ((nm / c1) / (jnp.sqrt(nv / c2) + ADAM_EPS) + ADAM_WD * w), nm, nv


def _adamw(name, w, g, m, v, after=()):
    rows, cols = w.shape
    tr = 128 if rows % 128 == 0 else rows

    def body(w_ref, g_ref, m_ref, v_ref, d_ref, nm_ref, nv_ref):
        d_ref[...], nm_ref[...], nv_ref[...] = _adamw_math(w_ref[...], g_ref[...], m_ref[...], v_ref[...])

    spec = pl.BlockSpec((tr, cols), lambda i: (i, 0))
    return _call(name, body, (rows // tr,), [spec] * 4, [spec] * 3, [_sds((rows, cols), F32)] * 3, after=after)(w, g, m, v)


def _adamw_from_halves(name, w, m, v, sources, half_index, transposed, after=()):
    n_layers, r, c = w.shape
    lanes = 1024
    after = tuple(after)
    if transposed:
        rows_half, tr = c // 2, 128
        grid = (n_layers, r // tr)
        w_spec = pl.BlockSpec((1, tr, c), lambda l, i, s: (l, i, 0))
        g_spec = lambda off: pl.BlockSpec((rows_half, tr), lambda l, i, s: (off // rows_half, i))
    else:
        rows_half = r // 2
        grid = (n_layers, 2)
        w_spec = pl.BlockSpec((1, rows_half, c), lambda l, k, s: (l, k, 0))
        g_spec = lambda off: pl.BlockSpec((rows_half, lanes), lambda l, k, s: (off // rows_half, 0))

    def body(s_ref, w_ref, m_ref, v_ref, t0_ref, o0_ref, t1_ref, o1_ref, *rest):
        g_ref, d_ref, nm_ref, nv_ref = rest[len(after):]
        layer, k, mine = pl.program_id(0), pl.program_id(1), s_ref[0]
        tot = jnp.where(layer == 0, t0_ref[...], t1_ref[...])
        oth = jnp.where(layer == 0, o0_ref[...], o1_ref[...])
        if transposed:
            g = jnp.concatenate([jnp.where(mine == 0, tot, oth), jnp.where(mine == 0, oth, tot)], axis=0).T
        else:
            g = jnp.where(k == mine, tot, oth)
        g_ref[0] = g
        d_ref[0], nm_ref[0], nv_ref[0] = _adamw_math(w_ref[0], g, m_ref[0], v_ref[0])

    (t0, o0, off0), (t1, o1, off1) = sources
    grid_spec = pltpu.PrefetchScalarGridSpec(
        num_scalar_prefetch=1, grid=grid,
        in_specs=[w_spec] * 3 + [g_spec(off0), g_spec(off0), g_spec(off1), g_spec(off1)] + [_ANY] * len(after),
        out_specs=[w_spec] * 4)
    return pl.pallas_call(
        body, name=name, grid_spec=grid_spec, out_shape=[_sds(w.shape, F32)] * 4,
        compiler_params=pltpu.CompilerParams(dimension_semantics=("arbitrary", "arbitrary"),
                                             vmem_limit_bytes=V7X_VMEM_LIMIT))(half_index, w, m, v, t0, o0, t1, o1, *after)


def _position():
    return lax.axis_index("x"), lax.axis_index("y"), lax.axis_index("c")


def _other_chips(x, y):
    return [(1 - x, y), (x, 1 - y), (1 - x, 1 - y)]


def _peers_chips(x, y, c):
    return [(cx, cy, c) for cx, cy in _other_chips(x, y)]


def _peers_sibling(x, y, c):
    return [(x, y, 1 - c)]


def _peers_chips_and_sibling(x, y, c):
    return _peers_chips(x, y, c) + _peers_sibling(x, y, c)


def _comm_call(name, body, n_in, out_shape, scratch, sequencer=None):
    if sequencer is None:
        return pl.pallas_call(
            body, name=name, in_specs=[_HBM] * n_in, out_specs=[_HBM] * len(out_shape), out_shape=out_shape,
            scratch_shapes=list(scratch),
            compiler_params=pltpu.CompilerParams(has_side_effects=True, vmem_limit_bytes=V7X_VMEM_LIMIT))
    collective_id, peers = sequencer

    def seq_body(*refs):
        barrier = pltpu.get_barrier_semaphore()
        plist = peers(*_position())
        for peer in plist:
            pl.semaphore_signal(barrier, inc=1, device_id=peer, device_id_type=MESH)
        pl.semaphore_wait(barrier, len(plist))
        body(*refs)

    return pl.kernel(seq_body, out_type=out_shape, mesh=plsc.ScalarSubcoreMesh(axis_name="sequencer", num_cores=1), name=name,
                     scratch_types=list(scratch), compiler_params=pltpu.CompilerParams(collective_id=collective_id))


def _n_chunks(rows, want, dtype):
    align = 16 if dtype == BF16 else 8
    n = want
    while n > 1 and (rows % n or (rows // n) % align):
        n -= 1
    return n


def _remote(src, dst, send_sem, recv_sem, device):
    return pltpu.make_async_remote_copy(src_ref=src, dst_ref=dst, send_sem=send_sem, recv_sem=recv_sem,
                                        device_id=device, device_id_type=MESH)


def _start_in_chunks(src, dst, send_sem, recv_sem, device, want=8):
    rows = src.shape[0]
    n = _n_chunks(rows, want, src.dtype)
    for i in range(n):
        part = pl.ds(i * (rows // n), rows // n)
        _remote(src.at[part], dst.at[part], send_sem, recv_sem, device).start()


def _all_gather_chips(name, shards, split, collective_id=None):
    n = len(shards)

    def body(*refs):
        ins, outs = refs[:n], refs[n:2 * n]
        send_sems, recv_sems, local_sems = refs[2 * n:]
        x, y, c = _position()
        me = 2 * x + y
        chips = _other_chips(x, y)
        sibling = (x, y, 1 - c)
        sends, forwards = [], []
        for t in range(n):
            pltpu.make_async_copy(ins[t], outs[t].at[me], local_sems.at[t]).start()
        for t in range(n):
            r = ins[t].shape[0]
            rows = pl.ds(c * (r // 2), r // 2) if split[t] else pl.ds(0, r)
            for k, (cx, cy) in enumerate(chips):
                src, dst = ins[t].at[rows], outs[t].at[me, rows]
                _start_in_chunks(src, dst, send_sems.at[t, k], recv_sems.at[t, k], (cx, cy, c), want=4)
                sends.append(_remote(src, dst, send_sems.at[t, k], recv_sems.at[t, k], (cx, cy, c)))
        for t in range(n):
            r = ins[t].shape[0]
            rows = pl.ds(c * (r // 2), r // 2) if split[t] else pl.ds(0, r)
            for k, (cx, cy) in enumerate(chips):
                landed = outs[t].at[2 * cx + cy, rows]
                _remote(landed, landed, send_sems.at[t, k], recv_sems.at[t, k], (cx, cy, c)).wait_recv()
                if split[t]:
                    _start_in_chunks(landed, landed, send_sems.at[t, 3 + k], recv_sems.at[t, 3 + k], sibling, want=4)
                    forwards.append(_remote(landed, landed, send_sems.at[t, 3 + k], recv_sems.at[t, 3 + k], sibling))
        for t in range(n):
            if split[t]:
                r = ins[t].shape[0]
                other = pl.ds((1 - c) * (r // 2), r // 2)
                for k, (cx, cy) in enumerate(chips):
                    landed = outs[t].at[2 * cx + cy, other]
                    pltpu.make_async_remote_copy(
                        src_ref=landed, dst_ref=landed, send_sem=send_sems.at[t, 3 + k], recv_sem=recv_sems.at[t, 3 + k],
                        device_id=sibling, device_id_type=MESH).wait_recv()
        for cp in sends + forwards:
            cp.wait_send()
        for t in range(n):
            pltpu.make_async_copy(ins[t], outs[t].at[me], local_sems.at[t]).wait()

    out_shape = [_sds((N_CHIPS,) + s.shape, s.dtype) for s in shards]
    sequencer = None if collective_id is None else (collective_id, _peers_chips_and_sibling)
    return _comm_call(name, body, n, out_shape,
                      [pltpu.SemaphoreType.DMA((n, 6)), pltpu.SemaphoreType.DMA((n, 6)), pltpu.SemaphoreType.DMA((n,))],
                      sequencer)(*shards)


def _swap_halves_with_sibling(name, blob, collective_id=None):
    def body(b_ref, theirs_ref, send_sem, recv_sem):
        x, y, c = _position()
        sibling = (x, y, 1 - c)
        for k in range(b_ref.shape[1]):
            _start_in_chunks(b_ref.at[1 - c, k], theirs_ref.at[k], send_sem, recv_sem, sibling)
        _remote(b_ref.at[1 - c], theirs_ref, send_sem, recv_sem, sibling).wait()

    return _comm_call(name, body, 1, [_sds(blob.shape[1:], blob.dtype)],
                      [pltpu.SemaphoreType.DMA(()), pltpu.SemaphoreType.DMA(())],
                      None if collective_id is None else (collective_id, _peers_sibling))(blob)[0]


def _scatter_to_chips(name, parts, collective_id=None):
    def body(p_ref, o_ref, send_sems, recv_sems, local_sems):
        x, y, c = _position()
        me = 2 * x + y
        rows = p_ref.shape[1]
        n_loc = _n_chunks(rows, 16, p_ref.dtype)
        locs = [pltpu.make_async_copy(p_ref.at[me, pl.ds(i * (rows // n_loc), rows // n_loc)],
                                      o_ref.at[me, pl.ds(i * (rows // n_loc), rows // n_loc)], local_sems.at[i])
                for i in range(n_loc)]
        for loc in locs:
            loc.start()
        sends = []
        for k, (cx, cy) in enumerate(_other_chips(x, y)):
            src, dst = p_ref.at[2 * cx + cy], o_ref.at[me]
            _start_in_chunks(src, dst, send_sems.at[k], recv_sems.at[k], (cx, cy, c))
            sends.append(_remote(src, dst, send_sems.at[k], recv_sems.at[k], (cx, cy, c)))
        for k, (cx, cy) in enumerate(_other_chips(x, y)):
            landed = o_ref.at[2 * cx + cy]
            _remote(landed, landed, send_sems.at[k], recv_sems.at[k], (cx, cy, c)).wait_recv()
        for cp in sends:
            cp.wait_send()
        for loc in locs:
            loc.wait()

    def local_sems_shape(rows):
        return pltpu.SemaphoreType.DMA((_n_chunks(rows, 16, parts.dtype),))

    return _comm_call(name, body, 1, [_sds(parts.shape, parts.dtype)],
                      [pltpu.SemaphoreType.DMA((3,)), pltpu.SemaphoreType.DMA((3,)), local_sems_shape(parts.shape[1])],
                      None if collective_id is None else (collective_id, _peers_chips))(parts)[0]


def _share_with_sibling(name, mine, collective_id=None):
    def body(m_ref, o_ref, send_sem, recv_sem):
        x, y, c = _position()
        sibling = (x, y, 1 - c)
        _start_in_chunks(m_ref, o_ref, send_sem, recv_sem, sibling, want=16)
        _remote(m_ref, o_ref, send_sem, recv_sem, sibling).wait()

    return _comm_call(name, body, 1, [_sds(mine.shape, mine.dtype)],
                      [pltpu.SemaphoreType.DMA(()), pltpu.SemaphoreType.DMA(())],
                      None if collective_id is None else (collective_id, _peers_sibling))(mine)[0]


def _row_tile(rows, cap=640):
    best = rows
    for t in range(16, min(rows, cap) + 1, 16):
        if rows % t == 0:
            best = t
    return best


_ANY = pl.BlockSpec(memory_space=pl.ANY)


def _add_my_half(name, blob, theirs, half_index, out_dtype, after):
    n, rows, cols = theirs.shape
    tr = _row_tile(rows)
    after = tuple(after)

    def body(c_ref, a_ref, b_ref, *rest):
        o_ref = rest[-1]
        o_ref[...] = (a_ref[0].astype(F32) + b_ref[...].astype(F32)).astype(out_dtype)

    spec = pl.BlockSpec((1, tr, cols), lambda k, i, c: (k, i, 0))
    grid_spec = pltpu.PrefetchScalarGridSpec(
        num_scalar_prefetch=1, grid=(n, rows // tr),
        in_specs=[pl.BlockSpec((1, 1, tr, cols), lambda k, i, c: (c[0], k, i, 0)), spec] + [_ANY] * len(after), out_specs=spec)
    return pl.pallas_call(
        body, name=name, grid_spec=grid_spec, out_shape=_sds(theirs.shape, out_dtype),
        compiler_params=pltpu.CompilerParams(dimension_semantics=("arbitrary", "arbitrary"),
                                             vmem_limit_bytes=V7X_VMEM_LIMIT))(half_index, blob, theirs, *after)


def _sum_slots(name, parts, after):
    n, rows, cols = parts.shape
    tr = _row_tile(rows)

    def body(p_ref, o_ref):
        acc = p_ref[0].astype(F32)
        for k in range(1, n):
            acc = acc + p_ref[k].astype(F32)
        o_ref[...] = acc

    return _call(name, body, (rows // tr,), [pl.BlockSpec((n, tr, cols), lambda i: (0, i, 0))],
                 pl.BlockSpec((tr, cols), lambda i: (i, 0)), _sds((rows, cols), F32), after=after)(parts)


def _reduce_small_adamw(name, grads, loss_tile, ws, ms, vs, after=()):
    n = len(grads)
    srcs = list(grads) + [loss_tile]
    after = tuple(after)

    def body(*refs):
        refs = refs[:4 * n + 1] + refs[4 * n + 1 + len(after):]
        g_in, w_in, m_in, v_in = refs[:n + 1], refs[n + 1:2 * n + 1], refs[2 * n + 1:3 * n + 1], refs[3 * n + 1:4 * n + 1]
        outs = refs[4 * n + 1:8 * n + 2]
        g_out, d_out, nm_out, nv_out, loss_out = outs[:n], outs[n:2 * n], outs[2 * n:3 * n], outs[3 * n:4 * n], outs[4 * n]
        bufs = refs[8 * n + 2:9 * n + 3]
        send_sems, recv_sems = refs[9 * n + 3:]
        x, y, c = _position()
        me = 4 * x + 2 * y + c
        chip = 2 * x + y
        peers = [(1 - x if dlt & 4 else x, 1 - y if dlt & 2 else y, 1 - c if dlt & 1 else c) for dlt in range(1, N_DEV)]
        sends = []
        for t in range(n + 1):
            bufs[t][me] = g_in[t][...]
            for k, peer in enumerate(peers):
                cp = _remote(g_in[t], bufs[t].at[me], send_sems.at[t, k], recv_sems.at[t, k], peer)
                cp.start()
                sends.append(cp)
        for t in range(n + 1):
            for k, (tx, ty, tc) in enumerate(peers):
                landed = bufs[t].at[4 * tx + 2 * ty + tc]
                _remote(landed, landed, send_sems.at[t, k], recv_sems.at[t, k], (tx, ty, tc)).wait_recv()
        for cp in sends:
            cp.wait_send()
        for t in range(n + 1):
            total = bufs[t][0]
            for k in range(1, N_DEV):
                total = total + bufs[t][k]
            if t == n:
                loss_out[...] = total
                continue
            cols = w_in[t].shape[1]
            if cols == total.shape[1]:
                g_out[t][...] = total
                d_out[t][...], nm_out[t][...], nv_out[t][...] = _adamw_math(w_in[t][...], total, m_in[t][...], v_in[t][...])
            else:
                for j in range(N_CHIPS):
                    @pl.when(chip == j)
                    def _(t=t, j=j, cols=cols, total=total):
                        mine = total[:, j * cols:(j + 1) * cols]
                        g_out[t][...] = mine
                        d_out[t][...], nm_out[t][...], nv_out[t][...] = _adamw_math(w_in[t][...], mine, m_in[t][...], v_in[t][...])

    w_shapes = [_sds(a.shape, F32) for a in ws]
    return pl.pallas_call(
        body, name=name, in_specs=[_VMEM] * (4 * n + 1) + [_ANY] * len(after), out_specs=[_VMEM] * (4 * n + 1),
        out_shape=w_shapes * 4 + [_sds(loss_tile.shape, F32)],
        scratch_shapes=[pltpu.VMEM((N_DEV,) + a.shape, F32) for a in srcs]
        + [pltpu.SemaphoreType.DMA((n + 1, N_DEV - 1)), pltpu.SemaphoreType.DMA((n + 1, N_DEV - 1))],
        compiler_params=pltpu.CompilerParams(has_side_effects=True, vmem_limit_bytes=V7X_VMEM_LIMIT))(
            *srcs, *ws, *ms, *vs, *after)


_BIG = ("ffn1_w_gate_up", "ffn1_w_down", "ffn2_w_gate_up", "ffn2_w_down", "ssm_w_in", "ssm_w_out", "w_kv", "attn_w_q", "attn_w_o")
_TRANSPOSED = ("ffn1_w_gate_up", "ffn2_w_gate_up")
_FROM_HALVES = _TRANSPOSED + ("ffn1_w_down", "ffn2_w_down")
_SMALL = ("meta_tokens", "ffn1_norm", "mix_norm", "ffn2_norm", "ssm_lambda_re", "ssm_lambda_im", "ssm_b_re", "ssm_b_im",
          "ssm_c_re", "ssm_c_im", "ssm_log_step", "ssm_d", "kv_norm", "k_norm", "q_norm", "attn_sinks")
_ORDER = ("meta_tokens", "ffn1_norm", "ffn1_w_gate_up", "ffn1_w_down", "mix_norm", "ffn2_norm", "ffn2_w_gate_up", "ffn2_w_down",
          "ssm_w_in", "ssm_lambda_re", "ssm_lambda_im", "ssm_b_re", "ssm_b_im", "ssm_c_re", "ssm_c_im", "ssm_log_step", "ssm_d",
          "ssm_w_out", "kv_norm", "w_kv", "k_norm", "attn_w_q", "q_norm", "attn_sinks", "attn_w_o")


def _step(x, target, w, m, v):
    n_ex, seq, d = x.shape
    n_meta = w["meta_tokens"].shape[0]
    n_main = n_ex * seq
    n_all = n_main + n_ex * META_BLOCK
    n_g, n_p, n_c = w["ssm_b_re"].shape[1:]
    hd = w["k_norm"].shape[0]
    n_kv = w["w_kv"].shape[1] // (2 * hd)
    n_q = w["attn_w_q"].shape[2] // hd
    qpk = n_q // n_kv
    px, py, pc = _position()
    chip = 2 * px + py

    def cast(name, layer=0):
        a = w[name]
        return _cast_layer(f"cast_{name}_{layer}", a if a.ndim == 3 else a[None], layer)

    g_a = _all_gather_chips("gather_first", [cast("ffn1_w_gate_up"), w["meta_tokens"], w["ssm_d"]], [True, False, False],
                            collective_id=12)
    g_d = _all_gather_chips("gather_next", [cast("ffn1_w_down"), cast("ssm_w_in"), cast("ssm_w_out")], [True] * 3, collective_id=13)
    second = [cast("ffn2_w_gate_up"), cast("ffn2_w_down"), cast("w_kv")]
    g_b = _all_gather_chips("gather_second", second, [True] * 3, collective_id=1)
    third = [cast("ffn1_w_gate_up", 1), cast("ffn1_w_down", 1), cast("attn_w_q"), cast("attn_w_o"),
             cast("ffn2_w_gate_up", 1), cast("ffn2_w_down", 1)]
    g_c = _all_gather_chips("gather_third", third, [True] * 6, collective_id=2)
    wgu = {("ffn1", 0): g_a[0], ("ffn1", 1): g_c[0], ("ffn2", 0): g_b[0], ("ffn2", 1): g_c[4]}
    wd = {("ffn1", 0): g_d[0], ("ffn1", 1): g_c[1], ("ffn2", 0): g_b[1], ("ffn2", 1): g_c[5]}
    wd = {key: a.reshape(-1, d) for key, a in wd.items()}
    w_in = g_d[1].reshape(d, -1)
    wout4 = g_d[2]
    w_q = g_c[2].reshape(d, -1)
    w_o = g_c[3].reshape(-1, d)
    w_kv = g_b[2].reshape(d, -1)
    meta_full = jnp.transpose(g_a[1], (1, 0, 2)).reshape(n_meta, d)
    dskip = g_a[2].reshape(1, -1)

    row1 = lambda a: a.reshape(1, -1)
    ssm_args = tuple(w[k][0] for k in ("ssm_lambda_re", "ssm_lambda_im", "ssm_log_step", "ssm_b_re", "ssm_b_im", "ssm_c_re", "ssm_c_im"))
    (bb, cb, a_re, a_im), ssm_vjp = jax.vjp(_ssm_mats, *ssm_args)
    bb16, cb16 = bb.astype(BF16), cb.astype(BF16)
    a_re_s, a_im_s = lax.stop_gradient(a_re), lax.stop_gradient(a_im)
    half = n_g * n_p // 4
    tabs_f = _scan_tables(a_re_s, a_im_s, False)
    tabs_b = _scan_tables(a_re_s, a_im_s, True)

    freqs = ROPE_THETA ** (-jnp.arange(0, hd // 2, dtype=F32) * 2.0 / hd)
    pos_main = jnp.tile(n_meta + jnp.arange(seq), n_ex)
    pos_meta = jnp.tile(jnp.maximum(jnp.arange(META_BLOCK) - (META_BLOCK - n_meta), 0), n_ex)
    ang = jnp.concatenate([pos_main, pos_meta]).astype(F32)[:, None] * freqs[None, :]
    cos = jnp.concatenate([jnp.cos(ang), jnp.cos(ang)] * 2, axis=1)
    sin_s = jnp.concatenate([-jnp.sin(ang), jnp.sin(ang)] * 2, axis=1)
    k_gain_t = jnp.tile(row1(w["k_norm"]), (1, n_kv))
    score_scale = hd ** -0.5
    q_gain_t = jnp.tile(row1(w["q_norm"][0]), (1, n_q)) * score_scale

    meta_block = jnp.concatenate([jnp.zeros((META_BLOCK - n_meta, d), F32), meta_full], axis=0)
    h0 = jnp.concatenate([x.reshape(n_main, d)] + [meta_block] * n_ex, axis=0)

    g = lambda name, layer: row1(w[name][layer])
    h1, gu1 = _ffn_fwd("l0_ffn1", h0, g("ffn1_norm", 0), wgu["ffn1", 0], wd["ffn1", 0], n_all)
    u, bu = _ssm_in("ssm_in", h1, g("mix_norm", 0), w_in, bb16, n_all)
    xs = _scan_fwd("ssm_scan", bu, tabs_f, n_ex, seq)
    h2, y = _ssm_out("ssm_out", xs, u, dskip, cb16, wout4, h1, n_all)
    h3, gu2 = _ffn_fwd("l0_ffn2", h2, g("ffn2_norm", 0), wgu["ffn2", 0], wd["ffn2", 0], n_all)
    kvraw, k, vv = _kv_proj("kv_proj", h3, row1(w["kv_norm"]), w_kv, k_gain_t, cos, sin_s, n_all, n_kv, hd)
    h4, gu3 = _ffn_fwd("l1_ffn1", h3, g("ffn1_norm", 1), wgu["ffn1", 1], wd["ffn1", 1], n_main)
    qraw, q = _q_proj("q_proj", h4, g("mix_norm", 1), w_q, q_gain_t, cos, sin_s, n_main, n_q, hd)
    sinks = row1(w["attn_sinks"][0])
    o, lse = _attn_fwd("attn_fwd", q, k, vv, sinks, n_ex, seq, n_meta, n_kv, qpk, hd)
    h5 = _attn_out("attn_out", o, h4, w_o, n_main)
    (dh6, loss_tile), gu4 = _ffn_fwd("l1_ffn2", h5, g("ffn2_norm", 1), wgu["ffn2", 1], wd["ffn2", 1], n_main,
                                     target=target.reshape(n_main, d))

    lanes = 1024

    def rs_start(tag, entries, ids):
        pieces = [gr.reshape(N_CHIPS, 2, -1, lanes) for _, _, gr in entries]
        blob = jnp.transpose(jnp.concatenate(pieces, axis=2), (1, 0, 2, 3)).astype(BF16)
        return dict(tag=tag, entries=entries, ids=ids, blob=blob, theirs=_swap_halves_with_sibling(tag + "_swap", blob, ids[0]))

    def rs_scatter(st, after):
        chip_sum = _add_my_half(st["tag"] + "_chip_sum", st["blob"], st["theirs"], jnp.reshape(pc, (1,)).astype(jnp.int32), BF16, after)
        st["chip_sum"] = chip_sum
        st["landed"] = _scatter_to_chips(st["tag"] + "_scatter", chip_sum, st["ids"][1])

    def rs_finish(st, after):
        total = _sum_slots(st["tag"] + "_sum", st["landed"], after)
        st["total"] = total
        other = _share_with_sibling(st["tag"] + "_share", total, st["ids"][2])
        halves = (jnp.where(pc == 0, total, other), jnp.where(pc == 0, other, total))
        out, off = {}, 0
        for name, layer, gr in st["entries"]:
            rows = gr.shape[1] * gr.shape[2] // lanes // 2
            if name in _FROM_HALVES:
                out[name, layer] = (total, other, off)
            else:
                out[name, layer] = jnp.concatenate([hv[off:off + rows].reshape(-1) for hv in halves])
            off += rows
        return out

    small = {}
    dh5, dg_f2l1, dwgu_f2l1, dwd_f2l1 = _ffn_bwd("l1_ffn2", dh6, h5, g("ffn2_norm", 1), gu4, wgu["ffn2", 1], wd["ffn2", 1], n_main)
    do = _attn_out_bwd("attn_out_bwd", dh5, w_o, n_main)
    dw_o = _tn_plain("attn_dwo", o, dh5, 1, o.shape[1], d, n_main, out_dtype=BF16).reshape(N_CHIPS, -1, d)
    dq, dk_main, dv_main, dk_meta, dv_meta, dsinks = _attn_bwd("attn_bwd", q, k, vv, sinks, o, lse, do, n_ex, seq, n_meta, n_kv, qpk, hd)
    dqraw, dh4, dq_gain, dg_mix1 = _q_bwd("q_bwd", dq, qraw, q_gain_t, cos, sin_s, w_q, h4, g("mix_norm", 1), dh5, n_main, n_q, hd)
    dw_q = _tn_rms("attn_dwq", h4, g("mix_norm", 1), dqraw, n_main, out_dtype=BF16).reshape(N_CHIPS, -1, dqraw.shape[1])
    dh3m, dg_f1l1, dwgu_f1l1, dwd_f1l1 = _ffn_bwd("l1_ffn1", dh4, h3, g("ffn1_norm", 1), gu3, wgu["ffn1", 1], wd["ffn1", 1], n_main)
    rs1 = rs_start("rs1", [("ffn2_w_gate_up", 1, dwgu_f2l1), ("ffn1_w_gate_up", 1, dwgu_f1l1), ("ffn2_w_down", 1, dwd_f2l1),
                           ("ffn1_w_down", 1, dwd_f1l1), ("attn_w_o", 0, dw_o), ("attn_w_q", 0, dw_q)], (3, 4, 5))

    def with_meta(main, meta):
        blocks = [jnp.pad(meta[b * n_meta:(b + 1) * n_meta], ((META_BLOCK - n_meta, 0), (0, 0))) for b in range(n_ex)]
        return jnp.concatenate([main] + blocks, axis=0)

    dkvraw, dh3, dk_gain, dg_kv = _kv_bwd("kv_bwd", with_meta(dk_main, dk_meta), with_meta(dv_main, dv_meta), kvraw, k_gain_t,
                                          cos, sin_s, w_kv, h3, row1(w["kv_norm"]), dh3m, n_all, n_main, n_kv, hd,
                                          after=(rs1["blob"],))
    rs_scatter(rs1, after=(dh3,))
    dw_kv = _tn_rms("kv_dw", h3, row1(w["kv_norm"]), dkvraw, n_all, out_dtype=BF16).reshape(N_CHIPS, -1, dkvraw.shape[1])
    dh2, dg_f2l0, dwgu_f2l0, dwd_f2l0 = _ffn_bwd("l0_ffn2", dh3, h2, g("ffn2_norm", 0), gu2, wgu["ffn2", 0], wd["ffn2", 0], n_all,
                                                 after=(rs1["chip_sum"],))
    reduced = rs_finish(rs1, after=(dh2, dwgu_f2l0, dwd_f2l0, dw_kv))
    rs0a = rs_start("rs0a", [("ffn2_w_gate_up", 0, dwgu_f2l0), ("ffn2_w_down", 0, dwd_f2l0), ("w_kv", 0, dw_kv)], (6, 7, 8))

    dy, dz, gx, dd = _ssm_out_bwd("ssm_out_bwd", dh2, y, u, cb16, wout4, n_all, after=(rs1["total"], rs0a["blob"]))
    rs_scatter(rs0a, after=(dy,))
    hw = y.shape[1]
    oc = wout4.shape[2]
    dw_out = _tn_plain("ssm_dwout", y, dz, wout4.shape[0], hw, oc, n_all, a_fn=_gelu, out_dtype=BF16)
    gbu, da = _scan_bwd("ssm_scan_bwd", gx, xs, tabs_b, n_ex, seq, after=(rs0a["chip_sum"],))
    du, dh1, dg_mix0 = _ssm_in_bwd("ssm_in_bwd", gbu, dy, dskip, bb16, w_in, h1, g("mix_norm", 0), dh2, n_all)
    reduced.update(rs_finish(rs0a, after=(dh1,)))
    dw_in = _tn_rms("ssm_dwin", h1, g("mix_norm", 0), du, n_all, out_dtype=BF16).reshape(N_CHIPS, -1, hw)
    (dh0, dh0_meta), dg_f1l0, dwgu_f1l0, dwd_f1l0 = _ffn_bwd("l0_ffn1", dh1, h0, g("ffn1_norm", 0), gu1, wgu["ffn1", 0], wd["ffn1", 0],
                                                             n_all, n_main, after=(rs0a["total"],))
    rs0b = rs_start("rs0b", [("ffn1_w_gate_up", 0, dwgu_f1l0), ("ffn1_w_down", 0, dwd_f1l0), ("ssm_w_out", 0, dw_out),
                             ("ssm_w_in", 0, dw_in)], (9, 10, 11))
    dcb = _tn_plain("ssm_dcb", xs, dy, 4, xs.shape[1] // 4, hw // 4, n_all, after=(rs0b["blob"],))
    rs_scatter(rs0b, after=(dcb,))
    dbb = _tn_plain("ssm_dbb", u, gbu, 4, hw // 4, gbu.shape[1] // 4, n_all, after=(rs0b["chip_sum"],))

    grad_x = dh0.reshape(n_ex, seq, d)
    da_sum = jnp.sum(da, axis=(0, 1)).reshape(4, 2, half)
    d_ssm = ssm_vjp((dbb, dcb, da_sum[:, 0].reshape(-1), da_sum[:, 1].reshape(-1)))
    for key, val in zip(("ssm_lambda_re", "ssm_lambda_im", "ssm_log_step", "ssm_b_re", "ssm_b_im", "ssm_c_re", "ssm_c_im"), d_ssm):
        small[key] = val[None]
    small["meta_tokens"] = sum(dh0_meta[META_BLOCK * (b + 1) - n_meta:META_BLOCK * (b + 1)] for b in range(n_ex))
    small["ffn1_norm"] = jnp.concatenate([dg_f1l0, dg_f1l1], axis=0)
    small["ffn2_norm"] = jnp.concatenate([dg_f2l0, dg_f2l1], axis=0)
    small["mix_norm"] = jnp.concatenate([dg_mix0, dg_mix1], axis=0)
    small["ssm_d"] = dd
    small["kv_norm"] = dg_kv.reshape(-1)
    small["k_norm"] = jnp.sum(dk_gain.reshape(n_kv, hd), axis=0)
    small["q_norm"] = jnp.sum(dq_gain.reshape(n_q, hd), axis=0, keepdims=True) * score_scale
    small["attn_sinks"] = dsinks

    def view(name, a):
        if name in ("ssm_b_re", "ssm_b_im"):
            return a.reshape(-1, 128)
        return a.reshape(1, -1) if a.ndim == 1 else a.reshape(-1, a.shape[-1])

    grads, deltas, new_m, new_v = {}, {}, {}, {}

    def adamw_matrix(name, after=()):
        shape = w[name].shape
        if name in _FROM_HALVES:
            grads[name], deltas[name], new_m[name], new_v[name] = _adamw_from_halves(
                "adamw_" + name, w[name], m[name], v[name], [reduced[name, 0], reduced[name, 1]],
                jnp.reshape(pc, (1,)).astype(jnp.int32), name in _TRANSPOSED, after=after)
            return new_v[name]
        layers = [reduced[name, layer] for layer in range(2) if (name, layer) in reduced]
        grads[name] = jnp.concatenate(layers).reshape(shape)
        two_d = lambda a: a.reshape(-1, shape[-1])
        dl, nm, nv = _adamw("adamw_" + name, two_d(w[name]), two_d(grads[name]), two_d(m[name]), two_d(v[name]), after=after)
        deltas[name], new_m[name], new_v[name] = dl.reshape(shape), nm.reshape(shape), nv.reshape(shape)
        return nv

    placed = (rs0b["chip_sum"],)
    for name in ("ffn2_w_down", "attn_w_o", "attn_w_q", "w_kv"):
        placed = (adamw_matrix(name, after=placed),)
    tail = _reduce_small_adamw("small_tail", [view(k, small[k]) for k in _SMALL], loss_tile,
                               *[[view(k, t[k]) for k in _SMALL] for t in (w, m, v)], after=placed)
    n_small = len(_SMALL)
    for i, k in enumerate(_SMALL):
        grads[k], deltas[k] = tail[i].reshape(w[k].shape), tail[n_small + i].reshape(w[k].shape)
        new_m[k], new_v[k] = tail[2 * n_small + i].reshape(w[k].shape), tail[3 * n_small + i].reshape(w[k].shape)
    loss = jnp.sum(tail[-1])
    reduced.update(rs_finish(rs0b, after=(tail[-1],)))
    adamw_matrix("ffn2_w_gate_up", after=(rs0b["total"],))
    for name in ("ffn1_w_gate_up", "ffn1_w_down", "ssm_w_in", "ssm_w_out"):
        adamw_matrix(name)
    return (loss, grad_x, *[grads[k] for k in _ORDER], *[deltas[k] for k in _ORDER], *[new_m[k] for k in _ORDER],
            *[new_v[k] for k in _ORDER])


def kernel(x, meta_tokens, ffn1_norm, ffn1_w_gate_up, ffn1_w_down, mix_norm, ffn2_norm, ffn2_w_gate_up, ffn2_w_down, ssm_w_in, ssm_lambda_re, ssm_lambda_im, ssm_b_re, ssm_b_im, ssm_c_re, ssm_c_im, ssm_log_step, ssm_d, ssm_w_out, kv_norm, w_kv, k_norm, attn_w_q, q_norm, attn_sinks, attn_w_o, loss_target, m_meta_tokens, m_ffn1_norm, m_ffn1_w_gate_up, m_ffn1_w_down, m_mix_norm, m_ffn2_norm, m_ffn2_w_gate_up, m_ffn2_w_down, m_ssm_w_in, m_ssm_lambda_re, m_ssm_lambda_im, m_ssm_b_re, m_ssm_b_im, m_ssm_c_re, m_ssm_c_im, m_ssm_log_step, m_ssm_d, m_ssm_w_out, m_kv_norm, m_w_kv, m_k_norm, m_attn_w_q, m_q_norm, m_attn_sinks, m_attn_w_o, v_meta_tokens, v_ffn1_norm, v_ffn1_w_gate_up, v_ffn1_w_down, v_mix_norm, v_ffn2_norm, v_ffn2_w_gate_up, v_ffn2_w_down, v_ssm_w_in, v_ssm_lambda_re, v_ssm_lambda_im, v_ssm_b_re, v_ssm_b_im, v_ssm_c_re, v_ssm_c_im, v_ssm_log_step, v_ssm_d, v_ssm_w_out, v_kv_norm, v_w_kv, v_k_norm, v_attn_w_q, v_q_norm, v_attn_sinks, v_attn_w_o):
    args = locals()
    w = {k: args[k] for k in _ORDER}
    m = {k: args["m_" + k] for k in _ORDER}
    v = {k: args["v_" + k] for k in _ORDER}
    return _step(x, loss_target, w, m, v)
```

```python
import functools
import math

import jax
import jax.numpy as jnp
from jax import lax
from jax.experimental import pallas as pl
from jax.experimental.pallas import tpu as pltpu
from jax.experimental.pallas import tpu_sc as plsc

F32 = jnp.float32
BF16 = jnp.bfloat16
MESH = pl.DeviceIdType.MESH

EPS = 1e-6
NEG_INF = -1e30
ROPE_THETA = 10000.0
WINDOW = 128
META_BLOCK = 128
ROW_TILE = 256
SUBLANES = 8
V7X_VMEM_LIMIT = 56 * 2**20
N_CHIPS = 4
N_DEV = 8

ADAM_LR = 0.001
ADAM_B1 = 0.9
ADAM_B2 = 0.999
ADAM_EPS = 1e-08
ADAM_WD = 0.01
ADAM_STEP = 10

_HBM = pl.BlockSpec(memory_space=pltpu.HBM)
_VMEM = pl.BlockSpec(memory_space=pltpu.VMEM)


def _call(name, body, grid, in_specs, out_specs, out_shape, scratch=(), after=()):
    after = tuple(after)
    n_in = len(in_specs)

    def wrapped(*refs):
        return body(*refs[:n_in], *refs[n_in + len(after):])

    call = pl.pallas_call(
        wrapped, name=name, grid=grid, in_specs=list(in_specs) + [pl.BlockSpec(memory_space=pl.ANY)] * len(after),
        out_specs=out_specs, out_shape=out_shape, scratch_shapes=list(scratch),
        compiler_params=pltpu.CompilerParams(dimension_semantics=("arbitrary",) * len(grid),
                                             vmem_limit_bytes=V7X_VMEM_LIMIT))
    return lambda *operands: call(*operands, *after)


def _sds(shape, dtype):
    return jax.ShapeDtypeStruct(tuple(shape), dtype)


def _dot(a, b):
    return jnp.dot(a.astype(BF16), b.astype(BF16), preferred_element_type=F32)


def _dot_nt(a, b):
    return lax.dot_general(a.astype(BF16), b.astype(BF16), (((1,), (1,)), ((), ())), preferred_element_type=F32)


def _dot_tn(a, b):
    return lax.dot_general(a.astype(BF16), b.astype(BF16), (((0,), (0,)), ((), ())), preferred_element_type=F32)


def _rms(h, g):
    return h * lax.rsqrt(jnp.mean(h * h, axis=-1, keepdims=True) + EPS) * g


def _rms_bwd(h, g, dn):
    r = lax.rsqrt(jnp.mean(h * h, axis=-1, keepdims=True) + EPS)
    xh = h * r
    dxh = dn * g
    dg = jnp.sum(dn * xh, axis=0, keepdims=True)
    dh = r * (dxh - xh * jnp.mean(dxh * xh, axis=-1, keepdims=True))
    return dh, dg


def _sigmoid(x):
    return 0.5 * jnp.tanh(0.5 * x) + 0.5


def _gelu(y):
    k = math.sqrt(2.0 / math.pi)
    return 0.5 * y * (1.0 + jnp.tanh(k * (y + 0.044715 * y * y * y)))


def _gelu_grad(y):
    k = math.sqrt(2.0 / math.pi)
    t = jnp.tanh(k * (y + 0.044715 * y * y * y))
    return 0.5 * (1.0 + t) + 0.5 * y * (1.0 - t * t) * k * (1.0 + 3.0 * 0.044715 * y * y)


def _partner(x, lane, d):
    width = x.shape[-1]
    return jnp.where((lane & d) == 0, pltpu.roll(x, width - d, 1), pltpu.roll(x, d, 1))


def _split_bf16(x):
    hi = x.astype(BF16)
    return hi, (x - hi.astype(F32)).astype(BF16)


def _head_sums(x, sel):
    hi, lo = _split_bf16(x)
    return jnp.dot(hi, sel, preferred_element_type=F32) + jnp.dot(lo, sel, preferred_element_type=F32)


def _head_expand(v, sel_t):
    hi, lo = _split_bf16(v)
    return jnp.dot(hi, sel_t, preferred_element_type=F32) + jnp.dot(lo, sel_t, preferred_element_type=F32)


def _tile_lanes(t, width):
    return jnp.concatenate([t] * (width // t.shape[-1]), axis=1)


def _head_prep(x, gain_t, cos2, sin2, sel, sel_t, hd):
    width = x.shape[-1]
    lane = lax.broadcasted_iota(jnp.int32, x.shape, 1)
    r = _head_expand(lax.rsqrt(_head_sums(x * x, sel) * (1.0 / hd) + EPS), sel_t)
    y = x * r * gain_t
    return y * _tile_lanes(cos2, width) + _partner(y, lane, hd // 2) * _tile_lanes(sin2, width)


def _head_prep_bwd(x, gain_t, cos2, sin2, sel, sel_t, d_out, hd):
    width = x.shape[-1]
    lane = lax.broadcasted_iota(jnp.int32, x.shape, 1)
    r = _head_expand(lax.rsqrt(_head_sums(x * x, sel) * (1.0 / hd) + EPS), sel_t)
    xhat = x * r
    dy = d_out * _tile_lanes(cos2, width) + _partner(d_out * _tile_lanes(sin2, width), lane, hd // 2)
    dgain = jnp.sum(dy * xhat, axis=0, keepdims=True)
    dxh = dy * gain_t
    mean = _head_expand(_head_sums(dxh * xhat, sel) * (1.0 / hd), sel_t)
    return r * (dxh - xhat * mean), dgain


def _head_selectors(n_heads, hd):
    sel = (jnp.arange(n_heads * hd)[:, None] // hd == jnp.arange(128)[None, :]).astype(BF16)
    return sel, sel.T


def _acc_out(ref, val, first):
    @pl.when(first)
    def _():
        ref[...] = jnp.zeros_like(ref)
    ref[...] += val


def _wide_row_tile(n_rows, cap=512):
    best = 128
    for t in range(128, cap + 1, 128):
        if n_rows % t == 0:
            best = t
    return best


def _ffn_up(name, h, g, w4, n_rows):
    nj, d, fc = w4.shape
    tm = _wide_row_tile(n_rows)

    def body(h_ref, g_ref, w_ref, o_ref, n_ref):
        n = _rms(h_ref[...], g_ref[...]).astype(BF16)
        n_ref[...] = n
        for j in range(nj):
            o_ref[:, j * fc:(j + 1) * fc] = _dot(n, w_ref[j]).astype(BF16)

    return _call(name, body, (n_rows // tm,),
                 [pl.BlockSpec((tm, d), lambda i: (i, 0)), pl.BlockSpec((1, d), lambda i: (0, 0)),
                  pl.BlockSpec((nj, d, fc), lambda i: (0, 0, 0))],
                 [pl.BlockSpec((tm, nj * fc), lambda i: (i, 0)), pl.BlockSpec((tm, d), lambda i: (i, 0))],
                 [_sds((n_rows, nj * fc), BF16), _sds((n_rows, d), BF16)])(h, g, w4)


def _ffn_down(name, gu, h, wd, n_rows, target=None):
    f, d = wd.shape
    tm = ROW_TILE

    def body(gu_ref, h_ref, w_ref, *rest):
        half_a = gu_ref[:, :f] * 0.5
        s = (half_a + half_a * jnp.tanh(half_a)) * gu_ref[:, f:]
        y = h_ref[...] + 0.5 * _dot(s, w_ref[...])
        if target is None:
            o_ref, s_ref = rest
            o_ref[...] = y
        else:
            t_ref, dy_ref, l_ref, s_ref = rest
            e = y - t_ref[...]
            dy_ref[...] = e * (1.0 / d)
            e2 = jnp.sum((e * e).reshape(tm // SUBLANES, SUBLANES, d), axis=0)
            part = e2[:, 0:128]
            for k in range(1, d // 128):
                part = part + e2[:, k * 128:(k + 1) * 128]
            _acc_out(l_ref, part * (0.5 / d), pl.program_id(0) == 0)
        s_ref[...] = s

    row = lambda width: pl.BlockSpec((tm, width), lambda i: (i, 0))
    in_specs = [row(2 * f), row(d), pl.BlockSpec((f, d), lambda i: (0, 0))]
    if target is None:
        return _call(name, body, (n_rows // tm,), in_specs, [row(d), row(f)],
                     [_sds((n_rows, d), F32), _sds((n_rows, f), BF16)])(gu, h, wd)
    return _call(name, body, (n_rows // tm,), in_specs + [row(d)],
                 [row(d), pl.BlockSpec((SUBLANES, 128), lambda i: (0, 0)), row(f)],
                 [_sds((n_rows, d), F32), _sds((SUBLANES, 128), F32), _sds((n_rows, f), BF16)])(gu, h, wd, target)


def _ffn_dgu(name, dh, gu, wd, n_rows, after=()):
    f, d = wd.shape
    tm = ROW_TILE

    def body(dh_ref, gu_ref, w_ref, o_ref):
        ds = _dot_nt(0.5 * dh_ref[...], w_ref[...]).astype(BF16)
        half_a = gu_ref[:, :f] * 0.5
        t = jnp.tanh(half_a)
        o_ref[:, :f] = ds * gu_ref[:, f:] * ((1.0 + t + half_a * (1.0 - t * t)) * 0.5)
        o_ref[:, f:] = ds * (half_a + half_a * t)

    return _call(name, body, (n_rows // tm,),
                 [pl.BlockSpec((tm, d), lambda i: (i, 0)), pl.BlockSpec((tm, 2 * f), lambda i: (i, 0)),
                  pl.BlockSpec((f, d), lambda i: (0, 0))],
                 pl.BlockSpec((tm, 2 * f), lambda i: (i, 0)), _sds((n_rows, 2 * f), BF16), after=after)(dh, gu, wd)


def _ffn_dh(name, dgu, h, g, dh, w4, n_rows, n_main=None, after=()):
    nj, d, fc = w4.shape
    tm = _wide_row_tile(n_rows) if n_main is None else ROW_TILE
    n_first = (n_rows if n_main is None else n_main) // tm

    def body(dgu_ref, h_ref, g_ref, dh_ref, w_ref, o_ref, *rest):
        dg_ref = rest[-1]
        i = pl.program_id(0)
        dn = _dot_nt(dgu_ref[:, 0:fc], w_ref[0])
        for j in range(1, nj):
            dn = dn + _dot_nt(dgu_ref[:, j * fc:(j + 1) * fc], w_ref[j])
        dhn, dg = _rms_bwd(h_ref[...], g_ref[...], dn)
        val = dh_ref[...] + dhn
        if n_main is None:
            o_ref[...] = val
        else:
            @pl.when(i < n_first)
            def _():
                o_ref[...] = val

            @pl.when(i >= n_first)
            def _():
                rest[0][...] = val
        _acc_out(dg_ref, dg, i == 0)

    out_specs = [pl.BlockSpec((tm, d), lambda i: (jnp.minimum(i, n_first - 1), 0))]
    out_shape = [_sds((n_first * tm, d), F32)]
    if n_main is not None:
        out_specs.append(pl.BlockSpec((tm, d), lambda i: (jnp.maximum(i - n_first, 0), 0)))
        out_shape.append(_sds((n_rows - n_main, d), F32))
    return _call(name, body, (n_rows // tm,),
                 [pl.BlockSpec((tm, nj * fc), lambda i: (i, 0)), pl.BlockSpec((tm, d), lambda i: (i, 0)),
                  pl.BlockSpec((1, d), lambda i: (0, 0)), pl.BlockSpec((tm, d), lambda i: (i, 0)),
                  pl.BlockSpec((nj, d, fc), lambda i: (0, 0, 0), pipeline_mode=pl.Buffered(1))],
                 out_specs + [pl.BlockSpec((1, d), lambda i: (0, 0))],
                 out_shape + [_sds((1, d), F32)], after=after)(dgu, h, g, dh, w4)


def _contract_tile(n_rows, cap=2816):
    best = ROW_TILE
    for t in range(ROW_TILE, cap + 1, ROW_TILE):
        if n_rows % t == 0:
            best = t
    return best


def _tn(name, operands, in_specs, prologue, nj, ma, nb, n_rows, tk, out_dtype=F32, after=(), side_by_side=False):
    n_k = n_rows // tk
    out_spec = pl.BlockSpec((1, ma, nb), lambda j, k: (j, 0, 0))
    if out_dtype == F32 and not side_by_side:
        def body(*refs):
            o_ref = refs[-1]
            a, b = prologue(pl.program_id(0), *refs[:-1])
            _acc_out(o_ref, _dot_tn(a, b)[None], pl.program_id(1) == 0)

        return _call(name, body, (nj, n_k), in_specs, out_spec, _sds((nj, ma, nb), F32), after=after)(*operands)

    def body_rounded(*refs):
        o_ref, acc_ref = refs[-2:]
        a, b = prologue(pl.program_id(0), *refs[:-2])
        _acc_out(acc_ref, _dot_tn(a, b), pl.program_id(1) == 0)

        @pl.when(pl.program_id(1) == n_k - 1)
        def _():
            o_ref[...] = acc_ref[...].astype(out_dtype).reshape(o_ref.shape)

    if side_by_side:
        out_spec, out_shape = pl.BlockSpec((ma, nb), lambda j, k: (0, j)), _sds((ma, nj * nb), out_dtype)
    else:
        out_shape = _sds((nj, ma, nb), out_dtype)
    return _call(name, body_rounded, (nj, n_k), in_specs, out_spec, out_shape,
                 scratch=[pltpu.VMEM((ma, nb), F32)], after=after)(*operands)


def _ffn_dwgu(name, n, dgu, nj, n_rows):
    d = n.shape[1]
    fc = dgu.shape[1] // nj
    tk = _contract_tile(n_rows, cap=2816)
    return _tn(name, (dgu, n),
               [pl.BlockSpec((tk, fc), lambda j, k: (k, j)), pl.BlockSpec((tk, d), lambda j, k: (k, 0))],
               lambda j, a_ref, b_ref: (a_ref[...], b_ref[...]), nj, fc, d, n_rows, tk, out_dtype=BF16)


def _ffn_dwd(name, s, dh, n_rows):
    f = s.shape[1]
    d = dh.shape[1]
    tk = _contract_tile(n_rows, cap=2048)
    halves = 2 if tk > 1024 else 1
    return _tn(name, (s, dh),
               [pl.BlockSpec((tk, f), lambda j, k: (k, 0)), pl.BlockSpec((tk, d // halves), lambda j, k: (k, j))],
               lambda j, s_ref, dh_ref: (s_ref[...], 0.5 * dh_ref[...]), halves, f, d // halves, n_rows, tk, out_dtype=BF16,
               side_by_side=True)


def _ffn_fwd(tag, h, g, w4, wd, n_rows, target=None):
    gu, n = _ffn_up(tag + "_up", h, g, w4, n_rows)
    *out, s = _ffn_down(tag + "_down", gu, h, wd, n_rows, target)
    return (out[0] if target is None else tuple(out)), (gu, n, s)


def _ffn_bwd(tag, dh_out, h, g, saved, w4, wd, n_rows, n_main=None, after=()):
    gu, n, s = saved
    nj = w4.shape[0]
    f, d = wd.shape
    dgu = _ffn_dgu(tag + "_dgu", dh_out, gu, wd, n_rows, after=after)
    dwd = _ffn_dwd(tag + "_dwd", s, dh_out, n_rows).reshape(N_CHIPS, f // N_CHIPS, d)
    *dh_parts, dg = _ffn_dh(tag + "_dh", dgu, h, g, dh_out, w4, n_rows, n_main)
    dwgu = _ffn_dwgu(tag + "_dwgu", n, dgu, nj, n_rows)
    dh_in = dh_parts[0] if n_main is None else tuple(dh_parts)
    return dh_in, dg, dwgu, dwd


def _ssm_in(name, h, g, w_in, bb, n_rows):
    d, hw = w_in.shape
    nj, uc, xc = bb.shape
    tm = ROW_TILE

    def body(h_ref, g_ref, w_ref, bb_ref, u_ref, bu_ref):
        u = _dot(_rms(h_ref[...], g_ref[...]), w_ref[...])
        u_ref[...] = u
        for j in range(nj):
            bu_ref[:, j * xc:(j + 1) * xc] = _dot(u[:, j * uc:(j + 1) * uc], bb_ref[j]).astype(BF16)

    return _call(name, body, (n_rows // tm,),
                 [pl.BlockSpec((tm, d), lambda i: (i, 0)), pl.BlockSpec((1, d), lambda i: (0, 0)),
                  pl.BlockSpec((d, hw), lambda i: (0, 0)), pl.BlockSpec((nj, uc, xc), lambda i: (0, 0, 0))],
                 [pl.BlockSpec((tm, hw), lambda i: (i, 0)), pl.BlockSpec((tm, nj * xc), lambda i: (i, 0))],
                 [_sds((n_rows, hw), F32), _sds((n_rows, nj * xc), BF16)])(h, g, w_in, bb)


def _cmul_add(xr, xi, ar, ai, sr, si):
    return xr + ar * sr - ai * si, xi + ar * si + ai * sr


def _scan_row_block(n_main_blocks, seq_blocks):
    return lambda b, i: jnp.where(i == 0, n_main_blocks + b, b * seq_blocks + i - 1)


def _scan_fwd(name, bu, tabs, n_ex, seq):
    n_rows, width = bu.shape
    nj = 4
    cw = width // nj
    half = cw // 2
    tq = META_BLOCK
    seq_blocks = seq // tq
    rb = _scan_row_block(n_ex * seq_blocks, seq_blocks)

    def body(bu_ref, tab_ref, x_ref, carry_ref):
        @pl.when(pl.program_id(1) == 0)
        def _():
            carry_ref[...] = jnp.zeros_like(carry_ref)

        for j in range(nj):
            re, im = slice(j * cw, j * cw + half), slice(j * cw + half, (j + 1) * cw)
            ch = slice(j * half, (j + 1) * half)

            def blk(k, c, re=re, im=im, ch=ch):
                t = [tab_ref[n * SUBLANES:(n + 1) * SUBLANES, ch] for n in range(8)]
                r0 = pl.multiple_of(k * SUBLANES, SUBLANES)
                xr = bu_ref[pl.ds(r0, SUBLANES), re].astype(F32)
                xi = bu_ref[pl.ds(r0, SUBLANES), im].astype(F32)
                for s, d in enumerate((1, 2, 4)):
                    xr, xi = _cmul_add(xr, xi, t[2 * s], t[2 * s + 1], pltpu.roll(xr, d, 0), pltpu.roll(xi, d, 0))
                xr, xi = _cmul_add(xr, xi, t[6], t[7], c[0], c[1])
                x_ref[pl.ds(r0, SUBLANES), re] = xr.astype(BF16)
                x_ref[pl.ds(r0, SUBLANES), im] = xi.astype(BF16)
                last = SUBLANES - 1
                return (jnp.broadcast_to(xr[last:last + 1, :], xr.shape), jnp.broadcast_to(xi[last:last + 1, :], xi.shape))

            c = lax.fori_loop(0, tq // SUBLANES, blk, (carry_ref[0, :, ch], carry_ref[1, :, ch]), unroll=2)
            carry_ref[0, :, ch] = c[0]
            carry_ref[1, :, ch] = c[1]

    return _call(name, body, (n_ex, seq_blocks + 1),
                 [pl.BlockSpec((tq, width), lambda b, i: (rb(b, i), 0)), pl.BlockSpec((8 * SUBLANES, nj * half), lambda b, i: (0, 0))],
                 pl.BlockSpec((tq, width), lambda b, i: (rb(b, i), 0)), _sds((n_rows, width), BF16),
                 scratch=[pltpu.VMEM((2, SUBLANES, nj * half), F32)])(bu, tabs)


def _scan_bwd(name, gx, x, tabs, n_ex, seq, after=()):
    n_rows, width = gx.shape
    nj = 4
    cw = width // nj
    half = cw // 2
    tq = META_BLOCK
    seq_blocks = seq // tq
    n_steps = seq_blocks + 1
    rb = _scan_row_block(n_ex * seq_blocks, seq_blocks)
    rbr = lambda b, i: rb(b, n_steps - 1 - i)

    def body(gx_ref, x_ref, tab_ref, g_ref, da_ref, carry_ref):
        @pl.when(pl.program_id(1) == 0)
        def _():
            carry_ref[...] = jnp.zeros_like(carry_ref)
            da_ref[...] = jnp.zeros_like(da_ref)
        row = lax.broadcasted_iota(jnp.int32, (SUBLANES, half), 0)
        n_blk = tq // SUBLANES

        for j in range(nj):
            re, im = slice(j * cw, j * cw + half), slice(j * cw + half, (j + 1) * cw)
            ch = slice(j * half, (j + 1) * half)

            def blk(kk, st, re=re, im=im, ch=ch):
                t = [tab_ref[n * SUBLANES:(n + 1) * SUBLANES, ch] for n in range(8)]
                cr, ci, dar, dai = st
                r0 = pl.multiple_of((n_blk - 1 - kk) * SUBLANES, SUBLANES)
                gr = gx_ref[pl.ds(r0, SUBLANES), re].astype(F32)
                gi = gx_ref[pl.ds(r0, SUBLANES), im].astype(F32)
                for s, d in enumerate((1, 2, 4)):
                    gr, gi = _cmul_add(gr, gi, t[2 * s], t[2 * s + 1],
                                       pltpu.roll(gr, SUBLANES - d, 0), pltpu.roll(gi, SUBLANES - d, 0))
                gr, gi = _cmul_add(gr, gi, t[6], t[7], cr, ci)
                g_ref[pl.ds(r0, SUBLANES), re] = gr.astype(BF16)
                g_ref[pl.ds(r0, SUBLANES), im] = gi.astype(BF16)
                hr = jnp.where(row == SUBLANES - 1, cr, pltpu.roll(gr, SUBLANES - 1, 0))
                hi = jnp.where(row == SUBLANES - 1, ci, pltpu.roll(gi, SUBLANES - 1, 0))
                xr = x_ref[pl.ds(r0, SUBLANES), re].astype(F32)
                xi = x_ref[pl.ds(r0, SUBLANES), im].astype(F32)
                dar = dar + xr * hr + xi * hi
                dai = dai + xr * hi - xi * hr
                return (jnp.broadcast_to(gr[0:1, :], gr.shape), jnp.broadcast_to(gi[0:1, :], gi.shape), dar, dai)

            st = lax.fori_loop(0, n_blk, blk, (carry_ref[0, :, ch], carry_ref[1, :, ch], da_ref[0, :, re], da_ref[0, :, im]),
                               unroll=2)
            carry_ref[0, :, ch] = st[0]
            carry_ref[1, :, ch] = st[1]
            da_ref[0, :, re] = st[2]
            da_ref[0, :, im] = st[3]

    return _call(name, body, (n_ex, n_steps),
                 [pl.BlockSpec((tq, width), lambda b, i: (rbr(b, i), 0)), pl.BlockSpec((tq, width), lambda b, i: (rbr(b, i), 0)),
                  pl.BlockSpec((8 * SUBLANES, nj * half), lambda b, i: (0, 0))],
                 [pl.BlockSpec((tq, width), lambda b, i: (rbr(b, i), 0)), pl.BlockSpec((1, SUBLANES, width), lambda b, i: (b, 0, 0))],
                 [_sds((n_rows, width), BF16), _sds((n_ex, SUBLANES, width), F32)],
                 scratch=[pltpu.VMEM((2, SUBLANES, nj * half), F32)], after=after)(gx, x, tabs)


def _ssm_z(gy, wout_ref, nj):
    return jnp.concatenate([_dot(gy, wout_ref[j]) for j in range(nj)], axis=1)


def _ssm_out(name, x, u, dskip, cb, wout4, h, n_rows):
    nj, xc, uc = cb.shape
    no, hw, oc = wout4.shape
    d = h.shape[1]
    tm = ROW_TILE

    def body(x_ref, u_ref, ds_ref, cb_ref, w_ref, h_ref, o_ref, y_ref):
        y = jnp.concatenate([_dot(x_ref[:, j * xc:(j + 1) * xc], cb_ref[j]) for j in range(nj)], axis=1)
        y = y + ds_ref[...] * u_ref[...]
        y_ref[...] = y
        z = _ssm_z(_gelu(y), w_ref, no)
        o_ref[...] = h_ref[...] + z[:, :d] * _sigmoid(z[:, d:])

    return _call(name, body, (n_rows // tm,),
                 [pl.BlockSpec((tm, nj * xc), lambda i: (i, 0)), pl.BlockSpec((tm, hw), lambda i: (i, 0)),
                  pl.BlockSpec((1, hw), lambda i: (0, 0)), pl.BlockSpec((nj, xc, uc), lambda i: (0, 0, 0)),
                  pl.BlockSpec((no, hw, oc), lambda i: (0, 0, 0)), pl.BlockSpec((tm, d), lambda i: (i, 0))],
                 [pl.BlockSpec((tm, d), lambda i: (i, 0)), pl.BlockSpec((tm, hw), lambda i: (i, 0))],
                 [_sds((n_rows, d), F32), _sds((n_rows, hw), F32)])(x, u, dskip, cb, wout4, h)


def _ssm_out_bwd(name, dh, y, u, cb, wout4, n_rows, after=()):
    nj, xc, uc = cb.shape
    no, hw, oc = wout4.shape
    d = dh.shape[1]
    tm = ROW_TILE

    def body(dh_ref, y_ref, u_ref, cb_ref, w_ref, dy_ref, dz_ref, gx_ref, dd_ref):
        y = y_ref[...]
        z = _ssm_z(_gelu(y), w_ref, no)
        za = z[:, :d]
        sg = _sigmoid(z[:, d:])
        dmix = dh_ref[...]
        dz = jnp.concatenate([dmix * sg, dmix * za * sg * (1.0 - sg)], axis=1).astype(BF16)
        dz_ref[...] = dz
        dgy = _dot_nt(dz[:, 0:oc], w_ref[0])
        for j in range(1, no):
            dgy = dgy + _dot_nt(dz[:, j * oc:(j + 1) * oc], w_ref[j])
        dy = dgy * _gelu_grad(y)
        dy_ref[...] = dy
        _acc_out(dd_ref, jnp.sum(dy * u_ref[...], axis=0, keepdims=True), pl.program_id(0) == 0)
        for j in range(nj):
            gx_ref[:, j * xc:(j + 1) * xc] = _dot_nt(dy[:, j * uc:(j + 1) * uc], cb_ref[j]).astype(BF16)

    return _call(name, body, (n_rows // tm,),
                 [pl.BlockSpec((tm, d), lambda i: (i, 0)), pl.BlockSpec((tm, hw), lambda i: (i, 0)),
                  pl.BlockSpec((tm, hw), lambda i: (i, 0)), pl.BlockSpec((nj, xc, uc), lambda i: (0, 0, 0)),
                  pl.BlockSpec((no, hw, oc), lambda i: (0, 0, 0))],
                 [pl.BlockSpec((tm, hw), lambda i: (i, 0)), pl.BlockSpec((tm, no * oc), lambda i: (i, 0)),
                  pl.BlockSpec((tm, nj * xc), lambda i: (i, 0)), pl.BlockSpec((1, hw), lambda i: (0, 0))],
                 [_sds((n_rows, hw), F32), _sds((n_rows, no * oc), BF16), _sds((n_rows, nj * xc), BF16),
                  _sds((1, hw), F32)], after=after)(dh, y, u, cb, wout4)


def _ssm_in_bwd(name, gbu, dy, dskip, bb, w_in, h, g, dh, n_rows):
    nj, uc, xc = bb.shape
    d, hw = w_in.shape
    tm = ROW_TILE

    def body(gb_ref, dy_ref, ds_ref, bb_ref, w_ref, h_ref, g_ref, dh_ref, du_ref, o_ref, dg_ref):
        du = jnp.concatenate([_dot_nt(gb_ref[:, j * xc:(j + 1) * xc], bb_ref[j]) for j in range(nj)], axis=1)
        du = du + dy_ref[...] * ds_ref[...]
        du_ref[...] = du.astype(BF16)
        dhn, dg = _rms_bwd(h_ref[...], g_ref[...], _dot_nt(du, w_ref[...]))
        o_ref[...] = dh_ref[...] + dhn
        _acc_out(dg_ref, dg, pl.program_id(0) == 0)

    return _call(name, body, (n_rows // tm,),
                 [pl.BlockSpec((tm, nj * xc), lambda i: (i, 0)), pl.BlockSpec((tm, hw), lambda i: (i, 0)),
                  pl.BlockSpec((1, hw), lambda i: (0, 0)), pl.BlockSpec((nj, uc, xc), lambda i: (0, 0, 0)),
                  pl.BlockSpec((d, hw), lambda i: (0, 0)), pl.BlockSpec((tm, d), lambda i: (i, 0)),
                  pl.BlockSpec((1, d), lambda i: (0, 0)), pl.BlockSpec((tm, d), lambda i: (i, 0))],
                 [pl.BlockSpec((tm, hw), lambda i: (i, 0)), pl.BlockSpec((tm, d), lambda i: (i, 0)),
                  pl.BlockSpec((1, d), lambda i: (0, 0))],
                 [_sds((n_rows, hw), BF16), _sds((n_rows, d), F32), _sds((1, d), F32)])(gbu, dy, dskip, bb, w_in, h, g, dh)


def _discretize(lam_re, lam_im, log_step, b_re, b_im):
    step = jnp.exp(log_step)[:, None]
    mag = jnp.exp(lam_re * step)
    ar = mag * jnp.cos(lam_im * step)
    ai = mag * jnp.sin(lam_im * step)
    den = lam_re * lam_re + lam_im * lam_im
    nr, ni = ar - 1.0, ai
    cr = (nr * lam_re + ni * lam_im) / den
    ci = (ni * lam_re - nr * lam_im) / den
    bbar_r = cr[..., None] * b_re - ci[..., None] * b_im
    bbar_i = cr[..., None] * b_im + ci[..., None] * b_re
    return ar, ai, bbar_r, bbar_i


def _ssm_mats(lam_re, lam_im, log_step, b_re, b_im, c_re, c_im):
    n_g, n_p, n_c = b_re.shape
    gpc = n_g // 4
    ar, ai, bbar_r, bbar_i = _discretize(lam_re, lam_im, log_step, b_re, b_im)
    eye = jnp.eye(gpc, dtype=F32)

    def in_map(bbar):
        return jnp.einsum('jgpc,gh->jgchp', bbar.reshape(4, gpc, n_p, n_c), eye).reshape(4, gpc * n_c, gpc * n_p)

    def out_map(c):
        return jnp.einsum('jgcp,gh->jgphc', c.reshape(4, gpc, n_c, n_p), eye).reshape(4, gpc * n_p, gpc * n_c)

    bb = jnp.concatenate([in_map(bbar_r), in_map(bbar_i)], axis=2)
    cb = jnp.concatenate([out_map(c_re), -out_map(c_im)], axis=1)
    return bb, cb, ar.reshape(-1), ai.reshape(-1)


def _chunked(v, half):
    return v.reshape(v.shape[:-1] + (4, half))


def _scan_tables(ar, ai, reverse):
    if reverse:
        ai = -ai
    pr, pi = [ar], [ai]
    for _ in range(SUBLANES - 1):
        pr, pi = pr + [pr[-1] * ar - pi[-1] * ai], pi + [pr[-1] * ai + pi[-1] * ar]
    row = jnp.arange(SUBLANES)[:, None]
    tabs = []
    for d in (1, 2, 4):
        keep = (row <= SUBLANES - 1 - d) if reverse else (row >= d)
        tabs += [jnp.where(keep, pr[d - 1][None, :], 0.0), jnp.where(keep, pi[d - 1][None, :], 0.0)]
    order = list(range(SUBLANES))[::-1] if reverse else list(range(SUBLANES))
    tabs += [jnp.stack([pr[k] for k in order]), jnp.stack([pi[k] for k in order])]
    return jnp.concatenate(tabs, axis=0)


def _kv_proj(name, h, g, w_kv, k_gain_t, cos2, sin2, n_rows, n_kv, hd):
    d, kvw = w_kv.shape
    kw = n_kv * hd
    tm = ROW_TILE

    sel, sel_t = _head_selectors(n_kv, hd)

    def body(h_ref, g_ref, w_ref, kg_ref, c_ref, s_ref, e_ref, et_ref, raw_ref, k_ref, v_ref):
        raw = _dot(_rms(h_ref[...], g_ref[...]), w_ref[...])
        raw_ref[...] = raw
        k_ref[...] = _head_prep(raw[:, :kw], kg_ref[...], c_ref[...], s_ref[...], e_ref[...], et_ref[...], hd).astype(BF16)
        v_ref[...] = raw[:, kw:].astype(BF16)

    return _call(name, body, (n_rows // tm,),
                 [pl.BlockSpec((tm, d), lambda i: (i, 0)), pl.BlockSpec((1, d), lambda i: (0, 0)),
                  pl.BlockSpec((d, kvw), lambda i: (0, 0)), pl.BlockSpec((1, kw), lambda i: (0, 0)),
                  pl.BlockSpec((tm, 2 * hd), lambda i: (i, 0)), pl.BlockSpec((tm, 2 * hd), lambda i: (i, 0)),
                  pl.BlockSpec(sel.shape, lambda i: (0, 0)), pl.BlockSpec(sel_t.shape, lambda i: (0, 0))],
                 [pl.BlockSpec((tm, kvw), lambda i: (i, 0)), pl.BlockSpec((tm, kw), lambda i: (i, 0)),
                  pl.BlockSpec((tm, kw), lambda i: (i, 0))],
                 [_sds((n_rows, kvw), F32), _sds((n_rows, kw), BF16), _sds((n_rows, kw), BF16)])(
                     h, g, w_kv, k_gain_t, cos2, sin2, sel, sel_t)


def _q_proj(name, h, g, w_q, q_gain_t, cos2, sin2, n_rows, n_q, hd):
    d, qw = w_q.shape
    tm = ROW_TILE

    sel, sel_t = _head_selectors(n_q, hd)

    def body(h_ref, g_ref, w_ref, qg_ref, c_ref, s_ref, e_ref, et_ref, raw_ref, q_ref):
        raw = _dot(_rms(h_ref[...], g_ref[...]), w_ref[...])
        raw_ref[...] = raw
        q_ref[...] = _head_prep(raw, qg_ref[...], c_ref[...], s_ref[...], e_ref[...], et_ref[...], hd).astype(BF16)

    return _call(name, body, (n_rows // tm,),
                 [pl.BlockSpec((tm, d), lambda i: (i, 0)), pl.BlockSpec((1, d), lambda i: (0, 0)),
                  pl.BlockSpec((d, qw), lambda i: (0, 0)), pl.BlockSpec((1, qw), lambda i: (0, 0)),
                  pl.BlockSpec((tm, 2 * hd), lambda i: (i, 0)), pl.BlockSpec((tm, 2 * hd), lambda i: (i, 0)),
                  pl.BlockSpec(sel.shape, lambda i: (0, 0)), pl.BlockSpec(sel_t.shape, lambda i: (0, 0))],
                 [pl.BlockSpec((tm, qw), lambda i: (i, 0)), pl.BlockSpec((tm, qw), lambda i: (i, 0))],
                 [_sds((n_rows, qw), F32), _sds((n_rows, qw), BF16)])(h, g, w_q, q_gain_t, cos2, sin2, sel, sel_t)


def _attn_specs(seq, n_ex, n_meta, kw):
    nb = seq // WINDOW
    meta_blk = lambda b: (n_ex * seq + META_BLOCK * b + META_BLOCK - n_meta) // n_meta
    return [pl.BlockSpec((WINDOW, kw), lambda b, n: (b * nb + jnp.maximum(n - 1, 0), 0)),
            pl.BlockSpec((WINDOW, kw), lambda b, n: (b * nb + n, 0)),
            pl.BlockSpec((n_meta, kw), lambda b, n: (meta_blk(b), 0))]


def _attn_bias(qpk, n_keys):
    rows = qpk * WINDOW
    qi = jnp.arange(rows)[:, None] & (WINDOW - 1)
    kj = jnp.arange(n_keys)[None, :]
    rel = qi + WINDOW - kj
    band = (rel >= 0) & (rel < WINDOW)
    meta = kj >= 2 * WINDOW
    first = (band & (kj >= WINDOW)) | meta
    return jnp.where(jnp.stack([first, band | meta]), 0.0, NEG_INF).astype(F32)


def _stack_heads(ref, h, qpk, hd, dtype=None):
    parts = [ref[:, (h * qpk + gq) * hd:(h * qpk + gq + 1) * hd] for gq in range(qpk)]
    out = jnp.concatenate(parts, axis=0)
    return out if dtype is None else out.astype(dtype)


def _col(tile, c):
    lane = lax.broadcasted_iota(jnp.int32, tile.shape, 1)
    return jnp.sum(jnp.where(lane == c, tile, 0.0), axis=-1, keepdims=True)


def _put_col(col, c, n):
    lane = lax.broadcasted_iota(jnp.int32, (col.shape[0], n), 1)
    return jnp.where(lane == c, col, 0.0)


def _stack_cols(tile, h, qpk):
    return jnp.concatenate([_col(tile, h * qpk + gq) for gq in range(qpk)], axis=0)


def _sink_col(sinks, h, qpk):
    return jnp.concatenate([jnp.broadcast_to(_col(sinks, h * qpk + gq), (WINDOW, 1)) for gq in range(qpk)], axis=0)


def _attn_fwd(name, q, k, v, sinks, n_ex, seq, n_meta, n_kv, qpk, hd):
    nb = seq // WINDOW
    n_q = n_kv * qpk
    kw = n_kv * hd
    qw = n_q * hd
    n_keys = 2 * WINDOW + n_meta
    bias = _attn_bias(qpk, n_keys)

    def body(q_ref, kp_ref, kc_ref, km_ref, vp_ref, vc_ref, vm_ref, sk_ref, bias_ref, o_ref, lse_ref):
        sinks_v = sk_ref[...]
        o_parts = []
        lse_all = jnp.zeros((WINDOW, n_q), F32)
        for h in range(n_kv):
            hs = slice(h * hd, (h + 1) * hd)
            kb = jnp.concatenate([kp_ref[:, hs], kc_ref[:, hs], km_ref[:, hs]], axis=0)
            vb = jnp.concatenate([vp_ref[:, hs], vc_ref[:, hs], vm_ref[:, hs]], axis=0)
            s = _dot_nt(_stack_heads(q_ref, h, qpk, hd), kb) + bias_ref[0]
            skc = _sink_col(sinks_v, h, qpk)
            m = jnp.maximum(jnp.max(s, axis=-1, keepdims=True), skc)
            p = jnp.exp(s - m)
            den = jnp.sum(p, axis=-1, keepdims=True) + jnp.exp(skc - m)
            o = _dot(p, vb) / den
            lse = m + jnp.log(den)
            for gq in range(qpk):
                o_parts.append(o[gq * WINDOW:(gq + 1) * WINDOW])
                lse_all = lse_all + _put_col(lse[gq * WINDOW:(gq + 1) * WINDOW], h * qpk + gq, n_q)
        o_ref[...] = jnp.concatenate(o_parts, axis=1).astype(BF16)
        lse_ref[...] = lse_all

    qspec = pl.BlockSpec((WINDOW, qw), lambda b, n: (b * nb + n, 0))
    return _call(name, body, (n_ex, nb),
                 [qspec] + _attn_specs(seq, n_ex, n_meta, kw) + _attn_specs(seq, n_ex, n_meta, kw)
                 + [pl.BlockSpec((1, n_q), lambda b, n: (0, 0)),
                    pl.BlockSpec((1,) + bias.shape[1:], lambda b, n: (jnp.minimum(n, 1), 0, 0))],
                 [qspec, pl.BlockSpec((WINDOW, n_q), lambda b, n: (b * nb + n, 0))],
                 [_sds((n_ex * seq, qw), BF16), _sds((n_ex * seq, n_q), F32)])(q, k, k, k, v, v, v, sinks, bias)


def _attn_bwd(name, q, k, v, sinks, o, lse, do, n_ex, seq, n_meta, n_kv, qpk, hd):
    nb = seq // WINDOW
    n_q = n_kv * qpk
    kw = n_kv * hd
    qw = n_q * hd
    n_keys = 2 * WINDOW + n_meta
    bias = _attn_bias(qpk, n_keys)

    def body(q_ref, kp_ref, kc_ref, km_ref, vp_ref, vc_ref, vm_ref, sk_ref, o_ref, lse_ref, do_ref, bias_ref,
             dq_ref, dk_ref, dv_ref, dkm_ref, dvm_ref, dsk_ref):
        n = pl.program_id(1)

        @pl.when(n == 0)
        def _():
            dk_ref[...] = jnp.zeros_like(dk_ref)
            dv_ref[...] = jnp.zeros_like(dv_ref)
            dkm_ref[...] = jnp.zeros_like(dkm_ref)
            dvm_ref[...] = jnp.zeros_like(dvm_ref)

        @pl.when((n == 0) & (pl.program_id(0) == 0))
        def _():
            dsk_ref[...] = jnp.zeros_like(dsk_ref)

        sinks_v = sk_ref[...]
        lse_v = lse_ref[...]
        dq_parts, dk_parts, dv_parts = [], [], []
        dsk = jnp.zeros((1, n_q), F32)
        for h in range(n_kv):
            hs = slice(h * hd, (h + 1) * hd)
            kb = jnp.concatenate([kp_ref[:, hs], kc_ref[:, hs], km_ref[:, hs]], axis=0)
            vb = jnp.concatenate([vp_ref[:, hs], vc_ref[:, hs], vm_ref[:, hs]], axis=0)
            qs = _stack_heads(q_ref, h, qpk, hd)
            dos = _stack_heads(do_ref, h, qpk, hd)
            delta = jnp.sum(dos.astype(F32) * _stack_heads(o_ref, h, qpk, hd, F32), axis=-1, keepdims=True)
            lse_c = _stack_cols(lse_v, h, qpk)
            p = jnp.exp(_dot_nt(qs, kb) + bias_ref[0] - lse_c)
            ds = p * (_dot_nt(dos, vb) - delta)
            dqs = _dot(ds, kb)
            dk_parts.append(_dot_tn(ds, qs))
            dv_parts.append(_dot_tn(p, dos))
            dsink = -jnp.exp(_sink_col(sinks_v, h, qpk) - lse_c) * delta
            for gq in range(qpk):
                dq_parts.append(dqs[gq * WINDOW:(gq + 1) * WINDOW])
                dsk = dsk + _put_col(jnp.sum(dsink[gq * WINDOW:(gq + 1) * WINDOW], axis=0, keepdims=True), h * qpk + gq, n_q)
        dq_ref[...] = jnp.concatenate(dq_parts, axis=1).astype(BF16)
        dsk_ref[...] += dsk
        dkb = jnp.concatenate(dk_parts, axis=1)
        dvb = jnp.concatenate(dv_parts, axis=1)
        prev = pl.ds(pl.multiple_of(jnp.maximum(n - 1, 0) * WINDOW, WINDOW), WINDOW)
        cur = pl.ds(pl.multiple_of(n * WINDOW, WINDOW), WINDOW)
        dk_ref[prev, :] += dkb[0:WINDOW]
        dv_ref[prev, :] += dvb[0:WINDOW]
        dk_ref[cur, :] += dkb[WINDOW:2 * WINDOW]
        dv_ref[cur, :] += dvb[WINDOW:2 * WINDOW]
        dkm_ref[...] += dkb[2 * WINDOW:]
        dvm_ref[...] += dvb[2 * WINDOW:]

    qspec = pl.BlockSpec((WINDOW, qw), lambda b, n: (b * nb + n, 0))
    exspec = pl.BlockSpec((seq, kw), lambda b, n: (b, 0))
    mspec = pl.BlockSpec((n_meta, kw), lambda b, n: (b, 0))
    return _call(name, body, (n_ex, nb),
                 [qspec] + _attn_specs(seq, n_ex, n_meta, kw) + _attn_specs(seq, n_ex, n_meta, kw)
                 + [pl.BlockSpec((1, n_q), lambda b, n: (0, 0)), qspec,
                    pl.BlockSpec((WINDOW, n_q), lambda b, n: (b * nb + n, 0)), qspec,
                    pl.BlockSpec((1,) + bias.shape[1:], lambda b, n: (jnp.minimum(n, 1), 0, 0))],
                 [qspec, exspec, exspec, mspec, mspec, pl.BlockSpec((1, n_q), lambda b, n: (0, 0))],
                 [_sds((n_ex * seq, qw), BF16), _sds((n_ex * seq, kw), F32), _sds((n_ex * seq, kw), F32),
                  _sds((n_ex * n_meta, kw), F32), _sds((n_ex * n_meta, kw), F32), _sds((1, n_q), F32)])(
                      q, k, k, k, v, v, v, sinks, o, lse, do, bias)


def _attn_out(name, o, h, w_o, n_rows):
    qw, d = w_o.shape
    tm = ROW_TILE

    def body(o_ref, h_ref, w_ref, out_ref):
        out_ref[...] = h_ref[...] + _dot(o_ref[...], w_ref[...])

    return _call(name, body, (n_rows // tm,),
                 [pl.BlockSpec((tm, qw), lambda i: (i, 0)), pl.BlockSpec((tm, d), lambda i: (i, 0)),
                  pl.BlockSpec((qw, d), lambda i: (0, 0))],
                 pl.BlockSpec((tm, d), lambda i: (i, 0)), _sds((n_rows, d), F32))(o, h, w_o)


def _attn_out_bwd(name, dh, w_o, n_rows):
    qw, d = w_o.shape
    tm = ROW_TILE

    def body(dh_ref, w_ref, do_ref):
        do_ref[...] = _dot_nt(dh_ref[...], w_ref[...]).astype(BF16)

    return _call(name, body, (n_rows // tm,),
                 [pl.BlockSpec((tm, d), lambda i: (i, 0)), pl.BlockSpec((qw, d), lambda i: (0, 0))],
                 pl.BlockSpec((tm, qw), lambda i: (i, 0)), _sds((n_rows, qw), BF16))(dh, w_o)


def _q_bwd(name, dq, qraw, q_gain_t, cos2, sin2, w_q, h, g, dh, n_rows, n_q, hd):
    d, qw = w_q.shape
    tm = ROW_TILE

    sel, sel_t = _head_selectors(n_q, hd)

    def body(dq_ref, raw_ref, qg_ref, c_ref, s_ref, e_ref, et_ref, w_ref, h_ref, g_ref, dh_ref, draw_ref, o_ref, dqg_ref, dg_ref):
        dx, dgain = _head_prep_bwd(raw_ref[...], qg_ref[...], c_ref[...], s_ref[...], e_ref[...], et_ref[...],
                                   dq_ref[...].astype(F32), hd)
        draw = dx.astype(BF16)
        draw_ref[...] = draw
        dhn, dg = _rms_bwd(h_ref[...], g_ref[...], _dot_nt(draw, w_ref[...]))
        o_ref[...] = dh_ref[...] + dhn
        first = pl.program_id(0) == 0
        _acc_out(dqg_ref, dgain, first)
        _acc_out(dg_ref, dg, first)

    row = lambda w: pl.BlockSpec((tm, w), lambda i: (i, 0))
    one = lambda w: pl.BlockSpec((1, w), lambda i: (0, 0))
    return _call(name, body, (n_rows // tm,),
                 [row(qw), row(qw), one(qw), row(2 * hd), row(2 * hd), pl.BlockSpec(sel.shape, lambda i: (0, 0)),
                  pl.BlockSpec(sel_t.shape, lambda i: (0, 0)), pl.BlockSpec((d, qw), lambda i: (0, 0)), row(d), one(d), row(d)],
                 [row(qw), row(d), one(qw), one(d)],
                 [_sds((n_rows, qw), BF16), _sds((n_rows, d), F32), _sds((1, qw), F32), _sds((1, d), F32)])(
                     dq, qraw, q_gain_t, cos2, sin2, sel, sel_t, w_q, h, g, dh)


def _kv_bwd(name, dk, dv, kvraw, k_gain_t, cos2, sin2, w_kv, h, g, dh_main, n_rows, n_main, n_kv, hd, after=()):
    d, kvw = w_kv.shape
    kw = n_kv * hd
    tm = ROW_TILE
    n_main_tiles = n_main // tm

    sel, sel_t = _head_selectors(n_kv, hd)

    def body(dk_ref, dv_ref, raw_ref, kg_ref, c_ref, s_ref, e_ref, et_ref, w_ref, h_ref, g_ref, dh_ref, draw_ref, o_ref, dkg_ref,
             dg_ref):
        i = pl.program_id(0)
        dx, dgain = _head_prep_bwd(raw_ref[:, :kw], kg_ref[...], c_ref[...], s_ref[...], e_ref[...], et_ref[...], dk_ref[...], hd)
        draw = jnp.concatenate([dx, dv_ref[...]], axis=1).astype(BF16)
        draw_ref[...] = draw
        dhn, dg = _rms_bwd(h_ref[...], g_ref[...], _dot_nt(draw, w_ref[...]))
        o_ref[...] = jnp.where(i < n_main_tiles, dh_ref[...], 0.0) + dhn
        _acc_out(dkg_ref, dgain, i == 0)
        _acc_out(dg_ref, dg, i == 0)

    row = lambda w: pl.BlockSpec((tm, w), lambda i: (i, 0))
    one = lambda w: pl.BlockSpec((1, w), lambda i: (0, 0))
    return _call(name, body, (n_rows // tm,),
                 [row(kw), row(kw), row(kvw), one(kw), row(2 * hd), row(2 * hd), pl.BlockSpec(sel.shape, lambda i: (0, 0)),
                  pl.BlockSpec(sel_t.shape, lambda i: (0, 0)), pl.BlockSpec((d, kvw), lambda i: (0, 0)), row(d),
                  one(d), pl.BlockSpec((tm, d), lambda i: (jnp.minimum(i, n_main_tiles - 1), 0))],
                 [row(kvw), row(d), one(kw), one(d)],
                 [_sds((n_rows, kvw), BF16), _sds((n_rows, d), F32), _sds((1, kw), F32), _sds((1, d), F32)], after=after)(
                     dk, dv, kvraw, k_gain_t, cos2, sin2, sel, sel_t, w_kv, h, g, dh_main)


def _tn_rms(name, h, g, b, n_rows, out_dtype=F32):
    d = h.shape[1]
    nb = b.shape[1]
    tk = _contract_tile(n_rows)
    return _tn(name, (h, g, b),
               [pl.BlockSpec((tk, d), lambda j, k: (k, 0)), pl.BlockSpec((1, d), lambda j, k: (0, 0)),
                pl.BlockSpec((tk, nb), lambda j, k: (k, 0))],
               lambda j, h_ref, g_ref, b_ref: (_rms(h_ref[...], g_ref[...]), b_ref[...]), 1, d, nb, n_rows, tk, out_dtype=out_dtype)


def _tn_plain(name, a, b, nj, a_cols, b_cols, n_rows, a_fn=None, out_dtype=F32, after=()):
    tk = _contract_tile(n_rows)
    fa = (lambda v: v) if a_fn is None else a_fn
    a_map = (lambda j, k: (k, j)) if a.shape[1] != a_cols else (lambda j, k: (k, 0))
    b_map = (lambda j, k: (k, j)) if b.shape[1] != b_cols else (lambda j, k: (k, 0))
    return _tn(name, (a, b), [pl.BlockSpec((tk, a_cols), a_map), pl.BlockSpec((tk, b_cols), b_map)],
               lambda j, a_ref, b_ref: (fa(a_ref[...]), b_ref[...]), nj, a_cols, b_cols, n_rows, tk, out_dtype=out_dtype,
               after=after)


def _cast_layer(name, a, layer):
    _, r, c = a.shape
    tr = _row_tile(r, 256)

    def body(a_ref, o_ref):
        o_ref[...] = a_ref[0].astype(BF16)

    return _call(name, body, (r // tr,), [pl.BlockSpec((1, tr, c), lambda i: (layer, i, 0))],
                 pl.BlockSpec((tr, c), lambda i: (i, 0)), _sds((r, c), BF16))(a)


def _adamw_math(w, g, m, v):
    c1 = 1.0 - ADAM_B1 ** ADAM_STEP
    c2 = 1.0 - ADAM_B2 ** ADAM_STEP
    nm = ADAM_B1 * m + (1.0 - ADAM_B1) * g
    nv = ADAM_B2 * v + (1.0 - ADAM_B2) * (g * g)
    return -ADAM_LR * ((nm / c1) / (jnp.sqrt(nv / c2) + ADAM_EPS) + ADAM_WD * w), nm, nv


def _adamw(name, w, g, m, v, after=()):
    rows, cols = w.shape
    tr = 128 if rows % 128 == 0 else rows

    def body(w_ref, g_ref, m_ref, v_ref, d_ref, nm_ref, nv_ref):
        d_ref[...], nm_ref[...], nv_ref[...] = _adamw_math(w_ref[...], g_ref[...], m_ref[...], v_ref[...])

    spec = pl.BlockSpec((tr, cols), lambda i: (i, 0))
    return _call(name, body, (rows // tr,), [spec] * 4, [spec] * 3, [_sds((rows, cols), F32)] * 3, after=after)(w, g, m, v)


def _adamw_from_halves(name, w, m, v, sources, half_index, transposed, after=()):
    n_layers, r, c = w.shape
    lanes = 1024
    after = tuple(after)
    if transposed:
        rows_half, tr = c // 2, 128
        grid = (n_layers, r // tr)
        w_spec = pl.BlockSpec((1, tr, c), lambda l, i, s: (l, i, 0))
        g_spec = lambda off: pl.BlockSpec((rows_half, tr), lambda l, i, s: (off // rows_half, i))
    else:
        rows_half = r // 2
        grid = (n_layers, 2)
        w_spec = pl.BlockSpec((1, rows_half, c), lambda l, k, s: (l, k, 0))
        g_spec = lambda off: pl.BlockSpec((rows_half, lanes), lambda l, k, s: (off // rows_half, 0))

    def body(s_ref, w_ref, m_ref, v_ref, t0_ref, o0_ref, t1_ref, o1_ref, *rest):
        g_ref, d_ref, nm_ref, nv_ref = rest[len(after):]
        layer, k, mine = pl.program_id(0), pl.program_id(1), s_ref[0]
        tot = jnp.where(layer == 0, t0_ref[...], t1_ref[...])
        oth = jnp.where(layer == 0, o0_ref[...], o1_ref[...])
        if transposed:
            g = jnp.concatenate([jnp.where(mine == 0, tot, oth), jnp.where(mine == 0, oth, tot)], axis=0).T
        else:
            g = jnp.where(k == mine, tot, oth)
        g_ref[0] = g
        d_ref[0], nm_ref[0], nv_ref[0] = _adamw_math(w_ref[0], g, m_ref[0], v_ref[0])

    (t0, o0, off0), (t1, o1, off1) = sources
    grid_spec = pltpu.PrefetchScalarGridSpec(
        num_scalar_prefetch=1, grid=grid,
        in_specs=[w_spec] * 3 + [g_spec(off0), g_spec(off0), g_spec(off1), g_spec(off1)] + [_ANY] * len(after),
        out_specs=[w_spec] * 4)
    return pl.pallas_call(
        body, name=name, grid_spec=grid_spec, out_shape=[_sds(w.shape, F32)] * 4,
        compiler_params=pltpu.CompilerParams(dimension_semantics=("arbitrary", "arbitrary"),
                                             vmem_limit_bytes=V7X_VMEM_LIMIT))(half_index, w, m, v, t0, o0, t1, o1, *after)


def _position():
    return lax.axis_index("x"), lax.axis_index("y"), lax.axis_index("c")


def _other_chips(x, y):
    return [(1 - x, y), (x, 1 - y), (1 - x, 1 - y)]


def _peers_chips(x, y, c):
    return [(cx, cy, c) for cx, cy in _other_chips(x, y)]


def _peers_sibling(x, y, c):
    return [(x, y, 1 - c)]


def _peers_chips_and_sibling(x, y, c):
    return _peers_chips(x, y, c) + _peers_sibling(x, y, c)


def _comm_call(name, body, n_in, out_shape, scratch, sequencer=None):
    if sequencer is None:
        return pl.pallas_call(
            body, name=name, in_specs=[_HBM] * n_in, out_specs=[_HBM] * len(out_shape), out_shape=out_shape,
            scratch_shapes=list(scratch),
            compiler_params=pltpu.CompilerParams(has_side_effects=True, vmem_limit_bytes=V7X_VMEM_LIMIT))
    collective_id, peers = sequencer

    def seq_body(*refs):
        barrier = pltpu.get_barrier_semaphore()
        plist = peers(*_position())
        for peer in plist:
            pl.semaphore_signal(barrier, inc=1, device_id=peer, device_id_type=MESH)
        pl.semaphore_wait(barrier, len(plist))
        body(*refs)

    return pl.kernel(seq_body, out_type=out_shape, mesh=plsc.ScalarSubcoreMesh(axis_name="sequencer", num_cores=1), name=name,
                     scratch_types=list(scratch), compiler_params=pltpu.CompilerParams(collective_id=collective_id))


def _n_chunks(rows, want, dtype):
    align = 16 if dtype == BF16 else 8
    n = want
    while n > 1 and (rows % n or (rows // n) % align):
        n -= 1
    return n


def _remote(src, dst, send_sem, recv_sem, device):
    return pltpu.make_async_remote_copy(src_ref=src, dst_ref=dst, send_sem=send_sem, recv_sem=recv_sem,
                                        device_id=device, device_id_type=MESH)


def _start_in_chunks(src, dst, send_sem, recv_sem, device, want=8):
    rows = src.shape[0]
    n = _n_chunks(rows, want, src.dtype)
    for i in range(n):
        part = pl.ds(i * (rows // n), rows // n)
        _remote(src.at[part], dst.at[part], send_sem, recv_sem, device).start()


def _all_gather_chips(name, shards, split, collective_id=None):
    n = len(shards)

    def body(*refs):
        ins, outs = refs[:n], refs[n:2 * n]
        send_sems, recv_sems, local_sems = refs[2 * n:]
        x, y, c = _position()
        me = 2 * x + y
        chips = _other_chips(x, y)
        sibling = (x, y, 1 - c)
        sends, forwards = [], []
        for t in range(n):
            pltpu.make_async_copy(ins[t], outs[t].at[me], local_sems.at[t]).start()
        for t in range(n):
            r = ins[t].shape[0]
            rows = pl.ds(c * (r // 2), r // 2) if split[t] else pl.ds(0, r)
            for k, (cx, cy) in enumerate(chips):
                src, dst = ins[t].at[rows], outs[t].at[me, rows]
                _start_in_chunks(src, dst, send_sems.at[t, k], recv_sems.at[t, k], (cx, cy, c), want=4)
                sends.append(_remote(src, dst, send_sems.at[t, k], recv_sems.at[t, k], (cx, cy, c)))
        for t in range(n):
            r = ins[t].shape[0]
            rows = pl.ds(c * (r // 2), r // 2) if split[t] else pl.ds(0, r)
            for k, (cx, cy) in enumerate(chips):
                landed = outs[t].at[2 * cx + cy, rows]
                _remote(landed, landed, send_sems.at[t, k], recv_sems.at[t, k], (cx, cy, c)).wait_recv()
                if split[t]:
                    _start_in_chunks(landed, landed, send_sems.at[t, 3 + k], recv_sems.at[t, 3 + k], sibling, want=4)
                    forwards.append(_remote(landed, landed, send_sems.at[t, 3 + k], recv_sems.at[t, 3 + k], sibling))
        for t in range(n):
            if split[t]:
                r = ins[t].shape[0]
                other = pl.ds((1 - c) * (r // 2), r // 2)
                for k, (cx, cy) in enumerate(chips):
                    landed = outs[t].at[2 * cx + cy, other]
                    pltpu.make_async_remote_copy(
                        src_ref=landed, dst_ref=landed, send_sem=send_sems.at[t, 3 + k], recv_sem=recv_sems.at[t, 3 + k],
                        device_id=sibling, device_id_type=MESH).wait_recv()
        for cp in sends + forwards:
            cp.wait_send()
        for t in range(n):
            pltpu.make_async_copy(ins[t], outs[t].at[me], local_sems.at[t]).wait()

    out_shape = [_sds((N_CHIPS,) + s.shape, s.dtype) for s in shards]
    sequencer = None if collective_id is None else (collective_id, _peers_chips_and_sibling)
    return _comm_call(name, body, n, out_shape,
                      [pltpu.SemaphoreType.DMA((n, 6)), pltpu.SemaphoreType.DMA((n, 6)), pltpu.SemaphoreType.DMA((n,))],
                      sequencer)(*shards)


def _swap_halves_with_sibling(name, blob, collective_id=None):
    def body(b_ref, theirs_ref, send_sem, recv_sem):
        x, y, c = _position()
        sibling = (x, y, 1 - c)
        for k in range(b_ref.shape[1]):
            _start_in_chunks(b_ref.at[1 - c, k], theirs_ref.at[k], send_sem, recv_sem, sibling)
        _remote(b_ref.at[1 - c], theirs_ref, send_sem, recv_sem, sibling).wait()

    return _comm_call(name, body, 1, [_sds(blob.shape[1:], blob.dtype)],
                      [pltpu.SemaphoreType.DMA(()), pltpu.SemaphoreType.DMA(())],
                      None if collective_id is None else (collective_id, _peers_sibling))(blob)[0]


def _scatter_to_chips(name, parts, collective_id=None):
    def body(p_ref, o_ref, send_sems, recv_sems, local_sems):
        x, y, c = _position()
        me = 2 * x + y
        rows = p_ref.shape[1]
        n_loc = _n_chunks(rows, 16, p_ref.dtype)
        locs = [pltpu.make_async_copy(p_ref.at[me, pl.ds(i * (rows // n_loc), rows // n_loc)],
                                      o_ref.at[me, pl.ds(i * (rows // n_loc), rows // n_loc)], local_sems.at[i])
                for i in range(n_loc)]
        for loc in locs:
            loc.start()
        sends = []
        for k, (cx, cy) in enumerate(_other_chips(x, y)):
            src, dst = p_ref.at[2 * cx + cy], o_ref.at[me]
            _start_in_chunks(src, dst, send_sems.at[k], recv_sems.at[k], (cx, cy, c))
            sends.append(_remote(src, dst, send_sems.at[k], recv_sems.at[k], (cx, cy, c)))
        for k, (cx, cy) in enumerate(_other_chips(x, y)):
            landed = o_ref.at[2 * cx + cy]
            _remote(landed, landed, send_sems.at[k], recv_sems.at[k], (cx, cy, c)).wait_recv()
        for cp in sends:
            cp.wait_send()
        for loc in locs:
            loc.wait()

    def local_sems_shape(rows):
        return pltpu.SemaphoreType.DMA((_n_chunks(rows, 16, parts.dtype),))

    return _comm_call(name, body, 1, [_sds(parts.shape, parts.dtype)],
                      [pltpu.SemaphoreType.DMA((3,)), pltpu.SemaphoreType.DMA((3,)), local_sems_shape(parts.shape[1])],
                      None if collective_id is None else (collective_id, _peers_chips))(parts)[0]


def _share_with_sibling(name, mine, collective_id=None):
    def body(m_ref, o_ref, send_sem, recv_sem):
        x, y, c = _position()
        sibling = (x, y, 1 - c)
        _start_in_chunks(m_ref, o_ref, send_sem, recv_sem, sibling, want=16)
        _remote(m_ref, o_ref, send_sem, recv_sem, sibling).wait()

    return _comm_call(name, body, 1, [_sds(mine.shape, mine.dtype)],
                      [pltpu.SemaphoreType.DMA(()), pltpu.SemaphoreType.DMA(())],
                      None if collective_id is None else (collective_id, _peers_sibling))(mine)[0]


def _row_tile(rows, cap=640):
    best = rows
    for t in range(16, min(rows, cap) + 1, 16):
        if rows % t == 0:
            best = t
    return best


_ANY = pl.BlockSpec(memory_space=pl.ANY)


def _add_my_half(name, blob, theirs, half_index, out_dtype, after):
    n, rows, cols = theirs.shape
    tr = _row_tile(rows)
    after = tuple(after)

    def body(c_ref, a_ref, b_ref, *rest):
        o_ref = rest[-1]
        o_ref[...] = (a_ref[0].astype(F32) + b_ref[...].astype(F32)).astype(out_dtype)

    spec = pl.BlockSpec((1, tr, cols), lambda k, i, c: (k, i, 0))
    grid_spec = pltpu.PrefetchScalarGridSpec(
        num_scalar_prefetch=1, grid=(n, rows // tr),
        in_specs=[pl.BlockSpec((1, 1, tr, cols), lambda k, i, c: (c[0], k, i, 0)), spec] + [_ANY] * len(after), out_specs=spec)
    return pl.pallas_call(
        body, name=name, grid_spec=grid_spec, out_shape=_sds(theirs.shape, out_dtype),
        compiler_params=pltpu.CompilerParams(dimension_semantics=("arbitrary", "arbitrary"),
                                             vmem_limit_bytes=V7X_VMEM_LIMIT))(half_index, blob, theirs, *after)


def _sum_slots(name, parts, after):
    n, rows, cols = parts.shape
    tr = _row_tile(rows)

    def body(p_ref, o_ref):
        acc = p_ref[0].astype(F32)
        for k in range(1, n):
            acc = acc + p_ref[k].astype(F32)
        o_ref[...] = acc

    return _call(name, body, (rows // tr,), [pl.BlockSpec((n, tr, cols), lambda i: (0, i, 0))],
                 pl.BlockSpec((tr, cols), lambda i: (i, 0)), _sds((rows, cols), F32), after=after)(parts)


def _reduce_small_adamw(name, grads, loss_tile, ws, ms, vs, after=()):
    n = len(grads)
    srcs = list(grads) + [loss_tile]
    after = tuple(after)

    def body(*refs):
        refs = refs[:4 * n + 1] + refs[4 * n + 1 + len(after):]
        g_in, w_in, m_in, v_in = refs[:n + 1], refs[n + 1:2 * n + 1], refs[2 * n + 1:3 * n + 1], refs[3 * n + 1:4 * n + 1]
        outs = refs[4 * n + 1:8 * n + 2]
        g_out, d_out, nm_out, nv_out, loss_out = outs[:n], outs[n:2 * n], outs[2 * n:3 * n], outs[3 * n:4 * n], outs[4 * n]
        bufs = refs[8 * n + 2:9 * n + 3]
        send_sems, recv_sems = refs[9 * n + 3:]
        x, y, c = _position()
        me = 4 * x + 2 * y + c
        chip = 2 * x + y
        peers = [(1 - x if dlt & 4 else x, 1 - y if dlt & 2 else y, 1 - c if dlt & 1 else c) for dlt in range(1, N_DEV)]
        sends = []
        for t in range(n + 1):
            bufs[t][me] = g_in[t][...]
            for k, peer in enumerate(peers):
                cp = _remote(g_in[t], bufs[t].at[me], send_sems.at[t, k], recv_sems.at[t, k], peer)
                cp.start()
                sends.append(cp)
        for t in range(n + 1):
            for k, (tx, ty, tc) in enumerate(peers):
                landed = bufs[t].at[4 * tx + 2 * ty + tc]
                _remote(landed, landed, send_sems.at[t, k], recv_sems.at[t, k], (tx, ty, tc)).wait_recv()
        for cp in sends:
            cp.wait_send()
        for t in range(n + 1):
            total = bufs[t][0]
            for k in range(1, N_DEV):
                total = total + bufs[t][k]
            if t == n:
                loss_out[...] = total
                continue
            cols = w_in[t].shape[1]
            if cols == total.shape[1]:
                g_out[t][...] = total
                d_out[t][...], nm_out[t][...], nv_out[t][...] = _adamw_math(w_in[t][...], total, m_in[t][...], v_in[t][...])
            else:
                for j in range(N_CHIPS):
                    @pl.when(chip == j)
                    def _(t=t, j=j, cols=cols, total=total):
                        mine = total[:, j * cols:(j + 1) * cols]
                        g_out[t][...] = mine
                        d_out[t][...], nm_out[t][...], nv_out[t][...] = _adamw_math(w_in[t][...], mine, m_in[t][...], v_in[t][...])

    w_shapes = [_sds(a.shape, F32) for a in ws]
    return pl.pallas_call(
        body, name=name, in_specs=[_VMEM] * (4 * n + 1) + [_ANY] * len(after), out_specs=[_VMEM] * (4 * n + 1),
        out_shape=w_shapes * 4 + [_sds(loss_tile.shape, F32)],
        scratch_shapes=[pltpu.VMEM((N_DEV,) + a.shape, F32) for a in srcs]
        + [pltpu.SemaphoreType.DMA((n + 1, N_DEV - 1)), pltpu.SemaphoreType.DMA((n + 1, N_DEV - 1))],
        compiler_params=pltpu.CompilerParams(has_side_effects=True, vmem_limit_bytes=V7X_VMEM_LIMIT))(
            *srcs, *ws, *ms, *vs, *after)


_BIG = ("ffn1_w_gate_up", "ffn1_w_down", "ffn2_w_gate_up", "ffn2_w_down", "ssm_w_in", "ssm_w_out", "w_kv", "attn_w_q", "attn_w_o")
_TRANSPOSED = ("ffn1_w_gate_up", "ffn2_w_gate_up")
_FROM_HALVES = _TRANSPOSED + ("ffn1_w_down", "ffn2_w_down")
_SMALL = ("meta_tokens", "ffn1_norm", "mix_norm", "ffn2_norm", "ssm_lambda_re", "ssm_lambda_im", "ssm_b_re", "ssm_b_im",
          "ssm_c_re", "ssm_c_im", "ssm_log_step", "ssm_d", "kv_norm", "k_norm", "q_norm", "attn_sinks")
_ORDER = ("meta_tokens", "ffn1_norm", "ffn1_w_gate_up", "ffn1_w_down", "mix_norm", "ffn2_norm", "ffn2_w_gate_up", "ffn2_w_down",
          "ssm_w_in", "ssm_lambda_re", "ssm_lambda_im", "ssm_b_re", "ssm_b_im", "ssm_c_re", "ssm_c_im", "ssm_log_step", "ssm_d",
          "ssm_w_out", "kv_norm", "w_kv", "k_norm", "attn_w_q", "q_norm", "attn_sinks", "attn_w_o")


def _step(x, target, w, m, v):
    n_ex, seq, d = x.shape
    n_meta = w["meta_tokens"].shape[0]
    n_main = n_ex * seq
    n_all = n_main + n_ex * META_BLOCK
    n_g, n_p, n_c = w["ssm_b_re"].shape[1:]
    hd = w["k_norm"].shape[0]
    n_kv = w["w_kv"].shape[1] // (2 * hd)
    n_q = w["attn_w_q"].shape[2] // hd
    qpk = n_q // n_kv
    px, py, pc = _position()
    chip = 2 * px + py

    def cast(name, layer=0):
        a = w[name]
        return _cast_layer(f"cast_{name}_{layer}", a if a.ndim == 3 else a[None], layer)

    g_a = _all_gather_chips("gather_first", [cast("ffn1_w_gate_up"), w["meta_tokens"], w["ssm_d"]], [True, False, False],
                            collective_id=12)
    g_d = _all_gather_chips("gather_next", [cast("ffn1_w_down"), cast("ssm_w_in"), cast("ssm_w_out")], [True] * 3, collective_id=13)
    second = [cast("ffn2_w_gate_up"), cast("ffn2_w_down"), cast("w_kv")]
    g_b = _all_gather_chips("gather_second", second, [True] * 3, collective_id=1)
    third = [cast("ffn1_w_gate_up", 1), cast("ffn1_w_down", 1), cast("attn_w_q"), cast("attn_w_o"),
             cast("ffn2_w_gate_up", 1), cast("ffn2_w_down", 1)]
    g_c = _all_gather_chips("gather_third", third, [True] * 6, collective_id=2)
    wgu = {("ffn1", 0): g_a[0], ("ffn1", 1): g_c[0], ("ffn2", 0): g_b[0], ("ffn2", 1): g_c[4]}
    wd = {("ffn1", 0): g_d[0], ("ffn1", 1): g_c[1], ("ffn2", 0): g_b[1], ("ffn2", 1): g_c[5]}
    wd = {key: a.reshape(-1, d) for key, a in wd.items()}
    w_in = g_d[1].reshape(d, -1)
    wout4 = g_d[2]
    w_q = g_c[2].reshape(d, -1)
    w_o = g_c[3].reshape(-1, d)
    w_kv = g_b[2].reshape(d, -1)
    meta_full = jnp.transpose(g_a[1], (1, 0, 2)).reshape(n_meta, d)
    dskip = g_a[2].reshape(1, -1)

    row1 = lambda a: a.reshape(1, -1)
    ssm_args = tuple(w[k][0] for k in ("ssm_lambda_re", "ssm_lambda_im", "ssm_log_step", "ssm_b_re", "ssm_b_im", "ssm_c_re", "ssm_c_im"))
    (bb, cb, a_re, a_im), ssm_vjp = jax.vjp(_ssm_mats, *ssm_args)
    bb16, cb16 = bb.astype(BF16), cb.astype(BF16)
    a_re_s, a_im_s = lax.stop_gradient(a_re), lax.stop_gradient(a_im)
    half = n_g * n_p // 4
    tabs_f = _scan_tables(a_re_s, a_im_s, False)
    tabs_b = _scan_tables(a_re_s, a_im_s, True)

    freqs = ROPE_THETA ** (-jnp.arange(0, hd // 2, dtype=F32) * 2.0 / hd)
    pos_main = jnp.tile(n_meta + jnp.arange(seq), n_ex)
    pos_meta = jnp.tile(jnp.maximum(jnp.arange(META_BLOCK) - (META_BLOCK - n_meta), 0), n_ex)
    ang = jnp.concatenate([pos_main, pos_meta]).astype(F32)[:, None] * freqs[None, :]
    cos = jnp.concatenate([jnp.cos(ang), jnp.cos(ang)] * 2, axis=1)
    sin_s = jnp.concatenate([-jnp.sin(ang), jnp.sin(ang)] * 2, axis=1)
    k_gain_t = jnp.tile(row1(w["k_norm"]), (1, n_kv))
    score_scale = hd ** -0.5
    q_gain_t = jnp.tile(row1(w["q_norm"][0]), (1, n_q)) * score_scale

    meta_block = jnp.concatenate([jnp.zeros((META_BLOCK - n_meta, d), F32), meta_full], axis=0)
    h0 = jnp.concatenate([x.reshape(n_main, d)] + [meta_block] * n_ex, axis=0)

    g = lambda name, layer: row1(w[name][layer])
    h1, gu1 = _ffn_fwd("l0_ffn1", h0, g("ffn1_norm", 0), wgu["ffn1", 0], wd["ffn1", 0], n_all)
    u, bu = _ssm_in("ssm_in", h1, g("mix_norm", 0), w_in, bb16, n_all)
    xs = _scan_fwd("ssm_scan", bu, tabs_f, n_ex, seq)
    h2, y = _ssm_out("ssm_out", xs, u, dskip, cb16, wout4, h1, n_all)
    h3, gu2 = _ffn_fwd("l0_ffn2", h2, g("ffn2_norm", 0), wgu["ffn2", 0], wd["ffn2", 0], n_all)
    kvraw, k, vv = _kv_proj("kv_proj", h3, row1(w["kv_norm"]), w_kv, k_gain_t, cos, sin_s, n_all, n_kv, hd)
    h4, gu3 = _ffn_fwd("l1_ffn1", h3, g("ffn1_norm", 1), wgu["ffn1", 1], wd["ffn1", 1], n_main)
    qraw, q = _q_proj("q_proj", h4, g("mix_norm", 1), w_q, q_gain_t, cos, sin_s, n_main, n_q, hd)
    sinks = row1(w["attn_sinks"][0])
    o, lse = _attn_fwd("attn_fwd", q, k, vv, sinks, n_ex, seq, n_meta, n_kv, qpk, hd)
    h5 = _attn_out("attn_out", o, h4, w_o, n_main)
    (dh6, loss_tile), gu4 = _ffn_fwd("l1_ffn2", h5, g("ffn2_norm", 1), wgu["ffn2", 1], wd["ffn2", 1], n_main,
                                     target=target.reshape(n_main, d))

    lanes = 1024

    def rs_start(tag, entries, ids):
        pieces = [gr.reshape(N_CHIPS, 2, -1, lanes) for _, _, gr in entries]
        blob = jnp.transpose(jnp.concatenate(pieces, axis=2), (1, 0, 2, 3)).astype(BF16)
        return dict(tag=tag, entries=entries, ids=ids, blob=blob, theirs=_swap_halves_with_sibling(tag + "_swap", blob, ids[0]))

    def rs_scatter(st, after):
        chip_sum = _add_my_half(st["tag"] + "_chip_sum", st["blob"], st["theirs"], jnp.reshape(pc, (1,)).astype(jnp.int32), BF16, after)
        st["chip_sum"] = chip_sum
        st["landed"] = _scatter_to_chips(st["tag"] + "_scatter", chip_sum, st["ids"][1])

    def rs_finish(st, after):
        total = _sum_slots(st["tag"] + "_sum", st["landed"], after)
        st["total"] = total
        other = _share_with_sibling(st["tag"] + "_share", total, st["ids"][2])
        halves = (jnp.where(pc == 0, total, other), jnp.where(pc == 0, other, total))
        out, off = {}, 0
        for name, layer, gr in st["entries"]:
            rows = gr.shape[1] * gr.shape[2] // lanes // 2
            if name in _FROM_HALVES:
                out[name, layer] = (total, other, off)
            else:
                out[name, layer] = jnp.concatenate([hv[off:off + rows].reshape(-1) for hv in halves])
            off += rows
        return out

    small = {}
    dh5, dg_f2l1, dwgu_f2l1, dwd_f2l1 = _ffn_bwd("l1_ffn2", dh6, h5, g("ffn2_norm", 1), gu4, wgu["ffn2", 1], wd["ffn2", 1], n_main)
    do = _attn_out_bwd("attn_out_bwd", dh5, w_o, n_main)
    dw_o = _tn_plain("attn_dwo", o, dh5, 1, o.shape[1], d, n_main, out_dtype=BF16).reshape(N_CHIPS, -1, d)
    dq, dk_main, dv_main, dk_meta, dv_meta, dsinks = _attn_bwd("attn_bwd", q, k, vv, sinks, o, lse, do, n_ex, seq, n_meta, n_kv, qpk, hd)
    dqraw, dh4, dq_gain, dg_mix1 = _q_bwd("q_bwd", dq, qraw, q_gain_t, cos, sin_s, w_q, h4, g("mix_norm", 1), dh5, n_main, n_q, hd)
    dw_q = _tn_rms("attn_dwq", h4, g("mix_norm", 1), dqraw, n_main, out_dtype=BF16).reshape(N_CHIPS, -1, dqraw.shape[1])
    dh3m, dg_f1l1, dwgu_f1l1, dwd_f1l1 = _ffn_bwd("l1_ffn1", dh4, h3, g("ffn1_norm", 1), gu3, wgu["ffn1", 1], wd["ffn1", 1], n_main)
    rs1 = rs_start("rs1", [("ffn2_w_gate_up", 1, dwgu_f2l1), ("ffn1_w_gate_up", 1, dwgu_f1l1), ("ffn2_w_down", 1, dwd_f2l1),
                           ("ffn1_w_down", 1, dwd_f1l1), ("attn_w_o", 0, dw_o), ("attn_w_q", 0, dw_q)], (3, 4, 5))

    def with_meta(main, meta):
        blocks = [jnp.pad(meta[b * n_meta:(b + 1) * n_meta], ((META_BLOCK - n_meta, 0), (0, 0))) for b in range(n_ex)]
        return jnp.concatenate([main] + blocks, axis=0)

    dkvraw, dh3, dk_gain, dg_kv = _kv_bwd("kv_bwd", with_meta(dk_main, dk_meta), with_meta(dv_main, dv_meta), kvraw, k_gain_t,
                                          cos, sin_s, w_kv, h3, row1(w["kv_norm"]), dh3m, n_all, n_main, n_kv, hd,
                                          after=(rs1["blob"],))
    rs_scatter(rs1, after=(dh3,))
    dw_kv = _tn_rms("kv_dw", h3, row1(w["kv_norm"]), dkvraw, n_all, out_dtype=BF16).reshape(N_CHIPS, -1, dkvraw.shape[1])
    dh2, dg_f2l0, dwgu_f2l0, dwd_f2l0 = _ffn_bwd("l0_ffn2", dh3, h2, g("ffn2_norm", 0), gu2, wgu["ffn2", 0], wd["ffn2", 0], n_all,
                                                 after=(rs1["chip_sum"],))
    reduced = rs_finish(rs1, after=(dh2, dwgu_f2l0, dwd_f2l0, dw_kv))
    rs0a = rs_start("rs0a", [("ffn2_w_gate_up", 0, dwgu_f2l0), ("ffn2_w_down", 0, dwd_f2l0), ("w_kv", 0, dw_kv)], (6, 7, 8))

    dy, dz, gx, dd = _ssm_out_bwd("ssm_out_bwd", dh2, y, u, cb16, wout4, n_all, after=(rs1["total"], rs0a["blob"]))
    rs_scatter(rs0a, after=(dy,))
    hw = y.shape[1]
    oc = wout4.shape[2]
    dw_out = _tn_plain("ssm_dwout", y, dz, wout4.shape[0], hw, oc, n_all, a_fn=_gelu, out_dtype=BF16)
    gbu, da = _scan_bwd("ssm_scan_bwd", gx, xs, tabs_b, n_ex, seq, after=(rs0a["chip_sum"],))
    du, dh1, dg_mix0 = _ssm_in_bwd("ssm_in_bwd", gbu, dy, dskip, bb16, w_in, h1, g("mix_norm", 0), dh2, n_all)
    reduced.update(rs_finish(rs0a, after=(dh1,)))
    dw_in = _tn_rms("ssm_dwin", h1, g("mix_norm", 0), du, n_all, out_dtype=BF16).reshape(N_CHIPS, -1, hw)
    (dh0, dh0_meta), dg_f1l0, dwgu_f1l0, dwd_f1l0 = _ffn_bwd("l0_ffn1", dh1, h0, g("ffn1_norm", 0), gu1, wgu["ffn1", 0], wd["ffn1", 0],
                                                             n_all, n_main, after=(rs0a["total"],))
    rs0b = rs_start("rs0b", [("ffn1_w_gate_up", 0, dwgu_f1l0), ("ffn1_w_down", 0, dwd_f1l0), ("ssm_w_out", 0, dw_out),
                             ("ssm_w_in", 0, dw_in)], (9, 10, 11))
    dcb = _tn_plain("ssm_dcb", xs, dy, 4, xs.shape[1] // 4, hw // 4, n_all, after=(rs0b["blob"],))
    rs_scatter(rs0b, after=(dcb,))
    dbb = _tn_plain("ssm_dbb", u, gbu, 4, hw // 4, gbu.shape[1] // 4, n_all, after=(rs0b["chip_sum"],))

    grad_x = dh0.reshape(n_ex, seq, d)
    da_sum = jnp.sum(da, axis=(0, 1)).reshape(4, 2, half)
    d_ssm = ssm_vjp((dbb, dcb, da_sum[:, 0].reshape(-1), da_sum[:, 1].reshape(-1)))
    for key, val in zip(("ssm_lambda_re", "ssm_lambda_im", "ssm_log_step", "ssm_b_re", "ssm_b_im", "ssm_c_re", "ssm_c_im"), d_ssm):
        small[key] = val[None]
    small["meta_tokens"] = sum(dh0_meta[META_BLOCK * (b + 1) - n_meta:META_BLOCK * (b + 1)] for b in range(n_ex))
    small["ffn1_norm"] = jnp.concatenate([dg_f1l0, dg_f1l1], axis=0)
    small["ffn2_norm"] = jnp.concatenate([dg_f2l0, dg_f2l1], axis=0)
    small["mix_norm"] = jnp.concatenate([dg_mix0, dg_mix1], axis=0)
    small["ssm_d"] = dd
    small["kv_norm"] = dg_kv.reshape(-1)
    small["k_norm"] = jnp.sum(dk_gain.reshape(n_kv, hd), axis=0)
    small["q_norm"] = jnp.sum(dq_gain.reshape(n_q, hd), axis=0, keepdims=True) * score_scale
    small["attn_sinks"] = dsinks

    def view(name, a):
        if name in ("ssm_b_re", "ssm_b_im"):
            return a.reshape(-1, 128)
        return a.reshape(1, -1) if a.ndim == 1 else a.reshape(-1, a.shape[-1])

    grads, deltas, new_m, new_v = {}, {}, {}, {}

    def adamw_matrix(name, after=()):
        shape = w[name].shape
        if name in _FROM_HALVES:
            grads[name], deltas[name], new_m[name], new_v[name] = _adamw_from_halves(
                "adamw_" + name, w[name], m[name], v[name], [reduced[name, 0], reduced[name, 1]],
                jnp.reshape(pc, (1,)).astype(jnp.int32), name in _TRANSPOSED, after=after)
            return new_v[name]
        layers = [reduced[name, layer] for layer in range(2) if (name, layer) in reduced]
        grads[name] = jnp.concatenate(layers).reshape(shape)
        two_d = lambda a: a.reshape(-1, shape[-1])
        dl, nm, nv = _adamw("adamw_" + name, two_d(w[name]), two_d(grads[name]), two_d(m[name]), two_d(v[name]), after=after)
        deltas[name], new_m[name], new_v[name] = dl.reshape(shape), nm.reshape(shape), nv.reshape(shape)
        return nv

    placed = (rs0b["chip_sum"],)
    for name in ("ffn2_w_down", "attn_w_o", "attn_w_q", "w_kv"):
        placed = (adamw_matrix(name, after=placed),)
    tail = _reduce_small_adamw("small_tail", [view(k, small[k]) for k in _SMALL], loss_tile,
                               *[[view(k, t[k]) for k in _SMALL] for t in (w, m, v)], after=placed)
    n_small = len(_SMALL)
    for i, k in enumerate(_SMALL):
        grads[k], deltas[k] = tail[i].reshape(w[k].shape), tail[n_small + i].reshape(w[k].shape)
        new_m[k], new_v[k] = tail[2 * n_small + i].reshape(w[k].shape), tail[3 * n_small + i].reshape(w[k].shape)
    loss = jnp.sum(tail[-1])
    reduced.update(rs_finish(rs0b, after=(tail[-1],)))
    adamw_matrix("ffn2_w_gate_up", after=(rs0b["total"],))
    for name in ("ffn1_w_gate_up", "ffn1_w_down", "ssm_w_in", "ssm_w_out"):
        adamw_matrix(name)
    return (loss, grad_x, *[grads[k] for k in _ORDER], *[deltas[k] for k in _ORDER], *[new_m[k] for k in _ORDER],
            *[new_v[k] for k in _ORDER])


def kernel(x, meta_tokens, ffn1_norm, ffn1_w_gate_up, ffn1_w_down, mix_norm, ffn2_norm, ffn2_w_gate_up, ffn2_w_down, ssm_w_in, ssm_lambda_re, ssm_lambda_im, ssm_b_re, ssm_b_im, ssm_c_re, ssm_c_im, ssm_log_step, ssm_d, ssm_w_out, kv_norm, w_kv, k_norm, attn_w_q, q_norm, attn_sinks, attn_w_o, loss_target, m_meta_tokens, m_ffn1_norm, m_ffn1_w_gate_up, m_ffn1_w_down, m_mix_norm, m_ffn2_norm, m_ffn2_w_gate_up, m_ffn2_w_down, m_ssm_w_in, m_ssm_lambda_re, m_ssm_lambda_im, m_ssm_b_re, m_ssm_b_im, m_ssm_c_re, m_ssm_c_im, m_ssm_log_step, m_ssm_d, m_ssm_w_out, m_kv_norm, m_w_kv, m_k_norm, m_attn_w_q, m_q_norm, m_attn_sinks, m_attn_w_o, v_meta_tokens, v_ffn1_norm, v_ffn1_w_gate_up, v_ffn1_w_down, v_mix_norm, v_ffn2_norm, v_ffn2_w_gate_up, v_ffn2_w_down, v_ssm_w_in, v_ssm_lambda_re, v_ssm_lambda_im, v_ssm_b_re, v_ssm_b_im, v_ssm_c_re, v_ssm_c_im, v_ssm_log_step, v_ssm_d, v_ssm_w_out, v_kv_norm, v_w_kv, v_k_norm, v_attn_w_q, v_q_norm, v_attn_sinks, v_attn_w_o):
    args = locals()
    w = {k: args[k] for k in _ORDER}
    m = {k: args["m_" + k] for k in _ORDER}
    v = {k: args["v_" + k] for k in _ORDER}
    return _step(x, loss_target, w, m, v)
```

```python
import functools
import math

import jax
import jax.numpy as jnp
from jax import lax
from jax.experimental import pallas as pl
from jax.experimental.pallas import tpu as pltpu
from jax.experimental.pallas import tpu_sc as plsc

F32 = jnp.float32
BF16 = jnp.bfloat16
MESH = pl.DeviceIdType.MESH

EPS = 1e-6
NEG_INF = -1e30
ROPE_THETA = 10000.0
WINDOW = 128
META_BLOCK = 128
ROW_TILE = 256
SUBLANES = 8
V7X_VMEM_LIMIT = 56 * 2**20
N_CHIPS = 4
N_DEV = 8

ADAM_LR = 0.001
ADAM_B1 = 0.9
ADAM_B2 = 0.999
ADAM_EPS = 1e-08
ADAM_WD = 0.01
ADAM_STEP = 10

_HBM = pl.BlockSpec(memory_space=pltpu.HBM)
_VMEM = pl.BlockSpec(memory_space=pltpu.VMEM)


def _call(name, body, grid, in_specs, out_specs, out_shape, scratch=(), after=()):
    after = tuple(after)
    n_in = len(in_specs)

    def wrapped(*refs):
        return body(*refs[:n_in], *refs[n_in + len(after):])

    call = pl.pallas_call(
        wrapped, name=name, grid=grid, in_specs=list(in_specs) + [pl.BlockSpec(memory_space=pl.ANY)] * len(after),
        out_specs=out_specs, out_shape=out_shape, scratch_shapes=list(scratch),
        compiler_params=pltpu.CompilerParams(dimension_semantics=("arbitrary",) * len(grid),
                                             vmem_limit_bytes=V7X_VMEM_LIMIT))
    return lambda *operands: call(*operands, *after)


def _sds(shape, dtype):
    return jax.ShapeDtypeStruct(tuple(shape), dtype)


def _dot(a, b):
    return jnp.dot(a.astype(BF16), b.astype(BF16), preferred_element_type=F32)


def _dot_nt(a, b):
    return lax.dot_general(a.astype(BF16), b.astype(BF16), (((1,), (1,)), ((), ())), preferred_element_type=F32)


def _dot_tn(a, b):
    return lax.dot_general(a.astype(BF16), b.astype(BF16), (((0,), (0,)), ((), ())), preferred_element_type=F32)


def _rms(h, g):
    return h * lax.rsqrt(jnp.mean(h * h, axis=-1, keepdims=True) + EPS) * g


def _rms_bwd(h, g, dn):
    r = lax.rsqrt(jnp.mean(h * h, axis=-1, keepdims=True) + EPS)
    xh = h * r
    dxh = dn * g
    dg = jnp.sum(dn * xh, axis=0, keepdims=True)
    dh = r * (dxh - xh * jnp.mean(dxh * xh, axis=-1, keepdims=True))
    return dh, dg


def _sigmoid(x):
    return 0.5 * jnp.tanh(0.5 * x) + 0.5


def _gelu(y):
    k = math.sqrt(2.0 / math.pi)
    return 0.5 * y * (1.0 + jnp.tanh(k * (y + 0.044715 * y * y * y)))


def _gelu_grad(y):
    k = math.sqrt(2.0 / math.pi)
    t = jnp.tanh(k * (y + 0.044715 * y * y * y))
    return 0.5 * (1.0 + t) + 0.5 * y * (1.0 - t * t) * k * (1.0 + 3.0 * 0.044715 * y * y)


def _partner(x, lane, d):
    width = x.shape[-1]
    return jnp.where((lane & d) == 0, pltpu.roll(x, width - d, 1), pltpu.roll(x, d, 1))


def _split_bf16(x):
    hi = x.astype(BF16)
    return hi, (x - hi.astype(F32)).astype(BF16)


def _head_sums(x, sel):
    hi, lo = _split_bf16(x)
    return jnp.dot(hi, sel, preferred_element_type=F32) + jnp.dot(lo, sel, preferred_element_type=F32)


def _head_expand(v, sel_t):
    hi, lo = _split_bf16(v)
    return jnp.dot(hi, sel_t, preferred_element_type=F32) + jnp.dot(lo, sel_t, preferred_element_type=F32)


def _tile_lanes(t, width):
    return jnp.concatenate([t] * (width // t.shape[-1]), axis=1)


def _head_prep(x, gain_t, cos2, sin2, sel, sel_t, hd):
    width = x.shape[-1]
    lane = lax.broadcasted_iota(jnp.int32, x.shape, 1)
    r = _head_expand(lax.rsqrt(_head_sums(x * x, sel) * (1.0 / hd) + EPS), sel_t)
    y = x * r * gain_t
    return y * _tile_lanes(cos2, width) + _partner(y, lane, hd // 2) * _tile_lanes(sin2, width)


def _head_prep_bwd(x, gain_t, cos2, sin2, sel, sel_t, d_out, hd):
    width = x.shape[-1]
    lane = lax.broadcasted_iota(jnp.int32, x.shape, 1)
    r = _head_expand(lax.rsqrt(_head_sums(x * x, sel) * (1.0 / hd) + EPS), sel_t)
    xhat = x * r
    dy = d_out * _tile_lanes(cos2, width) + _partner(d_out * _tile_lanes(sin2, width), lane, hd // 2)
    dgain = jnp.sum(dy * xhat, axis=0, keepdims=True)
    dxh = dy * gain_t
    mean = _head_expand(_head_sums(dxh * xhat, sel) * (1.0 / hd), sel_t)
    return r * (dxh - xhat * mean), dgain


def _head_selectors(n_heads, hd):
    sel = (jnp.arange(n_heads * hd)[:, None] // hd == jnp.arange(128)[None, :]).astype(BF16)
    return sel, sel.T


def _acc_out(ref, val, first):
    @pl.when(first)
    def _():
        ref[...] = jnp.zeros_like(ref)
    ref[...] += val


def _wide_row_tile(n_rows, cap=512):
    best = 128
    for t in range(128, cap + 1, 128):
        if n_rows % t == 0:
            best = t
    return best


def _ffn_up(name, h, g, w4, n_rows):
    nj, d, fc = w4.shape
    tm = _wide_row_tile(n_rows)

    def body(h_ref, g_ref, w_ref, o_ref, n_ref):
        n = _rms(h_ref[...], g_ref[...]).astype(BF16)
        n_ref[...] = n
        for j in range(nj):
            o_ref[:, j * fc:(j + 1) * fc] = _dot(n, w_ref[j]).astype(BF16)

    return _call(name, body, (n_rows // tm,),
                 [pl.BlockSpec((tm, d), lambda i: (i, 0)), pl.BlockSpec((1, d), lambda i: (0, 0)),
                  pl.BlockSpec((nj, d, fc), lambda i: (0, 0, 0))],
                 [pl.BlockSpec((tm, nj * fc), lambda i: (i, 0)), pl.BlockSpec((tm, d), lambda i: (i, 0))],
                 [_sds((n_rows, nj * fc), BF16), _sds((n_rows, d), BF16)])(h, g, w4)


def _ffn_down(name, gu, h, wd, n_rows, target=None):
    f, d = wd.shape
    tm = ROW_TILE

    def body(gu_ref, h_ref, w_ref, *rest):
        half_a = gu_ref[:, :f] * 0.5
        s = (half_a + half_a * jnp.tanh(half_a)) * gu_ref[:, f:]
        y = h_ref[...] + 0.5 * _dot(s, w_ref[...])
        if target is None:
            o_ref, s_ref = rest
            o_ref[...] = y
        else:
            t_ref, dy_ref, l_ref, s_ref = rest
            e = y - t_ref[...]
            dy_ref[...] = e * (1.0 / d)
            e2 = jnp.sum((e * e).reshape(tm // SUBLANES, SUBLANES, d), axis=0)
            part = e2[:, 0:128]
            for k in range(1, d // 128):
                part = part + e2[:, k * 128:(k + 1) * 128]
            _acc_out(l_ref, part * (0.5 / d), pl.program_id(0) == 0)
        s_ref[...] = s

    row = lambda width: pl.BlockSpec((tm, width), lambda i: (i, 0))
    in_specs = [row(2 * f), row(d), pl.BlockSpec((f, d), lambda i: (0, 0))]
    if target is None:
        return _call(name, body, (n_rows // tm,), in_specs, [row(d), row(f)],
                     [_sds((n_rows, d), F32), _sds((n_rows, f), BF16)])(gu, h, wd)
    return _call(name, body, (n_rows // tm,), in_specs + [row(d)],
                 [row(d), pl.BlockSpec((SUBLANES, 128), lambda i: (0, 0)), row(f)],
                 [_sds((n_rows, d), F32), _sds((SUBLANES, 128), F32), _sds((n_rows, f), BF16)])(gu, h, wd, target)


def _ffn_dgu(name, dh, gu, wd, n_rows, after=()):
    f, d = wd.shape
    tm = ROW_TILE

    def body(dh_ref, gu_ref, w_ref, o_ref):
        ds = _dot_nt(0.5 * dh_ref[...], w_ref[...]).astype(BF16)
        half_a = gu_ref[:, :f] * 0.5
        t = jnp.tanh(half_a)
        o_ref[:, :f] = ds * gu_ref[:, f:] * ((1.0 + t + half_a * (1.0 - t * t)) * 0.5)
        o_ref[:, f:] = ds * (half_a + half_a * t)

    return _call(name, body, (n_rows // tm,),
                 [pl.BlockSpec((tm, d), lambda i: (i, 0)), pl.BlockSpec((tm, 2 * f), lambda i: (i, 0)),
                  pl.BlockSpec((f, d), lambda i: (0, 0))],
                 pl.BlockSpec((tm, 2 * f), lambda i: (i, 0)), _sds((n_rows, 2 * f), BF16), after=after)(dh, gu, wd)


def _ffn_dh(name, dgu, h, g, dh, w4, n_rows, n_main=None, after=()):
    nj, d, fc = w4.shape
    tm = _wide_row_tile(n_rows) if n_main is None else ROW_TILE
    n_first = (n_rows if n_main is None else n_main) // tm

    def body(dgu_ref, h_ref, g_ref, dh_ref, w_ref, o_ref, *rest):
        dg_ref = rest[-1]
        i = pl.program_id(0)
        dn = _dot_nt(dgu_ref[:, 0:fc], w_ref[0])
        for j in range(1, nj):
            dn = dn + _dot_nt(dgu_ref[:, j * fc:(j + 1) * fc], w_ref[j])
        dhn, dg = _rms_bwd(h_ref[...], g_ref[...], dn)
        val = dh_ref[...] + dhn
        if n_main is None:
            o_ref[...] = val
        else:
            @pl.when(i < n_first)
            def _():
                o_ref[...] = val

            @pl.when(i >= n_first)
            def _():
                rest[0][...] = val
        _acc_out(dg_ref, dg, i == 0)

    out_specs = [pl.BlockSpec((tm, d), lambda i: (jnp.minimum(i, n_first - 1), 0))]
    out_shape = [_sds((n_first * tm, d), F32)]
    if n_main is not None:
        out_specs.append(pl.BlockSpec((tm, d), lambda i: (jnp.maximum(i - n_first, 0), 0)))
        out_shape.append(_sds((n_rows - n_main, d), F32))
    return _call(name, body, (n_rows // tm,),
                 [pl.BlockSpec((tm, nj * fc), lambda i: (i, 0)), pl.BlockSpec((tm, d), lambda i: (i, 0)),
                  pl.BlockSpec((1, d), lambda i: (0, 0)), pl.BlockSpec((tm, d), lambda i: (i, 0)),
                  pl.BlockSpec((nj, d, fc), lambda i: (0, 0, 0), pipeline_mode=pl.Buffered(1))],
                 out_specs + [pl.BlockSpec((1, d), lambda i: (0, 0))],
                 out_shape + [_sds((1, d), F32)], after=after)(dgu, h, g, dh, w4)


def _contract_tile(n_rows, cap=2816):
    best = ROW_TILE
    for t in range(ROW_TILE, cap + 1, ROW_TILE):
        if n_rows % t == 0:
            best = t
    return best


def _tn(name, operands, in_specs, prologue, nj, ma, nb, n_rows, tk, out_dtype=F32, after=(), side_by_side=False):
    n_k = n_rows // tk
    out_spec = pl.BlockSpec((1, ma, nb), lambda j, k: (j, 0, 0))
    if out_dtype == F32 and not side_by_side:
        def body(*refs):
            o_ref = refs[-1]
            a, b = prologue(pl.program_id(0), *refs[:-1])
            _acc_out(o_ref, _dot_tn(a, b)[None], pl.program_id(1) == 0)

        return _call(name, body, (nj, n_k), in_specs, out_spec, _sds((nj, ma, nb), F32), after=after)(*operands)

    def body_rounded(*refs):
        o_ref, acc_ref = refs[-2:]
        a, b = prologue(pl.program_id(0), *refs[:-2])
        _acc_out(acc_ref, _dot_tn(a, b), pl.program_id(1) == 0)

        @pl.when(pl.program_id(1) == n_k - 1)
        def _():
            o_ref[...] = acc_ref[...].astype(out_dtype).reshape(o_ref.shape)

    if side_by_side:
        out_spec, out_shape = pl.BlockSpec((ma, nb), lambda j, k: (0, j)), _sds((ma, nj * nb), out_dtype)
    else:
        out_shape = _sds((nj, ma, nb), out_dtype)
    return _call(name, body_rounded, (nj, n_k), in_specs, out_spec, out_shape,
                 scratch=[pltpu.VMEM((ma, nb), F32)], after=after)(*operands)


def _ffn_dwgu(name, n, dgu, nj, n_rows):
    d = n.shape[1]
    fc = dgu.shape[1] // nj
    tk = _contract_tile(n_rows, cap=2816)
    return _tn(name, (dgu, n),
               [pl.BlockSpec((tk, fc), lambda j, k: (k, j)), pl.BlockSpec((tk, d), lambda j, k: (k, 0))],
               lambda j, a_ref, b_ref: (a_ref[...], b_ref[...]), nj, fc, d, n_rows, tk, out_dtype=BF16)


def _ffn_dwd(name, s, dh, n_rows):
    f = s.shape[1]
    d = dh.shape[1]
    tk = _contract_tile(n_rows, cap=2048)
    halves = 2 if tk > 1024 else 1
    return _tn(name, (s, dh),
               [pl.BlockSpec((tk, f), lambda j, k: (k, 0)), pl.BlockSpec((tk, d // halves), lambda j, k: (k, j))],
               lambda j, s_ref, dh_ref: (s_ref[...], 0.5 * dh_ref[...]), halves, f, d // halves, n_rows, tk, out_dtype=BF16,
               side_by_side=True)


def _ffn_fwd(tag, h, g, w4, wd, n_rows, target=None):
    gu, n = _ffn_up(tag + "_up", h, g, w4, n_rows)
    *out, s = _ffn_down(tag + "_down", gu, h, wd, n_rows, target)
    return (out[0] if target is None else tuple(out)), (gu, n, s)


def _ffn_bwd(tag, dh_out, h, g, saved, w4, wd, n_rows, n_main=None, after=()):
    gu, n, s = saved
    nj = w4.shape[0]
    f, d = wd.shape
    dgu = _ffn_dgu(tag + "_dgu", dh_out, gu, wd, n_rows, after=after)
    dwd = _ffn_dwd(tag + "_dwd", s, dh_out, n_rows).reshape(N_CHIPS, f // N_CHIPS, d)
    *dh_parts, dg = _ffn_dh(tag + "_dh", dgu, h, g, dh_out, w4, n_rows, n_main)
    dwgu = _ffn_dwgu(tag + "_dwgu", n, dgu, nj, n_rows)
    dh_in = dh_parts[0] if n_main is None else tuple(dh_parts)
    return dh_in, dg, dwgu, dwd


def _ssm_in(name, h, g, w_in, bb, n_rows):
    d, hw = w_in.shape
    nj, uc, xc = bb.shape
    tm = ROW_TILE

    def body(h_ref, g_ref, w_ref, bb_ref, u_ref, bu_ref):
        u = _dot(_rms(h_ref[...], g_ref[...]), w_ref[...])
        u_ref[...] = u
        for j in range(nj):
            bu_ref[:, j * xc:(j + 1) * xc] = _dot(u[:, j * uc:(j + 1) * uc], bb_ref[j]).astype(BF16)

    return _call(name, body, (n_rows // tm,),
                 [pl.BlockSpec((tm, d), lambda i: (i, 0)), pl.BlockSpec((1, d), lambda i: (0, 0)),
                  pl.BlockSpec((d, hw), lambda i: (0, 0)), pl.BlockSpec((nj, uc, xc), lambda i: (0, 0, 0))],
                 [pl.BlockSpec((tm, hw), lambda i: (i, 0)), pl.BlockSpec((tm, nj * xc), lambda i: (i, 0))],
                 [_sds((n_rows, hw), F32), _sds((n_rows, nj * xc), BF16)])(h, g, w_in, bb)


def _cmul_add(xr, xi, ar, ai, sr, si):
    return xr + ar * sr - ai * si, xi + ar * si + ai * sr


def _scan_row_block(n_main_blocks, seq_blocks):
    return lambda b, i: jnp.where(i == 0, n_main_blocks + b, b * seq_blocks + i - 1)


def _scan_fwd(name, bu, tabs, n_ex, seq):
    n_rows, width = bu.shape
    nj = 4
    cw = width // nj
    half = cw // 2
    tq = META_BLOCK
    seq_blocks = seq // tq
    rb = _scan_row_block(n_ex * seq_blocks, seq_blocks)

    def body(bu_ref, tab_ref, x_ref, carry_ref):
        @pl.when(pl.program_id(1) == 0)
        def _():
            carry_ref[...] = jnp.zeros_like(carry_ref)

        for j in range(nj):
            re, im = slice(j * cw, j * cw + half), slice(j * cw + half, (j + 1) * cw)
            ch = slice(j * half, (j + 1) * half)

            def blk(k, c, re=re, im=im, ch=ch):
                t = [tab_ref[n * SUBLANES:(n + 1) * SUBLANES, ch] for n in range(8)]
                r0 = pl.multiple_of(k * SUBLANES, SUBLANES)
                xr = bu_ref[pl.ds(r0, SUBLANES), re].astype(F32)
                xi = bu_ref[pl.ds(r0, SUBLANES), im].astype(F32)
                for s, d in enumerate((1, 2, 4)):
                    xr, xi = _cmul_add(xr, xi, t[2 * s], t[2 * s + 1], pltpu.roll(xr, d, 0), pltpu.roll(xi, d, 0))
                xr, xi = _cmul_add(xr, xi, t[6], t[7], c[0], c[1])
                x_ref[pl.ds(r0, SUBLANES), re] = xr.astype(BF16)
                x_ref[pl.ds(r0, SUBLANES), im] = xi.astype(BF16)
                last = SUBLANES - 1
                return (jnp.broadcast_to(xr[last:last + 1, :], xr.shape), jnp.broadcast_to(xi[last:last + 1, :], xi.shape))

            c = lax.fori_loop(0, tq // SUBLANES, blk, (carry_ref[0, :, ch], carry_ref[1, :, ch]), unroll=2)
            carry_ref[0, :, ch] = c[0]
            carry_ref[1, :, ch] = c[1]

    return _call(name, body, (n_ex, seq_blocks + 1),
                 [pl.BlockSpec((tq, width), lambda b, i: (rb(b, i), 0)), pl.BlockSpec((8 * SUBLANES, nj * half), lambda b, i: (0, 0))],
                 pl.BlockSpec((tq, width), lambda b, i: (rb(b, i), 0)), _sds((n_rows, width), BF16),
                 scratch=[pltpu.VMEM((2, SUBLANES, nj * half), F32)])(bu, tabs)


def _scan_bwd(name, gx, x, tabs, n_ex, seq, after=()):
    n_rows, width = gx.shape
    nj = 4
    cw = width // nj
    half = cw // 2
    tq = META_BLOCK
    seq_blocks = seq // tq
    n_steps = seq_blocks + 1
    rb = _scan_row_block(n_ex * seq_blocks, seq_blocks)
    rbr = lambda b, i: rb(b, n_steps - 1 - i)

    def body(gx_ref, x_ref, tab_ref, g_ref, da_ref, carry_ref):
        @pl.when(pl.program_id(1) == 0)
        def _():
            carry_ref[...] = jnp.zeros_like(carry_ref)
            da_ref[...] = jnp.zeros_like(da_ref)
        row = lax.broadcasted_iota(jnp.int32, (SUBLANES, half), 0)
        n_blk = tq // SUBLANES

        for j in range(nj):
            re, im = slice(j * cw, j * cw + half), slice(j * cw + half, (j + 1) * cw)
            ch = slice(j * half, (j + 1) * half)

            def blk(kk, st, re=re, im=im, ch=ch):
                t = [tab_ref[n * SUBLANES:(n + 1) * SUBLANES, ch] for n in range(8)]
                cr, ci, dar, dai = st
                r0 = pl.multiple_of((n_blk - 1 - kk) * SUBLANES, SUBLANES)
                gr = gx_ref[pl.ds(r0, SUBLANES), re].astype(F32)
                gi = gx_ref[pl.ds(r0, SUBLANES), im].astype(F32)
                for s, d in enumerate((1, 2, 4)):
                    gr, gi = _cmul_add(gr, gi, t[2 * s], t[2 * s + 1],
                                       pltpu.roll(gr, SUBLANES - d, 0), pltpu.roll(gi, SUBLANES - d, 0))
                gr, gi = _cmul_add(gr, gi, t[6], t[7], cr, ci)
                g_ref[pl.ds(r0, SUBLANES), re] = gr.astype(BF16)
                g_ref[pl.ds(r0, SUBLANES), im] = gi.astype(BF16)
                hr = jnp.where(row == SUBLANES - 1, cr, pltpu.roll(gr, SUBLANES - 1, 0))
                hi = jnp.where(row == SUBLANES - 1, ci, pltpu.roll(gi, SUBLANES - 1, 0))
                xr = x_ref[pl.ds(r0, SUBLANES), re].astype(F32)
                xi = x_ref[pl.ds(r0, SUBLANES), im].astype(F32)
                dar = dar + xr * hr + xi * hi
                dai = dai + xr * hi - xi * hr
                return (jnp.broadcast_to(gr[0:1, :], gr.shape), jnp.broadcast_to(gi[0:1, :], gi.shape), dar, dai)

            st = lax.fori_loop(0, n_blk, blk, (carry_ref[0, :, ch], carry_ref[1, :, ch], da_ref[0, :, re], da_ref[0, :, im]),
                               unroll=2)
            carry_ref[0, :, ch] = st[0]
            carry_ref[1, :, ch] = st[1]
            da_ref[0, :, re] = st[2]
            da_ref[0, :, im] = st[3]

    return _call(name, body, (n_ex, n_steps),
                 [pl.BlockSpec((tq, width), lambda b, i: (rbr(b, i), 0)), pl.BlockSpec((tq, width), lambda b, i: (rbr(b, i), 0)),
                  pl.BlockSpec((8 * SUBLANES, nj * half), lambda b, i: (0, 0))],
                 [pl.BlockSpec((tq, width), lambda b, i: (rbr(b, i), 0)), pl.BlockSpec((1, SUBLANES, width), lambda b, i: (b, 0, 0))],
                 [_sds((n_rows, width), BF16), _sds((n_ex, SUBLANES, width), F32)],
                 scratch=[pltpu.VMEM((2, SUBLANES, nj * half), F32)], after=after)(gx, x, tabs)


def _ssm_z(gy, wout_ref, nj):
    return jnp.concatenate([_dot(gy, wout_ref[j]) for j in range(nj)], axis=1)


def _ssm_out(name, x, u, dskip, cb, wout4, h, n_rows):
    nj, xc, uc = cb.shape
    no, hw, oc = wout4.shape
    d = h.shape[1]
    tm = ROW_TILE

    def body(x_ref, u_ref, ds_ref, cb_ref, w_ref, h_ref, o_ref, y_ref):
        y = jnp.concatenate([_dot(x_ref[:, j * xc:(j + 1) * xc], cb_ref[j]) for j in range(nj)], axis=1)
        y = y + ds_ref[...] * u_ref[...]
        y_ref[...] = y
        z = _ssm_z(_gelu(y), w_ref, no)
        o_ref[...] = h_ref[...] + z[:, :d] * _sigmoid(z[:, d:])

    return _call(name, body, (n_rows // tm,),
                 [pl.BlockSpec((tm, nj * xc), lambda i: (i, 0)), pl.BlockSpec((tm, hw), lambda i: (i, 0)),
                  pl.BlockSpec((1, hw), lambda i: (0, 0)), pl.BlockSpec((nj, xc, uc), lambda i: (0, 0, 0)),
                  pl.BlockSpec((no, hw, oc), lambda i: (0, 0, 0)), pl.BlockSpec((tm, d), lambda i: (i, 0))],
                 [pl.BlockSpec((tm, d), lambda i: (i, 0)), pl.BlockSpec((tm, hw), lambda i: (i, 0))],
                 [_sds((n_rows, d), F32), _sds((n_rows, hw), F32)])(x, u, dskip, cb, wout4, h)


def _ssm_out_bwd(name, dh, y, u, cb, wout4, n_rows, after=()):
    nj, xc, uc = cb.shape
    no, hw, oc = wout4.shape
    d = dh.shape[1]
    tm = ROW_TILE

    def body(dh_ref, y_ref, u_ref, cb_ref, w_ref, dy_ref, dz_ref, gx_ref, dd_ref):
        y = y_ref[...]
        z = _ssm_z(_gelu(y), w_ref, no)
        za = z[:, :d]
        sg = _sigmoid(z[:, d:])
        dmix = dh_ref[...]
        dz = jnp.concatenate([dmix * sg, dmix * za * sg * (1.0 - sg)], axis=1).astype(BF16)
        dz_ref[...] = dz
        dgy = _dot_nt(dz[:, 0:oc], w_ref[0])
        for j in range(1, no):
            dgy = dgy + _dot_nt(dz[:, j * oc:(j + 1) * oc], w_ref[j])
        dy = dgy * _gelu_grad(y)
        dy_ref[...] = dy
        _acc_out(dd_ref, jnp.sum(dy * u_ref[...], axis=0, keepdims=True), pl.program_id(0) == 0)
        for j in range(nj):
            gx_ref[:, j * xc:(j + 1) * xc] = _dot_nt(dy[:, j * uc:(j + 1) * uc], cb_ref[j]).astype(BF16)

    return _call(name, body, (n_rows // tm,),
                 [pl.BlockSpec((tm, d), lambda i: (i, 0)), pl.BlockSpec((tm, hw), lambda i: (i, 0)),
                  pl.BlockSpec((tm, hw), lambda i: (i, 0)), pl.BlockSpec((nj, xc, uc), lambda i: (0, 0, 0)),
                  pl.BlockSpec((no, hw, oc), lambda i: (0, 0, 0))],
                 [pl.BlockSpec((tm, hw), lambda i: (i, 0)), pl.BlockSpec((tm, no * oc), lambda i: (i, 0)),
                  pl.BlockSpec((tm, nj * xc), lambda i: (i, 0)), pl.BlockSpec((1, hw), lambda i: (0, 0))],
                 [_sds((n_rows, hw), F32), _sds((n_rows, no * oc), BF16), _sds((n_rows, nj * xc), BF16),
                  _sds((1, hw), F32)], after=after)(dh, y, u, cb, wout4)


def _ssm_in_bwd(name, gbu, dy, dskip, bb, w_in, h, g, dh, n_rows):
    nj, uc, xc = bb.shape
    d, hw = w_in.shape
    tm = ROW_TILE

    def body(gb_ref, dy_ref, ds_ref, bb_ref, w_ref, h_ref, g_ref, dh_ref, du_ref, o_ref, dg_ref):
        du = jnp.concatenate([_dot_nt(gb_ref[:, j * xc:(j + 1) * xc], bb_ref[j]) for j in range(nj)], axis=1)
        du = du + dy_ref[...] * ds_ref[...]
        du_ref[...] = du.astype(BF16)
        dhn, dg = _rms_bwd(h_ref[...], g_ref[...], _dot_nt(du, w_ref[...]))
        o_ref[...] = dh_ref[...] + dhn
        _acc_out(dg_ref, dg, pl.program_id(0) == 0)

    return _call(name, body, (n_rows // tm,),
                 [pl.BlockSpec((tm, nj * xc), lambda i: (i, 0)), pl.BlockSpec((tm, hw), lambda i: (i, 0)),
                  pl.BlockSpec((1, hw), lambda i: (0, 0)), pl.BlockSpec((nj, uc, xc), lambda i: (0, 0, 0)),
                  pl.BlockSpec((d, hw), lambda i: (0, 0)), pl.BlockSpec((tm, d), lambda i: (i, 0)),
                  pl.BlockSpec((1, d), lambda i: (0, 0)), pl.BlockSpec((tm, d), lambda i: (i, 0))],
                 [pl.BlockSpec((tm, hw), lambda i: (i, 0)), pl.BlockSpec((tm, d), lambda i: (i, 0)),
                  pl.BlockSpec((1, d), lambda i: (0, 0))],
                 [_sds((n_rows, hw), BF16), _sds((n_rows, d), F32), _sds((1, d), F32)])(gbu, dy, dskip, bb, w_in, h, g, dh)


def _discretize(lam_re, lam_im, log_step, b_re, b_im):
    step = jnp.exp(log_step)[:, None]
    mag = jnp.exp(lam_re * step)
    ar = mag * jnp.cos(lam_im * step)
    ai = mag * jnp.sin(lam_im * step)
    den = lam_re * lam_re + lam_im * lam_im
    nr, ni = ar - 1.0, ai
    cr = (nr * lam_re + ni * lam_im) / den
    ci = (ni * lam_re - nr * lam_im) / den
    bbar_r = cr[..., None] * b_re - ci[..., None] * b_im
    bbar_i = cr[..., None] * b_im + ci[..., None] * b_re
    return ar, ai, bbar_r, bbar_i


def _ssm_mats(lam_re, lam_im, log_step, b_re, b_im, c_re, c_im):
    n_g, n_p, n_c = b_re.shape
    gpc = n_g // 4
    ar, ai, bbar_r, bbar_i = _discretize(lam_re, lam_im, log_step, b_re, b_im)
    eye = jnp.eye(gpc, dtype=F32)

    def in_map(bbar):
        return jnp.einsum('jgpc,gh->jgchp', bbar.reshape(4, gpc, n_p, n_c), eye).reshape(4, gpc * n_c, gpc * n_p)

    def out_map(c):
        return jnp.einsum('jgcp,gh->jgphc', c.reshape(4, gpc, n_c, n_p), eye).reshape(4, gpc * n_p, gpc * n_c)

    bb = jnp.concatenate([in_map(bbar_r), in_map(bbar_i)], axis=2)
    cb = jnp.concatenate([out_map(c_re), -out_map(c_im)], axis=1)
    return bb, cb, ar.reshape(-1), ai.reshape(-1)


def _chunked(v, half):
    return v.reshape(v.shape[:-1] + (4, half))


def _scan_tables(ar, ai, reverse):
    if reverse:
        ai = -ai
    pr, pi = [ar], [ai]
    for _ in range(SUBLANES - 1):
        pr, pi = pr + [pr[-1] * ar - pi[-1] * ai], pi + [pr[-1] * ai + pi[-1] * ar]
    row = jnp.arange(SUBLANES)[:, None]
    tabs = []
    for d in (1, 2, 4):
        keep = (row <= SUBLANES - 1 - d) if reverse else (row >= d)
        tabs += [jnp.where(keep, pr[d - 1][None, :], 0.0), jnp.where(keep, pi[d - 1][None, :], 0.0)]
    order = list(range(SUBLANES))[::-1] if reverse else list(range(SUBLANES))
    tabs += [jnp.stack([pr[k] for k in order]), jnp.stack([pi[k] for k in order])]
    return jnp.concatenate(tabs, axis=0)


def _kv_proj(name, h, g, w_kv, k_gain_t, cos2, sin2, n_rows, n_kv, hd):
    d, kvw = w_kv.shape
    kw = n_kv * hd
    tm = ROW_TILE

    sel, sel_t = _head_selectors(n_kv, hd)

    def body(h_ref, g_ref, w_ref, kg_ref, c_ref, s_ref, e_ref, et_ref, raw_ref, k_ref, v_ref):
        raw = _dot(_rms(h_ref[...], g_ref[...]), w_ref[...])
        raw_ref[...] = raw
        k_ref[...] = _head_prep(raw[:, :kw], kg_ref[...], c_ref[...], s_ref[...], e_ref[...], et_ref[...], hd).astype(BF16)
        v_ref[...] = raw[:, kw:].astype(BF16)

    return _call(name, body, (n_rows // tm,),
                 [pl.BlockSpec((tm, d), lambda i: (i, 0)), pl.BlockSpec((1, d), lambda i: (0, 0)),
                  pl.BlockSpec((d, kvw), lambda i: (0, 0)), pl.BlockSpec((1, kw), lambda i: (0, 0)),
                  pl.BlockSpec((tm, 2 * hd), lambda i: (i, 0)), pl.BlockSpec((tm, 2 * hd), lambda i: (i, 0)),
                  pl.BlockSpec(sel.shape, lambda i: (0, 0)), pl.BlockSpec(sel_t.shape, lambda i: (0, 0))],
                 [pl.BlockSpec((tm, kvw), lambda i: (i, 0)), pl.BlockSpec((tm, kw), lambda i: (i, 0)),
                  pl.BlockSpec((tm, kw), lambda i: (i, 0))],
                 [_sds((n_rows, kvw), F32), _sds((n_rows, kw), BF16), _sds((n_rows, kw), BF16)])(
                     h, g, w_kv, k_gain_t, cos2, sin2, sel, sel_t)


def _q_proj(name, h, g, w_q, q_gain_t, cos2, sin2, n_rows, n_q, hd):
    d, qw = w_q.shape
    tm = ROW_TILE

    sel, sel_t = _head_selectors(n_q, hd)

    def body(h_ref, g_ref, w_ref, qg_ref, c_ref, s_ref, e_ref, et_ref, raw_ref, q_ref):
        raw = _dot(_rms(h_ref[...], g_ref[...]), w_ref[...])
        raw_ref[...] = raw
        q_ref[...] = _head_prep(raw, qg_ref[...], c_ref[...], s_ref[...], e_ref[...], et_ref[...], hd).astype(BF16)

    return _call(name, body, (n_rows // tm,),
                 [pl.BlockSpec((tm, d), lambda i: (i, 0)), pl.BlockSpec((1, d), lambda i: (0, 0)),
                  pl.BlockSpec((d, qw), lambda i: (0, 0)), pl.BlockSpec((1, qw), lambda i: (0, 0)),
                  pl.BlockSpec((tm, 2 * hd), lambda i: (i, 0)), pl.BlockSpec((tm, 2 * hd), lambda i: (i, 0)),
                  pl.BlockSpec(sel.shape, lambda i: (0, 0)), pl.BlockSpec(sel_t.shape, lambda i: (0, 0))],
                 [pl.BlockSpec((tm, qw), lambda i: (i, 0)), pl.BlockSpec((tm, qw), lambda i: (i, 0))],
                 [_sds((n_rows, qw), F32), _sds((n_rows, qw), BF16)])(h, g, w_q, q_gain_t, cos2, sin2, sel, sel_t)


def _attn_specs(seq, n_ex, n_meta, kw):
    nb = seq // WINDOW
    meta_blk = lambda b: (n_ex * seq + META_BLOCK * b + META_BLOCK - n_meta) // n_meta
    return [pl.BlockSpec((WINDOW, kw), lambda b, n: (b * nb + jnp.maximum(n - 1, 0), 0)),
            pl.BlockSpec((WINDOW, kw), lambda b, n: (b * nb + n, 0)),
            pl.BlockSpec((n_meta, kw), lambda b, n: (meta_blk(b), 0))]


def _attn_bias(qpk, n_keys):
    rows = qpk * WINDOW
    qi = jnp.arange(rows)[:, None] & (WINDOW - 1)
    kj = jnp.arange(n_keys)[None, :]
    rel = qi + WINDOW - kj
    band = (rel >= 0) & (rel < WINDOW)
    meta = kj >= 2 * WINDOW
    first = (band & (kj >= WINDOW)) | meta
    return jnp.where(jnp.stack([first, band | meta]), 0.0, NEG_INF).astype(F32)


def _stack_heads(ref, h, qpk, hd, dtype=None):
    parts = [ref[:, (h * qpk + gq) * hd:(h * qpk + gq + 1) * hd] for gq in range(qpk)]
    out = jnp.concatenate(parts, axis=0)
    return out if dtype is None else out.astype(dtype)


def _col(tile, c):
    lane = lax.broadcasted_iota(jnp.int32, tile.shape, 1)
    return jnp.sum(jnp.where(lane == c, tile, 0.0), axis=-1, keepdims=True)


def _put_col(col, c, n):
    lane = lax.broadcasted_iota(jnp.int32, (col.shape[0], n), 1)
    return jnp.where(lane == c, col, 0.0)


def _stack_cols(tile, h, qpk):
    return jnp.concatenate([_col(tile, h * qpk + gq) for gq in range(qpk)], axis=0)


def _sink_col(sinks, h, qpk):
    return jnp.concatenate([jnp.broadcast_to(_col(sinks, h * qpk + gq), (WINDOW, 1)) for gq in range(qpk)], axis=0)


def _attn_fwd(name, q, k, v, sinks, n_ex, seq, n_meta, n_kv, qpk, hd):
    nb = seq // WINDOW
    n_q = n_kv * qpk
    kw = n_kv * hd
    qw = n_q * hd
    n_keys = 2 * WINDOW + n_meta
    bias = _attn_bias(qpk, n_keys)

    def body(q_ref, kp_ref, kc_ref, km_ref, vp_ref, vc_ref, vm_ref, sk_ref, bias_ref, o_ref, lse_ref):
        sinks_v = sk_ref[...]
        o_parts = []
        lse_all = jnp.zeros((WINDOW, n_q), F32)
        for h in range(n_kv):
            hs = slice(h * hd, (h + 1) * hd)
            kb = jnp.concatenate([kp_ref[:, hs], kc_ref[:, hs], km_ref[:, hs]], axis=0)
            vb = jnp.concatenate([vp_ref[:, hs], vc_ref[:, hs], vm_ref[:, hs]], axis=0)
            for gq in range(qpk):
                c = h * qpk + gq
                s = _dot_nt(q_ref[:, c * hd:(c + 1) * hd], kb) + bias_ref[0, 0:WINDOW, :]
                skc = _col(sinks_v, c)
                m = jnp.maximum(jnp.max(s, axis=-1, keepdims=True), skc)
                p = jnp.exp(s - m)
                den = jnp.sum(p, axis=-1, keepdims=True) + jnp.exp(skc - m)
                o_parts.append(_dot(p, vb) / den)
                lse_all = lse_all + _put_col(m + jnp.log(den), c, n_q)
        o_ref[...] = jnp.concatenate(o_parts, axis=1).astype(BF16)
        lse_ref[...] = lse_all

    qspec = pl.BlockSpec((WINDOW, qw), lambda b, n: (b * nb + n, 0))
    return _call(name, body, (n_ex, nb),
                 [qspec] + _attn_specs(seq, n_ex, n_meta, kw) + _attn_specs(seq, n_ex, n_meta, kw)
                 + [pl.BlockSpec((1, n_q), lambda b, n: (0, 0)),
                    pl.BlockSpec((1,) + bias.shape[1:], lambda b, n: (jnp.minimum(n, 1), 0, 0))],
                 [qspec, pl.BlockSpec((WINDOW, n_q), lambda b, n: (b * nb + n, 0))],
                 [_sds((n_ex * seq, qw), BF16), _sds((n_ex * seq, n_q), F32)])(q, k, k, k, v, v, v, sinks, bias)


def _attn_bwd(name, q, k, v, sinks, o, lse, do, n_ex, seq, n_meta, n_kv, qpk, hd):
    nb = seq // WINDOW
    n_q = n_kv * qpk
    kw = n_kv * hd
    qw = n_q * hd
    n_keys = 2 * WINDOW + n_meta
    bias = _attn_bias(qpk, n_keys)

    def body(q_ref, kp_ref, kc_ref, km_ref, vp_ref, vc_ref, vm_ref, sk_ref, o_ref, lse_ref, do_ref, bias_ref,
             dq_ref, dk_ref, dv_ref, dkm_ref, dvm_ref, dsk_ref):
        n = pl.program_id(1)

        @pl.when(n == 0)
        def _():
            dk_ref[...] = jnp.zeros_like(dk_ref)
            dv_ref[...] = jnp.zeros_like(dv_ref)
            dkm_ref[...] = jnp.zeros_like(dkm_ref)
            dvm_ref[...] = jnp.zeros_like(dvm_ref)

        @pl.when((n == 0) & (pl.program_id(0) == 0))
        def _():
            dsk_ref[...] = jnp.zeros_like(dsk_ref)

        sinks_v = sk_ref[...]
        lse_v = lse_ref[...]
        dq_parts, dk_parts, dv_parts = [], [], []
        dsk = jnp.zeros((1, n_q), F32)
        for h in range(n_kv):
            hs = slice(h * hd, (h + 1) * hd)
            kb = jnp.concatenate([kp_ref[:, hs], kc_ref[:, hs], km_ref[:, hs]], axis=0)
            vb = jnp.concatenate([vp_ref[:, hs], vc_ref[:, hs], vm_ref[:, hs]], axis=0)
            qs = _stack_heads(q_ref, h, qpk, hd)
            dos = _stack_heads(do_ref, h, qpk, hd)
            delta = jnp.sum(dos.astype(F32) * _stack_heads(o_ref, h, qpk, hd, F32), axis=-1, keepdims=True)
            lse_c = _stack_cols(lse_v, h, qpk)
            p = jnp.exp(_dot_nt(qs, kb) + bias_ref[0] - lse_c)
            ds = p * (_dot_nt(dos, vb) - delta)
            dqs = _dot(ds, kb)
            dk_parts.append(_dot_tn(ds, qs))
            dv_parts.append(_dot_tn(p, dos))
            dsink = -jnp.exp(_sink_col(sinks_v, h, qpk) - lse_c) * delta
            for gq in range(qpk):
                dq_parts.append(dqs[gq * WINDOW:(gq + 1) * WINDOW])
                dsk = dsk + _put_col(jnp.sum(dsink[gq * WINDOW:(gq + 1) * WINDOW], axis=0, keepdims=True), h * qpk + gq, n_q)
        dq_ref[...] = jnp.concatenate(dq_parts, axis=1).astype(BF16)
        dsk_ref[...] += dsk
        dkb = jnp.concatenate(dk_parts, axis=1)
        dvb = jnp.concatenate(dv_parts, axis=1)
        prev = pl.ds(pl.multiple_of(jnp.maximum(n - 1, 0) * WINDOW, WINDOW), WINDOW)
        cur = pl.ds(pl.multiple_of(n * WINDOW, WINDOW), WINDOW)
        dk_ref[prev, :] += dkb[0:WINDOW]
        dv_ref[prev, :] += dvb[0:WINDOW]
        dk_ref[cur, :] += dkb[WINDOW:2 * WINDOW]
        dv_ref[cur, :] += dvb[WINDOW:2 * WINDOW]
        dkm_ref[...] += dkb[2 * WINDOW:]
        dvm_ref[...] += dvb[2 * WINDOW:]

    qspec = pl.BlockSpec((WINDOW, qw), lambda b, n: (b * nb + n, 0))
    exspec = pl.BlockSpec((seq, kw), lambda b, n: (b, 0))
    mspec = pl.BlockSpec((n_meta, kw), lambda b, n: (b, 0))
    return _call(name, body, (n_ex, nb),
                 [qspec] + _attn_specs(seq, n_ex, n_meta, kw) + _attn_specs(seq, n_ex, n_meta, kw)
                 + [pl.BlockSpec((1, n_q), lambda b, n: (0, 0)), qspec,
                    pl.BlockSpec((WINDOW, n_q), lambda b, n: (b * nb + n, 0)), qspec,
                    pl.BlockSpec((1,) + bias.shape[1:], lambda b, n: (jnp.minimum(n, 1), 0, 0))],
                 [qspec, exspec, exspec, mspec, mspec, pl.BlockSpec((1, n_q), lambda b, n: (0, 0))],
                 [_sds((n_ex * seq, qw), BF16), _sds((n_ex * seq, kw), F32), _sds((n_ex * seq, kw), F32),
                  _sds((n_ex * n_meta, kw), F32), _sds((n_ex * n_meta, kw), F32), _sds((1, n_q), F32)])(
                      q, k, k, k, v, v, v, sinks, o, lse, do, bias)


def _attn_out(name, o, h, w_o, n_rows):
    qw, d = w_o.shape
    tm = ROW_TILE

    def body(o_ref, h_ref, w_ref, out_ref):
        out_ref[...] = h_ref[...] + _dot(o_ref[...], w_ref[...])

    return _call(name, body, (n_rows // tm,),
                 [pl.BlockSpec((tm, qw), lambda i: (i, 0)), pl.BlockSpec((tm, d), lambda i: (i, 0)),
                  pl.BlockSpec((qw, d), lambda i: (0, 0))],
                 pl.BlockSpec((tm, d), lambda i: (i, 0)), _sds((n_rows, d), F32))(o, h, w_o)


def _attn_out_bwd(name, dh, w_o, n_rows):
    qw, d = w_o.shape
    tm = ROW_TILE

    def body(dh_ref, w_ref, do_ref):
        do_ref[...] = _dot_nt(dh_ref[...], w_ref[...]).astype(BF16)

    return _call(name, body, (n_rows // tm,),
                 [pl.BlockSpec((tm, d), lambda i: (i, 0)), pl.BlockSpec((qw, d), lambda i: (0, 0))],
                 pl.BlockSpec((tm, qw), lambda i: (i, 0)), _sds((n_rows, qw), BF16))(dh, w_o)


def _q_bwd(name, dq, qraw, q_gain_t, cos2, sin2, w_q, h, g, dh, n_rows, n_q, hd):
    d, qw = w_q.shape
    tm = ROW_TILE

    sel, sel_t = _head_selectors(n_q, hd)

    def body(dq_ref, raw_ref, qg_ref, c_ref, s_ref, e_ref, et_ref, w_ref, h_ref, g_ref, dh_ref, draw_ref, o_ref, dqg_ref, dg_ref):
        dx, dgain = _head_prep_bwd(raw_ref[...], qg_ref[...], c_ref[...], s_ref[...], e_ref[...], et_ref[...],
                                   dq_ref[...].astype(F32), hd)
        draw = dx.astype(BF16)
        draw_ref[...] = draw
        dhn, dg = _rms_bwd(h_ref[...], g_ref[...], _dot_nt(draw, w_ref[...]))
        o_ref[...] = dh_ref[...] + dhn
        first = pl.program_id(0) == 0
        _acc_out(dqg_ref, dgain, first)
        _acc_out(dg_ref, dg, first)

    row = lambda w: pl.BlockSpec((tm, w), lambda i: (i, 0))
    one = lambda w: pl.BlockSpec((1, w), lambda i: (0, 0))
    return _call(name, body, (n_rows // tm,),
                 [row(qw), row(qw), one(qw), row(2 * hd), row(2 * hd), pl.BlockSpec(sel.shape, lambda i: (0, 0)),
                  pl.BlockSpec(sel_t.shape, lambda i: (0, 0)), pl.BlockSpec((d, qw), lambda i: (0, 0)), row(d), one(d), row(d)],
                 [row(qw), row(d), one(qw), one(d)],
                 [_sds((n_rows, qw), BF16), _sds((n_rows, d), F32), _sds((1, qw), F32), _sds((1, d), F32)])(
                     dq, qraw, q_gain_t, cos2, sin2, sel, sel_t, w_q, h, g, dh)


def _kv_bwd(name, dk, dv, kvraw, k_gain_t, cos2, sin2, w_kv, h, g, dh_main, n_rows, n_main, n_kv, hd, after=()):
    d, kvw = w_kv.shape
    kw = n_kv * hd
    tm = ROW_TILE
    n_main_tiles = n_main // tm

    sel, sel_t = _head_selectors(n_kv, hd)

    def body(dk_ref, dv_ref, raw_ref, kg_ref, c_ref, s_ref, e_ref, et_ref, w_ref, h_ref, g_ref, dh_ref, draw_ref, o_ref, dkg_ref,
             dg_ref):
        i = pl.program_id(0)
        dx, dgain = _head_prep_bwd(raw_ref[:, :kw], kg_ref[...], c_ref[...], s_ref[...], e_ref[...], et_ref[...], dk_ref[...], hd)
        draw = jnp.concatenate([dx, dv_ref[...]], axis=1).astype(BF16)
        draw_ref[...] = draw
        dhn, dg = _rms_bwd(h_ref[...], g_ref[...], _dot_nt(draw, w_ref[...]))
        o_ref[...] = jnp.where(i < n_main_tiles, dh_ref[...], 0.0) + dhn
        _acc_out(dkg_ref, dgain, i == 0)
        _acc_out(dg_ref, dg, i == 0)

    row = lambda w: pl.BlockSpec((tm, w), lambda i: (i, 0))
    one = lambda w: pl.BlockSpec((1, w), lambda i: (0, 0))
    return _call(name, body, (n_rows // tm,),
                 [row(kw), row(kw), row(kvw), one(kw), row(2 * hd), row(2 * hd), pl.BlockSpec(sel.shape, lambda i: (0, 0)),
                  pl.BlockSpec(sel_t.shape, lambda i: (0, 0)), pl.BlockSpec((d, kvw), lambda i: (0, 0)), row(d),
                  one(d), pl.BlockSpec((tm, d), lambda i: (jnp.minimum(i, n_main_tiles - 1), 0))],
                 [row(kvw), row(d), one(kw), one(d)],
                 [_sds((n_rows, kvw), BF16), _sds((n_rows, d), F32), _sds((1, kw), F32), _sds((1, d), F32)], after=after)(
                     dk, dv, kvraw, k_gain_t, cos2, sin2, sel, sel_t, w_kv, h, g, dh_main)


def _tn_rms(name, h, g, b, n_rows, out_dtype=F32):
    d = h.shape[1]
    nb = b.shape[1]
    tk = _contract_tile(n_rows)
    return _tn(name, (h, g, b),
               [pl.BlockSpec((tk, d), lambda j, k: (k, 0)), pl.BlockSpec((1, d), lambda j, k: (0, 0)),
                pl.BlockSpec((tk, nb), lambda j, k: (k, 0))],
               lambda j, h_ref, g_ref, b_ref: (_rms(h_ref[...], g_ref[...]), b_ref[...]), 1, d, nb, n_rows, tk, out_dtype=out_dtype)


def _tn_plain(name, a, b, nj, a_cols, b_cols, n_rows, a_fn=None, out_dtype=F32, after=()):
    tk = _contract_tile(n_rows)
    fa = (lambda v: v) if a_fn is None else a_fn
    a_map = (lambda j, k: (k, j)) if a.shape[1] != a_cols else (lambda j, k: (k, 0))
    b_map = (lambda j, k: (k, j)) if b.shape[1] != b_cols else (lambda j, k: (k, 0))
    return _tn(name, (a, b), [pl.BlockSpec((tk, a_cols), a_map), pl.BlockSpec((tk, b_cols), b_map)],
               lambda j, a_ref, b_ref: (fa(a_ref[...]), b_ref[...]), nj, a_cols, b_cols, n_rows, tk, out_dtype=out_dtype,
               after=after)


def _cast_layer(name, a, layer):
    _, r, c = a.shape
    tr = _row_tile(r, 256)

    def body(a_ref, o_ref):
        o_ref[...] = a_ref[0].astype(BF16)

    return _call(name, body, (r // tr,), [pl.BlockSpec((1, tr, c), lambda i: (layer, i, 0))],
                 pl.BlockSpec((tr, c), lambda i: (i, 0)), _sds((r, c), BF16))(a)


def _adamw_math(w, g, m, v):
    c1 = 1.0 - ADAM_B1 ** ADAM_STEP
    c2 = 1.0 - ADAM_B2 ** ADAM_STEP
    nm = ADAM_B1 * m + (1.0 - ADAM_B1) * g
    nv = ADAM_B2 * v + (1.0 - ADAM_B2) * (g * g)
    return -ADAM_LR * ((nm / c1) / (jnp.sqrt(nv / c2) + ADAM_EPS) + ADAM_WD * w), nm, nv


def _adamw(name, w, g, m, v, after=()):
    rows, cols = w.shape
    tr = 128 if rows % 128 == 0 else rows

    def body(w_ref, g_ref, m_ref, v_ref, d_ref, nm_ref, nv_ref):
        d_ref[...], nm_ref[...], nv_ref[...] = _adamw_math(w_ref[...], g_ref[...], m_ref[...], v_ref[...])

    spec = pl.BlockSpec((tr, cols), lambda i: (i, 0))
    return _call(name, body, (rows // tr,), [spec] * 4, [spec] * 3, [_sds((rows, cols), F32)] * 3, after=after)(w, g, m, v)


def _adamw_from_halves(name, w, m, v, sources, half_index, transposed, after=()):
    n_layers, r, c = w.shape
    lanes = 1024
    after = tuple(after)
    if transposed:
        rows_half, tr = c // 2, 128
        grid = (n_layers, r // tr)
        w_spec = pl.BlockSpec((1, tr, c), lambda l, i, s: (l, i, 0))
        g_spec = lambda off: pl.BlockSpec((rows_half, tr), lambda l, i, s: (off // rows_half, i))
    else:
        rows_half = r // 2
        grid = (n_layers, 2)
        w_spec = pl.BlockSpec((1, rows_half, c), lambda l, k, s: (l, k, 0))
        g_spec = lambda off: pl.BlockSpec((rows_half, lanes), lambda l, k, s: (off // rows_half, 0))

    def body(s_ref, w_ref, m_ref, v_ref, t0_ref, o0_ref, t1_ref, o1_ref, *rest):
        g_ref, d_ref, nm_ref, nv_ref = rest[len(after):]
        layer, k, mine = pl.program_id(0), pl.program_id(1), s_ref[0]
        tot = jnp.where(layer == 0, t0_ref[...], t1_ref[...])
        oth = jnp.where(layer == 0, o0_ref[...], o1_ref[...])
        if transposed:
            g = jnp.concatenate([jnp.where(mine == 0, tot, oth), jnp.where(mine == 0, oth, tot)], axis=0).T
        else:
            g = jnp.where(k == mine, tot, oth)
        g_ref[0] = g
        d_ref[0], nm_ref[0], nv_ref[0] = _adamw_math(w_ref[0], g, m_ref[0], v_ref[0])

    (t0, o0, off0), (t1, o1, off1) = sources
    grid_spec = pltpu.PrefetchScalarGridSpec(
        num_scalar_prefetch=1, grid=grid,
        in_specs=[w_spec] * 3 + [g_spec(off0), g_spec(off0), g_spec(off1), g_spec(off1)] + [_ANY] * len(after),
        out_specs=[w_spec] * 4)
    return pl.pallas_call(
        body, name=name, grid_spec=grid_spec, out_shape=[_sds(w.shape, F32)] * 4,
        compiler_params=pltpu.CompilerParams(dimension_semantics=("arbitrary", "arbitrary"),
                                             vmem_limit_bytes=V7X_VMEM_LIMIT))(half_index, w, m, v, t0, o0, t1, o1, *after)


def _position():
    return lax.axis_index("x"), lax.axis_index("y"), lax.axis_index("c")


def _other_chips(x, y):
    return [(1 - x, y), (x, 1 - y), (1 - x, 1 - y)]


def _peers_chips(x, y, c):
    return [(cx, cy, c) for cx, cy in _other_chips(x, y)]


def _peers_sibling(x, y, c):
    return [(x, y, 1 - c)]


def _peers_chips_and_sibling(x, y, c):
    return _peers_chips(x, y, c) + _peers_sibling(x, y, c)


def _comm_call(name, body, n_in, out_shape, scratch, sequencer=None):
    if sequencer is None:
        return pl.pallas_call(
            body, name=name, in_specs=[_HBM] * n_in, out_specs=[_HBM] * len(out_shape), out_shape=out_shape,
            scratch_shapes=list(scratch),
            compiler_params=pltpu.CompilerParams(has_side_effects=True, vmem_limit_bytes=V7X_VMEM_LIMIT))
    collective_id, peers = sequencer

    def seq_body(*refs):
        barrier = pltpu.get_barrier_semaphore()
        plist = peers(*_position())
        for peer in plist:
            pl.semaphore_signal(barrier, inc=1, device_id=peer, device_id_type=MESH)
        pl.semaphore_wait(barrier, len(plist))
        body(*refs)

    return pl.kernel(seq_body, out_type=out_shape, mesh=plsc.ScalarSubcoreMesh(axis_name="sequencer", num_cores=1), name=name,
                     scratch_types=list(scratch), compiler_params=pltpu.CompilerParams(collective_id=collective_id))


def _n_chunks(rows, want, dtype):
    align = 16 if dtype == BF16 else 8
    n = want
    while n > 1 and (rows % n or (rows // n) % align):
        n -= 1
    return n


def _remote(src, dst, send_sem, recv_sem, device):
    return pltpu.make_async_remote_copy(src_ref=src, dst_ref=dst, send_sem=send_sem, recv_sem=recv_sem,
                                        device_id=device, device_id_type=MESH)


def _start_in_chunks(src, dst, send_sem, recv_sem, device, want=8):
    rows = src.shape[0]
    n = _n_chunks(rows, want, src.dtype)
    for i in range(n):
        part = pl.ds(i * (rows // n), rows // n)
        _remote(src.at[part], dst.at[part], send_sem, recv_sem, device).start()


def _all_gather_chips(name, shards, split, collective_id=None):
    n = len(shards)

    def body(*refs):
        ins, outs = refs[:n], refs[n:2 * n]
        send_sems, recv_sems, local_sems = refs[2 * n:]
        x, y, c = _position()
        me = 2 * x + y
        chips = _other_chips(x, y)
        sibling = (x, y, 1 - c)
        sends, forwards = [], []
        for t in range(n):
            pltpu.make_async_copy(ins[t], outs[t].at[me], local_sems.at[t]).start()
        for t in range(n):
            r = ins[t].shape[0]
            rows = pl.ds(c * (r // 2), r // 2) if split[t] else pl.ds(0, r)
            for k, (cx, cy) in enumerate(chips):
                src, dst = ins[t].at[rows], outs[t].at[me, rows]
                _start_in_chunks(src, dst, send_sems.at[t, k], recv_sems.at[t, k], (cx, cy, c), want=4)
                sends.append(_remote(src, dst, send_sems.at[t, k], recv_sems.at[t, k], (cx, cy, c)))
        for t in range(n):
            r = ins[t].shape[0]
            rows = pl.ds(c * (r // 2), r // 2) if split[t] else pl.ds(0, r)
            for k, (cx, cy) in enumerate(chips):
                landed = outs[t].at[2 * cx + cy, rows]
                _remote(landed, landed, send_sems.at[t, k], recv_sems.at[t, k], (cx, cy, c)).wait_recv()
                if split[t]:
                    _start_in_chunks(landed, landed, send_sems.at[t, 3 + k], recv_sems.at[t, 3 + k], sibling, want=4)
                    forwards.append(_remote(landed, landed, send_sems.at[t, 3 + k], recv_sems.at[t, 3 + k], sibling))
        for t in range(n):
            if split[t]:
                r = ins[t].shape[0]
                other = pl.ds((1 - c) * (r // 2), r // 2)
                for k, (cx, cy) in enumerate(chips):
                    landed = outs[t].at[2 * cx + cy, other]
                    pltpu.make_async_remote_copy(
                        src_ref=landed, dst_ref=landed, send_sem=send_sems.at[t, 3 + k], recv_sem=recv_sems.at[t, 3 + k],
                        device_id=sibling, device_id_type=MESH).wait_recv()
        for cp in sends + forwards:
            cp.wait_send()
        for t in range(n):
            pltpu.make_async_copy(ins[t], outs[t].at[me], local_sems.at[t]).wait()

    out_shape = [_sds((N_CHIPS,) + s.shape, s.dtype) for s in shards]
    sequencer = None if collective_id is None else (collective_id, _peers_chips_and_sibling)
    return _comm_call(name, body, n, out_shape,
                      [pltpu.SemaphoreType.DMA((n, 6)), pltpu.SemaphoreType.DMA((n, 6)), pltpu.SemaphoreType.DMA((n,))],
                      sequencer)(*shards)


def _swap_halves_with_sibling(name, blob, collective_id=None):
    def body(b_ref, theirs_ref, send_sem, recv_sem):
        x, y, c = _position()
        sibling = (x, y, 1 - c)
        for k in range(b_ref.shape[1]):
            _start_in_chunks(b_ref.at[1 - c, k], theirs_ref.at[k], send_sem, recv_sem, sibling)
        _remote(b_ref.at[1 - c], theirs_ref, send_sem, recv_sem, sibling).wait()

    return _comm_call(name, body, 1, [_sds(blob.shape[1:], blob.dtype)],
                      [pltpu.SemaphoreType.DMA(()), pltpu.SemaphoreType.DMA(())],
                      None if collective_id is None else (collective_id, _peers_sibling))(blob)[0]


def _scatter_to_chips(name, parts, collective_id=None):
    def body(p_ref, o_ref, send_sems, recv_sems, local_sems):
        x, y, c = _position()
        me = 2 * x + y
        rows = p_ref.shape[1]
        n_loc = _n_chunks(rows, 16, p_ref.dtype)
        locs = [pltpu.make_async_copy(p_ref.at[me, pl.ds(i * (rows // n_loc), rows // n_loc)],
                                      o_ref.at[me, pl.ds(i * (rows // n_loc), rows // n_loc)], local_sems.at[i])
                for i in range(n_loc)]
        for loc in locs:
            loc.start()
        sends = []
        for k, (cx, cy) in enumerate(_other_chips(x, y)):
            src, dst = p_ref.at[2 * cx + cy], o_ref.at[me]
            _start_in_chunks(src, dst, send_sems.at[k], recv_sems.at[k], (cx, cy, c))
            sends.append(_remote(src, dst, send_sems.at[k], recv_sems.at[k], (cx, cy, c)))
        for k, (cx, cy) in enumerate(_other_chips(x, y)):
            landed = o_ref.at[2 * cx + cy]
            _remote(landed, landed, send_sems.at[k], recv_sems.at[k], (cx, cy, c)).wait_recv()
        for cp in sends:
            cp.wait_send()
        for loc in locs:
            loc.wait()

    def local_sems_shape(rows):
        return pltpu.SemaphoreType.DMA((_n_chunks(rows, 16, parts.dtype),))

    return _comm_call(name, body, 1, [_sds(parts.shape, parts.dtype)],
                      [pltpu.SemaphoreType.DMA((3,)), pltpu.SemaphoreType.DMA((3,)), local_sems_shape(parts.shape[1])],
                      None if collective_id is None else (collective_id, _peers_chips))(parts)[0]


def _share_with_sibling(name, mine, collective_id=None):
    def body(m_ref, o_ref, send_sem, recv_sem):
        x, y, c = _position()
        sibling = (x, y, 1 - c)
        _start_in_chunks(m_ref, o_ref, send_sem, recv_sem, sibling, want=16)
        _remote(m_ref, o_ref, send_sem, recv_sem, sibling).wait()

    return _comm_call(name, body, 1, [_sds(mine.shape, mine.dtype)],
                      [pltpu.SemaphoreType.DMA(()), pltpu.SemaphoreType.DMA(())],
                      None if collective_id is None else (collective_id, _peers_sibling))(mine)[0]


def _row_tile(rows, cap=640):
    best = rows
    for t in range(16, min(rows, cap) + 1, 16):
        if rows % t == 0:
            best = t
    return best


_ANY = pl.BlockSpec(memory_space=pl.ANY)


def _add_my_half(name, blob, theirs, half_index, out_dtype, after):
    n, rows, cols = theirs.shape
    tr = _row_tile(rows)
    after = tuple(after)

    def body(c_ref, a_ref, b_ref, *rest):
        o_ref = rest[-1]
        o_ref[...] = (a_ref[0].astype(F32) + b_ref[...].astype(F32)).astype(out_dtype)

    spec = pl.BlockSpec((1, tr, cols), lambda k, i, c: (k, i, 0))
    grid_spec = pltpu.PrefetchScalarGridSpec(
        num_scalar_prefetch=1, grid=(n, rows // tr),
        in_specs=[pl.BlockSpec((1, 1, tr, cols), lambda k, i, c: (c[0], k, i, 0)), spec] + [_ANY] * len(after), out_specs=spec)
    return pl.pallas_call(
        body, name=name, grid_spec=grid_spec, out_shape=_sds(theirs.shape, out_dtype),
        compiler_params=pltpu.CompilerParams(dimension_semantics=("arbitrary", "arbitrary"),
                                             vmem_limit_bytes=V7X_VMEM_LIMIT))(half_index, blob, theirs, *after)


def _sum_slots(name, parts, after):
    n, rows, cols = parts.shape
    tr = _row_tile(rows)

    def body(p_ref, o_ref):
        acc = p_ref[0].astype(F32)
        for k in range(1, n):
            acc = acc + p_ref[k].astype(F32)
        o_ref[...] = acc

    return _call(name, body, (rows // tr,), [pl.BlockSpec((n, tr, cols), lambda i: (0, i, 0))],
                 pl.BlockSpec((tr, cols), lambda i: (i, 0)), _sds((rows, cols), F32), after=after)(parts)


def _reduce_small_adamw(name, grads, loss_tile, ws, ms, vs, after=()):
    n = len(grads)
    srcs = list(grads) + [loss_tile]
    after = tuple(after)

    def body(*refs):
        refs = refs[:4 * n + 1] + refs[4 * n + 1 + len(after):]
        g_in, w_in, m_in, v_in = refs[:n + 1], refs[n + 1:2 * n + 1], refs[2 * n + 1:3 * n + 1], refs[3 * n + 1:4 * n + 1]
        outs = refs[4 * n + 1:8 * n + 2]
        g_out, d_out, nm_out, nv_out, loss_out = outs[:n], outs[n:2 * n], outs[2 * n:3 * n], outs[3 * n:4 * n], outs[4 * n]
        bufs = refs[8 * n + 2:9 * n + 3]
        send_sems, recv_sems = refs[9 * n + 3:]
        x, y, c = _position()
        me = 4 * x + 2 * y + c
        chip = 2 * x + y
        peers = [(1 - x if dlt & 4 else x, 1 - y if dlt & 2 else y, 1 - c if dlt & 1 else c) for dlt in range(1, N_DEV)]
        sends = []
        for t in range(n + 1):
            bufs[t][me] = g_in[t][...]
            for k, peer in enumerate(peers):
                cp = _remote(g_in[t], bufs[t].at[me], send_sems.at[t, k], recv_sems.at[t, k], peer)
                cp.start()
                sends.append(cp)
        for t in range(n + 1):
            for k, (tx, ty, tc) in enumerate(peers):
                landed = bufs[t].at[4 * tx + 2 * ty + tc]
                _remote(landed, landed, send_sems.at[t, k], recv_sems.at[t, k], (tx, ty, tc)).wait_recv()
        for cp in sends:
            cp.wait_send()
        for t in range(n + 1):
            total = bufs[t][0]
            for k in range(1, N_DEV):
                total = total + bufs[t][k]
            if t == n:
                loss_out[...] = total
                continue
            cols = w_in[t].shape[1]
            if cols == total.shape[1]:
                g_out[t][...] = total
                d_out[t][...], nm_out[t][...], nv_out[t][...] = _adamw_math(w_in[t][...], total, m_in[t][...], v_in[t][...])
            else:
                for j in range(N_CHIPS):
                    @pl.when(chip == j)
                    def _(t=t, j=j, cols=cols, total=total):
                        mine = total[:, j * cols:(j + 1) * cols]
                        g_out[t][...] = mine
                        d_out[t][...], nm_out[t][...], nv_out[t][...] = _adamw_math(w_in[t][...], mine, m_in[t][...], v_in[t][...])

    w_shapes = [_sds(a.shape, F32) for a in ws]
    return pl.pallas_call(
        body, name=name, in_specs=[_VMEM] * (4 * n + 1) + [_ANY] * len(after), out_specs=[_VMEM] * (4 * n + 1),
        out_shape=w_shapes * 4 + [_sds(loss_tile.shape, F32)],
        scratch_shapes=[pltpu.VMEM((N_DEV,) + a.shape, F32) for a in srcs]
        + [pltpu.SemaphoreType.DMA((n + 1, N_DEV - 1)), pltpu.SemaphoreType.DMA((n + 1, N_DEV - 1))],
        compiler_params=pltpu.CompilerParams(has_side_effects=True, vmem_limit_bytes=V7X_VMEM_LIMIT))(
            *srcs, *ws, *ms, *vs, *after)


_BIG = ("ffn1_w_gate_up", "ffn1_w_down", "ffn2_w_gate_up", "ffn2_w_down", "ssm_w_in", "ssm_w_out", "w_kv", "attn_w_q", "attn_w_o")
_TRANSPOSED = ("ffn1_w_gate_up", "ffn2_w_gate_up")
_FROM_HALVES = _TRANSPOSED + ("ffn1_w_down", "ffn2_w_down")
_SMALL = ("meta_tokens", "ffn1_norm", "mix_norm", "ffn2_norm", "ssm_lambda_re", "ssm_lambda_im", "ssm_b_re", "ssm_b_im",
          "ssm_c_re", "ssm_c_im", "ssm_log_step", "ssm_d", "kv_norm", "k_norm", "q_norm", "attn_sinks")
_ORDER = ("meta_tokens", "ffn1_norm", "ffn1_w_gate_up", "ffn1_w_down", "mix_norm", "ffn2_norm", "ffn2_w_gate_up", "ffn2_w_down",
          "ssm_w_in", "ssm_lambda_re", "ssm_lambda_im", "ssm_b_re", "ssm_b_im", "ssm_c_re", "ssm_c_im", "ssm_log_step", "ssm_d",
          "ssm_w_out", "kv_norm", "w_kv", "k_norm", "attn_w_q", "q_norm", "attn_sinks", "attn_w_o")


def _step(x, target, w, m, v):
    n_ex, seq, d = x.shape
    n_meta = w["meta_tokens"].shape[0]
    n_main = n_ex * seq
    n_all = n_main + n_ex * META_BLOCK
    n_g, n_p, n_c = w["ssm_b_re"].shape[1:]
    hd = w["k_norm"].shape[0]
    n_kv = w["w_kv"].shape[1] // (2 * hd)
    n_q = w["attn_w_q"].shape[2] // hd
    qpk = n_q // n_kv
    px, py, pc = _position()
    chip = 2 * px + py

    def cast(name, layer=0):
        a = w[name]
        return _cast_layer(f"cast_{name}_{layer}", a if a.ndim == 3 else a[None], layer)

    g_a = _all_gather_chips("gather_first", [cast("ffn1_w_gate_up"), w["meta_tokens"], w["ssm_d"]], [True, False, False],
                            collective_id=12)
    g_d = _all_gather_chips("gather_next", [cast("ffn1_w_down"), cast("ssm_w_in"), cast("ssm_w_out")], [True] * 3, collective_id=13)
    second = [cast("ffn2_w_gate_up"), cast("ffn2_w_down"), cast("w_kv")]
    g_b = _all_gather_chips("gather_second", second, [True] * 3, collective_id=1)
    third = [cast("ffn1_w_gate_up", 1), cast("ffn1_w_down", 1), cast("attn_w_q"), cast("attn_w_o"),
             cast("ffn2_w_gate_up", 1), cast("ffn2_w_down", 1)]
    g_c = _all_gather_chips("gather_third", third, [True] * 6, collective_id=2)
    wgu = {("ffn1", 0): g_a[0], ("ffn1", 1): g_c[0], ("ffn2", 0): g_b[0], ("ffn2", 1): g_c[4]}
    wd = {("ffn1", 0): g_d[0], ("ffn1", 1): g_c[1], ("ffn2", 0): g_b[1], ("ffn2", 1): g_c[5]}
    wd = {key: a.reshape(-1, d) for key, a in wd.items()}
    w_in = g_d[1].reshape(d, -1)
    wout4 = g_d[2]
    w_q = g_c[2].reshape(d, -1)
    w_o = g_c[3].reshape(-1, d)
    w_kv = g_b[2].reshape(d, -1)
    meta_full = jnp.transpose(g_a[1], (1, 0, 2)).reshape(n_meta, d)
    dskip = g_a[2].reshape(1, -1)

    row1 = lambda a: a.reshape(1, -1)
    ssm_args = tuple(w[k][0] for k in ("ssm_lambda_re", "ssm_lambda_im", "ssm_log_step", "ssm_b_re", "ssm_b_im", "ssm_c_re", "ssm_c_im"))
    (bb, cb, a_re, a_im), ssm_vjp = jax.vjp(_ssm_mats, *ssm_args)
    bb16, cb16 = bb.astype(BF16), cb.astype(BF16)
    a_re_s, a_im_s = lax.stop_gradient(a_re), lax.stop_gradient(a_im)
    half = n_g * n_p // 4
    tabs_f = _scan_tables(a_re_s, a_im_s, False)
    tabs_b = _scan_tables(a_re_s, a_im_s, True)

    freqs = ROPE_THETA ** (-jnp.arange(0, hd // 2, dtype=F32) * 2.0 / hd)
    pos_main = jnp.tile(n_meta + jnp.arange(seq), n_ex)
    pos_meta = jnp.tile(jnp.maximum(jnp.arange(META_BLOCK) - (META_BLOCK - n_meta), 0), n_ex)
    ang = jnp.concatenate([pos_main, pos_meta]).astype(F32)[:, None] * freqs[None, :]
    cos = jnp.concatenate([jnp.cos(ang), jnp.cos(ang)] * 2, axis=1)
    sin_s = jnp.concatenate([-jnp.sin(ang), jnp.sin(ang)] * 2, axis=1)
    k_gain_t = jnp.tile(row1(w["k_norm"]), (1, n_kv))
    score_scale = hd ** -0.5
    q_gain_t = jnp.tile(row1(w["q_norm"][0]), (1, n_q)) * score_scale

    meta_block = jnp.concatenate([jnp.zeros((META_BLOCK - n_meta, d), F32), meta_full], axis=0)
    h0 = jnp.concatenate([x.reshape(n_main, d)] + [meta_block] * n_ex, axis=0)

    g = lambda name, layer: row1(w[name][layer])
    h1, gu1 = _ffn_fwd("l0_ffn1", h0, g("ffn1_norm", 0), wgu["ffn1", 0], wd["ffn1", 0], n_all)
    u, bu = _ssm_in("ssm_in", h1, g("mix_norm", 0), w_in, bb16, n_all)
    xs = _scan_fwd("ssm_scan", bu, tabs_f, n_ex, seq)
    h2, y = _ssm_out("ssm_out", xs, u, dskip, cb16, wout4, h1, n_all)
    h3, gu2 = _ffn_fwd("l0_ffn2", h2, g("ffn2_norm", 0), wgu["ffn2", 0], wd["ffn2", 0], n_all)
    kvraw, k, vv = _kv_proj("kv_proj", h3, row1(w["kv_norm"]), w_kv, k_gain_t, cos, sin_s, n_all, n_kv, hd)
    h4, gu3 = _ffn_fwd("l1_ffn1", h3, g("ffn1_norm", 1), wgu["ffn1", 1], wd["ffn1", 1], n_main)
    qraw, q = _q_proj("q_proj", h4, g("mix_norm", 1), w_q, q_gain_t, cos, sin_s, n_main, n_q, hd)
    sinks = row1(w["attn_sinks"][0])
    o, lse = _attn_fwd("attn_fwd", q, k, vv, sinks, n_ex, seq, n_meta, n_kv, qpk, hd)
    h5 = _attn_out("attn_out", o, h4, w_o, n_main)
    (dh6, loss_tile), gu4 = _ffn_fwd("l1_ffn2", h5, g("ffn2_norm", 1), wgu["ffn2", 1], wd["ffn2", 1], n_main,
                                     target=target.reshape(n_main, d))

    lanes = 1024

    def rs_start(tag, entries, ids):
        pieces = [gr.reshape(N_CHIPS, 2, -1, lanes) for _, _, gr in entries]
        blob = jnp.transpose(jnp.concatenate(pieces, axis=2), (1, 0, 2, 3)).astype(BF16)
        return dict(tag=tag, entries=entries, ids=ids, blob=blob, theirs=_swap_halves_with_sibling(tag + "_swap", blob, ids[0]))

    def rs_scatter(st, after):
        chip_sum = _add_my_half(st["tag"] + "_chip_sum", st["blob"], st["theirs"], jnp.reshape(pc, (1,)).astype(jnp.int32), BF16, after)
        st["chip_sum"] = chip_sum
        st["landed"] = _scatter_to_chips(st["tag"] + "_scatter", chip_sum, st["ids"][1])

    def rs_finish(st, after):
        total = _sum_slots(st["tag"] + "_sum", st["landed"], after)
        st["total"] = total
        other = _share_with_sibling(st["tag"] + "_share", total, st["ids"][2])
        halves = (jnp.where(pc == 0, total, other), jnp.where(pc == 0, other, total))
        out, off = {}, 0
        for name, layer, gr in st["entries"]:
            rows = gr.shape[1] * gr.shape[2] // lanes // 2
            if name in _FROM_HALVES:
                out[name, layer] = (total, other, off)
            else:
                out[name, layer] = jnp.concatenate([hv[off:off + rows].reshape(-1) for hv in halves])
            off += rows
        return out

    small = {}
    dh5, dg_f2l1, dwgu_f2l1, dwd_f2l1 = _ffn_bwd("l1_ffn2", dh6, h5, g("ffn2_norm", 1), gu4, wgu["ffn2", 1], wd["ffn2", 1], n_main)
    do = _attn_out_bwd("attn_out_bwd", dh5, w_o, n_main)
    dw_o = _tn_plain("attn_dwo", o, dh5, 1, o.shape[1], d, n_main, out_dtype=BF16).reshape(N_CHIPS, -1, d)
    dq, dk_main, dv_main, dk_meta, dv_meta, dsinks = _attn_bwd("attn_bwd", q, k, vv, sinks, o, lse, do, n_ex, seq, n_meta, n_kv, qpk, hd)
    dqraw, dh4, dq_gain, dg_mix1 = _q_bwd("q_bwd", dq, qraw, q_gain_t, cos, sin_s, w_q, h4, g("mix_norm", 1), dh5, n_main, n_q, hd)
    dw_q = _tn_rms("attn_dwq", h4, g("mix_norm", 1), dqraw, n_main, out_dtype=BF16).reshape(N_CHIPS, -1, dqraw.shape[1])
    dh3m, dg_f1l1, dwgu_f1l1, dwd_f1l1 = _ffn_bwd("l1_ffn1", dh4, h3, g("ffn1_norm", 1), gu3, wgu["ffn1", 1], wd["ffn1", 1], n_main)
    rs1 = rs_start("rs1", [("ffn2_w_gate_up", 1, dwgu_f2l1), ("ffn1_w_gate_up", 1, dwgu_f1l1), ("ffn2_w_down", 1, dwd_f2l1),
                           ("ffn1_w_down", 1, dwd_f1l1), ("attn_w_o", 0, dw_o), ("attn_w_q", 0, dw_q)], (3, 4, 5))

    def with_meta(main, meta):
        blocks = [jnp.pad(meta[b * n_meta:(b + 1) * n_meta], ((META_BLOCK - n_meta, 0), (0, 0))) for b in range(n_ex)]
        return jnp.concatenate([main] + blocks, axis=0)

    dkvraw, dh3, dk_gain, dg_kv = _kv_bwd("kv_bwd", with_meta(dk_main, dk_meta), with_meta(dv_main, dv_meta), kvraw, k_gain_t,
                                          cos, sin_s, w_kv, h3, row1(w["kv_norm"]), dh3m, n_all, n_main, n_kv, hd,
                                          after=(rs1["blob"],))
    rs_scatter(rs1, after=(dh3,))
    dw_kv = _tn_rms("kv_dw", h3, row1(w["kv_norm"]), dkvraw, n_all, out_dtype=BF16).reshape(N_CHIPS, -1, dkvraw.shape[1])
    dh2, dg_f2l0, dwgu_f2l0, dwd_f2l0 = _ffn_bwd("l0_ffn2", dh3, h2, g("ffn2_norm", 0), gu2, wgu["ffn2", 0], wd["ffn2", 0], n_all,
                                                 after=(rs1["chip_sum"],))
    reduced = rs_finish(rs1, after=(dh2, dwgu_f2l0, dwd_f2l0, dw_kv))
    rs0a = rs_start("rs0a", [("ffn2_w_gate_up", 0, dwgu_f2l0), ("ffn2_w_down", 0, dwd_f2l0), ("w_kv", 0, dw_kv)], (6, 7, 8))

    dy, dz, gx, dd = _ssm_out_bwd("ssm_out_bwd", dh2, y, u, cb16, wout4, n_all, after=(rs1["total"], rs0a["blob"]))
    rs_scatter(rs0a, after=(dy,))
    hw = y.shape[1]
    oc = wout4.shape[2]
    dw_out = _tn_plain("ssm_dwout", y, dz, wout4.shape[0], hw, oc, n_all, a_fn=_gelu, out_dtype=BF16)
    gbu, da = _scan_bwd("ssm_scan_bwd", gx, xs, tabs_b, n_ex, seq, after=(rs0a["chip_sum"],))
    du, dh1, dg_mix0 = _ssm_in_bwd("ssm_in_bwd", gbu, dy, dskip, bb16, w_in, h1, g("mix_norm", 0), dh2, n_all)
    reduced.update(rs_finish(rs0a, after=(dh1,)))
    dw_in = _tn_rms("ssm_dwin", h1, g("mix_norm", 0), du, n_all, out_dtype=BF16).reshape(N_CHIPS, -1, hw)
    (dh0, dh0_meta), dg_f1l0, dwgu_f1l0, dwd_f1l0 = _ffn_bwd("l0_ffn1", dh1, h0, g("ffn1_norm", 0), gu1, wgu["ffn1", 0], wd["ffn1", 0],
                                                             n_all, n_main, after=(rs0a["total"],))
    rs0b = rs_start("rs0b", [("ffn1_w_gate_up", 0, dwgu_f1l0), ("ffn1_w_down", 0, dwd_f1l0), ("ssm_w_out", 0, dw_out),
                             ("ssm_w_in", 0, dw_in)], (9, 10, 11))
    dcb = _tn_plain("ssm_dcb", xs, dy, 4, xs.shape[1] // 4, hw // 4, n_all, after=(rs0b["blob"],))
    rs_scatter(rs0b, after=(dcb,))
    dbb = _tn_plain("ssm_dbb", u, gbu, 4, hw // 4, gbu.shape[1] // 4, n_all, after=(rs0b["chip_sum"],))

    grad_x = dh0.reshape(n_ex, seq, d)
    da_sum = jnp.sum(da, axis=(0, 1)).reshape(4, 2, half)
    d_ssm = ssm_vjp((dbb, dcb, da_sum[:, 0].reshape(-1), da_sum[:, 1].reshape(-1)))
    for key, val in zip(("ssm_lambda_re", "ssm_lambda_im", "ssm_log_step", "ssm_b_re", "ssm_b_im", "ssm_c_re", "ssm_c_im"), d_ssm):
        small[key] = val[None]
    small["meta_tokens"] = sum(dh0_meta[META_BLOCK * (b + 1) - n_meta:META_BLOCK * (b + 1)] for b in range(n_ex))
    small["ffn1_norm"] = jnp.concatenate([dg_f1l0, dg_f1l1], axis=0)
    small["ffn2_norm"] = jnp.concatenate([dg_f2l0, dg_f2l1], axis=0)
    small["mix_norm"] = jnp.concatenate([dg_mix0, dg_mix1], axis=0)
    small["ssm_d"] = dd
    small["kv_norm"] = dg_kv.reshape(-1)
    small["k_norm"] = jnp.sum(dk_gain.reshape(n_kv, hd), axis=0)
    small["q_norm"] = jnp.sum(dq_gain.reshape(n_q, hd), axis=0, keepdims=True) * score_scale
    small["attn_sinks"] = dsinks

    def view(name, a):
        if name in ("ssm_b_re", "ssm_b_im"):
            return a.reshape(-1, 128)
        return a.reshape(1, -1) if a.ndim == 1 else a.reshape(-1, a.shape[-1])

    grads, deltas, new_m, new_v = {}, {}, {}, {}

    def adamw_matrix(name, after=()):
        shape = w[name].shape
        if name in _FROM_HALVES:
            grads[name], deltas[name], new_m[name], new_v[name] = _adamw_from_halves(
                "adamw_" + name, w[name], m[name], v[name], [reduced[name, 0], reduced[name, 1]],
                jnp.reshape(pc, (1,)).astype(jnp.int32), name in _TRANSPOSED, after=after)
            return new_v[name]
        layers = [reduced[name, layer] for layer in range(2) if (name, layer) in reduced]
        grads[name] = jnp.concatenate(layers).reshape(shape)
        two_d = lambda a: a.reshape(-1, shape[-1])
        dl, nm, nv = _adamw("adamw_" + name, two_d(w[name]), two_d(grads[name]), two_d(m[name]), two_d(v[name]), after=after)
        deltas[name], new_m[name], new_v[name] = dl.reshape(shape), nm.reshape(shape), nv.reshape(shape)
        return nv

    placed = (rs0b["chip_sum"],)
    for name in ("ffn2_w_down", "attn_w_o", "attn_w_q", "w_kv"):
        placed = (adamw_matrix(name, after=placed),)
    tail = _reduce_small_adamw("small_tail", [view(k, small[k]) for k in _SMALL], loss_tile,
                               *[[view(k, t[k]) for k in _SMALL] for t in (w, m, v)], after=placed)
    n_small = len(_SMALL)
    for i, k in enumerate(_SMALL):
        grads[k], deltas[k] = tail[i].reshape(w[k].shape), tail[n_small + i].reshape(w[k].shape)
        new_m[k], new_v[k] = tail[2 * n_small + i].reshape(w[k].shape), tail[3 * n_small + i].reshape(w[k].shape)
    loss = jnp.sum(tail[-1])
    reduced.update(rs_finish(rs0b, after=(tail[-1],)))
    adamw_matrix("ffn2_w_gate_up", after=(rs0b["total"],))
    for name in ("ffn1_w_gate_up", "ffn1_w_down", "ssm_w_in", "ssm_w_out"):
        adamw_matrix(name)
    return (loss, grad_x, *[grads[k] for k in _ORDER], *[deltas[k] for k in _ORDER], *[new_m[k] for k in _ORDER],
            *[new_v[k] for k in _ORDER])


def kernel(x, meta_tokens, ffn1_norm, ffn1_w_gate_up, ffn1_w_down, mix_norm, ffn2_norm, ffn2_w_gate_up, ffn2_w_down, ssm_w_in, ssm_lambda_re, ssm_lambda_im, ssm_b_re, ssm_b_im, ssm_c_re, ssm_c_im, ssm_log_step, ssm_d, ssm_w_out, kv_norm, w_kv, k_norm, attn_w_q, q_norm, attn_sinks, attn_w_o, loss_target, m_meta_tokens, m_ffn1_norm, m_ffn1_w_gate_up, m_ffn1_w_down, m_mix_norm, m_ffn2_norm, m_ffn2_w_gate_up, m_ffn2_w_down, m_ssm_w_in, m_ssm_lambda_re, m_ssm_lambda_im, m_ssm_b_re, m_ssm_b_im, m_ssm_c_re, m_ssm_c_im, m_ssm_log_step, m_ssm_d, m_ssm_w_out, m_kv_norm, m_w_kv, m_k_norm, m_attn_w_q, m_q_norm, m_attn_sinks, m_attn_w_o, v_meta_tokens, v_ffn1_norm, v_ffn1_w_gate_up, v_ffn1_w_down, v_mix_norm, v_ffn2_norm, v_ffn2_w_gate_up, v_ffn2_w_down, v_ssm_w_in, v_ssm_lambda_re, v_ssm_lambda_im, v_ssm_b_re, v_ssm_b_im, v_ssm_c_re, v_ssm_c_im, v_ssm_log_step, v_ssm_d, v_ssm_w_out, v_kv_norm, v_w_kv, v_k_norm, v_attn_w_q, v_q_norm, v_attn_sinks, v_attn_w_o):
    args = locals()
    w = {k: args[k] for k in _ORDER}
    m = {k: args["m_" + k] for k in _ORDER}
    v = {k: args["v_" + k] for k in _ORDER}
    return _step(x, loss_target, w, m, v)
```

```python
import functools
import math

import jax
import jax.numpy as jnp
from jax import lax
from jax.experimental import pallas as pl
from jax.experimental.pallas import tpu as pltpu
from jax.experimental.pallas import tpu_sc as plsc

F32 = jnp.float32
BF16 = jnp.bfloat16
MESH = pl.DeviceIdType.MESH

EPS = 1e-6
NEG_INF = -1e30
ROPE_THETA = 10000.0
WINDOW = 128
META_BLOCK = 128
ROW_TILE = 256
SUBLANES = 8
V7X_VMEM_LIMIT = 56 * 2**20
N_CHIPS = 4
N_DEV = 8

ADAM_LR = 0.001
ADAM_B1 = 0.9
ADAM_B2 = 0.999
ADAM_EPS = 1e-08
ADAM_WD = 0.01
ADAM_STEP = 10

_HBM = pl.BlockSpec(memory_space=pltpu.HBM)
_VMEM = pl.BlockSpec(memory_space=pltpu.VMEM)


def _call(name, body, grid, in_specs, out_specs, out_shape, scratch=(), after=()):
    after = tuple(after)
    n_in = len(in_specs)

    def wrapped(*refs):
        return body(*refs[:n_in], *refs[n_in + len(after):])

    call = pl.pallas_call(
        wrapped, name=name, grid=grid, in_specs=list(in_specs) + [pl.BlockSpec(memory_space=pl.ANY)] * len(after),
        out_specs=out_specs, out_shape=out_shape, scratch_shapes=list(scratch),
        compiler_params=pltpu.CompilerParams(dimension_semantics=("arbitrary",) * len(grid),
                                             vmem_limit_bytes=V7X_VMEM_LIMIT))
    return lambda *operands: call(*operands, *after)


def _sds(shape, dtype):
    return jax.ShapeDtypeStruct(tuple(shape), dtype)


def _dot(a, b):
    return jnp.dot(a.astype(BF16), b.astype(BF16), preferred_element_type=F32)


def _dot_nt(a, b):
    return lax.dot_general(a.astype(BF16), b.astype(BF16), (((1,), (1,)), ((), ())), preferred_element_type=F32)


def _dot_tn(a, b):
    return lax.dot_general(a.astype(BF16), b.astype(BF16), (((0,), (0,)), ((), ())), preferred_element_type=F32)


def _rms(h, g):
    return h * lax.rsqrt(jnp.mean(h * h, axis=-1, keepdims=True) + EPS) * g


def _rms_bwd(h, g, dn):
    r = lax.rsqrt(jnp.mean(h * h, axis=-1, keepdims=True) + EPS)
    xh = h * r
    dxh = dn * g
    dg = jnp.sum(dn * xh, axis=0, keepdims=True)
    dh = r * (dxh - xh * jnp.mean(dxh * xh, axis=-1, keepdims=True))
    return dh, dg


def _sigmoid(x):
    return 0.5 * jnp.tanh(0.5 * x) + 0.5


def _gelu(y):
    k = math.sqrt(2.0 / math.pi)
    return 0.5 * y * (1.0 + jnp.tanh(k * (y + 0.044715 * y * y * y)))


def _gelu_grad(y):
    k = math.sqrt(2.0 / math.pi)
    t = jnp.tanh(k * (y + 0.044715 * y * y * y))
    return 0.5 * (1.0 + t) + 0.5 * y * (1.0 - t * t) * k * (1.0 + 3.0 * 0.044715 * y * y)


def _partner(x, lane, d):
    width = x.shape[-1]
    return jnp.where((lane & d) == 0, pltpu.roll(x, width - d, 1), pltpu.roll(x, d, 1))


def _split_bf16(x):
    hi = x.astype(BF16)
    return hi, (x - hi.astype(F32)).astype(BF16)


def _head_sums(x, sel):
    hi, lo = _split_bf16(x)
    return jnp.dot(hi, sel, preferred_element_type=F32) + jnp.dot(lo, sel, preferred_element_type=F32)


def _head_expand(v, sel_t):
    hi, lo = _split_bf16(v)
    return jnp.dot(hi, sel_t, preferred_element_type=F32) + jnp.dot(lo, sel_t, preferred_element_type=F32)


def _tile_lanes(t, width):
    return jnp.concatenate([t] * (width // t.shape[-1]), axis=1)


def _head_prep(x, gain_t, cos2, sin2, sel, sel_t, hd):
    width = x.shape[-1]
    lane = lax.broadcasted_iota(jnp.int32, x.shape, 1)
    r = _head_expand(lax.rsqrt(_head_sums(x * x, sel) * (1.0 / hd) + EPS), sel_t)
    y = x * r * gain_t
    return y * _tile_lanes(cos2, width) + _partner(y, lane, hd // 2) * _tile_lanes(sin2, width)


def _head_prep_bwd(x, gain_t, cos2, sin2, sel, sel_t, d_out, hd):
    width = x.shape[-1]
    lane = lax.broadcasted_iota(jnp.int32, x.shape, 1)
    r = _head_expand(lax.rsqrt(_head_sums(x * x, sel) * (1.0 / hd) + EPS), sel_t)
    xhat = x * r
    dy = d_out * _tile_lanes(cos2, width) + _partner(d_out * _tile_lanes(sin2, width), lane, hd // 2)
    dgain = jnp.sum(dy * xhat, axis=0, keepdims=True)
    dxh = dy * gain_t
    mean = _head_expand(_head_sums(dxh * xhat, sel) * (1.0 / hd), sel_t)
    return r * (dxh - xhat * mean), dgain


def _head_selectors(n_heads, hd):
    sel = (jnp.arange(n_heads * hd)[:, None] // hd == jnp.arange(128)[None, :]).astype(BF16)
    return sel, sel.T


def _acc_out(ref, val, first):
    @pl.when(first)
    def _():
        ref[...] = jnp.zeros_like(ref)
    ref[...] += val


def _wide_row_tile(n_rows, cap=512):
    best = 128
    for t in range(128, cap + 1, 128):
        if n_rows % t == 0:
            best = t
    return best


def _ffn_up(name, h, g, w4, n_rows):
    nj, d, fc = w4.shape
    tm = _wide_row_tile(n_rows)

    def body(h_ref, g_ref, w_ref, o_ref, n_ref):
        n = _rms(h_ref[...], g_ref[...]).astype(BF16)
        n_ref[...] = n
        for j in range(nj):
            o_ref[:, j * fc:(j + 1) * fc] = _dot(n, w_ref[j]).astype(BF16)

    return _call(name, body, (n_rows // tm,),
                 [pl.BlockSpec((tm, d), lambda i: (i, 0)), pl.BlockSpec((1, d), lambda i: (0, 0)),
                  pl.BlockSpec((nj, d, fc), lambda i: (0, 0, 0))],
                 [pl.BlockSpec((tm, nj * fc), lambda i: (i, 0)), pl.BlockSpec((tm, d), lambda i: (i, 0))],
                 [_sds((n_rows, nj * fc), BF16), _sds((n_rows, d), BF16)])(h, g, w4)


def _ffn_down(name, gu, h, wd, n_rows, target=None):
    f, d = wd.shape
    tm = ROW_TILE

    def body(gu_ref, h_ref, w_ref, *rest):
        half_a = gu_ref[:, :f] * 0.5
        s = (half_a + half_a * jnp.tanh(half_a)) * gu_ref[:, f:]
        y = h_ref[...] + 0.5 * _dot(s, w_ref[...])
        if target is None:
            o_ref, s_ref = rest
            o_ref[...] = y
        else:
            t_ref, dy_ref, l_ref, s_ref = rest
            e = y - t_ref[...]
            dy_ref[...] = e * (1.0 / d)
            e2 = jnp.sum((e * e).reshape(tm // SUBLANES, SUBLANES, d), axis=0)
            part = e2[:, 0:128]
            for k in range(1, d // 128):
                part = part + e2[:, k * 128:(k + 1) * 128]
            _acc_out(l_ref, part * (0.5 / d), pl.program_id(0) == 0)
        s_ref[...] = s

    row = lambda width: pl.BlockSpec((tm, width), lambda i: (i, 0))
    in_specs = [row(2 * f), row(d), pl.BlockSpec((f, d), lambda i: (0, 0))]
    if target is None:
        return _call(name, body, (n_rows // tm,), in_specs, [row(d), row(f)],
                     [_sds((n_rows, d), F32), _sds((n_rows, f), BF16)])(gu, h, wd)
    return _call(name, body, (n_rows // tm,), in_specs + [row(d)],
                 [row(d), pl.BlockSpec((SUBLANES, 128), lambda i: (0, 0)), row(f)],
                 [_sds((n_rows, d), F32), _sds((SUBLANES, 128), F32), _sds((n_rows, f), BF16)])(gu, h, wd, target)


def _ffn_dgu(name, dh, gu, wd, n_rows, after=()):
    f, d = wd.shape
    tm = ROW_TILE

    def body(dh_ref, gu_ref, w_ref, o_ref):
        ds = _dot_nt(0.5 * dh_ref[...], w_ref[...]).astype(BF16)
        half_a = gu_ref[:, :f] * 0.5
        t = jnp.tanh(half_a)
        o_ref[:, :f] = ds * gu_ref[:, f:] * ((1.0 + t + half_a * (1.0 - t * t)) * 0.5)
        o_ref[:, f:] = ds * (half_a + half_a * t)

    return _call(name, body, (n_rows // tm,),
                 [pl.BlockSpec((tm, d), lambda i: (i, 0)), pl.BlockSpec((tm, 2 * f), lambda i: (i, 0)),
                  pl.BlockSpec((f, d), lambda i: (0, 0))],
                 pl.BlockSpec((tm, 2 * f), lambda i: (i, 0)), _sds((n_rows, 2 * f), BF16), after=after)(dh, gu, wd)


def _ffn_dh(name, dgu, h, g, dh, w4, n_rows, n_main=None, after=()):
    nj, d, fc = w4.shape
    tm = _wide_row_tile(n_rows) if n_main is None else ROW_TILE
    n_first = (n_rows if n_main is None else n_main) // tm

    def body(dgu_ref, h_ref, g_ref, dh_ref, w_ref, o_ref, *rest):
        dg_ref = rest[-1]
        i = pl.program_id(0)
        dn = _dot_nt(dgu_ref[:, 0:fc], w_ref[0])
        for j in range(1, nj):
            dn = dn + _dot_nt(dgu_ref[:, j * fc:(j + 1) * fc], w_ref[j])
        dhn, dg = _rms_bwd(h_ref[...], g_ref[...], dn)
        val = dh_ref[...] + dhn
        if n_main is None:
            o_ref[...] = val
        else:
            @pl.when(i < n_first)
            def _():
                o_ref[...] = val

            @pl.when(i >= n_first)
            def _():
                rest[0][...] = val
        _acc_out(dg_ref, dg, i == 0)

    out_specs = [pl.BlockSpec((tm, d), lambda i: (jnp.minimum(i, n_first - 1), 0))]
    out_shape = [_sds((n_first * tm, d), F32)]
    if n_main is not None:
        out_specs.append(pl.BlockSpec((tm, d), lambda i: (jnp.maximum(i - n_first, 0), 0)))
        out_shape.append(_sds((n_rows - n_main, d), F32))
    return _call(name, body, (n_rows // tm,),
                 [pl.BlockSpec((tm, nj * fc), lambda i: (i, 0)), pl.BlockSpec((tm, d), lambda i: (i, 0)),
                  pl.BlockSpec((1, d), lambda i: (0, 0)), pl.BlockSpec((tm, d), lambda i: (i, 0)),
                  pl.BlockSpec((nj, d, fc), lambda i: (0, 0, 0), pipeline_mode=pl.Buffered(1))],
                 out_specs + [pl.BlockSpec((1, d), lambda i: (0, 0))],
                 out_shape + [_sds((1, d), F32)], after=after)(dgu, h, g, dh, w4)


def _contract_tile(n_rows, cap=2816):
    best = ROW_TILE
    for t in range(ROW_TILE, cap + 1, ROW_TILE):
        if n_rows % t == 0:
            best = t
    return best


def _tn(name, operands, in_specs, prologue, nj, ma, nb, n_rows, tk, out_dtype=F32, after=(), side_by_side=False):
    n_k = n_rows // tk
    out_spec = pl.BlockSpec((1, ma, nb), lambda j, k: (j, 0, 0))
    if out_dtype == F32 and not side_by_side:
        def body(*refs):
            o_ref = refs[-1]
            a, b = prologue(pl.program_id(0), *refs[:-1])
            _acc_out(o_ref, _dot_tn(a, b)[None], pl.program_id(1) == 0)

        return _call(name, body, (nj, n_k), in_specs, out_spec, _sds((nj, ma, nb), F32), after=after)(*operands)

    def body_rounded(*refs):
        o_ref, acc_ref = refs[-2:]
        a, b = prologue(pl.program_id(0), *refs[:-2])
        _acc_out(acc_ref, _dot_tn(a, b), pl.program_id(1) == 0)

        @pl.when(pl.program_id(1) == n_k - 1)
        def _():
            o_ref[...] = acc_ref[...].astype(out_dtype).reshape(o_ref.shape)

    if side_by_side:
        out_spec, out_shape = pl.BlockSpec((ma, nb), lambda j, k: (0, j)), _sds((ma, nj * nb), out_dtype)
    else:
        out_shape = _sds((nj, ma, nb), out_dtype)
    return _call(name, body_rounded, (nj, n_k), in_specs, out_spec, out_shape,
                 scratch=[pltpu.VMEM((ma, nb), F32)], after=after)(*operands)


def _ffn_dwgu(name, n, dgu, nj, n_rows):
    d = n.shape[1]
    fc = dgu.shape[1] // nj
    tk = _contract_tile(n_rows, cap=2816)
    return _tn(name, (dgu, n),
               [pl.BlockSpec((tk, fc), lambda j, k: (k, j)), pl.BlockSpec((tk, d), lambda j, k: (k, 0))],
               lambda j, a_ref, b_ref: (a_ref[...], b_ref[...]), nj, fc, d, n_rows, tk, out_dtype=BF16)


def _ffn_dwd(name, s, dh, n_rows):
    f = s.shape[1]
    d = dh.shape[1]
    tk = _contract_tile(n_rows, cap=2048)
    halves = 2 if tk > 1024 else 1
    return _tn(name, (s, dh),
               [pl.BlockSpec((tk, f), lambda j, k: (k, 0)), pl.BlockSpec((tk, d // halves), lambda j, k: (k, j))],
               lambda j, s_ref, dh_ref: (s_ref[...], 0.5 * dh_ref[...]), halves, f, d // halves, n_rows, tk, out_dtype=BF16,
               side_by_side=True)


def _ffn_fwd(tag, h, g, w4, wd, n_rows, target=None):
    gu, n = _ffn_up(tag + "_up", h, g, w4, n_rows)
    *out, s = _ffn_down(tag + "_down", gu, h, wd, n_rows, target)
    return (out[0] if target is None else tuple(out)), (gu, n, s)


def _ffn_bwd(tag, dh_out, h, g, saved, w4, wd, n_rows, n_main=None, after=()):
    gu, n, s = saved
    nj = w4.shape[0]
    f, d = wd.shape
    dgu = _ffn_dgu(tag + "_dgu", dh_out, gu, wd, n_rows, after=after)
    dwd = _ffn_dwd(tag + "_dwd", s, dh_out, n_rows).reshape(N_CHIPS, f // N_CHIPS, d)
    *dh_parts, dg = _ffn_dh(tag + "_dh", dgu, h, g, dh_out, w4, n_rows, n_main)
    dwgu = _ffn_dwgu(tag + "_dwgu", n, dgu, nj, n_rows)
    dh_in = dh_parts[0] if n_main is None else tuple(dh_parts)
    return dh_in, dg, dwgu, dwd


def _ssm_in(name, h, g, w_in, bb, n_rows):
    d, hw = w_in.shape
    nj, uc, xc = bb.shape
    tm = ROW_TILE

    def body(h_ref, g_ref, w_ref, bb_ref, u_ref, bu_ref):
        u = _dot(_rms(h_ref[...], g_ref[...]), w_ref[...])
        u_ref[...] = u
        for j in range(nj):
            bu_ref[:, j * xc:(j + 1) * xc] = _dot(u[:, j * uc:(j + 1) * uc], bb_ref[j]).astype(BF16)

    return _call(name, body, (n_rows // tm,),
                 [pl.BlockSpec((tm, d), lambda i: (i, 0)), pl.BlockSpec((1, d), lambda i: (0, 0)),
                  pl.BlockSpec((d, hw), lambda i: (0, 0)), pl.BlockSpec((nj, uc, xc), lambda i: (0, 0, 0))],
                 [pl.BlockSpec((tm, hw), lambda i: (i, 0)), pl.BlockSpec((tm, nj * xc), lambda i: (i, 0))],
                 [_sds((n_rows, hw), F32), _sds((n_rows, nj * xc), BF16)])(h, g, w_in, bb)


def _cmul_add(xr, xi, ar, ai, sr, si):
    return xr + ar * sr - ai * si, xi + ar * si + ai * sr


def _scan_row_block(n_main_blocks, seq_blocks):
    return lambda b, i: jnp.where(i == 0, n_main_blocks + b, b * seq_blocks + i - 1)


def _scan_fwd(name, bu, tabs, n_ex, seq):
    n_rows, width = bu.shape
    nj = 4
    cw = width // nj
    half = cw // 2
    tq = META_BLOCK
    seq_blocks = seq // tq
    rb = _scan_row_block(n_ex * seq_blocks, seq_blocks)

    def body(bu_ref, tab_ref, x_ref, carry_ref):
        @pl.when(pl.program_id(1) == 0)
        def _():
            carry_ref[...] = jnp.zeros_like(carry_ref)

        for j in range(nj):
            re, im = slice(j * cw, j * cw + half), slice(j * cw + half, (j + 1) * cw)
            ch = slice(j * half, (j + 1) * half)

            def blk(k, c, re=re, im=im, ch=ch):
                t = [tab_ref[n * SUBLANES:(n + 1) * SUBLANES, ch] for n in range(8)]
                r0 = pl.multiple_of(k * SUBLANES, SUBLANES)
                xr = bu_ref[pl.ds(r0, SUBLANES), re].astype(F32)
                xi = bu_ref[pl.ds(r0, SUBLANES), im].astype(F32)
                for s, d in enumerate((1, 2, 4)):
                    xr, xi = _cmul_add(xr, xi, t[2 * s], t[2 * s + 1], pltpu.roll(xr, d, 0), pltpu.roll(xi, d, 0))
                xr, xi = _cmul_add(xr, xi, t[6], t[7], c[0], c[1])
                x_ref[pl.ds(r0, SUBLANES), re] = xr.astype(BF16)
                x_ref[pl.ds(r0, SUBLANES), im] = xi.astype(BF16)
                last = SUBLANES - 1
                return (jnp.broadcast_to(xr[last:last + 1, :], xr.shape), jnp.broadcast_to(xi[last:last + 1, :], xi.shape))

            c = lax.fori_loop(0, tq // SUBLANES, blk, (carry_ref[0, :, ch], carry_ref[1, :, ch]), unroll=2)
            carry_ref[0, :, ch] = c[0]
            carry_ref[1, :, ch] = c[1]

    return _call(name, body, (n_ex, seq_blocks + 1),
                 [pl.BlockSpec((tq, width), lambda b, i: (rb(b, i), 0)), pl.BlockSpec((8 * SUBLANES, nj * half), lambda b, i: (0, 0))],
                 pl.BlockSpec((tq, width), lambda b, i: (rb(b, i), 0)), _sds((n_rows, width), BF16),
                 scratch=[pltpu.VMEM((2, SUBLANES, nj * half), F32)])(bu, tabs)


def _scan_bwd(name, gx, x, tabs, n_ex, seq, after=()):
    n_rows, width = gx.shape
    nj = 4
    cw = width // nj
    half = cw // 2
    tq = META_BLOCK
    seq_blocks = seq // tq
    n_steps = seq_blocks + 1
    rb = _scan_row_block(n_ex * seq_blocks, seq_blocks)
    rbr = lambda b, i: rb(b, n_steps - 1 - i)

    def body(gx_ref, x_ref, tab_ref, g_ref, da_ref, carry_ref):
        @pl.when(pl.program_id(1) == 0)
        def _():
            carry_ref[...] = jnp.zeros_like(carry_ref)
            da_ref[...] = jnp.zeros_like(da_ref)
        row = lax.broadcasted_iota(jnp.int32, (SUBLANES, half), 0)
        n_blk = tq // SUBLANES

        for j in range(nj):
            re, im = slice(j * cw, j * cw + half), slice(j * cw + half, (j + 1) * cw)
            ch = slice(j * half, (j + 1) * half)

            def blk(kk, st, re=re, im=im, ch=ch):
                t = [tab_ref[n * SUBLANES:(n + 1) * SUBLANES, ch] for n in range(8)]
                cr, ci, dar, dai = st
                r0 = pl.multiple_of((n_blk - 1 - kk) * SUBLANES, SUBLANES)
                gr = gx_ref[pl.ds(r0, SUBLANES), re].astype(F32)
                gi = gx_ref[pl.ds(r0, SUBLANES), im].astype(F32)
                for s, d in enumerate((1, 2, 4)):
                    gr, gi = _cmul_add(gr, gi, t[2 * s], t[2 * s + 1],
                                       pltpu.roll(gr, SUBLANES - d, 0), pltpu.roll(gi, SUBLANES - d, 0))
                gr, gi = _cmul_add(gr, gi, t[6], t[7], cr, ci)
                g_ref[pl.ds(r0, SUBLANES), re] = gr.astype(BF16)
                g_ref[pl.ds(r0, SUBLANES), im] = gi.astype(BF16)
                hr = jnp.where(row == SUBLANES - 1, cr, pltpu.roll(gr, SUBLANES - 1, 0))
                hi = jnp.where(row == SUBLANES - 1, ci, pltpu.roll(gi, SUBLANES - 1, 0))
                xr = x_ref[pl.ds(r0, SUBLANES), re].astype(F32)
                xi = x_ref[pl.ds(r0, SUBLANES), im].astype(F32)
                dar = dar + xr * hr + xi * hi
                dai = dai + xr * hi - xi * hr
                return (jnp.broadcast_to(gr[0:1, :], gr.shape), jnp.broadcast_to(gi[0:1, :], gi.shape), dar, dai)

            st = lax.fori_loop(0, n_blk, blk, (carry_ref[0, :, ch], carry_ref[1, :, ch], da_ref[0, :, re], da_ref[0, :, im]),
                               unroll=2)
            carry_ref[0, :, ch] = st[0]
            carry_ref[1, :, ch] = st[1]
            da_ref[0, :, re] = st[2]
            da_ref[0, :, im] = st[3]

    return _call(name, body, (n_ex, n_steps),
                 [pl.BlockSpec((tq, width), lambda b, i: (rbr(b, i), 0)), pl.BlockSpec((tq, width), lambda b, i: (rbr(b, i), 0)),
                  pl.BlockSpec((8 * SUBLANES, nj * half), lambda b, i: (0, 0))],
                 [pl.BlockSpec((tq, width), lambda b, i: (rbr(b, i), 0)), pl.BlockSpec((1, SUBLANES, width), lambda b, i: (b, 0, 0))],
                 [_sds((n_rows, width), BF16), _sds((n_ex, SUBLANES, width), F32)],
                 scratch=[pltpu.VMEM((2, SUBLANES, nj * half), F32)], after=after)(gx, x, tabs)


def _ssm_z(gy, wout_ref, nj):
    return jnp.concatenate([_dot(gy, wout_ref[j]) for j in range(nj)], axis=1)


def _ssm_out(name, x, u, dskip, cb, wout4, h, n_rows):
    nj, xc, uc = cb.shape
    no, hw, oc = wout4.shape
    d = h.shape[1]
    tm = ROW_TILE

    def body(x_ref, u_ref, ds_ref, cb_ref, w_ref, h_ref, o_ref, y_ref):
        y = jnp.concatenate([_dot(x_ref[:, j * xc:(j + 1) * xc], cb_ref[j]) for j in range(nj)], axis=1)
        y = y + ds_ref[...] * u_ref[...]
        y_ref[...] = y
        z = _ssm_z(_gelu(y), w_ref, no)
        o_ref[...] = h_ref[...] + z[:, :d] * _sigmoid(z[:, d:])

    return _call(name, body, (n_rows // tm,),
                 [pl.BlockSpec((tm, nj * xc), lambda i: (i, 0)), pl.BlockSpec((tm, hw), lambda i: (i, 0)),
                  pl.BlockSpec((1, hw), lambda i: (0, 0)), pl.BlockSpec((nj, xc, uc), lambda i: (0, 0, 0)),
                  pl.BlockSpec((no, hw, oc), lambda i: (0, 0, 0)), pl.BlockSpec((tm, d), lambda i: (i, 0))],
                 [pl.BlockSpec((tm, d), lambda i: (i, 0)), pl.BlockSpec((tm, hw), lambda i: (i, 0))],
                 [_sds((n_rows, d), F32), _sds((n_rows, hw), F32)])(x, u, dskip, cb, wout4, h)


def _ssm_out_bwd(name, dh, y, u, cb, wout4, n_rows, after=()):
    nj, xc, uc = cb.shape
    no, hw, oc = wout4.shape
    d = dh.shape[1]
    tm = ROW_TILE

    def body(dh_ref, y_ref, u_ref, cb_ref, w_ref, dy_ref, dz_ref, gx_ref, dd_ref):
        y = y_ref[...]
        z = _ssm_z(_gelu(y), w_ref, no)
        za = z[:, :d]
        sg = _sigmoid(z[:, d:])
        dmix = dh_ref[...]
        dz = jnp.concatenate([dmix * sg, dmix * za * sg * (1.0 - sg)], axis=1).astype(BF16)
        dz_ref[...] = dz
        dgy = _dot_nt(dz[:, 0:oc], w_ref[0])
        for j in range(1, no):
            dgy = dgy + _dot_nt(dz[:, j * oc:(j + 1) * oc], w_ref[j])
        dy = dgy * _gelu_grad(y)
        dy_ref[...] = dy
        _acc_out(dd_ref, jnp.sum(dy * u_ref[...], axis=0, keepdims=True), pl.program_id(0) == 0)
        for j in range(nj):
            gx_ref[:, j * xc:(j + 1) * xc] = _dot_nt(dy[:, j * uc:(j + 1) * uc], cb_ref[j]).astype(BF16)

    return _call(name, body, (n_rows // tm,),
                 [pl.BlockSpec((tm, d), lambda i: (i, 0)), pl.BlockSpec((tm, hw), lambda i: (i, 0)),
                  pl.BlockSpec((tm, hw), lambda i: (i, 0)), pl.BlockSpec((nj, xc, uc), lambda i: (0, 0, 0)),
                  pl.BlockSpec((no, hw, oc), lambda i: (0, 0, 0))],
                 [pl.BlockSpec((tm, hw), lambda i: (i, 0)), pl.BlockSpec((tm, no * oc), lambda i: (i, 0)),
                  pl.BlockSpec((tm, nj * xc), lambda i: (i, 0)), pl.BlockSpec((1, hw), lambda i: (0, 0))],
                 [_sds((n_rows, hw), F32), _sds((n_rows, no * oc), BF16), _sds((n_rows, nj * xc), BF16),
                  _sds((1, hw), F32)], after=after)(dh, y, u, cb, wout4)


def _ssm_in_bwd(name, gbu, dy, dskip, bb, w_in, h, g, dh, n_rows):
    nj, uc, xc = bb.shape
    d, hw = w_in.shape
    tm = ROW_TILE

    def body(gb_ref, dy_ref, ds_ref, bb_ref, w_ref, h_ref, g_ref, dh_ref, du_ref, o_ref, dg_ref):
        du = jnp.concatenate([_dot_nt(gb_ref[:, j * xc:(j + 1) * xc], bb_ref[j]) for j in range(nj)], axis=1)
        du = du + dy_ref[...] * ds_ref[...]
        du_ref[...] = du.astype(BF16)
        dhn, dg = _rms_bwd(h_ref[...], g_ref[...], _dot_nt(du, w_ref[...]))
        o_ref[...] = dh_ref[...] + dhn
        _acc_out(dg_ref, dg, pl.program_id(0) == 0)

    return _call(name, body, (n_rows // tm,),
                 [pl.BlockSpec((tm, nj * xc), lambda i: (i, 0)), pl.BlockSpec((tm, hw), lambda i: (i, 0)),
                  pl.BlockSpec((1, hw), lambda i: (0, 0)), pl.BlockSpec((nj, uc, xc), lambda i: (0, 0, 0)),
                  pl.BlockSpec((d, hw), lambda i: (0, 0)), pl.BlockSpec((tm, d), lambda i: (i, 0)),
                  pl.BlockSpec((1, d), lambda i: (0, 0)), pl.BlockSpec((tm, d), lambda i: (i, 0))],
                 [pl.BlockSpec((tm, hw), lambda i: (i, 0)), pl.BlockSpec((tm, d), lambda i: (i, 0)),
                  pl.BlockSpec((1, d), lambda i: (0, 0))],
                 [_sds((n_rows, hw), BF16), _sds((n_rows, d), F32), _sds((1, d), F32)])(gbu, dy, dskip, bb, w_in, h, g, dh)


def _discretize(lam_re, lam_im, log_step, b_re, b_im):
    step = jnp.exp(log_step)[:, None]
    mag = jnp.exp(lam_re * step)
    ar = mag * jnp.cos(lam_im * step)
    ai = mag * jnp.sin(lam_im * step)
    den = lam_re * lam_re + lam_im * lam_im
    nr, ni = ar - 1.0, ai
    cr = (nr * lam_re + ni * lam_im) / den
    ci = (ni * lam_re - nr * lam_im) / den
    bbar_r = cr[..., None] * b_re - ci[..., None] * b_im
    bbar_i = cr[..., None] * b_im + ci[..., None] * b_re
    return ar, ai, bbar_r, bbar_i


def _ssm_mats(lam_re, lam_im, log_step, b_re, b_im, c_re, c_im):
    n_g, n_p, n_c = b_re.shape
    gpc = n_g // 4
    ar, ai, bbar_r, bbar_i = _discretize(lam_re, lam_im, log_step, b_re, b_im)
    eye = jnp.eye(gpc, dtype=F32)

    def in_map(bbar):
        return jnp.einsum('jgpc,gh->jgchp', bbar.reshape(4, gpc, n_p, n_c), eye).reshape(4, gpc * n_c, gpc * n_p)

    def out_map(c):
        return jnp.einsum('jgcp,gh->jgphc', c.reshape(4, gpc, n_c, n_p), eye).reshape(4, gpc * n_p, gpc * n_c)

    bb = jnp.concatenate([in_map(bbar_r), in_map(bbar_i)], axis=2)
    cb = jnp.concatenate([out_map(c_re), -out_map(c_im)], axis=1)
    return bb, cb, ar.reshape(-1), ai.reshape(-1)


def _chunked(v, half):
    return v.reshape(v.shape[:-1] + (4, half))


def _scan_tables(ar, ai, reverse):
    if reverse:
        ai = -ai
    pr, pi = [ar], [ai]
    for _ in range(SUBLANES - 1):
        pr, pi = pr + [pr[-1] * ar - pi[-1] * ai], pi + [pr[-1] * ai + pi[-1] * ar]
    row = jnp.arange(SUBLANES)[:, None]
    tabs = []
    for d in (1, 2, 4):
        keep = (row <= SUBLANES - 1 - d) if reverse else (row >= d)
        tabs += [jnp.where(keep, pr[d - 1][None, :], 0.0), jnp.where(keep, pi[d - 1][None, :], 0.0)]
    order = list(range(SUBLANES))[::-1] if reverse else list(range(SUBLANES))
    tabs += [jnp.stack([pr[k] for k in order]), jnp.stack([pi[k] for k in order])]
    return jnp.concatenate(tabs, axis=0)


def _kv_proj(name, h, g, w_kv, k_gain_t, cos2, sin2, n_rows, n_kv, hd):
    d, kvw = w_kv.shape
    kw = n_kv * hd
    tm = ROW_TILE

    sel, sel_t = _head_selectors(n_kv, hd)

    def body(h_ref, g_ref, w_ref, kg_ref, c_ref, s_ref, e_ref, et_ref, raw_ref, k_ref, v_ref):
        raw = _dot(_rms(h_ref[...], g_ref[...]), w_ref[...])
        raw_ref[...] = raw
        k_ref[...] = _head_prep(raw[:, :kw], kg_ref[...], c_ref[...], s_ref[...], e_ref[...], et_ref[...], hd).astype(BF16)
        v_ref[...] = raw[:, kw:].astype(BF16)

    return _call(name, body, (n_rows // tm,),
                 [pl.BlockSpec((tm, d), lambda i: (i, 0)), pl.BlockSpec((1, d), lambda i: (0, 0)),
                  pl.BlockSpec((d, kvw), lambda i: (0, 0)), pl.BlockSpec((1, kw), lambda i: (0, 0)),
                  pl.BlockSpec((tm, 2 * hd), lambda i: (i, 0)), pl.BlockSpec((tm, 2 * hd), lambda i: (i, 0)),
                  pl.BlockSpec(sel.shape, lambda i: (0, 0)), pl.BlockSpec(sel_t.shape, lambda i: (0, 0))],
                 [pl.BlockSpec((tm, kvw), lambda i: (i, 0)), pl.BlockSpec((tm, kw), lambda i: (i, 0)),
                  pl.BlockSpec((tm, kw), lambda i: (i, 0))],
                 [_sds((n_rows, kvw), F32), _sds((n_rows, kw), BF16), _sds((n_rows, kw), BF16)])(
                     h, g, w_kv, k_gain_t, cos2, sin2, sel, sel_t)


def _q_proj(name, h, g, w_q, q_gain_t, cos2, sin2, n_rows, n_q, hd):
    d, qw = w_q.shape
    tm = ROW_TILE

    sel, sel_t = _head_selectors(n_q, hd)

    def body(h_ref, g_ref, w_ref, qg_ref, c_ref, s_ref, e_ref, et_ref, raw_ref, q_ref):
        raw = _dot(_rms(h_ref[...], g_ref[...]), w_ref[...])
        raw_ref[...] = raw
        q_ref[...] = _head_prep(raw, qg_ref[...], c_ref[...], s_ref[...], e_ref[...], et_ref[...], hd).astype(BF16)

    return _call(name, body, (n_rows // tm,),
                 [pl.BlockSpec((tm, d), lambda i: (i, 0)), pl.BlockSpec((1, d), lambda i: (0, 0)),
                  pl.BlockSpec((d, qw), lambda i: (0, 0)), pl.BlockSpec((1, qw), lambda i: (0, 0)),
                  pl.BlockSpec((tm, 2 * hd), lambda i: (i, 0)), pl.BlockSpec((tm, 2 * hd), lambda i: (i, 0)),
                  pl.BlockSpec(sel.shape, lambda i: (0, 0)), pl.BlockSpec(sel_t.shape, lambda i: (0, 0))],
                 [pl.BlockSpec((tm, qw), lambda i: (i, 0)), pl.BlockSpec((tm, qw), lambda i: (i, 0))],
                 [_sds((n_rows, qw), F32), _sds((n_rows, qw), BF16)])(h, g, w_q, q_gain_t, cos2, sin2, sel, sel_t)


def _attn_specs(seq, n_ex, n_meta, kw):
    nb = seq // WINDOW
    meta_blk = lambda b: (n_ex * seq + META_BLOCK * b + META_BLOCK - n_meta) // n_meta
    return [pl.BlockSpec((WINDOW, kw), lambda b, n: (b * nb + jnp.maximum(n - 1, 0), 0)),
            pl.BlockSpec((WINDOW, kw), lambda b, n: (b * nb + n, 0)),
            pl.BlockSpec((n_meta, kw), lambda b, n: (meta_blk(b), 0))]


def _attn_bias(qpk, n_keys):
    rows = qpk * WINDOW
    qi = jnp.arange(rows)[:, None] & (WINDOW - 1)
    kj = jnp.arange(n_keys)[None, :]
    rel = qi + WINDOW - kj
    band = (rel >= 0) & (rel < WINDOW)
    meta = kj >= 2 * WINDOW
    first = (band & (kj >= WINDOW)) | meta
    return jnp.where(jnp.stack([first, band | meta]), 0.0, NEG_INF).astype(F32)


def _stack_heads(ref, h, qpk, hd, dtype=None):
    parts = [ref[:, (h * qpk + gq) * hd:(h * qpk + gq + 1) * hd] for gq in range(qpk)]
    out = jnp.concatenate(parts, axis=0)
    return out if dtype is None else out.astype(dtype)


def _col(tile, c):
    lane = lax.broadcasted_iota(jnp.int32, tile.shape, 1)
    return jnp.sum(jnp.where(lane == c, tile, 0.0), axis=-1, keepdims=True)


def _put_col(col, c, n):
    lane = lax.broadcasted_iota(jnp.int32, (col.shape[0], n), 1)
    return jnp.where(lane == c, col, 0.0)


def _stack_cols(tile, h, qpk):
    return jnp.concatenate([_col(tile, h * qpk + gq) for gq in range(qpk)], axis=0)


def _sink_col(sinks, h, qpk):
    return jnp.concatenate([jnp.broadcast_to(_col(sinks, h * qpk + gq), (WINDOW, 1)) for gq in range(qpk)], axis=0)


def _attn_fwd(name, q, k, v, sinks, n_ex, seq, n_meta, n_kv, qpk, hd):
    nb = seq // WINDOW
    n_q = n_kv * qpk
    kw = n_kv * hd
    qw = n_q * hd
    n_keys = 2 * WINDOW + n_meta
    bias = _attn_bias(qpk, n_keys)

    def body(q_ref, kp_ref, kc_ref, km_ref, vp_ref, vc_ref, vm_ref, sk_ref, bias_ref, o_ref, lse_ref):
        sinks_v = sk_ref[...]
        o_parts = []
        lse_all = jnp.zeros((WINDOW, n_q), F32)
        for h in range(n_kv):
            hs = slice(h * hd, (h + 1) * hd)
            kb = jnp.concatenate([kp_ref[:, hs], kc_ref[:, hs], km_ref[:, hs]], axis=0)
            vb = jnp.concatenate([vp_ref[:, hs], vc_ref[:, hs], vm_ref[:, hs]], axis=0)
            for gq in range(qpk):
                c = h * qpk + gq
                s = _dot_nt(q_ref[:, c * hd:(c + 1) * hd], kb) + bias_ref[0, 0:WINDOW, :]
                skc = _col(sinks_v, c)
                m = jnp.maximum(jnp.max(s, axis=-1, keepdims=True), skc)
                p = jnp.exp(s - m)
                den = jnp.sum(p, axis=-1, keepdims=True) + jnp.exp(skc - m)
                o_parts.append(_dot(p, vb) / den)
                lse_all = lse_all + _put_col(m + jnp.log(den), c, n_q)
        o_ref[...] = jnp.concatenate(o_parts, axis=1).astype(BF16)
        lse_ref[...] = lse_all

    qspec = pl.BlockSpec((WINDOW, qw), lambda b, n: (b * nb + n, 0))
    return _call(name, body, (n_ex, nb),
                 [qspec] + _attn_specs(seq, n_ex, n_meta, kw) + _attn_specs(seq, n_ex, n_meta, kw)
                 + [pl.BlockSpec((1, n_q), lambda b, n: (0, 0)),
                    pl.BlockSpec((1,) + bias.shape[1:], lambda b, n: (jnp.minimum(n, 1), 0, 0))],
                 [qspec, pl.BlockSpec((WINDOW, n_q), lambda b, n: (b * nb + n, 0))],
                 [_sds((n_ex * seq, qw), BF16), _sds((n_ex * seq, n_q), F32)])(q, k, k, k, v, v, v, sinks, bias)


def _attn_bwd(name, q, k, v, sinks, o, lse, do, n_ex, seq, n_meta, n_kv, qpk, hd):
    nb = seq // WINDOW
    n_q = n_kv * qpk
    kw = n_kv * hd
    qw = n_q * hd
    n_keys = 2 * WINDOW + n_meta
    bias = _attn_bias(qpk, n_keys)

    def body(q_ref, kp_ref, kc_ref, km_ref, vp_ref, vc_ref, vm_ref, sk_ref, o_ref, lse_ref, do_ref, bias_ref,
             dq_ref, dk_ref, dv_ref, dkm_ref, dvm_ref, dsk_ref):
        n = pl.program_id(1)

        @pl.when(n == 0)
        def _():
            dk_ref[...] = jnp.zeros_like(dk_ref)
            dv_ref[...] = jnp.zeros_like(dv_ref)
            dkm_ref[...] = jnp.zeros_like(dkm_ref)
            dvm_ref[...] = jnp.zeros_like(dvm_ref)

        @pl.when((n == 0) & (pl.program_id(0) == 0))
        def _():
            dsk_ref[...] = jnp.zeros_like(dsk_ref)

        sinks_v = sk_ref[...]
        lse_v = lse_ref[...]
        grp = 2 if qpk % 2 == 0 else 1
        dq_parts, dk_parts, dv_parts = [], [], []
        dsk = jnp.zeros((1, n_q), F32)
        for h in range(n_kv):
            hs = slice(h * hd, (h + 1) * hd)
            kb = jnp.concatenate([kp_ref[:, hs], kc_ref[:, hs], km_ref[:, hs]], axis=0)
            vb = jnp.concatenate([vp_ref[:, hs], vc_ref[:, hs], vm_ref[:, hs]], axis=0)
            dk_h = dv_h = None
            for c0 in range(h * qpk, (h + 1) * qpk, grp):
                heads = range(c0, c0 + grp)
                stack = lambda ref: jnp.concatenate([ref[:, c * hd:(c + 1) * hd] for c in heads], axis=0)
                qs, dos = stack(q_ref), stack(do_ref)
                delta = jnp.sum(dos.astype(F32) * stack(o_ref).astype(F32), axis=-1, keepdims=True)
                lse_c = jnp.concatenate([_col(lse_v, c) for c in heads], axis=0)
                skc = jnp.concatenate([jnp.broadcast_to(_col(sinks_v, c), (WINDOW, 1)) for c in heads], axis=0)
                p = jnp.exp(_dot_nt(qs, kb) + bias_ref[0, 0:grp * WINDOW, :] - lse_c)
                ds = p * (_dot_nt(dos, vb) - delta)
                dqs = _dot(ds, kb)
                dk_g, dv_g = _dot_tn(ds, qs), _dot_tn(p, dos)
                dk_h = dk_g if dk_h is None else dk_h + dk_g
                dv_h = dv_g if dv_h is None else dv_h + dv_g
                dsink = -jnp.exp(skc - lse_c) * delta
                for i, c in enumerate(heads):
                    dq_parts.append(dqs[i * WINDOW:(i + 1) * WINDOW])
                    dsk = dsk + _put_col(jnp.sum(dsink[i * WINDOW:(i + 1) * WINDOW], axis=0, keepdims=True), c, n_q)
            dk_parts.append(dk_h)
            dv_parts.append(dv_h)
        dq_ref[...] = jnp.concatenate(dq_parts, axis=1).astype(BF16)
        dsk_ref[...] += dsk
        dkb = jnp.concatenate(dk_parts, axis=1)
        dvb = jnp.concatenate(dv_parts, axis=1)
        prev = pl.ds(pl.multiple_of(jnp.maximum(n - 1, 0) * WINDOW, WINDOW), WINDOW)
        cur = pl.ds(pl.multiple_of(n * WINDOW, WINDOW), WINDOW)
        dk_ref[prev, :] += dkb[0:WINDOW]
        dv_ref[prev, :] += dvb[0:WINDOW]
        dk_ref[cur, :] += dkb[WINDOW:2 * WINDOW]
        dv_ref[cur, :] += dvb[WINDOW:2 * WINDOW]
        dkm_ref[...] += dkb[2 * WINDOW:]
        dvm_ref[...] += dvb[2 * WINDOW:]

    qspec = pl.BlockSpec((WINDOW, qw), lambda b, n: (b * nb + n, 0))
    exspec = pl.BlockSpec((seq, kw), lambda b, n: (b, 0))
    mspec = pl.BlockSpec((n_meta, kw), lambda b, n: (b, 0))
    return _call(name, body, (n_ex, nb),
                 [qspec] + _attn_specs(seq, n_ex, n_meta, kw) + _attn_specs(seq, n_ex, n_meta, kw)
                 + [pl.BlockSpec((1, n_q), lambda b, n: (0, 0)), qspec,
                    pl.BlockSpec((WINDOW, n_q), lambda b, n: (b * nb + n, 0)), qspec,
                    pl.BlockSpec((1,) + bias.shape[1:], lambda b, n: (jnp.minimum(n, 1), 0, 0))],
                 [qspec, exspec, exspec, mspec, mspec, pl.BlockSpec((1, n_q), lambda b, n: (0, 0))],
                 [_sds((n_ex * seq, qw), BF16), _sds((n_ex * seq, kw), F32), _sds((n_ex * seq, kw), F32),
                  _sds((n_ex * n_meta, kw), F32), _sds((n_ex * n_meta, kw), F32), _sds((1, n_q), F32)])(
                      q, k, k, k, v, v, v, sinks, o, lse, do, bias)


def _attn_out(name, o, h, w_o, n_rows):
    qw, d = w_o.shape
    tm = ROW_TILE

    def body(o_ref, h_ref, w_ref, out_ref):
        out_ref[...] = h_ref[...] + _dot(o_ref[...], w_ref[...])

    return _call(name, body, (n_rows // tm,),
                 [pl.BlockSpec((tm, qw), lambda i: (i, 0)), pl.BlockSpec((tm, d), lambda i: (i, 0)),
                  pl.BlockSpec((qw, d), lambda i: (0, 0))],
                 pl.BlockSpec((tm, d), lambda i: (i, 0)), _sds((n_rows, d), F32))(o, h, w_o)


def _attn_out_bwd(name, dh, w_o, n_rows):
    qw, d = w_o.shape
    tm = ROW_TILE

    def body(dh_ref, w_ref, do_ref):
        do_ref[...] = _dot_nt(dh_ref[...], w_ref[...]).astype(BF16)

    return _call(name, body, (n_rows // tm,),
                 [pl.BlockSpec((tm, d), lambda i: (i, 0)), pl.BlockSpec((qw, d), lambda i: (0, 0))],
                 pl.BlockSpec((tm, qw), lambda i: (i, 0)), _sds((n_rows, qw), BF16))(dh, w_o)


def _q_bwd(name, dq, qraw, q_gain_t, cos2, sin2, w_q, h, g, dh, n_rows, n_q, hd):
    d, qw = w_q.shape
    tm = ROW_TILE

    sel, sel_t = _head_selectors(n_q, hd)

    def body(dq_ref, raw_ref, qg_ref, c_ref, s_ref, e_ref, et_ref, w_ref, h_ref, g_ref, dh_ref, draw_ref, o_ref, dqg_ref, dg_ref):
        dx, dgain = _head_prep_bwd(raw_ref[...], qg_ref[...], c_ref[...], s_ref[...], e_ref[...], et_ref[...],
                                   dq_ref[...].astype(F32), hd)
        draw = dx.astype(BF16)
        draw_ref[...] = draw
        dhn, dg = _rms_bwd(h_ref[...], g_ref[...], _dot_nt(draw, w_ref[...]))
        o_ref[...] = dh_ref[...] + dhn
        first = pl.program_id(0) == 0
        _acc_out(dqg_ref, dgain, first)
        _acc_out(dg_ref, dg, first)

    row = lambda w: pl.BlockSpec((tm, w), lambda i: (i, 0))
    one = lambda w: pl.BlockSpec((1, w), lambda i: (0, 0))
    return _call(name, body, (n_rows // tm,),
                 [row(qw), row(qw), one(qw), row(2 * hd), row(2 * hd), pl.BlockSpec(sel.shape, lambda i: (0, 0)),
                  pl.BlockSpec(sel_t.shape, lambda i: (0, 0)), pl.BlockSpec((d, qw), lambda i: (0, 0)), row(d), one(d), row(d)],
                 [row(qw), row(d), one(qw), one(d)],
                 [_sds((n_rows, qw), BF16), _sds((n_rows, d), F32), _sds((1, qw), F32), _sds((1, d), F32)])(
                     dq, qraw, q_gain_t, cos2, sin2, sel, sel_t, w_q, h, g, dh)


def _kv_bwd(name, dk, dv, kvraw, k_gain_t, cos2, sin2, w_kv, h, g, dh_main, n_rows, n_main, n_kv, hd, after=()):
    d, kvw = w_kv.shape
    kw = n_kv * hd
    tm = ROW_TILE
    n_main_tiles = n_main // tm

    sel, sel_t = _head_selectors(n_kv, hd)

    def body(dk_ref, dv_ref, raw_ref, kg_ref, c_ref, s_ref, e_ref, et_ref, w_ref, h_ref, g_ref, dh_ref, draw_ref, o_ref, dkg_ref,
             dg_ref):
        i = pl.program_id(0)
        dx, dgain = _head_prep_bwd(raw_ref[:, :kw], kg_ref[...], c_ref[...], s_ref[...], e_ref[...], et_ref[...], dk_ref[...], hd)
        draw = jnp.concatenate([dx, dv_ref[...]], axis=1).astype(BF16)
        draw_ref[...] = draw
        dhn, dg = _rms_bwd(h_ref[...], g_ref[...], _dot_nt(draw, w_ref[...]))
        o_ref[...] = jnp.where(i < n_main_tiles, dh_ref[...], 0.0) + dhn
        _acc_out(dkg_ref, dgain, i == 0)
        _acc_out(dg_ref, dg, i == 0)

    row = lambda w: pl.BlockSpec((tm, w), lambda i: (i, 0))
    one = lambda w: pl.BlockSpec((1, w), lambda i: (0, 0))
    return _call(name, body, (n_rows // tm,),
                 [row(kw), row(kw), row(kvw), one(kw), row(2 * hd), row(2 * hd), pl.BlockSpec(sel.shape, lambda i: (0, 0)),
                  pl.BlockSpec(sel_t.shape, lambda i: (0, 0)), pl.BlockSpec((d, kvw), lambda i: (0, 0)), row(d),
                  one(d), pl.BlockSpec((tm, d), lambda i: (jnp.minimum(i, n_main_tiles - 1), 0))],
                 [row(kvw), row(d), one(kw), one(d)],
                 [_sds((n_rows, kvw), BF16), _sds((n_rows, d), F32), _sds((1, kw), F32), _sds((1, d), F32)], after=after)(
                     dk, dv, kvraw, k_gain_t, cos2, sin2, sel, sel_t, w_kv, h, g, dh_main)


def _tn_rms(name, h, g, b, n_rows, out_dtype=F32):
    d = h.shape[1]
    nb = b.shape[1]
    tk = _contract_tile(n_rows)
    return _tn(name, (h, g, b),
               [pl.BlockSpec((tk, d), lambda j, k: (k, 0)), pl.BlockSpec((1, d), lambda j, k: (0, 0)),
                pl.BlockSpec((tk, nb), lambda j, k: (k, 0))],
               lambda j, h_ref, g_ref, b_ref: (_rms(h_ref[...], g_ref[...]), b_ref[...]), 1, d, nb, n_rows, tk, out_dtype=out_dtype)


def _tn_plain(name, a, b, nj, a_cols, b_cols, n_rows, a_fn=None, out_dtype=F32, after=()):
    tk = _contract_tile(n_rows)
    fa = (lambda v: v) if a_fn is None else a_fn
    a_map = (lambda j, k: (k, j)) if a.shape[1] != a_cols else (lambda j, k: (k, 0))
    b_map = (lambda j, k: (k, j)) if b.shape[1] != b_cols else (lambda j, k: (k, 0))
    return _tn(name, (a, b), [pl.BlockSpec((tk, a_cols), a_map), pl.BlockSpec((tk, b_cols), b_map)],
               lambda j, a_ref, b_ref: (fa(a_ref[...]), b_ref[...]), nj, a_cols, b_cols, n_rows, tk, out_dtype=out_dtype,
               after=after)


def _cast_layer(name, a, layer):
    _, r, c = a.shape
    tr = _row_tile(r, 256)

    def body(a_ref, o_ref):
        o_ref[...] = a_ref[0].astype(BF16)

    return _call(name, body, (r // tr,), [pl.BlockSpec((1, tr, c), lambda i: (layer, i, 0))],
                 pl.BlockSpec((tr, c), lambda i: (i, 0)), _sds((r, c), BF16))(a)


def _adamw_math(w, g, m, v):
    c1 = 1.0 - ADAM_B1 ** ADAM_STEP
    c2 = 1.0 - ADAM_B2 ** ADAM_STEP
    nm = ADAM_B1 * m + (1.0 - ADAM_B1) * g
    nv = ADAM_B2 * v + (1.0 - ADAM_B2) * (g * g)
    return -ADAM_LR * ((nm / c1) / (jnp.sqrt(nv / c2) + ADAM_EPS) + ADAM_WD * w), nm, nv


def _adamw(name, w, g, m, v, after=()):
    rows, cols = w.shape
    tr = 128 if rows % 128 == 0 else rows

    def body(w_ref, g_ref, m_ref, v_ref, d_ref, nm_ref, nv_ref):
        d_ref[...], nm_ref[...], nv_ref[...] = _adamw_math(w_ref[...], g_ref[...], m_ref[...], v_ref[...])

    spec = pl.BlockSpec((tr, cols), lambda i: (i, 0))
    return _call(name, body, (rows // tr,), [spec] * 4, [spec] * 3, [_sds((rows, cols), F32)] * 3, after=after)(w, g, m, v)


def _adamw_from_halves(name, w, m, v, sources, half_index, transposed, after=()):
    n_layers, r, c = w.shape
    lanes = 1024
    after = tuple(after)
    if transposed:
        rows_half, tr = c // 2, 128
        grid = (n_layers, r // tr)
        w_spec = pl.BlockSpec((1, tr, c), lambda l, i, s: (l, i, 0))
        g_spec = lambda off: pl.BlockSpec((rows_half, tr), lambda l, i, s: (off // rows_half, i))
    else:
        rows_half = r // 2
        grid = (n_layers, 2)
        w_spec = pl.BlockSpec((1, rows_half, c), lambda l, k, s: (l, k, 0))
        g_spec = lambda off: pl.BlockSpec((rows_half, lanes), lambda l, k, s: (off // rows_half, 0))

    def body(s_ref, w_ref, m_ref, v_ref, t0_ref, o0_ref, t1_ref, o1_ref, *rest):
        g_ref, d_ref, nm_ref, nv_ref = rest[len(after):]
        layer, k, mine = pl.program_id(0), pl.program_id(1), s_ref[0]
        tot = jnp.where(layer == 0, t0_ref[...], t1_ref[...])
        oth = jnp.where(layer == 0, o0_ref[...], o1_ref[...])
        if transposed:
            g = jnp.concatenate([jnp.where(mine == 0, tot, oth), jnp.where(mine == 0, oth, tot)], axis=0).T
        else:
            g = jnp.where(k == mine, tot, oth)
        g_ref[0] = g
        d_ref[0], nm_ref[0], nv_ref[0] = _adamw_math(w_ref[0], g, m_ref[0], v_ref[0])

    (t0, o0, off0), (t1, o1, off1) = sources
    grid_spec = pltpu.PrefetchScalarGridSpec(
        num_scalar_prefetch=1, grid=grid,
        in_specs=[w_spec] * 3 + [g_spec(off0), g_spec(off0), g_spec(off1), g_spec(off1)] + [_ANY] * len(after),
        out_specs=[w_spec] * 4)
    return pl.pallas_call(
        body, name=name, grid_spec=grid_spec, out_shape=[_sds(w.shape, F32)] * 4,
        compiler_params=pltpu.CompilerParams(dimension_semantics=("arbitrary", "arbitrary"),
                                             vmem_limit_bytes=V7X_VMEM_LIMIT))(half_index, w, m, v, t0, o0, t1, o1, *after)


def _position():
    return lax.axis_index("x"), lax.axis_index("y"), lax.axis_index("c")


def _other_chips(x, y):
    return [(1 - x, y), (x, 1 - y), (1 - x, 1 - y)]


def _peers_chips(x, y, c):
    return [(cx, cy, c) for cx, cy in _other_chips(x, y)]


def _peers_sibling(x, y, c):
    return [(x, y, 1 - c)]


def _peers_chips_and_sibling(x, y, c):
    return _peers_chips(x, y, c) + _peers_sibling(x, y, c)


def _comm_call(name, body, n_in, out_shape, scratch, sequencer=None):
    if sequencer is None:
        return pl.pallas_call(
            body, name=name, in_specs=[_HBM] * n_in, out_specs=[_HBM] * len(out_shape), out_shape=out_shape,
            scratch_shapes=list(scratch),
            compiler_params=pltpu.CompilerParams(has_side_effects=True, vmem_limit_bytes=V7X_VMEM_LIMIT))
    collective_id, peers = sequencer

    def seq_body(*refs):
        barrier = pltpu.get_barrier_semaphore()
        plist = peers(*_position())
        for peer in plist:
            pl.semaphore_signal(barrier, inc=1, device_id=peer, device_id_type=MESH)
        pl.semaphore_wait(barrier, len(plist))
        body(*refs)

    return pl.kernel(seq_body, out_type=out_shape, mesh=plsc.ScalarSubcoreMesh(axis_name="sequencer", num_cores=1), name=name,
                     scratch_types=list(scratch), compiler_params=pltpu.CompilerParams(collective_id=collective_id))


def _n_chunks(rows, want, dtype):
    align = 16 if dtype == BF16 else 8
    n = want
    while n > 1 and (rows % n or (rows // n) % align):
        n -= 1
    return n


def _remote(src, dst, send_sem, recv_sem, device):
    return pltpu.make_async_remote_copy(src_ref=src, dst_ref=dst, send_sem=send_sem, recv_sem=recv_sem,
                                        device_id=device, device_id_type=MESH)


def _start_in_chunks(src, dst, send_sem, recv_sem, device, want=8):
    rows = src.shape[0]
    n = _n_chunks(rows, want, src.dtype)
    for i in range(n):
        part = pl.ds(i * (rows // n), rows // n)
        _remote(src.at[part], dst.at[part], send_sem, recv_sem, device).start()


def _all_gather_chips(name, shards, split, collective_id=None):
    n = len(shards)

    def body(*refs):
        ins, outs = refs[:n], refs[n:2 * n]
        send_sems, recv_sems, local_sems = refs[2 * n:]
        x, y, c = _position()
        me = 2 * x + y
        chips = _other_chips(x, y)
        sibling = (x, y, 1 - c)
        sends, forwards = [], []
        for t in range(n):
            pltpu.make_async_copy(ins[t], outs[t].at[me], local_sems.at[t]).start()
        for t in range(n):
            r = ins[t].shape[0]
            rows = pl.ds(c * (r // 2), r // 2) if split[t] else pl.ds(0, r)
            for k, (cx, cy) in enumerate(chips):
                src, dst = ins[t].at[rows], outs[t].at[me, rows]
                _start_in_chunks(src, dst, send_sems.at[t, k], recv_sems.at[t, k], (cx, cy, c), want=4)
                sends.append(_remote(src, dst, send_sems.at[t, k], recv_sems.at[t, k], (cx, cy, c)))
        for t in range(n):
            r = ins[t].shape[0]
            rows = pl.ds(c * (r // 2), r // 2) if split[t] else pl.ds(0, r)
            for k, (cx, cy) in enumerate(chips):
                landed = outs[t].at[2 * cx + cy, rows]
                _remote(landed, landed, send_sems.at[t, k], recv_sems.at[t, k], (cx, cy, c)).wait_recv()
                if split[t]:
                    _start_in_chunks(landed, landed, send_sems.at[t, 3 + k], recv_sems.at[t, 3 + k], sibling, want=4)
                    forwards.append(_remote(landed, landed, send_sems.at[t, 3 + k], recv_sems.at[t, 3 + k], sibling))
        for t in range(n):
            if split[t]:
                r = ins[t].shape[0]
                other = pl.ds((1 - c) * (r // 2), r // 2)
                for k, (cx, cy) in enumerate(chips):
                    landed = outs[t].at[2 * cx + cy, other]
                    pltpu.make_async_remote_copy(
                        src_ref=landed, dst_ref=landed, send_sem=send_sems.at[t, 3 + k], recv_sem=recv_sems.at[t, 3 + k],
                        device_id=sibling, device_id_type=MESH).wait_recv()
        for cp in sends + forwards:
            cp.wait_send()
        for t in range(n):
            pltpu.make_async_copy(ins[t], outs[t].at[me], local_sems.at[t]).wait()

    out_shape = [_sds((N_CHIPS,) + s.shape, s.dtype) for s in shards]
    sequencer = None if collective_id is None else (collective_id, _peers_chips_and_sibling)
    return _comm_call(name, body, n, out_shape,
                      [pltpu.SemaphoreType.DMA((n, 6)), pltpu.SemaphoreType.DMA((n, 6)), pltpu.SemaphoreType.DMA((n,))],
                      sequencer)(*shards)


def _swap_halves_with_sibling(name, blob, collective_id=None):
    def body(b_ref, theirs_ref, send_sem, recv_sem):
        x, y, c = _position()
        sibling = (x, y, 1 - c)
        for k in range(b_ref.shape[1]):
            _start_in_chunks(b_ref.at[1 - c, k], theirs_ref.at[k], send_sem, recv_sem, sibling)
        _remote(b_ref.at[1 - c], theirs_ref, send_sem, recv_sem, sibling).wait()

    return _comm_call(name, body, 1, [_sds(blob.shape[1:], blob.dtype)],
                      [pltpu.SemaphoreType.DMA(()), pltpu.SemaphoreType.DMA(())],
                      None if collective_id is None else (collective_id, _peers_sibling))(blob)[0]


def _scatter_to_chips(name, parts, collective_id=None):
    def body(p_ref, o_ref, send_sems, recv_sems, local_sems):
        x, y, c = _position()
        me = 2 * x + y
        rows = p_ref.shape[1]
        n_loc = _n_chunks(rows, 16, p_ref.dtype)
        locs = [pltpu.make_async_copy(p_ref.at[me, pl.ds(i * (rows // n_loc), rows // n_loc)],
                                      o_ref.at[me, pl.ds(i * (rows // n_loc), rows // n_loc)], local_sems.at[i])
                for i in range(n_loc)]
        for loc in locs:
            loc.start()
        sends = []
        for k, (cx, cy) in enumerate(_other_chips(x, y)):
            src, dst = p_ref.at[2 * cx + cy], o_ref.at[me]
            _start_in_chunks(src, dst, send_sems.at[k], recv_sems.at[k], (cx, cy, c))
            sends.append(_remote(src, dst, send_sems.at[k], recv_sems.at[k], (cx, cy, c)))
        for k, (cx, cy) in enumerate(_other_chips(x, y)):
            landed = o_ref.at[2 * cx + cy]
            _remote(landed, landed, send_sems.at[k], recv_sems.at[k], (cx, cy, c)).wait_recv()
        for cp in sends:
            cp.wait_send()
        for loc in locs:
            loc.wait()

    def local_sems_shape(rows):
        return pltpu.SemaphoreType.DMA((_n_chunks(rows, 16, parts.dtype),))

    return _comm_call(name, body, 1, [_sds(parts.shape, parts.dtype)],
                      [pltpu.SemaphoreType.DMA((3,)), pltpu.SemaphoreType.DMA((3,)), local_sems_shape(parts.shape[1])],
                      None if collective_id is None else (collective_id, _peers_chips))(parts)[0]


def _share_with_sibling(name, mine, collective_id=None):
    def body(m_ref, o_ref, send_sem, recv_sem):
        x, y, c = _position()
        sibling = (x, y, 1 - c)
        _start_in_chunks(m_ref, o_ref, send_sem, recv_sem, sibling, want=16)
        _remote(m_ref, o_ref, send_sem, recv_sem, sibling).wait()

    return _comm_call(name, body, 1, [_sds(mine.shape, mine.dtype)],
                      [pltpu.SemaphoreType.DMA(()), pltpu.SemaphoreType.DMA(())],
                      None if collective_id is None else (collective_id, _peers_sibling))(mine)[0]


def _row_tile(rows, cap=640):
    best = rows
    for t in range(16, min(rows, cap) + 1, 16):
        if rows % t == 0:
            best = t
    return best


_ANY = pl.BlockSpec(memory_space=pl.ANY)


def _add_my_half(name, blob, theirs, half_index, out_dtype, after):
    n, rows, cols = theirs.shape
    tr = _row_tile(rows)
    after = tuple(after)

    def body(c_ref, a_ref, b_ref, *rest):
        o_ref = rest[-1]
        o_ref[...] = (a_ref[0].astype(F32) + b_ref[...].astype(F32)).astype(out_dtype)

    spec = pl.BlockSpec((1, tr, cols), lambda k, i, c: (k, i, 0))
    grid_spec = pltpu.PrefetchScalarGridSpec(
        num_scalar_prefetch=1, grid=(n, rows // tr),
        in_specs=[pl.BlockSpec((1, 1, tr, cols), lambda k, i, c: (c[0], k, i, 0)), spec] + [_ANY] * len(after), out_specs=spec)
    return pl.pallas_call(
        body, name=name, grid_spec=grid_spec, out_shape=_sds(theirs.shape, out_dtype),
        compiler_params=pltpu.CompilerParams(dimension_semantics=("arbitrary", "arbitrary"),
                                             vmem_limit_bytes=V7X_VMEM_LIMIT))(half_index, blob, theirs, *after)


def _sum_slots(name, parts, after):
    n, rows, cols = parts.shape
    tr = _row_tile(rows)

    def body(p_ref, o_ref):
        acc = p_ref[0].astype(F32)
        for k in range(1, n):
            acc = acc + p_ref[k].astype(F32)
        o_ref[...] = acc

    return _call(name, body, (rows // tr,), [pl.BlockSpec((n, tr, cols), lambda i: (0, i, 0))],
                 pl.BlockSpec((tr, cols), lambda i: (i, 0)), _sds((rows, cols), F32), after=after)(parts)


def _reduce_small_adamw(name, grads, loss_tile, ws, ms, vs, after=()):
    n = len(grads)
    srcs = list(grads) + [loss_tile]
    after = tuple(after)

    def body(*refs):
        refs = refs[:4 * n + 1] + refs[4 * n + 1 + len(after):]
        g_in, w_in, m_in, v_in = refs[:n + 1], refs[n + 1:2 * n + 1], refs[2 * n + 1:3 * n + 1], refs[3 * n + 1:4 * n + 1]
        outs = refs[4 * n + 1:8 * n + 2]
        g_out, d_out, nm_out, nv_out, loss_out = outs[:n], outs[n:2 * n], outs[2 * n:3 * n], outs[3 * n:4 * n], outs[4 * n]
        bufs = refs[8 * n + 2:9 * n + 3]
        send_sems, recv_sems = refs[9 * n + 3:]
        x, y, c = _position()
        me = 4 * x + 2 * y + c
        chip = 2 * x + y
        peers = [(1 - x if dlt & 4 else x, 1 - y if dlt & 2 else y, 1 - c if dlt & 1 else c) for dlt in range(1, N_DEV)]
        sends = []
        for t in range(n + 1):
            bufs[t][me] = g_in[t][...]
            for k, peer in enumerate(peers):
                cp = _remote(g_in[t], bufs[t].at[me], send_sems.at[t, k], recv_sems.at[t, k], peer)
                cp.start()
                sends.append(cp)
        for t in range(n + 1):
            for k, (tx, ty, tc) in enumerate(peers):
                landed = bufs[t].at[4 * tx + 2 * ty + tc]
                _remote(landed, landed, send_sems.at[t, k], recv_sems.at[t, k], (tx, ty, tc)).wait_recv()
        for cp in sends:
            cp.wait_send()
        for t in range(n + 1):
            total = bufs[t][0]
            for k in range(1, N_DEV):
                total = total + bufs[t][k]
            if t == n:
                loss_out[...] = total
                continue
            cols = w_in[t].shape[1]
            if cols == total.shape[1]:
                g_out[t][...] = total
                d_out[t][...], nm_out[t][...], nv_out[t][...] = _adamw_math(w_in[t][...], total, m_in[t][...], v_in[t][...])
            else:
                for j in range(N_CHIPS):
                    @pl.when(chip == j)
                    def _(t=t, j=j, cols=cols, total=total):
                        mine = total[:, j * cols:(j + 1) * cols]
                        g_out[t][...] = mine
                        d_out[t][...], nm_out[t][...], nv_out[t][...] = _adamw_math(w_in[t][...], mine, m_in[t][...], v_in[t][...])

    w_shapes = [_sds(a.shape, F32) for a in ws]
    return pl.pallas_call(
        body, name=name, in_specs=[_VMEM] * (4 * n + 1) + [_ANY] * len(after), out_specs=[_VMEM] * (4 * n + 1),
        out_shape=w_shapes * 4 + [_sds(loss_tile.shape, F32)],
        scratch_shapes=[pltpu.VMEM((N_DEV,) + a.shape, F32) for a in srcs]
        + [pltpu.SemaphoreType.DMA((n + 1, N_DEV - 1)), pltpu.SemaphoreType.DMA((n + 1, N_DEV - 1))],
        compiler_params=pltpu.CompilerParams(has_side_effects=True, vmem_limit_bytes=V7X_VMEM_LIMIT))(
            *srcs, *ws, *ms, *vs, *after)


_BIG = ("ffn1_w_gate_up", "ffn1_w_down", "ffn2_w_gate_up", "ffn2_w_down", "ssm_w_in", "ssm_w_out", "w_kv", "attn_w_q", "attn_w_o")
_TRANSPOSED = ("ffn1_w_gate_up", "ffn2_w_gate_up")
_FROM_HALVES = _TRANSPOSED + ("ffn1_w_down", "ffn2_w_down")
_SMALL = ("meta_tokens", "ffn1_norm", "mix_norm", "ffn2_norm", "ssm_lambda_re", "ssm_lambda_im", "ssm_b_re", "ssm_b_im",
          "ssm_c_re", "ssm_c_im", "ssm_log_step", "ssm_d", "kv_norm", "k_norm", "q_norm", "attn_sinks")
_ORDER = ("meta_tokens", "ffn1_norm", "ffn1_w_gate_up", "ffn1_w_down", "mix_norm", "ffn2_norm", "ffn2_w_gate_up", "ffn2_w_down",
          "ssm_w_in", "ssm_lambda_re", "ssm_lambda_im", "ssm_b_re", "ssm_b_im", "ssm_c_re", "ssm_c_im", "ssm_log_step", "ssm_d",
          "ssm_w_out", "kv_norm", "w_kv", "k_norm", "attn_w_q", "q_norm", "attn_sinks", "attn_w_o")


def _step(x, target, w, m, v):
    n_ex, seq, d = x.shape
    n_meta = w["meta_tokens"].shape[0]
    n_main = n_ex * seq
    n_all = n_main + n_ex * META_BLOCK
    n_g, n_p, n_c = w["ssm_b_re"].shape[1:]
    hd = w["k_norm"].shape[0]
    n_kv = w["w_kv"].shape[1] // (2 * hd)
    n_q = w["attn_w_q"].shape[2] // hd
    qpk = n_q // n_kv
    px, py, pc = _position()
    chip = 2 * px + py

    def cast(name, layer=0):
        a = w[name]
        return _cast_layer(f"cast_{name}_{layer}", a if a.ndim == 3 else a[None], layer)

    g_a = _all_gather_chips("gather_first", [cast("ffn1_w_gate_up"), w["meta_tokens"], w["ssm_d"]], [True, False, False],
                            collective_id=12)
    g_d = _all_gather_chips("gather_next", [cast("ffn1_w_down"), cast("ssm_w_in"), cast("ssm_w_out")], [True] * 3, collective_id=13)
    second = [cast("ffn2_w_gate_up"), cast("ffn2_w_down"), cast("w_kv")]
    g_b = _all_gather_chips("gather_second", second, [True] * 3, collective_id=1)
    third = [cast("ffn1_w_gate_up", 1), cast("ffn1_w_down", 1), cast("attn_w_q"), cast("attn_w_o"),
             cast("ffn2_w_gate_up", 1), cast("ffn2_w_down", 1)]
    g_c = _all_gather_chips("gather_third", third, [True] * 6, collective_id=2)
    wgu = {("ffn1", 0): g_a[0], ("ffn1", 1): g_c[0], ("ffn2", 0): g_b[0], ("ffn2", 1): g_c[4]}
    wd = {("ffn1", 0): g_d[0], ("ffn1", 1): g_c[1], ("ffn2", 0): g_b[1], ("ffn2", 1): g_c[5]}
    wd = {key: a.reshape(-1, d) for key, a in wd.items()}
    w_in = g_d[1].reshape(d, -1)
    wout4 = g_d[2]
    w_q = g_c[2].reshape(d, -1)
    w_o = g_c[3].reshape(-1, d)
    w_kv = g_b[2].reshape(d, -1)
    meta_full = jnp.transpose(g_a[1], (1, 0, 2)).reshape(n_meta, d)
    dskip = g_a[2].reshape(1, -1)

    row1 = lambda a: a.reshape(1, -1)
    ssm_args = tuple(w[k][0] for k in ("ssm_lambda_re", "ssm_lambda_im", "ssm_log_step", "ssm_b_re", "ssm_b_im", "ssm_c_re", "ssm_c_im"))
    (bb, cb, a_re, a_im), ssm_vjp = jax.vjp(_ssm_mats, *ssm_args)
    bb16, cb16 = bb.astype(BF16), cb.astype(BF16)
    a_re_s, a_im_s = lax.stop_gradient(a_re), lax.stop_gradient(a_im)
    half = n_g * n_p // 4
    tabs_f = _scan_tables(a_re_s, a_im_s, False)
    tabs_b = _scan_tables(a_re_s, a_im_s, True)

    freqs = ROPE_THETA ** (-jnp.arange(0, hd // 2, dtype=F32) * 2.0 / hd)
    pos_main = jnp.tile(n_meta + jnp.arange(seq), n_ex)
    pos_meta = jnp.tile(jnp.maximum(jnp.arange(META_BLOCK) - (META_BLOCK - n_meta), 0), n_ex)
    ang = jnp.concatenate([pos_main, pos_meta]).astype(F32)[:, None] * freqs[None, :]
    cos = jnp.concatenate([jnp.cos(ang), jnp.cos(ang)] * 2, axis=1)
    sin_s = jnp.concatenate([-jnp.sin(ang), jnp.sin(ang)] * 2, axis=1)
    k_gain_t = jnp.tile(row1(w["k_norm"]), (1, n_kv))
    score_scale = hd ** -0.5
    q_gain_t = jnp.tile(row1(w["q_norm"][0]), (1, n_q)) * score_scale

    meta_block = jnp.concatenate([jnp.zeros((META_BLOCK - n_meta, d), F32), meta_full], axis=0)
    h0 = jnp.concatenate([x.reshape(n_main, d)] + [meta_block] * n_ex, axis=0)

    g = lambda name, layer: row1(w[name][layer])
    h1, gu1 = _ffn_fwd("l0_ffn1", h0, g("ffn1_norm", 0), wgu["ffn1", 0], wd["ffn1", 0], n_all)
    u, bu = _ssm_in("ssm_in", h1, g("mix_norm", 0), w_in, bb16, n_all)
    xs = _scan_fwd("ssm_scan", bu, tabs_f, n_ex, seq)
    h2, y = _ssm_out("ssm_out", xs, u, dskip, cb16, wout4, h1, n_all)
    h3, gu2 = _ffn_fwd("l0_ffn2", h2, g("ffn2_norm", 0), wgu["ffn2", 0], wd["ffn2", 0], n_all)
    kvraw, k, vv = _kv_proj("kv_proj", h3, row1(w["kv_norm"]), w_kv, k_gain_t, cos, sin_s, n_all, n_kv, hd)
    h4, gu3 = _ffn_fwd("l1_ffn1", h3, g("ffn1_norm", 1), wgu["ffn1", 1], wd["ffn1", 1], n_main)
    qraw, q = _q_proj("q_proj", h4, g("mix_norm", 1), w_q, q_gain_t, cos, sin_s, n_main, n_q, hd)
    sinks = row1(w["attn_sinks"][0])
    o, lse = _attn_fwd("attn_fwd", q, k, vv, sinks, n_ex, seq, n_meta, n_kv, qpk, hd)
    h5 = _attn_out("attn_out", o, h4, w_o, n_main)
    (dh6, loss_tile), gu4 = _ffn_fwd("l1_ffn2", h5, g("ffn2_norm", 1), wgu["ffn2", 1], wd["ffn2", 1], n_main,
                                     target=target.reshape(n_main, d))

    lanes = 1024

    def rs_start(tag, entries, ids):
        pieces = [gr.reshape(N_CHIPS, 2, -1, lanes) for _, _, gr in entries]
        blob = jnp.transpose(jnp.concatenate(pieces, axis=2), (1, 0, 2, 3)).astype(BF16)
        return dict(tag=tag, entries=entries, ids=ids, blob=blob, theirs=_swap_halves_with_sibling(tag + "_swap", blob, ids[0]))

    def rs_scatter(st, after):
        chip_sum = _add_my_half(st["tag"] + "_chip_sum", st["blob"], st["theirs"], jnp.reshape(pc, (1,)).astype(jnp.int32), BF16, after)
        st["chip_sum"] = chip_sum
        st["landed"] = _scatter_to_chips(st["tag"] + "_scatter", chip_sum, st["ids"][1])

    def rs_finish(st, after):
        total = _sum_slots(st["tag"] + "_sum", st["landed"], after)
        st["total"] = total
        other = _share_with_sibling(st["tag"] + "_share", total, st["ids"][2])
        halves = (jnp.where(pc == 0, total, other), jnp.where(pc == 0, other, total))
        out, off = {}, 0
        for name, layer, gr in st["entries"]:
            rows = gr.shape[1] * gr.shape[2] // lanes // 2
            if name in _FROM_HALVES:
                out[name, layer] = (total, other, off)
            else:
                out[name, layer] = jnp.concatenate([hv[off:off + rows].reshape(-1) for hv in halves])
            off += rows
        return out

    small = {}
    dh5, dg_f2l1, dwgu_f2l1, dwd_f2l1 = _ffn_bwd("l1_ffn2", dh6, h5, g("ffn2_norm", 1), gu4, wgu["ffn2", 1], wd["ffn2", 1], n_main)
    do = _attn_out_bwd("attn_out_bwd", dh5, w_o, n_main)
    dw_o = _tn_plain("attn_dwo", o, dh5, 1, o.shape[1], d, n_main, out_dtype=BF16).reshape(N_CHIPS, -1, d)
    dq, dk_main, dv_main, dk_meta, dv_meta, dsinks = _attn_bwd("attn_bwd", q, k, vv, sinks, o, lse, do, n_ex, seq, n_meta, n_kv, qpk, hd)
    dqraw, dh4, dq_gain, dg_mix1 = _q_bwd("q_bwd", dq, qraw, q_gain_t, cos, sin_s, w_q, h4, g("mix_norm", 1), dh5, n_main, n_q, hd)
    dw_q = _tn_rms("attn_dwq", h4, g("mix_norm", 1), dqraw, n_main, out_dtype=BF16).reshape(N_CHIPS, -1, dqraw.shape[1])
    dh3m, dg_f1l1, dwgu_f1l1, dwd_f1l1 = _ffn_bwd("l1_ffn1", dh4, h3, g("ffn1_norm", 1), gu3, wgu["ffn1", 1], wd["ffn1", 1], n_main)
    rs1 = rs_start("rs1", [("ffn2_w_gate_up", 1, dwgu_f2l1), ("ffn1_w_gate_up", 1, dwgu_f1l1), ("ffn2_w_down", 1, dwd_f2l1),
                           ("ffn1_w_down", 1, dwd_f1l1), ("attn_w_o", 0, dw_o), ("attn_w_q", 0, dw_q)], (3, 4, 5))

    def with_meta(main, meta):
        blocks = [jnp.pad(meta[b * n_meta:(b + 1) * n_meta], ((META_BLOCK - n_meta, 0), (0, 0))) for b in range(n_ex)]
        return jnp.concatenate([main] + blocks, axis=0)

    dkvraw, dh3, dk_gain, dg_kv = _kv_bwd("kv_bwd", with_meta(dk_main, dk_meta), with_meta(dv_main, dv_meta), kvraw, k_gain_t,
                                          cos, sin_s, w_kv, h3, row1(w["kv_norm"]), dh3m, n_all, n_main, n_kv, hd,
                                          after=(rs1["blob"],))
    rs_scatter(rs1, after=(dh3,))
    dw_kv = _tn_rms("kv_dw", h3, row1(w["kv_norm"]), dkvraw, n_all, out_dtype=BF16).reshape(N_CHIPS, -1, dkvraw.shape[1])
    dh2, dg_f2l0, dwgu_f2l0, dwd_f2l0 = _ffn_bwd("l0_ffn2", dh3, h2, g("ffn2_norm", 0), gu2, wgu["ffn2", 0], wd["ffn2", 0], n_all,
                                                 after=(rs1["chip_sum"],))
    reduced = rs_finish(rs1, after=(dh2, dwgu_f2l0, dwd_f2l0, dw_kv))
    rs0a = rs_start("rs0a", [("ffn2_w_gate_up", 0, dwgu_f2l0), ("ffn2_w_down", 0, dwd_f2l0), ("w_kv", 0, dw_kv)], (6, 7, 8))

    dy, dz, gx, dd = _ssm_out_bwd("ssm_out_bwd", dh2, y, u, cb16, wout4, n_all, after=(rs1["total"], rs0a["blob"]))
    rs_scatter(rs0a, after=(dy,))
    hw = y.shape[1]
    oc = wout4.shape[2]
    dw_out = _tn_plain("ssm_dwout", y, dz, wout4.shape[0], hw, oc, n_all, a_fn=_gelu, out_dtype=BF16)
    gbu, da = _scan_bwd("ssm_scan_bwd", gx, xs, tabs_b, n_ex, seq, after=(rs0a["chip_sum"],))
    du, dh1, dg_mix0 = _ssm_in_bwd("ssm_in_bwd", gbu, dy, dskip, bb16, w_in, h1, g("mix_norm", 0), dh2, n_all)
    reduced.update(rs_finish(rs0a, after=(dh1,)))
    dw_in = _tn_rms("ssm_dwin", h1, g("mix_norm", 0), du, n_all, out_dtype=BF16).reshape(N_CHIPS, -1, hw)
    (dh0, dh0_meta), dg_f1l0, dwgu_f1l0, dwd_f1l0 = _ffn_bwd("l0_ffn1", dh1, h0, g("ffn1_norm", 0), gu1, wgu["ffn1", 0], wd["ffn1", 0],
                                                             n_all, n_main, after=(rs0a["total"],))
    rs0b = rs_start("rs0b", [("ffn1_w_gate_up", 0, dwgu_f1l0), ("ffn1_w_down", 0, dwd_f1l0), ("ssm_w_out", 0, dw_out),
                             ("ssm_w_in", 0, dw_in)], (9, 10, 11))
    dcb = _tn_plain("ssm_dcb", xs, dy, 4, xs.shape[1] // 4, hw // 4, n_all, after=(rs0b["blob"],))
    rs_scatter(rs0b, after=(dcb,))
    dbb = _tn_plain("ssm_dbb", u, gbu, 4, hw // 4, gbu.shape[1] // 4, n_all, after=(rs0b["chip_sum"],))

    grad_x = dh0.reshape(n_ex, seq, d)
    da_sum = jnp.sum(da, axis=(0, 1)).reshape(4, 2, half)
    d_ssm = ssm_vjp((dbb, dcb, da_sum[:, 0].reshape(-1), da_sum[:, 1].reshape(-1)))
    for key, val in zip(("ssm_lambda_re", "ssm_lambda_im", "ssm_log_step", "ssm_b_re", "ssm_b_im", "ssm_c_re", "ssm_c_im"), d_ssm):
        small[key] = val[None]
    small["meta_tokens"] = sum(dh0_meta[META_BLOCK * (b + 1) - n_meta:META_BLOCK * (b + 1)] for b in range(n_ex))
    small["ffn1_norm"] = jnp.concatenate([dg_f1l0, dg_f1l1], axis=0)
    small["ffn2_norm"] = jnp.concatenate([dg_f2l0, dg_f2l1], axis=0)
    small["mix_norm"] = jnp.concatenate([dg_mix0, dg_mix1], axis=0)
    small["ssm_d"] = dd
    small["kv_norm"] = dg_kv.reshape(-1)
    small["k_norm"] = jnp.sum(dk_gain.reshape(n_kv, hd), axis=0)
    small["q_norm"] = jnp.sum(dq_gain.reshape(n_q, hd), axis=0, keepdims=True) * score_scale
    small["attn_sinks"] = dsinks

    def view(name, a):
        if name in ("ssm_b_re", "ssm_b_im"):
            return a.reshape(-1, 128)
        return a.reshape(1, -1) if a.ndim == 1 else a.reshape(-1, a.shape[-1])

    grads, deltas, new_m, new_v = {}, {}, {}, {}

    def adamw_matrix(name, after=()):
        shape = w[name].shape
        if name in _FROM_HALVES:
            grads[name], deltas[name], new_m[name], new_v[name] = _adamw_from_halves(
                "adamw_" + name, w[name], m[name], v[name], [reduced[name, 0], reduced[name, 1]],
                jnp.reshape(pc, (1,)).astype(jnp.int32), name in _TRANSPOSED, after=after)
            return new_v[name]
        layers = [reduced[name, layer] for layer in range(2) if (name, layer) in reduced]
        grads[name] = jnp.concatenate(layers).reshape(shape)
        two_d = lambda a: a.reshape(-1, shape[-1])
        dl, nm, nv = _adamw("adamw_" + name, two_d(w[name]), two_d(grads[name]), two_d(m[name]), two_d(v[name]), after=after)
        deltas[name], new_m[name], new_v[name] = dl.reshape(shape), nm.reshape(shape), nv.reshape(shape)
        return nv

    placed = (rs0b["chip_sum"],)
    for name in ("ffn2_w_down", "attn_w_o", "attn_w_q", "w_kv"):
        placed = (adamw_matrix(name, after=placed),)
    tail = _reduce_small_adamw("small_tail", [view(k, small[k]) for k in _SMALL], loss_tile,
                               *[[view(k, t[k]) for k in _SMALL] for t in (w, m, v)], after=placed)
    n_small = len(_SMALL)
    for i, k in enumerate(_SMALL):
        grads[k], deltas[k] = tail[i].reshape(w[k].shape), tail[n_small + i].reshape(w[k].shape)
        new_m[k], new_v[k] = tail[2 * n_small + i].reshape(w[k].shape), tail[3 * n_small + i].reshape(w[k].shape)
    loss = jnp.sum(tail[-1])
    reduced.update(rs_finish(rs0b, after=(tail[-1],)))
    adamw_matrix("ffn2_w_gate_up", after=(rs0b["total"],))
    for name in ("ffn1_w_gate_up", "ffn1_w_down", "ssm_w_in", "ssm_w_out"):
        adamw_matrix(name)
    return (loss, grad_x, *[grads[k] for k in _ORDER], *[deltas[k] for k in _ORDER], *[new_m[k] for k in _ORDER],
            *[new_v[k] for k in _ORDER])


def kernel(x, meta_tokens, ffn1_norm, ffn1_w_gate_up, ffn1_w_down, mix_norm, ffn2_norm, ffn2_w_gate_up, ffn2_w_down, ssm_w_in, ssm_lambda_re, ssm_lambda_im, ssm_b_re, ssm_b_im, ssm_c_re, ssm_c_im, ssm_log_step, ssm_d, ssm_w_out, kv_norm, w_kv, k_norm, attn_w_q, q_norm, attn_sinks, attn_w_o, loss_target, m_meta_tokens, m_ffn1_norm, m_ffn1_w_gate_up, m_ffn1_w_down, m_mix_norm, m_ffn2_norm, m_ffn2_w_gate_up, m_ffn2_w_down, m_ssm_w_in, m_ssm_lambda_re, m_ssm_lambda_im, m_ssm_b_re, m_ssm_b_im, m_ssm_c_re, m_ssm_c_im, m_ssm_log_step, m_ssm_d, m_ssm_w_out, m_kv_norm, m_w_kv, m_k_norm, m_attn_w_q, m_q_norm, m_attn_sinks, m_attn_w_o, v_meta_tokens, v_ffn1_norm, v_ffn1_w_gate_up, v_ffn1_w_down, v_mix_norm, v_ffn2_norm, v_ffn2_w_gate_up, v_ffn2_w_down, v_ssm_w_in, v_ssm_lambda_re, v_ssm_lambda_im, v_ssm_b_re, v_ssm_b_im, v_ssm_c_re, v_ssm_c_im, v_ssm_log_step, v_ssm_d, v_ssm_w_out, v_kv_norm, v_w_kv, v_k_norm, v_attn_w_q, v_q_norm, v_attn_sinks, v_attn_w_o):
    args = locals()
    w = {k: args[k] for k in _ORDER}
    m = {k: args["m_" + k] for k in _ORDER}
    v = {k: args["v_" + k] for k in _ORDER}
    return _step(x, loss_target, w, m, v)
```

```python
import functools
import math

import jax
import jax.numpy as jnp
from jax import lax
from jax.experimental import pallas as pl
from jax.experimental.pallas import tpu as pltpu
from jax.experimental.pallas import tpu_sc as plsc

F32 = jnp.float32
BF16 = jnp.bfloat16
MESH = pl.DeviceIdType.MESH

EPS = 1e-6
NEG_INF = -1e30
ROPE_THETA = 10000.0
WINDOW = 128
META_BLOCK = 128
ROW_TILE = 256
SUBLANES = 8
V7X_VMEM_LIMIT = 56 * 2**20
N_CHIPS = 4
N_DEV = 8

ADAM_LR = 0.001
ADAM_B1 = 0.9
ADAM_B2 = 0.999
ADAM_EPS = 1e-08
ADAM_WD = 0.01
ADAM_STEP = 10

_HBM = pl.BlockSpec(memory_space=pltpu.HBM)
_VMEM = pl.BlockSpec(memory_space=pltpu.VMEM)


def _call(name, body, grid, in_specs, out_specs, out_shape, scratch=(), after=()):
    after = tuple(after)
    n_in = len(in_specs)

    def wrapped(*refs):
        return body(*refs[:n_in], *refs[n_in + len(after):])

    call = pl.pallas_call(
        wrapped, name=name, grid=grid, in_specs=list(in_specs) + [pl.BlockSpec(memory_space=pl.ANY)] * len(after),
        out_specs=out_specs, out_shape=out_shape, scratch_shapes=list(scratch),
        compiler_params=pltpu.CompilerParams(dimension_semantics=("arbitrary",) * len(grid),
                                             vmem_limit_bytes=V7X_VMEM_LIMIT))
    return lambda *operands: call(*operands, *after)


def _sds(shape, dtype):
    return jax.ShapeDtypeStruct(tuple(shape), dtype)


def _dot(a, b):
    return jnp.dot(a.astype(BF16), b.astype(BF16), preferred_element_type=F32)


def _dot_nt(a, b):
    return lax.dot_general(a.astype(BF16), b.astype(BF16), (((1,), (1,)), ((), ())), preferred_element_type=F32)


def _dot_tn(a, b):
    return lax.dot_general(a.astype(BF16), b.astype(BF16), (((0,), (0,)), ((), ())), preferred_element_type=F32)


def _rms(h, g):
    return h * lax.rsqrt(jnp.mean(h * h, axis=-1, keepdims=True) + EPS) * g


def _rms_bwd(h, g, dn):
    r = lax.rsqrt(jnp.mean(h * h, axis=-1, keepdims=True) + EPS)
    xh = h * r
    dxh = dn * g
    dg = jnp.sum(dn * xh, axis=0, keepdims=True)
    dh = r * (dxh - xh * jnp.mean(dxh * xh, axis=-1, keepdims=True))
    return dh, dg


def _sigmoid(x):
    return 0.5 * jnp.tanh(0.5 * x) + 0.5


def _gelu(y):
    k = math.sqrt(2.0 / math.pi)
    return 0.5 * y * (1.0 + jnp.tanh(k * (y + 0.044715 * y * y * y)))


def _gelu_grad(y):
    k = math.sqrt(2.0 / math.pi)
    t = jnp.tanh(k * (y + 0.044715 * y * y * y))
    return 0.5 * (1.0 + t) + 0.5 * y * (1.0 - t * t) * k * (1.0 + 3.0 * 0.044715 * y * y)


def _partner(x, lane, d):
    width = x.shape[-1]
    return jnp.where((lane & d) == 0, pltpu.roll(x, width - d, 1), pltpu.roll(x, d, 1))


def _split_bf16(x):
    hi = x.astype(BF16)
    return hi, (x - hi.astype(F32)).astype(BF16)


def _head_sums(x, sel):
    hi, lo = _split_bf16(x)
    return jnp.dot(hi, sel, preferred_element_type=F32) + jnp.dot(lo, sel, preferred_element_type=F32)


def _head_expand(v, sel_t):
    hi, lo = _split_bf16(v)
    return jnp.dot(hi, sel_t, preferred_element_type=F32) + jnp.dot(lo, sel_t, preferred_element_type=F32)


def _tile_lanes(t, width):
    return jnp.concatenate([t] * (width // t.shape[-1]), axis=1)


def _head_prep(x, gain_t, cos2, sin2, sel, sel_t, hd):
    width = x.shape[-1]
    lane = lax.broadcasted_iota(jnp.int32, x.shape, 1)
    r = _head_expand(lax.rsqrt(_head_sums(x * x, sel) * (1.0 / hd) + EPS), sel_t)
    y = x * r * gain_t
    return y * _tile_lanes(cos2, width) + _partner(y, lane, hd // 2) * _tile_lanes(sin2, width)


def _head_prep_bwd(x, gain_t, cos2, sin2, sel, sel_t, d_out, hd):
    width = x.shape[-1]
    lane = lax.broadcasted_iota(jnp.int32, x.shape, 1)
    r = _head_expand(lax.rsqrt(_head_sums(x * x, sel) * (1.0 / hd) + EPS), sel_t)
    xhat = x * r
    dy = d_out * _tile_lanes(cos2, width) + _partner(d_out * _tile_lanes(sin2, width), lane, hd // 2)
    dgain = jnp.sum(dy * xhat, axis=0, keepdims=True)
    dxh = dy * gain_t
    mean = _head_expand(_head_sums(dxh * xhat, sel) * (1.0 / hd), sel_t)
    return r * (dxh - xhat * mean), dgain


def _head_selectors(n_heads, hd):
    sel = (jnp.arange(n_heads * hd)[:, None] // hd == jnp.arange(128)[None, :]).astype(BF16)
    return sel, sel.T


def _acc_out(ref, val, first):
    @pl.when(first)
    def _():
        ref[...] = jnp.zeros_like(ref)
    ref[...] += val


def _wide_row_tile(n_rows, cap=512):
    best = 128
    for t in range(128, cap + 1, 128):
        if n_rows % t == 0:
            best = t
    return best


def _ffn_up(name, h, g, w4, n_rows):
    nj, d, fc = w4.shape
    tm = _wide_row_tile(n_rows)

    def body(h_ref, g_ref, w_ref, o_ref, n_ref):
        n = _rms(h_ref[...], g_ref[...]).astype(BF16)
        n_ref[...] = n
        for j in range(nj):
            o_ref[:, j * fc:(j + 1) * fc] = _dot(n, w_ref[j]).astype(BF16)

    return _call(name, body, (n_rows // tm,),
                 [pl.BlockSpec((tm, d), lambda i: (i, 0)), pl.BlockSpec((1, d), lambda i: (0, 0)),
                  pl.BlockSpec((nj, d, fc), lambda i: (0, 0, 0))],
                 [pl.BlockSpec((tm, nj * fc), lambda i: (i, 0)), pl.BlockSpec((tm, d), lambda i: (i, 0))],
                 [_sds((n_rows, nj * fc), BF16), _sds((n_rows, d), BF16)])(h, g, w4)


def _ffn_down(name, gu, h, wd, n_rows, target=None):
    f, d = wd.shape
    tm = ROW_TILE

    def body(gu_ref, h_ref, w_ref, *rest):
        half_a = gu_ref[:, :f] * 0.5
        s = (half_a + half_a * jnp.tanh(half_a)) * gu_ref[:, f:]
        y = h_ref[...] + 0.5 * _dot(s, w_ref[...])
        if target is None:
            o_ref, s_ref = rest
            o_ref[...] = y
        else:
            t_ref, dy_ref, l_ref, s_ref = rest
            e = y - t_ref[...]
            dy_ref[...] = e * (1.0 / d)
            e2 = jnp.sum((e * e).reshape(tm // SUBLANES, SUBLANES, d), axis=0)
            part = e2[:, 0:128]
            for k in range(1, d // 128):
                part = part + e2[:, k * 128:(k + 1) * 128]
            _acc_out(l_ref, part * (0.5 / d), pl.program_id(0) == 0)
        s_ref[...] = s

    row = lambda width: pl.BlockSpec((tm, width), lambda i: (i, 0))
    in_specs = [row(2 * f), row(d), pl.BlockSpec((f, d), lambda i: (0, 0))]
    if target is None:
        return _call(name, body, (n_rows // tm,), in_specs, [row(d), row(f)],
                     [_sds((n_rows, d), F32), _sds((n_rows, f), BF16)])(gu, h, wd)
    return _call(name, body, (n_rows // tm,), in_specs + [row(d)],
                 [row(d), pl.BlockSpec((SUBLANES, 128), lambda i: (0, 0)), row(f)],
                 [_sds((n_rows, d), F32), _sds((SUBLANES, 128), F32), _sds((n_rows, f), BF16)])(gu, h, wd, target)


def _ffn_dgu(name, dh, gu, wd, n_rows, after=()):
    f, d = wd.shape
    tm = ROW_TILE

    def body(dh_ref, gu_ref, w_ref, o_ref):
        ds = _dot_nt(0.5 * dh_ref[...], w_ref[...]).astype(BF16)
        half_a = gu_ref[:, :f] * 0.5
        t = jnp.tanh(half_a)
        o_ref[:, :f] = ds * gu_ref[:, f:] * ((1.0 + t + half_a * (1.0 - t * t)) * 0.5)
        o_ref[:, f:] = ds * (half_a + half_a * t)

    return _call(name, body, (n_rows // tm,),
                 [pl.BlockSpec((tm, d), lambda i: (i, 0)), pl.BlockSpec((tm, 2 * f), lambda i: (i, 0)),
                  pl.BlockSpec((f, d), lambda i: (0, 0))],
                 pl.BlockSpec((tm, 2 * f), lambda i: (i, 0)), _sds((n_rows, 2 * f), BF16), after=after)(dh, gu, wd)


def _ffn_dh(name, dgu, h, g, dh, w4, n_rows, n_main=None, after=()):
    nj, d, fc = w4.shape
    tm = _wide_row_tile(n_rows) if n_main is None else ROW_TILE
    n_first = (n_rows if n_main is None else n_main) // tm

    def body(dgu_ref, h_ref, g_ref, dh_ref, w_ref, o_ref, *rest):
        dg_ref = rest[-1]
        i = pl.program_id(0)
        dn = _dot_nt(dgu_ref[:, 0:fc], w_ref[0])
        for j in range(1, nj):
            dn = dn + _dot_nt(dgu_ref[:, j * fc:(j + 1) * fc], w_ref[j])
        dhn, dg = _rms_bwd(h_ref[...], g_ref[...], dn)
        val = dh_ref[...] + dhn
        if n_main is None:
            o_ref[...] = val
        else:
            @pl.when(i < n_first)
            def _():
                o_ref[...] = val

            @pl.when(i >= n_first)
            def _():
                rest[0][...] = val
        _acc_out(dg_ref, dg, i == 0)

    out_specs = [pl.BlockSpec((tm, d), lambda i: (jnp.minimum(i, n_first - 1), 0))]
    out_shape = [_sds((n_first * tm, d), F32)]
    if n_main is not None:
        out_specs.append(pl.BlockSpec((tm, d), lambda i: (jnp.maximum(i - n_first, 0), 0)))
        out_shape.append(_sds((n_rows - n_main, d), F32))
    return _call(name, body, (n_rows // tm,),
                 [pl.BlockSpec((tm, nj * fc), lambda i: (i, 0)), pl.BlockSpec((tm, d), lambda i: (i, 0)),
                  pl.BlockSpec((1, d), lambda i: (0, 0)), pl.BlockSpec((tm, d), lambda i: (i, 0)),
                  pl.BlockSpec((nj, d, fc), lambda i: (0, 0, 0), pipeline_mode=pl.Buffered(1))],
                 out_specs + [pl.BlockSpec((1, d), lambda i: (0, 0))],
                 out_shape + [_sds((1, d), F32)], after=after)(dgu, h, g, dh, w4)


def _contract_tile(n_rows, cap=2816):
    best = ROW_TILE
    for t in range(ROW_TILE, cap + 1, ROW_TILE):
        if n_rows % t == 0:
            best = t
    return best


def _tn(name, operands, in_specs, prologue, nj, ma, nb, n_rows, tk, out_dtype=F32, after=(), side_by_side=False):
    n_k = n_rows // tk
    out_spec = pl.BlockSpec((1, ma, nb), lambda j, k: (j, 0, 0))
    if out_dtype == F32 and not side_by_side:
        def body(*refs):
            o_ref = refs[-1]
            a, b = prologue(pl.program_id(0), *refs[:-1])
            _acc_out(o_ref, _dot_tn(a, b)[None], pl.program_id(1) == 0)

        return _call(name, body, (nj, n_k), in_specs, out_spec, _sds((nj, ma, nb), F32), after=after)(*operands)

    def body_rounded(*refs):
        o_ref, acc_ref = refs[-2:]
        a, b = prologue(pl.program_id(0), *refs[:-2])
        _acc_out(acc_ref, _dot_tn(a, b), pl.program_id(1) == 0)

        @pl.when(pl.program_id(1) == n_k - 1)
        def _():
            o_ref[...] = acc_ref[...].astype(out_dtype).reshape(o_ref.shape)

    if side_by_side:
        out_spec, out_shape = pl.BlockSpec((ma, nb), lambda j, k: (0, j)), _sds((ma, nj * nb), out_dtype)
    else:
        out_shape = _sds((nj, ma, nb), out_dtype)
    return _call(name, body_rounded, (nj, n_k), in_specs, out_spec, out_shape,
                 scratch=[pltpu.VMEM((ma, nb), F32)], after=after)(*operands)


def _ffn_dwgu(name, n, dgu, nj, n_rows):
    d = n.shape[1]
    fc = dgu.shape[1] // nj
    tk = _contract_tile(n_rows, cap=2816)
    return _tn(name, (dgu, n),
               [pl.BlockSpec((tk, fc), lambda j, k: (k, j)), pl.BlockSpec((tk, d), lambda j, k: (k, 0))],
               lambda j, a_ref, b_ref: (a_ref[...], b_ref[...]), nj, fc, d, n_rows, tk, out_dtype=BF16)


def _ffn_dwd(name, s, dh, n_rows):
    f = s.shape[1]
    d = dh.shape[1]
    tk = _contract_tile(n_rows, cap=2048)
    halves = 2 if tk > 1024 else 1
    return _tn(name, (s, dh),
               [pl.BlockSpec((tk, f), lambda j, k: (k, 0)), pl.BlockSpec((tk, d // halves), lambda j, k: (k, j))],
               lambda j, s_ref, dh_ref: (s_ref[...], 0.5 * dh_ref[...]), halves, f, d // halves, n_rows, tk, out_dtype=BF16,
               side_by_side=True)


def _ffn_fwd(tag, h, g, w4, wd, n_rows, target=None):
    gu, n = _ffn_up(tag + "_up", h, g, w4, n_rows)
    *out, s = _ffn_down(tag + "_down", gu, h, wd, n_rows, target)
    return (out[0] if target is None else tuple(out)), (gu, n, s)


def _ffn_bwd(tag, dh_out, h, g, saved, w4, wd, n_rows, n_main=None, after=()):
    gu, n, s = saved
    nj = w4.shape[0]
    f, d = wd.shape
    dgu = _ffn_dgu(tag + "_dgu", dh_out, gu, wd, n_rows, after=after)
    dwd = _ffn_dwd(tag + "_dwd", s, dh_out, n_rows).reshape(N_CHIPS, f // N_CHIPS, d)
    *dh_parts, dg = _ffn_dh(tag + "_dh", dgu, h, g, dh_out, w4, n_rows, n_main)
    dwgu = _ffn_dwgu(tag + "_dwgu", n, dgu, nj, n_rows)
    dh_in = dh_parts[0] if n_main is None else tuple(dh_parts)
    return dh_in, dg, dwgu, dwd


def _ssm_in(name, h, g, w_in, bb, n_rows):
    d, hw = w_in.shape
    nj, uc, xc = bb.shape
    tm = ROW_TILE

    def body(h_ref, g_ref, w_ref, bb_ref, u_ref, bu_ref):
        u = _dot(_rms(h_ref[...], g_ref[...]), w_ref[...])
        u_ref[...] = u
        for j in range(nj):
            bu_ref[:, j * xc:(j + 1) * xc] = _dot(u[:, j * uc:(j + 1) * uc], bb_ref[j]).astype(BF16)

    return _call(name, body, (n_rows // tm,),
                 [pl.BlockSpec((tm, d), lambda i: (i, 0)), pl.BlockSpec((1, d), lambda i: (0, 0)),
                  pl.BlockSpec((d, hw), lambda i: (0, 0)), pl.BlockSpec((nj, uc, xc), lambda i: (0, 0, 0))],
                 [pl.BlockSpec((tm, hw), lambda i: (i, 0)), pl.BlockSpec((tm, nj * xc), lambda i: (i, 0))],
                 [_sds((n_rows, hw), F32), _sds((n_rows, nj * xc), BF16)])(h, g, w_in, bb)


def _cmul_add(xr, xi, ar, ai, sr, si):
    return xr + ar * sr - ai * si, xi + ar * si + ai * sr


def _scan_row_block(n_main_blocks, seq_blocks):
    return lambda b, i: jnp.where(i == 0, n_main_blocks + b, b * seq_blocks + i - 1)


def _scan_fwd(name, bu, tabs, n_ex, seq):
    n_rows, width = bu.shape
    nj = 4
    cw = width // nj
    half = cw // 2
    tq = META_BLOCK
    seq_blocks = seq // tq
    rb = _scan_row_block(n_ex * seq_blocks, seq_blocks)

    def body(bu_ref, tab_ref, x_ref, carry_ref):
        @pl.when(pl.program_id(1) == 0)
        def _():
            carry_ref[...] = jnp.zeros_like(carry_ref)

        for j in range(nj):
            re, im = slice(j * cw, j * cw + half), slice(j * cw + half, (j + 1) * cw)
            ch = slice(j * half, (j + 1) * half)

            def blk(k, c, re=re, im=im, ch=ch):
                t = [tab_ref[n * SUBLANES:(n + 1) * SUBLANES, ch] for n in range(8)]
                r0 = pl.multiple_of(k * SUBLANES, SUBLANES)
                xr = bu_ref[pl.ds(r0, SUBLANES), re].astype(F32)
                xi = bu_ref[pl.ds(r0, SUBLANES), im].astype(F32)
                for s, d in enumerate((1, 2, 4)):
                    xr, xi = _cmul_add(xr, xi, t[2 * s], t[2 * s + 1], pltpu.roll(xr, d, 0), pltpu.roll(xi, d, 0))
                xr, xi = _cmul_add(xr, xi, t[6], t[7], c[0], c[1])
                x_ref[pl.ds(r0, SUBLANES), re] = xr.astype(BF16)
                x_ref[pl.ds(r0, SUBLANES), im] = xi.astype(BF16)
                last = SUBLANES - 1
                return (jnp.broadcast_to(xr[last:last + 1, :], xr.shape), jnp.broadcast_to(xi[last:last + 1, :], xi.shape))

            c = lax.fori_loop(0, tq // SUBLANES, blk, (carry_ref[0, :, ch], carry_ref[1, :, ch]), unroll=4)
            carry_ref[0, :, ch] = c[0]
            carry_ref[1, :, ch] = c[1]

    return _call(name, body, (n_ex, seq_blocks + 1),
                 [pl.BlockSpec((tq, width), lambda b, i: (rb(b, i), 0)), pl.BlockSpec((8 * SUBLANES, nj * half), lambda b, i: (0, 0))],
                 pl.BlockSpec((tq, width), lambda b, i: (rb(b, i), 0)), _sds((n_rows, width), BF16),
                 scratch=[pltpu.VMEM((2, SUBLANES, nj * half), F32)])(bu, tabs)


def _scan_bwd(name, gx, x, tabs, n_ex, seq, after=()):
    n_rows, width = gx.shape
    nj = 4
    cw = width // nj
    half = cw // 2
    tq = META_BLOCK
    seq_blocks = seq // tq
    n_steps = seq_blocks + 1
    rb = _scan_row_block(n_ex * seq_blocks, seq_blocks)
    rbr = lambda b, i: rb(b, n_steps - 1 - i)

    def body(gx_ref, x_ref, tab_ref, g_ref, da_ref, carry_ref):
        @pl.when(pl.program_id(1) == 0)
        def _():
            carry_ref[...] = jnp.zeros_like(carry_ref)
            da_ref[...] = jnp.zeros_like(da_ref)
        row = lax.broadcasted_iota(jnp.int32, (SUBLANES, half), 0)
        n_blk = tq // SUBLANES

        for j in range(nj):
            re, im = slice(j * cw, j * cw + half), slice(j * cw + half, (j + 1) * cw)
            ch = slice(j * half, (j + 1) * half)

            def blk(kk, st, re=re, im=im, ch=ch):
                t = [tab_ref[n * SUBLANES:(n + 1) * SUBLANES, ch] for n in range(8)]
                cr, ci, dar, dai = st
                r0 = pl.multiple_of((n_blk - 1 - kk) * SUBLANES, SUBLANES)
                gr = gx_ref[pl.ds(r0, SUBLANES), re].astype(F32)
                gi = gx_ref[pl.ds(r0, SUBLANES), im].astype(F32)
                for s, d in enumerate((1, 2, 4)):
                    gr, gi = _cmul_add(gr, gi, t[2 * s], t[2 * s + 1],
                                       pltpu.roll(gr, SUBLANES - d, 0), pltpu.roll(gi, SUBLANES - d, 0))
                gr, gi = _cmul_add(gr, gi, t[6], t[7], cr, ci)
                g_ref[pl.ds(r0, SUBLANES), re] = gr.astype(BF16)
                g_ref[pl.ds(r0, SUBLANES), im] = gi.astype(BF16)
                hr = jnp.where(row == SUBLANES - 1, cr, pltpu.roll(gr, SUBLANES - 1, 0))
                hi = jnp.where(row == SUBLANES - 1, ci, pltpu.roll(gi, SUBLANES - 1, 0))
                xr = x_ref[pl.ds(r0, SUBLANES), re].astype(F32)
                xi = x_ref[pl.ds(r0, SUBLANES), im].astype(F32)
                dar = dar + xr * hr + xi * hi
                dai = dai + xr * hi - xi * hr
                return (jnp.broadcast_to(gr[0:1, :], gr.shape), jnp.broadcast_to(gi[0:1, :], gi.shape), dar, dai)

            st = lax.fori_loop(0, n_blk, blk, (carry_ref[0, :, ch], carry_ref[1, :, ch], da_ref[0, :, re], da_ref[0, :, im]),
                               unroll=4)
            carry_ref[0, :, ch] = st[0]
            carry_ref[1, :, ch] = st[1]
            da_ref[0, :, re] = st[2]
            da_ref[0, :, im] = st[3]

    return _call(name, body, (n_ex, n_steps),
                 [pl.BlockSpec((tq, width), lambda b, i: (rbr(b, i), 0)), pl.BlockSpec((tq, width), lambda b, i: (rbr(b, i), 0)),
                  pl.BlockSpec((8 * SUBLANES, nj * half), lambda b, i: (0, 0))],
                 [pl.BlockSpec((tq, width), lambda b, i: (rbr(b, i), 0)), pl.BlockSpec((1, SUBLANES, width), lambda b, i: (b, 0, 0))],
                 [_sds((n_rows, width), BF16), _sds((n_ex, SUBLANES, width), F32)],
                 scratch=[pltpu.VMEM((2, SUBLANES, nj * half), F32)], after=after)(gx, x, tabs)


def _ssm_z(gy, wout_ref, nj):
    return jnp.concatenate([_dot(gy, wout_ref[j]) for j in range(nj)], axis=1)


def _ssm_out(name, x, u, dskip, cb, wout4, h, n_rows):
    nj, xc, uc = cb.shape
    no, hw, oc = wout4.shape
    d = h.shape[1]
    tm = ROW_TILE

    def body(x_ref, u_ref, ds_ref, cb_ref, w_ref, h_ref, o_ref, y_ref):
        y = jnp.concatenate([_dot(x_ref[:, j * xc:(j + 1) * xc], cb_ref[j]) for j in range(nj)], axis=1)
        y = y + ds_ref[...] * u_ref[...]
        y_ref[...] = y
        z = _ssm_z(_gelu(y), w_ref, no)
        o_ref[...] = h_ref[...] + z[:, :d] * _sigmoid(z[:, d:])

    return _call(name, body, (n_rows // tm,),
                 [pl.BlockSpec((tm, nj * xc), lambda i: (i, 0)), pl.BlockSpec((tm, hw), lambda i: (i, 0)),
                  pl.BlockSpec((1, hw), lambda i: (0, 0)), pl.BlockSpec((nj, xc, uc), lambda i: (0, 0, 0)),
                  pl.BlockSpec((no, hw, oc), lambda i: (0, 0, 0)), pl.BlockSpec((tm, d), lambda i: (i, 0))],
                 [pl.BlockSpec((tm, d), lambda i: (i, 0)), pl.BlockSpec((tm, hw), lambda i: (i, 0))],
                 [_sds((n_rows, d), F32), _sds((n_rows, hw), F32)])(x, u, dskip, cb, wout4, h)


def _ssm_out_bwd(name, dh, y, u, cb, wout4, n_rows, after=()):
    nj, xc, uc = cb.shape
    no, hw, oc = wout4.shape
    d = dh.shape[1]
    tm = ROW_TILE

    def body(dh_ref, y_ref, u_ref, cb_ref, w_ref, dy_ref, dz_ref, gx_ref, dd_ref):
        y = y_ref[...]
        z = _ssm_z(_gelu(y), w_ref, no)
        za = z[:, :d]
        sg = _sigmoid(z[:, d:])
        dmix = dh_ref[...]
        dz = jnp.concatenate([dmix * sg, dmix * za * sg * (1.0 - sg)], axis=1).astype(BF16)
        dz_ref[...] = dz
        dgy = _dot_nt(dz[:, 0:oc], w_ref[0])
        for j in range(1, no):
            dgy = dgy + _dot_nt(dz[:, j * oc:(j + 1) * oc], w_ref[j])
        dy = dgy * _gelu_grad(y)
        dy_ref[...] = dy
        _acc_out(dd_ref, jnp.sum(dy * u_ref[...], axis=0, keepdims=True), pl.program_id(0) == 0)
        for j in range(nj):
            gx_ref[:, j * xc:(j + 1) * xc] = _dot_nt(dy[:, j * uc:(j + 1) * uc], cb_ref[j]).astype(BF16)

    return _call(name, body, (n_rows // tm,),
                 [pl.BlockSpec((tm, d), lambda i: (i, 0)), pl.BlockSpec((tm, hw), lambda i: (i, 0)),
                  pl.BlockSpec((tm, hw), lambda i: (i, 0)), pl.BlockSpec((nj, xc, uc), lambda i: (0, 0, 0)),
                  pl.BlockSpec((no, hw, oc), lambda i: (0, 0, 0))],
                 [pl.BlockSpec((tm, hw), lambda i: (i, 0)), pl.BlockSpec((tm, no * oc), lambda i: (i, 0)),
                  pl.BlockSpec((tm, nj * xc), lambda i: (i, 0)), pl.BlockSpec((1, hw), lambda i: (0, 0))],
                 [_sds((n_rows, hw), F32), _sds((n_rows, no * oc), BF16), _sds((n_rows, nj * xc), BF16),
                  _sds((1, hw), F32)], after=after)(dh, y, u, cb, wout4)


def _ssm_in_bwd(name, gbu, dy, dskip, bb, w_in, h, g, dh, n_rows):
    nj, uc, xc = bb.shape
    d, hw = w_in.shape
    tm = ROW_TILE

    def body(gb_ref, dy_ref, ds_ref, bb_ref, w_ref, h_ref, g_ref, dh_ref, du_ref, o_ref, dg_ref):
        du = jnp.concatenate([_dot_nt(gb_ref[:, j * xc:(j + 1) * xc], bb_ref[j]) for j in range(nj)], axis=1)
        du = du + dy_ref[...] * ds_ref[...]
        du_ref[...] = du.astype(BF16)
        dhn, dg = _rms_bwd(h_ref[...], g_ref[...], _dot_nt(du, w_ref[...]))
        o_ref[...] = dh_ref[...] + dhn
        _acc_out(dg_ref, dg, pl.program_id(0) == 0)

    return _call(name, body, (n_rows // tm,),
                 [pl.BlockSpec((tm, nj * xc), lambda i: (i, 0)), pl.BlockSpec((tm, hw), lambda i: (i, 0)),
                  pl.BlockSpec((1, hw), lambda i: (0, 0)), pl.BlockSpec((nj, uc, xc), lambda i: (0, 0, 0)),
                  pl.BlockSpec((d, hw), lambda i: (0, 0)), pl.BlockSpec((tm, d), lambda i: (i, 0)),
                  pl.BlockSpec((1, d), lambda i: (0, 0)), pl.BlockSpec((tm, d), lambda i: (i, 0))],
                 [pl.BlockSpec((tm, hw), lambda i: (i, 0)), pl.BlockSpec((tm, d), lambda i: (i, 0)),
                  pl.BlockSpec((1, d), lambda i: (0, 0))],
                 [_sds((n_rows, hw), BF16), _sds((n_rows, d), F32), _sds((1, d), F32)])(gbu, dy, dskip, bb, w_in, h, g, dh)


def _discretize(lam_re, lam_im, log_step, b_re, b_im):
    step = jnp.exp(log_step)[:, None]
    mag = jnp.exp(lam_re * step)
    ar = mag * jnp.cos(lam_im * step)
    ai = mag * jnp.sin(lam_im * step)
    den = lam_re * lam_re + lam_im * lam_im
    nr, ni = ar - 1.0, ai
    cr = (nr * lam_re + ni * lam_im) / den
    ci = (ni * lam_re - nr * lam_im) / den
    bbar_r = cr[..., None] * b_re - ci[..., None] * b_im
    bbar_i = cr[..., None] * b_im + ci[..., None] * b_re
    return ar, ai, bbar_r, bbar_i


def _ssm_mats(lam_re, lam_im, log_step, b_re, b_im, c_re, c_im):
    n_g, n_p, n_c = b_re.shape
    gpc = n_g // 4
    ar, ai, bbar_r, bbar_i = _discretize(lam_re, lam_im, log_step, b_re, b_im)
    eye = jnp.eye(gpc, dtype=F32)

    def in_map(bbar):
        return jnp.einsum('jgpc,gh->jgchp', bbar.reshape(4, gpc, n_p, n_c), eye).reshape(4, gpc * n_c, gpc * n_p)

    def out_map(c):
        return jnp.einsum('jgcp,gh->jgphc', c.reshape(4, gpc, n_c, n_p), eye).reshape(4, gpc * n_p, gpc * n_c)

    bb = jnp.concatenate([in_map(bbar_r), in_map(bbar_i)], axis=2)
    cb = jnp.concatenate([out_map(c_re), -out_map(c_im)], axis=1)
    return bb, cb, ar.reshape(-1), ai.reshape(-1)


def _chunked(v, half):
    return v.reshape(v.shape[:-1] + (4, half))


def _scan_tables(ar, ai, reverse):
    if reverse:
        ai = -ai
    pr, pi = [ar], [ai]
    for _ in range(SUBLANES - 1):
        pr, pi = pr + [pr[-1] * ar - pi[-1] * ai], pi + [pr[-1] * ai + pi[-1] * ar]
    row = jnp.arange(SUBLANES)[:, None]
    tabs = []
    for d in (1, 2, 4):
        keep = (row <= SUBLANES - 1 - d) if reverse else (row >= d)
        tabs += [jnp.where(keep, pr[d - 1][None, :], 0.0), jnp.where(keep, pi[d - 1][None, :], 0.0)]
    order = list(range(SUBLANES))[::-1] if reverse else list(range(SUBLANES))
    tabs += [jnp.stack([pr[k] for k in order]), jnp.stack([pi[k] for k in order])]
    return jnp.concatenate(tabs, axis=0)


def _kv_proj(name, h, g, w_kv, k_gain_t, cos2, sin2, n_rows, n_kv, hd):
    d, kvw = w_kv.shape
    kw = n_kv * hd
    tm = ROW_TILE

    sel, sel_t = _head_selectors(n_kv, hd)

    def body(h_ref, g_ref, w_ref, kg_ref, c_ref, s_ref, e_ref, et_ref, raw_ref, k_ref, v_ref):
        raw = _dot(_rms(h_ref[...], g_ref[...]), w_ref[...])
        raw_ref[...] = raw
        k_ref[...] = _head_prep(raw[:, :kw], kg_ref[...], c_ref[...], s_ref[...], e_ref[...], et_ref[...], hd).astype(BF16)
        v_ref[...] = raw[:, kw:].astype(BF16)

    return _call(name, body, (n_rows // tm,),
                 [pl.BlockSpec((tm, d), lambda i: (i, 0)), pl.BlockSpec((1, d), lambda i: (0, 0)),
                  pl.BlockSpec((d, kvw), lambda i: (0, 0)), pl.BlockSpec((1, kw), lambda i: (0, 0)),
                  pl.BlockSpec((tm, 2 * hd), lambda i: (i, 0)), pl.BlockSpec((tm, 2 * hd), lambda i: (i, 0)),
                  pl.BlockSpec(sel.shape, lambda i: (0, 0)), pl.BlockSpec(sel_t.shape, lambda i: (0, 0))],
                 [pl.BlockSpec((tm, kvw), lambda i: (i, 0)), pl.BlockSpec((tm, kw), lambda i: (i, 0)),
                  pl.BlockSpec((tm, kw), lambda i: (i, 0))],
                 [_sds((n_rows, kvw), F32), _sds((n_rows, kw), BF16), _sds((n_rows, kw), BF16)])(
                     h, g, w_kv, k_gain_t, cos2, sin2, sel, sel_t)


def _q_proj(name, h, g, w_q, q_gain_t, cos2, sin2, n_rows, n_q, hd):
    d, qw = w_q.shape
    tm = ROW_TILE

    sel, sel_t = _head_selectors(n_q, hd)

    def body(h_ref, g_ref, w_ref, qg_ref, c_ref, s_ref, e_ref, et_ref, raw_ref, q_ref):
        raw = _dot(_rms(h_ref[...], g_ref[...]), w_ref[...])
        raw_ref[...] = raw
        q_ref[...] = _head_prep(raw, qg_ref[...], c_ref[...], s_ref[...], e_ref[...], et_ref[...], hd).astype(BF16)

    return _call(name, body, (n_rows // tm,),
                 [pl.BlockSpec((tm, d), lambda i: (i, 0)), pl.BlockSpec((1, d), lambda i: (0, 0)),
                  pl.BlockSpec((d, qw), lambda i: (0, 0)), pl.BlockSpec((1, qw), lambda i: (0, 0)),
                  pl.BlockSpec((tm, 2 * hd), lambda i: (i, 0)), pl.BlockSpec((tm, 2 * hd), lambda i: (i, 0)),
                  pl.BlockSpec(sel.shape, lambda i: (0, 0)), pl.BlockSpec(sel_t.shape, lambda i: (0, 0))],
                 [pl.BlockSpec((tm, qw), lambda i: (i, 0)), pl.BlockSpec((tm, qw), lambda i: (i, 0))],
                 [_sds((n_rows, qw), F32), _sds((n_rows, qw), BF16)])(h, g, w_q, q_gain_t, cos2, sin2, sel, sel_t)


def _attn_specs(seq, n_ex, n_meta, kw):
    nb = seq // WINDOW
    meta_blk = lambda b: (n_ex * seq + META_BLOCK * b + META_BLOCK - n_meta) // n_meta
    return [pl.BlockSpec((WINDOW, kw), lambda b, n: (b * nb + jnp.maximum(n - 1, 0), 0)),
            pl.BlockSpec((WINDOW, kw), lambda b, n: (b * nb + n, 0)),
            pl.BlockSpec((n_meta, kw), lambda b, n: (meta_blk(b), 0))]


def _attn_bias(qpk, n_keys):
    rows = qpk * WINDOW
    qi = jnp.arange(rows)[:, None] & (WINDOW - 1)
    kj = jnp.arange(n_keys)[None, :]
    rel = qi + WINDOW - kj
    band = (rel >= 0) & (rel < WINDOW)
    meta = kj >= 2 * WINDOW
    first = (band & (kj >= WINDOW)) | meta
    return jnp.where(jnp.stack([first, band | meta]), 0.0, NEG_INF).astype(F32)


def _stack_heads(ref, h, qpk, hd, dtype=None):
    parts = [ref[:, (h * qpk + gq) * hd:(h * qpk + gq + 1) * hd] for gq in range(qpk)]
    out = jnp.concatenate(parts, axis=0)
    return out if dtype is None else out.astype(dtype)


def _col(tile, c):
    lane = lax.broadcasted_iota(jnp.int32, tile.shape, 1)
    return jnp.sum(jnp.where(lane == c, tile, 0.0), axis=-1, keepdims=True)


def _put_col(col, c, n):
    lane = lax.broadcasted_iota(jnp.int32, (col.shape[0], n), 1)
    return jnp.where(lane == c, col, 0.0)


def _stack_cols(tile, h, qpk):
    return jnp.concatenate([_col(tile, h * qpk + gq) for gq in range(qpk)], axis=0)


def _sink_col(sinks, h, qpk):
    return jnp.concatenate([jnp.broadcast_to(_col(sinks, h * qpk + gq), (WINDOW, 1)) for gq in range(qpk)], axis=0)


def _attn_fwd(name, q, k, v, sinks, n_ex, seq, n_meta, n_kv, qpk, hd):
    nb = seq // WINDOW
    n_q = n_kv * qpk
    kw = n_kv * hd
    qw = n_q * hd
    n_keys = 2 * WINDOW + n_meta
    bias = _attn_bias(qpk, n_keys)

    def body(q_ref, kp_ref, kc_ref, km_ref, vp_ref, vc_ref, vm_ref, sk_ref, bias_ref, o_ref, lse_ref):
        sinks_v = sk_ref[...]
        o_parts = []
        lse_all = jnp.zeros((WINDOW, n_q), F32)
        for h in range(n_kv):
            hs = slice(h * hd, (h + 1) * hd)
            kb = jnp.concatenate([kp_ref[:, hs], kc_ref[:, hs], km_ref[:, hs]], axis=0)
            vb = jnp.concatenate([vp_ref[:, hs], vc_ref[:, hs], vm_ref[:, hs]], axis=0)
            for gq in range(qpk):
                c = h * qpk + gq
                s = _dot_nt(q_ref[:, c * hd:(c + 1) * hd], kb) + bias_ref[0, 0:WINDOW, :]
                skc = _col(sinks_v, c)
                m = jnp.maximum(jnp.max(s, axis=-1, keepdims=True), skc)
                p = jnp.exp(s - m)
                den = jnp.sum(p, axis=-1, keepdims=True) + jnp.exp(skc - m)
                o_parts.append(_dot(p, vb) / den)
                lse_all = lse_all + _put_col(m + jnp.log(den), c, n_q)
        o_ref[...] = jnp.concatenate(o_parts, axis=1).astype(BF16)
        lse_ref[...] = lse_all

    qspec = pl.BlockSpec((WINDOW, qw), lambda b, n: (b * nb + n, 0))
    return _call(name, body, (n_ex, nb),
                 [qspec] + _attn_specs(seq, n_ex, n_meta, kw) + _attn_specs(seq, n_ex, n_meta, kw)
                 + [pl.BlockSpec((1, n_q), lambda b, n: (0, 0)),
                    pl.BlockSpec((1,) + bias.shape[1:], lambda b, n: (jnp.minimum(n, 1), 0, 0))],
                 [qspec, pl.BlockSpec((WINDOW, n_q), lambda b, n: (b * nb + n, 0))],
                 [_sds((n_ex * seq, qw), BF16), _sds((n_ex * seq, n_q), F32)])(q, k, k, k, v, v, v, sinks, bias)


def _attn_bwd(name, q, k, v, sinks, o, lse, do, n_ex, seq, n_meta, n_kv, qpk, hd):
    nb = seq // WINDOW
    n_q = n_kv * qpk
    kw = n_kv * hd
    qw = n_q * hd
    n_keys = 2 * WINDOW + n_meta
    bias = _attn_bias(qpk, n_keys)

    def body(q_ref, kp_ref, kc_ref, km_ref, vp_ref, vc_ref, vm_ref, sk_ref, o_ref, lse_ref, do_ref, bias_ref,
             dq_ref, dk_ref, dv_ref, dkm_ref, dvm_ref, dsk_ref):
        n = pl.program_id(1)

        @pl.when(n == 0)
        def _():
            dk_ref[...] = jnp.zeros_like(dk_ref)
            dv_ref[...] = jnp.zeros_like(dv_ref)
            dkm_ref[...] = jnp.zeros_like(dkm_ref)
            dvm_ref[...] = jnp.zeros_like(dvm_ref)

        @pl.when((n == 0) & (pl.program_id(0) == 0))
        def _():
            dsk_ref[...] = jnp.zeros_like(dsk_ref)

        sinks_v = sk_ref[...]
        lse_v = lse_ref[...]
        grp = 2 if qpk % 2 == 0 else 1
        dq_parts, dk_parts, dv_parts = [], [], []
        dsk = jnp.zeros((1, n_q), F32)
        for h in range(n_kv):
            hs = slice(h * hd, (h + 1) * hd)
            kb = jnp.concatenate([kp_ref[:, hs], kc_ref[:, hs], km_ref[:, hs]], axis=0)
            vb = jnp.concatenate([vp_ref[:, hs], vc_ref[:, hs], vm_ref[:, hs]], axis=0)
            dk_h = dv_h = None
            for c0 in range(h * qpk, (h + 1) * qpk, grp):
                heads = range(c0, c0 + grp)
                stack = lambda ref: jnp.concatenate([ref[:, c * hd:(c + 1) * hd] for c in heads], axis=0)
                qs, dos = stack(q_ref), stack(do_ref)
                delta = jnp.sum(dos.astype(F32) * stack(o_ref).astype(F32), axis=-1, keepdims=True)
                lse_c = jnp.concatenate([_col(lse_v, c) for c in heads], axis=0)
                skc = jnp.concatenate([jnp.broadcast_to(_col(sinks_v, c), (WINDOW, 1)) for c in heads], axis=0)
                p = jnp.exp(_dot_nt(qs, kb) + bias_ref[0, 0:grp * WINDOW, :] - lse_c)
                ds = p * (_dot_nt(dos, vb) - delta)
                dqs = _dot(ds, kb)
                dk_g, dv_g = _dot_tn(ds, qs), _dot_tn(p, dos)
                dk_h = dk_g if dk_h is None else dk_h + dk_g
                dv_h = dv_g if dv_h is None else dv_h + dv_g
                dsink = -jnp.exp(skc - lse_c) * delta
                for i, c in enumerate(heads):
                    dq_parts.append(dqs[i * WINDOW:(i + 1) * WINDOW])
                    dsk = dsk + _put_col(jnp.sum(dsink[i * WINDOW:(i + 1) * WINDOW], axis=0, keepdims=True), c, n_q)
            dk_parts.append(dk_h)
            dv_parts.append(dv_h)
        dq_ref[...] = jnp.concatenate(dq_parts, axis=1).astype(BF16)
        dsk_ref[...] += dsk
        dkb = jnp.concatenate(dk_parts, axis=1)
        dvb = jnp.concatenate(dv_parts, axis=1)
        prev = pl.ds(pl.multiple_of(jnp.maximum(n - 1, 0) * WINDOW, WINDOW), WINDOW)
        cur = pl.ds(pl.multiple_of(n * WINDOW, WINDOW), WINDOW)
        dk_ref[prev, :] += dkb[0:WINDOW]
        dv_ref[prev, :] += dvb[0:WINDOW]
        dk_ref[cur, :] += dkb[WINDOW:2 * WINDOW]
        dv_ref[cur, :] += dvb[WINDOW:2 * WINDOW]
        dkm_ref[...] += dkb[2 * WINDOW:]
        dvm_ref[...] += dvb[2 * WINDOW:]

    qspec = pl.BlockSpec((WINDOW, qw), lambda b, n: (b * nb + n, 0))
    exspec = pl.BlockSpec((seq, kw), lambda b, n: (b, 0))
    mspec = pl.BlockSpec((n_meta, kw), lambda b, n: (b, 0))
    return _call(name, body, (n_ex, nb),
                 [qspec] + _attn_specs(seq, n_ex, n_meta, kw) + _attn_specs(seq, n_ex, n_meta, kw)
                 + [pl.BlockSpec((1, n_q), lambda b, n: (0, 0)), qspec,
                    pl.BlockSpec((WINDOW, n_q), lambda b, n: (b * nb + n, 0)), qspec,
                    pl.BlockSpec((1,) + bias.shape[1:], lambda b, n: (jnp.minimum(n, 1), 0, 0))],
                 [qspec, exspec, exspec, mspec, mspec, pl.BlockSpec((1, n_q), lambda b, n: (0, 0))],
                 [_sds((n_ex * seq, qw), BF16), _sds((n_ex * seq, kw), F32), _sds((n_ex * seq, kw), F32),
                  _sds((n_ex * n_meta, kw), F32), _sds((n_ex * n_meta, kw), F32), _sds((1, n_q), F32)])(
                      q, k, k, k, v, v, v, sinks, o, lse, do, bias)


def _attn_out(name, o, h, w_o, n_rows):
    qw, d = w_o.shape
    tm = ROW_TILE

    def body(o_ref, h_ref, w_ref, out_ref):
        out_ref[...] = h_ref[...] + _dot(o_ref[...], w_ref[...])

    return _call(name, body, (n_rows // tm,),
                 [pl.BlockSpec((tm, qw), lambda i: (i, 0)), pl.BlockSpec((tm, d), lambda i: (i, 0)),
                  pl.BlockSpec((qw, d), lambda i: (0, 0))],
                 pl.BlockSpec((tm, d), lambda i: (i, 0)), _sds((n_rows, d), F32))(o, h, w_o)


def _attn_out_bwd(name, dh, w_o, n_rows):
    qw, d = w_o.shape
    tm = ROW_TILE

    def body(dh_ref, w_ref, do_ref):
        do_ref[...] = _dot_nt(dh_ref[...], w_ref[...]).astype(BF16)

    return _call(name, body, (n_rows // tm,),
                 [pl.BlockSpec((tm, d), lambda i: (i, 0)), pl.BlockSpec((qw, d), lambda i: (0, 0))],
                 pl.BlockSpec((tm, qw), lambda i: (i, 0)), _sds((n_rows, qw), BF16))(dh, w_o)


def _q_bwd(name, dq, qraw, q_gain_t, cos2, sin2, w_q, h, g, dh, n_rows, n_q, hd):
    d, qw = w_q.shape
    tm = ROW_TILE

    sel, sel_t = _head_selectors(n_q, hd)

    def body(dq_ref, raw_ref, qg_ref, c_ref, s_ref, e_ref, et_ref, w_ref, h_ref, g_ref, dh_ref, draw_ref, o_ref, dqg_ref, dg_ref):
        dx, dgain = _head_prep_bwd(raw_ref[...], qg_ref[...], c_ref[...], s_ref[...], e_ref[...], et_ref[...],
                                   dq_ref[...].astype(F32), hd)
        draw = dx.astype(BF16)
        draw_ref[...] = draw
        dhn, dg = _rms_bwd(h_ref[...], g_ref[...], _dot_nt(draw, w_ref[...]))
        o_ref[...] = dh_ref[...] + dhn
        first = pl.program_id(0) == 0
        _acc_out(dqg_ref, dgain, first)
        _acc_out(dg_ref, dg, first)

    row = lambda w: pl.BlockSpec((tm, w), lambda i: (i, 0))
    one = lambda w: pl.BlockSpec((1, w), lambda i: (0, 0))
    return _call(name, body, (n_rows // tm,),
                 [row(qw), row(qw), one(qw), row(2 * hd), row(2 * hd), pl.BlockSpec(sel.shape, lambda i: (0, 0)),
                  pl.BlockSpec(sel_t.shape, lambda i: (0, 0)), pl.BlockSpec((d, qw), lambda i: (0, 0)), row(d), one(d), row(d)],
                 [row(qw), row(d), one(qw), one(d)],
                 [_sds((n_rows, qw), BF16), _sds((n_rows, d), F32), _sds((1, qw), F32), _sds((1, d), F32)])(
                     dq, qraw, q_gain_t, cos2, sin2, sel, sel_t, w_q, h, g, dh)


def _kv_bwd(name, dk, dv, kvraw, k_gain_t, cos2, sin2, w_kv, h, g, dh_main, n_rows, n_main, n_kv, hd, after=()):
    d, kvw = w_kv.shape
    kw = n_kv * hd
    tm = ROW_TILE
    n_main_tiles = n_main // tm

    sel, sel_t = _head_selectors(n_kv, hd)

    def body(dk_ref, dv_ref, raw_ref, kg_ref, c_ref, s_ref, e_ref, et_ref, w_ref, h_ref, g_ref, dh_ref, draw_ref, o_ref, dkg_ref,
             dg_ref):
        i = pl.program_id(0)
        dx, dgain = _head_prep_bwd(raw_ref[:, :kw], kg_ref[...], c_ref[...], s_ref[...], e_ref[...], et_ref[...], dk_ref[...], hd)
        draw = jnp.concatenate([dx, dv_ref[...]], axis=1).astype(BF16)
        draw_ref[...] = draw
        dhn, dg = _rms_bwd(h_ref[...], g_ref[...], _dot_nt(draw, w_ref[...]))
        o_ref[...] = jnp.where(i < n_main_tiles, dh_ref[...], 0.0) + dhn
        _acc_out(dkg_ref, dgain, i == 0)
        _acc_out(dg_ref, dg, i == 0)

    row = lambda w: pl.BlockSpec((tm, w), lambda i: (i, 0))
    one = lambda w: pl.BlockSpec((1, w), lambda i: (0, 0))
    return _call(name, body, (n_rows // tm,),
                 [row(kw), row(kw), row(kvw), one(kw), row(2 * hd), row(2 * hd), pl.BlockSpec(sel.shape, lambda i: (0, 0)),
                  pl.BlockSpec(sel_t.shape, lambda i: (0, 0)), pl.BlockSpec((d, kvw), lambda i: (0, 0)), row(d),
                  one(d), pl.BlockSpec((tm, d), lambda i: (jnp.minimum(i, n_main_tiles - 1), 0))],
                 [row(kvw), row(d), one(kw), one(d)],
                 [_sds((n_rows, kvw), BF16), _sds((n_rows, d), F32), _sds((1, kw), F32), _sds((1, d), F32)], after=after)(
                     dk, dv, kvraw, k_gain_t, cos2, sin2, sel, sel_t, w_kv, h, g, dh_main)


def _tn_rms(name, h, g, b, n_rows, out_dtype=F32):
    d = h.shape[1]
    nb = b.shape[1]
    tk = _contract_tile(n_rows)
    return _tn(name, (h, g, b),
               [pl.BlockSpec((tk, d), lambda j, k: (k, 0)), pl.BlockSpec((1, d), lambda j, k: (0, 0)),
                pl.BlockSpec((tk, nb), lambda j, k: (k, 0))],
               lambda j, h_ref, g_ref, b_ref: (_rms(h_ref[...], g_ref[...]), b_ref[...]), 1, d, nb, n_rows, tk, out_dtype=out_dtype)


def _tn_plain(name, a, b, nj, a_cols, b_cols, n_rows, a_fn=None, out_dtype=F32, after=()):
    tk = _contract_tile(n_rows)
    fa = (lambda v: v) if a_fn is None else a_fn
    a_map = (lambda j, k: (k, j)) if a.shape[1] != a_cols else (lambda j, k: (k, 0))
    b_map = (lambda j, k: (k, j)) if b.shape[1] != b_cols else (lambda j, k: (k, 0))
    return _tn(name, (a, b), [pl.BlockSpec((tk, a_cols), a_map), pl.BlockSpec((tk, b_cols), b_map)],
               lambda j, a_ref, b_ref: (fa(a_ref[...]), b_ref[...]), nj, a_cols, b_cols, n_rows, tk, out_dtype=out_dtype,
               after=after)


def _cast_layer(name, a, layer):
    _, r, c = a.shape
    tr = _row_tile(r, 256)

    def body(a_ref, o_ref):
        o_ref[...] = a_ref[0].astype(BF16)

    return _call(name, body, (r // tr,), [pl.BlockSpec((1, tr, c), lambda i: (layer, i, 0))],
                 pl.BlockSpec((tr, c), lambda i: (i, 0)), _sds((r, c), BF16))(a)


def _adamw_math(w, g, m, v):
    c1 = 1.0 - ADAM_B1 ** ADAM_STEP
    c2 = 1.0 - ADAM_B2 ** ADAM_STEP
    nm = ADAM_B1 * m + (1.0 - ADAM_B1) * g
    nv = ADAM_B2 * v + (1.0 - ADAM_B2) * (g * g)
    return -ADAM_LR * ((nm / c1) / (jnp.sqrt(nv / c2) + ADAM_EPS) + ADAM_WD * w), nm, nv


def _adamw(name, w, g, m, v, after=()):
    rows, cols = w.shape
    tr = 128 if rows % 128 == 0 else rows

    def body(w_ref, g_ref, m_ref, v_ref, d_ref, nm_ref, nv_ref):
        d_ref[...], nm_ref[...], nv_ref[...] = _adamw_math(w_ref[...], g_ref[...], m_ref[...], v_ref[...])

    spec = pl.BlockSpec((tr, cols), lambda i: (i, 0))
    return _call(name, body, (rows // tr,), [spec] * 4, [spec] * 3, [_sds((rows, cols), F32)] * 3, after=after)(w, g, m, v)


def _adamw_from_halves(name, w, m, v, sources, half_index, transposed, after=()):
    n_layers, r, c = w.shape
    lanes = 1024
    after = tuple(after)
    if transposed:
        rows_half, tr = c // 2, 128
        grid = (n_layers, r // tr)
        w_spec = pl.BlockSpec((1, tr, c), lambda l, i, s: (l, i, 0))
        g_spec = lambda off: pl.BlockSpec((rows_half, tr), lambda l, i, s: (off // rows_half, i))
    else:
        rows_half = r // 2
        grid = (n_layers, 2)
        w_spec = pl.BlockSpec((1, rows_half, c), lambda l, k, s: (l, k, 0))
        g_spec = lambda off: pl.BlockSpec((rows_half, lanes), lambda l, k, s: (off // rows_half, 0))

    def body(s_ref, w_ref, m_ref, v_ref, t0_ref, o0_ref, t1_ref, o1_ref, *rest):
        g_ref, d_ref, nm_ref, nv_ref = rest[len(after):]
        layer, k, mine = pl.program_id(0), pl.program_id(1), s_ref[0]
        tot = jnp.where(layer == 0, t0_ref[...], t1_ref[...])
        oth = jnp.where(layer == 0, o0_ref[...], o1_ref[...])
        if transposed:
            g = jnp.concatenate([jnp.where(mine == 0, tot, oth), jnp.where(mine == 0, oth, tot)], axis=0).T
        else:
            g = jnp.where(k == mine, tot, oth)
        g_ref[0] = g
        d_ref[0], nm_ref[0], nv_ref[0] = _adamw_math(w_ref[0], g, m_ref[0], v_ref[0])

    (t0, o0, off0), (t1, o1, off1) = sources
    grid_spec = pltpu.PrefetchScalarGridSpec(
        num_scalar_prefetch=1, grid=grid,
        in_specs=[w_spec] * 3 + [g_spec(off0), g_spec(off0), g_spec(off1), g_spec(off1)] + [_ANY] * len(after),
        out_specs=[w_spec] * 4)
    return pl.pallas_call(
        body, name=name, grid_spec=grid_spec, out_shape=[_sds(w.shape, F32)] * 4,
        compiler_params=pltpu.CompilerParams(dimension_semantics=("arbitrary", "arbitrary"),
                                             vmem_limit_bytes=V7X_VMEM_LIMIT))(half_index, w, m, v, t0, o0, t1, o1, *after)


def _position():
    return lax.axis_index("x"), lax.axis_index("y"), lax.axis_index("c")


def _other_chips(x, y):
    return [(1 - x, y), (x, 1 - y), (1 - x, 1 - y)]


def _peers_chips(x, y, c):
    return [(cx, cy, c) for cx, cy in _other_chips(x, y)]


def _peers_sibling(x, y, c):
    return [(x, y, 1 - c)]


def _peers_chips_and_sibling(x, y, c):
    return _peers_chips(x, y, c) + _peers_sibling(x, y, c)


def _comm_call(name, body, n_in, out_shape, scratch, sequencer=None):
    if sequencer is None:
        return pl.pallas_call(
            body, name=name, in_specs=[_HBM] * n_in, out_specs=[_HBM] * len(out_shape), out_shape=out_shape,
            scratch_shapes=list(scratch),
            compiler_params=pltpu.CompilerParams(has_side_effects=True, vmem_limit_bytes=V7X_VMEM_LIMIT))
    collective_id, peers = sequencer

    def seq_body(*refs):
        barrier = pltpu.get_barrier_semaphore()
        plist = peers(*_position())
        for peer in plist:
            pl.semaphore_signal(barrier, inc=1, device_id=peer, device_id_type=MESH)
        pl.semaphore_wait(barrier, len(plist))
        body(*refs)

    return pl.kernel(seq_body, out_type=out_shape, mesh=plsc.ScalarSubcoreMesh(axis_name="sequencer", num_cores=1), name=name,
                     scratch_types=list(scratch), compiler_params=pltpu.CompilerParams(collective_id=collective_id))


def _n_chunks(rows, want, dtype):
    align = 16 if dtype == BF16 else 8
    n = want
    while n > 1 and (rows % n or (rows // n) % align):
        n -= 1
    return n


def _remote(src, dst, send_sem, recv_sem, device):
    return pltpu.make_async_remote_copy(src_ref=src, dst_ref=dst, send_sem=send_sem, recv_sem=recv_sem,
                                        device_id=device, device_id_type=MESH)


def _start_in_chunks(src, dst, send_sem, recv_sem, device, want=8):
    rows = src.shape[0]
    n = _n_chunks(rows, want, src.dtype)
    for i in range(n):
        part = pl.ds(i * (rows // n), rows // n)
        _remote(src.at[part], dst.at[part], send_sem, recv_sem, device).start()


def _all_gather_chips(name, shards, split, collective_id=None):
    n = len(shards)

    def body(*refs):
        ins, outs = refs[:n], refs[n:2 * n]
        send_sems, recv_sems, local_sems = refs[2 * n:]
        x, y, c = _position()
        me = 2 * x + y
        chips = _other_chips(x, y)
        sibling = (x, y, 1 - c)
        sends, forwards = [], []
        for t in range(n):
            pltpu.make_async_copy(ins[t], outs[t].at[me], local_sems.at[t]).start()
        for t in range(n):
            r = ins[t].shape[0]
            rows = pl.ds(c * (r // 2), r // 2) if split[t] else pl.ds(0, r)
            for k, (cx, cy) in enumerate(chips):
                src, dst = ins[t].at[rows], outs[t].at[me, rows]
                _start_in_chunks(src, dst, send_sems.at[t, k], recv_sems.at[t, k], (cx, cy, c), want=4)
                sends.append(_remote(src, dst, send_sems.at[t, k], recv_sems.at[t, k], (cx, cy, c)))
        for t in range(n):
            r = ins[t].shape[0]
            rows = pl.ds(c * (r // 2), r // 2) if split[t] else pl.ds(0, r)
            for k, (cx, cy) in enumerate(chips):
                landed = outs[t].at[2 * cx + cy, rows]
                _remote(landed, landed, send_sems.at[t, k], recv_sems.at[t, k], (cx, cy, c)).wait_recv()
                if split[t]:
                    _start_in_chunks(landed, landed, send_sems.at[t, 3 + k], recv_sems.at[t, 3 + k], sibling, want=4)
                    forwards.append(_remote(landed, landed, send_sems.at[t, 3 + k], recv_sems.at[t, 3 + k], sibling))
        for t in range(n):
            if split[t]:
                r = ins[t].shape[0]
                other = pl.ds((1 - c) * (r // 2), r // 2)
                for k, (cx, cy) in enumerate(chips):
                    landed = outs[t].at[2 * cx + cy, other]
                    pltpu.make_async_remote_copy(
                        src_ref=landed, dst_ref=landed, send_sem=send_sems.at[t, 3 + k], recv_sem=recv_sems.at[t, 3 + k],
                        device_id=sibling, device_id_type=MESH).wait_recv()
        for cp in sends + forwards:
            cp.wait_send()
        for t in range(n):
            pltpu.make_async_copy(ins[t], outs[t].at[me], local_sems.at[t]).wait()

    out_shape = [_sds((N_CHIPS,) + s.shape, s.dtype) for s in shards]
    sequencer = None if collective_id is None else (collective_id, _peers_chips_and_sibling)
    return _comm_call(name, body, n, out_shape,
                      [pltpu.SemaphoreType.DMA((n, 6)), pltpu.SemaphoreType.DMA((n, 6)), pltpu.SemaphoreType.DMA((n,))],
                      sequencer)(*shards)


def _swap_halves_with_sibling(name, blob, collective_id=None):
    def body(b_ref, theirs_ref, send_sem, recv_sem):
        x, y, c = _position()
        sibling = (x, y, 1 - c)
        for k in range(b_ref.shape[1]):
            _start_in_chunks(b_ref.at[1 - c, k], theirs_ref.at[k], send_sem, recv_sem, sibling)
        _remote(b_ref.at[1 - c], theirs_ref, send_sem, recv_sem, sibling).wait()

    return _comm_call(name, body, 1, [_sds(blob.shape[1:], blob.dtype)],
                      [pltpu.SemaphoreType.DMA(()), pltpu.SemaphoreType.DMA(())],
                      None if collective_id is None else (collective_id, _peers_sibling))(blob)[0]


def _scatter_to_chips(name, parts, collective_id=None):
    def body(p_ref, o_ref, send_sems, recv_sems, local_sems):
        x, y, c = _position()
        me = 2 * x + y
        rows = p_ref.shape[1]
        n_loc = _n_chunks(rows, 16, p_ref.dtype)
        locs = [pltpu.make_async_copy(p_ref.at[me, pl.ds(i * (rows // n_loc), rows // n_loc)],
                                      o_ref.at[me, pl.ds(i * (rows // n_loc), rows // n_loc)], local_sems.at[i])
                for i in range(n_loc)]
        for loc in locs:
            loc.start()
        sends = []
        for k, (cx, cy) in enumerate(_other_chips(x, y)):
            src, dst = p_ref.at[2 * cx + cy], o_ref.at[me]
            _start_in_chunks(src, dst, send_sems.at[k], recv_sems.at[k], (cx, cy, c))
            sends.append(_remote(src, dst, send_sems.at[k], recv_sems.at[k], (cx, cy, c)))
        for k, (cx, cy) in enumerate(_other_chips(x, y)):
            landed = o_ref.at[2 * cx + cy]
            _remote(landed, landed, send_sems.at[k], recv_sems.at[k], (cx, cy, c)).wait_recv()
        for cp in sends:
            cp.wait_send()
        for loc in locs:
            loc.wait()

    def local_sems_shape(rows):
        return pltpu.SemaphoreType.DMA((_n_chunks(rows, 16, parts.dtype),))

    return _comm_call(name, body, 1, [_sds(parts.shape, parts.dtype)],
                      [pltpu.SemaphoreType.DMA((3,)), pltpu.SemaphoreType.DMA((3,)), local_sems_shape(parts.shape[1])],
                      None if collective_id is None else (collective_id, _peers_chips))(parts)[0]


def _share_with_sibling(name, mine, collective_id=None):
    def body(m_ref, o_ref, send_sem, recv_sem):
        x, y, c = _position()
        sibling = (x, y, 1 - c)
        _start_in_chunks(m_ref, o_ref, send_sem, recv_sem, sibling, want=16)
        _remote(m_ref, o_ref, send_sem, recv_sem, sibling).wait()

    return _comm_call(name, body, 1, [_sds(mine.shape, mine.dtype)],
                      [pltpu.SemaphoreType.DMA(()), pltpu.SemaphoreType.DMA(())],
                      None if collective_id is None else (collective_id, _peers_sibling))(mine)[0]


def _row_tile(rows, cap=640):
    best = rows
    for t in range(16, min(rows, cap) + 1, 16):
        if rows % t == 0:
            best = t
    return best


_ANY = pl.BlockSpec(memory_space=pl.ANY)


def _add_my_half(name, blob, theirs, half_index, out_dtype, after):
    n, rows, cols = theirs.shape
    tr = _row_tile(rows)
    after = tuple(after)

    def body(c_ref, a_ref, b_ref, *rest):
        o_ref = rest[-1]
        o_ref[...] = (a_ref[0].astype(F32) + b_ref[...].astype(F32)).astype(out_dtype)

    spec = pl.BlockSpec((1, tr, cols), lambda k, i, c: (k, i, 0))
    grid_spec = pltpu.PrefetchScalarGridSpec(
        num_scalar_prefetch=1, grid=(n, rows // tr),
        in_specs=[pl.BlockSpec((1, 1, tr, cols), lambda k, i, c: (c[0], k, i, 0)), spec] + [_ANY] * len(after), out_specs=spec)
    return pl.pallas_call(
        body, name=name, grid_spec=grid_spec, out_shape=_sds(theirs.shape, out_dtype),
        compiler_params=pltpu.CompilerParams(dimension_semantics=("arbitrary", "arbitrary"),
                                             vmem_limit_bytes=V7X_VMEM_LIMIT))(half_index, blob, theirs, *after)


def _sum_slots(name, parts, after):
    n, rows, cols = parts.shape
    tr = _row_tile(rows)

    def body(p_ref, o_ref):
        acc = p_ref[0].astype(F32)
        for k in range(1, n):
            acc = acc + p_ref[k].astype(F32)
        o_ref[...] = acc

    return _call(name, body, (rows // tr,), [pl.BlockSpec((n, tr, cols), lambda i: (0, i, 0))],
                 pl.BlockSpec((tr, cols), lambda i: (i, 0)), _sds((rows, cols), F32), after=after)(parts)


def _reduce_small_adamw(name, grads, loss_tile, ws, ms, vs, after=()):
    n = len(grads)
    srcs = list(grads) + [loss_tile]
    after = tuple(after)

    def body(*refs):
        refs = refs[:4 * n + 1] + refs[4 * n + 1 + len(after):]
        g_in, w_in, m_in, v_in = refs[:n + 1], refs[n + 1:2 * n + 1], refs[2 * n + 1:3 * n + 1], refs[3 * n + 1:4 * n + 1]
        outs = refs[4 * n + 1:8 * n + 2]
        g_out, d_out, nm_out, nv_out, loss_out = outs[:n], outs[n:2 * n], outs[2 * n:3 * n], outs[3 * n:4 * n], outs[4 * n]
        bufs = refs[8 * n + 2:9 * n + 3]
        send_sems, recv_sems = refs[9 * n + 3:]
        x, y, c = _position()
        me = 4 * x + 2 * y + c
        chip = 2 * x + y
        peers = [(1 - x if dlt & 4 else x, 1 - y if dlt & 2 else y, 1 - c if dlt & 1 else c) for dlt in range(1, N_DEV)]
        sends = []
        for t in range(n + 1):
            bufs[t][me] = g_in[t][...]
            for k, peer in enumerate(peers):
                cp = _remote(g_in[t], bufs[t].at[me], send_sems.at[t, k], recv_sems.at[t, k], peer)
                cp.start()
                sends.append(cp)
        for t in range(n + 1):
            for k, (tx, ty, tc) in enumerate(peers):
                landed = bufs[t].at[4 * tx + 2 * ty + tc]
                _remote(landed, landed, send_sems.at[t, k], recv_sems.at[t, k], (tx, ty, tc)).wait_recv()
        for cp in sends:
            cp.wait_send()
        for t in range(n + 1):
            total = bufs[t][0]
            for k in range(1, N_DEV):
                total = total + bufs[t][k]
            if t == n:
                loss_out[...] = total
                continue
            cols = w_in[t].shape[1]
            if cols == total.shape[1]:
                g_out[t][...] = total
                d_out[t][...], nm_out[t][...], nv_out[t][...] = _adamw_math(w_in[t][...], total, m_in[t][...], v_in[t][...])
            else:
                for j in range(N_CHIPS):
                    @pl.when(chip == j)
                    def _(t=t, j=j, cols=cols, total=total):
                        mine = total[:, j * cols:(j + 1) * cols]
                        g_out[t][...] = mine
                        d_out[t][...], nm_out[t][...], nv_out[t][...] = _adamw_math(w_in[t][...], mine, m_in[t][...], v_in[t][...])

    w_shapes = [_sds(a.shape, F32) for a in ws]
    return pl.pallas_call(
        body, name=name, in_specs=[_VMEM] * (4 * n + 1) + [_ANY] * len(after), out_specs=[_VMEM] * (4 * n + 1),
        out_shape=w_shapes * 4 + [_sds(loss_tile.shape, F32)],
        scratch_shapes=[pltpu.VMEM((N_DEV,) + a.shape, F32) for a in srcs]
        + [pltpu.SemaphoreType.DMA((n + 1, N_DEV - 1)), pltpu.SemaphoreType.DMA((n + 1, N_DEV - 1))],
        compiler_params=pltpu.CompilerParams(has_side_effects=True, vmem_limit_bytes=V7X_VMEM_LIMIT))(
            *srcs, *ws, *ms, *vs, *after)


_BIG = ("ffn1_w_gate_up", "ffn1_w_down", "ffn2_w_gate_up", "ffn2_w_down", "ssm_w_in", "ssm_w_out", "w_kv", "attn_w_q", "attn_w_o")
_TRANSPOSED = ("ffn1_w_gate_up", "ffn2_w_gate_up")
_FROM_HALVES = _TRANSPOSED + ("ffn1_w_down", "ffn2_w_down")
_SMALL = ("meta_tokens", "ffn1_norm", "mix_norm", "ffn2_norm", "ssm_lambda_re", "ssm_lambda_im", "ssm_b_re", "ssm_b_im",
          "ssm_c_re", "ssm_c_im", "ssm_log_step", "ssm_d", "kv_norm", "k_norm", "q_norm", "attn_sinks")
_ORDER = ("meta_tokens", "ffn1_norm", "ffn1_w_gate_up", "ffn1_w_down", "mix_norm", "ffn2_norm", "ffn2_w_gate_up", "ffn2_w_down",
          "ssm_w_in", "ssm_lambda_re", "ssm_lambda_im", "ssm_b_re", "ssm_b_im", "ssm_c_re", "ssm_c_im", "ssm_log_step", "ssm_d",
          "ssm_w_out", "kv_norm", "w_kv", "k_norm", "attn_w_q", "q_norm", "attn_sinks", "attn_w_o")


def _step(x, target, w, m, v):
    n_ex, seq, d = x.shape
    n_meta = w["meta_tokens"].shape[0]
    n_main = n_ex * seq
    n_all = n_main + n_ex * META_BLOCK
    n_g, n_p, n_c = w["ssm_b_re"].shape[1:]
    hd = w["k_norm"].shape[0]
    n_kv = w["w_kv"].shape[1] // (2 * hd)
    n_q = w["attn_w_q"].shape[2] // hd
    qpk = n_q // n_kv
    px, py, pc = _position()
    chip = 2 * px + py

    def cast(name, layer=0):
        a = w[name]
        return _cast_layer(f"cast_{name}_{layer}", a if a.ndim == 3 else a[None], layer)

    g_a = _all_gather_chips("gather_first", [cast("ffn1_w_gate_up"), w["meta_tokens"], w["ssm_d"]], [True, False, False],
                            collective_id=12)
    g_d = _all_gather_chips("gather_next", [cast("ffn1_w_down"), cast("ssm_w_in"), cast("ssm_w_out")], [True] * 3, collective_id=13)
    second = [cast("ffn2_w_gate_up"), cast("ffn2_w_down"), cast("w_kv")]
    g_b = _all_gather_chips("gather_second", second, [True] * 3, collective_id=1)
    third = [cast("ffn1_w_gate_up", 1), cast("ffn1_w_down", 1), cast("attn_w_q"), cast("attn_w_o"),
             cast("ffn2_w_gate_up", 1), cast("ffn2_w_down", 1)]
    g_c = _all_gather_chips("gather_third", third, [True] * 6, collective_id=2)
    wgu = {("ffn1", 0): g_a[0], ("ffn1", 1): g_c[0], ("ffn2", 0): g_b[0], ("ffn2", 1): g_c[4]}
    wd = {("ffn1", 0): g_d[0], ("ffn1", 1): g_c[1], ("ffn2", 0): g_b[1], ("ffn2", 1): g_c[5]}
    wd = {key: a.reshape(-1, d) for key, a in wd.items()}
    w_in = g_d[1].reshape(d, -1)
    wout4 = g_d[2]
    w_q = g_c[2].reshape(d, -1)
    w_o = g_c[3].reshape(-1, d)
    w_kv = g_b[2].reshape(d, -1)
    meta_full = jnp.transpose(g_a[1], (1, 0, 2)).reshape(n_meta, d)
    dskip = g_a[2].reshape(1, -1)

    row1 = lambda a: a.reshape(1, -1)
    ssm_args = tuple(w[k][0] for k in ("ssm_lambda_re", "ssm_lambda_im", "ssm_log_step", "ssm_b_re", "ssm_b_im", "ssm_c_re", "ssm_c_im"))
    (bb, cb, a_re, a_im), ssm_vjp = jax.vjp(_ssm_mats, *ssm_args)
    bb16, cb16 = bb.astype(BF16), cb.astype(BF16)
    a_re_s, a_im_s = lax.stop_gradient(a_re), lax.stop_gradient(a_im)
    half = n_g * n_p // 4
    tabs_f = _scan_tables(a_re_s, a_im_s, False)
    tabs_b = _scan_tables(a_re_s, a_im_s, True)

    freqs = ROPE_THETA ** (-jnp.arange(0, hd // 2, dtype=F32) * 2.0 / hd)
    pos_main = jnp.tile(n_meta + jnp.arange(seq), n_ex)
    pos_meta = jnp.tile(jnp.maximum(jnp.arange(META_BLOCK) - (META_BLOCK - n_meta), 0), n_ex)
    ang = jnp.concatenate([pos_main, pos_meta]).astype(F32)[:, None] * freqs[None, :]
    cos = jnp.concatenate([jnp.cos(ang), jnp.cos(ang)] * 2, axis=1)
    sin_s = jnp.concatenate([-jnp.sin(ang), jnp.sin(ang)] * 2, axis=1)
    k_gain_t = jnp.tile(row1(w["k_norm"]), (1, n_kv))
    score_scale = hd ** -0.5
    q_gain_t = jnp.tile(row1(w["q_norm"][0]), (1, n_q)) * score_scale

    meta_block = jnp.concatenate([jnp.zeros((META_BLOCK - n_meta, d), F32), meta_full], axis=0)
    h0 = jnp.concatenate([x.reshape(n_main, d)] + [meta_block] * n_ex, axis=0)

    g = lambda name, layer: row1(w[name][layer])
    h1, gu1 = _ffn_fwd("l0_ffn1", h0, g("ffn1_norm", 0), wgu["ffn1", 0], wd["ffn1", 0], n_all)
    u, bu = _ssm_in("ssm_in", h1, g("mix_norm", 0), w_in, bb16, n_all)
    xs = _scan_fwd("ssm_scan", bu, tabs_f, n_ex, seq)
    h2, y = _ssm_out("ssm_out", xs, u, dskip, cb16, wout4, h1, n_all)
    h3, gu2 = _ffn_fwd("l0_ffn2", h2, g("ffn2_norm", 0), wgu["ffn2", 0], wd["ffn2", 0], n_all)
    kvraw, k, vv = _kv_proj("kv_proj", h3, row1(w["kv_norm"]), w_kv, k_gain_t, cos, sin_s, n_all, n_kv, hd)
    h4, gu3 = _ffn_fwd("l1_ffn1", h3, g("ffn1_norm", 1), wgu["ffn1", 1], wd["ffn1", 1], n_main)
    qraw, q = _q_proj("q_proj", h4, g("mix_norm", 1), w_q, q_gain_t, cos, sin_s, n_main, n_q, hd)
    sinks = row1(w["attn_sinks"][0])
    o, lse = _attn_fwd("attn_fwd", q, k, vv, sinks, n_ex, seq, n_meta, n_kv, qpk, hd)
    h5 = _attn_out("attn_out", o, h4, w_o, n_main)
    (dh6, loss_tile), gu4 = _ffn_fwd("l1_ffn2", h5, g("ffn2_norm", 1), wgu["ffn2", 1], wd["ffn2", 1], n_main,
                                     target=target.reshape(n_main, d))

    lanes = 1024

    def rs_start(tag, entries, ids):
        pieces = [gr.reshape(N_CHIPS, 2, -1, lanes) for _, _, gr in entries]
        blob = jnp.transpose(jnp.concatenate(pieces, axis=2), (1, 0, 2, 3)).astype(BF16)
        return dict(tag=tag, entries=entries, ids=ids, blob=blob, theirs=_swap_halves_with_sibling(tag + "_swap", blob, ids[0]))

    def rs_scatter(st, after):
        chip_sum = _add_my_half(st["tag"] + "_chip_sum", st["blob"], st["theirs"], jnp.reshape(pc, (1,)).astype(jnp.int32), BF16, after)
        st["chip_sum"] = chip_sum
        st["landed"] = _scatter_to_chips(st["tag"] + "_scatter", chip_sum, st["ids"][1])

    def rs_finish(st, after):
        total = _sum_slots(st["tag"] + "_sum", st["landed"], after)
        st["total"] = total
        other = _share_with_sibling(st["tag"] + "_share", total, st["ids"][2])
        halves = (jnp.where(pc == 0, total, other), jnp.where(pc == 0, other, total))
        out, off = {}, 0
        for name, layer, gr in st["entries"]:
            rows = gr.shape[1] * gr.shape[2] // lanes // 2
            if name in _FROM_HALVES:
                out[name, layer] = (total, other, off)
            else:
                out[name, layer] = jnp.concatenate([hv[off:off + rows].reshape(-1) for hv in halves])
            off += rows
        return out

    small = {}
    dh5, dg_f2l1, dwgu_f2l1, dwd_f2l1 = _ffn_bwd("l1_ffn2", dh6, h5, g("ffn2_norm", 1), gu4, wgu["ffn2", 1], wd["ffn2", 1], n_main)
    do = _attn_out_bwd("attn_out_bwd", dh5, w_o, n_main)
    dw_o = _tn_plain("attn_dwo", o, dh5, 1, o.shape[1], d, n_main, out_dtype=BF16).reshape(N_CHIPS, -1, d)
    dq, dk_main, dv_main, dk_meta, dv_meta, dsinks = _attn_bwd("attn_bwd", q, k, vv, sinks, o, lse, do, n_ex, seq, n_meta, n_kv, qpk, hd)
    dqraw, dh4, dq_gain, dg_mix1 = _q_bwd("q_bwd", dq, qraw, q_gain_t, cos, sin_s, w_q, h4, g("mix_norm", 1), dh5, n_main, n_q, hd)
    dw_q = _tn_rms("attn_dwq", h4, g("mix_norm", 1), dqraw, n_main, out_dtype=BF16).reshape(N_CHIPS, -1, dqraw.shape[1])
    dh3m, dg_f1l1, dwgu_f1l1, dwd_f1l1 = _ffn_bwd("l1_ffn1", dh4, h3, g("ffn1_norm", 1), gu3, wgu["ffn1", 1], wd["ffn1", 1], n_main)
    rs1 = rs_start("rs1", [("ffn2_w_gate_up", 1, dwgu_f2l1), ("ffn1_w_gate_up", 1, dwgu_f1l1), ("ffn2_w_down", 1, dwd_f2l1),
                           ("ffn1_w_down", 1, dwd_f1l1), ("attn_w_o", 0, dw_o), ("attn_w_q", 0, dw_q)], (3, 4, 5))

    def with_meta(main, meta):
        blocks = [jnp.pad(meta[b * n_meta:(b + 1) * n_meta], ((META_BLOCK - n_meta, 0), (0, 0))) for b in range(n_ex)]
        return jnp.concatenate([main] + blocks, axis=0)

    dkvraw, dh3, dk_gain, dg_kv = _kv_bwd("kv_bwd", with_meta(dk_main, dk_meta), with_meta(dv_main, dv_meta), kvraw, k_gain_t,
                                          cos, sin_s, w_kv, h3, row1(w["kv_norm"]), dh3m, n_all, n_main, n_kv, hd,
                                          after=(rs1["blob"],))
    rs_scatter(rs1, after=(dh3,))
    dw_kv = _tn_rms("kv_dw", h3, row1(w["kv_norm"]), dkvraw, n_all, out_dtype=BF16).reshape(N_CHIPS, -1, dkvraw.shape[1])
    dh2, dg_f2l0, dwgu_f2l0, dwd_f2l0 = _ffn_bwd("l0_ffn2", dh3, h2, g("ffn2_norm", 0), gu2, wgu["ffn2", 0], wd["ffn2", 0], n_all,
                                                 after=(rs1["chip_sum"],))
    reduced = rs_finish(rs1, after=(dh2, dwgu_f2l0, dwd_f2l0, dw_kv))
    rs0a = rs_start("rs0a", [("ffn2_w_gate_up", 0, dwgu_f2l0), ("ffn2_w_down", 0, dwd_f2l0), ("w_kv", 0, dw_kv)], (6, 7, 8))

    dy, dz, gx, dd = _ssm_out_bwd("ssm_out_bwd", dh2, y, u, cb16, wout4, n_all, after=(rs1["total"], rs0a["blob"]))
    rs_scatter(rs0a, after=(dy,))
    hw = y.shape[1]
    oc = wout4.shape[2]
    dw_out = _tn_plain("ssm_dwout", y, dz, wout4.shape[0], hw, oc, n_all, a_fn=_gelu, out_dtype=BF16)
    gbu, da = _scan_bwd("ssm_scan_bwd", gx, xs, tabs_b, n_ex, seq, after=(rs0a["chip_sum"],))
    du, dh1, dg_mix0 = _ssm_in_bwd("ssm_in_bwd", gbu, dy, dskip, bb16, w_in, h1, g("mix_norm", 0), dh2, n_all)
    reduced.update(rs_finish(rs0a, after=(dh1,)))
    dw_in = _tn_rms("ssm_dwin", h1, g("mix_norm", 0), du, n_all, out_dtype=BF16).reshape(N_CHIPS, -1, hw)
    (dh0, dh0_meta), dg_f1l0, dwgu_f1l0, dwd_f1l0 = _ffn_bwd("l0_ffn1", dh1, h0, g("ffn1_norm", 0), gu1, wgu["ffn1", 0], wd["ffn1", 0],
                                                             n_all, n_main, after=(rs0a["total"],))
    rs0b = rs_start("rs0b", [("ffn1_w_gate_up", 0, dwgu_f1l0), ("ffn1_w_down", 0, dwd_f1l0), ("ssm_w_out", 0, dw_out),
                             ("ssm_w_in", 0, dw_in)], (9, 10, 11))
    dcb = _tn_plain("ssm_dcb", xs, dy, 4, xs.shape[1] // 4, hw // 4, n_all, after=(rs0b["blob"],))
    rs_scatter(rs0b, after=(dcb,))
    dbb = _tn_plain("ssm_dbb", u, gbu, 4, hw // 4, gbu.shape[1] // 4, n_all, after=(rs0b["chip_sum"],))

    grad_x = dh0.reshape(n_ex, seq, d)
    da_sum = jnp.sum(da, axis=(0, 1)).reshape(4, 2, half)
    d_ssm = ssm_vjp((dbb, dcb, da_sum[:, 0].reshape(-1), da_sum[:, 1].reshape(-1)))
    for key, val in zip(("ssm_lambda_re", "ssm_lambda_im", "ssm_log_step", "ssm_b_re", "ssm_b_im", "ssm_c_re", "ssm_c_im"), d_ssm):
        small[key] = val[None]
    small["meta_tokens"] = sum(dh0_meta[META_BLOCK * (b + 1) - n_meta:META_BLOCK * (b + 1)] for b in range(n_ex))
    small["ffn1_norm"] = jnp.concatenate([dg_f1l0, dg_f1l1], axis=0)
    small["ffn2_norm"] = jnp.concatenate([dg_f2l0, dg_f2l1], axis=0)
    small["mix_norm"] = jnp.concatenate([dg_mix0, dg_mix1], axis=0)
    small["ssm_d"] = dd
    small["kv_norm"] = dg_kv.reshape(-1)
    small["k_norm"] = jnp.sum(dk_gain.reshape(n_kv, hd), axis=0)
    small["q_norm"] = jnp.sum(dq_gain.reshape(n_q, hd), axis=0, keepdims=True) * score_scale
    small["attn_sinks"] = dsinks

    def view(name, a):
        if name in ("ssm_b_re", "ssm_b_im"):
            return a.reshape(-1, 128)
        return a.reshape(1, -1) if a.ndim == 1 else a.reshape(-1, a.shape[-1])

    grads, deltas, new_m, new_v = {}, {}, {}, {}

    def adamw_matrix(name, after=()):
        shape = w[name].shape
        if name in _FROM_HALVES:
            grads[name], deltas[name], new_m[name], new_v[name] = _adamw_from_halves(
                "adamw_" + name, w[name], m[name], v[name], [reduced[name, 0], reduced[name, 1]],
                jnp.reshape(pc, (1,)).astype(jnp.int32), name in _TRANSPOSED, after=after)
            return new_v[name]
        layers = [reduced[name, layer] for layer in range(2) if (name, layer) in reduced]
        grads[name] = jnp.concatenate(layers).reshape(shape)
        two_d = lambda a: a.reshape(-1, shape[-1])
        dl, nm, nv = _adamw("adamw_" + name, two_d(w[name]), two_d(grads[name]), two_d(m[name]), two_d(v[name]), after=after)
        deltas[name], new_m[name], new_v[name] = dl.reshape(shape), nm.reshape(shape), nv.reshape(shape)
        return nv

    placed = (rs0b["chip_sum"],)
    for name in ("ffn2_w_down", "attn_w_o", "attn_w_q", "w_kv"):
        placed = (adamw_matrix(name, after=placed),)
    tail = _reduce_small_adamw("small_tail", [view(k, small[k]) for k in _SMALL], loss_tile,
                               *[[view(k, t[k]) for k in _SMALL] for t in (w, m, v)], after=placed)
    n_small = len(_SMALL)
    for i, k in enumerate(_SMALL):
        grads[k], deltas[k] = tail[i].reshape(w[k].shape), tail[n_small + i].reshape(w[k].shape)
        new_m[k], new_v[k] = tail[2 * n_small + i].reshape(w[k].shape), tail[3 * n_small + i].reshape(w[k].shape)
    loss = jnp.sum(tail[-1])
    reduced.update(rs_finish(rs0b, after=(tail[-1],)))
    adamw_matrix("ffn2_w_gate_up", after=(rs0b["total"],))
    for name in ("ffn1_w_gate_up", "ffn1_w_down", "ssm_w_in", "ssm_w_out"):
        adamw_matrix(name)
    return (loss, grad_x, *[grads[k] for k in _ORDER], *[deltas[k] for k in _ORDER], *[new_m[k] for k in _ORDER],
            *[new_v[k] for k in _ORDER])


def kernel(x, meta_tokens, ffn1_norm, ffn1_w_gate_up, ffn1_w_down, mix_norm, ffn2_norm, ffn2_w_gate_up, ffn2_w_down, ssm_w_in, ssm_lambda_re, ssm_lambda_im, ssm_b_re, ssm_b_im, ssm_c_re, ssm_c_im, ssm_log_step, ssm_d, ssm_w_out, kv_norm, w_kv, k_norm, attn_w_q, q_norm, attn_sinks, attn_w_o, loss_target, m_meta_tokens, m_ffn1_norm, m_ffn1_w_gate_up, m_ffn1_w_down, m_mix_norm, m_ffn2_norm, m_ffn2_w_gate_up, m_ffn2_w_down, m_ssm_w_in, m_ssm_lambda_re, m_ssm_lambda_im, m_ssm_b_re, m_ssm_b_im, m_ssm_c_re, m_ssm_c_im, m_ssm_log_step, m_ssm_d, m_ssm_w_out, m_kv_norm, m_w_kv, m_k_norm, m_attn_w_q, m_q_norm, m_attn_sinks, m_attn_w_o, v_meta_tokens, v_ffn1_norm, v_ffn1_w_gate_up, v_ffn1_w_down, v_mix_norm, v_ffn2_norm, v_ffn2_w_gate_up, v_ffn2_w_down, v_ssm_w_in, v_ssm_lambda_re, v_ssm_lambda_im, v_ssm_b_re, v_ssm_b_im, v_ssm_c_re, v_ssm_c_im, v_ssm_log_step, v_ssm_d, v_ssm_w_out, v_kv_norm, v_w_kv, v_k_norm, v_attn_w_q, v_q_norm, v_attn_sinks, v_attn_w_o):
    args = locals()
    w = {k: args[k] for k in _ORDER}
    m = {k: args["m_" + k] for k in _ORDER}
    v = {k: args["v_" + k] for k in _ORDER}
    return _step(x, loss_target, w, m, v)
```

```python
import functools
import math

import jax
import jax.numpy as jnp
from jax import lax
from jax.experimental import pallas as pl
from jax.experimental.pallas import tpu as pltpu
from jax.experimental.pallas import tpu_sc as plsc

F32 = jnp.float32
BF16 = jnp.bfloat16
MESH = pl.DeviceIdType.MESH

EPS = 1e-6
NEG_INF = -1e30
ROPE_THETA = 10000.0
WINDOW = 128
META_BLOCK = 128
ROW_TILE = 256
SUBLANES = 8
V7X_VMEM_LIMIT = 56 * 2**20
N_CHIPS = 4
N_DEV = 8

ADAM_LR = 0.001
ADAM_B1 = 0.9
ADAM_B2 = 0.999
ADAM_EPS = 1e-08
ADAM_WD = 0.01
ADAM_STEP = 10

_HBM = pl.BlockSpec(memory_space=pltpu.HBM)
_VMEM = pl.BlockSpec(memory_space=pltpu.VMEM)


def _call(name, body, grid, in_specs, out_specs, out_shape, scratch=(), after=()):
    after = tuple(after)
    n_in = len(in_specs)

    def wrapped(*refs):
        return body(*refs[:n_in], *refs[n_in + len(after):])

    call = pl.pallas_call(
        wrapped, name=name, grid=grid, in_specs=list(in_specs) + [pl.BlockSpec(memory_space=pl.ANY)] * len(after),
        out_specs=out_specs, out_shape=out_shape, scratch_shapes=list(scratch),
        compiler_params=pltpu.CompilerParams(dimension_semantics=("arbitrary",) * len(grid),
                                             vmem_limit_bytes=V7X_VMEM_LIMIT))
    return lambda *operands: call(*operands, *after)


def _sds(shape, dtype):
    return jax.ShapeDtypeStruct(tuple(shape), dtype)


def _dot(a, b):
    return jnp.dot(a.astype(BF16), b.astype(BF16), preferred_element_type=F32)


def _dot_nt(a, b):
    return lax.dot_general(a.astype(BF16), b.astype(BF16), (((1,), (1,)), ((), ())), preferred_element_type=F32)


def _dot_tn(a, b):
    return lax.dot_general(a.astype(BF16), b.astype(BF16), (((0,), (0,)), ((), ())), preferred_element_type=F32)


def _rms(h, g):
    return h * lax.rsqrt(jnp.mean(h * h, axis=-1, keepdims=True) + EPS) * g


def _rms_bwd(h, g, dn):
    r = lax.rsqrt(jnp.mean(h * h, axis=-1, keepdims=True) + EPS)
    xh = h * r
    dxh = dn * g
    dg = jnp.sum(dn * xh, axis=0, keepdims=True)
    dh = r * (dxh - xh * jnp.mean(dxh * xh, axis=-1, keepdims=True))
    return dh, dg


def _sigmoid(x):
    return 0.5 * jnp.tanh(0.5 * x) + 0.5


def _gelu(y):
    k = math.sqrt(2.0 / math.pi)
    return 0.5 * y * (1.0 + jnp.tanh(k * (y + 0.044715 * y * y * y)))


def _gelu_grad(y):
    k = math.sqrt(2.0 / math.pi)
    t = jnp.tanh(k * (y + 0.044715 * y * y * y))
    return 0.5 * (1.0 + t) + 0.5 * y * (1.0 - t * t) * k * (1.0 + 3.0 * 0.044715 * y * y)


def _partner(x, lane, d):
    width = x.shape[-1]
    return jnp.where((lane & d) == 0, pltpu.roll(x, width - d, 1), pltpu.roll(x, d, 1))


def _split_bf16(x):
    hi = x.astype(BF16)
    return hi, (x - hi.astype(F32)).astype(BF16)


def _head_sums(x, sel):
    hi, lo = _split_bf16(x)
    return jnp.dot(hi, sel, preferred_element_type=F32) + jnp.dot(lo, sel, preferred_element_type=F32)


def _head_expand(v, sel_t):
    hi, lo = _split_bf16(v)
    return jnp.dot(hi, sel_t, preferred_element_type=F32) + jnp.dot(lo, sel_t, preferred_element_type=F32)


def _tile_lanes(t, width):
    return jnp.concatenate([t] * (width // t.shape[-1]), axis=1)


def _head_prep(x, gain_t, cos2, sin2, sel, sel_t, hd):
    width = x.shape[-1]
    lane = lax.broadcasted_iota(jnp.int32, x.shape, 1)
    r = _head_expand(lax.rsqrt(_head_sums(x * x, sel) * (1.0 / hd) + EPS), sel_t)
    y = x * r * gain_t
    return y * _tile_lanes(cos2, width) + _partner(y, lane, hd // 2) * _tile_lanes(sin2, width)


def _head_prep_bwd(x, gain_t, cos2, sin2, sel, sel_t, d_out, hd):
    width = x.shape[-1]
    lane = lax.broadcasted_iota(jnp.int32, x.shape, 1)
    r = _head_expand(lax.rsqrt(_head_sums(x * x, sel) * (1.0 / hd) + EPS), sel_t)
    xhat = x * r
    dy = d_out * _tile_lanes(cos2, width) + _partner(d_out * _tile_lanes(sin2, width), lane, hd // 2)
    dgain = jnp.sum(dy * xhat, axis=0, keepdims=True)
    dxh = dy * gain_t
    mean = _head_expand(_head_sums(dxh * xhat, sel) * (1.0 / hd), sel_t)
    return r * (dxh - xhat * mean), dgain


def _head_selectors(n_heads, hd):
    sel = (jnp.arange(n_heads * hd)[:, None] // hd == jnp.arange(128)[None, :]).astype(BF16)
    return sel, sel.T


def _acc_out(ref, val, first):
    @pl.when(first)
    def _():
        ref[...] = jnp.zeros_like(ref)
    ref[...] += val


def _wide_row_tile(n_rows, cap=512):
    best = 128
    for t in range(128, cap + 1, 128):
        if n_rows % t == 0:
            best = t
    return best


def _ffn_up(name, h, g, w4, n_rows):
    nj, d, fc = w4.shape
    tm = _wide_row_tile(n_rows)

    def body(h_ref, g_ref, w_ref, o_ref, n_ref):
        n = _rms(h_ref[...], g_ref[...]).astype(BF16)
        n_ref[...] = n
        for j in range(nj):
            o_ref[:, j * fc:(j + 1) * fc] = _dot(n, w_ref[j]).astype(BF16)

    return _call(name, body, (n_rows // tm,),
                 [pl.BlockSpec((tm, d), lambda i: (i, 0)), pl.BlockSpec((1, d), lambda i: (0, 0)),
                  pl.BlockSpec((nj, d, fc), lambda i: (0, 0, 0))],
                 [pl.BlockSpec((tm, nj * fc), lambda i: (i, 0)), pl.BlockSpec((tm, d), lambda i: (i, 0))],
                 [_sds((n_rows, nj * fc), BF16), _sds((n_rows, d), BF16)])(h, g, w4)


def _stream_rows(src, buf, sem, tm):
    i, n = pl.program_id(0), pl.num_programs(0)

    def fetch(step):
        slot = lax.rem(step, 3)
        return pltpu.make_async_copy(src.at[pl.ds(pl.multiple_of(step * tm, tm), tm)], buf.at[slot], sem.at[slot])

    @pl.when(i == 0)
    def _():
        fetch(i).start()
        fetch(i + 1).start()

    @pl.when(i + 2 < n)
    def _():
        fetch(i + 2).start()

    fetch(i).wait()
    return buf.at[lax.rem(i, 3)]


def _ffn_down(name, gu, h, wd, n_rows, target=None):
    f, d = wd.shape
    tm = ROW_TILE

    def body(gu_hbm, h_ref, w_ref, *rest):
        *rest, buf, sem = rest
        gu_ref = _stream_rows(gu_hbm, buf, sem, tm)
        half_a = gu_ref[:, :f] * 0.5
        s = (half_a + half_a * jnp.tanh(half_a)) * gu_ref[:, f:]
        y = h_ref[...] + 0.5 * _dot(s, w_ref[...])
        if target is None:
            o_ref, s_ref = rest
            o_ref[...] = y
        else:
            t_ref, dy_ref, l_ref, s_ref = rest
            e = y - t_ref[...]
            dy_ref[...] = e * (1.0 / d)
            e2 = jnp.sum((e * e).reshape(tm // SUBLANES, SUBLANES, d), axis=0)
            part = e2[:, 0:128]
            for k in range(1, d // 128):
                part = part + e2[:, k * 128:(k + 1) * 128]
            _acc_out(l_ref, part * (0.5 / d), pl.program_id(0) == 0)
        s_ref[...] = s

    row = lambda width: pl.BlockSpec((tm, width), lambda i: (i, 0))
    in_specs = [pl.BlockSpec(memory_space=pl.ANY), row(d),
                pl.BlockSpec((f, d), lambda i: (0, 0), pipeline_mode=pl.Buffered(1))]
    scratch = [pltpu.VMEM((3, tm, 2 * f), BF16), pltpu.SemaphoreType.DMA((3,))]
    if target is None:
        return _call(name, body, (n_rows // tm,), in_specs, [row(d), row(f)],
                     [_sds((n_rows, d), F32), _sds((n_rows, f), BF16)], scratch=scratch)(gu, h, wd)
    return _call(name, body, (n_rows // tm,), in_specs + [row(d)],
                 [row(d), pl.BlockSpec((SUBLANES, 128), lambda i: (0, 0)), row(f)],
                 [_sds((n_rows, d), F32), _sds((SUBLANES, 128), F32), _sds((n_rows, f), BF16)],
                 scratch=scratch)(gu, h, wd, target)


def _ffn_dgu(name, dh, gu, wd, n_rows, after=()):
    f, d = wd.shape
    tm = ROW_TILE

    def body(dh_ref, gu_hbm, w_ref, o_ref, buf, sem):
        gu_ref = _stream_rows(gu_hbm, buf, sem, tm)
        ds = _dot_nt(0.5 * dh_ref[...], w_ref[...]).astype(BF16)
        half_a = gu_ref[:, :f] * 0.5
        t = jnp.tanh(half_a)
        o_ref[:, :f] = ds * gu_ref[:, f:] * ((1.0 + t + half_a * (1.0 - t * t)) * 0.5)
        o_ref[:, f:] = ds * (half_a + half_a * t)

    return _call(name, body, (n_rows // tm,),
                 [pl.BlockSpec((tm, d), lambda i: (i, 0)), pl.BlockSpec(memory_space=pl.ANY),
                  pl.BlockSpec((f, d), lambda i: (0, 0), pipeline_mode=pl.Buffered(1))],
                 pl.BlockSpec((tm, 2 * f), lambda i: (i, 0)), _sds((n_rows, 2 * f), BF16),
                 scratch=[pltpu.VMEM((3, tm, 2 * f), BF16), pltpu.SemaphoreType.DMA((3,))], after=after)(dh, gu, wd)


def _ffn_dh(name, dgu, h, g, dh, w4, n_rows, n_main=None, after=()):
    nj, d, fc = w4.shape
    tm = _wide_row_tile(n_rows) if n_main is None else ROW_TILE
    n_first = (n_rows if n_main is None else n_main) // tm

    def body(dgu_ref, h_ref, g_ref, dh_ref, w_ref, o_ref, *rest):
        dg_ref = rest[-1]
        i = pl.program_id(0)
        dn = _dot_nt(dgu_ref[:, 0:fc], w_ref[0])
        for j in range(1, nj):
            dn = dn + _dot_nt(dgu_ref[:, j * fc:(j + 1) * fc], w_ref[j])
        dhn, dg = _rms_bwd(h_ref[...], g_ref[...], dn)
        val = dh_ref[...] + dhn
        if n_main is None:
            o_ref[...] = val
        else:
            @pl.when(i < n_first)
            def _():
                o_ref[...] = val

            @pl.when(i >= n_first)
            def _():
                rest[0][...] = val
        _acc_out(dg_ref, dg, i == 0)

    out_specs = [pl.BlockSpec((tm, d), lambda i: (jnp.minimum(i, n_first - 1), 0))]
    out_shape = [_sds((n_first * tm, d), F32)]
    if n_main is not None:
        out_specs.append(pl.BlockSpec((tm, d), lambda i: (jnp.maximum(i - n_first, 0), 0)))
        out_shape.append(_sds((n_rows - n_main, d), F32))
    return _call(name, body, (n_rows // tm,),
                 [pl.BlockSpec((tm, nj * fc), lambda i: (i, 0)), pl.BlockSpec((tm, d), lambda i: (i, 0)),
                  pl.BlockSpec((1, d), lambda i: (0, 0)), pl.BlockSpec((tm, d), lambda i: (i, 0)),
                  pl.BlockSpec((nj, d, fc), lambda i: (0, 0, 0), pipeline_mode=pl.Buffered(1))],
                 out_specs + [pl.BlockSpec((1, d), lambda i: (0, 0))],
                 out_shape + [_sds((1, d), F32)], after=after)(dgu, h, g, dh, w4)


def _contract_tile(n_rows, cap=2816):
    best = ROW_TILE
    for t in range(ROW_TILE, cap + 1, ROW_TILE):
        if n_rows % t == 0:
            best = t
    return best


def _tn(name, operands, in_specs, prologue, nj, ma, nb, n_rows, tk, out_dtype=F32, after=(), side_by_side=False):
    n_k = n_rows // tk
    out_spec = pl.BlockSpec((1, ma, nb), lambda j, k: (j, 0, 0))
    if out_dtype == F32 and not side_by_side:
        def body(*refs):
            o_ref = refs[-1]
            a, b = prologue(pl.program_id(0), *refs[:-1])
            _acc_out(o_ref, _dot_tn(a, b)[None], pl.program_id(1) == 0)

        return _call(name, body, (nj, n_k), in_specs, out_spec, _sds((nj, ma, nb), F32), after=after)(*operands)

    def body_rounded(*refs):
        o_ref, acc_ref = refs[-2:]
        a, b = prologue(pl.program_id(0), *refs[:-2])
        _acc_out(acc_ref, _dot_tn(a, b), pl.program_id(1) == 0)

        @pl.when(pl.program_id(1) == n_k - 1)
        def _():
            o_ref[...] = acc_ref[...].astype(out_dtype).reshape(o_ref.shape)

    if side_by_side:
        out_spec, out_shape = pl.BlockSpec((ma, nb), lambda j, k: (0, j)), _sds((ma, nj * nb), out_dtype)
    else:
        out_shape = _sds((nj, ma, nb), out_dtype)
    return _call(name, body_rounded, (nj, n_k), in_specs, out_spec, out_shape,
                 scratch=[pltpu.VMEM((ma, nb), F32)], after=after)(*operands)


def _ffn_dwgu(name, n, dgu, nj, n_rows):
    d = n.shape[1]
    fc = dgu.shape[1] // nj
    tk = _contract_tile(n_rows, cap=2816)
    return _tn(name, (dgu, n),
               [pl.BlockSpec((tk, fc), lambda j, k: (k, j)), pl.BlockSpec((tk, d), lambda j, k: (k, 0))],
               lambda j, a_ref, b_ref: (a_ref[...], b_ref[...]), nj, fc, d, n_rows, tk, out_dtype=BF16)


def _ffn_dwd(name, s, dh, n_rows):
    f = s.shape[1]
    d = dh.shape[1]
    tk = _contract_tile(n_rows, cap=2048)
    halves = 2 if tk > 1024 else 1
    return _tn(name, (s, dh),
               [pl.BlockSpec((tk, f), lambda j, k: (k, 0)), pl.BlockSpec((tk, d // halves), lambda j, k: (k, j))],
               lambda j, s_ref, dh_ref: (s_ref[...], 0.5 * dh_ref[...]), halves, f, d // halves, n_rows, tk, out_dtype=BF16,
               side_by_side=True)


def _ffn_fwd(tag, h, g, w4, wd, n_rows, target=None):
    gu, n = _ffn_up(tag + "_up", h, g, w4, n_rows)
    *out, s = _ffn_down(tag + "_down", gu, h, wd, n_rows, target)
    return (out[0] if target is None else tuple(out)), (gu, n, s)


def _ffn_bwd(tag, dh_out, h, g, saved, w4, wd, n_rows, n_main=None, after=()):
    gu, n, s = saved
    nj = w4.shape[0]
    f, d = wd.shape
    dgu = _ffn_dgu(tag + "_dgu", dh_out, gu, wd, n_rows, after=after)
    dwd = _ffn_dwd(tag + "_dwd", s, dh_out, n_rows).reshape(N_CHIPS, f // N_CHIPS, d)
    *dh_parts, dg = _ffn_dh(tag + "_dh", dgu, h, g, dh_out, w4, n_rows, n_main)
    dwgu = _ffn_dwgu(tag + "_dwgu", n, dgu, nj, n_rows)
    dh_in = dh_parts[0] if n_main is None else tuple(dh_parts)
    return dh_in, dg, dwgu, dwd


def _ssm_in(name, h, g, w_in, bb, n_rows):
    d, hw = w_in.shape
    nj, uc, xc = bb.shape
    tm = ROW_TILE

    def body(h_ref, g_ref, w_ref, bb_ref, u_ref, bu_ref):
        u = _dot(_rms(h_ref[...], g_ref[...]), w_ref[...])
        u_ref[...] = u
        for j in range(nj):
            bu_ref[:, j * xc:(j + 1) * xc] = _dot(u[:, j * uc:(j + 1) * uc], bb_ref[j]).astype(BF16)

    return _call(name, body, (n_rows // tm,),
                 [pl.BlockSpec((tm, d), lambda i: (i, 0)), pl.BlockSpec((1, d), lambda i: (0, 0)),
                  pl.BlockSpec((d, hw), lambda i: (0, 0)), pl.BlockSpec((nj, uc, xc), lambda i: (0, 0, 0))],
                 [pl.BlockSpec((tm, hw), lambda i: (i, 0)), pl.BlockSpec((tm, nj * xc), lambda i: (i, 0))],
                 [_sds((n_rows, hw), F32), _sds((n_rows, nj * xc), BF16)])(h, g, w_in, bb)


def _cmul_add(xr, xi, ar, ai, sr, si):
    return xr + ar * sr - ai * si, xi + ar * si + ai * sr


def _scan_row_block(n_main_blocks, seq_blocks):
    return lambda b, i: jnp.where(i == 0, n_main_blocks + b, b * seq_blocks + i - 1)


def _scan_fwd(name, bu, tabs, n_ex, seq):
    n_rows, width = bu.shape
    nj = 4
    cw = width // nj
    half = cw // 2
    tq = META_BLOCK
    seq_blocks = seq // tq
    rb = _scan_row_block(n_ex * seq_blocks, seq_blocks)

    def body(bu_ref, tab_ref, x_ref, carry_ref):
        @pl.when(pl.program_id(1) == 0)
        def _():
            carry_ref[...] = jnp.zeros_like(carry_ref)

        for j in range(nj):
            re, im = slice(j * cw, j * cw + half), slice(j * cw + half, (j + 1) * cw)
            ch = slice(j * half, (j + 1) * half)

            def blk(k, c, re=re, im=im, ch=ch):
                t = [tab_ref[n * SUBLANES:(n + 1) * SUBLANES, ch] for n in range(8)]
                r0 = pl.multiple_of(k * SUBLANES, SUBLANES)
                xr = bu_ref[pl.ds(r0, SUBLANES), re].astype(F32)
                xi = bu_ref[pl.ds(r0, SUBLANES), im].astype(F32)
                for s, d in enumerate((1, 2, 4)):
                    xr, xi = _cmul_add(xr, xi, t[2 * s], t[2 * s + 1], pltpu.roll(xr, d, 0), pltpu.roll(xi, d, 0))
                xr, xi = _cmul_add(xr, xi, t[6], t[7], c[0], c[1])
                x_ref[pl.ds(r0, SUBLANES), re] = xr.astype(BF16)
                x_ref[pl.ds(r0, SUBLANES), im] = xi.astype(BF16)
                last = SUBLANES - 1
                return (jnp.broadcast_to(xr[last:last + 1, :], xr.shape), jnp.broadcast_to(xi[last:last + 1, :], xi.shape))

            c = lax.fori_loop(0, tq // SUBLANES, blk, (carry_ref[0, :, ch], carry_ref[1, :, ch]), unroll=2)
            carry_ref[0, :, ch] = c[0]
            carry_ref[1, :, ch] = c[1]

    return _call(name, body, (n_ex, seq_blocks + 1),
                 [pl.BlockSpec((tq, width), lambda b, i: (rb(b, i), 0)), pl.BlockSpec((8 * SUBLANES, nj * half), lambda b, i: (0, 0))],
                 pl.BlockSpec((tq, width), lambda b, i: (rb(b, i), 0)), _sds((n_rows, width), BF16),
                 scratch=[pltpu.VMEM((2, SUBLANES, nj * half), F32)])(bu, tabs)


def _scan_bwd(name, gx, x, tabs, n_ex, seq, after=()):
    n_rows, width = gx.shape
    nj = 4
    cw = width // nj
    half = cw // 2
    tq = META_BLOCK
    seq_blocks = seq // tq
    n_steps = seq_blocks + 1
    rb = _scan_row_block(n_ex * seq_blocks, seq_blocks)
    rbr = lambda b, i: rb(b, n_steps - 1 - i)

    def body(gx_ref, x_ref, tab_ref, g_ref, da_ref, carry_ref):
        @pl.when(pl.program_id(1) == 0)
        def _():
            carry_ref[...] = jnp.zeros_like(carry_ref)
            da_ref[...] = jnp.zeros_like(da_ref)
        row = lax.broadcasted_iota(jnp.int32, (SUBLANES, half), 0)
        n_blk = tq // SUBLANES

        for j in range(nj):
            re, im = slice(j * cw, j * cw + half), slice(j * cw + half, (j + 1) * cw)
            ch = slice(j * half, (j + 1) * half)

            def blk(kk, st, re=re, im=im, ch=ch):
                t = [tab_ref[n * SUBLANES:(n + 1) * SUBLANES, ch] for n in range(8)]
                cr, ci, dar, dai = st
                r0 = pl.multiple_of((n_blk - 1 - kk) * SUBLANES, SUBLANES)
                gr = gx_ref[pl.ds(r0, SUBLANES), re].astype(F32)
                gi = gx_ref[pl.ds(r0, SUBLANES), im].astype(F32)
                for s, d in enumerate((1, 2, 4)):
                    gr, gi = _cmul_add(gr, gi, t[2 * s], t[2 * s + 1],
                                       pltpu.roll(gr, SUBLANES - d, 0), pltpu.roll(gi, SUBLANES - d, 0))
                gr, gi = _cmul_add(gr, gi, t[6], t[7], cr, ci)
                g_ref[pl.ds(r0, SUBLANES), re] = gr.astype(BF16)
                g_ref[pl.ds(r0, SUBLANES), im] = gi.astype(BF16)
                hr = jnp.where(row == SUBLANES - 1, cr, pltpu.roll(gr, SUBLANES - 1, 0))
                hi = jnp.where(row == SUBLANES - 1, ci, pltpu.roll(gi, SUBLANES - 1, 0))
                xr = x_ref[pl.ds(r0, SUBLANES), re].astype(F32)
                xi = x_ref[pl.ds(r0, SUBLANES), im].astype(F32)
                dar = dar + xr * hr + xi * hi
                dai = dai + xr * hi - xi * hr
                return (jnp.broadcast_to(gr[0:1, :], gr.shape), jnp.broadcast_to(gi[0:1, :], gi.shape), dar, dai)

            st = lax.fori_loop(0, n_blk, blk, (carry_ref[0, :, ch], carry_ref[1, :, ch], da_ref[0, :, re], da_ref[0, :, im]),
                               unroll=2)
            carry_ref[0, :, ch] = st[0]
            carry_ref[1, :, ch] = st[1]
            da_ref[0, :, re] = st[2]
            da_ref[0, :, im] = st[3]

    return _call(name, body, (n_ex, n_steps),
                 [pl.BlockSpec((tq, width), lambda b, i: (rbr(b, i), 0)), pl.BlockSpec((tq, width), lambda b, i: (rbr(b, i), 0)),
                  pl.BlockSpec((8 * SUBLANES, nj * half), lambda b, i: (0, 0))],
                 [pl.BlockSpec((tq, width), lambda b, i: (rbr(b, i), 0)), pl.BlockSpec((1, SUBLANES, width), lambda b, i: (b, 0, 0))],
                 [_sds((n_rows, width), BF16), _sds((n_ex, SUBLANES, width), F32)],
                 scratch=[pltpu.VMEM((2, SUBLANES, nj * half), F32)], after=after)(gx, x, tabs)


def _ssm_z(gy, wout_ref, nj):
    return jnp.concatenate([_dot(gy, wout_ref[j]) for j in range(nj)], axis=1)


def _ssm_out(name, x, u, dskip, cb, wout4, h, n_rows):
    nj, xc, uc = cb.shape
    no, hw, oc = wout4.shape
    d = h.shape[1]
    tm = ROW_TILE

    def body(x_ref, u_ref, ds_ref, cb_ref, w_ref, h_ref, o_ref, y_ref):
        y = jnp.concatenate([_dot(x_ref[:, j * xc:(j + 1) * xc], cb_ref[j]) for j in range(nj)], axis=1)
        y = y + ds_ref[...] * u_ref[...]
        y_ref[...] = y
        z = _ssm_z(_gelu(y), w_ref, no)
        o_ref[...] = h_ref[...] + z[:, :d] * _sigmoid(z[:, d:])

    return _call(name, body, (n_rows // tm,),
                 [pl.BlockSpec((tm, nj * xc), lambda i: (i, 0)), pl.BlockSpec((tm, hw), lambda i: (i, 0)),
                  pl.BlockSpec((1, hw), lambda i: (0, 0)), pl.BlockSpec((nj, xc, uc), lambda i: (0, 0, 0)),
                  pl.BlockSpec((no, hw, oc), lambda i: (0, 0, 0)), pl.BlockSpec((tm, d), lambda i: (i, 0))],
                 [pl.BlockSpec((tm, d), lambda i: (i, 0)), pl.BlockSpec((tm, hw), lambda i: (i, 0))],
                 [_sds((n_rows, d), F32), _sds((n_rows, hw), F32)])(x, u, dskip, cb, wout4, h)


def _ssm_out_bwd(name, dh, y, u, cb, wout4, n_rows, after=()):
    nj, xc, uc = cb.shape
    no, hw, oc = wout4.shape
    d = dh.shape[1]
    tm = ROW_TILE

    def body(dh_ref, y_ref, u_ref, cb_ref, w_ref, dy_ref, dz_ref, gx_ref, dd_ref):
        y = y_ref[...]
        z = _ssm_z(_gelu(y), w_ref, no)
        za = z[:, :d]
        sg = _sigmoid(z[:, d:])
        dmix = dh_ref[...]
        dz = jnp.concatenate([dmix * sg, dmix * za * sg * (1.0 - sg)], axis=1).astype(BF16)
        dz_ref[...] = dz
        dgy = _dot_nt(dz[:, 0:oc], w_ref[0])
        for j in range(1, no):
            dgy = dgy + _dot_nt(dz[:, j * oc:(j + 1) * oc], w_ref[j])
        dy = dgy * _gelu_grad(y)
        dy_ref[...] = dy
        _acc_out(dd_ref, jnp.sum(dy * u_ref[...], axis=0, keepdims=True), pl.program_id(0) == 0)
        for j in range(nj):
            gx_ref[:, j * xc:(j + 1) * xc] = _dot_nt(dy[:, j * uc:(j + 1) * uc], cb_ref[j]).astype(BF16)

    return _call(name, body, (n_rows // tm,),
                 [pl.BlockSpec((tm, d), lambda i: (i, 0)), pl.BlockSpec((tm, hw), lambda i: (i, 0)),
                  pl.BlockSpec((tm, hw), lambda i: (i, 0)), pl.BlockSpec((nj, xc, uc), lambda i: (0, 0, 0)),
                  pl.BlockSpec((no, hw, oc), lambda i: (0, 0, 0))],
                 [pl.BlockSpec((tm, hw), lambda i: (i, 0)), pl.BlockSpec((tm, no * oc), lambda i: (i, 0)),
                  pl.BlockSpec((tm, nj * xc), lambda i: (i, 0)), pl.BlockSpec((1, hw), lambda i: (0, 0))],
                 [_sds((n_rows, hw), F32), _sds((n_rows, no * oc), BF16), _sds((n_rows, nj * xc), BF16),
                  _sds((1, hw), F32)], after=after)(dh, y, u, cb, wout4)


def _ssm_in_bwd(name, gbu, dy, dskip, bb, w_in, h, g, dh, n_rows):
    nj, uc, xc = bb.shape
    d, hw = w_in.shape
    tm = ROW_TILE

    def body(gb_ref, dy_ref, ds_ref, bb_ref, w_ref, h_ref, g_ref, dh_ref, du_ref, o_ref, dg_ref):
        du = jnp.concatenate([_dot_nt(gb_ref[:, j * xc:(j + 1) * xc], bb_ref[j]) for j in range(nj)], axis=1)
        du = du + dy_ref[...] * ds_ref[...]
        du_ref[...] = du.astype(BF16)
        dhn, dg = _rms_bwd(h_ref[...], g_ref[...], _dot_nt(du, w_ref[...]))
        o_ref[...] = dh_ref[...] + dhn
        _acc_out(dg_ref, dg, pl.program_id(0) == 0)

    return _call(name, body, (n_rows // tm,),
                 [pl.BlockSpec((tm, nj * xc), lambda i: (i, 0)), pl.BlockSpec((tm, hw), lambda i: (i, 0)),
                  pl.BlockSpec((1, hw), lambda i: (0, 0)), pl.BlockSpec((nj, uc, xc), lambda i: (0, 0, 0)),
                  pl.BlockSpec((d, hw), lambda i: (0, 0)), pl.BlockSpec((tm, d), lambda i: (i, 0)),
                  pl.BlockSpec((1, d), lambda i: (0, 0)), pl.BlockSpec((tm, d), lambda i: (i, 0))],
                 [pl.BlockSpec((tm, hw), lambda i: (i, 0)), pl.BlockSpec((tm, d), lambda i: (i, 0)),
                  pl.BlockSpec((1, d), lambda i: (0, 0))],
                 [_sds((n_rows, hw), BF16), _sds((n_rows, d), F32), _sds((1, d), F32)])(gbu, dy, dskip, bb, w_in, h, g, dh)


def _discretize(lam_re, lam_im, log_step, b_re, b_im):
    step = jnp.exp(log_step)[:, None]
    mag = jnp.exp(lam_re * step)
    ar = mag * jnp.cos(lam_im * step)
    ai = mag * jnp.sin(lam_im * step)
    den = lam_re * lam_re + lam_im * lam_im
    nr, ni = ar - 1.0, ai
    cr = (nr * lam_re + ni * lam_im) / den
    ci = (ni * lam_re - nr * lam_im) / den
    bbar_r = cr[..., None] * b_re - ci[..., None] * b_im
    bbar_i = cr[..., None] * b_im + ci[..., None] * b_re
    return ar, ai, bbar_r, bbar_i


def _ssm_mats(lam_re, lam_im, log_step, b_re, b_im, c_re, c_im):
    n_g, n_p, n_c = b_re.shape
    gpc = n_g // 4
    ar, ai, bbar_r, bbar_i = _discretize(lam_re, lam_im, log_step, b_re, b_im)
    eye = jnp.eye(gpc, dtype=F32)

    def in_map(bbar):
        return jnp.einsum('jgpc,gh->jgchp', bbar.reshape(4, gpc, n_p, n_c), eye).reshape(4, gpc * n_c, gpc * n_p)

    def out_map(c):
        return jnp.einsum('jgcp,gh->jgphc', c.reshape(4, gpc, n_c, n_p), eye).reshape(4, gpc * n_p, gpc * n_c)

    bb = jnp.concatenate([in_map(bbar_r), in_map(bbar_i)], axis=2)
    cb = jnp.concatenate([out_map(c_re), -out_map(c_im)], axis=1)
    return bb, cb, ar.reshape(-1), ai.reshape(-1)


def _chunked(v, half):
    return v.reshape(v.shape[:-1] + (4, half))


def _scan_tables(ar, ai, reverse):
    if reverse:
        ai = -ai
    pr, pi = [ar], [ai]
    for _ in range(SUBLANES - 1):
        pr, pi = pr + [pr[-1] * ar - pi[-1] * ai], pi + [pr[-1] * ai + pi[-1] * ar]
    row = jnp.arange(SUBLANES)[:, None]
    tabs = []
    for d in (1, 2, 4):
        keep = (row <= SUBLANES - 1 - d) if reverse else (row >= d)
        tabs += [jnp.where(keep, pr[d - 1][None, :], 0.0), jnp.where(keep, pi[d - 1][None, :], 0.0)]
    order = list(range(SUBLANES))[::-1] if reverse else list(range(SUBLANES))
    tabs += [jnp.stack([pr[k] for k in order]), jnp.stack([pi[k] for k in order])]
    return jnp.concatenate(tabs, axis=0)


def _kv_proj(name, h, g, w_kv, k_gain_t, cos2, sin2, n_rows, n_kv, hd):
    d, kvw = w_kv.shape
    kw = n_kv * hd
    tm = ROW_TILE

    sel, sel_t = _head_selectors(n_kv, hd)

    def body(h_ref, g_ref, w_ref, kg_ref, c_ref, s_ref, e_ref, et_ref, raw_ref, k_ref, v_ref):
        raw = _dot(_rms(h_ref[...], g_ref[...]), w_ref[...])
        raw_ref[...] = raw
        k_ref[...] = _head_prep(raw[:, :kw], kg_ref[...], c_ref[...], s_ref[...], e_ref[...], et_ref[...], hd).astype(BF16)
        v_ref[...] = raw[:, kw:].astype(BF16)

    return _call(name, body, (n_rows // tm,),
                 [pl.BlockSpec((tm, d), lambda i: (i, 0)), pl.BlockSpec((1, d), lambda i: (0, 0)),
                  pl.BlockSpec((d, kvw), lambda i: (0, 0)), pl.BlockSpec((1, kw), lambda i: (0, 0)),
                  pl.BlockSpec((tm, 2 * hd), lambda i: (i, 0)), pl.BlockSpec((tm, 2 * hd), lambda i: (i, 0)),
                  pl.BlockSpec(sel.shape, lambda i: (0, 0)), pl.BlockSpec(sel_t.shape, lambda i: (0, 0))],
                 [pl.BlockSpec((tm, kvw), lambda i: (i, 0)), pl.BlockSpec((tm, kw), lambda i: (i, 0)),
                  pl.BlockSpec((tm, kw), lambda i: (i, 0))],
                 [_sds((n_rows, kvw), F32), _sds((n_rows, kw), BF16), _sds((n_rows, kw), BF16)])(
                     h, g, w_kv, k_gain_t, cos2, sin2, sel, sel_t)


def _q_proj(name, h, g, w_q, q_gain_t, cos2, sin2, n_rows, n_q, hd):
    d, qw = w_q.shape
    tm = ROW_TILE

    sel, sel_t = _head_selectors(n_q, hd)

    def body(h_ref, g_ref, w_ref, qg_ref, c_ref, s_ref, e_ref, et_ref, raw_ref, q_ref):
        raw = _dot(_rms(h_ref[...], g_ref[...]), w_ref[...])
        raw_ref[...] = raw
        q_ref[...] = _head_prep(raw, qg_ref[...], c_ref[...], s_ref[...], e_ref[...], et_ref[...], hd).astype(BF16)

    return _call(name, body, (n_rows // tm,),
                 [pl.BlockSpec((tm, d), lambda i: (i, 0)), pl.BlockSpec((1, d), lambda i: (0, 0)),
                  pl.BlockSpec((d, qw), lambda i: (0, 0)), pl.BlockSpec((1, qw), lambda i: (0, 0)),
                  pl.BlockSpec((tm, 2 * hd), lambda i: (i, 0)), pl.BlockSpec((tm, 2 * hd), lambda i: (i, 0)),
                  pl.BlockSpec(sel.shape, lambda i: (0, 0)), pl.BlockSpec(sel_t.shape, lambda i: (0, 0))],
                 [pl.BlockSpec((tm, qw), lambda i: (i, 0)), pl.BlockSpec((tm, qw), lambda i: (i, 0))],
                 [_sds((n_rows, qw), F32), _sds((n_rows, qw), BF16)])(h, g, w_q, q_gain_t, cos2, sin2, sel, sel_t)


def _attn_specs(seq, n_ex, n_meta, kw):
    nb = seq // WINDOW
    meta_blk = lambda b: (n_ex * seq + META_BLOCK * b + META_BLOCK - n_meta) // n_meta
    return [pl.BlockSpec((WINDOW, kw), lambda b, n: (b * nb + jnp.maximum(n - 1, 0), 0)),
            pl.BlockSpec((WINDOW, kw), lambda b, n: (b * nb + n, 0)),
            pl.BlockSpec((n_meta, kw), lambda b, n: (meta_blk(b), 0))]


def _attn_bias(qpk, n_keys):
    rows = qpk * WINDOW
    qi = jnp.arange(rows)[:, None] & (WINDOW - 1)
    kj = jnp.arange(n_keys)[None, :]
    rel = qi + WINDOW - kj
    band = (rel >= 0) & (rel < WINDOW)
    meta = kj >= 2 * WINDOW
    first = (band & (kj >= WINDOW)) | meta
    return jnp.where(jnp.stack([first, band | meta]), 0.0, NEG_INF).astype(F32)


def _stack_heads(ref, h, qpk, hd, dtype=None):
    parts = [ref[:, (h * qpk + gq) * hd:(h * qpk + gq + 1) * hd] for gq in range(qpk)]
    out = jnp.concatenate(parts, axis=0)
    return out if dtype is None else out.astype(dtype)


def _col(tile, c):
    lane = lax.broadcasted_iota(jnp.int32, tile.shape, 1)
    return jnp.sum(jnp.where(lane == c, tile, 0.0), axis=-1, keepdims=True)


def _put_col(col, c, n):
    lane = lax.broadcasted_iota(jnp.int32, (col.shape[0], n), 1)
    return jnp.where(lane == c, col, 0.0)


def _stack_cols(tile, h, qpk):
    return jnp.concatenate([_col(tile, h * qpk + gq) for gq in range(qpk)], axis=0)


def _sink_col(sinks, h, qpk):
    return jnp.concatenate([jnp.broadcast_to(_col(sinks, h * qpk + gq), (WINDOW, 1)) for gq in range(qpk)], axis=0)


def _attn_fwd(name, q, k, v, sinks, n_ex, seq, n_meta, n_kv, qpk, hd):
    nb = seq // WINDOW
    n_q = n_kv * qpk
    kw = n_kv * hd
    qw = n_q * hd
    n_keys = 2 * WINDOW + n_meta
    bias = _attn_bias(qpk, n_keys)

    def body(q_ref, kp_ref, kc_ref, km_ref, vp_ref, vc_ref, vm_ref, sk_ref, bias_ref, o_ref, lse_ref):
        sinks_v = sk_ref[...]
        o_parts = []
        lse_all = jnp.zeros((WINDOW, n_q), F32)
        for h in range(n_kv):
            hs = slice(h * hd, (h + 1) * hd)
            kb = jnp.concatenate([kp_ref[:, hs], kc_ref[:, hs], km_ref[:, hs]], axis=0)
            vb = jnp.concatenate([vp_ref[:, hs], vc_ref[:, hs], vm_ref[:, hs]], axis=0)
            for gq in range(qpk):
                c = h * qpk + gq
                s = _dot_nt(q_ref[:, c * hd:(c + 1) * hd], kb) + bias_ref[0, 0:WINDOW, :]
                skc = _col(sinks_v, c)
                m = jnp.maximum(jnp.max(s, axis=-1, keepdims=True), skc)
                p = jnp.exp(s - m)
                den = jnp.sum(p, axis=-1, keepdims=True) + jnp.exp(skc - m)
                o_parts.append(_dot(p, vb) / den)
                lse_all = lse_all + _put_col(m + jnp.log(den), c, n_q)
        o_ref[...] = jnp.concatenate(o_parts, axis=1).astype(BF16)
        lse_ref[...] = lse_all

    qspec = pl.BlockSpec((WINDOW, qw), lambda b, n: (b * nb + n, 0))
    return _call(name, body, (n_ex, nb),
                 [qspec] + _attn_specs(seq, n_ex, n_meta, kw) + _attn_specs(seq, n_ex, n_meta, kw)
                 + [pl.BlockSpec((1, n_q), lambda b, n: (0, 0)),
                    pl.BlockSpec((1,) + bias.shape[1:], lambda b, n: (jnp.minimum(n, 1), 0, 0))],
                 [qspec, pl.BlockSpec((WINDOW, n_q), lambda b, n: (b * nb + n, 0))],
                 [_sds((n_ex * seq, qw), BF16), _sds((n_ex * seq, n_q), F32)])(q, k, k, k, v, v, v, sinks, bias)


def _attn_bwd(name, q, k, v, sinks, o, lse, do, n_ex, seq, n_meta, n_kv, qpk, hd):
    nb = seq // WINDOW
    n_q = n_kv * qpk
    kw = n_kv * hd
    qw = n_q * hd
    n_keys = 2 * WINDOW + n_meta
    bias = _attn_bias(qpk, n_keys)

    def body(q_ref, kp_ref, kc_ref, km_ref, vp_ref, vc_ref, vm_ref, sk_ref, o_ref, lse_ref, do_ref, bias_ref,
             dq_ref, dk_ref, dv_ref, dkm_ref, dvm_ref, dsk_ref):
        n = pl.program_id(1)

        @pl.when(n == 0)
        def _():
            dk_ref[...] = jnp.zeros_like(dk_ref)
            dv_ref[...] = jnp.zeros_like(dv_ref)
            dkm_ref[...] = jnp.zeros_like(dkm_ref)
            dvm_ref[...] = jnp.zeros_like(dvm_ref)

        @pl.when((n == 0) & (pl.program_id(0) == 0))
        def _():
            dsk_ref[...] = jnp.zeros_like(dsk_ref)

        sinks_v = sk_ref[...]
        lse_v = lse_ref[...]
        grp = 2 if qpk % 2 == 0 else 1
        dq_parts, dk_parts, dv_parts = [], [], []
        dsk = jnp.zeros((1, n_q), F32)
        for h in range(n_kv):
            hs = slice(h * hd, (h + 1) * hd)
            kb = jnp.concatenate([kp_ref[:, hs], kc_ref[:, hs], km_ref[:, hs]], axis=0)
            vb = jnp.concatenate([vp_ref[:, hs], vc_ref[:, hs], vm_ref[:, hs]], axis=0)
            dk_h = dv_h = None
            for c0 in range(h * qpk, (h + 1) * qpk, grp):
                heads = range(c0, c0 + grp)
                stack = lambda ref: jnp.concatenate([ref[:, c * hd:(c + 1) * hd] for c in heads], axis=0)
                qs, dos = stack(q_ref), stack(do_ref)
                delta = jnp.sum(dos.astype(F32) * stack(o_ref).astype(F32), axis=-1, keepdims=True)
                lse_c = jnp.concatenate([_col(lse_v, c) for c in heads], axis=0)
                skc = jnp.concatenate([jnp.broadcast_to(_col(sinks_v, c), (WINDOW, 1)) for c in heads], axis=0)
                p = jnp.exp(_dot_nt(qs, kb) + bias_ref[0, 0:grp * WINDOW, :] - lse_c)
                ds = p * (_dot_nt(dos, vb) - delta)
                dqs = _dot(ds, kb)
                dk_g, dv_g = _dot_tn(ds, qs), _dot_tn(p, dos)
                dk_h = dk_g if dk_h is None else dk_h + dk_g
                dv_h = dv_g if dv_h is None else dv_h + dv_g
                dsink = -jnp.exp(skc - lse_c) * delta
                for i, c in enumerate(heads):
                    dq_parts.append(dqs[i * WINDOW:(i + 1) * WINDOW])
                    dsk = dsk + _put_col(jnp.sum(dsink[i * WINDOW:(i + 1) * WINDOW], axis=0, keepdims=True), c, n_q)
            dk_parts.append(dk_h)
            dv_parts.append(dv_h)
        dq_ref[...] = jnp.concatenate(dq_parts, axis=1).astype(BF16)
        dsk_ref[...] += dsk
        dkb = jnp.concatenate(dk_parts, axis=1)
        dvb = jnp.concatenate(dv_parts, axis=1)
        prev = pl.ds(pl.multiple_of(jnp.maximum(n - 1, 0) * WINDOW, WINDOW), WINDOW)
        cur = pl.ds(pl.multiple_of(n * WINDOW, WINDOW), WINDOW)
        dk_ref[prev, :] += dkb[0:WINDOW]
        dv_ref[prev, :] += dvb[0:WINDOW]
        dk_ref[cur, :] += dkb[WINDOW:2 * WINDOW]
        dv_ref[cur, :] += dvb[WINDOW:2 * WINDOW]
        dkm_ref[...] += dkb[2 * WINDOW:]
        dvm_ref[...] += dvb[2 * WINDOW:]

    qspec = pl.BlockSpec((WINDOW, qw), lambda b, n: (b * nb + n, 0))
    exspec = pl.BlockSpec((seq, kw), lambda b, n: (b, 0))
    mspec = pl.BlockSpec((n_meta, kw), lambda b, n: (b, 0))
    return _call(name, body, (n_ex, nb),
                 [qspec] + _attn_specs(seq, n_ex, n_meta, kw) + _attn_specs(seq, n_ex, n_meta, kw)
                 + [pl.BlockSpec((1, n_q), lambda b, n: (0, 0)), qspec,
                    pl.BlockSpec((WINDOW, n_q), lambda b, n: (b * nb + n, 0)), qspec,
                    pl.BlockSpec((1,) + bias.shape[1:], lambda b, n: (jnp.minimum(n, 1), 0, 0))],
                 [qspec, exspec, exspec, mspec, mspec, pl.BlockSpec((1, n_q), lambda b, n: (0, 0))],
                 [_sds((n_ex * seq, qw), BF16), _sds((n_ex * seq, kw), F32), _sds((n_ex * seq, kw), F32),
                  _sds((n_ex * n_meta, kw), F32), _sds((n_ex * n_meta, kw), F32), _sds((1, n_q), F32)])(
                      q, k, k, k, v, v, v, sinks, o, lse, do, bias)


def _attn_out(name, o, h, w_o, n_rows):
    qw, d = w_o.shape
    tm = ROW_TILE

    def body(o_ref, h_ref, w_ref, out_ref):
        out_ref[...] = h_ref[...] + _dot(o_ref[...], w_ref[...])

    return _call(name, body, (n_rows // tm,),
                 [pl.BlockSpec((tm, qw), lambda i: (i, 0)), pl.BlockSpec((tm, d), lambda i: (i, 0)),
                  pl.BlockSpec((qw, d), lambda i: (0, 0))],
                 pl.BlockSpec((tm, d), lambda i: (i, 0)), _sds((n_rows, d), F32))(o, h, w_o)


def _attn_out_bwd(name, dh, w_o, n_rows):
    qw, d = w_o.shape
    tm = ROW_TILE

    def body(dh_ref, w_ref, do_ref):
        do_ref[...] = _dot_nt(dh_ref[...], w_ref[...]).astype(BF16)

    return _call(name, body, (n_rows // tm,),
                 [pl.BlockSpec((tm, d), lambda i: (i, 0)), pl.BlockSpec((qw, d), lambda i: (0, 0))],
                 pl.BlockSpec((tm, qw), lambda i: (i, 0)), _sds((n_rows, qw), BF16))(dh, w_o)


def _q_bwd(name, dq, qraw, q_gain_t, cos2, sin2, w_q, h, g, dh, n_rows, n_q, hd):
    d, qw = w_q.shape
    tm = ROW_TILE

    sel, sel_t = _head_selectors(n_q, hd)

    def body(dq_ref, raw_ref, qg_ref, c_ref, s_ref, e_ref, et_ref, w_ref, h_ref, g_ref, dh_ref, draw_ref, o_ref, dqg_ref, dg_ref):
        dx, dgain = _head_prep_bwd(raw_ref[...], qg_ref[...], c_ref[...], s_ref[...], e_ref[...], et_ref[...],
                                   dq_ref[...].astype(F32), hd)
        draw = dx.astype(BF16)
        draw_ref[...] = draw
        dhn, dg = _rms_bwd(h_ref[...], g_ref[...], _dot_nt(draw, w_ref[...]))
        o_ref[...] = dh_ref[...] + dhn
        first = pl.program_id(0) == 0
        _acc_out(dqg_ref, dgain, first)
        _acc_out(dg_ref, dg, first)

    row = lambda w: pl.BlockSpec((tm, w), lambda i: (i, 0))
    one = lambda w: pl.BlockSpec((1, w), lambda i: (0, 0))
    return _call(name, body, (n_rows // tm,),
                 [row(qw), row(qw), one(qw), row(2 * hd), row(2 * hd), pl.BlockSpec(sel.shape, lambda i: (0, 0)),
                  pl.BlockSpec(sel_t.shape, lambda i: (0, 0)), pl.BlockSpec((d, qw), lambda i: (0, 0)), row(d), one(d), row(d)],
                 [row(qw), row(d), one(qw), one(d)],
                 [_sds((n_rows, qw), BF16), _sds((n_rows, d), F32), _sds((1, qw), F32), _sds((1, d), F32)])(
                     dq, qraw, q_gain_t, cos2, sin2, sel, sel_t, w_q, h, g, dh)


def _kv_bwd(name, dk, dv, kvraw, k_gain_t, cos2, sin2, w_kv, h, g, dh_main, n_rows, n_main, n_kv, hd, after=()):
    d, kvw = w_kv.shape
    kw = n_kv * hd
    tm = ROW_TILE
    n_main_tiles = n_main // tm

    sel, sel_t = _head_selectors(n_kv, hd)

    def body(dk_ref, dv_ref, raw_ref, kg_ref, c_ref, s_ref, e_ref, et_ref, w_ref, h_ref, g_ref, dh_ref, draw_ref, o_ref, dkg_ref,
             dg_ref):
        i = pl.program_id(0)
        dx, dgain = _head_prep_bwd(raw_ref[:, :kw], kg_ref[...], c_ref[...], s_ref[...], e_ref[...], et_ref[...], dk_ref[...], hd)
        draw = jnp.concatenate([dx, dv_ref[...]], axis=1).astype(BF16)
        draw_ref[...] = draw
        dhn, dg = _rms_bwd(h_ref[...], g_ref[...], _dot_nt(draw, w_ref[...]))
        o_ref[...] = jnp.where(i < n_main_tiles, dh_ref[...], 0.0) + dhn
        _acc_out(dkg_ref, dgain, i == 0)
        _acc_out(dg_ref, dg, i == 0)

    row = lambda w: pl.BlockSpec((tm, w), lambda i: (i, 0))
    one = lambda w: pl.BlockSpec((1, w), lambda i: (0, 0))
    return _call(name, body, (n_rows // tm,),
                 [row(kw), row(kw), row(kvw), one(kw), row(2 * hd), row(2 * hd), pl.BlockSpec(sel.shape, lambda i: (0, 0)),
                  pl.BlockSpec(sel_t.shape, lambda i: (0, 0)), pl.BlockSpec((d, kvw), lambda i: (0, 0)), row(d),
                  one(d), pl.BlockSpec((tm, d), lambda i: (jnp.minimum(i, n_main_tiles - 1), 0))],
                 [row(kvw), row(d), one(kw), one(d)],
                 [_sds((n_rows, kvw), BF16), _sds((n_rows, d), F32), _sds((1, kw), F32), _sds((1, d), F32)], after=after)(
                     dk, dv, kvraw, k_gain_t, cos2, sin2, sel, sel_t, w_kv, h, g, dh_main)


def _tn_rms(name, h, g, b, n_rows, out_dtype=F32):
    d = h.shape[1]
    nb = b.shape[1]
    tk = _contract_tile(n_rows)
    return _tn(name, (h, g, b),
               [pl.BlockSpec((tk, d), lambda j, k: (k, 0)), pl.BlockSpec((1, d), lambda j, k: (0, 0)),
                pl.BlockSpec((tk, nb), lambda j, k: (k, 0))],
               lambda j, h_ref, g_ref, b_ref: (_rms(h_ref[...], g_ref[...]), b_ref[...]), 1, d, nb, n_rows, tk, out_dtype=out_dtype)


def _tn_plain(name, a, b, nj, a_cols, b_cols, n_rows, a_fn=None, out_dtype=F32, after=()):
    tk = _contract_tile(n_rows)
    fa = (lambda v: v) if a_fn is None else a_fn
    a_map = (lambda j, k: (k, j)) if a.shape[1] != a_cols else (lambda j, k: (k, 0))
    b_map = (lambda j, k: (k, j)) if b.shape[1] != b_cols else (lambda j, k: (k, 0))
    return _tn(name, (a, b), [pl.BlockSpec((tk, a_cols), a_map), pl.BlockSpec((tk, b_cols), b_map)],
               lambda j, a_ref, b_ref: (fa(a_ref[...]), b_ref[...]), nj, a_cols, b_cols, n_rows, tk, out_dtype=out_dtype,
               after=after)


def _cast_layer(name, a, layer):
    _, r, c = a.shape
    tr = _row_tile(r, 256)

    def body(a_ref, o_ref):
        o_ref[...] = a_ref[0].astype(BF16)

    return _call(name, body, (r // tr,), [pl.BlockSpec((1, tr, c), lambda i: (layer, i, 0))],
                 pl.BlockSpec((tr, c), lambda i: (i, 0)), _sds((r, c), BF16))(a)


def _adamw_math(w, g, m, v):
    c1 = 1.0 - ADAM_B1 ** ADAM_STEP
    c2 = 1.0 - ADAM_B2 ** ADAM_STEP
    nm = ADAM_B1 * m + (1.0 - ADAM_B1) * g
    nv = ADAM_B2 * v + (1.0 - ADAM_B2) * (g * g)
    return -ADAM_LR * ((nm / c1) / (jnp.sqrt(nv / c2) + ADAM_EPS) + ADAM_WD * w), nm, nv


def _adamw(name, w, g, m, v, after=()):
    rows, cols = w.shape
    tr = 128 if rows % 128 == 0 else rows

    def body(w_ref, g_ref, m_ref, v_ref, d_ref, nm_ref, nv_ref):
        d_ref[...], nm_ref[...], nv_ref[...] = _adamw_math(w_ref[...], g_ref[...], m_ref[...], v_ref[...])

    spec = pl.BlockSpec((tr, cols), lambda i: (i, 0))
    return _call(name, body, (rows // tr,), [spec] * 4, [spec] * 3, [_sds((rows, cols), F32)] * 3, after=after)(w, g, m, v)


def _adamw_from_halves(name, w, m, v, sources, half_index, transposed, after=()):
    n_layers, r, c = w.shape
    lanes = 1024
    after = tuple(after)
    if transposed:
        rows_half, tr = c // 2, 128
        grid = (n_layers, r // tr)
        w_spec = pl.BlockSpec((1, tr, c), lambda l, i, s: (l, i, 0))
        g_spec = lambda off: pl.BlockSpec((rows_half, tr), lambda l, i, s: (off // rows_half, i))
    else:
        rows_half = r // 2
        grid = (n_layers, 2)
        w_spec = pl.BlockSpec((1, rows_half, c), lambda l, k, s: (l, k, 0))
        g_spec = lambda off: pl.BlockSpec((rows_half, lanes), lambda l, k, s: (off // rows_half, 0))

    def body(s_ref, w_ref, m_ref, v_ref, t0_ref, o0_ref, t1_ref, o1_ref, *rest):
        g_ref, d_ref, nm_ref, nv_ref = rest[len(after):]
        layer, k, mine = pl.program_id(0), pl.program_id(1), s_ref[0]
        tot = jnp.where(layer == 0, t0_ref[...], t1_ref[...])
        oth = jnp.where(layer == 0, o0_ref[...], o1_ref[...])
        if transposed:
            g = jnp.concatenate([jnp.where(mine == 0, tot, oth), jnp.where(mine == 0, oth, tot)], axis=0).T
        else:
            g = jnp.where(k == mine, tot, oth)
        g_ref[0] = g
        d_ref[0], nm_ref[0], nv_ref[0] = _adamw_math(w_ref[0], g, m_ref[0], v_ref[0])

    (t0, o0, off0), (t1, o1, off1) = sources
    grid_spec = pltpu.PrefetchScalarGridSpec(
        num_scalar_prefetch=1, grid=grid,
        in_specs=[w_spec] * 3 + [g_spec(off0), g_spec(off0), g_spec(off1), g_spec(off1)] + [_ANY] * len(after),
        out_specs=[w_spec] * 4)
    return pl.pallas_call(
        body, name=name, grid_spec=grid_spec, out_shape=[_sds(w.shape, F32)] * 4,
        compiler_params=pltpu.CompilerParams(dimension_semantics=("arbitrary", "arbitrary"),
                                             vmem_limit_bytes=V7X_VMEM_LIMIT))(half_index, w, m, v, t0, o0, t1, o1, *after)


def _position():
    return lax.axis_index("x"), lax.axis_index("y"), lax.axis_index("c")


def _other_chips(x, y):
    return [(1 - x, y), (x, 1 - y), (1 - x, 1 - y)]


def _peers_chips(x, y, c):
    return [(cx, cy, c) for cx, cy in _other_chips(x, y)]


def _peers_sibling(x, y, c):
    return [(x, y, 1 - c)]


def _peers_chips_and_sibling(x, y, c):
    return _peers_chips(x, y, c) + _peers_sibling(x, y, c)


def _comm_call(name, body, n_in, out_shape, scratch, sequencer=None):
    if sequencer is None:
        return pl.pallas_call(
            body, name=name, in_specs=[_HBM] * n_in, out_specs=[_HBM] * len(out_shape), out_shape=out_shape,
            scratch_shapes=list(scratch),
            compiler_params=pltpu.CompilerParams(has_side_effects=True, vmem_limit_bytes=V7X_VMEM_LIMIT))
    collective_id, peers = sequencer

    def seq_body(*refs):
        barrier = pltpu.get_barrier_semaphore()
        plist = peers(*_position())
        for peer in plist:
            pl.semaphore_signal(barrier, inc=1, device_id=peer, device_id_type=MESH)
        pl.semaphore_wait(barrier, len(plist))
        body(*refs)

    return pl.kernel(seq_body, out_type=out_shape, mesh=plsc.ScalarSubcoreMesh(axis_name="sequencer", num_cores=1), name=name,
                     scratch_types=list(scratch), compiler_params=pltpu.CompilerParams(collective_id=collective_id))


def _n_chunks(rows, want, dtype):
    align = 16 if dtype == BF16 else 8
    n = want
    while n > 1 and (rows % n or (rows // n) % align):
        n -= 1
    return n


def _remote(src, dst, send_sem, recv_sem, device):
    return pltpu.make_async_remote_copy(src_ref=src, dst_ref=dst, send_sem=send_sem, recv_sem=recv_sem,
                                        device_id=device, device_id_type=MESH)


def _start_in_chunks(src, dst, send_sem, recv_sem, device, want=8):
    rows = src.shape[0]
    n = _n_chunks(rows, want, src.dtype)
    for i in range(n):
        part = pl.ds(i * (rows // n), rows // n)
        _remote(src.at[part], dst.at[part], send_sem, recv_sem, device).start()


def _all_gather_chips(name, shards, split, collective_id=None):
    n = len(shards)

    def body(*refs):
        ins, outs = refs[:n], refs[n:2 * n]
        send_sems, recv_sems, local_sems = refs[2 * n:]
        x, y, c = _position()
        me = 2 * x + y
        chips = _other_chips(x, y)
        sibling = (x, y, 1 - c)
        sends, forwards = [], []
        for t in range(n):
            pltpu.make_async_copy(ins[t], outs[t].at[me], local_sems.at[t]).start()
        for t in range(n):
            r = ins[t].shape[0]
            rows = pl.ds(c * (r // 2), r // 2) if split[t] else pl.ds(0, r)
            for k, (cx, cy) in enumerate(chips):
                src, dst = ins[t].at[rows], outs[t].at[me, rows]
                _start_in_chunks(src, dst, send_sems.at[t, k], recv_sems.at[t, k], (cx, cy, c), want=4)
                sends.append(_remote(src, dst, send_sems.at[t, k], recv_sems.at[t, k], (cx, cy, c)))
        for t in range(n):
            r = ins[t].shape[0]
            rows = pl.ds(c * (r // 2), r // 2) if split[t] else pl.ds(0, r)
            for k, (cx, cy) in enumerate(chips):
                landed = outs[t].at[2 * cx + cy, rows]
                _remote(landed, landed, send_sems.at[t, k], recv_sems.at[t, k], (cx, cy, c)).wait_recv()
                if split[t]:
                    _start_in_chunks(landed, landed, send_sems.at[t, 3 + k], recv_sems.at[t, 3 + k], sibling, want=4)
                    forwards.append(_remote(landed, landed, send_sems.at[t, 3 + k], recv_sems.at[t, 3 + k], sibling))
        for t in range(n):
            if split[t]:
                r = ins[t].shape[0]
                other = pl.ds((1 - c) * (r // 2), r // 2)
                for k, (cx, cy) in enumerate(chips):
                    landed = outs[t].at[2 * cx + cy, other]
                    pltpu.make_async_remote_copy(
                        src_ref=landed, dst_ref=landed, send_sem=send_sems.at[t, 3 + k], recv_sem=recv_sems.at[t, 3 + k],
                        device_id=sibling, device_id_type=MESH).wait_recv()
        for cp in sends + forwards:
            cp.wait_send()
        for t in range(n):
            pltpu.make_async_copy(ins[t], outs[t].at[me], local_sems.at[t]).wait()

    out_shape = [_sds((N_CHIPS,) + s.shape, s.dtype) for s in shards]
    sequencer = None if collective_id is None else (collective_id, _peers_chips_and_sibling)
    return _comm_call(name, body, n, out_shape,
                      [pltpu.SemaphoreType.DMA((n, 6)), pltpu.SemaphoreType.DMA((n, 6)), pltpu.SemaphoreType.DMA((n,))],
                      sequencer)(*shards)


def _swap_halves_with_sibling(name, blob, collective_id=None):
    def body(b_ref, theirs_ref, send_sem, recv_sem):
        x, y, c = _position()
        sibling = (x, y, 1 - c)
        for k in range(b_ref.shape[1]):
            _start_in_chunks(b_ref.at[1 - c, k], theirs_ref.at[k], send_sem, recv_sem, sibling)
        _remote(b_ref.at[1 - c], theirs_ref, send_sem, recv_sem, sibling).wait()

    return _comm_call(name, body, 1, [_sds(blob.shape[1:], blob.dtype)],
                      [pltpu.SemaphoreType.DMA(()), pltpu.SemaphoreType.DMA(())],
                      None if collective_id is None else (collective_id, _peers_sibling))(blob)[0]


def _scatter_to_chips(name, parts, collective_id=None):
    def body(p_ref, o_ref, send_sems, recv_sems, local_sems):
        x, y, c = _position()
        me = 2 * x + y
        rows = p_ref.shape[1]
        n_loc = _n_chunks(rows, 16, p_ref.dtype)
        locs = [pltpu.make_async_copy(p_ref.at[me, pl.ds(i * (rows // n_loc), rows // n_loc)],
                                      o_ref.at[me, pl.ds(i * (rows // n_loc), rows // n_loc)], local_sems.at[i])
                for i in range(n_loc)]
        for loc in locs:
            loc.start()
        sends = []
        for k, (cx, cy) in enumerate(_other_chips(x, y)):
            src, dst = p_ref.at[2 * cx + cy], o_ref.at[me]
            _start_in_chunks(src, dst, send_sems.at[k], recv_sems.at[k], (cx, cy, c))
            sends.append(_remote(src, dst, send_sems.at[k], recv_sems.at[k], (cx, cy, c)))
        for k, (cx, cy) in enumerate(_other_chips(x, y)):
            landed = o_ref.at[2 * cx + cy]
            _remote(landed, landed, send_sems.at[k], recv_sems.at[k], (cx, cy, c)).wait_recv()
        for cp in sends:
            cp.wait_send()
        for loc in locs:
            loc.wait()

    def local_sems_shape(rows):
        return pltpu.SemaphoreType.DMA((_n_chunks(rows, 16, parts.dtype),))

    return _comm_call(name, body, 1, [_sds(parts.shape, parts.dtype)],
                      [pltpu.SemaphoreType.DMA((3,)), pltpu.SemaphoreType.DMA((3,)), local_sems_shape(parts.shape[1])],
                      None if collective_id is None else (collective_id, _peers_chips))(parts)[0]


def _share_with_sibling(name, mine, collective_id=None):
    def body(m_ref, o_ref, send_sem, recv_sem):
        x, y, c = _position()
        sibling = (x, y, 1 - c)
        _start_in_chunks(m_ref, o_ref, send_sem, recv_sem, sibling, want=16)
        _remote(m_ref, o_ref, send_sem, recv_sem, sibling).wait()

    return _comm_call(name, body, 1, [_sds(mine.shape, mine.dtype)],
                      [pltpu.SemaphoreType.DMA(()), pltpu.SemaphoreType.DMA(())],
                      None if collective_id is None else (collective_id, _peers_sibling))(mine)[0]


def _row_tile(rows, cap=640):
    best = rows
    for t in range(16, min(rows, cap) + 1, 16):
        if rows % t == 0:
            best = t
    return best


_ANY = pl.BlockSpec(memory_space=pl.ANY)


def _add_my_half(name, blob, theirs, half_index, out_dtype, after):
    n, rows, cols = theirs.shape
    tr = _row_tile(rows)
    after = tuple(after)

    def body(c_ref, a_ref, b_ref, *rest):
        o_ref = rest[-1]
        o_ref[...] = (a_ref[0].astype(F32) + b_ref[...].astype(F32)).astype(out_dtype)

    spec = pl.BlockSpec((1, tr, cols), lambda k, i, c: (k, i, 0))
    grid_spec = pltpu.PrefetchScalarGridSpec(
        num_scalar_prefetch=1, grid=(n, rows // tr),
        in_specs=[pl.BlockSpec((1, 1, tr, cols), lambda k, i, c: (c[0], k, i, 0)), spec] + [_ANY] * len(after), out_specs=spec)
    return pl.pallas_call(
        body, name=name, grid_spec=grid_spec, out_shape=_sds(theirs.shape, out_dtype),
        compiler_params=pltpu.CompilerParams(dimension_semantics=("arbitrary", "arbitrary"),
                                             vmem_limit_bytes=V7X_VMEM_LIMIT))(half_index, blob, theirs, *after)


def _sum_slots(name, parts, after):
    n, rows, cols = parts.shape
    tr = _row_tile(rows)

    def body(p_ref, o_ref):
        acc = p_ref[0].astype(F32)
        for k in range(1, n):
            acc = acc + p_ref[k].astype(F32)
        o_ref[...] = acc

    return _call(name, body, (rows // tr,), [pl.BlockSpec((n, tr, cols), lambda i: (0, i, 0))],
                 pl.BlockSpec((tr, cols), lambda i: (i, 0)), _sds((rows, cols), F32), after=after)(parts)


def _reduce_small_adamw(name, grads, loss_tile, ws, ms, vs, after=()):
    n = len(grads)
    srcs = list(grads) + [loss_tile]
    after = tuple(after)

    def body(*refs):
        refs = refs[:4 * n + 1] + refs[4 * n + 1 + len(after):]
        g_in, w_in, m_in, v_in = refs[:n + 1], refs[n + 1:2 * n + 1], refs[2 * n + 1:3 * n + 1], refs[3 * n + 1:4 * n + 1]
        outs = refs[4 * n + 1:8 * n + 2]
        g_out, d_out, nm_out, nv_out, loss_out = outs[:n], outs[n:2 * n], outs[2 * n:3 * n], outs[3 * n:4 * n], outs[4 * n]
        bufs = refs[8 * n + 2:9 * n + 3]
        send_sems, recv_sems = refs[9 * n + 3:]
        x, y, c = _position()
        me = 4 * x + 2 * y + c
        chip = 2 * x + y
        peers = [(1 - x if dlt & 4 else x, 1 - y if dlt & 2 else y, 1 - c if dlt & 1 else c) for dlt in range(1, N_DEV)]
        sends = []
        for t in range(n + 1):
            bufs[t][me] = g_in[t][...]
            for k, peer in enumerate(peers):
                cp = _remote(g_in[t], bufs[t].at[me], send_sems.at[t, k], recv_sems.at[t, k], peer)
                cp.start()
                sends.append(cp)
        for t in range(n + 1):
            for k, (tx, ty, tc) in enumerate(peers):
                landed = bufs[t].at[4 * tx + 2 * ty + tc]
                _remote(landed, landed, send_sems.at[t, k], recv_sems.at[t, k], (tx, ty, tc)).wait_recv()
        for cp in sends:
            cp.wait_send()
        for t in range(n + 1):
            total = bufs[t][0]
            for k in range(1, N_DEV):
                total = total + bufs[t][k]
            if t == n:
                loss_out[...] = total
                continue
            cols = w_in[t].shape[1]
            if cols == total.shape[1]:
                g_out[t][...] = total
                d_out[t][...], nm_out[t][...], nv_out[t][...] = _adamw_math(w_in[t][...], total, m_in[t][...], v_in[t][...])
            else:
                for j in range(N_CHIPS):
                    @pl.when(chip == j)
                    def _(t=t, j=j, cols=cols, total=total):
                        mine = total[:, j * cols:(j + 1) * cols]
                        g_out[t][...] = mine
                        d_out[t][...], nm_out[t][...], nv_out[t][...] = _adamw_math(w_in[t][...], mine, m_in[t][...], v_in[t][...])

    w_shapes = [_sds(a.shape, F32) for a in ws]
    return pl.pallas_call(
        body, name=name, in_specs=[_VMEM] * (4 * n + 1) + [_ANY] * len(after), out_specs=[_VMEM] * (4 * n + 1),
        out_shape=w_shapes * 4 + [_sds(loss_tile.shape, F32)],
        scratch_shapes=[pltpu.VMEM((N_DEV,) + a.shape, F32) for a in srcs]
        + [pltpu.SemaphoreType.DMA((n + 1, N_DEV - 1)), pltpu.SemaphoreType.DMA((n + 1, N_DEV - 1))],
        compiler_params=pltpu.CompilerParams(has_side_effects=True, vmem_limit_bytes=V7X_VMEM_LIMIT))(
            *srcs, *ws, *ms, *vs, *after)


_BIG = ("ffn1_w_gate_up", "ffn1_w_down", "ffn2_w_gate_up", "ffn2_w_down", "ssm_w_in", "ssm_w_out", "w_kv", "attn_w_q", "attn_w_o")
_TRANSPOSED = ("ffn1_w_gate_up", "ffn2_w_gate_up")
_FROM_HALVES = _TRANSPOSED + ("ffn1_w_down", "ffn2_w_down")
_SMALL = ("meta_tokens", "ffn1_norm", "mix_norm", "ffn2_norm", "ssm_lambda_re", "ssm_lambda_im", "ssm_b_re", "ssm_b_im",
          "ssm_c_re", "ssm_c_im", "ssm_log_step", "ssm_d", "kv_norm", "k_norm", "q_norm", "attn_sinks")
_ORDER = ("meta_tokens", "ffn1_norm", "ffn1_w_gate_up", "ffn1_w_down", "mix_norm", "ffn2_norm", "ffn2_w_gate_up", "ffn2_w_down",
          "ssm_w_in", "ssm_lambda_re", "ssm_lambda_im", "ssm_b_re", "ssm_b_im", "ssm_c_re", "ssm_c_im", "ssm_log_step", "ssm_d",
          "ssm_w_out", "kv_norm", "w_kv", "k_norm", "attn_w_q", "q_norm", "attn_sinks", "attn_w_o")


def _step(x, target, w, m, v):
    n_ex, seq, d = x.shape
    n_meta = w["meta_tokens"].shape[0]
    n_main = n_ex * seq
    n_all = n_main + n_ex * META_BLOCK
    n_g, n_p, n_c = w["ssm_b_re"].shape[1:]
    hd = w["k_norm"].shape[0]
    n_kv = w["w_kv"].shape[1] // (2 * hd)
    n_q = w["attn_w_q"].shape[2] // hd
    qpk = n_q // n_kv
    px, py, pc = _position()
    chip = 2 * px + py

    def cast(name, layer=0):
        a = w[name]
        return _cast_layer(f"cast_{name}_{layer}", a if a.ndim == 3 else a[None], layer)

    g_a = _all_gather_chips("gather_first", [cast("ffn1_w_gate_up"), w["meta_tokens"], w["ssm_d"]], [True, False, False],
                            collective_id=12)
    g_d = _all_gather_chips("gather_next", [cast("ffn1_w_down"), cast("ssm_w_in"), cast("ssm_w_out")], [True] * 3, collective_id=13)
    second = [cast("ffn2_w_gate_up"), cast("ffn2_w_down"), cast("w_kv")]
    g_b = _all_gather_chips("gather_second", second, [True] * 3, collective_id=1)
    third = [cast("ffn1_w_gate_up", 1), cast("ffn1_w_down", 1), cast("attn_w_q"), cast("attn_w_o"),
             cast("ffn2_w_gate_up", 1), cast("ffn2_w_down", 1)]
    g_c = _all_gather_chips("gather_third", third, [True] * 6, collective_id=2)
    wgu = {("ffn1", 0): g_a[0], ("ffn1", 1): g_c[0], ("ffn2", 0): g_b[0], ("ffn2", 1): g_c[4]}
    wd = {("ffn1", 0): g_d[0], ("ffn1", 1): g_c[1], ("ffn2", 0): g_b[1], ("ffn2", 1): g_c[5]}
    wd = {key: a.reshape(-1, d) for key, a in wd.items()}
    w_in = g_d[1].reshape(d, -1)
    wout4 = g_d[2]
    w_q = g_c[2].reshape(d, -1)
    w_o = g_c[3].reshape(-1, d)
    w_kv = g_b[2].reshape(d, -1)
    meta_full = jnp.transpose(g_a[1], (1, 0, 2)).reshape(n_meta, d)
    dskip = g_a[2].reshape(1, -1)

    row1 = lambda a: a.reshape(1, -1)
    ssm_args = tuple(w[k][0] for k in ("ssm_lambda_re", "ssm_lambda_im", "ssm_log_step", "ssm_b_re", "ssm_b_im", "ssm_c_re", "ssm_c_im"))
    (bb, cb, a_re, a_im), ssm_vjp = jax.vjp(_ssm_mats, *ssm_args)
    bb16, cb16 = bb.astype(BF16), cb.astype(BF16)
    a_re_s, a_im_s = lax.stop_gradient(a_re), lax.stop_gradient(a_im)
    half = n_g * n_p // 4
    tabs_f = _scan_tables(a_re_s, a_im_s, False)
    tabs_b = _scan_tables(a_re_s, a_im_s, True)

    freqs = ROPE_THETA ** (-jnp.arange(0, hd // 2, dtype=F32) * 2.0 / hd)
    pos_main = jnp.tile(n_meta + jnp.arange(seq), n_ex)
    pos_meta = jnp.tile(jnp.maximum(jnp.arange(META_BLOCK) - (META_BLOCK - n_meta), 0), n_ex)
    ang = jnp.concatenate([pos_main, pos_meta]).astype(F32)[:, None] * freqs[None, :]
    cos = jnp.concatenate([jnp.cos(ang), jnp.cos(ang)] * 2, axis=1)
    sin_s = jnp.concatenate([-jnp.sin(ang), jnp.sin(ang)] * 2, axis=1)
    k_gain_t = jnp.tile(row1(w["k_norm"]), (1, n_kv))
    score_scale = hd ** -0.5
    q_gain_t = jnp.tile(row1(w["q_norm"][0]), (1, n_q)) * score_scale

    meta_block = jnp.concatenate([jnp.zeros((META_BLOCK - n_meta, d), F32), meta_full], axis=0)
    h0 = jnp.concatenate([x.reshape(n_main, d)] + [meta_block] * n_ex, axis=0)

    g = lambda name, layer: row1(w[name][layer])
    h1, gu1 = _ffn_fwd("l0_ffn1", h0, g("ffn1_norm", 0), wgu["ffn1", 0], wd["ffn1", 0], n_all)
    u, bu = _ssm_in("ssm_in", h1, g("mix_norm", 0), w_in, bb16, n_all)
    xs = _scan_fwd("ssm_scan", bu, tabs_f, n_ex, seq)
    h2, y = _ssm_out("ssm_out", xs, u, dskip, cb16, wout4, h1, n_all)
    h3, gu2 = _ffn_fwd("l0_ffn2", h2, g("ffn2_norm", 0), wgu["ffn2", 0], wd["ffn2", 0], n_all)
    kvraw, k, vv = _kv_proj("kv_proj", h3, row1(w["kv_norm"]), w_kv, k_gain_t, cos, sin_s, n_all, n_kv, hd)
    h4, gu3 = _ffn_fwd("l1_ffn1", h3, g("ffn1_norm", 1), wgu["ffn1", 1], wd["ffn1", 1], n_main)
    qraw, q = _q_proj("q_proj", h4, g("mix_norm", 1), w_q, q_gain_t, cos, sin_s, n_main, n_q, hd)
    sinks = row1(w["attn_sinks"][0])
    o, lse = _attn_fwd("attn_fwd", q, k, vv, sinks, n_ex, seq, n_meta, n_kv, qpk, hd)
    h5 = _attn_out("attn_out", o, h4, w_o, n_main)
    (dh6, loss_tile), gu4 = _ffn_fwd("l1_ffn2", h5, g("ffn2_norm", 1), wgu["ffn2", 1], wd["ffn2", 1], n_main,
                                     target=target.reshape(n_main, d))

    lanes = 1024

    def rs_start(tag, entries, ids):
        pieces = [gr.reshape(N_CHIPS, 2, -1, lanes) for _, _, gr in entries]
        blob = jnp.transpose(jnp.concatenate(pieces, axis=2), (1, 0, 2, 3)).astype(BF16)
        return dict(tag=tag, entries=entries, ids=ids, blob=blob, theirs=_swap_halves_with_sibling(tag + "_swap", blob, ids[0]))

    def rs_scatter(st, after):
        chip_sum = _add_my_half(st["tag"] + "_chip_sum", st["blob"], st["theirs"], jnp.reshape(pc, (1,)).astype(jnp.int32), BF16, after)
        st["chip_sum"] = chip_sum
        st["landed"] = _scatter_to_chips(st["tag"] + "_scatter", chip_sum, st["ids"][1])

    def rs_finish(st, after):
        total = _sum_slots(st["tag"] + "_sum", st["landed"], after)
        st["total"] = total
        other = _share_with_sibling(st["tag"] + "_share", total, st["ids"][2])
        halves = (jnp.where(pc == 0, total, other), jnp.where(pc == 0, other, total))
        out, off = {}, 0
        for name, layer, gr in st["entries"]:
            rows = gr.shape[1] * gr.shape[2] // lanes // 2
            if name in _FROM_HALVES:
                out[name, layer] = (total, other, off)
            else:
                out[name, layer] = jnp.concatenate([hv[off:off + rows].reshape(-1) for hv in halves])
            off += rows
        return out

    small = {}
    dh5, dg_f2l1, dwgu_f2l1, dwd_f2l1 = _ffn_bwd("l1_ffn2", dh6, h5, g("ffn2_norm", 1), gu4, wgu["ffn2", 1], wd["ffn2", 1], n_main)
    do = _attn_out_bwd("attn_out_bwd", dh5, w_o, n_main)
    dw_o = _tn_plain("attn_dwo", o, dh5, 1, o.shape[1], d, n_main, out_dtype=BF16).reshape(N_CHIPS, -1, d)
    dq, dk_main, dv_main, dk_meta, dv_meta, dsinks = _attn_bwd("attn_bwd", q, k, vv, sinks, o, lse, do, n_ex, seq, n_meta, n_kv, qpk, hd)
    dqraw, dh4, dq_gain, dg_mix1 = _q_bwd("q_bwd", dq, qraw, q_gain_t, cos, sin_s, w_q, h4, g("mix_norm", 1), dh5, n_main, n_q, hd)
    dw_q = _tn_rms("attn_dwq", h4, g("mix_norm", 1), dqraw, n_main, out_dtype=BF16).reshape(N_CHIPS, -1, dqraw.shape[1])
    dh3m, dg_f1l1, dwgu_f1l1, dwd_f1l1 = _ffn_bwd("l1_ffn1", dh4, h3, g("ffn1_norm", 1), gu3, wgu["ffn1", 1], wd["ffn1", 1], n_main)
    rs1 = rs_start("rs1", [("ffn2_w_gate_up", 1, dwgu_f2l1), ("ffn1_w_gate_up", 1, dwgu_f1l1), ("ffn2_w_down", 1, dwd_f2l1),
                           ("ffn1_w_down", 1, dwd_f1l1), ("attn_w_o", 0, dw_o), ("attn_w_q", 0, dw_q)], (3, 4, 5))

    def with_meta(main, meta):
        blocks = [jnp.pad(meta[b * n_meta:(b + 1) * n_meta], ((META_BLOCK - n_meta, 0), (0, 0))) for b in range(n_ex)]
        return jnp.concatenate([main] + blocks, axis=0)

    dkvraw, dh3, dk_gain, dg_kv = _kv_bwd("kv_bwd", with_meta(dk_main, dk_meta), with_meta(dv_main, dv_meta), kvraw, k_gain_t,
                                          cos, sin_s, w_kv, h3, row1(w["kv_norm"]), dh3m, n_all, n_main, n_kv, hd,
                                          after=(rs1["blob"],))
    rs_scatter(rs1, after=(dh3,))
    dw_kv = _tn_rms("kv_dw", h3, row1(w["kv_norm"]), dkvraw, n_all, out_dtype=BF16).reshape(N_CHIPS, -1, dkvraw.shape[1])
    dh2, dg_f2l0, dwgu_f2l0, dwd_f2l0 = _ffn_bwd("l0_ffn2", dh3, h2, g("ffn2_norm", 0), gu2, wgu["ffn2", 0], wd["ffn2", 0], n_all,
                                                 after=(rs1["chip_sum"],))
    reduced = rs_finish(rs1, after=(dh2, dwgu_f2l0, dwd_f2l0, dw_kv))
    rs0a = rs_start("rs0a", [("ffn2_w_gate_up", 0, dwgu_f2l0), ("ffn2_w_down", 0, dwd_f2l0), ("w_kv", 0, dw_kv)], (6, 7, 8))

    dy, dz, gx, dd = _ssm_out_bwd("ssm_out_bwd", dh2, y, u, cb16, wout4, n_all, after=(rs1["total"], rs0a["blob"]))
    rs_scatter(rs0a, after=(dy,))
    hw = y.shape[1]
    oc = wout4.shape[2]
    dw_out = _tn_plain("ssm_dwout", y, dz, wout4.shape[0], hw, oc, n_all, a_fn=_gelu, out_dtype=BF16)
    gbu, da = _scan_bwd("ssm_scan_bwd", gx, xs, tabs_b, n_ex, seq, after=(rs0a["chip_sum"],))
    du, dh1, dg_mix0 = _ssm_in_bwd("ssm_in_bwd", gbu, dy, dskip, bb16, w_in, h1, g("mix_norm", 0), dh2, n_all)
    reduced.update(rs_finish(rs0a, after=(dh1,)))
    dw_in = _tn_rms("ssm_dwin", h1, g("mix_norm", 0), du, n_all, out_dtype=BF16).reshape(N_CHIPS, -1, hw)
    (dh0, dh0_meta), dg_f1l0, dwgu_f1l0, dwd_f1l0 = _ffn_bwd("l0_ffn1", dh1, h0, g("ffn1_norm", 0), gu1, wgu["ffn1", 0], wd["ffn1", 0],
                                                             n_all, n_main, after=(rs0a["total"],))
    rs0b = rs_start("rs0b", [("ffn1_w_gate_up", 0, dwgu_f1l0), ("ffn1_w_down", 0, dwd_f1l0), ("ssm_w_out", 0, dw_out),
                             ("ssm_w_in", 0, dw_in)], (9, 10, 11))
    dcb = _tn_plain("ssm_dcb", xs, dy, 4, xs.shape[1] // 4, hw // 4, n_all, after=(rs0b["blob"],))
    rs_scatter(rs0b, after=(dcb,))
    dbb = _tn_plain("ssm_dbb", u, gbu, 4, hw // 4, gbu.shape[1] // 4, n_all, after=(rs0b["chip_sum"],))

    grad_x = dh0.reshape(n_ex, seq, d)
    da_sum = jnp.sum(da, axis=(0, 1)).reshape(4, 2, half)
    d_ssm = ssm_vjp((dbb, dcb, da_sum[:, 0].reshape(-1), da_sum[:, 1].reshape(-1)))
    for key, val in zip(("ssm_lambda_re", "ssm_lambda_im", "ssm_log_step", "ssm_b_re", "ssm_b_im", "ssm_c_re", "ssm_c_im"), d_ssm):
        small[key] = val[None]
    small["meta_tokens"] = sum(dh0_meta[META_BLOCK * (b + 1) - n_meta:META_BLOCK * (b + 1)] for b in range(n_ex))
    small["ffn1_norm"] = jnp.concatenate([dg_f1l0, dg_f1l1], axis=0)
    small["ffn2_norm"] = jnp.concatenate([dg_f2l0, dg_f2l1], axis=0)
    small["mix_norm"] = jnp.concatenate([dg_mix0, dg_mix1], axis=0)
    small["ssm_d"] = dd
    small["kv_norm"] = dg_kv.reshape(-1)
    small["k_norm"] = jnp.sum(dk_gain.reshape(n_kv, hd), axis=0)
    small["q_norm"] = jnp.sum(dq_gain.reshape(n_q, hd), axis=0, keepdims=True) * score_scale
    small["attn_sinks"] = dsinks

    def view(name, a):
        if name in ("ssm_b_re", "ssm_b_im"):
            return a.reshape(-1, 128)
        return a.reshape(1, -1) if a.ndim == 1 else a.reshape(-1, a.shape[-1])

    grads, deltas, new_m, new_v = {}, {}, {}, {}

    def adamw_matrix(name, after=()):
        shape = w[name].shape
        if name in _FROM_HALVES:
            grads[name], deltas[name], new_m[name], new_v[name] = _adamw_from_halves(
                "adamw_" + name, w[name], m[name], v[name], [reduced[name, 0], reduced[name, 1]],
                jnp.reshape(pc, (1,)).astype(jnp.int32), name in _TRANSPOSED, after=after)
            return new_v[name]
        layers = [reduced[name, layer] for layer in range(2) if (name, layer) in reduced]
        grads[name] = jnp.concatenate(layers).reshape(shape)
        two_d = lambda a: a.reshape(-1, shape[-1])
        dl, nm, nv = _adamw("adamw_" + name, two_d(w[name]), two_d(grads[name]), two_d(m[name]), two_d(v[name]), after=after)
        deltas[name], new_m[name], new_v[name] = dl.reshape(shape), nm.reshape(shape), nv.reshape(shape)
        return nv

    placed = (rs0b["chip_sum"],)
    for name in ("ffn2_w_down", "attn_w_o", "attn_w_q", "w_kv"):
        placed = (adamw_matrix(name, after=placed),)
    tail = _reduce_small_adamw("small_tail", [view(k, small[k]) for k in _SMALL], loss_tile,
                               *[[view(k, t[k]) for k in _SMALL] for t in (w, m, v)], after=placed)
    n_small = len(_SMALL)
    for i, k in enumerate(_SMALL):
        grads[k], deltas[k] = tail[i].reshape(w[k].shape), tail[n_small + i].reshape(w[k].shape)
        new_m[k], new_v[k] = tail[2 * n_small + i].reshape(w[k].shape), tail[3 * n_small + i].reshape(w[k].shape)
    loss = jnp.sum(tail[-1])
    reduced.update(rs_finish(rs0b, after=(tail[-1],)))
    adamw_matrix("ffn2_w_gate_up", after=(rs0b["total"],))
    for name in ("ffn1_w_gate_up", "ffn1_w_down", "ssm_w_in", "ssm_w_out"):
        adamw_matrix(name)
    return (loss, grad_x, *[grads[k] for k in _ORDER], *[deltas[k] for k in _ORDER], *[new_m[k] for k in _ORDER],
            *[new_v[k] for k in _ORDER])


def kernel(x, meta_tokens, ffn1_norm, ffn1_w_gate_up, ffn1_w_down, mix_norm, ffn2_norm, ffn2_w_gate_up, ffn2_w_down, ssm_w_in, ssm_lambda_re, ssm_lambda_im, ssm_b_re, ssm_b_im, ssm_c_re, ssm_c_im, ssm_log_step, ssm_d, ssm_w_out, kv_norm, w_kv, k_norm, attn_w_q, q_norm, attn_sinks, attn_w_o, loss_target, m_meta_tokens, m_ffn1_norm, m_ffn1_w_gate_up, m_ffn1_w_down, m_mix_norm, m_ffn2_norm, m_ffn2_w_gate_up, m_ffn2_w_down, m_ssm_w_in, m_ssm_lambda_re, m_ssm_lambda_im, m_ssm_b_re, m_ssm_b_im, m_ssm_c_re, m_ssm_c_im, m_ssm_log_step, m_ssm_d, m_ssm_w_out, m_kv_norm, m_w_kv, m_k_norm, m_attn_w_q, m_q_norm, m_attn_sinks, m_attn_w_o, v_meta_tokens, v_ffn1_norm, v_ffn1_w_gate_up, v_ffn1_w_down, v_mix_norm, v_ffn2_norm, v_ffn2_w_gate_up, v_ffn2_w_down, v_ssm_w_in, v_ssm_lambda_re, v_ssm_lambda_im, v_ssm_b_re, v_ssm_b_im, v_ssm_c_re, v_ssm_c_im, v_ssm_log_step, v_ssm_d, v_ssm_w_out, v_kv_norm, v_w_kv, v_k_norm, v_attn_w_q, v_q_norm, v_attn_sinks, v_attn_w_o):
    args = locals()
    w = {k: args[k] for k in _ORDER}
    m = {k: args["m_" + k] for k in _ORDER}
    v = {k: args["v_" + k] for k in _ORDER}
    return _step(x, loss_target, w, m, v)
```
